```python
import math
import jax, jax.numpy as jnp
from jax import lax
import numpy as np

D_MODEL = 1024
BATCH = 8
SEQ = 8192
DEPTH = 2

RMS_EPS = 1e-6
N_BRANCH = 3

S5_WIDTH = D_MODEL // 2
S5_GROUP = 16
S5_GROUPS = S5_WIDTH // S5_GROUP
S5_STATE = 64
S5_STEP_MIN = 1e-3
S5_STEP_MAX = 1e-1

ATT_HEAD_DIM = 64
ATT_PAIRS = ((128, 1), (512, 4), (2048, 16))
ATT_HEADS_PER_GROUP = 4
ATT_HEADS = len(ATT_PAIRS) * ATT_HEADS_PER_GROUP
ATT_WIDTH = ATT_HEADS * ATT_HEAD_DIM
ATT_OUT_WIDTH = ATT_HEADS_PER_GROUP * ATT_HEAD_DIM
ATT_BLOCK = 128

SSD_HEAD_DIM = 64
SSD_WIDTH = 3 * D_MODEL // 4
SSD_HEADS = SSD_WIDTH // SSD_HEAD_DIM
SSD_GROUPS = 2
SSD_STATE = 128
SSD_CONV = 4
SSD_CHUNK = 128
SSD_CONV_DIM = SSD_WIDTH + 2 * SSD_GROUPS * SSD_STATE
SSD_DT_MIN = 1e-3
SSD_DT_MAX = 1e-1

IN_SPLITS = (S5_WIDTH, S5_WIDTH,
             ATT_WIDTH, ATT_WIDTH, ATT_WIDTH, ATT_OUT_WIDTH,
             SSD_CONV_DIM, SSD_HEADS, SSD_WIDTH,
             N_BRANCH * D_MODEL)
IN_WIDTH = sum(IN_SPLITS)

kernel_name = "hybrid_s5_dilated_attn_ssd_gated_merge"


def rms_norm(x, w):
    xf = x.astype(jnp.float32)
    y = xf * lax.rsqrt(jnp.mean(xf * xf, axis=-1, keepdims=True) + RMS_EPS)
    return y * w.astype(jnp.float32)


def s5_mixer(u, a_re, a_im, log_step, b_re, b_im, c_re, c_im, d, glu_w, glu_b):
    f32 = jnp.float32
    a_re, a_im = a_re.astype(f32), a_im.astype(f32)
    b_re, b_im = b_re.astype(f32), b_im.astype(f32)
    c_re, c_im = c_re.astype(f32), c_im.astype(f32)
    Bt, S, _ = u.shape
    ug = u.reshape(Bt, S, S5_GROUPS, S5_GROUP)
    step = jnp.exp(log_step.astype(f32))[:, None]
    mag = jnp.exp(a_re * step)
    ang = a_im * step
    lam_re, lam_im = mag * jnp.cos(ang), mag * jnp.sin(ang)
    num_re, num_im = lam_re - 1.0, lam_im
    den = a_re * a_re + a_im * a_im
    f_re = (num_re * a_re + num_im * a_im) / den
    f_im = (num_im * a_re - num_re * a_im) / den
    bb_re = f_re[..., None] * b_re - f_im[..., None] * b_im
    bb_im = f_re[..., None] * b_im + f_im[..., None] * b_re
    bu_re = jnp.einsum('gpi,bsgi->bsgp', bb_re, ug)
    bu_im = jnp.einsum('gpi,bsgi->bsgp', bb_im, ug)
    lam_re_t = jnp.broadcast_to(lam_re, (1, S) + lam_re.shape)
    lam_im_t = jnp.broadcast_to(lam_im, (1, S) + lam_im.shape)

    def combine(left, right):
        ar_l, ai_l, br_l, bi_l = left
        ar_r, ai_r, br_r, bi_r = right
        return (ar_r * ar_l - ai_r * ai_l,
                ar_r * ai_l + ai_r * ar_l,
                ar_r * br_l - ai_r * bi_l + br_r,
                ar_r * bi_l + ai_r * br_l + bi_r)

    _, _, h_re, h_im = lax.associative_scan(combine, (lam_re_t, lam_im_t, bu_re, bu_im), axis=1)
    y = jnp.einsum('gip,bsgp->bsgi', c_re, h_re) - jnp.einsum('gip,bsgp->bsgi', c_im, h_im)
    y = y.reshape(Bt, S, S5_WIDTH) + d.astype(f32) * u
    g = jax.nn.gelu(y)
    return g * jax.nn.sigmoid(g @ glu_w + glu_b)


def dilated_window_attention(q, k, v, window, dilation):
    Bt, S, H, Dh = q.shape
    span = window // dilation
    seg = dilation * ATT_BLOCK
    S_pad = -(-S // seg) * seg
    L = S_pad // dilation
    nb = L // ATT_BLOCK

    def to_strided(t):
        t = jnp.pad(t, ((0, 0), (0, S_pad - S), (0, 0), (0, 0)))
        t = t.reshape(Bt, L, dilation, H, Dh).transpose(0, 2, 1, 3, 4)
        return t.reshape(Bt, dilation, nb, ATT_BLOCK, H, Dh)

    def with_prev(t):
        prev = jnp.pad(t[:, :, :-1], ((0, 0), (0, 0), (1, 0), (0, 0), (0, 0), (0, 0)))
        return jnp.concatenate([prev, t], axis=3)

    qb = to_strided(q)
    kb = with_prev(to_strided(k))
    vb = with_prev(to_strided(v))
    s = jnp.einsum('brnqhd,brnkhd->brnhqk', qb, kb) * (Dh ** -0.5)
    qi = jnp.arange(ATT_BLOCK)[:, None] + ATT_BLOCK
    kj = jnp.arange(2 * ATT_BLOCK)[None, :]
    band = (qi - kj >= 0) & (qi - kj <= span)
    has_prev = (jnp.arange(nb) > 0)[:, None, None] | (kj >= ATT_BLOCK)[None]
    mask = band[None] & has_prev
    s = jnp.where(mask[None, None, :, None], s, -jnp.inf)
    m = jnp.max(s, axis=-1, keepdims=True)
    p = jnp.exp(s - m)
    l = jnp.sum(p, axis=-1, keepdims=True)
    o = jnp.einsum('brnhqk,brnkhd->brnqhd', p / l, vb)
    lse = (m + jnp.log(l))[..., 0]
    o = o.reshape(Bt, dilation, L, H, Dh).transpose(0, 2, 1, 3, 4).reshape(Bt, S_pad, H, Dh)[:, :S]
    lse = lse.transpose(0, 1, 2, 4, 3).reshape(Bt, dilation, L, H)
    lse = lse.transpose(0, 2, 1, 3).reshape(Bt, S_pad, H)[:, :S]
    return o, lse


def attention_mixer(q, k, v, q_norm_w, k_norm_w):
    Bt, S = q.shape[:2]
    q = rms_norm(q, q_norm_w)
    k = rms_norm(k, k_norm_w)
    v = v.astype(jnp.float32)
    outs, lses = [], []
    for g, (window, dilation) in enumerate(ATT_PAIRS):
        sl = slice(g * ATT_HEADS_PER_GROUP, (g + 1) * ATT_HEADS_PER_GROUP)
        o, l = dilated_window_attention(q[:, :, sl], k[:, :, sl], v[:, :, sl], window, dilation)
        outs.append(o)
        lses.append(l)
    o = jnp.stack(outs, axis=0)
    alpha = jax.nn.softmax(jnp.stack(lses, axis=0), axis=0)
    y = jnp.sum(alpha[..., None] * o, axis=0)
    return y.reshape(Bt, S, ATT_OUT_WIDTH)


def segsum(a):
    T = a.shape[-1]
    cs = jnp.cumsum(a, axis=-1)
    diff = cs[..., :, None] - cs[..., None, :]
    return jnp.where(jnp.tril(jnp.ones((T, T), dtype=bool)), diff, -jnp.inf)


def causal_depthwise_conv(x, w, b):
    y = lax.conv_general_dilated(x, w.astype(x.dtype)[:, None, :], window_strides=(1,),
                                 padding=((SSD_CONV - 1, 0),),
                                 dimension_numbers=('NWC', 'WIO', 'NWC'),
                                 feature_group_count=x.shape[-1])
    return y + b


def ssd_mixer(xbc, dt, z, conv_w, conv_b, dt_bias, a_log, d, norm_w):
    f32 = jnp.float32
    Bt, S, _ = xbc.shape
    E = SSD_HEADS // SSD_GROUPS
    nc = S // SSD_CHUNK
    xbc = jax.nn.silu(causal_depthwise_conv(xbc, conv_w, conv_b))
    xs, bm, cm = jnp.split(xbc, [SSD_WIDTH, SSD_WIDTH + SSD_GROUPS * SSD_STATE], axis=-1)
    xs = xs.reshape(Bt, nc, SSD_CHUNK, SSD_GROUPS, E, SSD_HEAD_DIM)
    bm = bm.reshape(Bt, nc, SSD_CHUNK, SSD_GROUPS, SSD_STATE)
    cm = cm.reshape(Bt, nc, SSD_CHUNK, SSD_GROUPS, SSD_STATE)
    dt = jax.nn.softplus(dt + dt_bias.astype(f32))
    a = -jnp.exp(a_log.astype(f32))
    dt_c = dt.reshape(Bt, nc, SSD_CHUNK, SSD_GROUPS, E)
    a_dt = (dt_c * a.reshape(SSD_GROUPS, E)).transpose(0, 3, 4, 1, 2)
    xdt = xs * dt_c[..., None]
    a_cs = jnp.cumsum(a_dt, axis=-1)
    decay_in = jnp.exp(segsum(a_dt))
    cb = jnp.einsum('bclgn,bcsgn->bgcls', cm, bm)
    y_diag = jnp.einsum('bgcls,bgecls,bcsgep->bclgep', cb, decay_in, xdt)
    decay_st = jnp.exp(a_cs[..., -1:] - a_cs)
    states = jnp.einsum('bclgn,bgecl,bclgep->bcgepn', bm, decay_st, xdt)
    states = jnp.concatenate([jnp.zeros_like(states[:, :1]), states], axis=1)
    chunk_a = jnp.pad(a_cs[..., -1], ((0, 0), (0, 0), (0, 0), (1, 0)))
    decay_chunk = jnp.exp(segsum(chunk_a))
    states = jnp.einsum('bgezc,bcgepn->bzgepn', decay_chunk, states)[:, :-1]
    y_off = jnp.einsum('bclgn,bcgepn,bgecl->bclgep', cm, states, jnp.exp(a_cs))
    y = y_diag + y_off + xs * d.astype(f32).reshape(SSD_GROUPS, E)[:, :, None]
    y = y.reshape(Bt, S, SSD_WIDTH)
    return rms_norm(y * jax.nn.silu(z), norm_w)


def hybrid_layer(x, norm_w, w_in, s5_a_re, s5_a_im, s5_log_step, s5_b_re, s5_b_im, s5_c_re,
                 s5_c_im, s5_d, s5_glu_w, s5_glu_b, q_norm_w, k_norm_w, conv_w, conv_b,
                 dt_bias, ssd_a_log, ssd_d, ssd_norm_w, proj_a, proj_b, proj_c, w_out):
    Bt, S, _ = x.shape
    h = rms_norm(x, norm_w)
    proj = h @ w_in
    (u_a, z_a, q, k, v, z_b, xbc, dt, z_c, gate_logits) = jnp.split(
        proj, np.cumsum(IN_SPLITS)[:-1].tolist(), axis=-1)
    y_a = s5_mixer(u_a, s5_a_re, s5_a_im, s5_log_step, s5_b_re, s5_b_im, s5_c_re, s5_c_im,
                   s5_d, s5_glu_w, s5_glu_b) * jax.nn.silu(z_a)
    hd = (Bt, S, ATT_HEADS, ATT_HEAD_DIM)
    y_b = attention_mixer(q.reshape(hd), k.reshape(hd), v.reshape(hd),
                          q_norm_w, k_norm_w) * jax.nn.silu(z_b)
    y_c = ssd_mixer(xbc, dt, z_c, conv_w, conv_b, dt_bias, ssd_a_log, ssd_d, ssd_norm_w)
    gates = jax.nn.sigmoid(gate_logits).reshape(Bt, S, N_BRANCH, D_MODEL)
    merged = (gates[:, :, 0] * (y_a @ proj_a)
              + gates[:, :, 1] * (y_b @ proj_b)
              + gates[:, :, 2] * (y_c @ proj_c))
    return x + (merged @ w_out).astype(x.dtype)


def _fwd_setup_inputs(seed: int = 0) -> dict:
    key = jax.random.key(seed)
    ks = jax.random.split(key, 32)
    L = DEPTH
    nrm = jax.random.normal
    P, I, G = S5_STATE, S5_GROUP, S5_GROUPS
    x = nrm(ks[0], (BATCH, SEQ, D_MODEL), jnp.float32)
    norm_w = 1.0 + 0.02 * nrm(ks[1], (L, D_MODEL))
    w_in = nrm(ks[2], (L, D_MODEL, IN_WIDTH)) * D_MODEL ** -0.5
    s5_a_re = -0.5 + 0.01 * nrm(ks[3], (L, G, P))
    s5_a_im = math.pi * jnp.arange(P, dtype=jnp.float32) + 0.01 * nrm(ks[4], (L, G, P))
    s5_log_step = jax.random.uniform(ks[5], (L, G), minval=math.log(S5_STEP_MIN),
                                     maxval=math.log(S5_STEP_MAX))
    s5_b_re = nrm(ks[6], (L, G, P, I)) * (2 * I) ** -0.5
    s5_b_im = nrm(ks[7], (L, G, P, I)) * (2 * I) ** -0.5
    s5_c_re = nrm(ks[8], (L, G, I, P)) * (2 * P) ** -0.5
    s5_c_im = nrm(ks[9], (L, G, I, P)) * (2 * P) ** -0.5
    s5_d = nrm(ks[10], (L, S5_WIDTH))
    s5_glu_w = nrm(ks[11], (L, S5_WIDTH, S5_WIDTH)) * S5_WIDTH ** -0.5
    s5_glu_b = 0.01 * nrm(ks[12], (L, S5_WIDTH))
    q_norm_w = 1.0 + 0.02 * nrm(ks[13], (L, ATT_HEAD_DIM))
    k_norm_w = 1.0 + 0.02 * nrm(ks[14], (L, ATT_HEAD_DIM))
    conv_w = nrm(ks[15], (L, SSD_CONV, SSD_CONV_DIM)) * SSD_CONV ** -0.5
    conv_b = 0.01 * nrm(ks[16], (L, SSD_CONV_DIM))
    dt0 = jnp.exp(jax.random.uniform(ks[17], (L, SSD_HEADS), minval=math.log(SSD_DT_MIN),
                                     maxval=math.log(SSD_DT_MAX)))
    dt_bias = dt0 + jnp.log(-jnp.expm1(-dt0))
    ssd_a_log = jnp.log(jax.random.uniform(ks[18], (L, SSD_HEADS), minval=1.0, maxval=16.0))
    ssd_d = 1.0 + 0.1 * nrm(ks[19], (L, SSD_HEADS))
    ssd_norm_w = 1.0 + 0.02 * nrm(ks[20], (L, SSD_WIDTH))
    proj_a = nrm(ks[21], (L, S5_WIDTH, D_MODEL)) * S5_WIDTH ** -0.5
    proj_b = nrm(ks[22], (L, ATT_OUT_WIDTH, D_MODEL)) * ATT_OUT_WIDTH ** -0.5
    proj_c = nrm(ks[23], (L, SSD_WIDTH, D_MODEL)) * SSD_WIDTH ** -0.5
    w_out = nrm(ks[24], (L, D_MODEL, D_MODEL)) * (0.5 * D_MODEL ** -0.5)
    return {"x": x, "norm_w": norm_w, "w_in": w_in,
            "s5_a_re": s5_a_re, "s5_a_im": s5_a_im, "s5_log_step": s5_log_step,
            "s5_b_re": s5_b_re, "s5_b_im": s5_b_im, "s5_c_re": s5_c_re, "s5_c_im": s5_c_im,
            "s5_d": s5_d, "s5_glu_w": s5_glu_w, "s5_glu_b": s5_glu_b,
            "q_norm_w": q_norm_w, "k_norm_w": k_norm_w,
            "conv_w": conv_w, "conv_b": conv_b, "dt_bias": dt_bias,
            "ssd_a_log": ssd_a_log, "ssd_d": ssd_d, "ssd_norm_w": ssd_norm_w,
            "proj_a": proj_a, "proj_b": proj_b, "proj_c": proj_c, "w_out": w_out}


def _fwd_reference(x, norm_w, w_in, s5_a_re, s5_a_im, s5_log_step, s5_b_re, s5_b_im, s5_c_re,
              s5_c_im, s5_d, s5_glu_w, s5_glu_b, q_norm_w, k_norm_w, conv_w, conv_b,
              dt_bias, ssd_a_log, ssd_d, ssd_norm_w, proj_a, proj_b, proj_c, w_out):
    for i in range(DEPTH):
        x = hybrid_layer(x, norm_w[i], w_in[i], s5_a_re[i], s5_a_im[i], s5_log_step[i],
                         s5_b_re[i], s5_b_im[i], s5_c_re[i], s5_c_im[i], s5_d[i],
                         s5_glu_w[i], s5_glu_b[i], q_norm_w[i], k_norm_w[i], conv_w[i],
                         conv_b[i], dt_bias[i], ssd_a_log[i], ssd_d[i], ssd_norm_w[i],
                         proj_a[i], proj_b[i], proj_c[i], w_out[i])
    return x


import jax as _jax
import jax.numpy as _jnp

TWIN_FORMAT = 'train_step'
FWD_PARAMS = ['x', 'norm_w', 'w_in', 's5_a_re', 's5_a_im', 's5_log_step', 's5_b_re', 's5_b_im', 's5_c_re', 's5_c_im', 's5_d', 's5_glu_w', 's5_glu_b', 'q_norm_w', 'k_norm_w', 'conv_w', 'conv_b', 'dt_bias', 'ssd_a_log', 'ssd_d', 'ssd_norm_w', 'proj_a', 'proj_b', 'proj_c', 'w_out']
TWIN_WEIGHTS = ['norm_w', 'w_in', 's5_a_re', 's5_a_im', 's5_log_step', 's5_b_re', 's5_b_im', 's5_c_re', 's5_c_im', 's5_d', 's5_glu_w', 's5_glu_b', 'q_norm_w', 'k_norm_w', 'conv_w', 'conv_b', 'dt_bias', 'ssd_a_log', 'ssd_d', 'ssd_norm_w', 'proj_a', 'proj_b', 'proj_c', 'w_out']
TWIN_DIFF_INPUT = 'x'
TWIN_INPUTS = ['x', 'norm_w', 'w_in', 's5_a_re', 's5_a_im', 's5_log_step', 's5_b_re', 's5_b_im', 's5_c_re', 's5_c_im', 's5_d', 's5_glu_w', 's5_glu_b', 'q_norm_w', 'k_norm_w', 'conv_w', 'conv_b', 'dt_bias', 'ssd_a_log', 'ssd_d', 'ssd_norm_w', 'proj_a', 'proj_b', 'proj_c', 'w_out', 'loss_target', 'm_norm_w', 'm_w_in', 'm_s5_a_re', 'm_s5_a_im', 'm_s5_log_step', 'm_s5_b_re', 'm_s5_b_im', 'm_s5_c_re', 'm_s5_c_im', 'm_s5_d', 'm_s5_glu_w', 'm_s5_glu_b', 'm_q_norm_w', 'm_k_norm_w', 'm_conv_w', 'm_conv_b', 'm_dt_bias', 'm_ssd_a_log', 'm_ssd_d', 'm_ssd_norm_w', 'm_proj_a', 'm_proj_b', 'm_proj_c', 'm_w_out', 'v_norm_w', 'v_w_in', 'v_s5_a_re', 'v_s5_a_im', 'v_s5_log_step', 'v_s5_b_re', 'v_s5_b_im', 'v_s5_c_re', 'v_s5_c_im', 'v_s5_d', 'v_s5_glu_w', 'v_s5_glu_b', 'v_q_norm_w', 'v_k_norm_w', 'v_conv_w', 'v_conv_b', 'v_dt_bias', 'v_ssd_a_log', 'v_ssd_d', 'v_ssd_norm_w', 'v_proj_a', 'v_proj_b', 'v_proj_c', 'v_w_out']
TWIN_OUTPUTS = ['loss', 'grad_x', 'grad_norm_w', 'grad_w_in', 'grad_s5_a_re', 'grad_s5_a_im', 'grad_s5_log_step', 'grad_s5_b_re', 'grad_s5_b_im', 'grad_s5_c_re', 'grad_s5_c_im', 'grad_s5_d', 'grad_s5_glu_w', 'grad_s5_glu_b', 'grad_q_norm_w', 'grad_k_norm_w', 'grad_conv_w', 'grad_conv_b', 'grad_dt_bias', 'grad_ssd_a_log', 'grad_ssd_d', 'grad_ssd_norm_w', 'grad_proj_a', 'grad_proj_b', 'grad_proj_c', 'grad_w_out', 'delta_norm_w', 'delta_w_in', 'delta_s5_a_re', 'delta_s5_a_im', 'delta_s5_log_step', 'delta_s5_b_re', 'delta_s5_b_im', 'delta_s5_c_re', 'delta_s5_c_im', 'delta_s5_d', 'delta_s5_glu_w', 'delta_s5_glu_b', 'delta_q_norm_w', 'delta_k_norm_w', 'delta_conv_w', 'delta_conv_b', 'delta_dt_bias', 'delta_ssd_a_log', 'delta_ssd_d', 'delta_ssd_norm_w', 'delta_proj_a', 'delta_proj_b', 'delta_proj_c', 'delta_w_out', 'new_m_norm_w', 'new_m_w_in', 'new_m_s5_a_re', 'new_m_s5_a_im', 'new_m_s5_log_step', 'new_m_s5_b_re', 'new_m_s5_b_im', 'new_m_s5_c_re', 'new_m_s5_c_im', 'new_m_s5_d', 'new_m_s5_glu_w', 'new_m_s5_glu_b', 'new_m_q_norm_w', 'new_m_k_norm_w', 'new_m_conv_w', 'new_m_conv_b', 'new_m_dt_bias', 'new_m_ssd_a_log', 'new_m_ssd_d', 'new_m_ssd_norm_w', 'new_m_proj_a', 'new_m_proj_b', 'new_m_proj_c', 'new_m_w_out', 'new_v_norm_w', 'new_v_w_in', 'new_v_s5_a_re', 'new_v_s5_a_im', 'new_v_s5_log_step', 'new_v_s5_b_re', 'new_v_s5_b_im', 'new_v_s5_c_re', 'new_v_s5_c_im', 'new_v_s5_d', 'new_v_s5_glu_w', 'new_v_s5_glu_b', 'new_v_q_norm_w', 'new_v_k_norm_w', 'new_v_conv_w', 'new_v_conv_b', 'new_v_dt_bias', 'new_v_ssd_a_log', 'new_v_ssd_d', 'new_v_ssd_norm_w', 'new_v_proj_a', 'new_v_proj_b', 'new_v_proj_c', 'new_v_w_out']
TWIN_LEAF_KINDS = {'loss': 'loss', 'grad_x': 'grad_x', 'grad_norm_w': 'grad_w', 'grad_w_in': 'grad_w', 'grad_s5_a_re': 'grad_w', 'grad_s5_a_im': 'grad_w', 'grad_s5_log_step': 'grad_w', 'grad_s5_b_re': 'grad_w', 'grad_s5_b_im': 'grad_w', 'grad_s5_c_re': 'grad_w', 'grad_s5_c_im': 'grad_w', 'grad_s5_d': 'grad_w', 'grad_s5_glu_w': 'grad_w', 'grad_s5_glu_b': 'grad_w', 'grad_q_norm_w': 'grad_w', 'grad_k_norm_w': 'grad_w', 'grad_conv_w': 'grad_w', 'grad_conv_b': 'grad_w', 'grad_dt_bias': 'grad_w', 'grad_ssd_a_log': 'grad_w', 'grad_ssd_d': 'grad_w', 'grad_ssd_norm_w': 'grad_w', 'grad_proj_a': 'grad_w', 'grad_proj_b': 'grad_w', 'grad_proj_c': 'grad_w', 'grad_w_out': 'grad_w', 'delta_norm_w': 'delta_w', 'delta_w_in': 'delta_w', 'delta_s5_a_re': 'delta_w', 'delta_s5_a_im': 'delta_w', 'delta_s5_log_step': 'delta_w', 'delta_s5_b_re': 'delta_w', 'delta_s5_b_im': 'delta_w', 'delta_s5_c_re': 'delta_w', 'delta_s5_c_im': 'delta_w', 'delta_s5_d': 'delta_w', 'delta_s5_glu_w': 'delta_w', 'delta_s5_glu_b': 'delta_w', 'delta_q_norm_w': 'delta_w', 'delta_k_norm_w': 'delta_w', 'delta_conv_w': 'delta_w', 'delta_conv_b': 'delta_w', 'delta_dt_bias': 'delta_w', 'delta_ssd_a_log': 'delta_w', 'delta_ssd_d': 'delta_w', 'delta_ssd_norm_w': 'delta_w', 'delta_proj_a': 'delta_w', 'delta_proj_b': 'delta_w', 'delta_proj_c': 'delta_w', 'delta_w_out': 'delta_w', 'new_m_norm_w': 'new_m', 'new_m_w_in': 'new_m', 'new_m_s5_a_re': 'new_m', 'new_m_s5_a_im': 'new_m', 'new_m_s5_log_step': 'new_m', 'new_m_s5_b_re': 'new_m', 'new_m_s5_b_im': 'new_m', 'new_m_s5_c_re': 'new_m', 'new_m_s5_c_im': 'new_m', 'new_m_s5_d': 'new_m', 'new_m_s5_glu_w': 'new_m', 'new_m_s5_glu_b': 'new_m', 'new_m_q_norm_w': 'new_m', 'new_m_k_norm_w': 'new_m', 'new_m_conv_w': 'new_m', 'new_m_conv_b': 'new_m', 'new_m_dt_bias': 'new_m', 'new_m_ssd_a_log': 'new_m', 'new_m_ssd_d': 'new_m', 'new_m_ssd_norm_w': 'new_m', 'new_m_proj_a': 'new_m', 'new_m_proj_b': 'new_m', 'new_m_proj_c': 'new_m', 'new_m_w_out': 'new_m', 'new_v_norm_w': 'new_v', 'new_v_w_in': 'new_v', 'new_v_s5_a_re': 'new_v', 'new_v_s5_a_im': 'new_v', 'new_v_s5_log_step': 'new_v', 'new_v_s5_b_re': 'new_v', 'new_v_s5_b_im': 'new_v', 'new_v_s5_c_re': 'new_v', 'new_v_s5_c_im': 'new_v', 'new_v_s5_d': 'new_v', 'new_v_s5_glu_w': 'new_v', 'new_v_s5_glu_b': 'new_v', 'new_v_q_norm_w': 'new_v', 'new_v_k_norm_w': 'new_v', 'new_v_conv_w': 'new_v', 'new_v_conv_b': 'new_v', 'new_v_dt_bias': 'new_v', 'new_v_ssd_a_log': 'new_v', 'new_v_ssd_d': 'new_v', 'new_v_ssd_norm_w': 'new_v', 'new_v_proj_a': 'new_v', 'new_v_proj_b': 'new_v', 'new_v_proj_c': 'new_v', 'new_v_w_out': 'new_v'}


def _forward(args):
    return _fwd_reference(*[args[k] for k in FWD_PARAMS])


def _output_shape():
    out = _jax.eval_shape(lambda: _forward(_fwd_setup_inputs(0)))
    return out.shape, out.dtype

N_MICROBATCH = 1
ADAM_LR = 0.001
ADAM_B1 = 0.9
ADAM_B2 = 0.999
ADAM_EPS = 1e-08
ADAM_WD = 0.01
ADAM_STEP = 10
PER_EXAMPLE_BATCH_AXIS = {'x': 0, 'loss_target': 0}
SHARED_INPUTS = []
_WEIGHT_DTYPES = {'norm_w': _jnp.float32, 'w_in': _jnp.float32, 's5_a_re': _jnp.float32, 's5_a_im': _jnp.float32, 's5_log_step': _jnp.float32, 's5_b_re': _jnp.float32, 's5_b_im': _jnp.float32, 's5_c_re': _jnp.float32, 's5_c_im': _jnp.float32, 's5_d': _jnp.float32, 's5_glu_w': _jnp.float32, 's5_glu_b': _jnp.float32, 'q_norm_w': _jnp.float32, 'k_norm_w': _jnp.float32, 'conv_w': _jnp.float32, 'conv_b': _jnp.float32, 'dt_bias': _jnp.float32, 'ssd_a_log': _jnp.float32, 'ssd_d': _jnp.float32, 'ssd_norm_w': _jnp.float32, 'proj_a': _jnp.float32, 'proj_b': _jnp.float32, 'proj_c': _jnp.float32, 'w_out': _jnp.float32}
MOMENT_SCALE = {'norm_w': 6.832285e-01, 'w_in': 6.965248e-02, 's5_a_re': 3.117346e-03, 's5_a_im': 2.450148e-03, 's5_log_step': 1.888136e+00, 's5_b_re': 1.442626e-03, 's5_b_im': 1.424299e-03, 's5_c_re': 3.120262e-03, 's5_c_im': 3.057585e-03, 's5_d': 4.677041e-01, 's5_glu_w': 8.246668e-02, 's5_glu_b': 2.804421e-01, 'q_norm_w': 8.607943e-02, 'k_norm_w': 8.607659e-02, 'conv_w': 1.800862e-01, 'conv_b': 4.908787e-01, 'dt_bias': 2.518351e-01, 'ssd_a_log': 3.167207e+00, 'ssd_d': 1.817538e+00, 'ssd_norm_w': 9.202413e+00, 'proj_a': 4.176934e-02, 'proj_b': 1.200052e-02, 'proj_c': 2.846495e-01, 'w_out': 4.757419e-01}


def _to_microbatches(a, axis):
    t = _jnp.moveaxis(a, axis, 0)
    t = t.reshape((N_MICROBATCH, t.shape[0] // N_MICROBATCH) + t.shape[1:])
    return _jnp.moveaxis(t, 1, axis + 1)


def setup_inputs(seed: int = 0) -> dict:
    inp = _fwd_setup_inputs(seed)
    key = _jax.random.fold_in(_jax.random.key(seed), 7919)
    shape, _ = _output_shape()
    out = dict(inp)
    out["loss_target"] = _jax.random.normal(_jax.random.fold_in(key, 0), shape, _jnp.float32)
    for i, name in enumerate(TWIN_WEIGHTS):
        w = inp[name].astype(_jnp.float32)
        if MOMENT_SCALE is None:
            s = _jnp.sqrt(_jnp.mean(_jnp.square(w)) + 1e-30)
        else:
            s = MOMENT_SCALE[name]
        km, kv = _jax.random.split(_jax.random.fold_in(key, i + 1))
        out[name] = w
        out["m_" + name] = s * _jax.random.normal(km, w.shape, _jnp.float32)
        out["v_" + name] = (s * s) * _jax.random.uniform(kv, w.shape, _jnp.float32, 0.5, 1.5)
    if N_MICROBATCH > 1:
        for name, axis in PER_EXAMPLE_BATCH_AXIS.items():
            out[name] = _to_microbatches(out[name], axis)
    return {'x': out['x'], 'norm_w': out['norm_w'], 'w_in': out['w_in'], 's5_a_re': out['s5_a_re'], 's5_a_im': out['s5_a_im'], 's5_log_step': out['s5_log_step'], 's5_b_re': out['s5_b_re'], 's5_b_im': out['s5_b_im'], 's5_c_re': out['s5_c_re'], 's5_c_im': out['s5_c_im'], 's5_d': out['s5_d'], 's5_glu_w': out['s5_glu_w'], 's5_glu_b': out['s5_glu_b'], 'q_norm_w': out['q_norm_w'], 'k_norm_w': out['k_norm_w'], 'conv_w': out['conv_w'], 'conv_b': out['conv_b'], 'dt_bias': out['dt_bias'], 'ssd_a_log': out['ssd_a_log'], 'ssd_d': out['ssd_d'], 'ssd_norm_w': out['ssd_norm_w'], 'proj_a': out['proj_a'], 'proj_b': out['proj_b'], 'proj_c': out['proj_c'], 'w_out': out['w_out'], 'loss_target': out['loss_target'], 'm_norm_w': out['m_norm_w'], 'm_w_in': out['m_w_in'], 'm_s5_a_re': out['m_s5_a_re'], 'm_s5_a_im': out['m_s5_a_im'], 'm_s5_log_step': out['m_s5_log_step'], 'm_s5_b_re': out['m_s5_b_re'], 'm_s5_b_im': out['m_s5_b_im'], 'm_s5_c_re': out['m_s5_c_re'], 'm_s5_c_im': out['m_s5_c_im'], 'm_s5_d': out['m_s5_d'], 'm_s5_glu_w': out['m_s5_glu_w'], 'm_s5_glu_b': out['m_s5_glu_b'], 'm_q_norm_w': out['m_q_norm_w'], 'm_k_norm_w': out['m_k_norm_w'], 'm_conv_w': out['m_conv_w'], 'm_conv_b': out['m_conv_b'], 'm_dt_bias': out['m_dt_bias'], 'm_ssd_a_log': out['m_ssd_a_log'], 'm_ssd_d': out['m_ssd_d'], 'm_ssd_norm_w': out['m_ssd_norm_w'], 'm_proj_a': out['m_proj_a'], 'm_proj_b': out['m_proj_b'], 'm_proj_c': out['m_proj_c'], 'm_w_out': out['m_w_out'], 'v_norm_w': out['v_norm_w'], 'v_w_in': out['v_w_in'], 'v_s5_a_re': out['v_s5_a_re'], 'v_s5_a_im': out['v_s5_a_im'], 'v_s5_log_step': out['v_s5_log_step'], 'v_s5_b_re': out['v_s5_b_re'], 'v_s5_b_im': out['v_s5_b_im'], 'v_s5_c_re': out['v_s5_c_re'], 'v_s5_c_im': out['v_s5_c_im'], 'v_s5_d': out['v_s5_d'], 'v_s5_glu_w': out['v_s5_glu_w'], 'v_s5_glu_b': out['v_s5_glu_b'], 'v_q_norm_w': out['v_q_norm_w'], 'v_k_norm_w': out['v_k_norm_w'], 'v_conv_w': out['v_conv_w'], 'v_conv_b': out['v_conv_b'], 'v_dt_bias': out['v_dt_bias'], 'v_ssd_a_log': out['v_ssd_a_log'], 'v_ssd_d': out['v_ssd_d'], 'v_ssd_norm_w': out['v_ssd_norm_w'], 'v_proj_a': out['v_proj_a'], 'v_proj_b': out['v_proj_b'], 'v_proj_c': out['v_proj_c'], 'v_w_out': out['v_w_out']}


def _loss(weights, diff, rest, loss_target):
    with _jax.named_scope("forward"):
        args = {**rest, TWIN_DIFF_INPUT: diff, **{k: w.astype(_WEIGHT_DTYPES[k]) for k, w in weights.items()}}
        y = _forward(args)
    with _jax.named_scope("loss_head"):
        err = _jnp.square(y.astype(_jnp.float32) - loss_target)
        return 0.5 * _jnp.sum(_jnp.mean(err, axis=-1)) if err.ndim else 0.5 * err


def _adamw(w, g, m, v):
    m = ADAM_B1 * m + (1.0 - ADAM_B1) * g
    v = ADAM_B2 * v + (1.0 - ADAM_B2) * _jnp.square(g)
    m_hat = m / (1.0 - ADAM_B1 ** ADAM_STEP)
    v_hat = v / (1.0 - ADAM_B2 ** ADAM_STEP)
    delta = -ADAM_LR * (m_hat / (_jnp.sqrt(v_hat) + ADAM_EPS) + ADAM_WD * w)
    return delta, m, v


def reference(x, norm_w, w_in, s5_a_re, s5_a_im, s5_log_step, s5_b_re, s5_b_im, s5_c_re, s5_c_im, s5_d, s5_glu_w, s5_glu_b, q_norm_w, k_norm_w, conv_w, conv_b, dt_bias, ssd_a_log, ssd_d, ssd_norm_w, proj_a, proj_b, proj_c, w_out, loss_target, m_norm_w, m_w_in, m_s5_a_re, m_s5_a_im, m_s5_log_step, m_s5_b_re, m_s5_b_im, m_s5_c_re, m_s5_c_im, m_s5_d, m_s5_glu_w, m_s5_glu_b, m_q_norm_w, m_k_norm_w, m_conv_w, m_conv_b, m_dt_bias, m_ssd_a_log, m_ssd_d, m_ssd_norm_w, m_proj_a, m_proj_b, m_proj_c, m_w_out, v_norm_w, v_w_in, v_s5_a_re, v_s5_a_im, v_s5_log_step, v_s5_b_re, v_s5_b_im, v_s5_c_re, v_s5_c_im, v_s5_d, v_s5_glu_w, v_s5_glu_b, v_q_norm_w, v_k_norm_w, v_conv_w, v_conv_b, v_dt_bias, v_ssd_a_log, v_ssd_d, v_ssd_norm_w, v_proj_a, v_proj_b, v_proj_c, v_w_out):
    given = dict(x=x, norm_w=norm_w, w_in=w_in, s5_a_re=s5_a_re, s5_a_im=s5_a_im, s5_log_step=s5_log_step, s5_b_re=s5_b_re, s5_b_im=s5_b_im, s5_c_re=s5_c_re, s5_c_im=s5_c_im, s5_d=s5_d, s5_glu_w=s5_glu_w, s5_glu_b=s5_glu_b, q_norm_w=q_norm_w, k_norm_w=k_norm_w, conv_w=conv_w, conv_b=conv_b, dt_bias=dt_bias, ssd_a_log=ssd_a_log, ssd_d=ssd_d, ssd_norm_w=ssd_norm_w, proj_a=proj_a, proj_b=proj_b, proj_c=proj_c, w_out=w_out, loss_target=loss_target, m_norm_w=m_norm_w, m_w_in=m_w_in, m_s5_a_re=m_s5_a_re, m_s5_a_im=m_s5_a_im, m_s5_log_step=m_s5_log_step, m_s5_b_re=m_s5_b_re, m_s5_b_im=m_s5_b_im, m_s5_c_re=m_s5_c_re, m_s5_c_im=m_s5_c_im, m_s5_d=m_s5_d, m_s5_glu_w=m_s5_glu_w, m_s5_glu_b=m_s5_glu_b, m_q_norm_w=m_q_norm_w, m_k_norm_w=m_k_norm_w, m_conv_w=m_conv_w, m_conv_b=m_conv_b, m_dt_bias=m_dt_bias, m_ssd_a_log=m_ssd_a_log, m_ssd_d=m_ssd_d, m_ssd_norm_w=m_ssd_norm_w, m_proj_a=m_proj_a, m_proj_b=m_proj_b, m_proj_c=m_proj_c, m_w_out=m_w_out, v_norm_w=v_norm_w, v_w_in=v_w_in, v_s5_a_re=v_s5_a_re, v_s5_a_im=v_s5_a_im, v_s5_log_step=v_s5_log_step, v_s5_b_re=v_s5_b_re, v_s5_b_im=v_s5_b_im, v_s5_c_re=v_s5_c_re, v_s5_c_im=v_s5_c_im, v_s5_d=v_s5_d, v_s5_glu_w=v_s5_glu_w, v_s5_glu_b=v_s5_glu_b, v_q_norm_w=v_q_norm_w, v_k_norm_w=v_k_norm_w, v_conv_w=v_conv_w, v_conv_b=v_conv_b, v_dt_bias=v_dt_bias, v_ssd_a_log=v_ssd_a_log, v_ssd_d=v_ssd_d, v_ssd_norm_w=v_ssd_norm_w, v_proj_a=v_proj_a, v_proj_b=v_proj_b, v_proj_c=v_proj_c, v_w_out=v_w_out)
    weights = {n: given[n] for n in TWIN_WEIGHTS}
    shared = {n: given[n] for n in SHARED_INPUTS}
    per_example = {n: given[n] for n in ['x']}
    grad_fn = _jax.value_and_grad(_loss, argnums=(0, 1))

    def one_microbatch(ex, loss_target):
        ex = dict(ex)
        diff = ex.pop(TWIN_DIFF_INPUT)
        return grad_fn(weights, diff, {**shared, **ex}, loss_target)

    if N_MICROBATCH == 1:
        loss, (grad_w, grad_x) = one_microbatch(per_example, given["loss_target"])
    else:
        def body(carry, xs):
            loss_sum, grad_sum = carry
            l_k, (gw_k, gx_k) = one_microbatch(xs[0], xs[1])
            with _jax.named_scope("update"):
                return (loss_sum + l_k, _jax.tree.map(_jnp.add, grad_sum, gw_k)), gx_k

        init = (_jnp.zeros((), _jnp.float32), _jax.tree.map(_jnp.zeros_like, weights))
        (loss, grad_w), grad_x = _jax.lax.scan(body, init, (per_example, given["loss_target"]))
    with _jax.named_scope("update"):
        delta_w, new_m, new_v = {}, {}, {}
        for n in TWIN_WEIGHTS:
            delta_w[n], new_m[n], new_v[n] = _adamw(weights[n], grad_w[n], given["m_" + n], given["v_" + n])
    return (loss, grad_x, *[grad_w[n] for n in TWIN_WEIGHTS], *[delta_w[n] for n in TWIN_WEIGHTS],
            *[new_m[n] for n in TWIN_WEIGHTS], *[new_v[n] for n in TWIN_WEIGHTS])
```

```python
import functools

import jax
import jax.numpy as jnp
from jax import lax
from jax.experimental import pallas as pl
from jax.experimental.pallas import tpu as pltpu

f32 = jnp.float32
bf16 = jnp.bfloat16

D_MODEL = 1024
RMS_EPS = 1e-6
V7X_VMEM_LIMIT = 60 * 1024 * 1024
LANES = 128
NN, NT, TN = ((1,), (0,)), ((1,), (1,)), ((0,), (0,))

S5_STATES = 2048
S5_ROWS = 256
ATT_SEG = 2048
ATT_BLOCK = 128
SSD_CHUNK = 128
SSD_WIDTH = 768
SSD_XBC = 1280
CONV_ROWS = 512
TAIL_ROWS = 128

ADAM_LR, ADAM_B1, ADAM_B2, ADAM_EPS, ADAM_WD, ADAM_STEP = 0.001, 0.9, 0.999, 1e-08, 0.01, 10

_C_UA, _C_ZA, _C_Q, _C_K, _C_V, _C_ZB, _C_XBC, _C_DT, _C_ZC, _C_GATE, _C_END = (
    0, 512, 1024, 1792, 2560, 3328, 3584, 4864, 4876, 5644, 8716)

_SHARDED = (("w_in", 2), ("s5_glu_w", 1), ("conv_w", 2), ("proj_a", 2), ("proj_b", 2), ("proj_c", 2), ("w_out", 1))
_REPL = ("norm_w", "s5_a_re", "s5_a_im", "s5_log_step", "s5_b_re", "s5_b_im", "s5_c_re", "s5_c_im", "s5_d",
         "s5_glu_b", "q_norm_w", "k_norm_w", "conv_b", "dt_bias", "ssd_a_log", "ssd_d", "ssd_norm_w")
_WEIGHTS = ("norm_w", "w_in", "s5_a_re", "s5_a_im", "s5_log_step", "s5_b_re", "s5_b_im", "s5_c_re", "s5_c_im",
            "s5_d", "s5_glu_w", "s5_glu_b", "q_norm_w", "k_norm_w", "conv_w", "conv_b", "dt_bias", "ssd_a_log",
            "ssd_d", "ssd_norm_w", "proj_a", "proj_b", "proj_c", "w_out")
PACK_ROWS = 512


def _dot(a, b, dims):
    return lax.dot_general(a.astype(bf16), b.astype(bf16), (dims, ((), ())), preferred_element_type=f32)


def _call(body, name, grid, in_specs, out_specs, out_shape, scratch=(), sem=None):
    return pl.pallas_call(
        body, name=name, grid=grid, in_specs=in_specs, out_specs=out_specs, out_shape=out_shape,
        scratch_shapes=list(scratch),
        compiler_params=pltpu.CompilerParams(dimension_semantics=sem, vmem_limit_bytes=V7X_VMEM_LIMIT))


def _tile(n, options=(1024, 768, 512, 384, 256, 128)):
    return next(t for t in options if n % t == 0)


@functools.partial(jax.custom_vjp, nondiff_argnums=(2,))
def _bdot(a, b, dims):
    return _dot(a, b, dims)


def _bdot_fwd(a, b, dims):
    return _dot(a, b, dims), (a, b)


def _bdot_bwd(dims, res, g):
    a, b = res
    if dims == NN:
        da, db = _dot(g, b, NT), _dot(a, g, TN)
    elif dims == NT:
        da, db = _dot(g, b, NN), _dot(g, a, TN)
    else:
        da, db = _dot(b, g, NT), _dot(a, g, NN)
    return da.astype(a.dtype), db.astype(b.dtype)


_bdot.defvjp(_bdot_fwd, _bdot_bwd)


@functools.partial(jax.custom_vjp, nondiff_argnums=(2,))
def _cdot(a, w, dims):
    return _dot(a, w, dims)


def _cdot_fwd(a, w, dims):
    return _dot(a, w, dims), w


def _cdot_bwd(dims, w, g):
    da = _dot(g, w, NT) if dims == NN else _dot(g, w, NN)
    return da, jnp.zeros_like(w)


_cdot.defvjp(_cdot_fwd, _cdot_bwd)


def _split3(x):
    hi = x.astype(bf16)
    r = x - hi.astype(f32)
    mid = r.astype(bf16)
    lo = (r - mid.astype(f32)).astype(bf16)
    return hi, mid, lo


@jax.custom_vjp
def _xdot_r(x, m):
    return sum(_dot(p, m, NN) for p in _split3(x))


def _xdot_r_fwd(x, m):
    return _xdot_r(x, m), m


def _xdot_r_bwd(m, g):
    return sum(_dot(p, m, NT) for p in _split3(g)), jnp.zeros_like(m)


_xdot_r.defvjp(_xdot_r_fwd, _xdot_r_bwd)


@jax.custom_vjp
def _xdot_l(m, x):
    return sum(_dot(m, p, NN) for p in _split3(x))


def _xdot_l_fwd(m, x):
    return _xdot_l(m, x), m


def _xdot_l_bwd(m, g):
    return jnp.zeros_like(m), sum(_dot(m, p, TN) for p in _split3(g))


_xdot_l.defvjp(_xdot_l_fwd, _xdot_l_bwd)


@jax.custom_vjp
def _softplus(x):
    e = jnp.exp(-jnp.abs(x))
    u = 1.0 + e
    log1p = jnp.where(u == 1.0, e, jnp.log(u) * (e / jnp.where(u == 1.0, 1.0, u - 1.0)))
    return jnp.maximum(x, 0.0) + log1p


def _softplus_fwd(x):
    return _softplus(x), x


def _softplus_bwd(x, g):
    return (g * jax.nn.sigmoid(x),)


_softplus.defvjp(_softplus_fwd, _softplus_bwd)


def _rms(x, w):
    return x * lax.rsqrt(jnp.mean(x * x, axis=-1, keepdims=True) + RMS_EPS) * w


def mm_nn(a, b, name, tm=512):
    M, K = a.shape
    N = b.shape[1]
    tn = _tile(N)

    def body(a_ref, b_ref, o_ref):
        o_ref[...] = _dot(a_ref[...], b_ref[...], NN)

    return _call(body, name, (M // tm, N // tn),
                 [pl.BlockSpec((tm, K), lambda i, j: (i, 0)), pl.BlockSpec((K, tn), lambda i, j: (0, j))],
                 pl.BlockSpec((tm, tn), lambda i, j: (i, j)), jax.ShapeDtypeStruct((M, N), f32),
                 sem=("parallel", "parallel"))(a, b)


def mm_nt(a, b, name, acc=None, tm=512):
    M, K = a.shape
    N = b.shape[0]
    tk = _tile(K)
    has_acc = acc is not None

    def body(*refs):
        a_ref, b_ref = refs[0], refs[1]
        o_ref = refs[-1]
        k = pl.program_id(1)
        p = _dot(a_ref[...], b_ref[...], NT)

        @pl.when(k == 0)
        def _():
            o_ref[...] = p + refs[2][...] if has_acc else p

        @pl.when(k > 0)
        def _():
            o_ref[...] += p

    specs = [pl.BlockSpec((tm, tk), lambda i, k: (i, k)), pl.BlockSpec((N, tk), lambda i, k: (0, k))]
    args = [a, b]
    if has_acc:
        specs.append(pl.BlockSpec((tm, N), lambda i, k: (i, 0)))
        args.append(acc)
    return _call(body, name, (M // tm, K // tk), specs, pl.BlockSpec((tm, N), lambda i, k: (i, 0)),
                 jax.ShapeDtypeStruct((M, N), f32), sem=("parallel", "arbitrary"))(*args)


def mm_tn(a, b, name, tk=512):
    K, M = a.shape
    N = b.shape[1]
    tn = _tile(N)

    def body(a_ref, b_ref, o_ref):
        k = pl.program_id(1)
        p = _dot(a_ref[...], b_ref[...], TN)

        @pl.when(k == 0)
        def _():
            o_ref[...] = p

        @pl.when(k > 0)
        def _():
            o_ref[...] += p

    return _call(body, name, (N // tn, K // tk),
                 [pl.BlockSpec((tk, M), lambda j, k: (k, 0)), pl.BlockSpec((tk, tn), lambda j, k: (k, j))],
                 pl.BlockSpec((M, tn), lambda j, k: (0, j)), jax.ShapeDtypeStruct((M, N), f32),
                 sem=("parallel", "arbitrary"))(a, b)


def rms_fwd(x, w, name, tm=512):
    S = x.shape[0]

    def body(x_ref, w_ref, o_ref):
        o_ref[...] = _rms(x_ref[...], w_ref[...]).astype(bf16)

    return _call(body, name, (S // tm,),
                 [pl.BlockSpec((tm, D_MODEL), lambda i: (i, 0)), pl.BlockSpec((1, D_MODEL), lambda i: (0, 0))],
                 pl.BlockSpec((tm, D_MODEL), lambda i: (i, 0)), jax.ShapeDtypeStruct((S, D_MODEL), bf16),
                 sem=("parallel",))(x, w)


def rms_bwd(x, w, dh, dres, name, tm=512):
    S = x.shape[0]

    def body(x_ref, w_ref, dh_ref, dr_ref, dx_ref, dw_ref):
        _, vjp = jax.vjp(_rms, x_ref[...], w_ref[...])
        dx, dw = vjp(dh_ref[...])
        dx_ref[...] = dx + dr_ref[...]

        @pl.when(pl.program_id(0) == 0)
        def _():
            dw_ref[...] = dw

        @pl.when(pl.program_id(0) > 0)
        def _():
            dw_ref[...] += dw

    row = pl.BlockSpec((tm, D_MODEL), lambda i: (i, 0))
    vec = pl.BlockSpec((1, D_MODEL), lambda i: (0, 0))
    return _call(body, name, (S // tm,), [row, vec, row, row], [row, vec],
                 [jax.ShapeDtypeStruct((S, D_MODEL), f32), jax.ShapeDtypeStruct((1, D_MODEL), f32)],
                 sem=("arbitrary",))(x, w, dh, dres)


def loss_and_grad(y, target, name, tm=512):
    S = y.shape[0]

    def body(y_ref, t_ref, dy_ref, l_ref):
        diff = y_ref[...] - t_ref[...]
        dy_ref[...] = diff * (1.0 / D_MODEL)
        part = jnp.full((8, LANES), 0.5 / D_MODEL * jnp.sum(diff * diff), f32)

        @pl.when(pl.program_id(0) == 0)
        def _():
            l_ref[...] = part

        @pl.when(pl.program_id(0) > 0)
        def _():
            l_ref[...] += part

    row = pl.BlockSpec((tm, D_MODEL), lambda i: (i, 0))
    return _call(body, name, (S // tm,), [row, row], [row, pl.BlockSpec((8, LANES), lambda i: (0, 0))],
                 [jax.ShapeDtypeStruct((S, D_MODEL), f32), jax.ShapeDtypeStruct((8, LANES), f32)],
                 sem=("arbitrary",))(y, target)


def _s5_discretize(a_re, a_im, log_step, b_re, b_im, c_re, c_im):
    step = jnp.exp(log_step)[:, None]
    mag = jnp.exp(a_re * step)
    ang = a_im * step
    lam_re, lam_im = mag * jnp.cos(ang), mag * jnp.sin(ang)
    num_re, num_im = lam_re - 1.0, lam_im
    den = a_re * a_re + a_im * a_im
    f_re = (num_re * a_re + num_im * a_im) / den
    f_im = (num_im * a_re - num_re * a_im) / den
    bb_re = f_re[..., None] * b_re - f_im[..., None] * b_im
    bb_im = f_re[..., None] * b_im + f_im[..., None] * b_re
    eye = jnp.eye(8, dtype=f32)

    def block_in(bb):
        t = bb.transpose(0, 2, 1).reshape(4, 8, 16, 1, 64)
        return (t * eye[None, :, None, :, None]).reshape(4, 128, 512)

    def block_out(c):
        t = c.transpose(0, 2, 1).reshape(4, 8, 64, 1, 16)
        return (t * eye[None, :, None, :, None]).reshape(4, 512, 128)

    return (lam_re.reshape(1, S5_STATES), lam_im.reshape(1, S5_STATES), block_in(bb_re), block_in(bb_im),
            block_out(c_re), block_out(c_im))


def _lam_powers(lam_re, lam_im):
    rows_re, rows_im = [lam_re], [lam_im]
    for _ in range(7):
        pr, pi = rows_re[-1], rows_im[-1]
        rows_re.append(pr * lam_re - pi * lam_im)
        rows_im.append(pr * lam_im + pi * lam_re)
    return jnp.concatenate(rows_re, 0), jnp.concatenate(rows_im, 0)


def s5_fwd(u, pw_re, pw_im, w_re, w_im, c_re, c_im, dvec, name):
    S = u.shape[0]
    R, NS = S5_ROWS, S5_STATES
    nb = R // 8

    def body(u_ref, pwr_ref, pwi_ref, wre_ref, wim_ref, cre_ref, cim_ref, d_ref, y_ref, hr_ref, hi_ref,
             car_re, car_im, cin_re, cin_im):
        @pl.when(pl.program_id(0) == 0)
        def _():
            car_re[...] = jnp.zeros_like(car_re)
            car_im[...] = jnp.zeros_like(car_im)

        u = u_ref[...]
        for j in range(4):
            uj = u[:, 128 * j:128 * (j + 1)]
            hr_ref[:, :, 512 * j:512 * (j + 1)] = _dot(uj, wre_ref[j], NN).reshape(nb, 8, 512)
            hi_ref[:, :, 512 * j:512 * (j + 1)] = _dot(uj, wim_ref[j], NN).reshape(nb, 8, 512)
        lr, li = pwr_ref[0:1, :], pwi_ref[0:1, :]
        for r in range(1, 8):
            pr, pi = hr_ref[:, r - 1, :], hi_ref[:, r - 1, :]
            hr_ref[:, r, :] = lr * pr - li * pi + hr_ref[:, r, :]
            hi_ref[:, r, :] = lr * pi + li * pr + hi_ref[:, r, :]
        l8r, l8i = pwr_ref[7:8, :], pwi_ref[7:8, :]

        def across(c, carry):
            gr, gi = carry
            cin_re[pl.ds(c, 1), :] = gr
            cin_im[pl.ds(c, 1), :] = gi
            er, ei = hr_ref[c, 7:8, :], hi_ref[c, 7:8, :]
            return l8r * gr - l8i * gi + er, l8r * gi + l8i * gr + ei

        gr, gi = lax.fori_loop(0, nb, across, (car_re[...], car_im[...]))
        car_re[...] = gr
        car_im[...] = gi
        cr, ci = cin_re[...], cin_im[...]
        for r in range(8):
            pr, pi = pwr_ref[r:r + 1, :], pwi_ref[r:r + 1, :]
            hr_ref[:, r, :] = hr_ref[:, r, :] + pr * cr - pi * ci
            hi_ref[:, r, :] = hi_ref[:, r, :] + pr * ci + pi * cr
        for j in range(4):
            sl = slice(512 * j, 512 * (j + 1))
            hrj = hr_ref[:, :, sl].reshape(R, 512)
            hij = hi_ref[:, :, sl].reshape(R, 512)
            y_ref[:, 128 * j:128 * (j + 1)] = (_dot(hrj, cre_ref[j], NN) - _dot(hij, cim_ref[j], NN)
                                               + d_ref[:, 128 * j:128 * (j + 1)] * u[:, 128 * j:128 * (j + 1)])

    full = lambda shape: pl.BlockSpec(shape, lambda i: (0,) * len(shape))
    hspec = pl.BlockSpec((nb, 8, NS), lambda i: (i, 0, 0))
    return _call(
        body, name, (S // R,),
        [pl.BlockSpec((R, 512), lambda i: (i, 0)), full((8, NS)), full((8, NS)), full((4, 128, 512)),
         full((4, 128, 512)), full((4, 512, 128)), full((4, 512, 128)), full((1, 512))],
        [pl.BlockSpec((R, 512), lambda i: (i, 0)), hspec, hspec],
        [jax.ShapeDtypeStruct((S, 512), f32), jax.ShapeDtypeStruct((S // 8, 8, NS), f32),
         jax.ShapeDtypeStruct((S // 8, 8, NS), f32)],
        scratch=[pltpu.VMEM((1, NS), f32), pltpu.VMEM((1, NS), f32), pltpu.VMEM((nb, NS), f32),
                 pltpu.VMEM((nb, NS), f32)],
        sem=("arbitrary",))(u, pw_re, pw_im, w_re.astype(bf16), w_im.astype(bf16), c_re.astype(bf16),
                            c_im.astype(bf16), dvec)


def s5_bwd(dy, u, h_re, h_im, pw_re, pw_im, w_re, w_im, c_re, c_im, dvec, name):
    S = u.shape[0]
    R, NS = S5_ROWS, S5_STATES
    nb = R // 8
    nchunk = S // R

    def body(dy_ref, u_ref, hr_ref, hi_ref, hpr_ref, hpi_ref, pwr_ref, pwi_ref, wre_ref, wim_ref, cre_ref, cim_ref,
             d_ref, du_ref, dwre_ref, dwim_ref, dcre_ref, dcim_ref, dlr_ref, dli_ref, dd_ref,
             ar, ai, car_re, car_im, cin_re, cin_im):
        i = pl.program_id(0)

        @pl.when(i == 0)
        def _():
            for ref in (car_re, car_im, dwre_ref, dwim_ref, dcre_ref, dcim_ref, dlr_ref, dli_ref, dd_ref):
                ref[...] = jnp.zeros_like(ref)

        dy = dy_ref[...]
        u = u_ref[...]
        for j in range(4):
            dyj = dy[:, 128 * j:128 * (j + 1)]
            ar[:, :, 512 * j:512 * (j + 1)] = _dot(dyj, cre_ref[j], NT).reshape(nb, 8, 512)
            ai[:, :, 512 * j:512 * (j + 1)] = -_dot(dyj, cim_ref[j], NT).reshape(nb, 8, 512)
        lr, li = pwr_ref[0:1, :], pwi_ref[0:1, :]
        for r in range(6, -1, -1):
            nr, ni = ar[:, r + 1, :], ai[:, r + 1, :]
            ar[:, r, :] = lr * nr + li * ni + ar[:, r, :]
            ai[:, r, :] = lr * ni - li * nr + ai[:, r, :]
        l8r, l8i = pwr_ref[7:8, :], pwi_ref[7:8, :]

        def across(k, carry):
            c = nb - 1 - k
            gr, gi = carry
            cin_re[pl.ds(c, 1), :] = gr
            cin_im[pl.ds(c, 1), :] = gi
            er, ei = ar[c, 0:1, :], ai[c, 0:1, :]
            return l8r * gr + l8i * gi + er, l8r * gi - l8i * gr + ei

        gr, gi = lax.fori_loop(0, nb, across, (car_re[...], car_im[...]))
        car_re[...] = gr
        car_im[...] = gi
        cr, ci = cin_re[...], cin_im[...]
        for r in range(8):
            pr, pi = pwr_ref[7 - r:8 - r, :], pwi_ref[7 - r:8 - r, :]
            ar[:, r, :] = ar[:, r, :] + pr * cr + pi * ci
            ai[:, r, :] = ai[:, r, :] + pr * ci - pi * cr

        acc_r = jnp.zeros((1, NS), f32)
        acc_i = jnp.zeros((1, NS), f32)
        for r in range(1, 8):
            xr, xi = hr_ref[:, r - 1, :], hi_ref[:, r - 1, :]
            br, bi = ar[:, r, :], ai[:, r, :]
            acc_r += jnp.sum(br * xr + bi * xi, axis=0, keepdims=True)
            acc_i += jnp.sum(bi * xr - br * xi, axis=0, keepdims=True)
        xr, xi = hr_ref[0:nb - 1, 7, :], hi_ref[0:nb - 1, 7, :]
        br, bi = ar[1:nb, 0, :], ai[1:nb, 0, :]
        acc_r += jnp.sum(br * xr + bi * xi, axis=0, keepdims=True)
        acc_i += jnp.sum(bi * xr - br * xi, axis=0, keepdims=True)
        has_prev = (i < nchunk - 1).astype(f32)
        xr, xi = hpr_ref[0, 7:8, :] * has_prev, hpi_ref[0, 7:8, :] * has_prev
        br, bi = ar[0, 0:1, :], ai[0, 0:1, :]
        dlr_ref[...] += acc_r + br * xr + bi * xi
        dli_ref[...] += acc_i + bi * xr - br * xi
        dd_ref[...] += jnp.sum(dy * u, axis=0, keepdims=True)

        for j in range(4):
            sl = slice(512 * j, 512 * (j + 1))
            cs = slice(128 * j, 128 * (j + 1))
            arj = ar[:, :, sl].reshape(R, 512)
            aij = ai[:, :, sl].reshape(R, 512)
            uj, dyj = u[:, cs], dy[:, cs]
            du_ref[:, cs] = _dot(arj, wre_ref[j], NT) + _dot(aij, wim_ref[j], NT) + d_ref[:, cs] * dyj
            dwre_ref[j] += _dot(uj, arj, TN)
            dwim_ref[j] += _dot(uj, aij, TN)
            dcre_ref[j] += _dot(hr_ref[:, :, sl].reshape(R, 512), dyj, TN)
            dcim_ref[j] -= _dot(hi_ref[:, :, sl].reshape(R, 512), dyj, TN)

    rev = lambda i: nchunk - 1 - i
    full = lambda shape: pl.BlockSpec(shape, lambda i: (0,) * len(shape))
    row = pl.BlockSpec((R, 512), lambda i: (rev(i), 0))
    hspec = pl.BlockSpec((nb, 8, NS), lambda i: (rev(i), 0, 0))
    hprev = pl.BlockSpec((1, 8, NS), lambda i: (jnp.maximum(rev(i) * nb - 1, 0), 0, 0))
    outs = _call(
        body, name, (nchunk,),
        [row, row, hspec, hspec, hprev, hprev, full((8, NS)), full((8, NS)), full((4, 128, 512)), full((4, 128, 512)),
         full((4, 512, 128)), full((4, 512, 128)), full((1, 512))],
        [row, full((4, 128, 512)), full((4, 128, 512)), full((4, 512, 128)), full((4, 512, 128)),
         full((1, NS)), full((1, NS)), full((1, 512))],
        [jax.ShapeDtypeStruct((S, 512), f32), jax.ShapeDtypeStruct((4, 128, 512), f32),
         jax.ShapeDtypeStruct((4, 128, 512), f32), jax.ShapeDtypeStruct((4, 512, 128), f32),
         jax.ShapeDtypeStruct((4, 512, 128), f32), jax.ShapeDtypeStruct((1, NS), f32),
         jax.ShapeDtypeStruct((1, NS), f32), jax.ShapeDtypeStruct((1, 512), f32)],
        scratch=[pltpu.VMEM((nb, 8, NS), f32), pltpu.VMEM((nb, 8, NS), f32), pltpu.VMEM((1, NS), f32),
                 pltpu.VMEM((1, NS), f32), pltpu.VMEM((nb, NS), f32), pltpu.VMEM((nb, NS), f32)],
        sem=("arbitrary",))(dy, u, h_re, h_im, h_re, h_im, pw_re, pw_im, w_re.astype(bf16), w_im.astype(bf16),
                            c_re.astype(bf16), c_im.astype(bf16), dvec)
    return outs


def _rows(start, n, d):
    return pl.ds(pl.multiple_of(start, ATT_BLOCK), n) if d == 1 else pl.ds(start, n, stride=d)


def _att_block(q, k, v, qw, kw, has_prev):
    lane = lax.broadcasted_iota(jnp.int32, (1, LANES), 1)
    hm = [(lane < 64).astype(f32), (lane >= 64).astype(f32)]

    def head_norm(x, w):
        x2 = x * x
        sc = sum(hm[h] * lax.rsqrt(jnp.sum(x2 * hm[h], axis=-1, keepdims=True) * (1.0 / 64) + RMS_EPS)
                 for h in range(2))
        return x * sc * w

    qn, kn = head_norm(q, qw), head_norm(k, kw)
    qi = lax.broadcasted_iota(jnp.int32, (ATT_BLOCK, 2 * ATT_BLOCK), 0) + ATT_BLOCK
    kj = lax.broadcasted_iota(jnp.int32, (ATT_BLOCK, 2 * ATT_BLOCK), 1)
    mask = (qi - kj >= 0) & (qi - kj <= ATT_BLOCK) & (has_prev | (kj >= ATT_BLOCK))
    o = jnp.zeros((ATT_BLOCK, LANES), f32)
    lse = jnp.zeros((ATT_BLOCK, LANES), f32)
    for h in range(2):
        s = _bdot(qn * hm[h], kn, NT) * 0.125
        s = jnp.where(mask, s, -jnp.inf)
        m = jnp.max(s, axis=-1, keepdims=True)
        p = jnp.exp(s - m)
        l = jnp.sum(p, axis=-1, keepdims=True)
        o = o + hm[h] * _bdot(p / l, v, NN)
        lse = lse + hm[h] * (m + jnp.log(l))
    return o, lse


def att_fwd(p_att, qw, kw, d, name):
    S = p_att.shape[0]
    SEG = ATT_SEG
    nblk = SEG // ATT_BLOCK

    def body(p_ref, qw_ref, kw_ref, o_ref, l_ref, q_s, k_ext, v_ext, o_s, l_s):
        seg = pl.program_id(1)

        @pl.when(seg == 0)
        def _():
            k_ext[SEG:, :] = jnp.zeros((SEG, LANES), f32)
            v_ext[SEG:, :] = jnp.zeros((SEG, LANES), f32)

        k_ext[:SEG, :] = k_ext[SEG:, :]
        v_ext[:SEG, :] = v_ext[SEG:, :]
        q_s[...] = p_ref[:, 0:128]
        k_ext[SEG:, :] = p_ref[:, 128:256]
        v_ext[SEG:, :] = p_ref[:, 256:384]
        qw_v, kw_v = qw_ref[...], kw_ref[...]

        def blk(b, carry):
            j, r = b // d, b % d
            qs = j * (ATT_BLOCK * d) + r
            ks = SEG + qs - ATT_BLOCK * d
            o, lse = _att_block(q_s[_rows(qs, ATT_BLOCK, d), :], k_ext[_rows(ks, 2 * ATT_BLOCK, d), :],
                                v_ext[_rows(ks, 2 * ATT_BLOCK, d), :], qw_v, kw_v, (seg > 0) | (j > 0))
            o_s[_rows(qs, ATT_BLOCK, d), :] = o
            l_s[_rows(qs, ATT_BLOCK, d), :] = lse
            return carry

        lax.fori_loop(0, nblk, blk, 0)
        o_ref[...] = o_s[...]
        l_ref[...] = l_s[...]

    vec = pl.BlockSpec((1, LANES), lambda hh, s: (0, 0))
    out = pl.BlockSpec((SEG, LANES), lambda hh, s: (s, hh))
    return _call(body, name, (2, S // SEG), [pl.BlockSpec((SEG, 384), lambda hh, s: (s, hh)), vec, vec],
                 [out, out], [jax.ShapeDtypeStruct((S, 256), f32), jax.ShapeDtypeStruct((S, 256), f32)],
                 scratch=[pltpu.VMEM((SEG, LANES), f32), pltpu.VMEM((2 * SEG, LANES), f32),
                          pltpu.VMEM((2 * SEG, LANES), f32), pltpu.VMEM((SEG, LANES), f32),
                          pltpu.VMEM((SEG, LANES), f32)],
                 sem=("arbitrary", "arbitrary"))(p_att, qw, kw)


def att_bwd(p_att, do, dlse, qw, kw, d, name):
    S = p_att.shape[0]
    SEG = ATT_SEG
    nseg = S // SEG
    nblk = SEG // ATT_BLOCK

    def body(p_ref, pp_ref, do_ref, dl_ref, qw_ref, kw_ref, dp_ref, dqw_ref, dkw_ref,
             q_s, k_ext, v_ext, dq_s, dk_ext, dv_ext):
        hh, i = pl.program_id(0), pl.program_id(1)
        seg = nseg - 1 - i

        @pl.when(i == 0)
        def _():
            dk_ext[...] = jnp.zeros_like(dk_ext)
            dv_ext[...] = jnp.zeros_like(dv_ext)

        @pl.when((i == 0) & (hh == 0))
        def _():
            dqw_ref[...] = jnp.zeros_like(dqw_ref)
            dkw_ref[...] = jnp.zeros_like(dkw_ref)

        dk_ext[SEG:, :] = dk_ext[:SEG, :]
        dv_ext[SEG:, :] = dv_ext[:SEG, :]
        dk_ext[:SEG, :] = jnp.zeros((SEG, LANES), f32)
        dv_ext[:SEG, :] = jnp.zeros((SEG, LANES), f32)
        q_s[...] = p_ref[:, 0:128]
        k_ext[SEG:, :] = p_ref[:, 128:256]
        v_ext[SEG:, :] = p_ref[:, 256:384]
        k_ext[:SEG, :] = pp_ref[:, 128:256]
        v_ext[:SEG, :] = pp_ref[:, 256:384]
        qw_v, kw_v = qw_ref[...], kw_ref[...]

        def blk(b, carry):
            dqw, dkw = carry
            j, r = b // d, b % d
            qs = j * (ATT_BLOCK * d) + r
            ks = SEG + qs - ATT_BLOCK * d
            has_prev = (seg > 0) | (j > 0)
            qrows, krows = _rows(qs, ATT_BLOCK, d), _rows(ks, 2 * ATT_BLOCK, d)
            _, vjp = jax.vjp(lambda q, k, v, a, b_: _att_block(q, k, v, a, b_, has_prev),
                             q_s[qrows, :], k_ext[krows, :], v_ext[krows, :], qw_v, kw_v)
            dq, dk, dv, dqw_b, dkw_b = vjp((do_ref[qrows, :], dl_ref[qrows, :]))
            dq_s[qrows, :] = dq
            dk_ext[krows, :] = dk_ext[krows, :] + dk
            dv_ext[krows, :] = dv_ext[krows, :] + dv
            return dqw + dqw_b, dkw + dkw_b

        zero = jnp.zeros((1, LANES), f32)
        dqw, dkw = lax.fori_loop(0, nblk, blk, (zero, zero))
        dqw_ref[...] += dqw
        dkw_ref[...] += dkw
        dp_ref[:, 0:128] = dq_s[...]
        dp_ref[:, 128:256] = dk_ext[SEG:, :]
        dp_ref[:, 256:384] = dv_ext[SEG:, :]

    rev = lambda i: nseg - 1 - i
    vec = pl.BlockSpec((1, LANES), lambda hh, i: (0, 0))
    cur = pl.BlockSpec((SEG, 384), lambda hh, i: (rev(i), hh))
    prev = pl.BlockSpec((SEG, 384), lambda hh, i: (jnp.maximum(rev(i) - 1, 0), hh))
    col = pl.BlockSpec((SEG, LANES), lambda hh, i: (rev(i), hh))
    big = pltpu.VMEM((2 * SEG, LANES), f32)
    one = pltpu.VMEM((SEG, LANES), f32)
    return _call(body, name, (2, nseg), [cur, prev, col, col, vec, vec], [cur, vec, vec],
                 [jax.ShapeDtypeStruct((S, 768), f32), jax.ShapeDtypeStruct((1, LANES), f32),
                  jax.ShapeDtypeStruct((1, LANES), f32)],
                 scratch=[one, big, big, one, big, big],
                 sem=("arbitrary", "arbitrary"))(p_att, p_att, do, dlse, qw, kw)


def conv_fwd(p_ssd, conv_w, conv_b, name):
    S = p_ssd.shape[0]
    tm, C = CONV_ROWS, SSD_XBC

    def body(x_ref, xp_ref, w_ref, b_ref, o_ref, ext):
        first = (pl.program_id(0) == 0)
        ext[0:8, :] = jnp.where(first, 0.0, xp_ref[:, 0:C])
        ext[8:, :] = x_ref[:, 0:C]
        acc = b_ref[...] + w_ref[3:4, :] * ext[pl.ds(8, tm), :]
        for k in range(1, 4):
            acc = acc + w_ref[3 - k:4 - k, :] * ext[pl.ds(8 - k, tm), :]
        o_ref[...] = jax.nn.silu(acc)

    return _call(body, name, (S // tm,),
                 [pl.BlockSpec((tm, 1536), lambda i: (i, 0)),
                  pl.BlockSpec((8, 1536), lambda i: (jnp.maximum(i * (tm // 8) - 1, 0), 0)),
                  pl.BlockSpec((4, C), lambda i: (0, 0)), pl.BlockSpec((1, C), lambda i: (0, 0))],
                 pl.BlockSpec((tm, C), lambda i: (i, 0)), jax.ShapeDtypeStruct((S, C), f32),
                 scratch=[pltpu.VMEM((tm + 8, C), f32)], sem=("parallel",))(p_ssd, p_ssd, conv_w, conv_b)


def conv_bwd(p_ssd, dact, ddt, conv_w, conv_b, name):
    S = p_ssd.shape[0]
    tm, C = CONV_ROWS, SSD_XBC
    nblk = S // tm

    def body(x_ref, xp_ref, xn_ref, da_ref, dan_ref, ddt_ref, w_ref, b_ref, dp_ref, dw_ref, db_ref, ext, dpre):
        i = pl.program_id(0)
        ext[0:8, :] = jnp.where(i == 0, 0.0, xp_ref[:, 0:C])
        ext[8:tm + 8, :] = x_ref[:, 0:C]
        ext[tm + 8:, :] = xn_ref[:, 0:C]
        pre = b_ref[...] + w_ref[3:4, :] * ext[pl.ds(8, tm + 8), :]
        for k in range(1, 4):
            pre = pre + w_ref[3 - k:4 - k, :] * ext[pl.ds(8 - k, tm + 8), :]
        sg = jax.nn.sigmoid(pre)
        dsilu = sg * (1.0 + pre * (1.0 - sg))
        dpre[0:tm, :] = da_ref[...] * dsilu[0:tm, :]
        dpre[tm:, :] = jnp.where(i == nblk - 1, 0.0, dan_ref[...] * dsilu[tm:, :])
        dx = w_ref[3:4, :] * dpre[pl.ds(0, tm), :]
        for k in range(1, 4):
            dx = dx + w_ref[3 - k:4 - k, :] * dpre[pl.ds(k, tm), :]
        dp_ref[:, 0:C] = dx
        dp_ref[:, C:C + 128] = ddt_ref[...]
        dp_ref[:, C + 128:] = jnp.zeros((tm, 128), f32)
        dcur = dpre[pl.ds(0, tm), :]
        dws = [jnp.sum(dcur * ext[pl.ds(8 - (3 - j), tm), :], axis=0, keepdims=True) for j in range(4)]
        dbs = jnp.sum(dcur, axis=0, keepdims=True)

        @pl.when(i == 0)
        def _():
            dw_ref[...] = jnp.zeros_like(dw_ref)
            db_ref[...] = jnp.zeros_like(db_ref)

        for j in range(4):
            dw_ref[j:j + 1, :] += dws[j]
        db_ref[...] += dbs

    t8 = tm // 8
    return _call(body, name, (nblk,),
                 [pl.BlockSpec((tm, 1536), lambda i: (i, 0)),
                  pl.BlockSpec((8, 1536), lambda i: (jnp.maximum(i * t8 - 1, 0), 0)),
                  pl.BlockSpec((8, 1536), lambda i: (jnp.minimum((i + 1) * t8, S // 8 - 1), 0)),
                  pl.BlockSpec((tm, C), lambda i: (i, 0)),
                  pl.BlockSpec((8, C), lambda i: (jnp.minimum((i + 1) * t8, S // 8 - 1), 0)),
                  pl.BlockSpec((tm, 128), lambda i: (i, 0)),
                  pl.BlockSpec((4, C), lambda i: (0, 0)), pl.BlockSpec((1, C), lambda i: (0, 0))],
                 [pl.BlockSpec((tm, 1536), lambda i: (i, 0)), pl.BlockSpec((4, C), lambda i: (0, 0)),
                  pl.BlockSpec((1, C), lambda i: (0, 0))],
                 [jax.ShapeDtypeStruct((S, 1536), f32), jax.ShapeDtypeStruct((4, C), f32),
                  jax.ShapeDtypeStruct((1, C), f32)],
                 scratch=[pltpu.VMEM((tm + 16, C), f32), pltpu.VMEM((tm + 8, C), f32)],
                 sem=("arbitrary",))(p_ssd, p_ssd, p_ssd, dact, dact, ddt, conv_w, conv_b)


def _ssd_chunk(xbc, dtr, state, dt_bias, a_log, d_full):
    T = SSD_CHUNK
    r_i = lax.broadcasted_iota(jnp.int32, (T, T), 0)
    c_i = lax.broadcasted_iota(jnp.int32, (T, T), 1)
    tril = c_i <= r_i
    tri = tril.astype(bf16)
    e_rows = lax.broadcasted_iota(jnp.int32, (T, SSD_WIDTH), 0)
    e_cols = lax.broadcasted_iota(jnp.int32, (T, SSD_WIDTH), 1)
    expand = (e_cols // 64 == e_rows).astype(bf16)
    w_rows = lax.broadcasted_iota(jnp.int32, (T, 12 * T), 0)
    w_cols = lax.broadcasted_iota(jnp.int32, (T, 12 * T), 1)
    expand_wide = (w_cols // T == w_rows).astype(bf16)
    lane = lax.broadcasted_iota(jnp.int32, (1, LANES), 1)
    hm = [(lane < 64).astype(f32), (lane >= 64).astype(f32)]

    xs, bm, cm = xbc[:, :768], xbc[:, 768:1024], xbc[:, 1024:1280]
    dt = _softplus(dtr + dt_bias)
    a_dt = dt * (-jnp.exp(a_log))
    a_cs = _xdot_l(tri, a_dt)
    dt_full = _xdot_r(dt, expand)
    acs_full = _xdot_r(a_cs, expand)
    acs_wide = _xdot_r(a_cs, expand_wide)
    last = lax.broadcasted_iota(jnp.int32, (T, SSD_WIDTH), 0) == T - 1
    tot_full = jnp.sum(jnp.where(last, acs_full, 0.0), axis=0, keepdims=True)
    xdt = xs * dt_full
    xw = xdt * jnp.exp(tot_full - acs_full)
    eacs = jnp.exp(acs_full)
    st_parts, off_parts, diag_parts = [], [], []
    for g in range(2):
        bg, cg = bm[:, 128 * g:128 * (g + 1)], cm[:, 128 * g:128 * (g + 1)]
        cols = slice(384 * g, 384 * (g + 1))
        st_parts.append(_bdot(bg, xw[:, cols], TN))
        off_parts.append(_bdot(cg, state[:, cols], NN))
        cb = _bdot(cg, bg, NT)
        for pp in range(3 * g, 3 * g + 3):
            xp = xdt[:, 128 * pp:128 * (pp + 1)]
            acc = jnp.zeros((T, LANES), f32)
            for hh in range(2):
                a_col = acs_wide[:, T * (2 * pp + hh):T * (2 * pp + hh + 1)]
                decay = jnp.where(tril, jnp.exp(jnp.minimum(a_col - a_col.T, 0.0)), 0.0)
                acc = acc + _bdot(cb * decay, xp * hm[hh], NN)
            diag_parts.append(acc)
    new_state = state * jnp.exp(tot_full) + jnp.concatenate(st_parts, axis=1)
    y = jnp.concatenate(diag_parts, axis=1) + jnp.concatenate(off_parts, axis=1) * eacs + xs * d_full
    return y, new_state


def ssd_fwd(xact, p_ssd, dt_bias, a_log, d_full, name):
    S = xact.shape[0]
    T = SSD_CHUNK

    def body(x_ref, p_ref, b_ref, a_ref, d_ref, y_ref, s_ref, state):
        @pl.when(pl.program_id(0) == 0)
        def _():
            state[...] = jnp.zeros_like(state)

        st = state[...]
        s_ref[0] = st
        y, new = _ssd_chunk(x_ref[...], p_ref[...], st, b_ref[...], a_ref[...], d_ref[...])
        y_ref[...] = y
        state[...] = new

    vec = lambda n: pl.BlockSpec((1, n), lambda i: (0, 0))
    return _call(body, name, (S // T,),
                 [pl.BlockSpec((T, SSD_XBC), lambda i: (i, 0)), pl.BlockSpec((T, 128), lambda i: (i, 10)),
                  vec(128), vec(128), vec(768)],
                 [pl.BlockSpec((T, 768), lambda i: (i, 0)), pl.BlockSpec((1, T, 768), lambda i: (i, 0, 0))],
                 [jax.ShapeDtypeStruct((S, 768), f32), jax.ShapeDtypeStruct((S // T, T, 768), f32)],
                 scratch=[pltpu.VMEM((T, 768), f32)], sem=("arbitrary",))(xact, p_ssd, dt_bias, a_log, d_full)


def ssd_bwd(xact, p_ssd, states, dy, dt_bias, a_log, d_full, name):
    S = xact.shape[0]
    T = SSD_CHUNK
    nc = S // T

    def body(x_ref, p_ref, s_ref, dy_ref, b_ref, a_ref, d_ref, dx_ref, ddt_ref, db_ref, da_ref, dd_ref, dstate):
        i = pl.program_id(0)

        @pl.when(i == 0)
        def _():
            for ref in (dstate, db_ref, da_ref, dd_ref):
                ref[...] = jnp.zeros_like(ref)

        _, vjp = jax.vjp(_ssd_chunk, x_ref[...], p_ref[...], s_ref[0], b_ref[...], a_ref[...], d_ref[...])
        dx, ddt, dst, db, da, dd = vjp((dy_ref[...], dstate[...]))
        dx_ref[...] = dx
        ddt_ref[...] = ddt
        dstate[...] = dst
        db_ref[...] += db
        da_ref[...] += da
        dd_ref[...] += dd

    rev = lambda i: nc - 1 - i
    vec = lambda n: pl.BlockSpec((1, n), lambda i: (0, 0))
    return _call(body, name, (nc,),
                 [pl.BlockSpec((T, SSD_XBC), lambda i: (rev(i), 0)), pl.BlockSpec((T, 128), lambda i: (rev(i), 10)),
                  pl.BlockSpec((1, T, 768), lambda i: (rev(i), 0, 0)), pl.BlockSpec((T, 768), lambda i: (rev(i), 0)),
                  vec(128), vec(128), vec(768)],
                 [pl.BlockSpec((T, SSD_XBC), lambda i: (rev(i), 0)), pl.BlockSpec((T, 128), lambda i: (rev(i), 0)),
                  vec(128), vec(128), vec(768)],
                 [jax.ShapeDtypeStruct((S, SSD_XBC), f32), jax.ShapeDtypeStruct((S, 128), f32),
                  jax.ShapeDtypeStruct((1, 128), f32), jax.ShapeDtypeStruct((1, 128), f32),
                  jax.ShapeDtypeStruct((1, 768), f32)],
                 scratch=[pltpu.VMEM((T, 768), f32)],
                 sem=("arbitrary",))(xact, p_ssd, states, dy, dt_bias, a_log, d_full)


def _tail_fn(ys5, pt, o0, o1, o2, l0, l1, l2, yssd, glu_b, nw, pr_glu, pr_a, pr_b, pr_c, x, weights):
    glu_w, pa, pb, pc, wo = weights
    gates = jax.nn.sigmoid(pt[:, :3072])
    za, zb, zc = pt[:, 3072:3584], pt[:, 3584:3840], pt[:, 3840:4608]
    g = jax.nn.gelu(ys5)
    ya = g * jax.nn.sigmoid(_cdot(g, glu_w, NN) + glu_b + pr_glu) * jax.nn.silu(za)
    m = jnp.maximum(jnp.maximum(l0, l1), l2)
    e0, e1, e2 = jnp.exp(l0 - m), jnp.exp(l1 - m), jnp.exp(l2 - m)
    yb = (e0 * o0 + e1 * o1 + e2 * o2) / (e0 + e1 + e2) * jax.nn.silu(zb)
    yc = _rms(yssd * jax.nn.silu(zc), nw)
    merged = (gates[:, :1024] * (_cdot(ya, pa, NN) + pr_a) + gates[:, 1024:2048] * (_cdot(yb, pb, NN) + pr_b)
              + gates[:, 2048:] * (_cdot(yc, pc, NN) + pr_c))
    out = x + _cdot(merged, wo, NN)
    return out, (g, ya, yb, yc, merged)


def _tail_specs(tm):
    row = lambda n: pl.BlockSpec((tm, n), lambda i: (i, 0))
    full = lambda a, b: pl.BlockSpec((a, b), lambda i: (0, 0))
    acts = [row(512), row(4608)] + [row(256)] * 6 + [row(768), row(D_MODEL)]
    consts = [full(1, 512), full(1, 768), full(512, 512), full(512, D_MODEL), full(256, D_MODEL),
              full(768, D_MODEL), full(D_MODEL, D_MODEL)]
    return row, full, acts, consts


def tail_fwd(ys5, pt, os_, ls_, yssd, x, glu_b, nw, weights, name):
    S = x.shape[0]
    tm = TAIL_ROWS
    row, full, acts, consts = _tail_specs(tm)

    def body(ys5_ref, pt_ref, o0, o1, o2, l0, l1, l2, yssd_ref, x_ref, gb_ref, nw_ref, gw, pa, pb, pc, wo, out_ref):
        z = lambda n: jnp.zeros((tm, n), f32)
        out, _ = _tail_fn(ys5_ref[...], pt_ref[...], o0[...], o1[...], o2[...], l0[...], l1[...], l2[...],
                          yssd_ref[...], gb_ref[...], nw_ref[...], z(512), z(D_MODEL), z(D_MODEL), z(D_MODEL),
                          x_ref[...], (gw[...], pa[...], pb[...], pc[...], wo[...]))
        out_ref[...] = out

    return _call(body, name, (S // tm,), acts + consts, row(D_MODEL), jax.ShapeDtypeStruct((S, D_MODEL), f32),
                 sem=("parallel",))(ys5, pt, *os_, *ls_, yssd, x, glu_b, nw, *weights)


def tail_bwd(ys5, pt, os_, ls_, yssd, dout, glu_b, nw, weights, name):
    S = dout.shape[0]
    tm = TAIL_ROWS
    row, full, acts, consts = _tail_specs(tm)

    def body(ys5_ref, pt_ref, o0, o1, o2, l0, l1, l2, yssd_ref, dout_ref, gb_ref, nw_ref, gw, pa, pb, pc, wo,
             dys5_ref, dpt_ref, do0, do1, do2, dl0, dl1, dl2, dyssd_ref, dgb_ref, dnw_ref,
             g_ref, ya_ref, yb_ref, yc_ref, mg_ref, dglu_ref, dpa_ref, dpb_ref, dpc_ref):
        z = lambda n: jnp.zeros((tm, n), f32)
        w = (gw[...], pa[...], pb[...], pc[...], wo[...])
        fn = lambda *a: _tail_fn(*a, z(D_MODEL), w)
        _, vjp, aux = jax.vjp(fn, ys5_ref[...], pt_ref[...], o0[...], o1[...], o2[...], l0[...], l1[...], l2[...],
                              yssd_ref[...], gb_ref[...], nw_ref[...], z(512), z(D_MODEL), z(D_MODEL), z(D_MODEL),
                              has_aux=True)
        (dys5, dpt, d0, d1, d2, e0, e1, e2, dyssd, dgb, dnw, dglu, dpa, dpb, dpc) = vjp(dout_ref[...])
        dys5_ref[...] = dys5
        dpt_ref[...] = dpt
        for ref, val in ((do0, d0), (do1, d1), (do2, d2), (dl0, e0), (dl1, e1), (dl2, e2)):
            ref[...] = val
        dyssd_ref[...] = dyssd
        g, ya, yb, yc, merged = aux
        for ref, val in ((g_ref, g), (ya_ref, ya), (yb_ref, yb), (yc_ref, yc), (mg_ref, merged),
                         (dglu_ref, dglu), (dpa_ref, dpa), (dpb_ref, dpb), (dpc_ref, dpc)):
            ref[...] = val.astype(bf16)

        @pl.when(pl.program_id(0) == 0)
        def _():
            dgb_ref[...] = dgb
            dnw_ref[...] = dnw

        @pl.when(pl.program_id(0) > 0)
        def _():
            dgb_ref[...] += dgb
            dnw_ref[...] += dnw

    sd = lambda n, dt=f32: jax.ShapeDtypeStruct((S, n), dt)
    out_specs = ([row(512), row(4608)] + [row(256)] * 6 + [row(768), full(1, 512), full(1, 768)]
                 + [row(512), row(512), row(256), row(768), row(D_MODEL), row(512)] + [row(D_MODEL)] * 3)
    out_shape = ([sd(512), sd(4608)] + [sd(256)] * 6 + [sd(768), jax.ShapeDtypeStruct((1, 512), f32),
                                                          jax.ShapeDtypeStruct((1, 768), f32)]
                 + [sd(512, bf16), sd(512, bf16), sd(256, bf16), sd(768, bf16), sd(D_MODEL, bf16), sd(512, bf16)]
                 + [sd(D_MODEL, bf16)] * 3)
    return _call(body, name, (S // tm,), acts + consts, out_specs, out_shape,
                 sem=("arbitrary",))(ys5, pt, *os_, *ls_, yssd, dout, glu_b, nw, *weights)


def _in_proj_segments(w):
    c = lambda a, b: w[:, a:b]
    atts = []
    for g in range(3):
        parts = []
        for hh in range(2):
            o = 64 * (4 * g + 2 * hh)
            parts += [c(_C_Q + o, _C_Q + o + 128), c(_C_K + o, _C_K + o + 128), c(_C_V + o, _C_V + o + 128)]
        atts.append(jnp.concatenate(parts, axis=1))
    ssd = jnp.concatenate([c(_C_XBC, _C_ZC), jnp.zeros((D_MODEL, 1536 - (_C_ZC - _C_XBC)), w.dtype)], axis=1)
    tail = jnp.concatenate([c(_C_GATE, _C_END), c(_C_ZA, _C_Q), c(_C_ZB, _C_XBC), c(_C_ZC, _C_GATE)], axis=1)
    return [c(_C_UA, _C_ZA)] + atts + [ssd, tail]


def _in_proj_grad(ds5, datts, dssd, dtail):
    pick = lambda off: jnp.concatenate([datts[g][:, 384 * hh + off:384 * hh + off + 128]
                                        for g in range(3) for hh in range(2)], axis=1)
    return jnp.concatenate([ds5, dtail[:, 3072:3584], pick(0), pick(128), pick(256), dtail[:, 3584:3840],
                            dssd[:, :_C_ZC - _C_XBC], dtail[:, 3840:4608], dtail[:, :3072]], axis=1)


def _prep_layer(p):
    q = {}
    q["segs"] = [s.astype(bf16) for s in _in_proj_segments(p["w_in"])]
    disc = _s5_discretize(p["s5_a_re"], p["s5_a_im"], p["s5_log_step"], p["s5_b_re"], p["s5_b_im"],
                          p["s5_c_re"], p["s5_c_im"])
    q["s5"] = disc
    q["pw"] = _lam_powers(disc[0], disc[1])
    q["s5_d"] = p["s5_d"].reshape(1, 512)
    q["qw"] = jnp.tile(p["q_norm_w"], 2).reshape(1, LANES)
    q["kw"] = jnp.tile(p["k_norm_w"], 2).reshape(1, LANES)
    q["conv_w"] = p["conv_w"]
    q["conv_b"] = p["conv_b"].reshape(1, SSD_XBC)
    pad = lambda v: jnp.pad(v, (0, LANES - v.shape[0])).reshape(1, LANES)
    q["dt_bias"], q["a_log"] = pad(p["dt_bias"]), pad(p["ssd_a_log"])
    q["d_full"] = jnp.repeat(p["ssd_d"], 64).reshape(1, SSD_WIDTH)
    q["glu_b"] = p["s5_glu_b"].reshape(1, 512)
    q["nw"] = p["ssd_norm_w"].reshape(1, SSD_WIDTH)
    q["norm_w"] = p["norm_w"].reshape(1, D_MODEL)
    q["tailw"] = tuple(p[n].astype(bf16) for n in ("s5_glu_w", "proj_a", "proj_b", "proj_c", "w_out"))
    return q


_DILATIONS = (1, 4, 16)


def layer_fwd(x, q, tag):
    h = rms_fwd(x, q["norm_w"], f"rms_fwd{tag}")
    p_s5, p_a0, p_a1, p_a2, p_ssd, p_tail = [mm_nn(h, w, f"inproj{k}{tag}") for k, w in enumerate(q["segs"])]
    _, _, w_re, w_im, c_re, c_im = q["s5"]
    ys5, h_re, h_im = s5_fwd(p_s5, *q["pw"], w_re, w_im, c_re, c_im, q["s5_d"], f"s5_fwd{tag}")
    p_atts = (p_a0, p_a1, p_a2)
    os_, ls_ = [], []
    for g, d in enumerate(_DILATIONS):
        o, l = att_fwd(p_atts[g], q["qw"], q["kw"], d, f"att_fwd{g}{tag}")
        os_.append(o)
        ls_.append(l)
    xact = conv_fwd(p_ssd, q["conv_w"], q["conv_b"], f"conv_fwd{tag}")
    yssd, states = ssd_fwd(xact, p_ssd, q["dt_bias"], q["a_log"], q["d_full"], f"ssd_fwd{tag}")
    out = tail_fwd(ys5, p_tail, os_, ls_, yssd, x, q["glu_b"], q["nw"], q["tailw"], f"tail_fwd{tag}")
    saved = dict(x=x, h=h, p_s5=p_s5, p_atts=p_atts, p_ssd=p_ssd, p_tail=p_tail, ys5=ys5, h_re=h_re, h_im=h_im,
                 os=os_, ls=ls_, xact=xact, yssd=yssd, states=states)
    return out, saved


def layer_bwd(dout, sv, q, p, tag):
    S = dout.shape[0]
    (dys5, dp_tail, do0, do1, do2, dl0, dl1, dl2, dyssd, dglu_b, dnw, g_b, ya_b, yb_b, yc_b, mg_b, dglu_b16,
     dpa_b, dpb_b, dpc_b) = tail_bwd(sv["ys5"], sv["p_tail"], sv["os"], sv["ls"], sv["yssd"], dout, q["glu_b"],
                                     q["nw"], q["tailw"], f"tail_bwd{tag}")
    grads = {}
    grads["s5_glu_w"] = mm_tn(g_b, dglu_b16, f"dglu_w{tag}")
    grads["proj_a"] = mm_tn(ya_b, dpa_b, f"dproj_a{tag}")
    grads["proj_b"] = mm_tn(yb_b, dpb_b, f"dproj_b{tag}")
    grads["proj_c"] = mm_tn(yc_b, dpc_b, f"dproj_c{tag}")
    grads["w_out"] = mm_tn(mg_b, dout, f"dw_out{tag}")
    grads["s5_glu_b"] = dglu_b.reshape(512)
    grads["ssd_norm_w"] = dnw.reshape(SSD_WIDTH)

    dxact, ddt, ddt_bias, da_log, dd_full = ssd_bwd(sv["xact"], sv["p_ssd"], sv["states"], dyssd, q["dt_bias"],
                                                    q["a_log"], q["d_full"], f"ssd_bwd{tag}")
    dp_ssd, dconv_w, dconv_b = conv_bwd(sv["p_ssd"], dxact, ddt, q["conv_w"], q["conv_b"], f"conv_bwd{tag}")
    grads["dt_bias"] = ddt_bias[0, :12]
    grads["ssd_a_log"] = da_log[0, :12]
    grads["ssd_d"] = dd_full.reshape(12, 64).sum(axis=1)
    grads["conv_w"] = dconv_w
    grads["conv_b"] = dconv_b.reshape(SSD_XBC)

    dp_atts, dqw, dkw = [], 0.0, 0.0
    for g, d in enumerate(_DILATIONS):
        dp, a, b = att_bwd(sv["p_atts"][g], (do0, do1, do2)[g], (dl0, dl1, dl2)[g], q["qw"], q["kw"], d,
                           f"att_bwd{g}{tag}")
        dp_atts.append(dp)
        dqw, dkw = dqw + a, dkw + b
    grads["q_norm_w"] = dqw.reshape(2, 64).sum(axis=0)
    grads["k_norm_w"] = dkw.reshape(2, 64).sum(axis=0)

    _, _, w_re, w_im, c_re, c_im = q["s5"]
    dp_s5, dwre, dwim, dcre, dcim, dlam_re, dlam_im, dd = s5_bwd(
        dys5, sv["p_s5"], sv["h_re"], sv["h_im"], *q["pw"], w_re, w_im, c_re, c_im, q["s5_d"], f"s5_bwd{tag}")
    s5_names = ("s5_a_re", "s5_a_im", "s5_log_step", "s5_b_re", "s5_b_im", "s5_c_re", "s5_c_im")
    _, disc_vjp = jax.vjp(_s5_discretize, *[p[n] for n in s5_names])
    for n, gr in zip(s5_names, disc_vjp((dlam_re, dlam_im, dwre, dwim, dcre, dcim))):
        grads[n] = gr
    grads["s5_d"] = dd.reshape(512)

    dsegs = [dp_s5] + dp_atts + [dp_ssd, dp_tail]
    dws = [mm_tn(sv["h"], ds, f"dw_in{k}{tag}") for k, ds in enumerate(dsegs)]
    grads["w_in"] = _in_proj_grad(dws[0], dws[1:4], dws[4], dws[5])
    dh = None
    for k, (ds, w) in enumerate(zip(dsegs, q["segs"])):
        dh = mm_nt(ds, w, f"dh{k}{tag}", acc=dh)
    dx, dnorm_w = rms_bwd(sv["x"], q["norm_w"], dh, dout, f"rms_bwd{tag}")
    grads["norm_w"] = dnorm_w.reshape(D_MODEL)
    return dx, grads


_ANY = pl.BlockSpec(memory_space=pl.ANY)


def _chip_exchange(x, name, broadcast):
    R = x.shape[-2]

    def body(x_ref, o_ref, send_sems, recv_sems, local_sem):
        mx, my, mc = lax.axis_index("x"), lax.axis_index("y"), lax.axis_index("c")
        me = 2 * mx + my
        copies = []
        for t, (px, py) in enumerate(((1 - mx, my), (mx, 1 - my), (1 - mx, 1 - my))):
            src = x_ref if broadcast else x_ref.at[2 * px + py]
            cp = pltpu.make_async_remote_copy(src_ref=src, dst_ref=o_ref.at[me], send_sem=send_sems.at[t],
                                              recv_sem=recv_sems.at[t], device_id=(px, py, mc),
                                              device_id_type=pl.DeviceIdType.MESH)
            cp.start()
            copies.append(cp)
        mine = pltpu.make_async_copy(x_ref if broadcast else x_ref.at[me], o_ref.at[me], local_sem)
        mine.start()
        for cp in copies:
            cp.wait()
        mine.wait()

    return pl.pallas_call(
        body, name=name, in_specs=[_ANY], out_specs=_ANY, out_shape=jax.ShapeDtypeStruct((4, R, LANES), f32),
        scratch_shapes=[pltpu.SemaphoreType.DMA((3,)), pltpu.SemaphoreType.DMA((3,)), pltpu.SemaphoreType.DMA],
    )(x)


def _sibling_exchange(x, name):
    def body(x_ref, o_ref, send_sem, recv_sem):
        peer = (lax.axis_index("x"), lax.axis_index("y"), 1 - lax.axis_index("c"))
        cp = pltpu.make_async_remote_copy(src_ref=x_ref, dst_ref=o_ref, send_sem=send_sem, recv_sem=recv_sem,
                                          device_id=peer, device_id_type=pl.DeviceIdType.MESH)
        cp.start()
        cp.wait()

    return pl.pallas_call(
        body, name=name, in_specs=[_ANY], out_specs=_ANY, out_shape=jax.ShapeDtypeStruct(x.shape, f32),
        scratch_shapes=[pltpu.SemaphoreType.DMA, pltpu.SemaphoreType.DMA],
    )(x)


def _sum4(x, name):
    R = x.shape[1]
    tr = _tile(R, (512, 256, 128))

    def body(x_ref, o_ref):
        o_ref[...] = ((x_ref[0] + x_ref[1]) + x_ref[2]) + x_ref[3]

    return _call(body, name, (R // tr,), [pl.BlockSpec((4, tr, LANES), lambda i: (0, i, 0))],
                 pl.BlockSpec((tr, LANES), lambda i: (i, 0)), jax.ShapeDtypeStruct((R, LANES), f32),
                 sem=("parallel",))(x)


def _adamw(ga, gb, w, m, v, name):
    R = w.shape[0]
    tr = _tile(R, (512, 256, 128))
    c1 = 1.0 - ADAM_B1 ** ADAM_STEP
    c2 = 1.0 - ADAM_B2 ** ADAM_STEP

    def body(ga_ref, gb_ref, w_ref, m_ref, v_ref, g_ref, d_ref, nm_ref, nv_ref):
        g = ga_ref[...] + gb_ref[...]
        m = ADAM_B1 * m_ref[...] + (1.0 - ADAM_B1) * g
        v = ADAM_B2 * v_ref[...] + (1.0 - ADAM_B2) * (g * g)
        g_ref[...] = g
        nm_ref[...] = m
        nv_ref[...] = v
        d_ref[...] = -ADAM_LR * ((m / c1) / (jnp.sqrt(v / c2) + ADAM_EPS) + ADAM_WD * w_ref[...])

    spec = pl.BlockSpec((tr, LANES), lambda i: (i, 0))
    sd = jax.ShapeDtypeStruct((R, LANES), f32)
    return _call(body, name, (R // tr,), [spec] * 5, [spec] * 4, [sd] * 4, sem=("parallel",))(ga, gb, w, m, v)


def _pack(arrays):
    flat = jnp.concatenate([a.reshape(-1) for a in arrays])
    unit = PACK_ROWS * LANES
    n = -(-flat.shape[0] // unit) * unit
    return jnp.pad(flat, (0, n - flat.shape[0])).reshape(n // LANES, LANES)


def _unpack(buf, shapes):
    flat = buf.reshape(-1)
    out, off = [], 0
    for s in shapes:
        n = 1
        for dim in s:
            n *= dim
        out.append(flat[off:off + n].reshape(s))
        off += n
    return out


def _to_shards(full, axis):
    s = full.shape
    t = full.reshape(s[:axis] + (4, s[axis] // 4) + s[axis + 1:])
    return jnp.moveaxis(t, axis, 0)


def _from_shards(sh, axis):
    t = jnp.moveaxis(sh, 0, axis)
    s = t.shape
    return t.reshape(s[:axis] + (s[axis] * s[axis + 1],) + s[axis + 2:])


def kernel(x, norm_w, w_in, s5_a_re, s5_a_im, s5_log_step, s5_b_re, s5_b_im, s5_c_re, s5_c_im, s5_d, s5_glu_w, s5_glu_b, q_norm_w, k_norm_w, conv_w, conv_b, dt_bias, ssd_a_log, ssd_d, ssd_norm_w, proj_a, proj_b, proj_c, w_out, loss_target, m_norm_w, m_w_in, m_s5_a_re, m_s5_a_im, m_s5_log_step, m_s5_b_re, m_s5_b_im, m_s5_c_re, m_s5_c_im, m_s5_d, m_s5_glu_w, m_s5_glu_b, m_q_norm_w, m_k_norm_w, m_conv_w, m_conv_b, m_dt_bias, m_ssd_a_log, m_ssd_d, m_ssd_norm_w, m_proj_a, m_proj_b, m_proj_c, m_w_out, v_norm_w, v_w_in, v_s5_a_re, v_s5_a_im, v_s5_log_step, v_s5_b_re, v_s5_b_im, v_s5_c_re, v_s5_c_im, v_s5_d, v_s5_glu_w, v_s5_glu_b, v_q_norm_w, v_k_norm_w, v_conv_w, v_conv_b, v_dt_bias, v_ssd_a_log, v_ssd_d, v_ssd_norm_w, v_proj_a, v_proj_b, v_proj_c, v_w_out):
    given = dict(locals())
    W = {n: given[n] for n in _WEIGHTS}
    M = {n: given["m_" + n] for n in _WEIGHTS}
    V = {n: given["v_" + n] for n in _WEIGHTS}
    n_layers = norm_w.shape[0]

    local_shapes = [W[n].shape for n, _ in _SHARDED]
    gathered = _chip_exchange(_pack([W[n] for n, _ in _SHARDED]), "gather_weights", broadcast=True)
    full = dict(W)
    pieces = [_unpack(gathered[j], local_shapes) for j in range(4)]
    for k, (n, axis) in enumerate(_SHARDED):
        full[n] = _from_shards(jnp.stack([pieces[j][k] for j in range(4)]), axis)

    xs = x[0]
    qs, saves = [], []
    act = xs
    for l in range(n_layers):
        p = {n: full[n][l] for n in _WEIGHTS}
        q = _prep_layer(p)
        act, sv = layer_fwd(act, q, f"_l{l}")
        qs.append((q, p))
        saves.append(sv)
    dact, lsum = loss_and_grad(act, loss_target[0], "loss")
    loss = lax.psum(lsum[0, 0], ("x", "y", "c"))
    layer_grads = [None] * n_layers
    for l in reversed(range(n_layers)):
        q, p = qs[l]
        dact, layer_grads[l] = layer_bwd(dact, saves[l], q, p, f"_l{l}")
    grad_x = dact[None]
    G = {n: jnp.stack([layer_grads[l][n] for l in range(n_layers)]) for n in _WEIGHTS}

    repl_shapes = [W[n].shape for n in _REPL]
    small = _pack([G[n] for n in _REPL])
    quarter = small.shape[0] // 4
    big = [_to_shards(G[n], axis).reshape(4, -1) for n, axis in _SHARDED]
    big = jnp.concatenate(big, axis=1)
    unit = PACK_ROWS * LANES
    nbig = -(-big.shape[1] // unit) * unit
    big = jnp.pad(big, ((0, 0), (0, nbig - big.shape[1]))).reshape(4, nbig // LANES, LANES)
    gpack = jnp.concatenate([big, small.reshape(4, quarter, LANES)], axis=1)
    mine = _sum4(_chip_exchange(gpack, "scatter_grads", broadcast=False), "sum_chips")
    other = _sibling_exchange(mine, "swap_cores")
    rbig = nbig // LANES

    wp, mp, vp = (_pack([T[n] for n, _ in _SHARDED]) for T in (W, M, V))
    outs_big = _adamw(mine[:rbig], other[:rbig], wp, mp, vp, "adamw_sharded")
    big_out = [_unpack(o, local_shapes) for o in outs_big]

    zero = jnp.zeros((quarter, LANES), f32)
    gq = _adamw(mine[rbig:], other[rbig:], zero, zero, zero, "sum_cores_small")[0]
    gsmall = _chip_exchange(gq, "gather_small", broadcast=True).reshape(4 * quarter, LANES)
    ws, ms, vs = (_pack([T[n] for n in _REPL]) for T in (W, M, V))
    outs_small = _adamw(gsmall, jnp.zeros_like(gsmall), ws, ms, vs, "adamw_replicated")
    small_out = [_unpack(o, repl_shapes) for o in outs_small]

    res = [dict(), dict(), dict(), dict()]
    for kind in range(4):
        for k, (n, _) in enumerate(_SHARDED):
            res[kind][n] = big_out[kind][k]
        for k, n in enumerate(_REPL):
            res[kind][n] = small_out[kind][k]
    return (loss, grad_x, *[res[0][n] for n in _WEIGHTS], *[res[1][n] for n in _WEIGHTS],
            *[res[2][n] for n in _WEIGHTS], *[res[3][n] for n in _WEIGHTS])
```

```python
import functools

import jax
import jax.numpy as jnp
from jax import lax
from jax.experimental import pallas as pl
from jax.experimental.pallas import tpu as pltpu

f32 = jnp.float32
bf16 = jnp.bfloat16

D_MODEL = 1024
RMS_EPS = 1e-6
V7X_VMEM_LIMIT = 60 * 1024 * 1024
LANES = 128
NN, NT, TN = ((1,), (0,)), ((1,), (1,)), ((0,), (0,))

S5_STATES = 2048
S5_ROWS = 256
ATT_SEG = 2048
ATT_BLOCK = 128
SSD_CHUNK = 128
SSD_WIDTH = 768
SSD_XBC = 1280
CONV_ROWS = 512
TAIL_ROWS = 128

ADAM_LR, ADAM_B1, ADAM_B2, ADAM_EPS, ADAM_WD, ADAM_STEP = 0.001, 0.9, 0.999, 1e-08, 0.01, 10

_C_UA, _C_ZA, _C_Q, _C_K, _C_V, _C_ZB, _C_XBC, _C_DT, _C_ZC, _C_GATE, _C_END = (
    0, 512, 1024, 1792, 2560, 3328, 3584, 4864, 4876, 5644, 8716)

_SHARDED = (("s5_glu_w", 1), ("conv_w", 2), ("proj_a", 2), ("proj_b", 2), ("proj_c", 2), ("w_out", 1))
W_IN_SHARD = 2179
_REPL = ("norm_w", "s5_a_re", "s5_a_im", "s5_log_step", "s5_b_re", "s5_b_im", "s5_c_re", "s5_c_im", "s5_d",
         "s5_glu_b", "q_norm_w", "k_norm_w", "conv_b", "dt_bias", "ssd_a_log", "ssd_d", "ssd_norm_w")
_WEIGHTS = ("norm_w", "w_in", "s5_a_re", "s5_a_im", "s5_log_step", "s5_b_re", "s5_b_im", "s5_c_re", "s5_c_im",
            "s5_d", "s5_glu_w", "s5_glu_b", "q_norm_w", "k_norm_w", "conv_w", "conv_b", "dt_bias", "ssd_a_log",
            "ssd_d", "ssd_norm_w", "proj_a", "proj_b", "proj_c", "w_out")
PACK_ROWS = 512


def _dot(a, b, dims):
    return lax.dot_general(a.astype(bf16), b.astype(bf16), (dims, ((), ())), preferred_element_type=f32)


def _call(body, name, grid, in_specs, out_specs, out_shape, scratch=(), sem=None):
    return pl.pallas_call(
        body, name=name, grid=grid, in_specs=in_specs, out_specs=out_specs, out_shape=out_shape,
        scratch_shapes=list(scratch),
        compiler_params=pltpu.CompilerParams(dimension_semantics=sem, vmem_limit_bytes=V7X_VMEM_LIMIT))


def _tile(n, options=(1024, 768, 512, 384, 256, 128)):
    return next(t for t in options if n % t == 0)


@functools.partial(jax.custom_vjp, nondiff_argnums=(2,))
def _bdot(a, b, dims):
    return _dot(a, b, dims)


def _bdot_fwd(a, b, dims):
    return _dot(a, b, dims), (a, b)


def _bdot_bwd(dims, res, g):
    a, b = res
    if dims == NN:
        da, db = _dot(g, b, NT), _dot(a, g, TN)
    elif dims == NT:
        da, db = _dot(g, b, NN), _dot(g, a, TN)
    else:
        da, db = _dot(b, g, NT), _dot(a, g, NN)
    return da.astype(a.dtype), db.astype(b.dtype)


_bdot.defvjp(_bdot_fwd, _bdot_bwd)


@functools.partial(jax.custom_vjp, nondiff_argnums=(2,))
def _cdot(a, w, dims):
    return _dot(a, w, dims)


def _cdot_fwd(a, w, dims):
    return _dot(a, w, dims), w


def _cdot_bwd(dims, w, g):
    da = _dot(g, w, NT) if dims == NN else _dot(g, w, NN)
    return da, jnp.zeros_like(w)


_cdot.defvjp(_cdot_fwd, _cdot_bwd)


def _split3(x):
    hi = x.astype(bf16)
    r = x - hi.astype(f32)
    mid = r.astype(bf16)
    lo = (r - mid.astype(f32)).astype(bf16)
    return hi, mid, lo


@jax.custom_vjp
def _xdot_r(x, m):
    return sum(_dot(p, m, NN) for p in _split3(x))


def _xdot_r_fwd(x, m):
    return _xdot_r(x, m), m


def _xdot_r_bwd(m, g):
    return sum(_dot(p, m, NT) for p in _split3(g)), jnp.zeros_like(m)


_xdot_r.defvjp(_xdot_r_fwd, _xdot_r_bwd)


@jax.custom_vjp
def _xdot_l(m, x):
    return sum(_dot(m, p, NN) for p in _split3(x))


def _xdot_l_fwd(m, x):
    return _xdot_l(m, x), m


def _xdot_l_bwd(m, g):
    return jnp.zeros_like(m), sum(_dot(m, p, TN) for p in _split3(g))


_xdot_l.defvjp(_xdot_l_fwd, _xdot_l_bwd)


@jax.custom_vjp
def _softplus(x):
    e = jnp.exp(-jnp.abs(x))
    u = 1.0 + e
    log1p = jnp.where(u == 1.0, e, jnp.log(u) * (e / jnp.where(u == 1.0, 1.0, u - 1.0)))
    return jnp.maximum(x, 0.0) + log1p


def _softplus_fwd(x):
    return _softplus(x), x


def _softplus_bwd(x, g):
    return (g * jax.nn.sigmoid(x),)


_softplus.defvjp(_softplus_fwd, _softplus_bwd)


def _rms(x, w):
    return x * lax.rsqrt(jnp.mean(x * x, axis=-1, keepdims=True) + RMS_EPS) * w


def mm_nn(a, b, name, tm=1024):
    M, K = a.shape
    N = b.shape[1]
    tn = _tile(N)

    def body(a_ref, b_ref, o_ref):
        o_ref[...] = _dot(a_ref[...], b_ref[...], NN)

    return _call(body, name, (M // tm, N // tn),
                 [pl.BlockSpec((tm, K), lambda i, j: (i, 0)), pl.BlockSpec((K, tn), lambda i, j: (0, j))],
                 pl.BlockSpec((tm, tn), lambda i, j: (i, j)), jax.ShapeDtypeStruct((M, N), f32),
                 sem=("parallel", "parallel"))(a, b)


def mm_nt(a, b, name, acc=None, tm=1024):
    M, K = a.shape
    N = b.shape[0]
    tk = _tile(K)
    has_acc = acc is not None

    def body(*refs):
        a_ref, b_ref = refs[0], refs[1]
        o_ref = refs[-1]
        k = pl.program_id(1)
        p = _dot(a_ref[...], b_ref[...], NT)

        @pl.when(k == 0)
        def _():
            o_ref[...] = p + refs[2][...] if has_acc else p

        @pl.when(k > 0)
        def _():
            o_ref[...] += p

    specs = [pl.BlockSpec((tm, tk), lambda i, k: (i, k)), pl.BlockSpec((N, tk), lambda i, k: (0, k))]
    args = [a, b]
    if has_acc:
        specs.append(pl.BlockSpec((tm, N), lambda i, k: (i, 0)))
        args.append(acc)
    return _call(body, name, (M // tm, K // tk), specs, pl.BlockSpec((tm, N), lambda i, k: (i, 0)),
                 jax.ShapeDtypeStruct((M, N), f32), sem=("parallel", "arbitrary"))(*args)


def mm_tn(a, b, name, tk=1024):
    K, M = a.shape
    N = b.shape[1]
    tn = _tile(N)

    def body(a_ref, b_ref, o_ref):
        k = pl.program_id(1)
        p = _dot(a_ref[...], b_ref[...], TN)

        @pl.when(k == 0)
        def _():
            o_ref[...] = p

        @pl.when(k > 0)
        def _():
            o_ref[...] += p

    return _call(body, name, (N // tn, K // tk),
                 [pl.BlockSpec((tk, M), lambda j, k: (k, 0)), pl.BlockSpec((tk, tn), lambda j, k: (k, j))],
                 pl.BlockSpec((M, tn), lambda j, k: (0, j)), jax.ShapeDtypeStruct((M, N), f32),
                 sem=("parallel", "arbitrary"))(a, b)


def rms_fwd(x, w, name, tm=512):
    S = x.shape[0]

    def body(x_ref, w_ref, o_ref):
        o_ref[...] = _rms(x_ref[...], w_ref[...]).astype(bf16)

    return _call(body, name, (S // tm,),
                 [pl.BlockSpec((tm, D_MODEL), lambda i: (i, 0)), pl.BlockSpec((1, D_MODEL), lambda i: (0, 0))],
                 pl.BlockSpec((tm, D_MODEL), lambda i: (i, 0)), jax.ShapeDtypeStruct((S, D_MODEL), bf16),
                 sem=("parallel",))(x, w)


def rms_bwd(x, w, dh, dres, name, tm=512):
    S = x.shape[0]

    def body(x_ref, w_ref, dh_ref, dr_ref, dx_ref, dw_ref):
        _, vjp = jax.vjp(_rms, x_ref[...], w_ref[...])
        dx, dw = vjp(dh_ref[...])
        dx_ref[...] = dx + dr_ref[...]

        @pl.when(pl.program_id(0) == 0)
        def _():
            dw_ref[...] = dw

        @pl.when(pl.program_id(0) > 0)
        def _():
            dw_ref[...] += dw

    row = pl.BlockSpec((tm, D_MODEL), lambda i: (i, 0))
    vec = pl.BlockSpec((1, D_MODEL), lambda i: (0, 0))
    return _call(body, name, (S // tm,), [row, vec, row, row], [row, vec],
                 [jax.ShapeDtypeStruct((S, D_MODEL), f32), jax.ShapeDtypeStruct((1, D_MODEL), f32)],
                 sem=("arbitrary",))(x, w, dh, dres)


def loss_and_grad(y, target, name, tm=512):
    S = y.shape[0]

    def body(y_ref, t_ref, dy_ref, l_ref):
        diff = y_ref[...] - t_ref[...]
        dy_ref[...] = diff * (1.0 / D_MODEL)
        part = jnp.full((8, LANES), 0.5 / D_MODEL * jnp.sum(diff * diff), f32)

        @pl.when(pl.program_id(0) == 0)
        def _():
            l_ref[...] = part

        @pl.when(pl.program_id(0) > 0)
        def _():
            l_ref[...] += part

    row = pl.BlockSpec((tm, D_MODEL), lambda i: (i, 0))
    return _call(body, name, (S // tm,), [row, row], [row, pl.BlockSpec((8, LANES), lambda i: (0, 0))],
                 [jax.ShapeDtypeStruct((S, D_MODEL), f32), jax.ShapeDtypeStruct((8, LANES), f32)],
                 sem=("arbitrary",))(y, target)


def _s5_discretize(a_re, a_im, log_step, b_re, b_im, c_re, c_im):
    step = jnp.exp(log_step)[:, None]
    mag = jnp.exp(a_re * step)
    ang = a_im * step
    lam_re, lam_im = mag * jnp.cos(ang), mag * jnp.sin(ang)
    num_re, num_im = lam_re - 1.0, lam_im
    den = a_re * a_re + a_im * a_im
    f_re = (num_re * a_re + num_im * a_im) / den
    f_im = (num_im * a_re - num_re * a_im) / den
    bb_re = f_re[..., None] * b_re - f_im[..., None] * b_im
    bb_im = f_re[..., None] * b_im + f_im[..., None] * b_re
    eye = jnp.eye(8, dtype=f32)

    def block_in(bb):
        t = bb.transpose(0, 2, 1).reshape(4, 8, 16, 1, 64)
        return (t * eye[None, :, None, :, None]).reshape(4, 128, 512)

    def block_out(c):
        t = c.transpose(0, 2, 1).reshape(4, 8, 64, 1, 16)
        return (t * eye[None, :, None, :, None]).reshape(4, 512, 128)

    return (lam_re.reshape(1, S5_STATES), lam_im.reshape(1, S5_STATES), block_in(bb_re), block_in(bb_im),
            block_out(c_re), block_out(c_im))


def _lam_powers(lam_re, lam_im):
    rows_re, rows_im = [lam_re], [lam_im]
    for _ in range(7):
        pr, pi = rows_re[-1], rows_im[-1]
        rows_re.append(pr * lam_re - pi * lam_im)
        rows_im.append(pr * lam_im + pi * lam_re)
    return jnp.concatenate(rows_re, 0), jnp.concatenate(rows_im, 0)


def s5_fwd(u, pw_re, pw_im, w_re, w_im, c_re, c_im, dvec, name):
    S = u.shape[0]
    R, NS = S5_ROWS, S5_STATES
    nb = R // 8

    def body(u_ref, pwr_ref, pwi_ref, wre_ref, wim_ref, cre_ref, cim_ref, d_ref, y_ref, hr_ref, hi_ref,
             car_re, car_im, cin_re, cin_im):
        @pl.when(pl.program_id(0) == 0)
        def _():
            car_re[...] = jnp.zeros_like(car_re)
            car_im[...] = jnp.zeros_like(car_im)

        u = u_ref[...]
        for j in range(4):
            uj = u[:, 128 * j:128 * (j + 1)]
            hr_ref[:, :, 512 * j:512 * (j + 1)] = _dot(uj, wre_ref[j], NN).reshape(nb, 8, 512)
            hi_ref[:, :, 512 * j:512 * (j + 1)] = _dot(uj, wim_ref[j], NN).reshape(nb, 8, 512)
        lr, li = pwr_ref[0:1, :], pwi_ref[0:1, :]
        for r in range(1, 8):
            pr, pi = hr_ref[:, r - 1, :], hi_ref[:, r - 1, :]
            hr_ref[:, r, :] = lr * pr - li * pi + hr_ref[:, r, :]
            hi_ref[:, r, :] = lr * pi + li * pr + hi_ref[:, r, :]
        l8r, l8i = pwr_ref[7:8, :], pwi_ref[7:8, :]

        def across(c, carry):
            gr, gi = carry
            cin_re[pl.ds(c, 1), :] = gr
            cin_im[pl.ds(c, 1), :] = gi
            er, ei = hr_ref[c, 7:8, :], hi_ref[c, 7:8, :]
            return l8r * gr - l8i * gi + er, l8r * gi + l8i * gr + ei

        gr, gi = lax.fori_loop(0, nb, across, (car_re[...], car_im[...]))
        car_re[...] = gr
        car_im[...] = gi
        cr, ci = cin_re[...], cin_im[...]
        for r in range(8):
            pr, pi = pwr_ref[r:r + 1, :], pwi_ref[r:r + 1, :]
            hr_ref[:, r, :] = hr_ref[:, r, :] + pr * cr - pi * ci
            hi_ref[:, r, :] = hi_ref[:, r, :] + pr * ci + pi * cr
        for j in range(4):
            sl = slice(512 * j, 512 * (j + 1))
            hrj = hr_ref[:, :, sl].reshape(R, 512)
            hij = hi_ref[:, :, sl].reshape(R, 512)
            y_ref[:, 128 * j:128 * (j + 1)] = (_dot(hrj, cre_ref[j], NN) - _dot(hij, cim_ref[j], NN)
                                               + d_ref[:, 128 * j:128 * (j + 1)] * u[:, 128 * j:128 * (j + 1)])

    full = lambda shape: pl.BlockSpec(shape, lambda i: (0,) * len(shape))
    hspec = pl.BlockSpec((nb, 8, NS), lambda i: (i, 0, 0))
    return _call(
        body, name, (S // R,),
        [pl.BlockSpec((R, 512), lambda i: (i, 0)), full((8, NS)), full((8, NS)), full((4, 128, 512)),
         full((4, 128, 512)), full((4, 512, 128)), full((4, 512, 128)), full((1, 512))],
        [pl.BlockSpec((R, 512), lambda i: (i, 0)), hspec, hspec],
        [jax.ShapeDtypeStruct((S, 512), f32), jax.ShapeDtypeStruct((S // 8, 8, NS), f32),
         jax.ShapeDtypeStruct((S // 8, 8, NS), f32)],
        scratch=[pltpu.VMEM((1, NS), f32), pltpu.VMEM((1, NS), f32), pltpu.VMEM((nb, NS), f32),
                 pltpu.VMEM((nb, NS), f32)],
        sem=("arbitrary",))(u, pw_re, pw_im, w_re.astype(bf16), w_im.astype(bf16), c_re.astype(bf16),
                            c_im.astype(bf16), dvec)


def s5_bwd(dy, u, h_re, h_im, pw_re, pw_im, w_re, w_im, c_re, c_im, dvec, name):
    S = u.shape[0]
    R, NS = S5_ROWS, S5_STATES
    nb = R // 8
    nchunk = S // R

    def body(dy_ref, u_ref, hr_ref, hi_ref, hpr_ref, hpi_ref, pwr_ref, pwi_ref, wre_ref, wim_ref, cre_ref, cim_ref,
             d_ref, du_ref, dwre_ref, dwim_ref, dcre_ref, dcim_ref, dlr_ref, dli_ref, dd_ref,
             ar, ai, car_re, car_im, cin_re, cin_im):
        i = pl.program_id(0)

        @pl.when(i == 0)
        def _():
            for ref in (car_re, car_im, dwre_ref, dwim_ref, dcre_ref, dcim_ref, dlr_ref, dli_ref, dd_ref):
                ref[...] = jnp.zeros_like(ref)

        dy = dy_ref[...]
        u = u_ref[...]
        for j in range(4):
            dyj = dy[:, 128 * j:128 * (j + 1)]
            ar[:, :, 512 * j:512 * (j + 1)] = _dot(dyj, cre_ref[j], NT).reshape(nb, 8, 512)
            ai[:, :, 512 * j:512 * (j + 1)] = -_dot(dyj, cim_ref[j], NT).reshape(nb, 8, 512)
        lr, li = pwr_ref[0:1, :], pwi_ref[0:1, :]
        for r in range(6, -1, -1):
            nr, ni = ar[:, r + 1, :], ai[:, r + 1, :]
            ar[:, r, :] = lr * nr + li * ni + ar[:, r, :]
            ai[:, r, :] = lr * ni - li * nr + ai[:, r, :]
        l8r, l8i = pwr_ref[7:8, :], pwi_ref[7:8, :]

        def across(k, carry):
            c = nb - 1 - k
            gr, gi = carry
            cin_re[pl.ds(c, 1), :] = gr
            cin_im[pl.ds(c, 1), :] = gi
            er, ei = ar[c, 0:1, :], ai[c, 0:1, :]
            return l8r * gr + l8i * gi + er, l8r * gi - l8i * gr + ei

        gr, gi = lax.fori_loop(0, nb, across, (car_re[...], car_im[...]))
        car_re[...] = gr
        car_im[...] = gi
        cr, ci = cin_re[...], cin_im[...]
        for r in range(8):
            pr, pi = pwr_ref[7 - r:8 - r, :], pwi_ref[7 - r:8 - r, :]
            ar[:, r, :] = ar[:, r, :] + pr * cr + pi * ci
            ai[:, r, :] = ai[:, r, :] + pr * ci - pi * cr

        acc_r = jnp.zeros((1, NS), f32)
        acc_i = jnp.zeros((1, NS), f32)
        for r in range(1, 8):
            xr, xi = hr_ref[:, r - 1, :], hi_ref[:, r - 1, :]
            br, bi = ar[:, r, :], ai[:, r, :]
            acc_r += jnp.sum(br * xr + bi * xi, axis=0, keepdims=True)
            acc_i += jnp.sum(bi * xr - br * xi, axis=0, keepdims=True)
        xr, xi = hr_ref[0:nb - 1, 7, :], hi_ref[0:nb - 1, 7, :]
        br, bi = ar[1:nb, 0, :], ai[1:nb, 0, :]
        acc_r += jnp.sum(br * xr + bi * xi, axis=0, keepdims=True)
        acc_i += jnp.sum(bi * xr - br * xi, axis=0, keepdims=True)
        has_prev = (i < nchunk - 1).astype(f32)
        xr, xi = hpr_ref[0, 7:8, :] * has_prev, hpi_ref[0, 7:8, :] * has_prev
        br, bi = ar[0, 0:1, :], ai[0, 0:1, :]
        dlr_ref[...] += acc_r + br * xr + bi * xi
        dli_ref[...] += acc_i + bi * xr - br * xi
        dd_ref[...] += jnp.sum(dy * u, axis=0, keepdims=True)

        for j in range(4):
            sl = slice(512 * j, 512 * (j + 1))
            cs = slice(128 * j, 128 * (j + 1))
            arj = ar[:, :, sl].reshape(R, 512)
            aij = ai[:, :, sl].reshape(R, 512)
            uj, dyj = u[:, cs], dy[:, cs]
            du_ref[:, cs] = _dot(arj, wre_ref[j], NT) + _dot(aij, wim_ref[j], NT) + d_ref[:, cs] * dyj
            dwre_ref[j] += _dot(uj, arj, TN)
            dwim_ref[j] += _dot(uj, aij, TN)
            dcre_ref[j] += _dot(hr_ref[:, :, sl].reshape(R, 512), dyj, TN)
            dcim_ref[j] -= _dot(hi_ref[:, :, sl].reshape(R, 512), dyj, TN)

    rev = lambda i: nchunk - 1 - i
    full = lambda shape: pl.BlockSpec(shape, lambda i: (0,) * len(shape))
    row = pl.BlockSpec((R, 512), lambda i: (rev(i), 0))
    hspec = pl.BlockSpec((nb, 8, NS), lambda i: (rev(i), 0, 0))
    hprev = pl.BlockSpec((1, 8, NS), lambda i: (jnp.maximum(rev(i) * nb - 1, 0), 0, 0))
    outs = _call(
        body, name, (nchunk,),
        [row, row, hspec, hspec, hprev, hprev, full((8, NS)), full((8, NS)), full((4, 128, 512)), full((4, 128, 512)),
         full((4, 512, 128)), full((4, 512, 128)), full((1, 512))],
        [row, full((4, 128, 512)), full((4, 128, 512)), full((4, 512, 128)), full((4, 512, 128)),
         full((1, NS)), full((1, NS)), full((1, 512))],
        [jax.ShapeDtypeStruct((S, 512), f32), jax.ShapeDtypeStruct((4, 128, 512), f32),
         jax.ShapeDtypeStruct((4, 128, 512), f32), jax.ShapeDtypeStruct((4, 512, 128), f32),
         jax.ShapeDtypeStruct((4, 512, 128), f32), jax.ShapeDtypeStruct((1, NS), f32),
         jax.ShapeDtypeStruct((1, NS), f32), jax.ShapeDtypeStruct((1, 512), f32)],
        scratch=[pltpu.VMEM((nb, 8, NS), f32), pltpu.VMEM((nb, 8, NS), f32), pltpu.VMEM((1, NS), f32),
                 pltpu.VMEM((1, NS), f32), pltpu.VMEM((nb, NS), f32), pltpu.VMEM((nb, NS), f32)],
        sem=("arbitrary",))(dy, u, h_re, h_im, h_re, h_im, pw_re, pw_im, w_re.astype(bf16), w_im.astype(bf16),
                            c_re.astype(bf16), c_im.astype(bf16), dvec)
    return outs


def _rows(start, n, d):
    return pl.ds(pl.multiple_of(start, ATT_BLOCK), n) if d == 1 else pl.ds(start, n, stride=d)


def _att_block(q, k, v, qw, kw, has_prev):
    lane = lax.broadcasted_iota(jnp.int32, (1, LANES), 1)
    hm = [(lane < 64).astype(f32), (lane >= 64).astype(f32)]

    def head_norm(x, w):
        x2 = x * x
        sc = sum(hm[h] * lax.rsqrt(jnp.sum(x2 * hm[h], axis=-1, keepdims=True) * (1.0 / 64) + RMS_EPS)
                 for h in range(2))
        return x * sc * w

    qn, kn = head_norm(q, qw), head_norm(k, kw)
    qi = lax.broadcasted_iota(jnp.int32, (ATT_BLOCK, 2 * ATT_BLOCK), 0) + ATT_BLOCK
    kj = lax.broadcasted_iota(jnp.int32, (ATT_BLOCK, 2 * ATT_BLOCK), 1)
    mask = (qi - kj >= 0) & (qi - kj <= ATT_BLOCK) & (has_prev | (kj >= ATT_BLOCK))
    o = jnp.zeros((ATT_BLOCK, LANES), f32)
    lse = jnp.zeros((ATT_BLOCK, LANES), f32)
    for h in range(2):
        s = _bdot(qn * hm[h], kn, NT) * 0.125
        s = jnp.where(mask, s, -jnp.inf)
        m = jnp.max(s, axis=-1, keepdims=True)
        p = jnp.exp(s - m)
        l = jnp.sum(p, axis=-1, keepdims=True)
        o = o + hm[h] * _bdot(p / l, v, NN)
        lse = lse + hm[h] * (m + jnp.log(l))
    return o, lse


def att_fwd(p_att, qw, kw, d, name):
    S = p_att.shape[0]
    SEG = ATT_SEG
    nblk = SEG // ATT_BLOCK

    def body(p_ref, qw_ref, kw_ref, o_ref, l_ref, q_s, k_ext, v_ext, o_s, l_s):
        seg = pl.program_id(1)

        @pl.when(seg == 0)
        def _():
            k_ext[SEG:, :] = jnp.zeros((SEG, LANES), f32)
            v_ext[SEG:, :] = jnp.zeros((SEG, LANES), f32)

        k_ext[:SEG, :] = k_ext[SEG:, :]
        v_ext[:SEG, :] = v_ext[SEG:, :]
        q_s[...] = p_ref[:, 0:128]
        k_ext[SEG:, :] = p_ref[:, 128:256]
        v_ext[SEG:, :] = p_ref[:, 256:384]
        qw_v, kw_v = qw_ref[...], kw_ref[...]

        def blk(b, carry):
            j, r = b // d, b % d
            qs = j * (ATT_BLOCK * d) + r
            ks = SEG + qs - ATT_BLOCK * d
            o, lse = _att_block(q_s[_rows(qs, ATT_BLOCK, d), :], k_ext[_rows(ks, 2 * ATT_BLOCK, d), :],
                                v_ext[_rows(ks, 2 * ATT_BLOCK, d), :], qw_v, kw_v, (seg > 0) | (j > 0))
            o_s[_rows(qs, ATT_BLOCK, d), :] = o
            l_s[_rows(qs, ATT_BLOCK, d), :] = lse
            return carry

        lax.fori_loop(0, nblk, blk, 0)
        o_ref[...] = o_s[...]
        l_ref[...] = l_s[...]

    vec = pl.BlockSpec((1, LANES), lambda hh, s: (0, 0))
    out = pl.BlockSpec((SEG, LANES), lambda hh, s: (s, hh))
    return _call(body, name, (2, S // SEG), [pl.BlockSpec((SEG, 384), lambda hh, s: (s, hh)), vec, vec],
                 [out, out], [jax.ShapeDtypeStruct((S, 256), f32), jax.ShapeDtypeStruct((S, 256), f32)],
                 scratch=[pltpu.VMEM((SEG, LANES), f32), pltpu.VMEM((2 * SEG, LANES), f32),
                          pltpu.VMEM((2 * SEG, LANES), f32), pltpu.VMEM((SEG, LANES), f32),
                          pltpu.VMEM((SEG, LANES), f32)],
                 sem=("arbitrary", "arbitrary"))(p_att, qw, kw)


def att_bwd(p_att, do, dlse, qw, kw, d, name):
    S = p_att.shape[0]
    SEG = ATT_SEG
    nseg = S // SEG
    nblk = SEG // ATT_BLOCK

    def body(p_ref, pp_ref, do_ref, dl_ref, qw_ref, kw_ref, dp_ref, dqw_ref, dkw_ref,
             q_s, k_ext, v_ext, dq_s, dk_ext, dv_ext):
        hh, i = pl.program_id(0), pl.program_id(1)
        seg = nseg - 1 - i

        @pl.when(i == 0)
        def _():
            dk_ext[...] = jnp.zeros_like(dk_ext)
            dv_ext[...] = jnp.zeros_like(dv_ext)

        @pl.when((i == 0) & (hh == 0))
        def _():
            dqw_ref[...] = jnp.zeros_like(dqw_ref)
            dkw_ref[...] = jnp.zeros_like(dkw_ref)

        dk_ext[SEG:, :] = dk_ext[:SEG, :]
        dv_ext[SEG:, :] = dv_ext[:SEG, :]
        dk_ext[:SEG, :] = jnp.zeros((SEG, LANES), f32)
        dv_ext[:SEG, :] = jnp.zeros((SEG, LANES), f32)
        q_s[...] = p_ref[:, 0:128]
        k_ext[SEG:, :] = p_ref[:, 128:256]
        v_ext[SEG:, :] = p_ref[:, 256:384]
        k_ext[:SEG, :] = pp_ref[:, 128:256]
        v_ext[:SEG, :] = pp_ref[:, 256:384]
        qw_v, kw_v = qw_ref[...], kw_ref[...]

        def blk(b, carry):
            dqw, dkw = carry
            j, r = b // d, b % d
            qs = j * (ATT_BLOCK * d) + r
            ks = SEG + qs - ATT_BLOCK * d
            has_prev = (seg > 0) | (j > 0)
            qrows, krows = _rows(qs, ATT_BLOCK, d), _rows(ks, 2 * ATT_BLOCK, d)
            _, vjp = jax.vjp(lambda q, k, v, a, b_: _att_block(q, k, v, a, b_, has_prev),
                             q_s[qrows, :], k_ext[krows, :], v_ext[krows, :], qw_v, kw_v)
            dq, dk, dv, dqw_b, dkw_b = vjp((do_ref[qrows, :], dl_ref[qrows, :]))
            dq_s[qrows, :] = dq
            dk_ext[krows, :] = dk_ext[krows, :] + dk
            dv_ext[krows, :] = dv_ext[krows, :] + dv
            return dqw + dqw_b, dkw + dkw_b

        zero = jnp.zeros((1, LANES), f32)
        dqw, dkw = lax.fori_loop(0, nblk, blk, (zero, zero))
        dqw_ref[...] += dqw
        dkw_ref[...] += dkw
        dp_ref[:, 0:128] = dq_s[...]
        dp_ref[:, 128:256] = dk_ext[SEG:, :]
        dp_ref[:, 256:384] = dv_ext[SEG:, :]

    rev = lambda i: nseg - 1 - i
    vec = pl.BlockSpec((1, LANES), lambda hh, i: (0, 0))
    cur = pl.BlockSpec((SEG, 384), lambda hh, i: (rev(i), hh))
    prev = pl.BlockSpec((SEG, 384), lambda hh, i: (jnp.maximum(rev(i) - 1, 0), hh))
    col = pl.BlockSpec((SEG, LANES), lambda hh, i: (rev(i), hh))
    big = pltpu.VMEM((2 * SEG, LANES), f32)
    one = pltpu.VMEM((SEG, LANES), f32)
    return _call(body, name, (2, nseg), [cur, prev, col, col, vec, vec], [cur, vec, vec],
                 [jax.ShapeDtypeStruct((S, 768), f32), jax.ShapeDtypeStruct((1, LANES), f32),
                  jax.ShapeDtypeStruct((1, LANES), f32)],
                 scratch=[one, big, big, one, big, big],
                 sem=("arbitrary", "arbitrary"))(p_att, p_att, do, dlse, qw, kw)


def conv_fwd(p_ssd, conv_w, conv_b, name):
    S = p_ssd.shape[0]
    tm, C = CONV_ROWS, SSD_XBC

    def body(x_ref, xp_ref, w_ref, b_ref, o_ref, ext):
        first = (pl.program_id(0) == 0)
        ext[0:8, :] = jnp.where(first, 0.0, xp_ref[:, 0:C])
        ext[8:, :] = x_ref[:, 0:C]
        acc = b_ref[...] + w_ref[3:4, :] * ext[pl.ds(8, tm), :]
        for k in range(1, 4):
            acc = acc + w_ref[3 - k:4 - k, :] * ext[pl.ds(8 - k, tm), :]
        o_ref[...] = jax.nn.silu(acc)

    return _call(body, name, (S // tm,),
                 [pl.BlockSpec((tm, 1536), lambda i: (i, 0)),
                  pl.BlockSpec((8, 1536), lambda i: (jnp.maximum(i * (tm // 8) - 1, 0), 0)),
                  pl.BlockSpec((4, C), lambda i: (0, 0)), pl.BlockSpec((1, C), lambda i: (0, 0))],
                 pl.BlockSpec((tm, C), lambda i: (i, 0)), jax.ShapeDtypeStruct((S, C), f32),
                 scratch=[pltpu.VMEM((tm + 8, C), f32)], sem=("parallel",))(p_ssd, p_ssd, conv_w, conv_b)


def conv_bwd(p_ssd, dact, ddt, conv_w, conv_b, name):
    S = p_ssd.shape[0]
    tm, C = CONV_ROWS, SSD_XBC
    nblk = S // tm

    def body(x_ref, xp_ref, xn_ref, da_ref, dan_ref, ddt_ref, w_ref, b_ref, dp_ref, dw_ref, db_ref, ext, dpre):
        i = pl.program_id(0)
        ext[0:8, :] = jnp.where(i == 0, 0.0, xp_ref[:, 0:C])
        ext[8:tm + 8, :] = x_ref[:, 0:C]
        ext[tm + 8:, :] = xn_ref[:, 0:C]
        pre = b_ref[...] + w_ref[3:4, :] * ext[pl.ds(8, tm + 8), :]
        for k in range(1, 4):
            pre = pre + w_ref[3 - k:4 - k, :] * ext[pl.ds(8 - k, tm + 8), :]
        sg = jax.nn.sigmoid(pre)
        dsilu = sg * (1.0 + pre * (1.0 - sg))
        dpre[0:tm, :] = da_ref[...] * dsilu[0:tm, :]
        dpre[tm:, :] = jnp.where(i == nblk - 1, 0.0, dan_ref[...] * dsilu[tm:, :])
        dx = w_ref[3:4, :] * dpre[pl.ds(0, tm), :]
        for k in range(1, 4):
            dx = dx + w_ref[3 - k:4 - k, :] * dpre[pl.ds(k, tm), :]
        dp_ref[:, 0:C] = dx
        dp_ref[:, C:C + 128] = ddt_ref[...]
        dp_ref[:, C + 128:] = jnp.zeros((tm, 128), f32)
        dcur = dpre[pl.ds(0, tm), :]
        dws = [jnp.sum(dcur * ext[pl.ds(8 - (3 - j), tm), :], axis=0, keepdims=True) for j in range(4)]
        dbs = jnp.sum(dcur, axis=0, keepdims=True)

        @pl.when(i == 0)
        def _():
            dw_ref[...] = jnp.zeros_like(dw_ref)
            db_ref[...] = jnp.zeros_like(db_ref)

        for j in range(4):
            dw_ref[j:j + 1, :] += dws[j]
        db_ref[...] += dbs

    t8 = tm // 8
    return _call(body, name, (nblk,),
                 [pl.BlockSpec((tm, 1536), lambda i: (i, 0)),
                  pl.BlockSpec((8, 1536), lambda i: (jnp.maximum(i * t8 - 1, 0), 0)),
                  pl.BlockSpec((8, 1536), lambda i: (jnp.minimum((i + 1) * t8, S // 8 - 1), 0)),
                  pl.BlockSpec((tm, C), lambda i: (i, 0)),
                  pl.BlockSpec((8, C), lambda i: (jnp.minimum((i + 1) * t8, S // 8 - 1), 0)),
                  pl.BlockSpec((tm, 128), lambda i: (i, 0)),
                  pl.BlockSpec((4, C), lambda i: (0, 0)), pl.BlockSpec((1, C), lambda i: (0, 0))],
                 [pl.BlockSpec((tm, 1536), lambda i: (i, 0)), pl.BlockSpec((4, C), lambda i: (0, 0)),
                  pl.BlockSpec((1, C), lambda i: (0, 0))],
                 [jax.ShapeDtypeStruct((S, 1536), f32), jax.ShapeDtypeStruct((4, C), f32),
                  jax.ShapeDtypeStruct((1, C), f32)],
                 scratch=[pltpu.VMEM((tm + 16, C), f32), pltpu.VMEM((tm + 8, C), f32)],
                 sem=("arbitrary",))(p_ssd, p_ssd, p_ssd, dact, dact, ddt, conv_w, conv_b)


def _ssd_chunk(xbc, dtr, state, dt_bias, a_log, d_full):
    T = SSD_CHUNK
    r_i = lax.broadcasted_iota(jnp.int32, (T, T), 0)
    c_i = lax.broadcasted_iota(jnp.int32, (T, T), 1)
    tril = c_i <= r_i
    tri = tril.astype(bf16)
    e_rows = lax.broadcasted_iota(jnp.int32, (T, SSD_WIDTH), 0)
    e_cols = lax.broadcasted_iota(jnp.int32, (T, SSD_WIDTH), 1)
    expand = (e_cols // 64 == e_rows).astype(bf16)
    w_rows = lax.broadcasted_iota(jnp.int32, (T, 12 * T), 0)
    w_cols = lax.broadcasted_iota(jnp.int32, (T, 12 * T), 1)
    expand_wide = (w_cols // T == w_rows).astype(bf16)
    lane = lax.broadcasted_iota(jnp.int32, (1, LANES), 1)
    hm = [(lane < 64).astype(f32), (lane >= 64).astype(f32)]

    xs, bm, cm = xbc[:, :768], xbc[:, 768:1024], xbc[:, 1024:1280]
    dt = _softplus(dtr + dt_bias)
    a_dt = dt * (-jnp.exp(a_log))
    a_cs = _xdot_l(tri, a_dt)
    dt_full = _xdot_r(dt, expand)
    acs_full = _xdot_r(a_cs, expand)
    acs_wide = _xdot_r(a_cs, expand_wide)
    last = lax.broadcasted_iota(jnp.int32, (T, SSD_WIDTH), 0) == T - 1
    tot_full = jnp.sum(jnp.where(last, acs_full, 0.0), axis=0, keepdims=True)
    xdt = xs * dt_full
    xw = xdt * jnp.exp(tot_full - acs_full)
    eacs = jnp.exp(acs_full)
    st_parts, off_parts, diag_parts = [], [], []
    for g in range(2):
        bg, cg = bm[:, 128 * g:128 * (g + 1)], cm[:, 128 * g:128 * (g + 1)]
        cols = slice(384 * g, 384 * (g + 1))
        st_parts.append(_bdot(bg, xw[:, cols], TN))
        off_parts.append(_bdot(cg, state[:, cols], NN))
        cb = _bdot(cg, bg, NT)
        for pp in range(3 * g, 3 * g + 3):
            xp = xdt[:, 128 * pp:128 * (pp + 1)]
            acc = jnp.zeros((T, LANES), f32)
            for hh in range(2):
                a_col = acs_wide[:, T * (2 * pp + hh):T * (2 * pp + hh + 1)]
                decay = jnp.where(tril, jnp.exp(jnp.minimum(a_col - a_col.T, 0.0)), 0.0)
                acc = acc + _bdot(cb * decay, xp * hm[hh], NN)
            diag_parts.append(acc)
    new_state = state * jnp.exp(tot_full) + jnp.concatenate(st_parts, axis=1)
    y = jnp.concatenate(diag_parts, axis=1) + jnp.concatenate(off_parts, axis=1) * eacs + xs * d_full
    return y, new_state


def ssd_fwd(xact, p_ssd, dt_bias, a_log, d_full, name):
    S = xact.shape[0]
    T = SSD_CHUNK

    def body(x_ref, p_ref, b_ref, a_ref, d_ref, y_ref, s_ref, state):
        @pl.when(pl.program_id(0) == 0)
        def _():
            state[...] = jnp.zeros_like(state)

        st = state[...]
        s_ref[0] = st
        y, new = _ssd_chunk(x_ref[...], p_ref[...], st, b_ref[...], a_ref[...], d_ref[...])
        y_ref[...] = y
        state[...] = new

    vec = lambda n: pl.BlockSpec((1, n), lambda i: (0, 0))
    return _call(body, name, (S // T,),
                 [pl.BlockSpec((T, SSD_XBC), lambda i: (i, 0)), pl.BlockSpec((T, 128), lambda i: (i, 10)),
                  vec(128), vec(128), vec(768)],
                 [pl.BlockSpec((T, 768), lambda i: (i, 0)), pl.BlockSpec((1, T, 768), lambda i: (i, 0, 0))],
                 [jax.ShapeDtypeStruct((S, 768), f32), jax.ShapeDtypeStruct((S // T, T, 768), f32)],
                 scratch=[pltpu.VMEM((T, 768), f32)], sem=("arbitrary",))(xact, p_ssd, dt_bias, a_log, d_full)


def ssd_bwd(xact, p_ssd, states, dy, dt_bias, a_log, d_full, name):
    S = xact.shape[0]
    T = SSD_CHUNK
    nc = S // T

    def body(x_ref, p_ref, s_ref, dy_ref, b_ref, a_ref, d_ref, dx_ref, ddt_ref, db_ref, da_ref, dd_ref, dstate):
        i = pl.program_id(0)

        @pl.when(i == 0)
        def _():
            for ref in (dstate, db_ref, da_ref, dd_ref):
                ref[...] = jnp.zeros_like(ref)

        _, vjp = jax.vjp(_ssd_chunk, x_ref[...], p_ref[...], s_ref[0], b_ref[...], a_ref[...], d_ref[...])
        dx, ddt, dst, db, da, dd = vjp((dy_ref[...], dstate[...]))
        dx_ref[...] = dx
        ddt_ref[...] = ddt
        dstate[...] = dst
        db_ref[...] += db
        da_ref[...] += da
        dd_ref[...] += dd

    rev = lambda i: nc - 1 - i
    vec = lambda n: pl.BlockSpec((1, n), lambda i: (0, 0))
    return _call(body, name, (nc,),
                 [pl.BlockSpec((T, SSD_XBC), lambda i: (rev(i), 0)), pl.BlockSpec((T, 128), lambda i: (rev(i), 10)),
                  pl.BlockSpec((1, T, 768), lambda i: (rev(i), 0, 0)), pl.BlockSpec((T, 768), lambda i: (rev(i), 0)),
                  vec(128), vec(128), vec(768)],
                 [pl.BlockSpec((T, SSD_XBC), lambda i: (rev(i), 0)), pl.BlockSpec((T, 128), lambda i: (rev(i), 0)),
                  vec(128), vec(128), vec(768)],
                 [jax.ShapeDtypeStruct((S, SSD_XBC), f32), jax.ShapeDtypeStruct((S, 128), f32),
                  jax.ShapeDtypeStruct((1, 128), f32), jax.ShapeDtypeStruct((1, 128), f32),
                  jax.ShapeDtypeStruct((1, 768), f32)],
                 scratch=[pltpu.VMEM((T, 768), f32)],
                 sem=("arbitrary",))(xact, p_ssd, states, dy, dt_bias, a_log, d_full)


def _tail_fn(ys5, pt, o0, o1, o2, l0, l1, l2, yssd, glu_b, nw, pr_glu, pr_a, pr_b, pr_c, x, weights):
    glu_w, pa, pb, pc, wo = weights
    gates = jax.nn.sigmoid(pt[:, :3072])
    za, zb, zc = pt[:, 3072:3584], pt[:, 3584:3840], pt[:, 3840:4608]
    g = jax.nn.gelu(ys5)
    ya = g * jax.nn.sigmoid(_cdot(g, glu_w, NN) + glu_b + pr_glu) * jax.nn.silu(za)
    m = jnp.maximum(jnp.maximum(l0, l1), l2)
    e0, e1, e2 = jnp.exp(l0 - m), jnp.exp(l1 - m), jnp.exp(l2 - m)
    yb = (e0 * o0 + e1 * o1 + e2 * o2) / (e0 + e1 + e2) * jax.nn.silu(zb)
    yc = _rms(yssd * jax.nn.silu(zc), nw)
    merged = (gates[:, :1024] * (_cdot(ya, pa, NN) + pr_a) + gates[:, 1024:2048] * (_cdot(yb, pb, NN) + pr_b)
              + gates[:, 2048:] * (_cdot(yc, pc, NN) + pr_c))
    out = x + _cdot(merged, wo, NN)
    return out, (g, ya, yb, yc, merged)


def _tail_specs(tm):
    row = lambda n: pl.BlockSpec((tm, n), lambda i: (i, 0))
    full = lambda a, b: pl.BlockSpec((a, b), lambda i: (0, 0))
    acts = [row(512), row(4608)] + [row(256)] * 6 + [row(768), row(D_MODEL)]
    consts = [full(1, 512), full(1, 768), full(512, 512), full(512, D_MODEL), full(256, D_MODEL),
              full(768, D_MODEL), full(D_MODEL, D_MODEL)]
    return row, full, acts, consts


def tail_fwd(ys5, pt, os_, ls_, yssd, x, glu_b, nw, weights, name):
    S = x.shape[0]
    tm = TAIL_ROWS
    row, full, acts, consts = _tail_specs(tm)

    def body(ys5_ref, pt_ref, o0, o1, o2, l0, l1, l2, yssd_ref, x_ref, gb_ref, nw_ref, gw, pa, pb, pc, wo, out_ref):
        z = lambda n: jnp.zeros((tm, n), f32)
        out, _ = _tail_fn(ys5_ref[...], pt_ref[...], o0[...], o1[...], o2[...], l0[...], l1[...], l2[...],
                          yssd_ref[...], gb_ref[...], nw_ref[...], z(512), z(D_MODEL), z(D_MODEL), z(D_MODEL),
                          x_ref[...], (gw[...], pa[...], pb[...], pc[...], wo[...]))
        out_ref[...] = out

    return _call(body, name, (S // tm,), acts + consts, row(D_MODEL), jax.ShapeDtypeStruct((S, D_MODEL), f32),
                 sem=("parallel",))(ys5, pt, *os_, *ls_, yssd, x, glu_b, nw, *weights)


def tail_bwd(ys5, pt, os_, ls_, yssd, dout, glu_b, nw, weights, name):
    S = dout.shape[0]
    tm = TAIL_ROWS
    row, full, acts, consts = _tail_specs(tm)

    def body(ys5_ref, pt_ref, o0, o1, o2, l0, l1, l2, yssd_ref, dout_ref, gb_ref, nw_ref, gw, pa, pb, pc, wo,
             dys5_ref, dpt_ref, do0, do1, do2, dl0, dl1, dl2, dyssd_ref, dgb_ref, dnw_ref,
             g_ref, ya_ref, yb_ref, yc_ref, mg_ref, dglu_ref, dpa_ref, dpb_ref, dpc_ref):
        z = lambda n: jnp.zeros((tm, n), f32)
        w = (gw[...], pa[...], pb[...], pc[...], wo[...])
        fn = lambda *a: _tail_fn(*a, z(D_MODEL), w)
        _, vjp, aux = jax.vjp(fn, ys5_ref[...], pt_ref[...], o0[...], o1[...], o2[...], l0[...], l1[...], l2[...],
                              yssd_ref[...], gb_ref[...], nw_ref[...], z(512), z(D_MODEL), z(D_MODEL), z(D_MODEL),
                              has_aux=True)
        (dys5, dpt, d0, d1, d2, e0, e1, e2, dyssd, dgb, dnw, dglu, dpa, dpb, dpc) = vjp(dout_ref[...])
        dys5_ref[...] = dys5
        dpt_ref[...] = dpt
        for ref, val in ((do0, d0), (do1, d1), (do2, d2), (dl0, e0), (dl1, e1), (dl2, e2)):
            ref[...] = val
        dyssd_ref[...] = dyssd
        g, ya, yb, yc, merged = aux
        for ref, val in ((g_ref, g), (ya_ref, ya), (yb_ref, yb), (yc_ref, yc), (mg_ref, merged),
                         (dglu_ref, dglu), (dpa_ref, dpa), (dpb_ref, dpb), (dpc_ref, dpc)):
            ref[...] = val.astype(bf16)

        @pl.when(pl.program_id(0) == 0)
        def _():
            dgb_ref[...] = dgb
            dnw_ref[...] = dnw

        @pl.when(pl.program_id(0) > 0)
        def _():
            dgb_ref[...] += dgb
            dnw_ref[...] += dnw

    sd = lambda n, dt=f32: jax.ShapeDtypeStruct((S, n), dt)
    out_specs = ([row(512), row(4608)] + [row(256)] * 6 + [row(768), full(1, 512), full(1, 768)]
                 + [row(512), row(512), row(256), row(768), row(D_MODEL), row(512)] + [row(D_MODEL)] * 3)
    out_shape = ([sd(512), sd(4608)] + [sd(256)] * 6 + [sd(768), jax.ShapeDtypeStruct((1, 512), f32),
                                                          jax.ShapeDtypeStruct((1, 768), f32)]
                 + [sd(512, bf16), sd(512, bf16), sd(256, bf16), sd(768, bf16), sd(D_MODEL, bf16), sd(512, bf16)]
                 + [sd(D_MODEL, bf16)] * 3)
    return _call(body, name, (S // tm,), acts + consts, out_specs, out_shape,
                 sem=("arbitrary",))(ys5, pt, *os_, *ls_, yssd, dout, glu_b, nw, *weights)


def _in_proj_segments(w):
    c = lambda a, b: w[:, a:b]
    atts = []
    for g in range(3):
        parts = []
        for hh in range(2):
            o = 64 * (4 * g + 2 * hh)
            parts += [c(_C_Q + o, _C_Q + o + 128), c(_C_K + o, _C_K + o + 128), c(_C_V + o, _C_V + o + 128)]
        atts.append(jnp.concatenate(parts, axis=1))
    ssd = jnp.concatenate([c(_C_XBC, _C_ZC), jnp.zeros((D_MODEL, 1536 - (_C_ZC - _C_XBC)), w.dtype)], axis=1)
    tail = jnp.concatenate([c(_C_GATE, _C_END), c(_C_ZA, _C_Q), c(_C_ZB, _C_XBC), c(_C_ZC, _C_GATE)], axis=1)
    return [c(_C_UA, _C_ZA)] + atts + [ssd, tail]


def _in_proj_grad(ds5, datts, dssd, dtail):
    pick = lambda off: jnp.concatenate([datts[g][:, 384 * hh + off:384 * hh + off + 128]
                                        for g in range(3) for hh in range(2)], axis=1)
    return jnp.concatenate([ds5, dtail[:, 3072:3584], pick(0), pick(128), pick(256), dtail[:, 3584:3840],
                            dssd[:, :_C_ZC - _C_XBC], dtail[:, 3840:4608], dtail[:, :3072]], axis=1)


def _prep_layer(p):
    q = {}
    q["segs"] = [s.astype(bf16) for s in _in_proj_segments(p["w_in"])]
    disc = _s5_discretize(p["s5_a_re"], p["s5_a_im"], p["s5_log_step"], p["s5_b_re"], p["s5_b_im"],
                          p["s5_c_re"], p["s5_c_im"])
    q["s5"] = disc
    q["pw"] = _lam_powers(disc[0], disc[1])
    q["s5_d"] = p["s5_d"].reshape(1, 512)
    q["qw"] = jnp.tile(p["q_norm_w"], 2).reshape(1, LANES)
    q["kw"] = jnp.tile(p["k_norm_w"], 2).reshape(1, LANES)
    q["conv_w"] = p["conv_w"]
    q["conv_b"] = p["conv_b"].reshape(1, SSD_XBC)
    pad = lambda v: jnp.pad(v, (0, LANES - v.shape[0])).reshape(1, LANES)
    q["dt_bias"], q["a_log"] = pad(p["dt_bias"]), pad(p["ssd_a_log"])
    q["d_full"] = jnp.repeat(p["ssd_d"], 64).reshape(1, SSD_WIDTH)
    q["glu_b"] = p["s5_glu_b"].reshape(1, 512)
    q["nw"] = p["ssd_norm_w"].reshape(1, SSD_WIDTH)
    q["norm_w"] = p["norm_w"].reshape(1, D_MODEL)
    q["tailw"] = tuple(p[n].astype(bf16) for n in ("s5_glu_w", "proj_a", "proj_b", "proj_c", "w_out"))
    return q


_DILATIONS = (1, 4, 16)


def layer_fwd(x, q, tag):
    h = rms_fwd(x, q["norm_w"], f"rms_fwd{tag}")
    p_s5, p_a0, p_a1, p_a2, p_ssd, p_tail = [mm_nn(h, w, f"inproj{k}{tag}") for k, w in enumerate(q["segs"])]
    _, _, w_re, w_im, c_re, c_im = q["s5"]
    ys5, h_re, h_im = s5_fwd(p_s5, *q["pw"], w_re, w_im, c_re, c_im, q["s5_d"], f"s5_fwd{tag}")
    p_atts = (p_a0, p_a1, p_a2)
    os_, ls_ = [], []
    for g, d in enumerate(_DILATIONS):
        o, l = att_fwd(p_atts[g], q["qw"], q["kw"], d, f"att_fwd{g}{tag}")
        os_.append(o)
        ls_.append(l)
    xact = conv_fwd(p_ssd, q["conv_w"], q["conv_b"], f"conv_fwd{tag}")
    yssd, states = ssd_fwd(xact, p_ssd, q["dt_bias"], q["a_log"], q["d_full"], f"ssd_fwd{tag}")
    out = tail_fwd(ys5, p_tail, os_, ls_, yssd, x, q["glu_b"], q["nw"], q["tailw"], f"tail_fwd{tag}")
    saved = dict(x=x, h=h, p_s5=p_s5, p_atts=p_atts, p_ssd=p_ssd, p_tail=p_tail, ys5=ys5, h_re=h_re, h_im=h_im,
                 os=os_, ls=ls_, xact=xact, yssd=yssd, states=states)
    return out, saved


def layer_bwd(dout, sv, q, p, tag):
    S = dout.shape[0]
    (dys5, dp_tail, do0, do1, do2, dl0, dl1, dl2, dyssd, dglu_b, dnw, g_b, ya_b, yb_b, yc_b, mg_b, dglu_b16,
     dpa_b, dpb_b, dpc_b) = tail_bwd(sv["ys5"], sv["p_tail"], sv["os"], sv["ls"], sv["yssd"], dout, q["glu_b"],
                                     q["nw"], q["tailw"], f"tail_bwd{tag}")
    grads = {}
    grads["s5_glu_w"] = mm_tn(g_b, dglu_b16, f"dglu_w{tag}")
    grads["proj_a"] = mm_tn(ya_b, dpa_b, f"dproj_a{tag}")
    grads["proj_b"] = mm_tn(yb_b, dpb_b, f"dproj_b{tag}")
    grads["proj_c"] = mm_tn(yc_b, dpc_b, f"dproj_c{tag}")
    grads["w_out"] = mm_tn(mg_b, dout, f"dw_out{tag}")
    grads["s5_glu_b"] = dglu_b.reshape(512)
    grads["ssd_norm_w"] = dnw.reshape(SSD_WIDTH)

    dxact, ddt, ddt_bias, da_log, dd_full = ssd_bwd(sv["xact"], sv["p_ssd"], sv["states"], dyssd, q["dt_bias"],
                                                    q["a_log"], q["d_full"], f"ssd_bwd{tag}")
    dp_ssd, dconv_w, dconv_b = conv_bwd(sv["p_ssd"], dxact, ddt, q["conv_w"], q["conv_b"], f"conv_bwd{tag}")
    grads["dt_bias"] = ddt_bias[0, :12]
    grads["ssd_a_log"] = da_log[0, :12]
    grads["ssd_d"] = dd_full.reshape(12, 64).sum(axis=1)
    grads["conv_w"] = dconv_w
    grads["conv_b"] = dconv_b.reshape(SSD_XBC)

    dp_atts, dqw, dkw = [], 0.0, 0.0
    for g, d in enumerate(_DILATIONS):
        dp, a, b = att_bwd(sv["p_atts"][g], (do0, do1, do2)[g], (dl0, dl1, dl2)[g], q["qw"], q["kw"], d,
                           f"att_bwd{g}{tag}")
        dp_atts.append(dp)
        dqw, dkw = dqw + a, dkw + b
    grads["q_norm_w"] = dqw.reshape(2, 64).sum(axis=0)
    grads["k_norm_w"] = dkw.reshape(2, 64).sum(axis=0)

    _, _, w_re, w_im, c_re, c_im = q["s5"]
    dp_s5, dwre, dwim, dcre, dcim, dlam_re, dlam_im, dd = s5_bwd(
        dys5, sv["p_s5"], sv["h_re"], sv["h_im"], *q["pw"], w_re, w_im, c_re, c_im, q["s5_d"], f"s5_bwd{tag}")
    s5_names = ("s5_a_re", "s5_a_im", "s5_log_step", "s5_b_re", "s5_b_im", "s5_c_re", "s5_c_im")
    _, disc_vjp = jax.vjp(_s5_discretize, *[p[n] for n in s5_names])
    for n, gr in zip(s5_names, disc_vjp((dlam_re, dlam_im, dwre, dwim, dcre, dcim))):
        grads[n] = gr
    grads["s5_d"] = dd.reshape(512)

    dsegs = [dp_s5] + dp_atts + [dp_ssd, dp_tail]
    dws = [mm_tn(sv["h"], ds, f"dw_in{k}{tag}") for k, ds in enumerate(dsegs)]
    grads["w_in"] = _in_proj_grad(dws[0], dws[1:4], dws[4], dws[5])
    dh = None
    for k, (ds, w) in enumerate(zip(dsegs, q["segs"])):
        dh = mm_nt(ds, w, f"dh{k}{tag}", acc=dh)
    dx, dnorm_w = rms_bwd(sv["x"], q["norm_w"], dh, dout, f"rms_bwd{tag}")
    grads["norm_w"] = dnorm_w.reshape(D_MODEL)
    return dx, grads


_ANY = pl.BlockSpec(memory_space=pl.ANY)


def _chip_exchange(x, name, broadcast):
    shape = tuple(x.shape) if broadcast else tuple(x.shape[1:])

    def body(x_ref, o_ref, send_sems, recv_sems, local_sem):
        mx, my, mc = lax.axis_index("x"), lax.axis_index("y"), lax.axis_index("c")
        me = 2 * mx + my
        copies = []
        for t, (px, py) in enumerate(((1 - mx, my), (mx, 1 - my), (1 - mx, 1 - my))):
            src = x_ref if broadcast else x_ref.at[2 * px + py]
            cp = pltpu.make_async_remote_copy(src_ref=src, dst_ref=o_ref.at[me], send_sem=send_sems.at[t],
                                              recv_sem=recv_sems.at[t], device_id=(px, py, mc),
                                              device_id_type=pl.DeviceIdType.MESH)
            cp.start()
            copies.append(cp)
        mine = pltpu.make_async_copy(x_ref if broadcast else x_ref.at[me], o_ref.at[me], local_sem)
        mine.start()
        for cp in copies:
            cp.wait()
        mine.wait()

    return pl.pallas_call(
        body, name=name, in_specs=[_ANY], out_specs=_ANY, out_shape=jax.ShapeDtypeStruct((4,) + shape, x.dtype),
        scratch_shapes=[pltpu.SemaphoreType.DMA((3,)), pltpu.SemaphoreType.DMA((3,)), pltpu.SemaphoreType.DMA],
    )(x)


def _sibling_gather(xs, name):
    n = len(xs)

    def body(*refs):
        x_refs, o_refs = refs[:n], refs[n:2 * n]
        send_sems, recv_sems, local_sems = refs[2 * n:]
        mx, my, mc = lax.axis_index("x"), lax.axis_index("y"), lax.axis_index("c")
        copies = []
        for t in range(n):
            cp = pltpu.make_async_remote_copy(src_ref=x_refs[t], dst_ref=o_refs[t].at[mc], send_sem=send_sems.at[t],
                                              recv_sem=recv_sems.at[t], device_id=(mx, my, 1 - mc),
                                              device_id_type=pl.DeviceIdType.MESH)
            cp.start()
            mine = pltpu.make_async_copy(x_refs[t], o_refs[t].at[mc], local_sems.at[t])
            mine.start()
            copies += [cp, mine]
        for cp in copies:
            cp.wait()

    return pl.pallas_call(
        body, name=name, in_specs=[_ANY] * n, out_specs=[_ANY] * n,
        out_shape=[jax.ShapeDtypeStruct((2,) + tuple(x.shape), x.dtype) for x in xs],
        scratch_shapes=[pltpu.SemaphoreType.DMA((n,)), pltpu.SemaphoreType.DMA((n,)), pltpu.SemaphoreType.DMA((n,))],
    )(*xs)


def _sibling_exchange(x, name):
    def body(x_ref, o_ref, send_sem, recv_sem):
        peer = (lax.axis_index("x"), lax.axis_index("y"), 1 - lax.axis_index("c"))
        cp = pltpu.make_async_remote_copy(src_ref=x_ref, dst_ref=o_ref, send_sem=send_sem, recv_sem=recv_sem,
                                          device_id=peer, device_id_type=pl.DeviceIdType.MESH)
        cp.start()
        cp.wait()

    return pl.pallas_call(
        body, name=name, in_specs=[_ANY], out_specs=_ANY, out_shape=jax.ShapeDtypeStruct(x.shape, x.dtype),
        scratch_shapes=[pltpu.SemaphoreType.DMA, pltpu.SemaphoreType.DMA],
    )(x)


def _rows_tile(rows, row_bytes, budget=1 << 20):
    return next(t for t in (512, 256, 128, 64, 32, 16, 8) if rows % t == 0 and t * row_bytes <= budget)


def _padded_row_bytes(cols):
    return -(-cols // LANES) * LANES * 4


def _add2(a, b, name):
    R, C = a.shape
    tr = _rows_tile(R, _padded_row_bytes(C))

    def body(a_ref, b_ref, o_ref):
        o_ref[...] = a_ref[...] + b_ref[...]

    spec = pl.BlockSpec((tr, C), lambda i: (i, 0))
    return _call(body, name, (R // tr,), [spec, spec], spec, jax.ShapeDtypeStruct((R, C), f32),
                 sem=("parallel",))(a, b)


def _sum4(x, name):
    R = x.shape[1]
    tr = _tile(R, (512, 256, 128))

    def body(x_ref, o_ref):
        o_ref[...] = ((x_ref[0] + x_ref[1]) + x_ref[2]) + x_ref[3]

    return _call(body, name, (R // tr,), [pl.BlockSpec((4, tr, LANES), lambda i: (0, i, 0))],
                 pl.BlockSpec((tr, LANES), lambda i: (i, 0)), jax.ShapeDtypeStruct((R, LANES), f32),
                 sem=("parallel",))(x)


def _adamw(g_parts, w, m, v, name):
    stacked = not isinstance(g_parts, (tuple, list))
    k = g_parts.shape[0] if stacked else len(g_parts)
    R, C = w.shape
    tr = _rows_tile(R, _padded_row_bytes(C))
    c1 = 1.0 - ADAM_B1 ** ADAM_STEP
    c2 = 1.0 - ADAM_B2 ** ADAM_STEP

    def body(*refs):
        w_ref, m_ref, v_ref, g_ref, d_ref, nm_ref, nv_ref = refs[-7:]
        if stacked:
            g = refs[0][0]
            for j in range(1, k):
                g = g + refs[0][j]
        else:
            g = refs[0][...]
            for r in refs[1:k]:
                g = g + r[...]
        m = ADAM_B1 * m_ref[...] + (1.0 - ADAM_B1) * g
        v = ADAM_B2 * v_ref[...] + (1.0 - ADAM_B2) * (g * g)
        g_ref[...] = g
        nm_ref[...] = m
        nv_ref[...] = v
        d_ref[...] = -ADAM_LR * ((m / c1) / (jnp.sqrt(v / c2) + ADAM_EPS) + ADAM_WD * w_ref[...])

    spec = pl.BlockSpec((tr, C), lambda i: (i, 0))
    sd = jax.ShapeDtypeStruct((R, C), f32)
    g_specs = [pl.BlockSpec((k, tr, C), lambda i: (0, i, 0))] if stacked else [spec] * k
    g_args = [g_parts] if stacked else list(g_parts)
    return _call(body, name, (R // tr,), g_specs + [spec] * 3, [spec] * 4, [sd] * 4,
                 sem=("parallel",))(*g_args, w, m, v)


def _pack(arrays):
    flat = jnp.concatenate([a.reshape(-1) for a in arrays])
    unit = PACK_ROWS * LANES
    n = -(-flat.shape[0] // unit) * unit
    return jnp.pad(flat, (0, n - flat.shape[0])).reshape(n // LANES, LANES)


def _unpack(buf, shapes):
    flat = buf.reshape(-1)
    out, off = [], 0
    for s in shapes:
        n = 1
        for dim in s:
            n *= dim
        out.append(flat[off:off + n].reshape(s))
        off += n
    return out


def _to_shards(full, axis):
    s = full.shape
    t = full.reshape(s[:axis] + (4, s[axis] // 4) + s[axis + 1:])
    return jnp.moveaxis(t, axis, 0)


def _from_shards(sh, axis):
    t = jnp.moveaxis(sh, 0, axis)
    s = t.shape
    return t.reshape(s[:axis] + (s[axis] * s[axis + 1],) + s[axis + 2:])


def kernel(x, norm_w, w_in, s5_a_re, s5_a_im, s5_log_step, s5_b_re, s5_b_im, s5_c_re, s5_c_im, s5_d, s5_glu_w, s5_glu_b, q_norm_w, k_norm_w, conv_w, conv_b, dt_bias, ssd_a_log, ssd_d, ssd_norm_w, proj_a, proj_b, proj_c, w_out, loss_target, m_norm_w, m_w_in, m_s5_a_re, m_s5_a_im, m_s5_log_step, m_s5_b_re, m_s5_b_im, m_s5_c_re, m_s5_c_im, m_s5_d, m_s5_glu_w, m_s5_glu_b, m_q_norm_w, m_k_norm_w, m_conv_w, m_conv_b, m_dt_bias, m_ssd_a_log, m_ssd_d, m_ssd_norm_w, m_proj_a, m_proj_b, m_proj_c, m_w_out, v_norm_w, v_w_in, v_s5_a_re, v_s5_a_im, v_s5_log_step, v_s5_b_re, v_s5_b_im, v_s5_c_re, v_s5_c_im, v_s5_d, v_s5_glu_w, v_s5_glu_b, v_q_norm_w, v_k_norm_w, v_conv_w, v_conv_b, v_dt_bias, v_ssd_a_log, v_ssd_d, v_ssd_norm_w, v_proj_a, v_proj_b, v_proj_c, v_w_out):
    given = dict(locals())
    W = {n: given[n] for n in _WEIGHTS}
    M = {n: given["m_" + n] for n in _WEIGHTS}
    V = {n: given["v_" + n] for n in _WEIGHTS}
    n_layers = norm_w.shape[0]
    assert n_layers == 2
    c = lax.axis_index("c")

    local_shapes = [W[n].shape for n, _ in _SHARDED]
    gathered = _chip_exchange(_pack([W[n] for n, _ in _SHARDED]), "gather_weights", broadcast=True)
    full = dict(W)
    full["w_in"] = _from_shards(_chip_exchange(w_in.astype(bf16), "gather_w_in", broadcast=True), 2)
    pieces = [_unpack(gathered[j], local_shapes) for j in range(4)]
    for k, (n, axis) in enumerate(_SHARDED):
        full[n] = _from_shards(jnp.stack([pieces[j][k] for j in range(4)]), axis)

    xs = x[0]
    qs, saves = [], []
    act = xs
    for l in range(n_layers):
        p = {n: full[n][l] for n in _WEIGHTS}
        q = _prep_layer(p)
        act, sv = layer_fwd(act, q, f"_l{l}")
        qs.append((q, p))
        saves.append(sv)
    dact, lsum = loss_and_grad(act, loss_target[0], "loss")
    loss = lax.psum(lsum[0, 0], ("x", "y", "c"))
    layer_grads = [None] * n_layers
    for l in reversed(range(n_layers)):
        q, p = qs[l]
        dact, layer_grads[l] = layer_bwd(dact, saves[l], q, p, f"_l{l}")
    grad_x = dact[None]
    G = {n: jnp.stack([layer_grads[l][n] for l in range(n_layers)]) for n in _WEIGHTS if n != "w_in"}

    g0, g1 = layer_grads[0]["w_in"], layer_grads[1]["w_in"]
    from_sibling = _sibling_exchange(jnp.where(c == 0, g1, g0), "swap_w_in_grads")
    g_chip = _add2(jnp.where(c == 0, g0, g1), from_sibling, "sum_cores_w_in")
    shards = jnp.stack([g_chip[:, W_IN_SHARD * k:W_IN_SHARD * (k + 1)] for k in range(4)])
    landed = _chip_exchange(shards, "scatter_w_in_grads", broadcast=False)
    mine_of = lambda t: lax.dynamic_index_in_dim(t, c, 0, keepdims=False)
    w_in_out = _sibling_gather(_adamw(landed, mine_of(w_in), mine_of(m_w_in), mine_of(v_w_in), "adamw_w_in"),
                               "share_w_in_updates")

    repl_shapes = [W[n].shape for n in _REPL]
    small = _pack([G[n] for n in _REPL])
    quarter = small.shape[0] // 4
    big = [_to_shards(G[n], axis).reshape(4, -1) for n, axis in _SHARDED]
    big = jnp.concatenate(big, axis=1)
    unit = PACK_ROWS * LANES
    nbig = -(-big.shape[1] // unit) * unit
    big = jnp.pad(big, ((0, 0), (0, nbig - big.shape[1]))).reshape(4, nbig // LANES, LANES)
    gpack = jnp.concatenate([big, small.reshape(4, quarter, LANES)], axis=1)
    mine = _sum4(_chip_exchange(gpack, "scatter_grads", broadcast=False), "sum_chips")
    other = _sibling_exchange(mine, "swap_cores")
    rbig = nbig // LANES

    wp, mp, vp = (_pack([T[n] for n, _ in _SHARDED]) for T in (W, M, V))
    outs_big = _adamw((mine[:rbig], other[:rbig]), wp, mp, vp, "adamw_sharded")
    big_out = [_unpack(o, local_shapes) for o in outs_big]

    gq = _add2(mine[rbig:], other[rbig:], "sum_cores_small")
    gsmall = _chip_exchange(gq, "gather_small", broadcast=True).reshape(4 * quarter, LANES)
    ws, ms, vs = (_pack([T[n] for n in _REPL]) for T in (W, M, V))
    outs_small = _adamw((gsmall,), ws, ms, vs, "adamw_replicated")
    small_out = [_unpack(o, repl_shapes) for o in outs_small]

    res = [dict(), dict(), dict(), dict()]
    for kind in range(4):
        res[kind]["w_in"] = w_in_out[kind]
        for k, (n, _) in enumerate(_SHARDED):
            res[kind][n] = big_out[kind][k]
        for k, n in enumerate(_REPL):
            res[kind][n] = small_out[kind][k]
    return (loss, grad_x, *[res[0][n] for n in _WEIGHTS], *[res[1][n] for n in _WEIGHTS],
            *[res[2][n] for n in _WEIGHTS], *[res[3][n] for n in _WEIGHTS])
```

```python
import functools

import jax
import jax.numpy as jnp
from jax import lax
from jax.experimental import pallas as pl
from jax.experimental.pallas import tpu as pltpu

f32 = jnp.float32
bf16 = jnp.bfloat16

D_MODEL = 1024
RMS_EPS = 1e-6
V7X_VMEM_LIMIT = 60 * 1024 * 1024
LANES = 128
NN, NT, TN = ((1,), (0,)), ((1,), (1,)), ((0,), (0,))

S5_STATES = 2048
S5_ROWS = 256
ATT_SEG = 2048
ATT_BLOCK = 128
SSD_CHUNK = 128
SSD_WIDTH = 768
SSD_XBC = 1280
CONV_ROWS = 512
TAIL_ROWS = 128

ADAM_LR, ADAM_B1, ADAM_B2, ADAM_EPS, ADAM_WD, ADAM_STEP = 0.001, 0.9, 0.999, 1e-08, 0.01, 10

_C_UA, _C_ZA, _C_Q, _C_K, _C_V, _C_ZB, _C_XBC, _C_DT, _C_ZC, _C_GATE, _C_END = (
    0, 512, 1024, 1792, 2560, 3328, 3584, 4864, 4876, 5644, 8716)

_SHARDED = (("s5_glu_w", 1), ("conv_w", 2), ("proj_a", 2), ("proj_b", 2), ("proj_c", 2), ("w_out", 1))
W_IN_SHARD = 2179
_REPL = ("norm_w", "s5_a_re", "s5_a_im", "s5_log_step", "s5_b_re", "s5_b_im", "s5_c_re", "s5_c_im", "s5_d",
         "s5_glu_b", "q_norm_w", "k_norm_w", "conv_b", "dt_bias", "ssd_a_log", "ssd_d", "ssd_norm_w")
_WEIGHTS = ("norm_w", "w_in", "s5_a_re", "s5_a_im", "s5_log_step", "s5_b_re", "s5_b_im", "s5_c_re", "s5_c_im",
            "s5_d", "s5_glu_w", "s5_glu_b", "q_norm_w", "k_norm_w", "conv_w", "conv_b", "dt_bias", "ssd_a_log",
            "ssd_d", "ssd_norm_w", "proj_a", "proj_b", "proj_c", "w_out")
PACK_ROWS = 512


def _dot(a, b, dims):
    return lax.dot_general(a.astype(bf16), b.astype(bf16), (dims, ((), ())), preferred_element_type=f32)


def _call(body, name, grid, in_specs, out_specs, out_shape, scratch=(), sem=None):
    return pl.pallas_call(
        body, name=name, grid=grid, in_specs=in_specs, out_specs=out_specs, out_shape=out_shape,
        scratch_shapes=list(scratch),
        compiler_params=pltpu.CompilerParams(dimension_semantics=sem, vmem_limit_bytes=V7X_VMEM_LIMIT))


def _tile(n, options=(1024, 768, 512, 384, 256, 128)):
    return next(t for t in options if n % t == 0)


@functools.partial(jax.custom_vjp, nondiff_argnums=(2,))
def _bdot(a, b, dims):
    return _dot(a, b, dims)


def _bdot_fwd(a, b, dims):
    return _dot(a, b, dims), (a, b)


def _bdot_bwd(dims, res, g):
    a, b = res
    if dims == NN:
        da, db = _dot(g, b, NT), _dot(a, g, TN)
    elif dims == NT:
        da, db = _dot(g, b, NN), _dot(g, a, TN)
    else:
        da, db = _dot(b, g, NT), _dot(a, g, NN)
    return da.astype(a.dtype), db.astype(b.dtype)


_bdot.defvjp(_bdot_fwd, _bdot_bwd)


@functools.partial(jax.custom_vjp, nondiff_argnums=(2,))
def _cdot(a, w, dims):
    return _dot(a, w, dims)


def _cdot_fwd(a, w, dims):
    return _dot(a, w, dims), w


def _cdot_bwd(dims, w, g):
    da = _dot(g, w, NT) if dims == NN else _dot(g, w, NN)
    return da, jnp.zeros_like(w)


_cdot.defvjp(_cdot_fwd, _cdot_bwd)


def _split3(x):
    hi = x.astype(bf16)
    r = x - hi.astype(f32)
    mid = r.astype(bf16)
    lo = (r - mid.astype(f32)).astype(bf16)
    return hi, mid, lo


@jax.custom_vjp
def _xdot_r(x, m):
    return sum(_dot(p, m, NN) for p in _split3(x))


def _xdot_r_fwd(x, m):
    return _xdot_r(x, m), m


def _xdot_r_bwd(m, g):
    return sum(_dot(p, m, NT) for p in _split3(g)), jnp.zeros_like(m)


_xdot_r.defvjp(_xdot_r_fwd, _xdot_r_bwd)


@jax.custom_vjp
def _xdot_l(m, x):
    return sum(_dot(m, p, NN) for p in _split3(x))


def _xdot_l_fwd(m, x):
    return _xdot_l(m, x), m


def _xdot_l_bwd(m, g):
    return jnp.zeros_like(m), sum(_dot(m, p, TN) for p in _split3(g))


_xdot_l.defvjp(_xdot_l_fwd, _xdot_l_bwd)


@jax.custom_vjp
def _softplus(x):
    e = jnp.exp(-jnp.abs(x))
    u = 1.0 + e
    log1p = jnp.where(u == 1.0, e, jnp.log(u) * (e / jnp.where(u == 1.0, 1.0, u - 1.0)))
    return jnp.maximum(x, 0.0) + log1p


def _softplus_fwd(x):
    return _softplus(x), x


def _softplus_bwd(x, g):
    return (g * jax.nn.sigmoid(x),)


_softplus.defvjp(_softplus_fwd, _softplus_bwd)


def _rms(x, w):
    return x * lax.rsqrt(jnp.mean(x * x, axis=-1, keepdims=True) + RMS_EPS) * w


def mm_nn(a, b, name, tm=1024):
    M, K = a.shape
    N = b.shape[1]
    tn = _tile(N)

    def body(a_ref, b_ref, o_ref):
        o_ref[...] = _dot(a_ref[...], b_ref[...], NN)

    return _call(body, name, (M // tm, N // tn),
                 [pl.BlockSpec((tm, K), lambda i, j: (i, 0)), pl.BlockSpec((K, tn), lambda i, j: (0, j))],
                 pl.BlockSpec((tm, tn), lambda i, j: (i, j)), jax.ShapeDtypeStruct((M, N), f32),
                 sem=("parallel", "parallel"))(a, b)


def mm_nt(a, b, name, acc=None, tm=1024):
    M, K = a.shape
    N = b.shape[0]
    tk = _tile(K)
    has_acc = acc is not None

    def body(*refs):
        a_ref, b_ref = refs[0], refs[1]
        o_ref = refs[-1]
        k = pl.program_id(1)
        p = _dot(a_ref[...], b_ref[...], NT)

        @pl.when(k == 0)
        def _():
            o_ref[...] = p + refs[2][...] if has_acc else p

        @pl.when(k > 0)
        def _():
            o_ref[...] += p

    specs = [pl.BlockSpec((tm, tk), lambda i, k: (i, k)), pl.BlockSpec((N, tk), lambda i, k: (0, k))]
    args = [a, b]
    if has_acc:
        specs.append(pl.BlockSpec((tm, N), lambda i, k: (i, 0)))
        args.append(acc)
    return _call(body, name, (M // tm, K // tk), specs, pl.BlockSpec((tm, N), lambda i, k: (i, 0)),
                 jax.ShapeDtypeStruct((M, N), f32), sem=("parallel", "arbitrary"))(*args)


def mm_tn(a, b, name, tk=1024):
    K, M = a.shape
    N = b.shape[1]
    tn = _tile(N)

    def body(a_ref, b_ref, o_ref):
        k = pl.program_id(1)
        p = _dot(a_ref[...], b_ref[...], TN)

        @pl.when(k == 0)
        def _():
            o_ref[...] = p

        @pl.when(k > 0)
        def _():
            o_ref[...] += p

    return _call(body, name, (N // tn, K // tk),
                 [pl.BlockSpec((tk, M), lambda j, k: (k, 0)), pl.BlockSpec((tk, tn), lambda j, k: (k, j))],
                 pl.BlockSpec((M, tn), lambda j, k: (0, j)), jax.ShapeDtypeStruct((M, N), f32),
                 sem=("parallel", "arbitrary"))(a, b)


def rms_fwd(x, w, name, tm=512):
    S = x.shape[0]

    def body(x_ref, w_ref, o_ref):
        o_ref[...] = _rms(x_ref[...], w_ref[...]).astype(bf16)

    return _call(body, name, (S // tm,),
                 [pl.BlockSpec((tm, D_MODEL), lambda i: (i, 0)), pl.BlockSpec((1, D_MODEL), lambda i: (0, 0))],
                 pl.BlockSpec((tm, D_MODEL), lambda i: (i, 0)), jax.ShapeDtypeStruct((S, D_MODEL), bf16),
                 sem=("parallel",))(x, w)


def rms_bwd(x, w, dh, dres, name, tm=512):
    S = x.shape[0]

    def body(x_ref, w_ref, dh_ref, dr_ref, dx_ref, dw_ref):
        _, vjp = jax.vjp(_rms, x_ref[...], w_ref[...])
        dx, dw = vjp(dh_ref[...])
        dx_ref[...] = dx + dr_ref[...]

        @pl.when(pl.program_id(0) == 0)
        def _():
            dw_ref[...] = dw

        @pl.when(pl.program_id(0) > 0)
        def _():
            dw_ref[...] += dw

    row = pl.BlockSpec((tm, D_MODEL), lambda i: (i, 0))
    vec = pl.BlockSpec((1, D_MODEL), lambda i: (0, 0))
    return _call(body, name, (S // tm,), [row, vec, row, row], [row, vec],
                 [jax.ShapeDtypeStruct((S, D_MODEL), f32), jax.ShapeDtypeStruct((1, D_MODEL), f32)],
                 sem=("arbitrary",))(x, w, dh, dres)


def loss_and_grad(y, target, name, tm=512):
    S = y.shape[0]

    def body(y_ref, t_ref, dy_ref, l_ref):
        diff = y_ref[...] - t_ref[...]
        dy_ref[...] = diff * (1.0 / D_MODEL)
        part = jnp.full((8, LANES), 0.5 / D_MODEL * jnp.sum(diff * diff), f32)

        @pl.when(pl.program_id(0) == 0)
        def _():
            l_ref[...] = part

        @pl.when(pl.program_id(0) > 0)
        def _():
            l_ref[...] += part

    row = pl.BlockSpec((tm, D_MODEL), lambda i: (i, 0))
    return _call(body, name, (S // tm,), [row, row], [row, pl.BlockSpec((8, LANES), lambda i: (0, 0))],
                 [jax.ShapeDtypeStruct((S, D_MODEL), f32), jax.ShapeDtypeStruct((8, LANES), f32)],
                 sem=("arbitrary",))(y, target)


def _s5_discretize(a_re, a_im, log_step, b_re, b_im, c_re, c_im):
    step = jnp.exp(log_step)[:, None]
    mag = jnp.exp(a_re * step)
    ang = a_im * step
    lam_re, lam_im = mag * jnp.cos(ang), mag * jnp.sin(ang)
    num_re, num_im = lam_re - 1.0, lam_im
    den = a_re * a_re + a_im * a_im
    f_re = (num_re * a_re + num_im * a_im) / den
    f_im = (num_im * a_re - num_re * a_im) / den
    bb_re = f_re[..., None] * b_re - f_im[..., None] * b_im
    bb_im = f_re[..., None] * b_im + f_im[..., None] * b_re
    eye = jnp.eye(8, dtype=f32)

    def block_in(bb):
        t = bb.transpose(0, 2, 1).reshape(4, 8, 16, 1, 64)
        return (t * eye[None, :, None, :, None]).reshape(4, 128, 512)

    def block_out(c):
        t = c.transpose(0, 2, 1).reshape(4, 8, 64, 1, 16)
        return (t * eye[None, :, None, :, None]).reshape(4, 512, 128)

    return (lam_re.reshape(1, S5_STATES), lam_im.reshape(1, S5_STATES), block_in(bb_re), block_in(bb_im),
            block_out(c_re), block_out(c_im))


def _lam_powers(lam_re, lam_im):
    rows_re, rows_im = [lam_re], [lam_im]
    for _ in range(7):
        pr, pi = rows_re[-1], rows_im[-1]
        rows_re.append(pr * lam_re - pi * lam_im)
        rows_im.append(pr * lam_im + pi * lam_re)
    return jnp.concatenate(rows_re, 0), jnp.concatenate(rows_im, 0)


def s5_fwd(u, pw_re, pw_im, w_re, w_im, c_re, c_im, dvec, name):
    S = u.shape[0]
    R, NS = S5_ROWS, S5_STATES
    nb = R // 8

    def body(u_ref, pwr_ref, pwi_ref, wre_ref, wim_ref, cre_ref, cim_ref, d_ref, y_ref, hr_ref, hi_ref,
             car_re, car_im, cin_re, cin_im, up, yp):
        @pl.when(pl.program_id(0) == 0)
        def _():
            car_re[...] = jnp.zeros_like(car_re)
            car_im[...] = jnp.zeros_like(car_im)

        slab = lambda r: pl.ds(r * nb, nb)
        for r in range(8):
            up[slab(r), :] = u_ref[:, r, :]
        u = up[...]
        for j in range(4):
            uj = u[:, 128 * j:128 * (j + 1)]
            hr_ref[:, 512 * j:512 * (j + 1)] = _dot(uj, wre_ref[j], NN)
            hi_ref[:, 512 * j:512 * (j + 1)] = _dot(uj, wim_ref[j], NN)
        lr, li = pwr_ref[0:1, :], pwi_ref[0:1, :]
        for r in range(1, 8):
            pr, pi = hr_ref[slab(r - 1), :], hi_ref[slab(r - 1), :]
            hr_ref[slab(r), :] = lr * pr - li * pi + hr_ref[slab(r), :]
            hi_ref[slab(r), :] = lr * pi + li * pr + hi_ref[slab(r), :]
        l8r, l8i = pwr_ref[7:8, :], pwi_ref[7:8, :]

        def across(c, carry):
            gr, gi = carry
            cin_re[pl.ds(c, 1), :] = gr
            cin_im[pl.ds(c, 1), :] = gi
            er, ei = hr_ref[pl.ds(7 * nb + c, 1), :], hi_ref[pl.ds(7 * nb + c, 1), :]
            return l8r * gr - l8i * gi + er, l8r * gi + l8i * gr + ei

        gr, gi = lax.fori_loop(0, nb, across, (car_re[...], car_im[...]))
        car_re[...] = gr
        car_im[...] = gi
        cr, ci = cin_re[...], cin_im[...]
        for r in range(8):
            pr, pi = pwr_ref[r:r + 1, :], pwi_ref[r:r + 1, :]
            hr_ref[slab(r), :] = hr_ref[slab(r), :] + pr * cr - pi * ci
            hi_ref[slab(r), :] = hi_ref[slab(r), :] + pr * ci + pi * cr
        for j in range(4):
            sl = slice(512 * j, 512 * (j + 1))
            cs = slice(128 * j, 128 * (j + 1))
            yp[:, cs] = (_dot(hr_ref[:, sl], cre_ref[j], NN) - _dot(hi_ref[:, sl], cim_ref[j], NN)
                         + d_ref[:, cs] * u[:, cs])
        for r in range(8):
            y_ref[:, r, :] = yp[slab(r), :]

    full = lambda shape: pl.BlockSpec(shape, lambda i: (0,) * len(shape))
    hspec = pl.BlockSpec((R, NS), lambda i: (i, 0))
    uspec = pl.BlockSpec((nb, 8, 512), lambda i: (i, 0, 0))
    y, h_re, h_im = _call(
        body, name, (S // R,),
        [uspec, full((8, NS)), full((8, NS)), full((4, 128, 512)),
         full((4, 128, 512)), full((4, 512, 128)), full((4, 512, 128)), full((1, 512))],
        [uspec, hspec, hspec],
        [jax.ShapeDtypeStruct((S // 8, 8, 512), f32), jax.ShapeDtypeStruct((S, NS), f32),
         jax.ShapeDtypeStruct((S, NS), f32)],
        scratch=[pltpu.VMEM((1, NS), f32), pltpu.VMEM((1, NS), f32), pltpu.VMEM((nb, NS), f32),
                 pltpu.VMEM((nb, NS), f32), pltpu.VMEM((R, 512), f32), pltpu.VMEM((R, 512), f32)],
        sem=("arbitrary",))(u.reshape(S // 8, 8, 512), pw_re, pw_im, w_re.astype(bf16), w_im.astype(bf16),
                            c_re.astype(bf16), c_im.astype(bf16), dvec)
    return y.reshape(S, 512), h_re, h_im


def s5_bwd(dy, u, h_re, h_im, pw_re, pw_im, w_re, w_im, c_re, c_im, dvec, name):
    S = u.shape[0]
    R, NS = S5_ROWS, S5_STATES
    nb = R // 8
    nchunk = S // R

    def body(dy_ref, u_ref, hr_ref, hi_ref, hpr_ref, hpi_ref, pwr_ref, pwi_ref, wre_ref, wim_ref, cre_ref, cim_ref,
             d_ref, du_ref, dwre_ref, dwim_ref, dcre_ref, dcim_ref, dlr_ref, dli_ref, dd_ref,
             ar, ai, car_re, car_im, cin_re, cin_im, up, dyp, dup):
        i = pl.program_id(0)

        @pl.when(i == 0)
        def _():
            for ref in (car_re, car_im, dwre_ref, dwim_ref, dcre_ref, dcim_ref, dlr_ref, dli_ref, dd_ref):
                ref[...] = jnp.zeros_like(ref)

        slab = lambda r: pl.ds(r * nb, nb)
        for r in range(8):
            up[slab(r), :] = u_ref[:, r, :]
            dyp[slab(r), :] = dy_ref[:, r, :]
        dy = dyp[...]
        u = up[...]
        for j in range(4):
            dyj = dy[:, 128 * j:128 * (j + 1)]
            ar[:, 512 * j:512 * (j + 1)] = _dot(dyj, cre_ref[j], NT)
            ai[:, 512 * j:512 * (j + 1)] = -_dot(dyj, cim_ref[j], NT)
        lr, li = pwr_ref[0:1, :], pwi_ref[0:1, :]
        for r in range(6, -1, -1):
            nr, ni = ar[slab(r + 1), :], ai[slab(r + 1), :]
            ar[slab(r), :] = lr * nr + li * ni + ar[slab(r), :]
            ai[slab(r), :] = lr * ni - li * nr + ai[slab(r), :]
        l8r, l8i = pwr_ref[7:8, :], pwi_ref[7:8, :]

        def across(k, carry):
            c = nb - 1 - k
            gr, gi = carry
            cin_re[pl.ds(c, 1), :] = gr
            cin_im[pl.ds(c, 1), :] = gi
            er, ei = ar[pl.ds(c, 1), :], ai[pl.ds(c, 1), :]
            return l8r * gr + l8i * gi + er, l8r * gi - l8i * gr + ei

        gr, gi = lax.fori_loop(0, nb, across, (car_re[...], car_im[...]))
        car_re[...] = gr
        car_im[...] = gi
        cr, ci = cin_re[...], cin_im[...]
        for r in range(8):
            pr, pi = pwr_ref[7 - r:8 - r, :], pwi_ref[7 - r:8 - r, :]
            ar[slab(r), :] = ar[slab(r), :] + pr * cr + pi * ci
            ai[slab(r), :] = ai[slab(r), :] + pr * ci - pi * cr

        acc_r = jnp.zeros((1, NS), f32)
        acc_i = jnp.zeros((1, NS), f32)
        has_prev = (i < nchunk - 1).astype(f32)
        top = lax.broadcasted_iota(jnp.int32, (nb, NS), 0) == 0
        for r in range(8):
            if r == 0:
                xr = jnp.where(top, hpr_ref[7:8, :] * has_prev, pltpu.roll(hr_ref[slab(7), :], 1, 0))
                xi = jnp.where(top, hpi_ref[7:8, :] * has_prev, pltpu.roll(hi_ref[slab(7), :], 1, 0))
            else:
                xr, xi = hr_ref[slab(r - 1), :], hi_ref[slab(r - 1), :]
            br, bi = ar[slab(r), :], ai[slab(r), :]
            acc_r += jnp.sum(br * xr + bi * xi, axis=0, keepdims=True)
            acc_i += jnp.sum(bi * xr - br * xi, axis=0, keepdims=True)
        dlr_ref[...] += acc_r
        dli_ref[...] += acc_i
        dd_ref[...] += jnp.sum(dy * u, axis=0, keepdims=True)

        for j in range(4):
            sl = slice(512 * j, 512 * (j + 1))
            cs = slice(128 * j, 128 * (j + 1))
            arj, aij = ar[:, sl], ai[:, sl]
            uj, dyj = u[:, cs], dy[:, cs]
            dup[:, cs] = _dot(arj, wre_ref[j], NT) + _dot(aij, wim_ref[j], NT) + d_ref[:, cs] * dyj
            dwre_ref[j] += _dot(uj, arj, TN)
            dwim_ref[j] += _dot(uj, aij, TN)
            dcre_ref[j] += _dot(hr_ref[:, sl], dyj, TN)
            dcim_ref[j] -= _dot(hi_ref[:, sl], dyj, TN)
        for r in range(8):
            du_ref[:, r, :] = dup[slab(r), :]

    rev = lambda i: nchunk - 1 - i
    full = lambda shape: pl.BlockSpec(shape, lambda i: (0,) * len(shape))
    row = pl.BlockSpec((nb, 8, 512), lambda i: (rev(i), 0, 0))
    hspec = pl.BlockSpec((R, NS), lambda i: (rev(i), 0))
    hprev = pl.BlockSpec((8, NS), lambda i: (jnp.maximum(rev(i) * nb - 1, 0), 0))
    outs = _call(
        body, name, (nchunk,),
        [row, row, hspec, hspec, hprev, hprev, full((8, NS)), full((8, NS)), full((4, 128, 512)), full((4, 128, 512)),
         full((4, 512, 128)), full((4, 512, 128)), full((1, 512))],
        [row, full((4, 128, 512)), full((4, 128, 512)), full((4, 512, 128)), full((4, 512, 128)),
         full((1, NS)), full((1, NS)), full((1, 512))],
        [jax.ShapeDtypeStruct((S // 8, 8, 512), f32), jax.ShapeDtypeStruct((4, 128, 512), f32),
         jax.ShapeDtypeStruct((4, 128, 512), f32), jax.ShapeDtypeStruct((4, 512, 128), f32),
         jax.ShapeDtypeStruct((4, 512, 128), f32), jax.ShapeDtypeStruct((1, NS), f32),
         jax.ShapeDtypeStruct((1, NS), f32), jax.ShapeDtypeStruct((1, 512), f32)],
        scratch=[pltpu.VMEM((R, NS), f32), pltpu.VMEM((R, NS), f32), pltpu.VMEM((1, NS), f32),
                 pltpu.VMEM((1, NS), f32), pltpu.VMEM((nb, NS), f32), pltpu.VMEM((nb, NS), f32),
                 pltpu.VMEM((R, 512), f32), pltpu.VMEM((R, 512), f32), pltpu.VMEM((R, 512), f32)],
        sem=("arbitrary",))(dy.reshape(S // 8, 8, 512), u.reshape(S // 8, 8, 512), h_re, h_im, h_re, h_im, pw_re,
                            pw_im, w_re.astype(bf16), w_im.astype(bf16), c_re.astype(bf16), c_im.astype(bf16), dvec)
    return (outs[0].reshape(S, 512),) + tuple(outs[1:])


def _rows(start, n, d):
    return pl.ds(pl.multiple_of(start, ATT_BLOCK), n) if d == 1 else pl.ds(start, n, stride=d)


def _att_block(q, k, v, qw, kw, has_prev):
    lane = lax.broadcasted_iota(jnp.int32, (1, LANES), 1)
    hm = [(lane < 64).astype(f32), (lane >= 64).astype(f32)]

    def head_norm(x, w):
        x2 = x * x
        sc = sum(hm[h] * lax.rsqrt(jnp.sum(x2 * hm[h], axis=-1, keepdims=True) * (1.0 / 64) + RMS_EPS)
                 for h in range(2))
        return x * sc * w

    qn, kn = head_norm(q, qw), head_norm(k, kw)
    qi = lax.broadcasted_iota(jnp.int32, (ATT_BLOCK, 2 * ATT_BLOCK), 0) + ATT_BLOCK
    kj = lax.broadcasted_iota(jnp.int32, (ATT_BLOCK, 2 * ATT_BLOCK), 1)
    mask = (qi - kj >= 0) & (qi - kj <= ATT_BLOCK) & (has_prev | (kj >= ATT_BLOCK))
    o = jnp.zeros((ATT_BLOCK, LANES), f32)
    lse = jnp.zeros((ATT_BLOCK, LANES), f32)
    for h in range(2):
        s = _bdot(qn * hm[h], kn, NT) * 0.125
        s = jnp.where(mask, s, -jnp.inf)
        m = jnp.max(s, axis=-1, keepdims=True)
        p = jnp.exp(s - m)
        l = jnp.sum(p, axis=-1, keepdims=True)
        o = o + hm[h] * _bdot(p / l, v, NN)
        lse = lse + hm[h] * (m + jnp.log(l))
    return o, lse


def att_fwd(p_att, qw, kw, d, name):
    S = p_att.shape[0]
    SEG = ATT_SEG
    nblk = SEG // ATT_BLOCK

    def body(p_ref, qw_ref, kw_ref, o_ref, l_ref, q_s, k_ext, v_ext, o_s, l_s):
        seg = pl.program_id(1)

        @pl.when(seg == 0)
        def _():
            k_ext[SEG:, :] = jnp.zeros((SEG, LANES), f32)
            v_ext[SEG:, :] = jnp.zeros((SEG, LANES), f32)

        k_ext[:SEG, :] = k_ext[SEG:, :]
        v_ext[:SEG, :] = v_ext[SEG:, :]
        q_s[...] = p_ref[:, 0:128]
        k_ext[SEG:, :] = p_ref[:, 128:256]
        v_ext[SEG:, :] = p_ref[:, 256:384]
        qw_v, kw_v = qw_ref[...], kw_ref[...]

        def blk(b, carry):
            j, r = b // d, b % d
            qs = j * (ATT_BLOCK * d) + r
            ks = SEG + qs - ATT_BLOCK * d
            o, lse = _att_block(q_s[_rows(qs, ATT_BLOCK, d), :], k_ext[_rows(ks, 2 * ATT_BLOCK, d), :],
                                v_ext[_rows(ks, 2 * ATT_BLOCK, d), :], qw_v, kw_v, (seg > 0) | (j > 0))
            o_s[_rows(qs, ATT_BLOCK, d), :] = o
            l_s[_rows(qs, ATT_BLOCK, d), :] = lse
            return carry

        lax.fori_loop(0, nblk, blk, 0)
        o_ref[...] = o_s[...]
        l_ref[...] = l_s[...]

    vec = pl.BlockSpec((1, LANES), lambda hh, s: (0, 0))
    out = pl.BlockSpec((SEG, LANES), lambda hh, s: (s, hh))
    return _call(body, name, (2, S // SEG), [pl.BlockSpec((SEG, 384), lambda hh, s: (s, hh)), vec, vec],
                 [out, out], [jax.ShapeDtypeStruct((S, 256), f32), jax.ShapeDtypeStruct((S, 256), f32)],
                 scratch=[pltpu.VMEM((SEG, LANES), f32), pltpu.VMEM((2 * SEG, LANES), f32),
                          pltpu.VMEM((2 * SEG, LANES), f32), pltpu.VMEM((SEG, LANES), f32),
                          pltpu.VMEM((SEG, LANES), f32)],
                 sem=("arbitrary", "arbitrary"))(p_att, qw, kw)


def att_bwd(p_att, do, dlse, qw, kw, d, name):
    S = p_att.shape[0]
    SEG = ATT_SEG
    nseg = S // SEG
    nblk = SEG // ATT_BLOCK

    def body(p_ref, pp_ref, do_ref, dl_ref, qw_ref, kw_ref, dp_ref, dqw_ref, dkw_ref,
             q_s, k_ext, v_ext, dq_s, dk_ext, dv_ext):
        hh, i = pl.program_id(0), pl.program_id(1)
        seg = nseg - 1 - i

        @pl.when(i == 0)
        def _():
            dk_ext[...] = jnp.zeros_like(dk_ext)
            dv_ext[...] = jnp.zeros_like(dv_ext)

        @pl.when((i == 0) & (hh == 0))
        def _():
            dqw_ref[...] = jnp.zeros_like(dqw_ref)
            dkw_ref[...] = jnp.zeros_like(dkw_ref)

        dk_ext[SEG:, :] = dk_ext[:SEG, :]
        dv_ext[SEG:, :] = dv_ext[:SEG, :]
        dk_ext[:SEG, :] = jnp.zeros((SEG, LANES), f32)
        dv_ext[:SEG, :] = jnp.zeros((SEG, LANES), f32)
        q_s[...] = p_ref[:, 0:128]
        k_ext[SEG:, :] = p_ref[:, 128:256]
        v_ext[SEG:, :] = p_ref[:, 256:384]
        k_ext[:SEG, :] = pp_ref[:, 128:256]
        v_ext[:SEG, :] = pp_ref[:, 256:384]
        qw_v, kw_v = qw_ref[...], kw_ref[...]

        def blk(b, carry):
            dqw, dkw = carry
            j, r = b // d, b % d
            qs = j * (ATT_BLOCK * d) + r
            ks = SEG + qs - ATT_BLOCK * d
            has_prev = (seg > 0) | (j > 0)
            qrows, krows = _rows(qs, ATT_BLOCK, d), _rows(ks, 2 * ATT_BLOCK, d)
            _, vjp = jax.vjp(lambda q, k, v, a, b_: _att_block(q, k, v, a, b_, has_prev),
                             q_s[qrows, :], k_ext[krows, :], v_ext[krows, :], qw_v, kw_v)
            dq, dk, dv, dqw_b, dkw_b = vjp((do_ref[qrows, :], dl_ref[qrows, :]))
            dq_s[qrows, :] = dq
            dk_ext[krows, :] = dk_ext[krows, :] + dk
            dv_ext[krows, :] = dv_ext[krows, :] + dv
            return dqw + dqw_b, dkw + dkw_b

        zero = jnp.zeros((1, LANES), f32)
        dqw, dkw = lax.fori_loop(0, nblk, blk, (zero, zero))
        dqw_ref[...] += dqw
        dkw_ref[...] += dkw
        dp_ref[:, 0:128] = dq_s[...]
        dp_ref[:, 128:256] = dk_ext[SEG:, :]
        dp_ref[:, 256:384] = dv_ext[SEG:, :]

    rev = lambda i: nseg - 1 - i
    vec = pl.BlockSpec((1, LANES), lambda hh, i: (0, 0))
    cur = pl.BlockSpec((SEG, 384), lambda hh, i: (rev(i), hh))
    prev = pl.BlockSpec((SEG, 384), lambda hh, i: (jnp.maximum(rev(i) - 1, 0), hh))
    col = pl.BlockSpec((SEG, LANES), lambda hh, i: (rev(i), hh))
    big = pltpu.VMEM((2 * SEG, LANES), f32)
    one = pltpu.VMEM((SEG, LANES), f32)
    return _call(body, name, (2, nseg), [cur, prev, col, col, vec, vec], [cur, vec, vec],
                 [jax.ShapeDtypeStruct((S, 768), f32), jax.ShapeDtypeStruct((1, LANES), f32),
                  jax.ShapeDtypeStruct((1, LANES), f32)],
                 scratch=[one, big, big, one, big, big],
                 sem=("arbitrary", "arbitrary"))(p_att, p_att, do, dlse, qw, kw)


def conv_fwd(p_ssd, conv_w, conv_b, name):
    S = p_ssd.shape[0]
    tm, C = CONV_ROWS, SSD_XBC

    def body(x_ref, xp_ref, w_ref, b_ref, o_ref, ext):
        first = (pl.program_id(0) == 0)
        ext[0:8, :] = jnp.where(first, 0.0, xp_ref[:, 0:C])
        ext[8:, :] = x_ref[:, 0:C]
        acc = b_ref[...] + w_ref[3:4, :] * ext[pl.ds(8, tm), :]
        for k in range(1, 4):
            acc = acc + w_ref[3 - k:4 - k, :] * ext[pl.ds(8 - k, tm), :]
        o_ref[...] = jax.nn.silu(acc)

    return _call(body, name, (S // tm,),
                 [pl.BlockSpec((tm, 1536), lambda i: (i, 0)),
                  pl.BlockSpec((8, 1536), lambda i: (jnp.maximum(i * (tm // 8) - 1, 0), 0)),
                  pl.BlockSpec((4, C), lambda i: (0, 0)), pl.BlockSpec((1, C), lambda i: (0, 0))],
                 pl.BlockSpec((tm, C), lambda i: (i, 0)), jax.ShapeDtypeStruct((S, C), f32),
                 scratch=[pltpu.VMEM((tm + 8, C), f32)], sem=("parallel",))(p_ssd, p_ssd, conv_w, conv_b)


def conv_bwd(p_ssd, dact, ddt, conv_w, conv_b, name):
    S = p_ssd.shape[0]
    tm, C = CONV_ROWS, SSD_XBC
    nblk = S // tm

    def body(x_ref, xp_ref, xn_ref, da_ref, dan_ref, ddt_ref, w_ref, b_ref, dp_ref, dw_ref, db_ref, ext, dpre):
        i = pl.program_id(0)
        ext[0:8, :] = jnp.where(i == 0, 0.0, xp_ref[:, 0:C])
        ext[8:tm + 8, :] = x_ref[:, 0:C]
        ext[tm + 8:, :] = xn_ref[:, 0:C]
        pre = b_ref[...] + w_ref[3:4, :] * ext[pl.ds(8, tm + 8), :]
        for k in range(1, 4):
            pre = pre + w_ref[3 - k:4 - k, :] * ext[pl.ds(8 - k, tm + 8), :]
        sg = jax.nn.sigmoid(pre)
        dsilu = sg * (1.0 + pre * (1.0 - sg))
        dpre[0:tm, :] = da_ref[...] * dsilu[0:tm, :]
        dpre[tm:, :] = jnp.where(i == nblk - 1, 0.0, dan_ref[...] * dsilu[tm:, :])
        dx = w_ref[3:4, :] * dpre[pl.ds(0, tm), :]
        for k in range(1, 4):
            dx = dx + w_ref[3 - k:4 - k, :] * dpre[pl.ds(k, tm), :]
        dp_ref[:, 0:C] = dx
        dp_ref[:, C:C + 128] = ddt_ref[...]
        dp_ref[:, C + 128:] = jnp.zeros((tm, 128), f32)
        dcur = dpre[pl.ds(0, tm), :]
        dws = [jnp.sum(dcur * ext[pl.ds(8 - (3 - j), tm), :], axis=0, keepdims=True) for j in range(4)]
        dbs = jnp.sum(dcur, axis=0, keepdims=True)

        @pl.when(i == 0)
        def _():
            dw_ref[...] = jnp.zeros_like(dw_ref)
            db_ref[...] = jnp.zeros_like(db_ref)

        for j in range(4):
            dw_ref[j:j + 1, :] += dws[j]
        db_ref[...] += dbs

    t8 = tm // 8
    return _call(body, name, (nblk,),
                 [pl.BlockSpec((tm, 1536), lambda i: (i, 0)),
                  pl.BlockSpec((8, 1536), lambda i: (jnp.maximum(i * t8 - 1, 0), 0)),
                  pl.BlockSpec((8, 1536), lambda i: (jnp.minimum((i + 1) * t8, S // 8 - 1), 0)),
                  pl.BlockSpec((tm, C), lambda i: (i, 0)),
                  pl.BlockSpec((8, C), lambda i: (jnp.minimum((i + 1) * t8, S // 8 - 1), 0)),
                  pl.BlockSpec((tm, 128), lambda i: (i, 0)),
                  pl.BlockSpec((4, C), lambda i: (0, 0)), pl.BlockSpec((1, C), lambda i: (0, 0))],
                 [pl.BlockSpec((tm, 1536), lambda i: (i, 0)), pl.BlockSpec((4, C), lambda i: (0, 0)),
                  pl.BlockSpec((1, C), lambda i: (0, 0))],
                 [jax.ShapeDtypeStruct((S, 1536), f32), jax.ShapeDtypeStruct((4, C), f32),
                  jax.ShapeDtypeStruct((1, C), f32)],
                 scratch=[pltpu.VMEM((tm + 16, C), f32), pltpu.VMEM((tm + 8, C), f32)],
                 sem=("arbitrary",))(p_ssd, p_ssd, p_ssd, dact, dact, ddt, conv_w, conv_b)


def _ssd_chunk(xbc, dtr, state, dt_bias, a_log, d_full):
    T = SSD_CHUNK
    r_i = lax.broadcasted_iota(jnp.int32, (T, T), 0)
    c_i = lax.broadcasted_iota(jnp.int32, (T, T), 1)
    tril = c_i <= r_i
    tri = tril.astype(bf16)
    e_rows = lax.broadcasted_iota(jnp.int32, (T, SSD_WIDTH), 0)
    e_cols = lax.broadcasted_iota(jnp.int32, (T, SSD_WIDTH), 1)
    expand = (e_cols // 64 == e_rows).astype(bf16)
    w_rows = lax.broadcasted_iota(jnp.int32, (T, 12 * T), 0)
    w_cols = lax.broadcasted_iota(jnp.int32, (T, 12 * T), 1)
    expand_wide = (w_cols // T == w_rows).astype(bf16)
    lane = lax.broadcasted_iota(jnp.int32, (1, LANES), 1)
    hm = [(lane < 64).astype(f32), (lane >= 64).astype(f32)]

    xs, bm, cm = xbc[:, :768], xbc[:, 768:1024], xbc[:, 1024:1280]
    dt = _softplus(dtr + dt_bias)
    a_dt = dt * (-jnp.exp(a_log))
    a_cs = _xdot_l(tri, a_dt)
    dt_full = _xdot_r(dt, expand)
    acs_full = _xdot_r(a_cs, expand)
    acs_wide = _xdot_r(a_cs, expand_wide)
    last = lax.broadcasted_iota(jnp.int32, (T, SSD_WIDTH), 0) == T - 1
    tot_full = jnp.sum(jnp.where(last, acs_full, 0.0), axis=0, keepdims=True)
    xdt = xs * dt_full
    xw = xdt * jnp.exp(tot_full - acs_full)
    eacs = jnp.exp(acs_full)
    st_parts, off_parts, diag_parts = [], [], []
    for g in range(2):
        bg, cg = bm[:, 128 * g:128 * (g + 1)], cm[:, 128 * g:128 * (g + 1)]
        cols = slice(384 * g, 384 * (g + 1))
        st_parts.append(_bdot(bg, xw[:, cols], TN))
        off_parts.append(_bdot(cg, state[:, cols], NN))
        cb = _bdot(cg, bg, NT)
        for pp in range(3 * g, 3 * g + 3):
            xp = xdt[:, 128 * pp:128 * (pp + 1)]
            acc = jnp.zeros((T, LANES), f32)
            for hh in range(2):
                a_col = acs_wide[:, T * (2 * pp + hh):T * (2 * pp + hh + 1)]
                decay = jnp.where(tril, jnp.exp(jnp.minimum(a_col - a_col.T, 0.0)), 0.0)
                acc = acc + _bdot(cb * decay, xp * hm[hh], NN)
            diag_parts.append(acc)
    new_state = state * jnp.exp(tot_full) + jnp.concatenate(st_parts, axis=1)
    y = jnp.concatenate(diag_parts, axis=1) + jnp.concatenate(off_parts, axis=1) * eacs + xs * d_full
    return y, new_state


def ssd_fwd(xact, p_ssd, dt_bias, a_log, d_full, name):
    S = xact.shape[0]
    T = SSD_CHUNK

    def body(x_ref, p_ref, b_ref, a_ref, d_ref, y_ref, s_ref, state):
        @pl.when(pl.program_id(0) == 0)
        def _():
            state[...] = jnp.zeros_like(state)

        st = state[...]
        s_ref[0] = st
        y, new = _ssd_chunk(x_ref[...], p_ref[...], st, b_ref[...], a_ref[...], d_ref[...])
        y_ref[...] = y
        state[...] = new

    vec = lambda n: pl.BlockSpec((1, n), lambda i: (0, 0))
    return _call(body, name, (S // T,),
                 [pl.BlockSpec((T, SSD_XBC), lambda i: (i, 0)), pl.BlockSpec((T, 128), lambda i: (i, 10)),
                  vec(128), vec(128), vec(768)],
                 [pl.BlockSpec((T, 768), lambda i: (i, 0)), pl.BlockSpec((1, T, 768), lambda i: (i, 0, 0))],
                 [jax.ShapeDtypeStruct((S, 768), f32), jax.ShapeDtypeStruct((S // T, T, 768), f32)],
                 scratch=[pltpu.VMEM((T, 768), f32)], sem=("arbitrary",))(xact, p_ssd, dt_bias, a_log, d_full)


def ssd_bwd(xact, p_ssd, states, dy, dt_bias, a_log, d_full, name):
    S = xact.shape[0]
    T = SSD_CHUNK
    nc = S // T

    def body(x_ref, p_ref, s_ref, dy_ref, b_ref, a_ref, d_ref, dx_ref, ddt_ref, db_ref, da_ref, dd_ref, dstate):
        i = pl.program_id(0)

        @pl.when(i == 0)
        def _():
            for ref in (dstate, db_ref, da_ref, dd_ref):
                ref[...] = jnp.zeros_like(ref)

        _, vjp = jax.vjp(_ssd_chunk, x_ref[...], p_ref[...], s_ref[0], b_ref[...], a_ref[...], d_ref[...])
        dx, ddt, dst, db, da, dd = vjp((dy_ref[...], dstate[...]))
        dx_ref[...] = dx
        ddt_ref[...] = ddt
        dstate[...] = dst
        db_ref[...] += db
        da_ref[...] += da
        dd_ref[...] += dd

    rev = lambda i: nc - 1 - i
    vec = lambda n: pl.BlockSpec((1, n), lambda i: (0, 0))
    return _call(body, name, (nc,),
                 [pl.BlockSpec((T, SSD_XBC), lambda i: (rev(i), 0)), pl.BlockSpec((T, 128), lambda i: (rev(i), 10)),
                  pl.BlockSpec((1, T, 768), lambda i: (rev(i), 0, 0)), pl.BlockSpec((T, 768), lambda i: (rev(i), 0)),
                  vec(128), vec(128), vec(768)],
                 [pl.BlockSpec((T, SSD_XBC), lambda i: (rev(i), 0)), pl.BlockSpec((T, 128), lambda i: (rev(i), 0)),
                  vec(128), vec(128), vec(768)],
                 [jax.ShapeDtypeStruct((S, SSD_XBC), f32), jax.ShapeDtypeStruct((S, 128), f32),
                  jax.ShapeDtypeStruct((1, 128), f32), jax.ShapeDtypeStruct((1, 128), f32),
                  jax.ShapeDtypeStruct((1, 768), f32)],
                 scratch=[pltpu.VMEM((T, 768), f32)],
                 sem=("arbitrary",))(xact, p_ssd, states, dy, dt_bias, a_log, d_full)


def _tail_fn(ys5, pt, o0, o1, o2, l0, l1, l2, yssd, glu_b, nw, pr_glu, pr_a, pr_b, pr_c, x, weights):
    glu_w, pa, pb, pc, wo = weights
    gates = jax.nn.sigmoid(pt[:, :3072])
    za, zb, zc = pt[:, 3072:3584], pt[:, 3584:3840], pt[:, 3840:4608]
    g = jax.nn.gelu(ys5)
    ya = g * jax.nn.sigmoid(_cdot(g, glu_w, NN) + glu_b + pr_glu) * jax.nn.silu(za)
    m = jnp.maximum(jnp.maximum(l0, l1), l2)
    e0, e1, e2 = jnp.exp(l0 - m), jnp.exp(l1 - m), jnp.exp(l2 - m)
    yb = (e0 * o0 + e1 * o1 + e2 * o2) / (e0 + e1 + e2) * jax.nn.silu(zb)
    yc = _rms(yssd * jax.nn.silu(zc), nw)
    merged = (gates[:, :1024] * (_cdot(ya, pa, NN) + pr_a) + gates[:, 1024:2048] * (_cdot(yb, pb, NN) + pr_b)
              + gates[:, 2048:] * (_cdot(yc, pc, NN) + pr_c))
    out = x + _cdot(merged, wo, NN)
    return out, (g, ya, yb, yc, merged)


def _tail_specs(tm):
    row = lambda n: pl.BlockSpec((tm, n), lambda i: (i, 0))
    full = lambda a, b: pl.BlockSpec((a, b), lambda i: (0, 0))
    acts = [row(512), row(4608)] + [row(256)] * 6 + [row(768), row(D_MODEL)]
    consts = [full(1, 512), full(1, 768), full(512, 512), full(512, D_MODEL), full(256, D_MODEL),
              full(768, D_MODEL), full(D_MODEL, D_MODEL)]
    return row, full, acts, consts


def tail_fwd(ys5, pt, os_, ls_, yssd, x, glu_b, nw, weights, name):
    S = x.shape[0]
    tm = TAIL_ROWS
    row, full, acts, consts = _tail_specs(tm)

    def body(ys5_ref, pt_ref, o0, o1, o2, l0, l1, l2, yssd_ref, x_ref, gb_ref, nw_ref, gw, pa, pb, pc, wo, out_ref):
        z = lambda n: jnp.zeros((tm, n), f32)
        out, _ = _tail_fn(ys5_ref[...], pt_ref[...], o0[...], o1[...], o2[...], l0[...], l1[...], l2[...],
                          yssd_ref[...], gb_ref[...], nw_ref[...], z(512), z(D_MODEL), z(D_MODEL), z(D_MODEL),
                          x_ref[...], (gw[...], pa[...], pb[...], pc[...], wo[...]))
        out_ref[...] = out

    return _call(body, name, (S // tm,), acts + consts, row(D_MODEL), jax.ShapeDtypeStruct((S, D_MODEL), f32),
                 sem=("parallel",))(ys5, pt, *os_, *ls_, yssd, x, glu_b, nw, *weights)


def tail_bwd(ys5, pt, os_, ls_, yssd, dout, glu_b, nw, weights, name):
    S = dout.shape[0]
    tm = TAIL_ROWS
    row, full, acts, consts = _tail_specs(tm)

    def body(ys5_ref, pt_ref, o0, o1, o2, l0, l1, l2, yssd_ref, dout_ref, gb_ref, nw_ref, gw, pa, pb, pc, wo,
             dys5_ref, dpt_ref, do0, do1, do2, dl0, dl1, dl2, dyssd_ref, dgb_ref, dnw_ref,
             g_ref, ya_ref, yb_ref, yc_ref, mg_ref, dglu_ref, dpa_ref, dpb_ref, dpc_ref):
        z = lambda n: jnp.zeros((tm, n), f32)
        w = (gw[...], pa[...], pb[...], pc[...], wo[...])
        fn = lambda *a: _tail_fn(*a, z(D_MODEL), w)
        _, vjp, aux = jax.vjp(fn, ys5_ref[...], pt_ref[...], o0[...], o1[...], o2[...], l0[...], l1[...], l2[...],
                              yssd_ref[...], gb_ref[...], nw_ref[...], z(512), z(D_MODEL), z(D_MODEL), z(D_MODEL),
                              has_aux=True)
        (dys5, dpt, d0, d1, d2, e0, e1, e2, dyssd, dgb, dnw, dglu, dpa, dpb, dpc) = vjp(dout_ref[...])
        dys5_ref[...] = dys5
        dpt_ref[...] = dpt
        for ref, val in ((do0, d0), (do1, d1), (do2, d2), (dl0, e0), (dl1, e1), (dl2, e2)):
            ref[...] = val
        dyssd_ref[...] = dyssd
        g, ya, yb, yc, merged = aux
        for ref, val in ((g_ref, g), (ya_ref, ya), (yb_ref, yb), (yc_ref, yc), (mg_ref, merged),
                         (dglu_ref, dglu), (dpa_ref, dpa), (dpb_ref, dpb), (dpc_ref, dpc)):
            ref[...] = val.astype(bf16)

        @pl.when(pl.program_id(0) == 0)
        def _():
            dgb_ref[...] = dgb
            dnw_ref[...] = dnw

        @pl.when(pl.program_id(0) > 0)
        def _():
            dgb_ref[...] += dgb
            dnw_ref[...] += dnw

    sd = lambda n, dt=f32: jax.ShapeDtypeStruct((S, n), dt)
    out_specs = ([row(512), row(4608)] + [row(256)] * 6 + [row(768), full(1, 512), full(1, 768)]
                 + [row(512), row(512), row(256), row(768), row(D_MODEL), row(512)] + [row(D_MODEL)] * 3)
    out_shape = ([sd(512), sd(4608)] + [sd(256)] * 6 + [sd(768), jax.ShapeDtypeStruct((1, 512), f32),
                                                          jax.ShapeDtypeStruct((1, 768), f32)]
                 + [sd(512, bf16), sd(512, bf16), sd(256, bf16), sd(768, bf16), sd(D_MODEL, bf16), sd(512, bf16)]
                 + [sd(D_MODEL, bf16)] * 3)
    return _call(body, name, (S // tm,), acts + consts, out_specs, out_shape,
                 sem=("arbitrary",))(ys5, pt, *os_, *ls_, yssd, dout, glu_b, nw, *weights)


def _in_proj_segments(w):
    c = lambda a, b: w[:, a:b]
    atts = []
    for g in range(3):
        parts = []
        for hh in range(2):
            o = 64 * (4 * g + 2 * hh)
            parts += [c(_C_Q + o, _C_Q + o + 128), c(_C_K + o, _C_K + o + 128), c(_C_V + o, _C_V + o + 128)]
        atts.append(jnp.concatenate(parts, axis=1))
    ssd = jnp.concatenate([c(_C_XBC, _C_ZC), jnp.zeros((D_MODEL, 1536 - (_C_ZC - _C_XBC)), w.dtype)], axis=1)
    tail = jnp.concatenate([c(_C_GATE, _C_END), c(_C_ZA, _C_Q), c(_C_ZB, _C_XBC), c(_C_ZC, _C_GATE)], axis=1)
    return [c(_C_UA, _C_ZA)] + atts + [ssd, tail]


def _in_proj_grad(ds5, datts, dssd, dtail):
    pick = lambda off: jnp.concatenate([datts[g][:, 384 * hh + off:384 * hh + off + 128]
                                        for g in range(3) for hh in range(2)], axis=1)
    return jnp.concatenate([ds5, dtail[:, 3072:3584], pick(0), pick(128), pick(256), dtail[:, 3584:3840],
                            dssd[:, :_C_ZC - _C_XBC], dtail[:, 3840:4608], dtail[:, :3072]], axis=1)


def _prep_layer(p):
    q = {}
    q["segs"] = [s.astype(bf16) for s in _in_proj_segments(p["w_in"])]
    disc = _s5_discretize(p["s5_a_re"], p["s5_a_im"], p["s5_log_step"], p["s5_b_re"], p["s5_b_im"],
                          p["s5_c_re"], p["s5_c_im"])
    q["s5"] = disc
    q["pw"] = _lam_powers(disc[0], disc[1])
    q["s5_d"] = p["s5_d"].reshape(1, 512)
    q["qw"] = jnp.tile(p["q_norm_w"], 2).reshape(1, LANES)
    q["kw"] = jnp.tile(p["k_norm_w"], 2).reshape(1, LANES)
    q["conv_w"] = p["conv_w"]
    q["conv_b"] = p["conv_b"].reshape(1, SSD_XBC)
    pad = lambda v: jnp.pad(v, (0, LANES - v.shape[0])).reshape(1, LANES)
    q["dt_bias"], q["a_log"] = pad(p["dt_bias"]), pad(p["ssd_a_log"])
    q["d_full"] = jnp.repeat(p["ssd_d"], 64).reshape(1, SSD_WIDTH)
    q["glu_b"] = p["s5_glu_b"].reshape(1, 512)
    q["nw"] = p["ssd_norm_w"].reshape(1, SSD_WIDTH)
    q["norm_w"] = p["norm_w"].reshape(1, D_MODEL)
    q["tailw"] = tuple(p[n].astype(bf16) for n in ("s5_glu_w", "proj_a", "proj_b", "proj_c", "w_out"))
    return q


_DILATIONS = (1, 4, 16)


def layer_fwd(x, q, tag):
    h = rms_fwd(x, q["norm_w"], f"rms_fwd{tag}")
    p_s5, p_a0, p_a1, p_a2, p_ssd, p_tail = [mm_nn(h, w, f"inproj{k}{tag}") for k, w in enumerate(q["segs"])]
    _, _, w_re, w_im, c_re, c_im = q["s5"]
    ys5, h_re, h_im = s5_fwd(p_s5, *q["pw"], w_re, w_im, c_re, c_im, q["s5_d"], f"s5_fwd{tag}")
    p_atts = (p_a0, p_a1, p_a2)
    os_, ls_ = [], []
    for g, d in enumerate(_DILATIONS):
        o, l = att_fwd(p_atts[g], q["qw"], q["kw"], d, f"att_fwd{g}{tag}")
        os_.append(o)
        ls_.append(l)
    xact = conv_fwd(p_ssd, q["conv_w"], q["conv_b"], f"conv_fwd{tag}")
    yssd, states = ssd_fwd(xact, p_ssd, q["dt_bias"], q["a_log"], q["d_full"], f"ssd_fwd{tag}")
    out = tail_fwd(ys5, p_tail, os_, ls_, yssd, x, q["glu_b"], q["nw"], q["tailw"], f"tail_fwd{tag}")
    saved = dict(x=x, h=h, p_s5=p_s5, p_atts=p_atts, p_ssd=p_ssd, p_tail=p_tail, ys5=ys5, h_re=h_re, h_im=h_im,
                 os=os_, ls=ls_, xact=xact, yssd=yssd, states=states)
    return out, saved


def layer_bwd(dout, sv, q, p, tag):
    S = dout.shape[0]
    (dys5, dp_tail, do0, do1, do2, dl0, dl1, dl2, dyssd, dglu_b, dnw, g_b, ya_b, yb_b, yc_b, mg_b, dglu_b16,
     dpa_b, dpb_b, dpc_b) = tail_bwd(sv["ys5"], sv["p_tail"], sv["os"], sv["ls"], sv["yssd"], dout, q["glu_b"],
                                     q["nw"], q["tailw"], f"tail_bwd{tag}")
    grads = {}
    grads["s5_glu_w"] = mm_tn(g_b, dglu_b16, f"dglu_w{tag}")
    grads["proj_a"] = mm_tn(ya_b, dpa_b, f"dproj_a{tag}")
    grads["proj_b"] = mm_tn(yb_b, dpb_b, f"dproj_b{tag}")
    grads["proj_c"] = mm_tn(yc_b, dpc_b, f"dproj_c{tag}")
    grads["w_out"] = mm_tn(mg_b, dout, f"dw_out{tag}")
    grads["s5_glu_b"] = dglu_b.reshape(512)
    grads["ssd_norm_w"] = dnw.reshape(SSD_WIDTH)

    dxact, ddt, ddt_bias, da_log, dd_full = ssd_bwd(sv["xact"], sv["p_ssd"], sv["states"], dyssd, q["dt_bias"],
                                                    q["a_log"], q["d_full"], f"ssd_bwd{tag}")
    dp_ssd, dconv_w, dconv_b = conv_bwd(sv["p_ssd"], dxact, ddt, q["conv_w"], q["conv_b"], f"conv_bwd{tag}")
    grads["dt_bias"] = ddt_bias[0, :12]
    grads["ssd_a_log"] = da_log[0, :12]
    grads["ssd_d"] = dd_full.reshape(12, 64).sum(axis=1)
    grads["conv_w"] = dconv_w
    grads["conv_b"] = dconv_b.reshape(SSD_XBC)

    dp_atts, dqw, dkw = [], 0.0, 0.0
    for g, d in enumerate(_DILATIONS):
        dp, a, b = att_bwd(sv["p_atts"][g], (do0, do1, do2)[g], (dl0, dl1, dl2)[g], q["qw"], q["kw"], d,
                           f"att_bwd{g}{tag}")
        dp_atts.append(dp)
        dqw, dkw = dqw + a, dkw + b
    grads["q_norm_w"] = dqw.reshape(2, 64).sum(axis=0)
    grads["k_norm_w"] = dkw.reshape(2, 64).sum(axis=0)

    _, _, w_re, w_im, c_re, c_im = q["s5"]
    dp_s5, dwre, dwim, dcre, dcim, dlam_re, dlam_im, dd = s5_bwd(
        dys5, sv["p_s5"], sv["h_re"], sv["h_im"], *q["pw"], w_re, w_im, c_re, c_im, q["s5_d"], f"s5_bwd{tag}")
    s5_names = ("s5_a_re", "s5_a_im", "s5_log_step", "s5_b_re", "s5_b_im", "s5_c_re", "s5_c_im")
    _, disc_vjp = jax.vjp(_s5_discretize, *[p[n] for n in s5_names])
    for n, gr in zip(s5_names, disc_vjp((dlam_re, dlam_im, dwre, dwim, dcre, dcim))):
        grads[n] = gr
    grads["s5_d"] = dd.reshape(512)

    dsegs = [dp_s5] + dp_atts + [dp_ssd, dp_tail]
    dws = [mm_tn(sv["h"], ds, f"dw_in{k}{tag}") for k, ds in enumerate(dsegs)]
    grads["w_in"] = _in_proj_grad(dws[0], dws[1:4], dws[4], dws[5])
    dh = None
    for k, (ds, w) in enumerate(zip(dsegs, q["segs"])):
        dh = mm_nt(ds, w, f"dh{k}{tag}", acc=dh)
    dx, dnorm_w = rms_bwd(sv["x"], q["norm_w"], dh, dout, f"rms_bwd{tag}")
    grads["norm_w"] = dnorm_w.reshape(D_MODEL)
    return dx, grads


_ANY = pl.BlockSpec(memory_space=pl.ANY)


def _chip_exchange(x, name, broadcast):
    shape = tuple(x.shape) if broadcast else tuple(x.shape[1:])

    def body(x_ref, o_ref, send_sems, recv_sems):
        mx, my, mc = lax.axis_index("x"), lax.axis_index("y"), lax.axis_index("c")
        me = 2 * mx + my
        copies = []
        for t, (px, py) in enumerate(((1 - mx, my), (mx, 1 - my), (1 - mx, 1 - my))):
            src = x_ref if broadcast else x_ref.at[2 * px + py]
            cp = pltpu.make_async_remote_copy(src_ref=src, dst_ref=o_ref.at[me], send_sem=send_sems.at[t],
                                              recv_sem=recv_sems.at[t], device_id=(px, py, mc),
                                              device_id_type=pl.DeviceIdType.MESH)
            cp.start()
            copies.append(cp)
        for cp in copies:
            cp.wait()

    landed = pl.pallas_call(
        body, name=name, in_specs=[_ANY], out_specs=_ANY, out_shape=jax.ShapeDtypeStruct((4,) + shape, x.dtype),
        scratch_shapes=[pltpu.SemaphoreType.DMA((3,)), pltpu.SemaphoreType.DMA((3,))],
    )(x)
    me = 2 * lax.axis_index("x") + lax.axis_index("y")
    own = x[None] if broadcast else lax.dynamic_index_in_dim(x, me, 0, keepdims=True)
    return lax.dynamic_update_index_in_dim(landed, own, me, 0)


def _sibling_exchange(xs, name):
    n = len(xs)

    def body(*refs):
        x_refs, o_refs, send_sems, recv_sems = refs[:n], refs[n:2 * n], refs[2 * n], refs[2 * n + 1]
        peer = (lax.axis_index("x"), lax.axis_index("y"), 1 - lax.axis_index("c"))
        copies = []
        for t in range(n):
            cp = pltpu.make_async_remote_copy(src_ref=x_refs[t], dst_ref=o_refs[t], send_sem=send_sems.at[t],
                                              recv_sem=recv_sems.at[t], device_id=peer,
                                              device_id_type=pl.DeviceIdType.MESH)
            cp.start()
            copies.append(cp)
        for cp in copies:
            cp.wait()

    return pl.pallas_call(
        body, name=name, in_specs=[_ANY] * n, out_specs=[_ANY] * n,
        out_shape=[jax.ShapeDtypeStruct(x.shape, x.dtype) for x in xs],
        scratch_shapes=[pltpu.SemaphoreType.DMA((n,)), pltpu.SemaphoreType.DMA((n,))],
    )(*xs)


def _rows_tile(rows, row_bytes, budget=1 << 20):
    return next(t for t in (512, 256, 128, 64, 32, 16, 8) if rows % t == 0 and t * row_bytes <= budget)


def _padded_row_bytes(cols):
    return -(-cols // LANES) * LANES * 4


def _add2(a, b, name):
    R, C = a.shape
    tr = _rows_tile(R, _padded_row_bytes(C))

    def body(a_ref, b_ref, o_ref):
        o_ref[...] = a_ref[...] + b_ref[...]

    spec = pl.BlockSpec((tr, C), lambda i: (i, 0))
    return _call(body, name, (R // tr,), [spec, spec], spec, jax.ShapeDtypeStruct((R, C), f32),
                 sem=("parallel",))(a, b)


def _sum4(x, name):
    R = x.shape[1]
    tr = _tile(R, (512, 256, 128))

    def body(x_ref, o_ref):
        o_ref[...] = ((x_ref[0] + x_ref[1]) + x_ref[2]) + x_ref[3]

    return _call(body, name, (R // tr,), [pl.BlockSpec((4, tr, LANES), lambda i: (0, i, 0))],
                 pl.BlockSpec((tr, LANES), lambda i: (i, 0)), jax.ShapeDtypeStruct((R, LANES), f32),
                 sem=("parallel",))(x)


def _adamw(g_parts, w, m, v, name):
    stacked = not isinstance(g_parts, (tuple, list))
    k = g_parts.shape[0] if stacked else len(g_parts)
    R, C = w.shape
    tr = _rows_tile(R, _padded_row_bytes(C))
    c1 = 1.0 - ADAM_B1 ** ADAM_STEP
    c2 = 1.0 - ADAM_B2 ** ADAM_STEP

    def body(*refs):
        w_ref, m_ref, v_ref, g_ref, d_ref, nm_ref, nv_ref = refs[-7:]
        if stacked:
            g = refs[0][0]
            for j in range(1, k):
                g = g + refs[0][j]
        else:
            g = refs[0][...]
            for r in refs[1:k]:
                g = g + r[...]
        m = ADAM_B1 * m_ref[...] + (1.0 - ADAM_B1) * g
        v = ADAM_B2 * v_ref[...] + (1.0 - ADAM_B2) * (g * g)
        g_ref[...] = g
        nm_ref[...] = m
        nv_ref[...] = v
        d_ref[...] = -ADAM_LR * ((m / c1) / (jnp.sqrt(v / c2) + ADAM_EPS) + ADAM_WD * w_ref[...])

    spec = pl.BlockSpec((tr, C), lambda i: (i, 0))
    sd = jax.ShapeDtypeStruct((R, C), f32)
    g_specs = [pl.BlockSpec((k, tr, C), lambda i: (0, i, 0))] if stacked else [spec] * k
    g_args = [g_parts] if stacked else list(g_parts)
    return _call(body, name, (R // tr,), g_specs + [spec] * 3, [spec] * 4, [sd] * 4,
                 sem=("parallel",))(*g_args, w, m, v)


def _pack(arrays):
    flat = jnp.concatenate([a.reshape(-1) for a in arrays])
    unit = PACK_ROWS * LANES
    n = -(-flat.shape[0] // unit) * unit
    return jnp.pad(flat, (0, n - flat.shape[0])).reshape(n // LANES, LANES)


def _unpack(buf, shapes):
    flat = buf.reshape(-1)
    out, off = [], 0
    for s in shapes:
        n = 1
        for dim in s:
            n *= dim
        out.append(flat[off:off + n].reshape(s))
        off += n
    return out


def _to_shards(full, axis):
    s = full.shape
    t = full.reshape(s[:axis] + (4, s[axis] // 4) + s[axis + 1:])
    return jnp.moveaxis(t, axis, 0)


def _from_shards(sh, axis):
    t = jnp.moveaxis(sh, 0, axis)
    s = t.shape
    return t.reshape(s[:axis] + (s[axis] * s[axis + 1],) + s[axis + 2:])


def kernel(x, norm_w, w_in, s5_a_re, s5_a_im, s5_log_step, s5_b_re, s5_b_im, s5_c_re, s5_c_im, s5_d, s5_glu_w, s5_glu_b, q_norm_w, k_norm_w, conv_w, conv_b, dt_bias, ssd_a_log, ssd_d, ssd_norm_w, proj_a, proj_b, proj_c, w_out, loss_target, m_norm_w, m_w_in, m_s5_a_re, m_s5_a_im, m_s5_log_step, m_s5_b_re, m_s5_b_im, m_s5_c_re, m_s5_c_im, m_s5_d, m_s5_glu_w, m_s5_glu_b, m_q_norm_w, m_k_norm_w, m_conv_w, m_conv_b, m_dt_bias, m_ssd_a_log, m_ssd_d, m_ssd_norm_w, m_proj_a, m_proj_b, m_proj_c, m_w_out, v_norm_w, v_w_in, v_s5_a_re, v_s5_a_im, v_s5_log_step, v_s5_b_re, v_s5_b_im, v_s5_c_re, v_s5_c_im, v_s5_d, v_s5_glu_w, v_s5_glu_b, v_q_norm_w, v_k_norm_w, v_conv_w, v_conv_b, v_dt_bias, v_ssd_a_log, v_ssd_d, v_ssd_norm_w, v_proj_a, v_proj_b, v_proj_c, v_w_out):
    given = dict(locals())
    W = {n: given[n] for n in _WEIGHTS}
    M = {n: given["m_" + n] for n in _WEIGHTS}
    V = {n: given["v_" + n] for n in _WEIGHTS}
    n_layers = norm_w.shape[0]
    assert n_layers == 2
    c = lax.axis_index("c")

    local_shapes = [W[n].shape for n, _ in _SHARDED]
    gathered = _chip_exchange(_pack([W[n] for n, _ in _SHARDED]), "gather_weights", broadcast=True)
    full = dict(W)
    full["w_in"] = _from_shards(_chip_exchange(w_in.astype(bf16), "gather_w_in", broadcast=True), 2)
    pieces = [_unpack(gathered[j], local_shapes) for j in range(4)]
    for k, (n, axis) in enumerate(_SHARDED):
        full[n] = _from_shards(jnp.stack([pieces[j][k] for j in range(4)]), axis)

    xs = x[0]
    qs, saves = [], []
    act = xs
    for l in range(n_layers):
        p = {n: full[n][l] for n in _WEIGHTS}
        q = _prep_layer(p)
        act, sv = layer_fwd(act, q, f"_l{l}")
        qs.append((q, p))
        saves.append(sv)
    dact, lsum = loss_and_grad(act, loss_target[0], "loss")
    loss = lax.psum(lsum[0, 0], ("x", "y", "c"))
    layer_grads = [None] * n_layers
    for l in reversed(range(n_layers)):
        q, p = qs[l]
        dact, layer_grads[l] = layer_bwd(dact, saves[l], q, p, f"_l{l}")
    grad_x = dact[None]
    G = {n: jnp.stack([layer_grads[l][n] for l in range(n_layers)]) for n in _WEIGHTS if n != "w_in"}

    g0, g1 = layer_grads[0]["w_in"], layer_grads[1]["w_in"]
    from_sibling, = _sibling_exchange([jnp.where(c == 0, g1, g0)], "swap_w_in_grads")
    g_chip = _add2(jnp.where(c == 0, g0, g1), from_sibling, "sum_cores_w_in")
    shards = jnp.stack([g_chip[:, W_IN_SHARD * k:W_IN_SHARD * (k + 1)] for k in range(4)])
    landed = _chip_exchange(shards, "scatter_w_in_grads", broadcast=False)
    mine_of = lambda t: lax.dynamic_index_in_dim(t, c, 0, keepdims=False)
    w_in_mine = _adamw(landed, mine_of(w_in), mine_of(m_w_in), mine_of(v_w_in), "adamw_w_in")
    w_in_theirs = _sibling_exchange(w_in_mine, "share_w_in_updates")
    w_in_out = [jnp.where(c == 0, jnp.stack([a, b]), jnp.stack([b, a])) for a, b in zip(w_in_mine, w_in_theirs)]

    repl_shapes = [W[n].shape for n in _REPL]
    small = _pack([G[n] for n in _REPL])
    quarter = small.shape[0] // 4
    big = [_to_shards(G[n], axis).reshape(4, -1) for n, axis in _SHARDED]
    big = jnp.concatenate(big, axis=1)
    unit = PACK_ROWS * LANES
    nbig = -(-big.shape[1] // unit) * unit
    big = jnp.pad(big, ((0, 0), (0, nbig - big.shape[1]))).reshape(4, nbig // LANES, LANES)
    gpack = jnp.concatenate([big, small.reshape(4, quarter, LANES)], axis=1)
    mine = _sum4(_chip_exchange(gpack, "scatter_grads", broadcast=False), "sum_chips")
    other, = _sibling_exchange([mine], "swap_cores")
    rbig = nbig // LANES

    wp, mp, vp = (_pack([T[n] for n, _ in _SHARDED]) for T in (W, M, V))
    outs_big = _adamw((mine[:rbig], other[:rbig]), wp, mp, vp, "adamw_sharded")
    big_out = [_unpack(o, local_shapes) for o in outs_big]

    gq = _add2(mine[rbig:], other[rbig:], "sum_cores_small")
    gsmall = _chip_exchange(gq, "gather_small", broadcast=True).reshape(4 * quarter, LANES)
    ws, ms, vs = (_pack([T[n] for n in _REPL]) for T in (W, M, V))
    outs_small = _adamw((gsmall,), ws, ms, vs, "adamw_replicated")
    small_out = [_unpack(o, repl_shapes) for o in outs_small]

    res = [dict(), dict(), dict(), dict()]
    for kind in range(4):
        res[kind]["w_in"] = w_in_out[kind]
        for k, (n, _) in enumerate(_SHARDED):
            res[kind][n] = big_out[kind][k]
        for k, n in enumerate(_REPL):
            res[kind][n] = small_out[kind][k]
    return (loss, grad_x, *[res[0][n] for n in _WEIGHTS], *[res[1][n] for n in _WEIGHTS],
            *[res[2][n] for n in _WEIGHTS], *[res[3][n] for n in _WEIGHTS])
```

```python
import functools

import jax
import jax.numpy as jnp
from jax import lax
from jax.experimental import pallas as pl
from jax.experimental.pallas import tpu as pltpu

f32 = jnp.float32
bf16 = jnp.bfloat16

D_MODEL = 1024
RMS_EPS = 1e-6
V7X_VMEM_LIMIT = 60 * 1024 * 1024
LANES = 128
NN, NT, TN = ((1,), (0,)), ((1,), (1,)), ((0,), (0,))

S5_STATES = 2048
S5_ROWS = 256
ATT_SEG = 2048
ATT_BLOCK = 128
SSD_CHUNK = 128
SSD_WIDTH = 768
SSD_XBC = 1280
CONV_ROWS = 512
TAIL_ROWS = 128

ADAM_LR, ADAM_B1, ADAM_B2, ADAM_EPS, ADAM_WD, ADAM_STEP = 0.001, 0.9, 0.999, 1e-08, 0.01, 10

_C_UA, _C_ZA, _C_Q, _C_K, _C_V, _C_ZB, _C_XBC, _C_DT, _C_ZC, _C_GATE, _C_END = (
    0, 512, 1024, 1792, 2560, 3328, 3584, 4864, 4876, 5644, 8716)

_SHARDED = (("s5_glu_w", 1), ("conv_w", 2), ("proj_a", 2), ("proj_b", 2), ("proj_c", 2), ("w_out", 1))
W_IN_SHARD = 2179
_REPL = ("norm_w", "s5_a_re", "s5_a_im", "s5_log_step", "s5_b_re", "s5_b_im", "s5_c_re", "s5_c_im", "s5_d",
         "s5_glu_b", "q_norm_w", "k_norm_w", "conv_b", "dt_bias", "ssd_a_log", "ssd_d", "ssd_norm_w")
_WEIGHTS = ("norm_w", "w_in", "s5_a_re", "s5_a_im", "s5_log_step", "s5_b_re", "s5_b_im", "s5_c_re", "s5_c_im",
            "s5_d", "s5_glu_w", "s5_glu_b", "q_norm_w", "k_norm_w", "conv_w", "conv_b", "dt_bias", "ssd_a_log",
            "ssd_d", "ssd_norm_w", "proj_a", "proj_b", "proj_c", "w_out")
PACK_ROWS = 512


def _dot(a, b, dims):
    return lax.dot_general(a.astype(bf16), b.astype(bf16), (dims, ((), ())), preferred_element_type=f32)


def _call(body, name, grid, in_specs, out_specs, out_shape, scratch=(), sem=None):
    return pl.pallas_call(
        body, name=name, grid=grid, in_specs=in_specs, out_specs=out_specs, out_shape=out_shape,
        scratch_shapes=list(scratch),
        compiler_params=pltpu.CompilerParams(dimension_semantics=sem, vmem_limit_bytes=V7X_VMEM_LIMIT))


def _tile(n, options=(1024, 768, 512, 384, 256, 128)):
    return next(t for t in options if n % t == 0)


@functools.partial(jax.custom_vjp, nondiff_argnums=(2,))
def _bdot(a, b, dims):
    return _dot(a, b, dims)


def _bdot_fwd(a, b, dims):
    return _dot(a, b, dims), (a, b)


def _bdot_bwd(dims, res, g):
    a, b = res
    if dims == NN:
        da, db = _dot(g, b, NT), _dot(a, g, TN)
    elif dims == NT:
        da, db = _dot(g, b, NN), _dot(g, a, TN)
    else:
        da, db = _dot(b, g, NT), _dot(a, g, NN)
    return da.astype(a.dtype), db.astype(b.dtype)


_bdot.defvjp(_bdot_fwd, _bdot_bwd)


@functools.partial(jax.custom_vjp, nondiff_argnums=(2,))
def _cdot(a, w, dims):
    return _dot(a, w, dims)


def _cdot_fwd(a, w, dims):
    return _dot(a, w, dims), w


def _cdot_bwd(dims, w, g):
    da = _dot(g, w, NT) if dims == NN else _dot(g, w, NN)
    return da, jnp.zeros_like(w)


_cdot.defvjp(_cdot_fwd, _cdot_bwd)


def _split3(x):
    hi = x.astype(bf16)
    r = x - hi.astype(f32)
    mid = r.astype(bf16)
    lo = (r - mid.astype(f32)).astype(bf16)
    return hi, mid, lo


@jax.custom_vjp
def _xdot_r(x, m):
    return sum(_dot(p, m, NN) for p in _split3(x))


def _xdot_r_fwd(x, m):
    return _xdot_r(x, m), m


def _xdot_r_bwd(m, g):
    return sum(_dot(p, m, NT) for p in _split3(g)), jnp.zeros_like(m)


_xdot_r.defvjp(_xdot_r_fwd, _xdot_r_bwd)


@jax.custom_vjp
def _xdot_l(m, x):
    return sum(_dot(m, p, NN) for p in _split3(x))


def _xdot_l_fwd(m, x):
    return _xdot_l(m, x), m


def _xdot_l_bwd(m, g):
    return jnp.zeros_like(m), sum(_dot(m, p, TN) for p in _split3(g))


_xdot_l.defvjp(_xdot_l_fwd, _xdot_l_bwd)


@jax.custom_vjp
def _softplus(x):
    e = jnp.exp(-jnp.abs(x))
    u = 1.0 + e
    log1p = jnp.where(u == 1.0, e, jnp.log(u) * (e / jnp.where(u == 1.0, 1.0, u - 1.0)))
    return jnp.maximum(x, 0.0) + log1p


def _softplus_fwd(x):
    return _softplus(x), x


def _softplus_bwd(x, g):
    return (g * jax.nn.sigmoid(x),)


_softplus.defvjp(_softplus_fwd, _softplus_bwd)


def _rms(x, w):
    return x * lax.rsqrt(jnp.mean(x * x, axis=-1, keepdims=True) + RMS_EPS) * w


def mm_nn(a, b, name, tm=1024):
    M, K = a.shape
    N = b.shape[1]
    tn = _tile(N)

    def body(a_ref, b_ref, o_ref):
        o_ref[...] = _dot(a_ref[...], b_ref[...], NN)

    return _call(body, name, (M // tm, N // tn),
                 [pl.BlockSpec((tm, K), lambda i, j: (i, 0)), pl.BlockSpec((K, tn), lambda i, j: (0, j))],
                 pl.BlockSpec((tm, tn), lambda i, j: (i, j)), jax.ShapeDtypeStruct((M, N), f32),
                 sem=("parallel", "parallel"))(a, b)


def mm_nt(a, b, name, acc=None, tm=1024):
    M, K = a.shape
    N = b.shape[0]
    tk = _tile(K)
    has_acc = acc is not None

    def body(*refs):
        a_ref, b_ref = refs[0], refs[1]
        o_ref = refs[-1]
        k = pl.program_id(1)
        p = _dot(a_ref[...], b_ref[...], NT)

        @pl.when(k == 0)
        def _():
            o_ref[...] = p + refs[2][...] if has_acc else p

        @pl.when(k > 0)
        def _():
            o_ref[...] += p

    specs = [pl.BlockSpec((tm, tk), lambda i, k: (i, k)), pl.BlockSpec((N, tk), lambda i, k: (0, k))]
    args = [a, b]
    if has_acc:
        specs.append(pl.BlockSpec((tm, N), lambda i, k: (i, 0)))
        args.append(acc)
    return _call(body, name, (M // tm, K // tk), specs, pl.BlockSpec((tm, N), lambda i, k: (i, 0)),
                 jax.ShapeDtypeStruct((M, N), f32), sem=("parallel", "arbitrary"))(*args)


def mm_tn(a, b, name, tk=1024):
    K, M = a.shape
    N = b.shape[1]
    tn = _tile(N)

    def body(a_ref, b_ref, o_ref):
        k = pl.program_id(1)
        p = _dot(a_ref[...], b_ref[...], TN)

        @pl.when(k == 0)
        def _():
            o_ref[...] = p

        @pl.when(k > 0)
        def _():
            o_ref[...] += p

    return _call(body, name, (N // tn, K // tk),
                 [pl.BlockSpec((tk, M), lambda j, k: (k, 0)), pl.BlockSpec((tk, tn), lambda j, k: (k, j))],
                 pl.BlockSpec((M, tn), lambda j, k: (0, j)), jax.ShapeDtypeStruct((M, N), f32),
                 sem=("parallel", "arbitrary"))(a, b)


def rms_fwd(x, w, name, tm=512):
    S = x.shape[0]

    def body(x_ref, w_ref, o_ref):
        o_ref[...] = _rms(x_ref[...], w_ref[...]).astype(bf16)

    return _call(body, name, (S // tm,),
                 [pl.BlockSpec((tm, D_MODEL), lambda i: (i, 0)), pl.BlockSpec((1, D_MODEL), lambda i: (0, 0))],
                 pl.BlockSpec((tm, D_MODEL), lambda i: (i, 0)), jax.ShapeDtypeStruct((S, D_MODEL), bf16),
                 sem=("parallel",))(x, w)


def rms_bwd(x, w, dh, dres, name, tm=512):
    S = x.shape[0]

    def body(x_ref, w_ref, dh_ref, dr_ref, dx_ref, dw_ref):
        _, vjp = jax.vjp(_rms, x_ref[...], w_ref[...])
        dx, dw = vjp(dh_ref[...])
        dx_ref[...] = dx + dr_ref[...]

        @pl.when(pl.program_id(0) == 0)
        def _():
            dw_ref[...] = dw

        @pl.when(pl.program_id(0) > 0)
        def _():
            dw_ref[...] += dw

    row = pl.BlockSpec((tm, D_MODEL), lambda i: (i, 0))
    vec = pl.BlockSpec((1, D_MODEL), lambda i: (0, 0))
    return _call(body, name, (S // tm,), [row, vec, row, row], [row, vec],
                 [jax.ShapeDtypeStruct((S, D_MODEL), f32), jax.ShapeDtypeStruct((1, D_MODEL), f32)],
                 sem=("arbitrary",))(x, w, dh, dres)


def loss_and_grad(y, target, name, tm=512):
    S = y.shape[0]

    def body(y_ref, t_ref, dy_ref, l_ref):
        diff = y_ref[...] - t_ref[...]
        dy_ref[...] = diff * (1.0 / D_MODEL)
        part = jnp.full((8, LANES), 0.5 / D_MODEL * jnp.sum(diff * diff), f32)

        @pl.when(pl.program_id(0) == 0)
        def _():
            l_ref[...] = part

        @pl.when(pl.program_id(0) > 0)
        def _():
            l_ref[...] += part

    row = pl.BlockSpec((tm, D_MODEL), lambda i: (i, 0))
    return _call(body, name, (S // tm,), [row, row], [row, pl.BlockSpec((8, LANES), lambda i: (0, 0))],
                 [jax.ShapeDtypeStruct((S, D_MODEL), f32), jax.ShapeDtypeStruct((8, LANES), f32)],
                 sem=("arbitrary",))(y, target)


def _s5_discretize(a_re, a_im, log_step, b_re, b_im, c_re, c_im):
    step = jnp.exp(log_step)[:, None]
    mag = jnp.exp(a_re * step)
    ang = a_im * step
    lam_re, lam_im = mag * jnp.cos(ang), mag * jnp.sin(ang)
    num_re, num_im = lam_re - 1.0, lam_im
    den = a_re * a_re + a_im * a_im
    f_re = (num_re * a_re + num_im * a_im) / den
    f_im = (num_im * a_re - num_re * a_im) / den
    bb_re = f_re[..., None] * b_re - f_im[..., None] * b_im
    bb_im = f_re[..., None] * b_im + f_im[..., None] * b_re
    eye = jnp.eye(8, dtype=f32)

    def block_in(bb):
        t = bb.transpose(0, 2, 1).reshape(4, 8, 16, 1, 64)
        return (t * eye[None, :, None, :, None]).reshape(4, 128, 512)

    def block_out(c):
        t = c.transpose(0, 2, 1).reshape(4, 8, 64, 1, 16)
        return (t * eye[None, :, None, :, None]).reshape(4, 512, 128)

    return (lam_re.reshape(1, S5_STATES), lam_im.reshape(1, S5_STATES), block_in(bb_re), block_in(bb_im),
            block_out(c_re), block_out(c_im))


def _lam_powers(lam_re, lam_im):
    rows_re, rows_im = [lam_re], [lam_im]
    for _ in range(7):
        pr, pi = rows_re[-1], rows_im[-1]
        rows_re.append(pr * lam_re - pi * lam_im)
        rows_im.append(pr * lam_im + pi * lam_re)
    return jnp.concatenate(rows_re, 0), jnp.concatenate(rows_im, 0)


def s5_fwd(u, pw_re, pw_im, w_re, w_im, c_re, c_im, dvec, name):
    S = u.shape[0]
    R, NS = S5_ROWS, S5_STATES
    nb = R // 8

    def body(u_ref, pwr_ref, pwi_ref, wre_ref, wim_ref, cre_ref, cim_ref, d_ref, y_ref, hr_ref, hi_ref,
             car_re, car_im, cin_re, cin_im, up, yp):
        @pl.when(pl.program_id(0) == 0)
        def _():
            car_re[...] = jnp.zeros_like(car_re)
            car_im[...] = jnp.zeros_like(car_im)

        slab = lambda r: pl.ds(r * nb, nb)
        for r in range(8):
            up[slab(r), :] = u_ref[:, r, :]
        u = up[...]
        for j in range(4):
            uj = u[:, 128 * j:128 * (j + 1)]
            hr_ref[:, 512 * j:512 * (j + 1)] = _dot(uj, wre_ref[j], NN)
            hi_ref[:, 512 * j:512 * (j + 1)] = _dot(uj, wim_ref[j], NN)
        lr, li = pwr_ref[0:1, :], pwi_ref[0:1, :]
        for r in range(1, 8):
            pr, pi = hr_ref[slab(r - 1), :], hi_ref[slab(r - 1), :]
            hr_ref[slab(r), :] = lr * pr - li * pi + hr_ref[slab(r), :]
            hi_ref[slab(r), :] = lr * pi + li * pr + hi_ref[slab(r), :]
        l8r, l8i = pwr_ref[7:8, :], pwi_ref[7:8, :]

        def across(c, carry):
            gr, gi = carry
            cin_re[pl.ds(c, 1), :] = gr
            cin_im[pl.ds(c, 1), :] = gi
            er, ei = hr_ref[pl.ds(7 * nb + c, 1), :], hi_ref[pl.ds(7 * nb + c, 1), :]
            return l8r * gr - l8i * gi + er, l8r * gi + l8i * gr + ei

        gr, gi = lax.fori_loop(0, nb, across, (car_re[...], car_im[...]))
        car_re[...] = gr
        car_im[...] = gi
        cr, ci = cin_re[...], cin_im[...]
        for r in range(8):
            pr, pi = pwr_ref[r:r + 1, :], pwi_ref[r:r + 1, :]
            hr_ref[slab(r), :] = hr_ref[slab(r), :] + pr * cr - pi * ci
            hi_ref[slab(r), :] = hi_ref[slab(r), :] + pr * ci + pi * cr
        for j in range(4):
            sl = slice(512 * j, 512 * (j + 1))
            cs = slice(128 * j, 128 * (j + 1))
            yp[:, cs] = (_dot(hr_ref[:, sl], cre_ref[j], NN) - _dot(hi_ref[:, sl], cim_ref[j], NN)
                         + d_ref[:, cs] * u[:, cs])
        for r in range(8):
            y_ref[:, r, :] = yp[slab(r), :]

    full = lambda shape: pl.BlockSpec(shape, lambda i: (0,) * len(shape))
    hspec = pl.BlockSpec((R, NS), lambda i: (i, 0))
    uspec = pl.BlockSpec((nb, 8, 512), lambda i: (i, 0, 0))
    y, h_re, h_im = _call(
        body, name, (S // R,),
        [uspec, full((8, NS)), full((8, NS)), full((4, 128, 512)),
         full((4, 128, 512)), full((4, 512, 128)), full((4, 512, 128)), full((1, 512))],
        [uspec, hspec, hspec],
        [jax.ShapeDtypeStruct((S // 8, 8, 512), f32), jax.ShapeDtypeStruct((S, NS), f32),
         jax.ShapeDtypeStruct((S, NS), f32)],
        scratch=[pltpu.VMEM((1, NS), f32), pltpu.VMEM((1, NS), f32), pltpu.VMEM((nb, NS), f32),
                 pltpu.VMEM((nb, NS), f32), pltpu.VMEM((R, 512), f32), pltpu.VMEM((R, 512), f32)],
        sem=("arbitrary",))(u.reshape(S // 8, 8, 512), pw_re, pw_im, w_re.astype(bf16), w_im.astype(bf16),
                            c_re.astype(bf16), c_im.astype(bf16), dvec)
    return y.reshape(S, 512), h_re, h_im


def s5_bwd(dy, u, h_re, h_im, pw_re, pw_im, w_re, w_im, c_re, c_im, dvec, name):
    S = u.shape[0]
    R, NS = S5_ROWS, S5_STATES
    nb = R // 8
    nchunk = S // R

    def body(dy_ref, u_ref, hr_ref, hi_ref, hpr_ref, hpi_ref, pwr_ref, pwi_ref, wre_ref, wim_ref, cre_ref, cim_ref,
             d_ref, du_ref, dwre_ref, dwim_ref, dcre_ref, dcim_ref, dlr_ref, dli_ref, dd_ref,
             ar, ai, car_re, car_im, cin_re, cin_im, up, dyp, dup):
        i = pl.program_id(0)

        @pl.when(i == 0)
        def _():
            for ref in (car_re, car_im, dwre_ref, dwim_ref, dcre_ref, dcim_ref, dlr_ref, dli_ref, dd_ref):
                ref[...] = jnp.zeros_like(ref)

        slab = lambda r: pl.ds(r * nb, nb)
        for r in range(8):
            up[slab(r), :] = u_ref[:, r, :]
            dyp[slab(r), :] = dy_ref[:, r, :]
        dy = dyp[...]
        u = up[...]
        for j in range(4):
            dyj = dy[:, 128 * j:128 * (j + 1)]
            ar[:, 512 * j:512 * (j + 1)] = _dot(dyj, cre_ref[j], NT)
            ai[:, 512 * j:512 * (j + 1)] = -_dot(dyj, cim_ref[j], NT)
        lr, li = pwr_ref[0:1, :], pwi_ref[0:1, :]
        for r in range(6, -1, -1):
            nr, ni = ar[slab(r + 1), :], ai[slab(r + 1), :]
            ar[slab(r), :] = lr * nr + li * ni + ar[slab(r), :]
            ai[slab(r), :] = lr * ni - li * nr + ai[slab(r), :]
        l8r, l8i = pwr_ref[7:8, :], pwi_ref[7:8, :]

        def across(k, carry):
            c = nb - 1 - k
            gr, gi = carry
            cin_re[pl.ds(c, 1), :] = gr
            cin_im[pl.ds(c, 1), :] = gi
            er, ei = ar[pl.ds(c, 1), :], ai[pl.ds(c, 1), :]
            return l8r * gr + l8i * gi + er, l8r * gi - l8i * gr + ei

        gr, gi = lax.fori_loop(0, nb, across, (car_re[...], car_im[...]))
        car_re[...] = gr
        car_im[...] = gi
        cr, ci = cin_re[...], cin_im[...]
        for r in range(8):
            pr, pi = pwr_ref[7 - r:8 - r, :], pwi_ref[7 - r:8 - r, :]
            ar[slab(r), :] = ar[slab(r), :] + pr * cr + pi * ci
            ai[slab(r), :] = ai[slab(r), :] + pr * ci - pi * cr

        acc_r = jnp.zeros((1, NS), f32)
        acc_i = jnp.zeros((1, NS), f32)
        has_prev = (i < nchunk - 1).astype(f32)
        top = lax.broadcasted_iota(jnp.int32, (nb, NS), 0) == 0
        for r in range(8):
            if r == 0:
                xr = jnp.where(top, hpr_ref[7:8, :] * has_prev, pltpu.roll(hr_ref[slab(7), :], 1, 0))
                xi = jnp.where(top, hpi_ref[7:8, :] * has_prev, pltpu.roll(hi_ref[slab(7), :], 1, 0))
            else:
                xr, xi = hr_ref[slab(r - 1), :], hi_ref[slab(r - 1), :]
            br, bi = ar[slab(r), :], ai[slab(r), :]
            acc_r += jnp.sum(br * xr + bi * xi, axis=0, keepdims=True)
            acc_i += jnp.sum(bi * xr - br * xi, axis=0, keepdims=True)
        dlr_ref[...] += acc_r
        dli_ref[...] += acc_i
        dd_ref[...] += jnp.sum(dy * u, axis=0, keepdims=True)

        for j in range(4):
            sl = slice(512 * j, 512 * (j + 1))
            cs = slice(128 * j, 128 * (j + 1))
            arj, aij = ar[:, sl], ai[:, sl]
            uj, dyj = u[:, cs], dy[:, cs]
            dup[:, cs] = _dot(arj, wre_ref[j], NT) + _dot(aij, wim_ref[j], NT) + d_ref[:, cs] * dyj
            dwre_ref[j] += _dot(uj, arj, TN)
            dwim_ref[j] += _dot(uj, aij, TN)
            dcre_ref[j] += _dot(hr_ref[:, sl], dyj, TN)
            dcim_ref[j] -= _dot(hi_ref[:, sl], dyj, TN)
        for r in range(8):
            du_ref[:, r, :] = dup[slab(r), :]

    rev = lambda i: nchunk - 1 - i
    full = lambda shape: pl.BlockSpec(shape, lambda i: (0,) * len(shape))
    row = pl.BlockSpec((nb, 8, 512), lambda i: (rev(i), 0, 0))
    hspec = pl.BlockSpec((R, NS), lambda i: (rev(i), 0))
    hprev = pl.BlockSpec((8, NS), lambda i: (jnp.maximum(rev(i) * nb - 1, 0), 0))
    outs = _call(
        body, name, (nchunk,),
        [row, row, hspec, hspec, hprev, hprev, full((8, NS)), full((8, NS)), full((4, 128, 512)), full((4, 128, 512)),
         full((4, 512, 128)), full((4, 512, 128)), full((1, 512))],
        [row, full((4, 128, 512)), full((4, 128, 512)), full((4, 512, 128)), full((4, 512, 128)),
         full((1, NS)), full((1, NS)), full((1, 512))],
        [jax.ShapeDtypeStruct((S // 8, 8, 512), f32), jax.ShapeDtypeStruct((4, 128, 512), f32),
         jax.ShapeDtypeStruct((4, 128, 512), f32), jax.ShapeDtypeStruct((4, 512, 128), f32),
         jax.ShapeDtypeStruct((4, 512, 128), f32), jax.ShapeDtypeStruct((1, NS), f32),
         jax.ShapeDtypeStruct((1, NS), f32), jax.ShapeDtypeStruct((1, 512), f32)],
        scratch=[pltpu.VMEM((R, NS), f32), pltpu.VMEM((R, NS), f32), pltpu.VMEM((1, NS), f32),
                 pltpu.VMEM((1, NS), f32), pltpu.VMEM((nb, NS), f32), pltpu.VMEM((nb, NS), f32),
                 pltpu.VMEM((R, 512), f32), pltpu.VMEM((R, 512), f32), pltpu.VMEM((R, 512), f32)],
        sem=("arbitrary",))(dy.reshape(S // 8, 8, 512), u.reshape(S // 8, 8, 512), h_re, h_im, h_re, h_im, pw_re,
                            pw_im, w_re.astype(bf16), w_im.astype(bf16), c_re.astype(bf16), c_im.astype(bf16), dvec)
    return (outs[0].reshape(S, 512),) + tuple(outs[1:])


def _rows(start, n, d):
    return pl.ds(pl.multiple_of(start, ATT_BLOCK), n) if d == 1 else pl.ds(start, n, stride=d)


def _att_block(q, k, v, qw, kw, has_prev):
    lane = lax.broadcasted_iota(jnp.int32, (1, LANES), 1)
    hm = [(lane < 64).astype(f32), (lane >= 64).astype(f32)]

    def head_norm(x, w):
        x2 = x * x
        sc = sum(hm[h] * lax.rsqrt(jnp.sum(x2 * hm[h], axis=-1, keepdims=True) * (1.0 / 64) + RMS_EPS)
                 for h in range(2))
        return x * sc * w

    qn, kn = head_norm(q, qw), head_norm(k, kw)
    qi = lax.broadcasted_iota(jnp.int32, (ATT_BLOCK, 2 * ATT_BLOCK), 0) + ATT_BLOCK
    kj = lax.broadcasted_iota(jnp.int32, (ATT_BLOCK, 2 * ATT_BLOCK), 1)
    mask = (qi - kj >= 0) & (qi - kj <= ATT_BLOCK) & (has_prev | (kj >= ATT_BLOCK))
    o = jnp.zeros((ATT_BLOCK, LANES), f32)
    lse = jnp.zeros((ATT_BLOCK, LANES), f32)
    for h in range(2):
        s = _bdot(qn * hm[h], kn, NT) * 0.125
        s = jnp.where(mask, s, -jnp.inf)
        m = jnp.max(s, axis=-1, keepdims=True)
        p = jnp.exp(s - m)
        l = jnp.sum(p, axis=-1, keepdims=True)
        o = o + hm[h] * _bdot(p / l, v, NN)
        lse = lse + hm[h] * (m + jnp.log(l))
    return o, lse


def att_fwd(p_att, qw, kw, d, name):
    S = p_att.shape[0]
    SEG = ATT_SEG
    nblk = SEG // ATT_BLOCK

    def body(p_ref, qw_ref, kw_ref, o_ref, l_ref, q_s, k_ext, v_ext, o_s, l_s):
        seg = pl.program_id(1)

        @pl.when(seg == 0)
        def _():
            k_ext[SEG:, :] = jnp.zeros((SEG, LANES), f32)
            v_ext[SEG:, :] = jnp.zeros((SEG, LANES), f32)

        k_ext[:SEG, :] = k_ext[SEG:, :]
        v_ext[:SEG, :] = v_ext[SEG:, :]
        q_s[...] = p_ref[:, 0:128]
        k_ext[SEG:, :] = p_ref[:, 128:256]
        v_ext[SEG:, :] = p_ref[:, 256:384]
        qw_v, kw_v = qw_ref[...], kw_ref[...]

        def blk(b, carry):
            j, r = b // d, b % d
            qs = j * (ATT_BLOCK * d) + r
            ks = SEG + qs - ATT_BLOCK * d
            o, lse = _att_block(q_s[_rows(qs, ATT_BLOCK, d), :], k_ext[_rows(ks, 2 * ATT_BLOCK, d), :],
                                v_ext[_rows(ks, 2 * ATT_BLOCK, d), :], qw_v, kw_v, (seg > 0) | (j > 0))
            o_s[_rows(qs, ATT_BLOCK, d), :] = o
            l_s[_rows(qs, ATT_BLOCK, d), :] = lse
            return carry

        lax.fori_loop(0, nblk, blk, 0, unroll=2)
        o_ref[...] = o_s[...]
        l_ref[...] = l_s[...]

    vec = pl.BlockSpec((1, LANES), lambda hh, s: (0, 0))
    out = pl.BlockSpec((SEG, LANES), lambda hh, s: (s, hh))
    return _call(body, name, (2, S // SEG), [pl.BlockSpec((SEG, 384), lambda hh, s: (s, hh)), vec, vec],
                 [out, out], [jax.ShapeDtypeStruct((S, 256), f32), jax.ShapeDtypeStruct((S, 256), f32)],
                 scratch=[pltpu.VMEM((SEG, LANES), f32), pltpu.VMEM((2 * SEG, LANES), f32),
                          pltpu.VMEM((2 * SEG, LANES), f32), pltpu.VMEM((SEG, LANES), f32),
                          pltpu.VMEM((SEG, LANES), f32)],
                 sem=("arbitrary", "arbitrary"))(p_att, qw, kw)


def att_bwd(p_att, do, dlse, qw, kw, d, name):
    S = p_att.shape[0]
    SEG = ATT_SEG
    nseg = S // SEG
    nblk = SEG // ATT_BLOCK

    def body(p_ref, pp_ref, do_ref, dl_ref, qw_ref, kw_ref, dp_ref, dqw_ref, dkw_ref,
             q_s, k_ext, v_ext, dq_s, dk_ext, dv_ext):
        hh, i = pl.program_id(0), pl.program_id(1)
        seg = nseg - 1 - i

        @pl.when(i == 0)
        def _():
            dk_ext[...] = jnp.zeros_like(dk_ext)
            dv_ext[...] = jnp.zeros_like(dv_ext)

        @pl.when((i == 0) & (hh == 0))
        def _():
            dqw_ref[...] = jnp.zeros_like(dqw_ref)
            dkw_ref[...] = jnp.zeros_like(dkw_ref)

        dk_ext[SEG:, :] = dk_ext[:SEG, :]
        dv_ext[SEG:, :] = dv_ext[:SEG, :]
        dk_ext[:SEG, :] = jnp.zeros((SEG, LANES), f32)
        dv_ext[:SEG, :] = jnp.zeros((SEG, LANES), f32)
        q_s[...] = p_ref[:, 0:128]
        k_ext[SEG:, :] = p_ref[:, 128:256]
        v_ext[SEG:, :] = p_ref[:, 256:384]
        k_ext[:SEG, :] = pp_ref[:, 128:256]
        v_ext[:SEG, :] = pp_ref[:, 256:384]
        qw_v, kw_v = qw_ref[...], kw_ref[...]

        def blk(b, carry):
            dqw, dkw = carry
            j, r = b // d, b % d
            qs = j * (ATT_BLOCK * d) + r
            ks = SEG + qs - ATT_BLOCK * d
            has_prev = (seg > 0) | (j > 0)
            qrows, krows = _rows(qs, ATT_BLOCK, d), _rows(ks, 2 * ATT_BLOCK, d)
            _, vjp = jax.vjp(lambda q, k, v, a, b_: _att_block(q, k, v, a, b_, has_prev),
                             q_s[qrows, :], k_ext[krows, :], v_ext[krows, :], qw_v, kw_v)
            dq, dk, dv, dqw_b, dkw_b = vjp((do_ref[qrows, :], dl_ref[qrows, :]))
            dq_s[qrows, :] = dq
            dk_ext[krows, :] = dk_ext[krows, :] + dk
            dv_ext[krows, :] = dv_ext[krows, :] + dv
            return dqw + dqw_b, dkw + dkw_b

        zero = jnp.zeros((1, LANES), f32)
        dqw, dkw = lax.fori_loop(0, nblk, blk, (zero, zero), unroll=2)
        dqw_ref[...] += dqw
        dkw_ref[...] += dkw
        dp_ref[:, 0:128] = dq_s[...]
        dp_ref[:, 128:256] = dk_ext[SEG:, :]
        dp_ref[:, 256:384] = dv_ext[SEG:, :]

    rev = lambda i: nseg - 1 - i
    vec = pl.BlockSpec((1, LANES), lambda hh, i: (0, 0))
    cur = pl.BlockSpec((SEG, 384), lambda hh, i: (rev(i), hh))
    prev = pl.BlockSpec((SEG, 384), lambda hh, i: (jnp.maximum(rev(i) - 1, 0), hh))
    col = pl.BlockSpec((SEG, LANES), lambda hh, i: (rev(i), hh))
    big = pltpu.VMEM((2 * SEG, LANES), f32)
    one = pltpu.VMEM((SEG, LANES), f32)
    return _call(body, name, (2, nseg), [cur, prev, col, col, vec, vec], [cur, vec, vec],
                 [jax.ShapeDtypeStruct((S, 768), f32), jax.ShapeDtypeStruct((1, LANES), f32),
                  jax.ShapeDtypeStruct((1, LANES), f32)],
                 scratch=[one, big, big, one, big, big],
                 sem=("arbitrary", "arbitrary"))(p_att, p_att, do, dlse, qw, kw)


def conv_fwd(p_ssd, conv_w, conv_b, name):
    S = p_ssd.shape[0]
    tm, C = CONV_ROWS, SSD_XBC

    def body(x_ref, xp_ref, w_ref, b_ref, o_ref, ext):
        first = (pl.program_id(0) == 0)
        ext[0:8, :] = jnp.where(first, 0.0, xp_ref[:, 0:C])
        ext[8:, :] = x_ref[:, 0:C]
        acc = b_ref[...] + w_ref[3:4, :] * ext[pl.ds(8, tm), :]
        for k in range(1, 4):
            acc = acc + w_ref[3 - k:4 - k, :] * ext[pl.ds(8 - k, tm), :]
        o_ref[...] = jax.nn.silu(acc)

    return _call(body, name, (S // tm,),
                 [pl.BlockSpec((tm, 1536), lambda i: (i, 0)),
                  pl.BlockSpec((8, 1536), lambda i: (jnp.maximum(i * (tm // 8) - 1, 0), 0)),
                  pl.BlockSpec((4, C), lambda i: (0, 0)), pl.BlockSpec((1, C), lambda i: (0, 0))],
                 pl.BlockSpec((tm, C), lambda i: (i, 0)), jax.ShapeDtypeStruct((S, C), f32),
                 scratch=[pltpu.VMEM((tm + 8, C), f32)], sem=("parallel",))(p_ssd, p_ssd, conv_w, conv_b)


def conv_bwd(p_ssd, dact, ddt, conv_w, conv_b, name):
    S = p_ssd.shape[0]
    tm, C = CONV_ROWS, SSD_XBC
    nblk = S // tm

    def body(x_ref, xp_ref, xn_ref, da_ref, dan_ref, ddt_ref, w_ref, b_ref, dp_ref, dw_ref, db_ref, ext, dpre):
        i = pl.program_id(0)
        ext[0:8, :] = jnp.where(i == 0, 0.0, xp_ref[:, 0:C])
        ext[8:tm + 8, :] = x_ref[:, 0:C]
        ext[tm + 8:, :] = xn_ref[:, 0:C]
        pre = b_ref[...] + w_ref[3:4, :] * ext[pl.ds(8, tm + 8), :]
        for k in range(1, 4):
            pre = pre + w_ref[3 - k:4 - k, :] * ext[pl.ds(8 - k, tm + 8), :]
        sg = jax.nn.sigmoid(pre)
        dsilu = sg * (1.0 + pre * (1.0 - sg))
        dpre[0:tm, :] = da_ref[...] * dsilu[0:tm, :]
        dpre[tm:, :] = jnp.where(i == nblk - 1, 0.0, dan_ref[...] * dsilu[tm:, :])
        dx = w_ref[3:4, :] * dpre[pl.ds(0, tm), :]
        for k in range(1, 4):
            dx = dx + w_ref[3 - k:4 - k, :] * dpre[pl.ds(k, tm), :]
        dp_ref[:, 0:C] = dx
        dp_ref[:, C:C + 128] = ddt_ref[...]
        dp_ref[:, C + 128:] = jnp.zeros((tm, 128), f32)
        dcur = dpre[pl.ds(0, tm), :]
        dws = [jnp.sum(dcur * ext[pl.ds(8 - (3 - j), tm), :], axis=0, keepdims=True) for j in range(4)]
        dbs = jnp.sum(dcur, axis=0, keepdims=True)

        @pl.when(i == 0)
        def _():
            dw_ref[...] = jnp.zeros_like(dw_ref)
            db_ref[...] = jnp.zeros_like(db_ref)

        for j in range(4):
            dw_ref[j:j + 1, :] += dws[j]
        db_ref[...] += dbs

    t8 = tm // 8
    return _call(body, name, (nblk,),
                 [pl.BlockSpec((tm, 1536), lambda i: (i, 0)),
                  pl.BlockSpec((8, 1536), lambda i: (jnp.maximum(i * t8 - 1, 0), 0)),
                  pl.BlockSpec((8, 1536), lambda i: (jnp.minimum((i + 1) * t8, S // 8 - 1), 0)),
                  pl.BlockSpec((tm, C), lambda i: (i, 0)),
                  pl.BlockSpec((8, C), lambda i: (jnp.minimum((i + 1) * t8, S // 8 - 1), 0)),
                  pl.BlockSpec((tm, 128), lambda i: (i, 0)),
                  pl.BlockSpec((4, C), lambda i: (0, 0)), pl.BlockSpec((1, C), lambda i: (0, 0))],
                 [pl.BlockSpec((tm, 1536), lambda i: (i, 0)), pl.BlockSpec((4, C), lambda i: (0, 0)),
                  pl.BlockSpec((1, C), lambda i: (0, 0))],
                 [jax.ShapeDtypeStruct((S, 1536), f32), jax.ShapeDtypeStruct((4, C), f32),
                  jax.ShapeDtypeStruct((1, C), f32)],
                 scratch=[pltpu.VMEM((tm + 16, C), f32), pltpu.VMEM((tm + 8, C), f32)],
                 sem=("arbitrary",))(p_ssd, p_ssd, p_ssd, dact, dact, ddt, conv_w, conv_b)


def _ssd_chunk(xbc, dtr, state, dt_bias, a_log, d_full):
    T = SSD_CHUNK
    r_i = lax.broadcasted_iota(jnp.int32, (T, T), 0)
    c_i = lax.broadcasted_iota(jnp.int32, (T, T), 1)
    tril = c_i <= r_i
    tri = tril.astype(bf16)
    e_rows = lax.broadcasted_iota(jnp.int32, (T, SSD_WIDTH), 0)
    e_cols = lax.broadcasted_iota(jnp.int32, (T, SSD_WIDTH), 1)
    expand = (e_cols // 64 == e_rows).astype(bf16)
    w_rows = lax.broadcasted_iota(jnp.int32, (T, 12 * T), 0)
    w_cols = lax.broadcasted_iota(jnp.int32, (T, 12 * T), 1)
    expand_wide = (w_cols // T == w_rows).astype(bf16)
    lane = lax.broadcasted_iota(jnp.int32, (1, LANES), 1)
    hm = [(lane < 64).astype(f32), (lane >= 64).astype(f32)]

    xs, bm, cm = xbc[:, :768], xbc[:, 768:1024], xbc[:, 1024:1280]
    dt = _softplus(dtr + dt_bias)
    a_dt = dt * (-jnp.exp(a_log))
    a_cs = _xdot_l(tri, a_dt)
    dt_full = _xdot_r(dt, expand)
    acs_full = _xdot_r(a_cs, expand)
    acs_wide = _xdot_r(a_cs, expand_wide)
    last = lax.broadcasted_iota(jnp.int32, (T, SSD_WIDTH), 0) == T - 1
    tot_full = jnp.sum(jnp.where(last, acs_full, 0.0), axis=0, keepdims=True)
    xdt = xs * dt_full
    xw = xdt * jnp.exp(tot_full - acs_full)
    eacs = jnp.exp(acs_full)
    st_parts, off_parts, diag_parts = [], [], []
    for g in range(2):
        bg, cg = bm[:, 128 * g:128 * (g + 1)], cm[:, 128 * g:128 * (g + 1)]
        cols = slice(384 * g, 384 * (g + 1))
        st_parts.append(_bdot(bg, xw[:, cols], TN))
        off_parts.append(_bdot(cg, state[:, cols], NN))
        cb = _bdot(cg, bg, NT)
        for pp in range(3 * g, 3 * g + 3):
            xp = xdt[:, 128 * pp:128 * (pp + 1)]
            acc = jnp.zeros((T, LANES), f32)
            for hh in range(2):
                a_col = acs_wide[:, T * (2 * pp + hh):T * (2 * pp + hh + 1)]
                decay = jnp.where(tril, jnp.exp(jnp.minimum(a_col - a_col.T, 0.0)), 0.0)
                acc = acc + _bdot(cb * decay, xp * hm[hh], NN)
            diag_parts.append(acc)
    new_state = state * jnp.exp(tot_full) + jnp.concatenate(st_parts, axis=1)
    y = jnp.concatenate(diag_parts, axis=1) + jnp.concatenate(off_parts, axis=1) * eacs + xs * d_full
    return y, new_state


def ssd_fwd(xact, p_ssd, dt_bias, a_log, d_full, name):
    S = xact.shape[0]
    T = SSD_CHUNK

    def body(x_ref, p_ref, b_ref, a_ref, d_ref, y_ref, s_ref, state):
        @pl.when(pl.program_id(0) == 0)
        def _():
            state[...] = jnp.zeros_like(state)

        st = state[...]
        s_ref[0] = st
        y, new = _ssd_chunk(x_ref[...], p_ref[...], st, b_ref[...], a_ref[...], d_ref[...])
        y_ref[...] = y
        state[...] = new

    vec = lambda n: pl.BlockSpec((1, n), lambda i: (0, 0))
    return _call(body, name, (S // T,),
                 [pl.BlockSpec((T, SSD_XBC), lambda i: (i, 0)), pl.BlockSpec((T, 128), lambda i: (i, 10)),
                  vec(128), vec(128), vec(768)],
                 [pl.BlockSpec((T, 768), lambda i: (i, 0)), pl.BlockSpec((1, T, 768), lambda i: (i, 0, 0))],
                 [jax.ShapeDtypeStruct((S, 768), f32), jax.ShapeDtypeStruct((S // T, T, 768), f32)],
                 scratch=[pltpu.VMEM((T, 768), f32)], sem=("arbitrary",))(xact, p_ssd, dt_bias, a_log, d_full)


def ssd_bwd(xact, p_ssd, states, dy, dt_bias, a_log, d_full, name):
    S = xact.shape[0]
    T = SSD_CHUNK
    nc = S // T

    def body(x_ref, p_ref, s_ref, dy_ref, b_ref, a_ref, d_ref, dx_ref, ddt_ref, db_ref, da_ref, dd_ref, dstate):
        i = pl.program_id(0)

        @pl.when(i == 0)
        def _():
            for ref in (dstate, db_ref, da_ref, dd_ref):
                ref[...] = jnp.zeros_like(ref)

        _, vjp = jax.vjp(_ssd_chunk, x_ref[...], p_ref[...], s_ref[0], b_ref[...], a_ref[...], d_ref[...])
        dx, ddt, dst, db, da, dd = vjp((dy_ref[...], dstate[...]))
        dx_ref[...] = dx
        ddt_ref[...] = ddt
        dstate[...] = dst
        db_ref[...] += db
        da_ref[...] += da
        dd_ref[...] += dd

    rev = lambda i: nc - 1 - i
    vec = lambda n: pl.BlockSpec((1, n), lambda i: (0, 0))
    return _call(body, name, (nc,),
                 [pl.BlockSpec((T, SSD_XBC), lambda i: (rev(i), 0)), pl.BlockSpec((T, 128), lambda i: (rev(i), 10)),
                  pl.BlockSpec((1, T, 768), lambda i: (rev(i), 0, 0)), pl.BlockSpec((T, 768), lambda i: (rev(i), 0)),
                  vec(128), vec(128), vec(768)],
                 [pl.BlockSpec((T, SSD_XBC), lambda i: (rev(i), 0)), pl.BlockSpec((T, 128), lambda i: (rev(i), 0)),
                  vec(128), vec(128), vec(768)],
                 [jax.ShapeDtypeStruct((S, SSD_XBC), f32), jax.ShapeDtypeStruct((S, 128), f32),
                  jax.ShapeDtypeStruct((1, 128), f32), jax.ShapeDtypeStruct((1, 128), f32),
                  jax.ShapeDtypeStruct((1, 768), f32)],
                 scratch=[pltpu.VMEM((T, 768), f32)],
                 sem=("arbitrary",))(xact, p_ssd, states, dy, dt_bias, a_log, d_full)


def _tail_fn(ys5, pt, o0, o1, o2, l0, l1, l2, yssd, glu_b, nw, pr_glu, pr_a, pr_b, pr_c, x, weights):
    glu_w, pa, pb, pc, wo = weights
    gates = jax.nn.sigmoid(pt[:, :3072])
    za, zb, zc = pt[:, 3072:3584], pt[:, 3584:3840], pt[:, 3840:4608]
    g = jax.nn.gelu(ys5)
    ya = g * jax.nn.sigmoid(_cdot(g, glu_w, NN) + glu_b + pr_glu) * jax.nn.silu(za)
    m = jnp.maximum(jnp.maximum(l0, l1), l2)
    e0, e1, e2 = jnp.exp(l0 - m), jnp.exp(l1 - m), jnp.exp(l2 - m)
    yb = (e0 * o0 + e1 * o1 + e2 * o2) / (e0 + e1 + e2) * jax.nn.silu(zb)
    yc = _rms(yssd * jax.nn.silu(zc), nw)
    merged = (gates[:, :1024] * (_cdot(ya, pa, NN) + pr_a) + gates[:, 1024:2048] * (_cdot(yb, pb, NN) + pr_b)
              + gates[:, 2048:] * (_cdot(yc, pc, NN) + pr_c))
    out = x + _cdot(merged, wo, NN)
    return out, (g, ya, yb, yc, merged)


def _tail_specs(tm):
    row = lambda n: pl.BlockSpec((tm, n), lambda i: (i, 0))
    full = lambda a, b: pl.BlockSpec((a, b), lambda i: (0, 0))
    acts = [row(512), row(4608)] + [row(256)] * 6 + [row(768), row(D_MODEL)]
    consts = [full(1, 512), full(1, 768), full(512, 512), full(512, D_MODEL), full(256, D_MODEL),
              full(768, D_MODEL), full(D_MODEL, D_MODEL)]
    return row, full, acts, consts


def tail_fwd(ys5, pt, os_, ls_, yssd, x, glu_b, nw, weights, name):
    S = x.shape[0]
    tm = TAIL_ROWS
    row, full, acts, consts = _tail_specs(tm)

    def body(ys5_ref, pt_ref, o0, o1, o2, l0, l1, l2, yssd_ref, x_ref, gb_ref, nw_ref, gw, pa, pb, pc, wo, out_ref):
        z = lambda n: jnp.zeros((tm, n), f32)
        out, _ = _tail_fn(ys5_ref[...], pt_ref[...], o0[...], o1[...], o2[...], l0[...], l1[...], l2[...],
                          yssd_ref[...], gb_ref[...], nw_ref[...], z(512), z(D_MODEL), z(D_MODEL), z(D_MODEL),
                          x_ref[...], (gw[...], pa[...], pb[...], pc[...], wo[...]))
        out_ref[...] = out

    return _call(body, name, (S // tm,), acts + consts, row(D_MODEL), jax.ShapeDtypeStruct((S, D_MODEL), f32),
                 sem=("parallel",))(ys5, pt, *os_, *ls_, yssd, x, glu_b, nw, *weights)


def tail_bwd(ys5, pt, os_, ls_, yssd, dout, glu_b, nw, weights, name):
    S = dout.shape[0]
    tm = TAIL_ROWS
    row, full, acts, consts = _tail_specs(tm)

    def body(ys5_ref, pt_ref, o0, o1, o2, l0, l1, l2, yssd_ref, dout_ref, gb_ref, nw_ref, gw, pa, pb, pc, wo,
             dys5_ref, dpt_ref, do0, do1, do2, dl0, dl1, dl2, dyssd_ref, dgb_ref, dnw_ref,
             g_ref, ya_ref, yb_ref, yc_ref, mg_ref, dglu_ref, dpa_ref, dpb_ref, dpc_ref):
        z = lambda n: jnp.zeros((tm, n), f32)
        w = (gw[...], pa[...], pb[...], pc[...], wo[...])
        fn = lambda *a: _tail_fn(*a, z(D_MODEL), w)
        _, vjp, aux = jax.vjp(fn, ys5_ref[...], pt_ref[...], o0[...], o1[...], o2[...], l0[...], l1[...], l2[...],
                              yssd_ref[...], gb_ref[...], nw_ref[...], z(512), z(D_MODEL), z(D_MODEL), z(D_MODEL),
                              has_aux=True)
        (dys5, dpt, d0, d1, d2, e0, e1, e2, dyssd, dgb, dnw, dglu, dpa, dpb, dpc) = vjp(dout_ref[...])
        dys5_ref[...] = dys5
        dpt_ref[...] = dpt
        for ref, val in ((do0, d0), (do1, d1), (do2, d2), (dl0, e0), (dl1, e1), (dl2, e2)):
            ref[...] = val
        dyssd_ref[...] = dyssd
        g, ya, yb, yc, merged = aux
        for ref, val in ((g_ref, g), (ya_ref, ya), (yb_ref, yb), (yc_ref, yc), (mg_ref, merged),
                         (dglu_ref, dglu), (dpa_ref, dpa), (dpb_ref, dpb), (dpc_ref, dpc)):
            ref[...] = val.astype(bf16)

        @pl.when(pl.program_id(0) == 0)
        def _():
            dgb_ref[...] = dgb
            dnw_ref[...] = dnw

        @pl.when(pl.program_id(0) > 0)
        def _():
            dgb_ref[...] += dgb
            dnw_ref[...] += dnw

    sd = lambda n, dt=f32: jax.ShapeDtypeStruct((S, n), dt)
    out_specs = ([row(512), row(4608)] + [row(256)] * 6 + [row(768), full(1, 512), full(1, 768)]
                 + [row(512), row(512), row(256), row(768), row(D_MODEL), row(512)] + [row(D_MODEL)] * 3)
    out_shape = ([sd(512), sd(4608)] + [sd(256)] * 6 + [sd(768), jax.ShapeDtypeStruct((1, 512), f32),
                                                          jax.ShapeDtypeStruct((1, 768), f32)]
                 + [sd(512, bf16), sd(512, bf16), sd(256, bf16), sd(768, bf16), sd(D_MODEL, bf16), sd(512, bf16)]
                 + [sd(D_MODEL, bf16)] * 3)
    return _call(body, name, (S // tm,), acts + consts, out_specs, out_shape,
                 sem=("arbitrary",))(ys5, pt, *os_, *ls_, yssd, dout, glu_b, nw, *weights)


def _in_proj_segments(w):
    c = lambda a, b: w[:, a:b]
    atts = []
    for g in range(3):
        parts = []
        for hh in range(2):
            o = 64 * (4 * g + 2 * hh)
            parts += [c(_C_Q + o, _C_Q + o + 128), c(_C_K + o, _C_K + o + 128), c(_C_V + o, _C_V + o + 128)]
        atts.append(jnp.concatenate(parts, axis=1))
    ssd = jnp.concatenate([c(_C_XBC, _C_ZC), jnp.zeros((D_MODEL, 1536 - (_C_ZC - _C_XBC)), w.dtype)], axis=1)
    tail = jnp.concatenate([c(_C_GATE, _C_END), c(_C_ZA, _C_Q), c(_C_ZB, _C_XBC), c(_C_ZC, _C_GATE)], axis=1)
    return [c(_C_UA, _C_ZA)] + atts + [ssd, tail]


def _in_proj_grad(ds5, datts, dssd, dtail):
    pick = lambda off: jnp.concatenate([datts[g][:, 384 * hh + off:384 * hh + off + 128]
                                        for g in range(3) for hh in range(2)], axis=1)
    return jnp.concatenate([ds5, dtail[:, 3072:3584], pick(0), pick(128), pick(256), dtail[:, 3584:3840],
                            dssd[:, :_C_ZC - _C_XBC], dtail[:, 3840:4608], dtail[:, :3072]], axis=1)


def _prep_layer(p):
    q = {}
    q["segs"] = [s.astype(bf16) for s in _in_proj_segments(p["w_in"])]
    disc = _s5_discretize(p["s5_a_re"], p["s5_a_im"], p["s5_log_step"], p["s5_b_re"], p["s5_b_im"],
                          p["s5_c_re"], p["s5_c_im"])
    q["s5"] = disc
    q["pw"] = _lam_powers(disc[0], disc[1])
    q["s5_d"] = p["s5_d"].reshape(1, 512)
    q["qw"] = jnp.tile(p["q_norm_w"], 2).reshape(1, LANES)
    q["kw"] = jnp.tile(p["k_norm_w"], 2).reshape(1, LANES)
    q["conv_w"] = p["conv_w"]
    q["conv_b"] = p["conv_b"].reshape(1, SSD_XBC)
    pad = lambda v: jnp.pad(v, (0, LANES - v.shape[0])).reshape(1, LANES)
    q["dt_bias"], q["a_log"] = pad(p["dt_bias"]), pad(p["ssd_a_log"])
    q["d_full"] = jnp.repeat(p["ssd_d"], 64).reshape(1, SSD_WIDTH)
    q["glu_b"] = p["s5_glu_b"].reshape(1, 512)
    q["nw"] = p["ssd_norm_w"].reshape(1, SSD_WIDTH)
    q["norm_w"] = p["norm_w"].reshape(1, D_MODEL)
    q["tailw"] = tuple(p[n].astype(bf16) for n in ("s5_glu_w", "proj_a", "proj_b", "proj_c", "w_out"))
    return q


_DILATIONS = (1, 4, 16)


def layer_fwd(x, q, tag):
    h = rms_fwd(x, q["norm_w"], f"rms_fwd{tag}")
    p_s5, p_a0, p_a1, p_a2, p_ssd, p_tail = [mm_nn(h, w, f"inproj{k}{tag}") for k, w in enumerate(q["segs"])]
    _, _, w_re, w_im, c_re, c_im = q["s5"]
    ys5, h_re, h_im = s5_fwd(p_s5, *q["pw"], w_re, w_im, c_re, c_im, q["s5_d"], f"s5_fwd{tag}")
    p_atts = (p_a0, p_a1, p_a2)
    os_, ls_ = [], []
    for g, d in enumerate(_DILATIONS):
        o, l = att_fwd(p_atts[g], q["qw"], q["kw"], d, f"att_fwd{g}{tag}")
        os_.append(o)
        ls_.append(l)
    xact = conv_fwd(p_ssd, q["conv_w"], q["conv_b"], f"conv_fwd{tag}")
    yssd, states = ssd_fwd(xact, p_ssd, q["dt_bias"], q["a_log"], q["d_full"], f"ssd_fwd{tag}")
    out = tail_fwd(ys5, p_tail, os_, ls_, yssd, x, q["glu_b"], q["nw"], q["tailw"], f"tail_fwd{tag}")
    saved = dict(x=x, h=h, p_s5=p_s5, p_atts=p_atts, p_ssd=p_ssd, p_tail=p_tail, ys5=ys5, h_re=h_re, h_im=h_im,
                 os=os_, ls=ls_, xact=xact, yssd=yssd, states=states)
    return out, saved


def layer_bwd(dout, sv, q, p, tag):
    S = dout.shape[0]
    (dys5, dp_tail, do0, do1, do2, dl0, dl1, dl2, dyssd, dglu_b, dnw, g_b, ya_b, yb_b, yc_b, mg_b, dglu_b16,
     dpa_b, dpb_b, dpc_b) = tail_bwd(sv["ys5"], sv["p_tail"], sv["os"], sv["ls"], sv["yssd"], dout, q["glu_b"],
                                     q["nw"], q["tailw"], f"tail_bwd{tag}")
    grads = {}
    grads["s5_glu_w"] = mm_tn(g_b, dglu_b16, f"dglu_w{tag}")
    grads["proj_a"] = mm_tn(ya_b, dpa_b, f"dproj_a{tag}")
    grads["proj_b"] = mm_tn(yb_b, dpb_b, f"dproj_b{tag}")
    grads["proj_c"] = mm_tn(yc_b, dpc_b, f"dproj_c{tag}")
    grads["w_out"] = mm_tn(mg_b, dout, f"dw_out{tag}")
    grads["s5_glu_b"] = dglu_b.reshape(512)
    grads["ssd_norm_w"] = dnw.reshape(SSD_WIDTH)

    dxact, ddt, ddt_bias, da_log, dd_full = ssd_bwd(sv["xact"], sv["p_ssd"], sv["states"], dyssd, q["dt_bias"],
                                                    q["a_log"], q["d_full"], f"ssd_bwd{tag}")
    dp_ssd, dconv_w, dconv_b = conv_bwd(sv["p_ssd"], dxact, ddt, q["conv_w"], q["conv_b"], f"conv_bwd{tag}")
    grads["dt_bias"] = ddt_bias[0, :12]
    grads["ssd_a_log"] = da_log[0, :12]
    grads["ssd_d"] = dd_full.reshape(12, 64).sum(axis=1)
    grads["conv_w"] = dconv_w
    grads["conv_b"] = dconv_b.reshape(SSD_XBC)

    dp_atts, dqw, dkw = [], 0.0, 0.0
    for g, d in enumerate(_DILATIONS):
        dp, a, b = att_bwd(sv["p_atts"][g], (do0, do1, do2)[g], (dl0, dl1, dl2)[g], q["qw"], q["kw"], d,
                           f"att_bwd{g}{tag}")
        dp_atts.append(dp)
        dqw, dkw = dqw + a, dkw + b
    grads["q_norm_w"] = dqw.reshape(2, 64).sum(axis=0)
    grads["k_norm_w"] = dkw.reshape(2, 64).sum(axis=0)

    _, _, w_re, w_im, c_re, c_im = q["s5"]
    dp_s5, dwre, dwim, dcre, dcim, dlam_re, dlam_im, dd = s5_bwd(
        dys5, sv["p_s5"], sv["h_re"], sv["h_im"], *q["pw"], w_re, w_im, c_re, c_im, q["s5_d"], f"s5_bwd{tag}")
    s5_names = ("s5_a_re", "s5_a_im", "s5_log_step", "s5_b_re", "s5_b_im", "s5_c_re", "s5_c_im")
    _, disc_vjp = jax.vjp(_s5_discretize, *[p[n] for n in s5_names])
    for n, gr in zip(s5_names, disc_vjp((dlam_re, dlam_im, dwre, dwim, dcre, dcim))):
        grads[n] = gr
    grads["s5_d"] = dd.reshape(512)

    dsegs = [dp_s5] + dp_atts + [dp_ssd, dp_tail]
    dws = [mm_tn(sv["h"], ds, f"dw_in{k}{tag}") for k, ds in enumerate(dsegs)]
    grads["w_in"] = _in_proj_grad(dws[0], dws[1:4], dws[4], dws[5])
    dh = None
    for k, (ds, w) in enumerate(zip(dsegs, q["segs"])):
        dh = mm_nt(ds, w, f"dh{k}{tag}", acc=dh)
    dx, dnorm_w = rms_bwd(sv["x"], q["norm_w"], dh, dout, f"rms_bwd{tag}")
    grads["norm_w"] = dnorm_w.reshape(D_MODEL)
    return dx, grads


_ANY = pl.BlockSpec(memory_space=pl.ANY)


def _chip_exchange(x, name, broadcast):
    shape = tuple(x.shape) if broadcast else tuple(x.shape[1:])

    def body(x_ref, o_ref, send_sems, recv_sems):
        mx, my, mc = lax.axis_index("x"), lax.axis_index("y"), lax.axis_index("c")
        me = 2 * mx + my
        copies = []
        for t, (px, py) in enumerate(((1 - mx, my), (mx, 1 - my), (1 - mx, 1 - my))):
            src = x_ref if broadcast else x_ref.at[2 * px + py]
            cp = pltpu.make_async_remote_copy(src_ref=src, dst_ref=o_ref.at[me], send_sem=send_sems.at[t],
                                              recv_sem=recv_sems.at[t], device_id=(px, py, mc),
                                              device_id_type=pl.DeviceIdType.MESH)
            cp.start()
            copies.append(cp)
        for cp in copies:
            cp.wait()

    landed = pl.pallas_call(
        body, name=name, in_specs=[_ANY], out_specs=_ANY, out_shape=jax.ShapeDtypeStruct((4,) + shape, x.dtype),
        scratch_shapes=[pltpu.SemaphoreType.DMA((3,)), pltpu.SemaphoreType.DMA((3,))],
    )(x)
    me = 2 * lax.axis_index("x") + lax.axis_index("y")
    own = x[None] if broadcast else lax.dynamic_index_in_dim(x, me, 0, keepdims=True)
    return lax.dynamic_update_index_in_dim(landed, own, me, 0)


def _sibling_exchange(xs, name, both=False):
    n = len(xs)

    def body(*refs):
        x_refs, o_refs, send_sems, recv_sems = refs[:n], refs[n:2 * n], refs[2 * n], refs[2 * n + 1]
        mc = lax.axis_index("c")
        peer = (lax.axis_index("x"), lax.axis_index("y"), 1 - mc)
        copies = []
        for t in range(n):
            cp = pltpu.make_async_remote_copy(src_ref=x_refs[t], dst_ref=o_refs[t].at[mc] if both else o_refs[t],
                                              send_sem=send_sems.at[t], recv_sem=recv_sems.at[t], device_id=peer,
                                              device_id_type=pl.DeviceIdType.MESH)
            cp.start()
            copies.append(cp)
        for cp in copies:
            cp.wait()

    lead = (2,) if both else ()
    outs = pl.pallas_call(
        body, name=name, in_specs=[_ANY] * n, out_specs=[_ANY] * n,
        out_shape=[jax.ShapeDtypeStruct(lead + tuple(x.shape), x.dtype) for x in xs],
        scratch_shapes=[pltpu.SemaphoreType.DMA((n,)), pltpu.SemaphoreType.DMA((n,))],
    )(*xs)
    if both:
        c = lax.axis_index("c")
        outs = [lax.dynamic_update_index_in_dim(o, x[None], c, 0) for o, x in zip(outs, xs)]
    return outs


def _rows_tile(rows, row_bytes, budget=1 << 20):
    return next(t for t in (512, 256, 128, 64, 32, 16, 8) if rows % t == 0 and t * row_bytes <= budget)


def _padded_row_bytes(cols):
    return -(-cols // LANES) * LANES * 4


def _add2(a, b, name):
    R, C = a.shape
    tr = _rows_tile(R, _padded_row_bytes(C))

    def body(a_ref, b_ref, o_ref):
        o_ref[...] = a_ref[...] + b_ref[...]

    spec = pl.BlockSpec((tr, C), lambda i: (i, 0))
    return _call(body, name, (R // tr,), [spec, spec], spec, jax.ShapeDtypeStruct((R, C), f32),
                 sem=("parallel",))(a, b)


def _sum4(x, name):
    R = x.shape[1]
    tr = _tile(R, (512, 256, 128))

    def body(x_ref, o_ref):
        p = [x_ref[j].astype(f32) for j in range(4)]
        o_ref[...] = ((p[0] + p[1]) + p[2]) + p[3]

    return _call(body, name, (R // tr,), [pl.BlockSpec((4, tr, LANES), lambda i: (0, i, 0))],
                 pl.BlockSpec((tr, LANES), lambda i: (i, 0)), jax.ShapeDtypeStruct((R, LANES), f32),
                 sem=("parallel",))(x)


def _adamw(g_parts, w, m, v, name):
    stacked = not isinstance(g_parts, (tuple, list))
    k = g_parts.shape[0] if stacked else len(g_parts)
    R, C = w.shape
    tr = _rows_tile(R, _padded_row_bytes(C))
    c1 = 1.0 - ADAM_B1 ** ADAM_STEP
    c2 = 1.0 - ADAM_B2 ** ADAM_STEP

    def body(*refs):
        w_ref, m_ref, v_ref, g_ref, d_ref, nm_ref, nv_ref = refs[-7:]
        if stacked:
            g = refs[0][0].astype(f32)
            for j in range(1, k):
                g = g + refs[0][j].astype(f32)
        else:
            g = refs[0][...]
            for r in refs[1:k]:
                g = g + r[...]
        m = ADAM_B1 * m_ref[...] + (1.0 - ADAM_B1) * g
        v = ADAM_B2 * v_ref[...] + (1.0 - ADAM_B2) * (g * g)
        g_ref[...] = g
        nm_ref[...] = m
        nv_ref[...] = v
        d_ref[...] = -ADAM_LR * ((m / c1) / (jnp.sqrt(v / c2) + ADAM_EPS) + ADAM_WD * w_ref[...])

    spec = pl.BlockSpec((tr, C), lambda i: (i, 0))
    sd = jax.ShapeDtypeStruct((R, C), f32)
    g_specs = [pl.BlockSpec((k, tr, C), lambda i: (0, i, 0))] if stacked else [spec] * k
    g_args = [g_parts] if stacked else list(g_parts)
    return _call(body, name, (R // tr,), g_specs + [spec] * 3, [spec] * 4, [sd] * 4,
                 sem=("parallel",))(*g_args, w, m, v)


def _pack(arrays):
    flat = jnp.concatenate([a.reshape(-1) for a in arrays])
    unit = PACK_ROWS * LANES
    n = -(-flat.shape[0] // unit) * unit
    return jnp.pad(flat, (0, n - flat.shape[0])).reshape(n // LANES, LANES)


def _unpack(buf, shapes):
    flat = buf.reshape(-1)
    out, off = [], 0
    for s in shapes:
        n = 1
        for dim in s:
            n *= dim
        out.append(flat[off:off + n].reshape(s))
        off += n
    return out


def _to_shards(full, axis):
    s = full.shape
    t = full.reshape(s[:axis] + (4, s[axis] // 4) + s[axis + 1:])
    return jnp.moveaxis(t, axis, 0)


def _from_shards(sh, axis):
    t = jnp.moveaxis(sh, 0, axis)
    s = t.shape
    return t.reshape(s[:axis] + (s[axis] * s[axis + 1],) + s[axis + 2:])


def kernel(x, norm_w, w_in, s5_a_re, s5_a_im, s5_log_step, s5_b_re, s5_b_im, s5_c_re, s5_c_im, s5_d, s5_glu_w, s5_glu_b, q_norm_w, k_norm_w, conv_w, conv_b, dt_bias, ssd_a_log, ssd_d, ssd_norm_w, proj_a, proj_b, proj_c, w_out, loss_target, m_norm_w, m_w_in, m_s5_a_re, m_s5_a_im, m_s5_log_step, m_s5_b_re, m_s5_b_im, m_s5_c_re, m_s5_c_im, m_s5_d, m_s5_glu_w, m_s5_glu_b, m_q_norm_w, m_k_norm_w, m_conv_w, m_conv_b, m_dt_bias, m_ssd_a_log, m_ssd_d, m_ssd_norm_w, m_proj_a, m_proj_b, m_proj_c, m_w_out, v_norm_w, v_w_in, v_s5_a_re, v_s5_a_im, v_s5_log_step, v_s5_b_re, v_s5_b_im, v_s5_c_re, v_s5_c_im, v_s5_d, v_s5_glu_w, v_s5_glu_b, v_q_norm_w, v_k_norm_w, v_conv_w, v_conv_b, v_dt_bias, v_ssd_a_log, v_ssd_d, v_ssd_norm_w, v_proj_a, v_proj_b, v_proj_c, v_w_out):
    given = dict(locals())
    W = {n: given[n] for n in _WEIGHTS}
    M = {n: given["m_" + n] for n in _WEIGHTS}
    V = {n: given["v_" + n] for n in _WEIGHTS}
    n_layers = norm_w.shape[0]
    assert n_layers == 2
    c = lax.axis_index("c")

    mine_of = lambda t: lax.dynamic_index_in_dim(t, c, 0, keepdims=False)
    as_payload = lambda n: lax.bitcast_convert_type(W[n], bf16) if n == "conv_w" else W[n].astype(bf16)
    payload_shapes = [W[n].shape + ((2,) if n == "conv_w" else ()) for n, _ in _SHARDED]
    gathered = _chip_exchange(_pack([as_payload(n) for n, _ in _SHARDED]), "gather_weights", broadcast=True)
    w_in_layers, = _sibling_exchange(
        [_chip_exchange(mine_of(w_in).astype(bf16), "gather_w_in", broadcast=True)], "share_w_in", both=True)
    full = dict(W)
    pieces = [_unpack(gathered[j], payload_shapes) for j in range(4)]
    for k, (n, axis) in enumerate(_SHARDED):
        sh = jnp.stack([pieces[j][k] for j in range(4)])
        full[n] = _from_shards(lax.bitcast_convert_type(sh, f32) if n == "conv_w" else sh, axis)

    xs = x[0]
    qs, saves = [], []
    act = xs
    for l in range(n_layers):
        p = {n: full[n][l] for n in _WEIGHTS if n != "w_in"}
        p["w_in"] = _from_shards(w_in_layers[l], 1)
        q = _prep_layer(p)
        act, sv = layer_fwd(act, q, f"_l{l}")
        qs.append((q, p))
        saves.append(sv)
    dact, lsum = loss_and_grad(act, loss_target[0], "loss")
    loss = lax.psum(lsum[0, 0], ("x", "y", "c"))
    layer_grads = [None] * n_layers
    for l in reversed(range(n_layers)):
        q, p = qs[l]
        dact, layer_grads[l] = layer_bwd(dact, saves[l], q, p, f"_l{l}")
    grad_x = dact[None]
    G = {n: jnp.stack([layer_grads[l][n] for l in range(n_layers)]) for n in _WEIGHTS if n != "w_in"}

    g0, g1 = layer_grads[0]["w_in"], layer_grads[1]["w_in"]
    from_sibling, = _sibling_exchange([jnp.where(c == 0, g1, g0)], "swap_w_in_grads")
    g_chip = _add2(jnp.where(c == 0, g0, g1), from_sibling, "sum_cores_w_in")
    shards = jnp.stack([g_chip[:, W_IN_SHARD * k:W_IN_SHARD * (k + 1)] for k in range(4)])
    landed = _chip_exchange(shards.astype(bf16), "scatter_w_in_grads", broadcast=False)
    w_in_mine = _adamw(landed, mine_of(w_in), mine_of(m_w_in), mine_of(v_w_in), "adamw_w_in")
    w_in_out = _sibling_exchange(w_in_mine, "share_w_in_updates", both=True)

    repl_shapes = [W[n].shape for n in _REPL]
    small = _pack([G[n] for n in _REPL])
    quarter = small.shape[0] // 4
    big = [_to_shards(G[n], axis).reshape(4, -1) for n, axis in _SHARDED]
    big = jnp.concatenate(big, axis=1)
    unit = PACK_ROWS * LANES
    nbig = -(-big.shape[1] // unit) * unit
    big = jnp.pad(big, ((0, 0), (0, nbig - big.shape[1]))).reshape(4, nbig // LANES, LANES)
    gpack = jnp.concatenate([big, small.reshape(4, quarter, LANES)], axis=1)
    mine = _sum4(_chip_exchange(gpack.astype(bf16), "scatter_grads", broadcast=False), "sum_chips")
    other, = _sibling_exchange([mine], "swap_cores")
    rbig = nbig // LANES

    wp, mp, vp = (_pack([T[n] for n, _ in _SHARDED]) for T in (W, M, V))
    outs_big = _adamw((mine[:rbig], other[:rbig]), wp, mp, vp, "adamw_sharded")
    big_out = [_unpack(o, [W[n].shape for n, _ in _SHARDED]) for o in outs_big]

    gq = _add2(mine[rbig:], other[rbig:], "sum_cores_small")
    gsmall = _chip_exchange(gq, "gather_small", broadcast=True).reshape(4 * quarter, LANES)
    ws, ms, vs = (_pack([T[n] for n in _REPL]) for T in (W, M, V))
    outs_small = _adamw((gsmall,), ws, ms, vs, "adamw_replicated")
    small_out = [_unpack(o, repl_shapes) for o in outs_small]

    res = [dict(), dict(), dict(), dict()]
    for kind in range(4):
        res[kind]["w_in"] = w_in_out[kind]
        for k, (n, _) in enumerate(_SHARDED):
            res[kind][n] = big_out[kind][k]
        for k, n in enumerate(_REPL):
            res[kind][n] = small_out[kind][k]
    return (loss, grad_x, *[res[0][n] for n in _WEIGHTS], *[res[1][n] for n in _WEIGHTS],
            *[res[2][n] for n in _WEIGHTS], *[res[3][n] for n in _WEIGHTS])
```

```python
import functools

import jax
import jax.numpy as jnp
from jax import lax
from jax.experimental import pallas as pl
from jax.experimental.pallas import tpu as pltpu

f32 = jnp.float32
bf16 = jnp.bfloat16

D_MODEL = 1024
RMS_EPS = 1e-6
V7X_VMEM_LIMIT = 60 * 1024 * 1024
LANES = 128
NN, NT, TN = ((1,), (0,)), ((1,), (1,)), ((0,), (0,))

S5_STATES = 2048
S5_ROWS = 256
ATT_SEG = 2048
ATT_BLOCK = 128
SSD_CHUNK = 128
SSD_WIDTH = 768
SSD_XBC = 1280
CONV_ROWS = 512
TAIL_ROWS = 128

ADAM_LR, ADAM_B1, ADAM_B2, ADAM_EPS, ADAM_WD, ADAM_STEP = 0.001, 0.9, 0.999, 1e-08, 0.01, 10

_C_UA, _C_ZA, _C_Q, _C_K, _C_V, _C_ZB, _C_XBC, _C_DT, _C_ZC, _C_GATE, _C_END = (
    0, 512, 1024, 1792, 2560, 3328, 3584, 4864, 4876, 5644, 8716)

_SHARDED = (("s5_glu_w", 1), ("conv_w", 2), ("proj_a", 2), ("proj_b", 2), ("proj_c", 2), ("w_out", 1))
W_IN_SHARD = 2179
_REPL = ("norm_w", "s5_a_re", "s5_a_im", "s5_log_step", "s5_b_re", "s5_b_im", "s5_c_re", "s5_c_im", "s5_d",
         "s5_glu_b", "q_norm_w", "k_norm_w", "conv_b", "dt_bias", "ssd_a_log", "ssd_d", "ssd_norm_w")
_WEIGHTS = ("norm_w", "w_in", "s5_a_re", "s5_a_im", "s5_log_step", "s5_b_re", "s5_b_im", "s5_c_re", "s5_c_im",
            "s5_d", "s5_glu_w", "s5_glu_b", "q_norm_w", "k_norm_w", "conv_w", "conv_b", "dt_bias", "ssd_a_log",
            "ssd_d", "ssd_norm_w", "proj_a", "proj_b", "proj_c", "w_out")
PACK_ROWS = 512


def _dot(a, b, dims):
    return lax.dot_general(a.astype(bf16), b.astype(bf16), (dims, ((), ())), preferred_element_type=f32)


def _call(body, name, grid, in_specs, out_specs, out_shape, scratch=(), sem=None):
    return pl.pallas_call(
        body, name=name, grid=grid, in_specs=in_specs, out_specs=out_specs, out_shape=out_shape,
        scratch_shapes=list(scratch),
        compiler_params=pltpu.CompilerParams(dimension_semantics=sem, vmem_limit_bytes=V7X_VMEM_LIMIT))


def _tile(n, options=(1024, 768, 512, 384, 256, 128)):
    return next(t for t in options if n % t == 0)


@functools.partial(jax.custom_vjp, nondiff_argnums=(2,))
def _bdot(a, b, dims):
    return _dot(a, b, dims)


def _bdot_fwd(a, b, dims):
    return _dot(a, b, dims), (a, b)


def _bdot_bwd(dims, res, g):
    a, b = res
    if dims == NN:
        da, db = _dot(g, b, NT), _dot(a, g, TN)
    elif dims == NT:
        da, db = _dot(g, b, NN), _dot(g, a, TN)
    else:
        da, db = _dot(b, g, NT), _dot(a, g, NN)
    return da.astype(a.dtype), db.astype(b.dtype)


_bdot.defvjp(_bdot_fwd, _bdot_bwd)


@functools.partial(jax.custom_vjp, nondiff_argnums=(2,))
def _cdot(a, w, dims):
    return _dot(a, w, dims)


def _cdot_fwd(a, w, dims):
    return _dot(a, w, dims), w


def _cdot_bwd(dims, w, g):
    da = _dot(g, w, NT) if dims == NN else _dot(g, w, NN)
    return da, jnp.zeros_like(w)


_cdot.defvjp(_cdot_fwd, _cdot_bwd)


def _split3(x):
    hi = x.astype(bf16)
    r = x - hi.astype(f32)
    mid = r.astype(bf16)
    lo = (r - mid.astype(f32)).astype(bf16)
    return hi, mid, lo


@jax.custom_vjp
def _xdot_r(x, m):
    return sum(_dot(p, m, NN) for p in _split3(x))


def _xdot_r_fwd(x, m):
    return _xdot_r(x, m), m


def _xdot_r_bwd(m, g):
    return sum(_dot(p, m, NT) for p in _split3(g)), jnp.zeros_like(m)


_xdot_r.defvjp(_xdot_r_fwd, _xdot_r_bwd)


@jax.custom_vjp
def _xdot_l(m, x):
    return sum(_dot(m, p, NN) for p in _split3(x))


def _xdot_l_fwd(m, x):
    return _xdot_l(m, x), m


def _xdot_l_bwd(m, g):
    return jnp.zeros_like(m), sum(_dot(m, p, TN) for p in _split3(g))


_xdot_l.defvjp(_xdot_l_fwd, _xdot_l_bwd)


@jax.custom_vjp
def _softplus(x):
    e = jnp.exp(-jnp.abs(x))
    u = 1.0 + e
    log1p = jnp.where(u == 1.0, e, jnp.log(u) * (e / jnp.where(u == 1.0, 1.0, u - 1.0)))
    return jnp.maximum(x, 0.0) + log1p


def _softplus_fwd(x):
    return _softplus(x), x


def _softplus_bwd(x, g):
    return (g * jax.nn.sigmoid(x),)


_softplus.defvjp(_softplus_fwd, _softplus_bwd)


def _rms(x, w):
    return x * lax.rsqrt(jnp.mean(x * x, axis=-1, keepdims=True) + RMS_EPS) * w


def mm_nn(a, b, name, tm=1024):
    M, K = a.shape
    N = b.shape[1]
    tn = _tile(N)

    def body(a_ref, b_ref, o_ref):
        o_ref[...] = _dot(a_ref[...], b_ref[...], NN)

    return _call(body, name, (M // tm, N // tn),
                 [pl.BlockSpec((tm, K), lambda i, j: (i, 0)), pl.BlockSpec((K, tn), lambda i, j: (0, j))],
                 pl.BlockSpec((tm, tn), lambda i, j: (i, j)), jax.ShapeDtypeStruct((M, N), f32),
                 sem=("parallel", "parallel"))(a, b)


def mm_nt(a, b, name, acc=None, tm=1024):
    M, K = a.shape
    N = b.shape[0]
    tk = _tile(K)
    has_acc = acc is not None

    def body(*refs):
        a_ref, b_ref = refs[0], refs[1]
        o_ref = refs[-1]
        k = pl.program_id(1)
        p = _dot(a_ref[...], b_ref[...], NT)

        @pl.when(k == 0)
        def _():
            o_ref[...] = p + refs[2][...] if has_acc else p

        @pl.when(k > 0)
        def _():
            o_ref[...] += p

    specs = [pl.BlockSpec((tm, tk), lambda i, k: (i, k)), pl.BlockSpec((N, tk), lambda i, k: (0, k))]
    args = [a, b]
    if has_acc:
        specs.append(pl.BlockSpec((tm, N), lambda i, k: (i, 0)))
        args.append(acc)
    return _call(body, name, (M // tm, K // tk), specs, pl.BlockSpec((tm, N), lambda i, k: (i, 0)),
                 jax.ShapeDtypeStruct((M, N), f32), sem=("parallel", "arbitrary"))(*args)


def mm_tn(a, b, name, tk=1024):
    K, M = a.shape
    N = b.shape[1]
    tn = _tile(N)

    def body(a_ref, b_ref, o_ref):
        k = pl.program_id(1)
        p = _dot(a_ref[...], b_ref[...], TN)

        @pl.when(k == 0)
        def _():
            o_ref[...] = p

        @pl.when(k > 0)
        def _():
            o_ref[...] += p

    return _call(body, name, (N // tn, K // tk),
                 [pl.BlockSpec((tk, M), lambda j, k: (k, 0)), pl.BlockSpec((tk, tn), lambda j, k: (k, j))],
                 pl.BlockSpec((M, tn), lambda j, k: (0, j)), jax.ShapeDtypeStruct((M, N), f32),
                 sem=("parallel", "arbitrary"))(a, b)


def rms_fwd(x, w, name, tm=512):
    S = x.shape[0]

    def body(x_ref, w_ref, o_ref):
        o_ref[...] = _rms(x_ref[...], w_ref[...]).astype(bf16)

    return _call(body, name, (S // tm,),
                 [pl.BlockSpec((tm, D_MODEL), lambda i: (i, 0)), pl.BlockSpec((1, D_MODEL), lambda i: (0, 0))],
                 pl.BlockSpec((tm, D_MODEL), lambda i: (i, 0)), jax.ShapeDtypeStruct((S, D_MODEL), bf16),
                 sem=("parallel",))(x, w)


def rms_bwd(x, w, dh, dres, name, tm=512):
    S = x.shape[0]

    def body(x_ref, w_ref, dh_ref, dr_ref, dx_ref, dw_ref):
        _, vjp = jax.vjp(_rms, x_ref[...], w_ref[...])
        dx, dw = vjp(dh_ref[...])
        dx_ref[...] = dx + dr_ref[...]

        @pl.when(pl.program_id(0) == 0)
        def _():
            dw_ref[...] = dw

        @pl.when(pl.program_id(0) > 0)
        def _():
            dw_ref[...] += dw

    row = pl.BlockSpec((tm, D_MODEL), lambda i: (i, 0))
    vec = pl.BlockSpec((1, D_MODEL), lambda i: (0, 0))
    return _call(body, name, (S // tm,), [row, vec, row, row], [row, vec],
                 [jax.ShapeDtypeStruct((S, D_MODEL), f32), jax.ShapeDtypeStruct((1, D_MODEL), f32)],
                 sem=("arbitrary",))(x, w, dh, dres)


def loss_and_grad(y, target, name, tm=512):
    S = y.shape[0]

    def body(y_ref, t_ref, dy_ref, l_ref):
        diff = y_ref[...] - t_ref[...]
        dy_ref[...] = diff * (1.0 / D_MODEL)
        part = jnp.full((8, LANES), 0.5 / D_MODEL * jnp.sum(diff * diff), f32)

        @pl.when(pl.program_id(0) == 0)
        def _():
            l_ref[...] = part

        @pl.when(pl.program_id(0) > 0)
        def _():
            l_ref[...] += part

    row = pl.BlockSpec((tm, D_MODEL), lambda i: (i, 0))
    return _call(body, name, (S // tm,), [row, row], [row, pl.BlockSpec((8, LANES), lambda i: (0, 0))],
                 [jax.ShapeDtypeStruct((S, D_MODEL), f32), jax.ShapeDtypeStruct((8, LANES), f32)],
                 sem=("arbitrary",))(y, target)


def _s5_discretize(a_re, a_im, log_step, b_re, b_im, c_re, c_im):
    step = jnp.exp(log_step)[:, None]
    mag = jnp.exp(a_re * step)
    ang = a_im * step
    lam_re, lam_im = mag * jnp.cos(ang), mag * jnp.sin(ang)
    num_re, num_im = lam_re - 1.0, lam_im
    den = a_re * a_re + a_im * a_im
    f_re = (num_re * a_re + num_im * a_im) / den
    f_im = (num_im * a_re - num_re * a_im) / den
    bb_re = f_re[..., None] * b_re - f_im[..., None] * b_im
    bb_im = f_re[..., None] * b_im + f_im[..., None] * b_re
    eye = jnp.eye(8, dtype=f32)

    def block_in(bb):
        t = bb.transpose(0, 2, 1).reshape(4, 8, 16, 1, 64)
        return (t * eye[None, :, None, :, None]).reshape(4, 128, 512)

    def block_out(c):
        t = c.transpose(0, 2, 1).reshape(4, 8, 64, 1, 16)
        return (t * eye[None, :, None, :, None]).reshape(4, 512, 128)

    return (lam_re.reshape(1, S5_STATES), lam_im.reshape(1, S5_STATES), block_in(bb_re), block_in(bb_im),
            block_out(c_re), block_out(c_im))


def _lam_powers(lam_re, lam_im):
    rows_re, rows_im = [lam_re], [lam_im]
    for _ in range(7):
        pr, pi = rows_re[-1], rows_im[-1]
        rows_re.append(pr * lam_re - pi * lam_im)
        rows_im.append(pr * lam_im + pi * lam_re)
    return jnp.concatenate(rows_re, 0), jnp.concatenate(rows_im, 0)


def s5_fwd(u, pw_re, pw_im, w_re, w_im, c_re, c_im, dvec, name):
    S = u.shape[0]
    R, NS = S5_ROWS, S5_STATES
    nb = R // 8

    def body(u_ref, pwr_ref, pwi_ref, wre_ref, wim_ref, cre_ref, cim_ref, d_ref, y_ref, hr_ref, hi_ref,
             car_re, car_im, cin_re, cin_im, up, yp):
        @pl.when(pl.program_id(0) == 0)
        def _():
            car_re[...] = jnp.zeros_like(car_re)
            car_im[...] = jnp.zeros_like(car_im)

        slab = lambda r: pl.ds(r * nb, nb)
        for r in range(8):
            up[slab(r), :] = u_ref[:, r, :]
        u = up[...]
        for j in range(4):
            uj = u[:, 128 * j:128 * (j + 1)]
            hr_ref[:, 512 * j:512 * (j + 1)] = _dot(uj, wre_ref[j], NN)
            hi_ref[:, 512 * j:512 * (j + 1)] = _dot(uj, wim_ref[j], NN)
        lr, li = pwr_ref[0:1, :], pwi_ref[0:1, :]
        for r in range(1, 8):
            pr, pi = hr_ref[slab(r - 1), :], hi_ref[slab(r - 1), :]
            hr_ref[slab(r), :] = lr * pr - li * pi + hr_ref[slab(r), :]
            hi_ref[slab(r), :] = lr * pi + li * pr + hi_ref[slab(r), :]
        l8r, l8i = pwr_ref[7:8, :], pwi_ref[7:8, :]

        def across(c, carry):
            gr, gi = carry
            cin_re[pl.ds(c, 1), :] = gr
            cin_im[pl.ds(c, 1), :] = gi
            er, ei = hr_ref[pl.ds(7 * nb + c, 1), :], hi_ref[pl.ds(7 * nb + c, 1), :]
            return l8r * gr - l8i * gi + er, l8r * gi + l8i * gr + ei

        gr, gi = lax.fori_loop(0, nb, across, (car_re[...], car_im[...]))
        car_re[...] = gr
        car_im[...] = gi
        cr, ci = cin_re[...], cin_im[...]
        for r in range(8):
            pr, pi = pwr_ref[r:r + 1, :], pwi_ref[r:r + 1, :]
            hr_ref[slab(r), :] = hr_ref[slab(r), :] + pr * cr - pi * ci
            hi_ref[slab(r), :] = hi_ref[slab(r), :] + pr * ci + pi * cr
        for j in range(4):
            sl = slice(512 * j, 512 * (j + 1))
            cs = slice(128 * j, 128 * (j + 1))
            yp[:, cs] = (_dot(hr_ref[:, sl], cre_ref[j], NN) - _dot(hi_ref[:, sl], cim_ref[j], NN)
                         + d_ref[:, cs] * u[:, cs])
        for r in range(8):
            y_ref[:, r, :] = yp[slab(r), :]

    full = lambda shape: pl.BlockSpec(shape, lambda i: (0,) * len(shape))
    hspec = pl.BlockSpec((R, NS), lambda i: (i, 0))
    uspec = pl.BlockSpec((nb, 8, 512), lambda i: (i, 0, 0))
    y, h_re, h_im = _call(
        body, name, (S // R,),
        [uspec, full((8, NS)), full((8, NS)), full((4, 128, 512)),
         full((4, 128, 512)), full((4, 512, 128)), full((4, 512, 128)), full((1, 512))],
        [uspec, hspec, hspec],
        [jax.ShapeDtypeStruct((S // 8, 8, 512), f32), jax.ShapeDtypeStruct((S, NS), f32),
         jax.ShapeDtypeStruct((S, NS), f32)],
        scratch=[pltpu.VMEM((1, NS), f32), pltpu.VMEM((1, NS), f32), pltpu.VMEM((nb, NS), f32),
                 pltpu.VMEM((nb, NS), f32), pltpu.VMEM((R, 512), f32), pltpu.VMEM((R, 512), f32)],
        sem=("arbitrary",))(u.reshape(S // 8, 8, 512), pw_re, pw_im, w_re.astype(bf16), w_im.astype(bf16),
                            c_re.astype(bf16), c_im.astype(bf16), dvec)
    return y.reshape(S, 512), h_re, h_im


def s5_bwd(dy, u, h_re, h_im, pw_re, pw_im, w_re, w_im, c_re, c_im, dvec, name):
    S = u.shape[0]
    R, NS = S5_ROWS, S5_STATES
    nb = R // 8
    nchunk = S // R

    def body(dy_ref, u_ref, hr_ref, hi_ref, hpr_ref, hpi_ref, pwr_ref, pwi_ref, wre_ref, wim_ref, cre_ref, cim_ref,
             d_ref, du_ref, dwre_ref, dwim_ref, dcre_ref, dcim_ref, dlr_ref, dli_ref, dd_ref,
             ar, ai, car_re, car_im, cin_re, cin_im, up, dyp, dup):
        i = pl.program_id(0)

        @pl.when(i == 0)
        def _():
            for ref in (car_re, car_im, dwre_ref, dwim_ref, dcre_ref, dcim_ref, dlr_ref, dli_ref, dd_ref):
                ref[...] = jnp.zeros_like(ref)

        slab = lambda r: pl.ds(r * nb, nb)
        for r in range(8):
            up[slab(r), :] = u_ref[:, r, :]
            dyp[slab(r), :] = dy_ref[:, r, :]
        dy = dyp[...]
        u = up[...]
        for j in range(4):
            dyj = dy[:, 128 * j:128 * (j + 1)]
            ar[:, 512 * j:512 * (j + 1)] = _dot(dyj, cre_ref[j], NT)
            ai[:, 512 * j:512 * (j + 1)] = -_dot(dyj, cim_ref[j], NT)
        lr, li = pwr_ref[0:1, :], pwi_ref[0:1, :]
        for r in range(6, -1, -1):
            nr, ni = ar[slab(r + 1), :], ai[slab(r + 1), :]
            ar[slab(r), :] = lr * nr + li * ni + ar[slab(r), :]
            ai[slab(r), :] = lr * ni - li * nr + ai[slab(r), :]
        l8r, l8i = pwr_ref[7:8, :], pwi_ref[7:8, :]

        def across(k, carry):
            c = nb - 1 - k
            gr, gi = carry
            cin_re[pl.ds(c, 1), :] = gr
            cin_im[pl.ds(c, 1), :] = gi
            er, ei = ar[pl.ds(c, 1), :], ai[pl.ds(c, 1), :]
            return l8r * gr + l8i * gi + er, l8r * gi - l8i * gr + ei

        gr, gi = lax.fori_loop(0, nb, across, (car_re[...], car_im[...]))
        car_re[...] = gr
        car_im[...] = gi
        cr, ci = cin_re[...], cin_im[...]
        for r in range(8):
            pr, pi = pwr_ref[7 - r:8 - r, :], pwi_ref[7 - r:8 - r, :]
            ar[slab(r), :] = ar[slab(r), :] + pr * cr + pi * ci
            ai[slab(r), :] = ai[slab(r), :] + pr * ci - pi * cr

        acc_r = jnp.zeros((1, NS), f32)
        acc_i = jnp.zeros((1, NS), f32)
        has_prev = (i < nchunk - 1).astype(f32)
        top = lax.broadcasted_iota(jnp.int32, (nb, NS), 0) == 0
        for r in range(8):
            if r == 0:
                xr = jnp.where(top, hpr_ref[7:8, :] * has_prev, pltpu.roll(hr_ref[slab(7), :], 1, 0))
                xi = jnp.where(top, hpi_ref[7:8, :] * has_prev, pltpu.roll(hi_ref[slab(7), :], 1, 0))
            else:
                xr, xi = hr_ref[slab(r - 1), :], hi_ref[slab(r - 1), :]
            br, bi = ar[slab(r), :], ai[slab(r), :]
            acc_r += jnp.sum(br * xr + bi * xi, axis=0, keepdims=True)
            acc_i += jnp.sum(bi * xr - br * xi, axis=0, keepdims=True)
        dlr_ref[...] += acc_r
        dli_ref[...] += acc_i
        dd_ref[...] += jnp.sum(dy * u, axis=0, keepdims=True)

        for j in range(4):
            sl = slice(512 * j, 512 * (j + 1))
            cs = slice(128 * j, 128 * (j + 1))
            arj, aij = ar[:, sl], ai[:, sl]
            uj, dyj = u[:, cs], dy[:, cs]
            dup[:, cs] = _dot(arj, wre_ref[j], NT) + _dot(aij, wim_ref[j], NT) + d_ref[:, cs] * dyj
            dwre_ref[j] += _dot(uj, arj, TN)
            dwim_ref[j] += _dot(uj, aij, TN)
            dcre_ref[j] += _dot(hr_ref[:, sl], dyj, TN)
            dcim_ref[j] -= _dot(hi_ref[:, sl], dyj, TN)
        for r in range(8):
            du_ref[:, r, :] = dup[slab(r), :]

    rev = lambda i: nchunk - 1 - i
    full = lambda shape: pl.BlockSpec(shape, lambda i: (0,) * len(shape))
    row = pl.BlockSpec((nb, 8, 512), lambda i: (rev(i), 0, 0))
    hspec = pl.BlockSpec((R, NS), lambda i: (rev(i), 0))
    hprev = pl.BlockSpec((8, NS), lambda i: (jnp.maximum(rev(i) * nb - 1, 0), 0))
    outs = _call(
        body, name, (nchunk,),
        [row, row, hspec, hspec, hprev, hprev, full((8, NS)), full((8, NS)), full((4, 128, 512)), full((4, 128, 512)),
         full((4, 512, 128)), full((4, 512, 128)), full((1, 512))],
        [row, full((4, 128, 512)), full((4, 128, 512)), full((4, 512, 128)), full((4, 512, 128)),
         full((1, NS)), full((1, NS)), full((1, 512))],
        [jax.ShapeDtypeStruct((S // 8, 8, 512), f32), jax.ShapeDtypeStruct((4, 128, 512), f32),
         jax.ShapeDtypeStruct((4, 128, 512), f32), jax.ShapeDtypeStruct((4, 512, 128), f32),
         jax.ShapeDtypeStruct((4, 512, 128), f32), jax.ShapeDtypeStruct((1, NS), f32),
         jax.ShapeDtypeStruct((1, NS), f32), jax.ShapeDtypeStruct((1, 512), f32)],
        scratch=[pltpu.VMEM((R, NS), f32), pltpu.VMEM((R, NS), f32), pltpu.VMEM((1, NS), f32),
                 pltpu.VMEM((1, NS), f32), pltpu.VMEM((nb, NS), f32), pltpu.VMEM((nb, NS), f32),
                 pltpu.VMEM((R, 512), f32), pltpu.VMEM((R, 512), f32), pltpu.VMEM((R, 512), f32)],
        sem=("arbitrary",))(dy.reshape(S // 8, 8, 512), u.reshape(S // 8, 8, 512), h_re, h_im, h_re, h_im, pw_re,
                            pw_im, w_re.astype(bf16), w_im.astype(bf16), c_re.astype(bf16), c_im.astype(bf16), dvec)
    return (outs[0].reshape(S, 512),) + tuple(outs[1:])


def _rows(start, n, d):
    return pl.ds(pl.multiple_of(start, ATT_BLOCK), n) if d == 1 else pl.ds(start, n, stride=d)


def _att_block(q, k, v, qw, kw, has_prev):
    lane = lax.broadcasted_iota(jnp.int32, (1, LANES), 1)
    hm = [(lane < 64).astype(f32), (lane >= 64).astype(f32)]

    def head_norm(x, w):
        x2 = x * x
        sc = sum(hm[h] * lax.rsqrt(jnp.sum(x2 * hm[h], axis=-1, keepdims=True) * (1.0 / 64) + RMS_EPS)
                 for h in range(2))
        return x * sc * w

    qn, kn = head_norm(q, qw), head_norm(k, kw)
    qi = lax.broadcasted_iota(jnp.int32, (ATT_BLOCK, 2 * ATT_BLOCK), 0) + ATT_BLOCK
    kj = lax.broadcasted_iota(jnp.int32, (ATT_BLOCK, 2 * ATT_BLOCK), 1)
    mask = (qi - kj >= 0) & (qi - kj <= ATT_BLOCK) & (has_prev | (kj >= ATT_BLOCK))
    o = jnp.zeros((ATT_BLOCK, LANES), f32)
    lse = jnp.zeros((ATT_BLOCK, LANES), f32)
    for h in range(2):
        s = _bdot(qn * hm[h], kn, NT) * 0.125
        s = jnp.where(mask, s, -jnp.inf)
        m = jnp.max(s, axis=-1, keepdims=True)
        p = jnp.exp(s - m)
        l = jnp.sum(p, axis=-1, keepdims=True)
        o = o + hm[h] * _bdot(p / l, v, NN)
        lse = lse + hm[h] * (m + jnp.log(l))
    return o, lse


def att_fwd(p_att, qw, kw, d, name):
    S = p_att.shape[0]
    SEG = ATT_SEG
    nblk = SEG // ATT_BLOCK

    def body(p_ref, qw_ref, kw_ref, o_ref, l_ref, q_s, k_ext, v_ext, o_s, l_s):
        seg = pl.program_id(1)

        @pl.when(seg == 0)
        def _():
            k_ext[SEG:, :] = jnp.zeros((SEG, LANES), f32)
            v_ext[SEG:, :] = jnp.zeros((SEG, LANES), f32)

        k_ext[:SEG, :] = k_ext[SEG:, :]
        v_ext[:SEG, :] = v_ext[SEG:, :]
        q_s[...] = p_ref[:, 0:128]
        k_ext[SEG:, :] = p_ref[:, 128:256]
        v_ext[SEG:, :] = p_ref[:, 256:384]
        qw_v, kw_v = qw_ref[...], kw_ref[...]

        def blk(b, carry):
            j, r = b // d, b % d
            qs = j * (ATT_BLOCK * d) + r
            ks = SEG + qs - ATT_BLOCK * d
            o, lse = _att_block(q_s[_rows(qs, ATT_BLOCK, d), :], k_ext[_rows(ks, 2 * ATT_BLOCK, d), :],
                                v_ext[_rows(ks, 2 * ATT_BLOCK, d), :], qw_v, kw_v, (seg > 0) | (j > 0))
            o_s[_rows(qs, ATT_BLOCK, d), :] = o
            l_s[_rows(qs, ATT_BLOCK, d), :] = lse
            return carry

        lax.fori_loop(0, nblk, blk, 0, unroll=4)
        o_ref[...] = o_s[...]
        l_ref[...] = l_s[...]

    vec = pl.BlockSpec((1, LANES), lambda hh, s: (0, 0))
    out = pl.BlockSpec((SEG, LANES), lambda hh, s: (s, hh))
    return _call(body, name, (2, S // SEG), [pl.BlockSpec((SEG, 384), lambda hh, s: (s, hh)), vec, vec],
                 [out, out], [jax.ShapeDtypeStruct((S, 256), f32), jax.ShapeDtypeStruct((S, 256), f32)],
                 scratch=[pltpu.VMEM((SEG, LANES), f32), pltpu.VMEM((2 * SEG, LANES), f32),
                          pltpu.VMEM((2 * SEG, LANES), f32), pltpu.VMEM((SEG, LANES), f32),
                          pltpu.VMEM((SEG, LANES), f32)],
                 sem=("arbitrary", "arbitrary"))(p_att, qw, kw)


def att_bwd(p_att, do, dlse, qw, kw, d, name):
    S = p_att.shape[0]
    SEG = ATT_SEG
    nseg = S // SEG
    nblk = SEG // ATT_BLOCK

    def body(p_ref, pp_ref, do_ref, dl_ref, qw_ref, kw_ref, dp_ref, dqw_ref, dkw_ref,
             q_s, k_ext, v_ext, dq_s, dk_ext, dv_ext):
        hh, i = pl.program_id(0), pl.program_id(1)
        seg = nseg - 1 - i

        @pl.when(i == 0)
        def _():
            dk_ext[...] = jnp.zeros_like(dk_ext)
            dv_ext[...] = jnp.zeros_like(dv_ext)

        @pl.when((i == 0) & (hh == 0))
        def _():
            dqw_ref[...] = jnp.zeros_like(dqw_ref)
            dkw_ref[...] = jnp.zeros_like(dkw_ref)

        dk_ext[SEG:, :] = dk_ext[:SEG, :]
        dv_ext[SEG:, :] = dv_ext[:SEG, :]
        dk_ext[:SEG, :] = jnp.zeros((SEG, LANES), f32)
        dv_ext[:SEG, :] = jnp.zeros((SEG, LANES), f32)
        q_s[...] = p_ref[:, 0:128]
        k_ext[SEG:, :] = p_ref[:, 128:256]
        v_ext[SEG:, :] = p_ref[:, 256:384]
        k_ext[:SEG, :] = pp_ref[:, 128:256]
        v_ext[:SEG, :] = pp_ref[:, 256:384]
        qw_v, kw_v = qw_ref[...], kw_ref[...]

        def blk_pair(i2, carry):
            dqw, dkw = carry
            done = []
            for u in range(2):
                b = 2 * i2 + u
                j, r = b // d, b % d
                qs = j * (ATT_BLOCK * d) + r
                ks = SEG + qs - ATT_BLOCK * d
                has_prev = (seg > 0) | (j > 0)
                qrows, krows = _rows(qs, ATT_BLOCK, d), _rows(ks, 2 * ATT_BLOCK, d)
                _, vjp = jax.vjp(lambda q, k, v, a, b_, hp=has_prev: _att_block(q, k, v, a, b_, hp),
                                 q_s[qrows, :], k_ext[krows, :], v_ext[krows, :], qw_v, kw_v)
                dq, dk, dv, dqw_b, dkw_b = vjp((do_ref[qrows, :], dl_ref[qrows, :]))
                dqw, dkw = dqw + dqw_b, dkw + dkw_b
                done.append((qrows, krows, dq, dk, dv))
            for qrows, krows, dq, dk, dv in done:
                dq_s[qrows, :] = dq
                dk_ext[krows, :] = dk_ext[krows, :] + dk
                dv_ext[krows, :] = dv_ext[krows, :] + dv
            return dqw, dkw

        zero = jnp.zeros((1, LANES), f32)
        dqw, dkw = lax.fori_loop(0, nblk // 2, blk_pair, (zero, zero))
        dqw_ref[...] += dqw
        dkw_ref[...] += dkw
        dp_ref[:, 0:128] = dq_s[...].astype(bf16)
        dp_ref[:, 128:256] = dk_ext[SEG:, :].astype(bf16)
        dp_ref[:, 256:384] = dv_ext[SEG:, :].astype(bf16)

    rev = lambda i: nseg - 1 - i
    vec = pl.BlockSpec((1, LANES), lambda hh, i: (0, 0))
    cur = pl.BlockSpec((SEG, 384), lambda hh, i: (rev(i), hh))
    prev = pl.BlockSpec((SEG, 384), lambda hh, i: (jnp.maximum(rev(i) - 1, 0), hh))
    col = pl.BlockSpec((SEG, LANES), lambda hh, i: (rev(i), hh))
    big = pltpu.VMEM((2 * SEG, LANES), f32)
    one = pltpu.VMEM((SEG, LANES), f32)
    return _call(body, name, (2, nseg), [cur, prev, col, col, vec, vec], [cur, vec, vec],
                 [jax.ShapeDtypeStruct((S, 768), bf16), jax.ShapeDtypeStruct((1, LANES), f32),
                  jax.ShapeDtypeStruct((1, LANES), f32)],
                 scratch=[one, big, big, one, big, big],
                 sem=("arbitrary", "arbitrary"))(p_att, p_att, do, dlse, qw, kw)


def conv_fwd(p_ssd, conv_w, conv_b, name):
    S = p_ssd.shape[0]
    tm, C = CONV_ROWS, SSD_XBC

    def body(x_ref, xp_ref, w_ref, b_ref, o_ref, ext):
        first = (pl.program_id(0) == 0)
        ext[0:8, :] = jnp.where(first, 0.0, xp_ref[:, 0:C])
        ext[8:, :] = x_ref[:, 0:C]
        acc = b_ref[...] + w_ref[3:4, :] * ext[pl.ds(8, tm), :]
        for k in range(1, 4):
            acc = acc + w_ref[3 - k:4 - k, :] * ext[pl.ds(8 - k, tm), :]
        o_ref[...] = jax.nn.silu(acc)

    return _call(body, name, (S // tm,),
                 [pl.BlockSpec((tm, 1536), lambda i: (i, 0)),
                  pl.BlockSpec((8, 1536), lambda i: (jnp.maximum(i * (tm // 8) - 1, 0), 0)),
                  pl.BlockSpec((4, C), lambda i: (0, 0)), pl.BlockSpec((1, C), lambda i: (0, 0))],
                 pl.BlockSpec((tm, C), lambda i: (i, 0)), jax.ShapeDtypeStruct((S, C), f32),
                 scratch=[pltpu.VMEM((tm + 8, C), f32)], sem=("parallel",))(p_ssd, p_ssd, conv_w, conv_b)


def conv_bwd(p_ssd, dact, ddt, conv_w, conv_b, name):
    S = p_ssd.shape[0]
    tm, C = CONV_ROWS, SSD_XBC
    nblk = S // tm

    def body(x_ref, xp_ref, xn_ref, da_ref, dan_ref, ddt_ref, w_ref, b_ref, dp_ref, dw_ref, db_ref, ext, dpre):
        i = pl.program_id(0)
        ext[0:8, :] = jnp.where(i == 0, 0.0, xp_ref[:, 0:C])
        ext[8:tm + 8, :] = x_ref[:, 0:C]
        ext[tm + 8:, :] = xn_ref[:, 0:C]
        pre = b_ref[...] + w_ref[3:4, :] * ext[pl.ds(8, tm + 8), :]
        for k in range(1, 4):
            pre = pre + w_ref[3 - k:4 - k, :] * ext[pl.ds(8 - k, tm + 8), :]
        sg = jax.nn.sigmoid(pre)
        dsilu = sg * (1.0 + pre * (1.0 - sg))
        dpre[0:tm, :] = da_ref[...] * dsilu[0:tm, :]
        dpre[tm:, :] = jnp.where(i == nblk - 1, 0.0, dan_ref[...] * dsilu[tm:, :])
        dx = w_ref[3:4, :] * dpre[pl.ds(0, tm), :]
        for k in range(1, 4):
            dx = dx + w_ref[3 - k:4 - k, :] * dpre[pl.ds(k, tm), :]
        dp_ref[:, 0:C] = dx.astype(bf16)
        dp_ref[:, C:C + 128] = ddt_ref[...].astype(bf16)
        dp_ref[:, C + 128:] = jnp.zeros((tm, 128), bf16)
        dcur = dpre[pl.ds(0, tm), :]
        dws = [jnp.sum(dcur * ext[pl.ds(8 - (3 - j), tm), :], axis=0, keepdims=True) for j in range(4)]
        dbs = jnp.sum(dcur, axis=0, keepdims=True)

        @pl.when(i == 0)
        def _():
            dw_ref[...] = jnp.zeros_like(dw_ref)
            db_ref[...] = jnp.zeros_like(db_ref)

        for j in range(4):
            dw_ref[j:j + 1, :] += dws[j]
        db_ref[...] += dbs

    t8 = tm // 8
    return _call(body, name, (nblk,),
                 [pl.BlockSpec((tm, 1536), lambda i: (i, 0)),
                  pl.BlockSpec((8, 1536), lambda i: (jnp.maximum(i * t8 - 1, 0), 0)),
                  pl.BlockSpec((8, 1536), lambda i: (jnp.minimum((i + 1) * t8, S // 8 - 1), 0)),
                  pl.BlockSpec((tm, C), lambda i: (i, 0)),
                  pl.BlockSpec((8, C), lambda i: (jnp.minimum((i + 1) * t8, S // 8 - 1), 0)),
                  pl.BlockSpec((tm, 128), lambda i: (i, 0)),
                  pl.BlockSpec((4, C), lambda i: (0, 0)), pl.BlockSpec((1, C), lambda i: (0, 0))],
                 [pl.BlockSpec((tm, 1536), lambda i: (i, 0)), pl.BlockSpec((4, C), lambda i: (0, 0)),
                  pl.BlockSpec((1, C), lambda i: (0, 0))],
                 [jax.ShapeDtypeStruct((S, 1536), bf16), jax.ShapeDtypeStruct((4, C), f32),
                  jax.ShapeDtypeStruct((1, C), f32)],
                 scratch=[pltpu.VMEM((tm + 16, C), f32), pltpu.VMEM((tm + 8, C), f32)],
                 sem=("arbitrary",))(p_ssd, p_ssd, p_ssd, dact, dact, ddt, conv_w, conv_b)


def _ssd_chunk(xbc, dtr, state, dt_bias, a_log, d_full):
    T = SSD_CHUNK
    r_i = lax.broadcasted_iota(jnp.int32, (T, T), 0)
    c_i = lax.broadcasted_iota(jnp.int32, (T, T), 1)
    tril = c_i <= r_i
    tri = tril.astype(bf16)
    e_rows = lax.broadcasted_iota(jnp.int32, (T, SSD_WIDTH), 0)
    e_cols = lax.broadcasted_iota(jnp.int32, (T, SSD_WIDTH), 1)
    expand = (e_cols // 64 == e_rows).astype(bf16)
    w_rows = lax.broadcasted_iota(jnp.int32, (T, 12 * T), 0)
    w_cols = lax.broadcasted_iota(jnp.int32, (T, 12 * T), 1)
    expand_wide = (w_cols // T == w_rows).astype(bf16)
    lane = lax.broadcasted_iota(jnp.int32, (1, LANES), 1)
    hm = [(lane < 64).astype(f32), (lane >= 64).astype(f32)]

    xs, bm, cm = xbc[:, :768], xbc[:, 768:1024], xbc[:, 1024:1280]
    dt = _softplus(dtr + dt_bias)
    a_dt = dt * (-jnp.exp(a_log))
    a_cs = _xdot_l(tri, a_dt)
    dt_full = _xdot_r(dt, expand)
    acs_full = _xdot_r(a_cs, expand)
    acs_wide = _xdot_r(a_cs, expand_wide)
    last = lax.broadcasted_iota(jnp.int32, (T, SSD_WIDTH), 0) == T - 1
    tot_full = jnp.sum(jnp.where(last, acs_full, 0.0), axis=0, keepdims=True)
    xdt = xs * dt_full
    xw = xdt * jnp.exp(tot_full - acs_full)
    eacs = jnp.exp(acs_full)
    st_parts, off_parts, diag_parts = [], [], []
    for g in range(2):
        bg, cg = bm[:, 128 * g:128 * (g + 1)], cm[:, 128 * g:128 * (g + 1)]
        cols = slice(384 * g, 384 * (g + 1))
        st_parts.append(_bdot(bg, xw[:, cols], TN))
        off_parts.append(_bdot(cg, state[:, cols], NN))
        cb = _bdot(cg, bg, NT)
        for pp in range(3 * g, 3 * g + 3):
            xp = xdt[:, 128 * pp:128 * (pp + 1)]
            acc = jnp.zeros((T, LANES), f32)
            for hh in range(2):
                a_col = acs_wide[:, T * (2 * pp + hh):T * (2 * pp + hh + 1)]
                decay = jnp.where(tril, jnp.exp(jnp.minimum(a_col - a_col.T, 0.0)), 0.0)
                acc = acc + _bdot(cb * decay, xp * hm[hh], NN)
            diag_parts.append(acc)
    new_state = state * jnp.exp(tot_full) + jnp.concatenate(st_parts, axis=1)
    y = jnp.concatenate(diag_parts, axis=1) + jnp.concatenate(off_parts, axis=1) * eacs + xs * d_full
    return y, new_state


def ssd_fwd(xact, p_ssd, dt_bias, a_log, d_full, name):
    S = xact.shape[0]
    T = SSD_CHUNK

    def body(x_ref, p_ref, b_ref, a_ref, d_ref, y_ref, s_ref, state):
        @pl.when(pl.program_id(0) == 0)
        def _():
            state[...] = jnp.zeros_like(state)

        st = state[...]
        s_ref[0] = st
        y, new = _ssd_chunk(x_ref[...], p_ref[...], st, b_ref[...], a_ref[...], d_ref[...])
        y_ref[...] = y
        state[...] = new

    vec = lambda n: pl.BlockSpec((1, n), lambda i: (0, 0))
    return _call(body, name, (S // T,),
                 [pl.BlockSpec((T, SSD_XBC), lambda i: (i, 0)), pl.BlockSpec((T, 128), lambda i: (i, 10)),
                  vec(128), vec(128), vec(768)],
                 [pl.BlockSpec((T, 768), lambda i: (i, 0)), pl.BlockSpec((1, T, 768), lambda i: (i, 0, 0))],
                 [jax.ShapeDtypeStruct((S, 768), f32), jax.ShapeDtypeStruct((S // T, T, 768), f32)],
                 scratch=[pltpu.VMEM((T, 768), f32)], sem=("arbitrary",))(xact, p_ssd, dt_bias, a_log, d_full)


def ssd_bwd(xact, p_ssd, states, dy, dt_bias, a_log, d_full, name):
    S = xact.shape[0]
    T = SSD_CHUNK
    nc = S // T

    def body(x_ref, p_ref, s_ref, dy_ref, b_ref, a_ref, d_ref, dx_ref, ddt_ref, db_ref, da_ref, dd_ref, dstate):
        i = pl.program_id(0)

        @pl.when(i == 0)
        def _():
            for ref in (dstate, db_ref, da_ref, dd_ref):
                ref[...] = jnp.zeros_like(ref)

        _, vjp = jax.vjp(_ssd_chunk, x_ref[...], p_ref[...], s_ref[0], b_ref[...], a_ref[...], d_ref[...])
        dx, ddt, dst, db, da, dd = vjp((dy_ref[...], dstate[...]))
        dx_ref[...] = dx
        ddt_ref[...] = ddt
        dstate[...] = dst
        db_ref[...] += db
        da_ref[...] += da
        dd_ref[...] += dd

    rev = lambda i: nc - 1 - i
    vec = lambda n: pl.BlockSpec((1, n), lambda i: (0, 0))
    return _call(body, name, (nc,),
                 [pl.BlockSpec((T, SSD_XBC), lambda i: (rev(i), 0)), pl.BlockSpec((T, 128), lambda i: (rev(i), 10)),
                  pl.BlockSpec((1, T, 768), lambda i: (rev(i), 0, 0)), pl.BlockSpec((T, 768), lambda i: (rev(i), 0)),
                  vec(128), vec(128), vec(768)],
                 [pl.BlockSpec((T, SSD_XBC), lambda i: (rev(i), 0)), pl.BlockSpec((T, 128), lambda i: (rev(i), 0)),
                  vec(128), vec(128), vec(768)],
                 [jax.ShapeDtypeStruct((S, SSD_XBC), f32), jax.ShapeDtypeStruct((S, 128), f32),
                  jax.ShapeDtypeStruct((1, 128), f32), jax.ShapeDtypeStruct((1, 128), f32),
                  jax.ShapeDtypeStruct((1, 768), f32)],
                 scratch=[pltpu.VMEM((T, 768), f32)],
                 sem=("arbitrary",))(xact, p_ssd, states, dy, dt_bias, a_log, d_full)


def _tail_fn(ys5, pt, o0, o1, o2, l0, l1, l2, yssd, glu_b, nw, pr_glu, pr_a, pr_b, pr_c, x, weights):
    glu_w, pa, pb, pc, wo = weights
    gates = jax.nn.sigmoid(pt[:, :3072])
    za, zb, zc = pt[:, 3072:3584], pt[:, 3584:3840], pt[:, 3840:4608]
    g = jax.nn.gelu(ys5)
    ya = g * jax.nn.sigmoid(_cdot(g, glu_w, NN) + glu_b + pr_glu) * jax.nn.silu(za)
    m = jnp.maximum(jnp.maximum(l0, l1), l2)
    e0, e1, e2 = jnp.exp(l0 - m), jnp.exp(l1 - m), jnp.exp(l2 - m)
    yb = (e0 * o0 + e1 * o1 + e2 * o2) / (e0 + e1 + e2) * jax.nn.silu(zb)
    yc = _rms(yssd * jax.nn.silu(zc), nw)
    merged = (gates[:, :1024] * (_cdot(ya, pa, NN) + pr_a) + gates[:, 1024:2048] * (_cdot(yb, pb, NN) + pr_b)
              + gates[:, 2048:] * (_cdot(yc, pc, NN) + pr_c))
    out = x + _cdot(merged, wo, NN)
    return out, (g, ya, yb, yc, merged)


def _tail_specs(tm):
    row = lambda n: pl.BlockSpec((tm, n), lambda i: (i, 0))
    full = lambda a, b: pl.BlockSpec((a, b), lambda i: (0, 0))
    acts = [row(512), row(4608)] + [row(256)] * 6 + [row(768), row(D_MODEL)]
    consts = [full(1, 512), full(1, 768), full(512, 512), full(512, D_MODEL), full(256, D_MODEL),
              full(768, D_MODEL), full(D_MODEL, D_MODEL)]
    return row, full, acts, consts


def tail_fwd(ys5, pt, os_, ls_, yssd, x, glu_b, nw, weights, name):
    S = x.shape[0]
    tm = TAIL_ROWS
    row, full, acts, consts = _tail_specs(tm)

    def body(ys5_ref, pt_ref, o0, o1, o2, l0, l1, l2, yssd_ref, x_ref, gb_ref, nw_ref, gw, pa, pb, pc, wo, out_ref):
        z = lambda n: jnp.zeros((tm, n), f32)
        out, _ = _tail_fn(ys5_ref[...], pt_ref[...], o0[...], o1[...], o2[...], l0[...], l1[...], l2[...],
                          yssd_ref[...], gb_ref[...], nw_ref[...], z(512), z(D_MODEL), z(D_MODEL), z(D_MODEL),
                          x_ref[...], (gw[...], pa[...], pb[...], pc[...], wo[...]))
        out_ref[...] = out

    return _call(body, name, (S // tm,), acts + consts, row(D_MODEL), jax.ShapeDtypeStruct((S, D_MODEL), f32),
                 sem=("parallel",))(ys5, pt, *os_, *ls_, yssd, x, glu_b, nw, *weights)


def tail_bwd(ys5, pt, os_, ls_, yssd, dout, glu_b, nw, weights, name):
    S = dout.shape[0]
    tm = TAIL_ROWS
    row, full, acts, consts = _tail_specs(tm)

    def body(ys5_ref, pt_ref, o0, o1, o2, l0, l1, l2, yssd_ref, dout_ref, gb_ref, nw_ref, gw, pa, pb, pc, wo,
             dys5_ref, dpt_ref, do0, do1, do2, dl0, dl1, dl2, dyssd_ref, dgb_ref, dnw_ref,
             g_ref, ya_ref, yb_ref, yc_ref, mg_ref, dglu_ref, dpa_ref, dpb_ref, dpc_ref):
        z = lambda n: jnp.zeros((tm, n), f32)
        w = (gw[...], pa[...], pb[...], pc[...], wo[...])
        fn = lambda *a: _tail_fn(*a, z(D_MODEL), w)
        _, vjp, aux = jax.vjp(fn, ys5_ref[...], pt_ref[...], o0[...], o1[...], o2[...], l0[...], l1[...], l2[...],
                              yssd_ref[...], gb_ref[...], nw_ref[...], z(512), z(D_MODEL), z(D_MODEL), z(D_MODEL),
                              has_aux=True)
        (dys5, dpt, d0, d1, d2, e0, e1, e2, dyssd, dgb, dnw, dglu, dpa, dpb, dpc) = vjp(dout_ref[...])
        dys5_ref[...] = dys5
        dpt_ref[...] = dpt.astype(bf16)
        for ref, val in ((do0, d0), (do1, d1), (do2, d2), (dl0, e0), (dl1, e1), (dl2, e2)):
            ref[...] = val
        dyssd_ref[...] = dyssd
        g, ya, yb, yc, merged = aux
        for ref, val in ((g_ref, g), (ya_ref, ya), (yb_ref, yb), (yc_ref, yc), (mg_ref, merged),
                         (dglu_ref, dglu), (dpa_ref, dpa), (dpb_ref, dpb), (dpc_ref, dpc)):
            ref[...] = val.astype(bf16)

        @pl.when(pl.program_id(0) == 0)
        def _():
            dgb_ref[...] = dgb
            dnw_ref[...] = dnw

        @pl.when(pl.program_id(0) > 0)
        def _():
            dgb_ref[...] += dgb
            dnw_ref[...] += dnw

    sd = lambda n, dt=f32: jax.ShapeDtypeStruct((S, n), dt)
    out_specs = ([row(512), row(4608)] + [row(256)] * 6 + [row(768), full(1, 512), full(1, 768)]
                 + [row(512), row(512), row(256), row(768), row(D_MODEL), row(512)] + [row(D_MODEL)] * 3)
    out_shape = ([sd(512), sd(4608, bf16)] + [sd(256)] * 6 + [sd(768), jax.ShapeDtypeStruct((1, 512), f32),
                                                          jax.ShapeDtypeStruct((1, 768), f32)]
                 + [sd(512, bf16), sd(512, bf16), sd(256, bf16), sd(768, bf16), sd(D_MODEL, bf16), sd(512, bf16)]
                 + [sd(D_MODEL, bf16)] * 3)
    return _call(body, name, (S // tm,), acts + consts, out_specs, out_shape,
                 sem=("arbitrary",))(ys5, pt, *os_, *ls_, yssd, dout, glu_b, nw, *weights)


def _in_proj_segments(shards):
    dtype = shards[0].dtype

    def c(a, b):
        parts = []
        for k, sh in enumerate(shards):
            lo, hi = max(a, W_IN_SHARD * k), min(b, W_IN_SHARD * (k + 1))
            if lo < hi:
                parts.append(sh[:, lo - W_IN_SHARD * k:hi - W_IN_SHARD * k])
        return parts[0] if len(parts) == 1 else jnp.concatenate(parts, axis=1)

    atts = []
    for g in range(3):
        parts = []
        for hh in range(2):
            o = 64 * (4 * g + 2 * hh)
            parts += [c(_C_Q + o, _C_Q + o + 128), c(_C_K + o, _C_K + o + 128), c(_C_V + o, _C_V + o + 128)]
        atts.append(jnp.concatenate(parts, axis=1))
    ssd = jnp.concatenate([c(_C_XBC, _C_ZC), jnp.zeros((D_MODEL, 1536 - (_C_ZC - _C_XBC)), dtype)], axis=1)
    tail = jnp.concatenate([c(_C_GATE, _C_END), c(_C_ZA, _C_Q), c(_C_ZB, _C_XBC), c(_C_ZC, _C_GATE)], axis=1)
    return [c(_C_UA, _C_ZA)] + atts + [ssd, tail]


def _in_proj_grad(ds5, datts, dssd, dtail):
    pick = lambda off: [datts[g][:, 384 * hh + off:384 * hh + off + 128] for g in range(3) for hh in range(2)]
    pieces = ([ds5, dtail[:, 3072:3584]] + pick(0) + pick(128) + pick(256)
              + [dtail[:, 3584:3840], dssd[:, :_C_ZC - _C_XBC], dtail[:, 3840:4608], dtail[:, :3072]])
    shards, start = [[] for _ in range(4)], 0
    for piece in pieces:
        width = piece.shape[1]
        for k in range(4):
            lo, hi = max(start, W_IN_SHARD * k), min(start + width, W_IN_SHARD * (k + 1))
            if lo < hi:
                shards[k].append(piece[:, lo - start:hi - start])
        start += width
    return jnp.stack([jnp.concatenate(s, axis=1) for s in shards])


def _prep_layer(p):
    q = {}
    q["segs"] = [s.astype(bf16) for s in _in_proj_segments(p["w_in"])]
    disc = _s5_discretize(p["s5_a_re"], p["s5_a_im"], p["s5_log_step"], p["s5_b_re"], p["s5_b_im"],
                          p["s5_c_re"], p["s5_c_im"])
    q["s5"] = disc
    q["pw"] = _lam_powers(disc[0], disc[1])
    q["s5_d"] = p["s5_d"].reshape(1, 512)
    q["qw"] = jnp.tile(p["q_norm_w"], 2).reshape(1, LANES)
    q["kw"] = jnp.tile(p["k_norm_w"], 2).reshape(1, LANES)
    q["conv_w"] = p["conv_w"]
    q["conv_b"] = p["conv_b"].reshape(1, SSD_XBC)
    pad = lambda v: jnp.pad(v, (0, LANES - v.shape[0])).reshape(1, LANES)
    q["dt_bias"], q["a_log"] = pad(p["dt_bias"]), pad(p["ssd_a_log"])
    q["d_full"] = jnp.repeat(p["ssd_d"], 64).reshape(1, SSD_WIDTH)
    q["glu_b"] = p["s5_glu_b"].reshape(1, 512)
    q["nw"] = p["ssd_norm_w"].reshape(1, SSD_WIDTH)
    q["norm_w"] = p["norm_w"].reshape(1, D_MODEL)
    q["tailw"] = tuple(p[n].astype(bf16) for n in ("s5_glu_w", "proj_a", "proj_b", "proj_c", "w_out"))
    return q


_DILATIONS = (1, 4, 16)


def layer_fwd(x, q, tag):
    h = rms_fwd(x, q["norm_w"], f"rms_fwd{tag}")
    p_s5, p_a0, p_a1, p_a2, p_ssd, p_tail = [mm_nn(h, w, f"inproj{k}{tag}") for k, w in enumerate(q["segs"])]
    _, _, w_re, w_im, c_re, c_im = q["s5"]
    ys5, h_re, h_im = s5_fwd(p_s5, *q["pw"], w_re, w_im, c_re, c_im, q["s5_d"], f"s5_fwd{tag}")
    p_atts = (p_a0, p_a1, p_a2)
    os_, ls_ = [], []
    for g, d in enumerate(_DILATIONS):
        o, l = att_fwd(p_atts[g], q["qw"], q["kw"], d, f"att_fwd{g}{tag}")
        os_.append(o)
        ls_.append(l)
    xact = conv_fwd(p_ssd, q["conv_w"], q["conv_b"], f"conv_fwd{tag}")
    yssd, states = ssd_fwd(xact, p_ssd, q["dt_bias"], q["a_log"], q["d_full"], f"ssd_fwd{tag}")
    out = tail_fwd(ys5, p_tail, os_, ls_, yssd, x, q["glu_b"], q["nw"], q["tailw"], f"tail_fwd{tag}")
    saved = dict(x=x, h=h, p_s5=p_s5, p_atts=p_atts, p_ssd=p_ssd, p_tail=p_tail, ys5=ys5, h_re=h_re, h_im=h_im,
                 os=os_, ls=ls_, xact=xact, yssd=yssd, states=states)
    return out, saved


def layer_bwd(dout, sv, q, p, tag):
    S = dout.shape[0]
    (dys5, dp_tail, do0, do1, do2, dl0, dl1, dl2, dyssd, dglu_b, dnw, g_b, ya_b, yb_b, yc_b, mg_b, dglu_b16,
     dpa_b, dpb_b, dpc_b) = tail_bwd(sv["ys5"], sv["p_tail"], sv["os"], sv["ls"], sv["yssd"], dout, q["glu_b"],
                                     q["nw"], q["tailw"], f"tail_bwd{tag}")
    grads = {}
    grads["s5_glu_w"] = mm_tn(g_b, dglu_b16, f"dglu_w{tag}")
    grads["proj_a"] = mm_tn(ya_b, dpa_b, f"dproj_a{tag}")
    grads["proj_b"] = mm_tn(yb_b, dpb_b, f"dproj_b{tag}")
    grads["proj_c"] = mm_tn(yc_b, dpc_b, f"dproj_c{tag}")
    grads["w_out"] = mm_tn(mg_b, dout, f"dw_out{tag}")
    grads["s5_glu_b"] = dglu_b.reshape(512)
    grads["ssd_norm_w"] = dnw.reshape(SSD_WIDTH)

    dxact, ddt, ddt_bias, da_log, dd_full = ssd_bwd(sv["xact"], sv["p_ssd"], sv["states"], dyssd, q["dt_bias"],
                                                    q["a_log"], q["d_full"], f"ssd_bwd{tag}")
    dp_ssd, dconv_w, dconv_b = conv_bwd(sv["p_ssd"], dxact, ddt, q["conv_w"], q["conv_b"], f"conv_bwd{tag}")
    grads["dt_bias"] = ddt_bias[0, :12]
    grads["ssd_a_log"] = da_log[0, :12]
    grads["ssd_d"] = dd_full.reshape(12, 64).sum(axis=1)
    grads["conv_w"] = dconv_w
    grads["conv_b"] = dconv_b.reshape(SSD_XBC)

    dp_atts, dqw, dkw = [], 0.0, 0.0
    for g, d in enumerate(_DILATIONS):
        dp, a, b = att_bwd(sv["p_atts"][g], (do0, do1, do2)[g], (dl0, dl1, dl2)[g], q["qw"], q["kw"], d,
                           f"att_bwd{g}{tag}")
        dp_atts.append(dp)
        dqw, dkw = dqw + a, dkw + b
    grads["q_norm_w"] = dqw.reshape(2, 64).sum(axis=0)
    grads["k_norm_w"] = dkw.reshape(2, 64).sum(axis=0)

    _, _, w_re, w_im, c_re, c_im = q["s5"]
    dp_s5, dwre, dwim, dcre, dcim, dlam_re, dlam_im, dd = s5_bwd(
        dys5, sv["p_s5"], sv["h_re"], sv["h_im"], *q["pw"], w_re, w_im, c_re, c_im, q["s5_d"], f"s5_bwd{tag}")
    s5_names = ("s5_a_re", "s5_a_im", "s5_log_step", "s5_b_re", "s5_b_im", "s5_c_re", "s5_c_im")
    _, disc_vjp = jax.vjp(_s5_discretize, *[p[n] for n in s5_names])
    for n, gr in zip(s5_names, disc_vjp((dlam_re, dlam_im, dwre, dwim, dcre, dcim))):
        grads[n] = gr
    grads["s5_d"] = dd.reshape(512)

    dsegs = [dp_s5] + dp_atts + [dp_ssd, dp_tail]
    dws = [mm_tn(sv["h"], ds, f"dw_in{k}{tag}") for k, ds in enumerate(dsegs)]
    grads["w_in"] = _in_proj_grad(dws[0], dws[1:4], dws[4], dws[5])
    dh = None
    for k, (ds, w) in enumerate(zip(dsegs, q["segs"])):
        dh = mm_nt(ds, w, f"dh{k}{tag}", acc=dh)
    dx, dnorm_w = rms_bwd(sv["x"], q["norm_w"], dh, dout, f"rms_bwd{tag}")
    grads["norm_w"] = dnorm_w.reshape(D_MODEL)
    return dx, grads


_ANY = pl.BlockSpec(memory_space=pl.ANY)


def _chip_exchange(x, name, broadcast):
    shape = tuple(x.shape) if broadcast else tuple(x.shape[1:])

    def body(x_ref, o_ref, send_sems, recv_sems):
        mx, my, mc = lax.axis_index("x"), lax.axis_index("y"), lax.axis_index("c")
        me = 2 * mx + my
        copies = []
        for t, (px, py) in enumerate(((1 - mx, my), (mx, 1 - my), (1 - mx, 1 - my))):
            src = x_ref if broadcast else x_ref.at[2 * px + py]
            cp = pltpu.make_async_remote_copy(src_ref=src, dst_ref=o_ref.at[me], send_sem=send_sems.at[t],
                                              recv_sem=recv_sems.at[t], device_id=(px, py, mc),
                                              device_id_type=pl.DeviceIdType.MESH)
            cp.start()
            copies.append(cp)
        for cp in copies:
            cp.wait()

    landed = pl.pallas_call(
        body, name=name, in_specs=[_ANY], out_specs=_ANY, out_shape=jax.ShapeDtypeStruct((4,) + shape, x.dtype),
        scratch_shapes=[pltpu.SemaphoreType.DMA((3,)), pltpu.SemaphoreType.DMA((3,))],
    )(x)
    me = 2 * lax.axis_index("x") + lax.axis_index("y")
    own = x[None] if broadcast else lax.dynamic_index_in_dim(x, me, 0, keepdims=True)
    return lax.dynamic_update_index_in_dim(landed, own, me, 0)


def _sibling_exchange(xs, name, both=False):
    n = len(xs)

    def body(*refs):
        x_refs, o_refs, send_sems, recv_sems = refs[:n], refs[n:2 * n], refs[2 * n], refs[2 * n + 1]
        mc = lax.axis_index("c")
        peer = (lax.axis_index("x"), lax.axis_index("y"), 1 - mc)
        copies = []
        for t in range(n):
            cp = pltpu.make_async_remote_copy(src_ref=x_refs[t], dst_ref=o_refs[t].at[mc] if both else o_refs[t],
                                              send_sem=send_sems.at[t], recv_sem=recv_sems.at[t], device_id=peer,
                                              device_id_type=pl.DeviceIdType.MESH)
            cp.start()
            copies.append(cp)
        for cp in copies:
            cp.wait()

    lead = (2,) if both else ()
    outs = pl.pallas_call(
        body, name=name, in_specs=[_ANY] * n, out_specs=[_ANY] * n,
        out_shape=[jax.ShapeDtypeStruct(lead + tuple(x.shape), x.dtype) for x in xs],
        scratch_shapes=[pltpu.SemaphoreType.DMA((n,)), pltpu.SemaphoreType.DMA((n,))],
    )(*xs)
    if both:
        c = lax.axis_index("c")
        outs = [lax.dynamic_update_index_in_dim(o, x[None], c, 0) for o, x in zip(outs, xs)]
    return outs


def _rows_tile(rows, row_bytes, budget=1 << 20):
    return next(t for t in (512, 256, 128, 64, 32, 16, 8) if rows % t == 0 and t * row_bytes <= budget)


def _padded_row_bytes(cols):
    return -(-cols // LANES) * LANES * 4


def _add2(a, b, name, out_dtype=f32):
    R, C = a.shape
    tr = _rows_tile(R, _padded_row_bytes(C))

    def body(a_ref, b_ref, o_ref):
        o_ref[...] = (a_ref[...] + b_ref[...]).astype(out_dtype)

    spec = pl.BlockSpec((tr, C), lambda i: (i, 0))
    return _call(body, name, (R // tr,), [spec, spec], spec, jax.ShapeDtypeStruct((R, C), out_dtype),
                 sem=("parallel",))(a, b)


def _sum4(x, name):
    R = x.shape[1]
    tr = _tile(R, (512, 256, 128))

    def body(x_ref, o_ref):
        p = [x_ref[j].astype(f32) for j in range(4)]
        o_ref[...] = ((p[0] + p[1]) + p[2]) + p[3]

    return _call(body, name, (R // tr,), [pl.BlockSpec((4, tr, LANES), lambda i: (0, i, 0))],
                 pl.BlockSpec((tr, LANES), lambda i: (i, 0)), jax.ShapeDtypeStruct((R, LANES), f32),
                 sem=("parallel",))(x)


def _adamw(g_parts, w, m, v, name):
    stacked = not isinstance(g_parts, (tuple, list))
    k = g_parts.shape[0] if stacked else len(g_parts)
    R, C = w.shape
    tr = _rows_tile(R, _padded_row_bytes(C))
    c1 = 1.0 - ADAM_B1 ** ADAM_STEP
    c2 = 1.0 - ADAM_B2 ** ADAM_STEP

    def body(*refs):
        w_ref, m_ref, v_ref, g_ref, d_ref, nm_ref, nv_ref = refs[-7:]
        if stacked:
            g = refs[0][0].astype(f32)
            for j in range(1, k):
                g = g + refs[0][j].astype(f32)
        else:
            g = refs[0][...]
            for r in refs[1:k]:
                g = g + r[...]
        m = ADAM_B1 * m_ref[...] + (1.0 - ADAM_B1) * g
        v = ADAM_B2 * v_ref[...] + (1.0 - ADAM_B2) * (g * g)
        g_ref[...] = g
        nm_ref[...] = m
        nv_ref[...] = v
        d_ref[...] = -ADAM_LR * ((m / c1) / (jnp.sqrt(v / c2) + ADAM_EPS) + ADAM_WD * w_ref[...])

    spec = pl.BlockSpec((tr, C), lambda i: (i, 0))
    sd = jax.ShapeDtypeStruct((R, C), f32)
    g_specs = [pl.BlockSpec((k, tr, C), lambda i: (0, i, 0))] if stacked else [spec] * k
    g_args = [g_parts] if stacked else list(g_parts)
    return _call(body, name, (R // tr,), g_specs + [spec] * 3, [spec] * 4, [sd] * 4,
                 sem=("parallel",))(*g_args, w, m, v)


def _pack(arrays):
    flat = jnp.concatenate([a.reshape(-1) for a in arrays])
    unit = PACK_ROWS * LANES
    n = -(-flat.shape[0] // unit) * unit
    return jnp.pad(flat, (0, n - flat.shape[0])).reshape(n // LANES, LANES)


def _unpack(buf, shapes):
    flat = buf.reshape(-1)
    out, off = [], 0
    for s in shapes:
        n = 1
        for dim in s:
            n *= dim
        out.append(flat[off:off + n].reshape(s))
        off += n
    return out


def _to_shards(full, axis):
    s = full.shape
    t = full.reshape(s[:axis] + (4, s[axis] // 4) + s[axis + 1:])
    return jnp.moveaxis(t, axis, 0)


def _from_shards(sh, axis):
    t = jnp.moveaxis(sh, 0, axis)
    s = t.shape
    return t.reshape(s[:axis] + (s[axis] * s[axis + 1],) + s[axis + 2:])


def kernel(x, norm_w, w_in, s5_a_re, s5_a_im, s5_log_step, s5_b_re, s5_b_im, s5_c_re, s5_c_im, s5_d, s5_glu_w, s5_glu_b, q_norm_w, k_norm_w, conv_w, conv_b, dt_bias, ssd_a_log, ssd_d, ssd_norm_w, proj_a, proj_b, proj_c, w_out, loss_target, m_norm_w, m_w_in, m_s5_a_re, m_s5_a_im, m_s5_log_step, m_s5_b_re, m_s5_b_im, m_s5_c_re, m_s5_c_im, m_s5_d, m_s5_glu_w, m_s5_glu_b, m_q_norm_w, m_k_norm_w, m_conv_w, m_conv_b, m_dt_bias, m_ssd_a_log, m_ssd_d, m_ssd_norm_w, m_proj_a, m_proj_b, m_proj_c, m_w_out, v_norm_w, v_w_in, v_s5_a_re, v_s5_a_im, v_s5_log_step, v_s5_b_re, v_s5_b_im, v_s5_c_re, v_s5_c_im, v_s5_d, v_s5_glu_w, v_s5_glu_b, v_q_norm_w, v_k_norm_w, v_conv_w, v_conv_b, v_dt_bias, v_ssd_a_log, v_ssd_d, v_ssd_norm_w, v_proj_a, v_proj_b, v_proj_c, v_w_out):
    given = dict(locals())
    W = {n: given[n] for n in _WEIGHTS}
    M = {n: given["m_" + n] for n in _WEIGHTS}
    V = {n: given["v_" + n] for n in _WEIGHTS}
    n_layers = norm_w.shape[0]
    assert n_layers == 2
    c = lax.axis_index("c")

    mine_of = lambda t: lax.dynamic_index_in_dim(t, c, 0, keepdims=False)
    as_payload = lambda n: lax.bitcast_convert_type(W[n], bf16) if n == "conv_w" else W[n].astype(bf16)
    payload_shapes = [W[n].shape + ((2,) if n == "conv_w" else ()) for n, _ in _SHARDED]
    gathered = _chip_exchange(_pack([as_payload(n) for n, _ in _SHARDED]), "gather_weights", broadcast=True)
    w_in_layers, = _sibling_exchange(
        [_chip_exchange(mine_of(w_in).astype(bf16), "gather_w_in", broadcast=True)], "share_w_in", both=True)
    full = dict(W)
    pieces = [_unpack(gathered[j], payload_shapes) for j in range(4)]
    for k, (n, axis) in enumerate(_SHARDED):
        sh = jnp.stack([pieces[j][k] for j in range(4)])
        full[n] = _from_shards(lax.bitcast_convert_type(sh, f32) if n == "conv_w" else sh, axis)

    xs = x[0]
    qs, saves = [], []
    act = xs
    for l in range(n_layers):
        p = {n: full[n][l] for n in _WEIGHTS if n != "w_in"}
        p["w_in"] = [w_in_layers[l, k] for k in range(4)]
        q = _prep_layer(p)
        act, sv = layer_fwd(act, q, f"_l{l}")
        qs.append((q, p))
        saves.append(sv)
    dact, lsum = loss_and_grad(act, loss_target[0], "loss")
    loss = lax.psum(lsum[0, 0], ("x", "y", "c"))
    layer_grads = [None] * n_layers
    for l in reversed(range(n_layers)):
        q, p = qs[l]
        dact, layer_grads[l] = layer_bwd(dact, saves[l], q, p, f"_l{l}")
    grad_x = dact[None]
    G = {n: jnp.stack([layer_grads[l][n] for l in range(n_layers)]) for n in _WEIGHTS if n != "w_in"}

    g0, g1 = layer_grads[0]["w_in"], layer_grads[1]["w_in"]
    from_sibling, = _sibling_exchange([jnp.where(c == 0, g1, g0)], "swap_w_in_grads")
    flat = lambda t: t.reshape(4 * D_MODEL, W_IN_SHARD)
    shards = _add2(flat(jnp.where(c == 0, g0, g1)), flat(from_sibling), "sum_cores_w_in", out_dtype=bf16)
    landed = _chip_exchange(shards.reshape(4, D_MODEL, W_IN_SHARD), "scatter_w_in_grads", broadcast=False)
    w_in_mine = _adamw(landed, mine_of(w_in), mine_of(m_w_in), mine_of(v_w_in), "adamw_w_in")
    w_in_out = _sibling_exchange(w_in_mine, "share_w_in_updates", both=True)

    repl_shapes = [W[n].shape for n in _REPL]
    small = _pack([G[n] for n in _REPL])
    quarter = small.shape[0] // 4
    big = [_to_shards(G[n], axis).reshape(4, -1) for n, axis in _SHARDED]
    big = jnp.concatenate(big, axis=1)
    unit = PACK_ROWS * LANES
    nbig = -(-big.shape[1] // unit) * unit
    big = jnp.pad(big, ((0, 0), (0, nbig - big.shape[1]))).reshape(4, nbig // LANES, LANES)
    gpack = jnp.concatenate([big, small.reshape(4, quarter, LANES)], axis=1)
    mine = _sum4(_chip_exchange(gpack.astype(bf16), "scatter_grads", broadcast=False), "sum_chips")
    other, = _sibling_exchange([mine], "swap_cores")
    rbig = nbig // LANES

    wp, mp, vp = (_pack([T[n] for n, _ in _SHARDED]) for T in (W, M, V))
    outs_big = _adamw((mine[:rbig], other[:rbig]), wp, mp, vp, "adamw_sharded")
    big_out = [_unpack(o, [W[n].shape for n, _ in _SHARDED]) for o in outs_big]

    gq = _add2(mine[rbig:], other[rbig:], "sum_cores_small")
    gsmall = _chip_exchange(gq, "gather_small", broadcast=True).reshape(4 * quarter, LANES)
    ws, ms, vs = (_pack([T[n] for n in _REPL]) for T in (W, M, V))
    outs_small = _adamw((gsmall,), ws, ms, vs, "adamw_replicated")
    small_out = [_unpack(o, repl_shapes) for o in outs_small]

    res = [dict(), dict(), dict(), dict()]
    for kind in range(4):
        res[kind]["w_in"] = w_in_out[kind]
        for k, (n, _) in enumerate(_SHARDED):
            res[kind][n] = big_out[kind][k]
        for k, n in enumerate(_REPL):
            res[kind][n] = small_out[kind][k]
    return (loss, grad_x, *[res[0][n] for n in _WEIGHTS], *[res[1][n] for n in _WEIGHTS],
            *[res[2][n] for n in _WEIGHTS], *[res[3][n] for n in _WEIGHTS])
```

```python
import functools

import jax
import jax.numpy as jnp
from jax import lax
from jax.experimental import pallas as pl
from jax.experimental.pallas import tpu as pltpu

f32 = jnp.float32
bf16 = jnp.bfloat16

D_MODEL = 1024
RMS_EPS = 1e-6
V7X_VMEM_LIMIT = 60 * 1024 * 1024
LANES = 128
NN, NT, TN = ((1,), (0,)), ((1,), (1,)), ((0,), (0,))

S5_STATES = 2048
S5_ROWS = 256
ATT_SEG = 2048
ATT_BLOCK = 128
SSD_CHUNK = 128
SSD_WIDTH = 768
SSD_XBC = 1280
CONV_ROWS = 512
TAIL_ROWS = 128

ADAM_LR, ADAM_B1, ADAM_B2, ADAM_EPS, ADAM_WD, ADAM_STEP = 0.001, 0.9, 0.999, 1e-08, 0.01, 10

_C_UA, _C_ZA, _C_Q, _C_K, _C_V, _C_ZB, _C_XBC, _C_DT, _C_ZC, _C_GATE, _C_END = (
    0, 512, 1024, 1792, 2560, 3328, 3584, 4864, 4876, 5644, 8716)

_SHARDED = (("s5_glu_w", 1), ("conv_w", 2), ("proj_a", 2), ("proj_b", 2), ("proj_c", 2), ("w_out", 1))
W_IN_SHARD = 2179
_REPL = ("norm_w", "s5_a_re", "s5_a_im", "s5_log_step", "s5_b_re", "s5_b_im", "s5_c_re", "s5_c_im", "s5_d",
         "s5_glu_b", "q_norm_w", "k_norm_w", "conv_b", "dt_bias", "ssd_a_log", "ssd_d", "ssd_norm_w")
_WEIGHTS = ("norm_w", "w_in", "s5_a_re", "s5_a_im", "s5_log_step", "s5_b_re", "s5_b_im", "s5_c_re", "s5_c_im",
            "s5_d", "s5_glu_w", "s5_glu_b", "q_norm_w", "k_norm_w", "conv_w", "conv_b", "dt_bias", "ssd_a_log",
            "ssd_d", "ssd_norm_w", "proj_a", "proj_b", "proj_c", "w_out")
PACK_ROWS = 512


def _dot(a, b, dims):
    return lax.dot_general(a.astype(bf16), b.astype(bf16), (dims, ((), ())), preferred_element_type=f32)


def _call(body, name, grid, in_specs, out_specs, out_shape, scratch=(), sem=None):
    return pl.pallas_call(
        body, name=name, grid=grid, in_specs=in_specs, out_specs=out_specs, out_shape=out_shape,
        scratch_shapes=list(scratch),
        compiler_params=pltpu.CompilerParams(dimension_semantics=sem, vmem_limit_bytes=V7X_VMEM_LIMIT))


def _tile(n, options=(1024, 768, 512, 384, 256, 128)):
    return next(t for t in options if n % t == 0)


@functools.partial(jax.custom_vjp, nondiff_argnums=(2,))
def _bdot(a, b, dims):
    return _dot(a, b, dims)


def _bdot_fwd(a, b, dims):
    return _dot(a, b, dims), (a, b)


def _bdot_bwd(dims, res, g):
    a, b = res
    if dims == NN:
        da, db = _dot(g, b, NT), _dot(a, g, TN)
    elif dims == NT:
        da, db = _dot(g, b, NN), _dot(g, a, TN)
    else:
        da, db = _dot(b, g, NT), _dot(a, g, NN)
    return da.astype(a.dtype), db.astype(b.dtype)


_bdot.defvjp(_bdot_fwd, _bdot_bwd)


@functools.partial(jax.custom_vjp, nondiff_argnums=(2,))
def _cdot(a, w, dims):
    return _dot(a, w, dims)


def _cdot_fwd(a, w, dims):
    return _dot(a, w, dims), w


def _cdot_bwd(dims, w, g):
    da = _dot(g, w, NT) if dims == NN else _dot(g, w, NN)
    return da, jnp.zeros_like(w)


_cdot.defvjp(_cdot_fwd, _cdot_bwd)


def _split3(x):
    hi = x.astype(bf16)
    r = x - hi.astype(f32)
    mid = r.astype(bf16)
    lo = (r - mid.astype(f32)).astype(bf16)
    return hi, mid, lo


@jax.custom_vjp
def _xdot_l(m, x):
    return sum(_dot(m, p, NN) for p in _split3(x))


def _xdot_l_fwd(m, x):
    return _xdot_l(m, x), m


def _xdot_l_bwd(m, g):
    return jnp.zeros_like(m), sum(_dot(m, p, TN) for p in _split3(g))


_xdot_l.defvjp(_xdot_l_fwd, _xdot_l_bwd)


@jax.custom_vjp
def _softplus(x):
    e = jnp.exp(-jnp.abs(x))
    u = 1.0 + e
    log1p = jnp.where(u == 1.0, e, jnp.log(u) * (e / jnp.where(u == 1.0, 1.0, u - 1.0)))
    return jnp.maximum(x, 0.0) + log1p


def _softplus_fwd(x):
    return _softplus(x), x


def _softplus_bwd(x, g):
    return (g * jax.nn.sigmoid(x),)


_softplus.defvjp(_softplus_fwd, _softplus_bwd)


def _rms(x, w):
    return x * lax.rsqrt(jnp.mean(x * x, axis=-1, keepdims=True) + RMS_EPS) * w


def mm_nn(a, b, name, tm=1024):
    M, K = a.shape
    N = b.shape[1]
    tn = _tile(N)

    def body(a_ref, b_ref, o_ref):
        o_ref[...] = _dot(a_ref[...], b_ref[...], NN)

    return _call(body, name, (M // tm, N // tn),
                 [pl.BlockSpec((tm, K), lambda i, j: (i, 0)), pl.BlockSpec((K, tn), lambda i, j: (0, j))],
                 pl.BlockSpec((tm, tn), lambda i, j: (i, j)), jax.ShapeDtypeStruct((M, N), f32),
                 sem=("parallel", "parallel"))(a, b)


def mm_nt(a, b, name, acc=None, tm=1024):
    M, K = a.shape
    N = b.shape[0]
    tk = _tile(K)
    has_acc = acc is not None

    def body(*refs):
        a_ref, b_ref = refs[0], refs[1]
        o_ref = refs[-1]
        k = pl.program_id(1)
        p = _dot(a_ref[...], b_ref[...], NT)

        @pl.when(k == 0)
        def _():
            o_ref[...] = p + refs[2][...] if has_acc else p

        @pl.when(k > 0)
        def _():
            o_ref[...] += p

    specs = [pl.BlockSpec((tm, tk), lambda i, k: (i, k)), pl.BlockSpec((N, tk), lambda i, k: (0, k))]
    args = [a, b]
    if has_acc:
        specs.append(pl.BlockSpec((tm, N), lambda i, k: (i, 0)))
        args.append(acc)
    return _call(body, name, (M // tm, K // tk), specs, pl.BlockSpec((tm, N), lambda i, k: (i, 0)),
                 jax.ShapeDtypeStruct((M, N), f32), sem=("parallel", "arbitrary"))(*args)


def mm_tn(a, b, name, tk=1024):
    K, M = a.shape
    N = b.shape[1]
    tn = _tile(N)

    def body(a_ref, b_ref, o_ref):
        k = pl.program_id(1)
        p = _dot(a_ref[...], b_ref[...], TN)

        @pl.when(k == 0)
        def _():
            o_ref[...] = p

        @pl.when(k > 0)
        def _():
            o_ref[...] += p

    return _call(body, name, (N // tn, K // tk),
                 [pl.BlockSpec((tk, M), lambda j, k: (k, 0)), pl.BlockSpec((tk, tn), lambda j, k: (k, j))],
                 pl.BlockSpec((M, tn), lambda j, k: (0, j)), jax.ShapeDtypeStruct((M, N), f32),
                 sem=("parallel", "arbitrary"))(a, b)


def rms_fwd(x, w, name, tm=512):
    S = x.shape[0]

    def body(x_ref, w_ref, o_ref):
        o_ref[...] = _rms(x_ref[...], w_ref[...]).astype(bf16)

    return _call(body, name, (S // tm,),
                 [pl.BlockSpec((tm, D_MODEL), lambda i: (i, 0)), pl.BlockSpec((1, D_MODEL), lambda i: (0, 0))],
                 pl.BlockSpec((tm, D_MODEL), lambda i: (i, 0)), jax.ShapeDtypeStruct((S, D_MODEL), bf16),
                 sem=("parallel",))(x, w)


def rms_bwd(x, w, dh, dres, name, tm=512):
    S = x.shape[0]

    def body(x_ref, w_ref, dh_ref, dr_ref, dx_ref, dw_ref):
        _, vjp = jax.vjp(_rms, x_ref[...], w_ref[...])
        dx, dw = vjp(dh_ref[...])
        dx_ref[...] = dx + dr_ref[...]

        @pl.when(pl.program_id(0) == 0)
        def _():
            dw_ref[...] = dw

        @pl.when(pl.program_id(0) > 0)
        def _():
            dw_ref[...] += dw

    row = pl.BlockSpec((tm, D_MODEL), lambda i: (i, 0))
    vec = pl.BlockSpec((1, D_MODEL), lambda i: (0, 0))
    return _call(body, name, (S // tm,), [row, vec, row, row], [row, vec],
                 [jax.ShapeDtypeStruct((S, D_MODEL), f32), jax.ShapeDtypeStruct((1, D_MODEL), f32)],
                 sem=("arbitrary",))(x, w, dh, dres)


def loss_and_grad(y, target, name, tm=512):
    S = y.shape[0]

    def body(y_ref, t_ref, dy_ref, l_ref):
        diff = y_ref[...] - t_ref[...]
        dy_ref[...] = diff * (1.0 / D_MODEL)
        part = jnp.full((8, LANES), 0.5 / D_MODEL * jnp.sum(diff * diff), f32)

        @pl.when(pl.program_id(0) == 0)
        def _():
            l_ref[...] = part

        @pl.when(pl.program_id(0) > 0)
        def _():
            l_ref[...] += part

    row = pl.BlockSpec((tm, D_MODEL), lambda i: (i, 0))
    return _call(body, name, (S // tm,), [row, row], [row, pl.BlockSpec((8, LANES), lambda i: (0, 0))],
                 [jax.ShapeDtypeStruct((S, D_MODEL), f32), jax.ShapeDtypeStruct((8, LANES), f32)],
                 sem=("arbitrary",))(y, target)


def _s5_discretize(a_re, a_im, log_step, b_re, b_im, c_re, c_im):
    step = jnp.exp(log_step)[:, None]
    mag = jnp.exp(a_re * step)
    ang = a_im * step
    lam_re, lam_im = mag * jnp.cos(ang), mag * jnp.sin(ang)
    num_re, num_im = lam_re - 1.0, lam_im
    den = a_re * a_re + a_im * a_im
    f_re = (num_re * a_re + num_im * a_im) / den
    f_im = (num_im * a_re - num_re * a_im) / den
    bb_re = f_re[..., None] * b_re - f_im[..., None] * b_im
    bb_im = f_re[..., None] * b_im + f_im[..., None] * b_re
    eye = jnp.eye(8, dtype=f32)

    def block_in(bb):
        t = bb.transpose(0, 2, 1).reshape(4, 8, 16, 1, 64)
        return (t * eye[None, :, None, :, None]).reshape(4, 128, 512)

    def block_out(c):
        t = c.transpose(0, 2, 1).reshape(4, 8, 64, 1, 16)
        return (t * eye[None, :, None, :, None]).reshape(4, 512, 128)

    return (lam_re.reshape(1, S5_STATES), lam_im.reshape(1, S5_STATES), block_in(bb_re), block_in(bb_im),
            block_out(c_re), block_out(c_im))


def _lam_powers(lam_re, lam_im):
    rows_re, rows_im = [lam_re], [lam_im]
    for _ in range(7):
        pr, pi = rows_re[-1], rows_im[-1]
        rows_re.append(pr * lam_re - pi * lam_im)
        rows_im.append(pr * lam_im + pi * lam_re)
    return jnp.concatenate(rows_re, 0), jnp.concatenate(rows_im, 0)


def s5_fwd(u, pw_re, pw_im, w_re, w_im, c_re, c_im, dvec, name):
    S = u.shape[0]
    R, NS = S5_ROWS, S5_STATES
    nb = R // 8

    def body(u_ref, pwr_ref, pwi_ref, wre_ref, wim_ref, cre_ref, cim_ref, d_ref, y_ref, hr_ref, hi_ref,
             car_re, car_im, cin_re, cin_im, up, yp):
        @pl.when(pl.program_id(0) == 0)
        def _():
            car_re[...] = jnp.zeros_like(car_re)
            car_im[...] = jnp.zeros_like(car_im)

        slab = lambda r: pl.ds(r * nb, nb)
        for r in range(8):
            up[slab(r), :] = u_ref[:, r, :]
        u = up[...]
        for j in range(4):
            uj = u[:, 128 * j:128 * (j + 1)]
            hr_ref[:, 512 * j:512 * (j + 1)] = _dot(uj, wre_ref[j], NN)
            hi_ref[:, 512 * j:512 * (j + 1)] = _dot(uj, wim_ref[j], NN)
        lr, li = pwr_ref[0:1, :], pwi_ref[0:1, :]
        for r in range(1, 8):
            pr, pi = hr_ref[slab(r - 1), :], hi_ref[slab(r - 1), :]
            hr_ref[slab(r), :] = lr * pr - li * pi + hr_ref[slab(r), :]
            hi_ref[slab(r), :] = lr * pi + li * pr + hi_ref[slab(r), :]
        l8r, l8i = pwr_ref[7:8, :], pwi_ref[7:8, :]

        def across(c, carry):
            gr, gi = carry
            cin_re[pl.ds(c, 1), :] = gr
            cin_im[pl.ds(c, 1), :] = gi
            er, ei = hr_ref[pl.ds(7 * nb + c, 1), :], hi_ref[pl.ds(7 * nb + c, 1), :]
            return l8r * gr - l8i * gi + er, l8r * gi + l8i * gr + ei

        gr, gi = lax.fori_loop(0, nb, across, (car_re[...], car_im[...]))
        car_re[...] = gr
        car_im[...] = gi
        cr, ci = cin_re[...], cin_im[...]
        for r in range(8):
            pr, pi = pwr_ref[r:r + 1, :], pwi_ref[r:r + 1, :]
            hr_ref[slab(r), :] = hr_ref[slab(r), :] + pr * cr - pi * ci
            hi_ref[slab(r), :] = hi_ref[slab(r), :] + pr * ci + pi * cr
        for j in range(4):
            sl = slice(512 * j, 512 * (j + 1))
            cs = slice(128 * j, 128 * (j + 1))
            yp[:, cs] = (_dot(hr_ref[:, sl], cre_ref[j], NN) - _dot(hi_ref[:, sl], cim_ref[j], NN)
                         + d_ref[:, cs] * u[:, cs])
        for r in range(8):
            y_ref[:, r, :] = yp[slab(r), :]

    full = lambda shape: pl.BlockSpec(shape, lambda i: (0,) * len(shape))
    hspec = pl.BlockSpec((R, NS), lambda i: (i, 0))
    uspec = pl.BlockSpec((nb, 8, 512), lambda i: (i, 0, 0))
    y, h_re, h_im = _call(
        body, name, (S // R,),
        [uspec, full((8, NS)), full((8, NS)), full((4, 128, 512)),
         full((4, 128, 512)), full((4, 512, 128)), full((4, 512, 128)), full((1, 512))],
        [uspec, hspec, hspec],
        [jax.ShapeDtypeStruct((S // 8, 8, 512), f32), jax.ShapeDtypeStruct((S, NS), f32),
         jax.ShapeDtypeStruct((S, NS), f32)],
        scratch=[pltpu.VMEM((1, NS), f32), pltpu.VMEM((1, NS), f32), pltpu.VMEM((nb, NS), f32),
                 pltpu.VMEM((nb, NS), f32), pltpu.VMEM((R, 512), f32), pltpu.VMEM((R, 512), f32)],
        sem=("arbitrary",))(u.reshape(S // 8, 8, 512), pw_re, pw_im, w_re.astype(bf16), w_im.astype(bf16),
                            c_re.astype(bf16), c_im.astype(bf16), dvec)
    return y.reshape(S, 512), h_re, h_im


def s5_bwd(dy, u, h_re, h_im, pw_re, pw_im, w_re, w_im, c_re, c_im, dvec, name):
    S = u.shape[0]
    R, NS = S5_ROWS, S5_STATES
    nb = R // 8
    nchunk = S // R

    def body(dy_ref, u_ref, hr_ref, hi_ref, hpr_ref, hpi_ref, pwr_ref, pwi_ref, wre_ref, wim_ref, cre_ref, cim_ref,
             d_ref, du_ref, dwre_ref, dwim_ref, dcre_ref, dcim_ref, dlr_ref, dli_ref, dd_ref,
             ar, ai, car_re, car_im, cin_re, cin_im, up, dyp, dup):
        i = pl.program_id(0)

        @pl.when(i == 0)
        def _():
            for ref in (car_re, car_im, dwre_ref, dwim_ref, dcre_ref, dcim_ref, dlr_ref, dli_ref, dd_ref):
                ref[...] = jnp.zeros_like(ref)

        slab = lambda r: pl.ds(r * nb, nb)
        for r in range(8):
            up[slab(r), :] = u_ref[:, r, :]
            dyp[slab(r), :] = dy_ref[:, r, :]
        dy = dyp[...]
        u = up[...]
        for j in range(4):
            dyj = dy[:, 128 * j:128 * (j + 1)]
            ar[:, 512 * j:512 * (j + 1)] = _dot(dyj, cre_ref[j], NT)
            ai[:, 512 * j:512 * (j + 1)] = -_dot(dyj, cim_ref[j], NT)
        lr, li = pwr_ref[0:1, :], pwi_ref[0:1, :]
        for r in range(6, -1, -1):
            nr, ni = ar[slab(r + 1), :], ai[slab(r + 1), :]
            ar[slab(r), :] = lr * nr + li * ni + ar[slab(r), :]
            ai[slab(r), :] = lr * ni - li * nr + ai[slab(r), :]
        l8r, l8i = pwr_ref[7:8, :], pwi_ref[7:8, :]

        def across(k, carry):
            c = nb - 1 - k
            gr, gi = carry
            cin_re[pl.ds(c, 1), :] = gr
            cin_im[pl.ds(c, 1), :] = gi
            er, ei = ar[pl.ds(c, 1), :], ai[pl.ds(c, 1), :]
            return l8r * gr + l8i * gi + er, l8r * gi - l8i * gr + ei

        gr, gi = lax.fori_loop(0, nb, across, (car_re[...], car_im[...]))
        car_re[...] = gr
        car_im[...] = gi
        cr, ci = cin_re[...], cin_im[...]
        for r in range(8):
            pr, pi = pwr_ref[7 - r:8 - r, :], pwi_ref[7 - r:8 - r, :]
            ar[slab(r), :] = ar[slab(r), :] + pr * cr + pi * ci
            ai[slab(r), :] = ai[slab(r), :] + pr * ci - pi * cr

        acc_r = jnp.zeros((1, NS), f32)
        acc_i = jnp.zeros((1, NS), f32)
        has_prev = (i < nchunk - 1).astype(f32)
        top = lax.broadcasted_iota(jnp.int32, (nb, NS), 0) == 0
        for r in range(8):
            if r == 0:
                xr = jnp.where(top, hpr_ref[7:8, :] * has_prev, pltpu.roll(hr_ref[slab(7), :], 1, 0))
                xi = jnp.where(top, hpi_ref[7:8, :] * has_prev, pltpu.roll(hi_ref[slab(7), :], 1, 0))
            else:
                xr, xi = hr_ref[slab(r - 1), :], hi_ref[slab(r - 1), :]
            br, bi = ar[slab(r), :], ai[slab(r), :]
            acc_r += jnp.sum(br * xr + bi * xi, axis=0, keepdims=True)
            acc_i += jnp.sum(bi * xr - br * xi, axis=0, keepdims=True)
        dlr_ref[...] += acc_r
        dli_ref[...] += acc_i
        dd_ref[...] += jnp.sum(dy * u, axis=0, keepdims=True)

        for j in range(4):
            sl = slice(512 * j, 512 * (j + 1))
            cs = slice(128 * j, 128 * (j + 1))
            arj, aij = ar[:, sl], ai[:, sl]
            uj, dyj = u[:, cs], dy[:, cs]
            dup[:, cs] = _dot(arj, wre_ref[j], NT) + _dot(aij, wim_ref[j], NT) + d_ref[:, cs] * dyj
            dwre_ref[j] += _dot(uj, arj, TN)
            dwim_ref[j] += _dot(uj, aij, TN)
            dcre_ref[j] += _dot(hr_ref[:, sl], dyj, TN)
            dcim_ref[j] -= _dot(hi_ref[:, sl], dyj, TN)
        for r in range(8):
            du_ref[:, r, :] = dup[slab(r), :]

    rev = lambda i: nchunk - 1 - i
    full = lambda shape: pl.BlockSpec(shape, lambda i: (0,) * len(shape))
    row = pl.BlockSpec((nb, 8, 512), lambda i: (rev(i), 0, 0))
    hspec = pl.BlockSpec((R, NS), lambda i: (rev(i), 0))
    hprev = pl.BlockSpec((8, NS), lambda i: (jnp.maximum(rev(i) * nb - 1, 0), 0))
    outs = _call(
        body, name, (nchunk,),
        [row, row, hspec, hspec, hprev, hprev, full((8, NS)), full((8, NS)), full((4, 128, 512)), full((4, 128, 512)),
         full((4, 512, 128)), full((4, 512, 128)), full((1, 512))],
        [row, full((4, 128, 512)), full((4, 128, 512)), full((4, 512, 128)), full((4, 512, 128)),
         full((1, NS)), full((1, NS)), full((1, 512))],
        [jax.ShapeDtypeStruct((S // 8, 8, 512), f32), jax.ShapeDtypeStruct((4, 128, 512), f32),
         jax.ShapeDtypeStruct((4, 128, 512), f32), jax.ShapeDtypeStruct((4, 512, 128), f32),
         jax.ShapeDtypeStruct((4, 512, 128), f32), jax.ShapeDtypeStruct((1, NS), f32),
         jax.ShapeDtypeStruct((1, NS), f32), jax.ShapeDtypeStruct((1, 512), f32)],
        scratch=[pltpu.VMEM((R, NS), f32), pltpu.VMEM((R, NS), f32), pltpu.VMEM((1, NS), f32),
                 pltpu.VMEM((1, NS), f32), pltpu.VMEM((nb, NS), f32), pltpu.VMEM((nb, NS), f32),
                 pltpu.VMEM((R, 512), f32), pltpu.VMEM((R, 512), f32), pltpu.VMEM((R, 512), f32)],
        sem=("arbitrary",))(dy.reshape(S // 8, 8, 512), u.reshape(S // 8, 8, 512), h_re, h_im, h_re, h_im, pw_re,
                            pw_im, w_re.astype(bf16), w_im.astype(bf16), c_re.astype(bf16), c_im.astype(bf16), dvec)
    return (outs[0].reshape(S, 512),) + tuple(outs[1:])


def _rows(start, n, d):
    return pl.ds(pl.multiple_of(start, ATT_BLOCK), n) if d == 1 else pl.ds(start, n, stride=d)


def _head_masks():
    lane = lax.broadcasted_iota(jnp.int32, (1, LANES), 1)
    return [(lane < 64).astype(f32), (lane >= 64).astype(f32)]


def _head_norm(x, w, hm):
    x2 = x * x
    r = [lax.rsqrt(jnp.sum(x2 * hm[h], axis=-1, keepdims=True) * (1.0 / 64) + RMS_EPS) for h in range(2)]
    sc = hm[0] * r[0] + hm[1] * r[1]
    return x * sc * w, sc, r


def _head_norm_bwd(x, w, sc, r, dxn, hm):
    dw = jnp.sum(dxn * x * sc, axis=0, keepdims=True)
    t = dxn * w
    tx = t * x
    corr = sum(hm[h] * (r[h] * r[h] * r[h]) * jnp.sum(tx * hm[h], axis=-1, keepdims=True) for h in range(2))
    return t * sc - x * corr * (1.0 / 64), dw


def _att_mask(has_prev):
    qi = lax.broadcasted_iota(jnp.int32, (ATT_BLOCK, 2 * ATT_BLOCK), 0) + ATT_BLOCK
    kj = lax.broadcasted_iota(jnp.int32, (ATT_BLOCK, 2 * ATT_BLOCK), 1)
    return (qi - kj >= 0) & (qi - kj <= ATT_BLOCK) & (has_prev | (kj >= ATT_BLOCK))


def _att_block_bwd(q, k, v, o, lse, do, dlse, qw, kw, has_prev):
    hm = _head_masks()
    mask = _att_mask(has_prev)
    qn, qsc, qr = _head_norm(q, qw, hm)
    kn, ksc, kr = _head_norm(k, kw, hm)
    dqn = jnp.zeros((ATT_BLOCK, LANES), f32)
    dkn = jnp.zeros((2 * ATT_BLOCK, LANES), f32)
    dv = jnp.zeros((2 * ATT_BLOCK, LANES), f32)
    for h in range(2):
        qh, do_h = qn * hm[h], do * hm[h]
        s = _dot(qh, kn, NT) * 0.125
        p = jnp.exp(jnp.where(mask, s - lse[:, 64 * h:64 * h + 1], -jnp.inf))
        dp = _dot(do_h, v, NT)
        delta = jnp.sum(do_h * o, axis=-1, keepdims=True)
        dl = jnp.sum(dlse * hm[h], axis=-1, keepdims=True)
        ds = p * (dp - delta + dl) * 0.125
        dqn = dqn + hm[h] * _dot(ds, kn, NN)
        dkn = dkn + _dot(ds, qh, TN)
        dv = dv + _dot(p, do_h, TN)
    dq, dqw = _head_norm_bwd(q, qw, qsc, qr, dqn, hm)
    dk, dkw = _head_norm_bwd(k, kw, ksc, kr, dkn, hm)
    return dq, dk, dv, dqw, dkw


def _att_block(q, k, v, qw, kw, has_prev):
    hm = _head_masks()
    qn, kn = _head_norm(q, qw, hm)[0], _head_norm(k, kw, hm)[0]
    mask = _att_mask(has_prev)
    o = jnp.zeros((ATT_BLOCK, LANES), f32)
    lse = jnp.zeros((ATT_BLOCK, LANES), f32)
    for h in range(2):
        s = _bdot(qn * hm[h], kn, NT) * 0.125
        s = jnp.where(mask, s, -jnp.inf)
        m = jnp.max(s, axis=-1, keepdims=True)
        p = jnp.exp(s - m)
        l = jnp.sum(p, axis=-1, keepdims=True)
        o = o + hm[h] * _bdot(p / l, v, NN)
        lse = lse + hm[h] * (m + jnp.log(l))
    return o, lse


def att_fwd(p_att, qw, kw, d, name):
    S = p_att.shape[0]
    SEG = ATT_SEG
    nblk = SEG // ATT_BLOCK

    def body(p_ref, qw_ref, kw_ref, o_ref, l_ref, q_s, k_ext, v_ext, o_s, l_s):
        seg = pl.program_id(1)

        @pl.when(seg == 0)
        def _():
            k_ext[SEG:, :] = jnp.zeros((SEG, LANES), f32)
            v_ext[SEG:, :] = jnp.zeros((SEG, LANES), f32)

        k_ext[:SEG, :] = k_ext[SEG:, :]
        v_ext[:SEG, :] = v_ext[SEG:, :]
        q_s[...] = p_ref[:, 0:128]
        k_ext[SEG:, :] = p_ref[:, 128:256]
        v_ext[SEG:, :] = p_ref[:, 256:384]
        qw_v, kw_v = qw_ref[...], kw_ref[...]

        def blk(b, carry):
            j, r = b // d, b % d
            qs = j * (ATT_BLOCK * d) + r
            ks = SEG + qs - ATT_BLOCK * d
            o, lse = _att_block(q_s[_rows(qs, ATT_BLOCK, d), :], k_ext[_rows(ks, 2 * ATT_BLOCK, d), :],
                                v_ext[_rows(ks, 2 * ATT_BLOCK, d), :], qw_v, kw_v, (seg > 0) | (j > 0))
            o_s[_rows(qs, ATT_BLOCK, d), :] = o
            l_s[_rows(qs, ATT_BLOCK, d), :] = lse
            return carry

        lax.fori_loop(0, nblk, blk, 0, unroll=4)
        o_ref[...] = o_s[...]
        l_ref[...] = l_s[...]

    vec = pl.BlockSpec((1, LANES), lambda hh, s: (0, 0))
    out = pl.BlockSpec((SEG, LANES), lambda hh, s: (s, hh))
    return _call(body, name, (2, S // SEG), [pl.BlockSpec((SEG, 384), lambda hh, s: (s, hh)), vec, vec],
                 [out, out], [jax.ShapeDtypeStruct((S, 256), f32), jax.ShapeDtypeStruct((S, 256), f32)],
                 scratch=[pltpu.VMEM((SEG, LANES), f32), pltpu.VMEM((2 * SEG, LANES), f32),
                          pltpu.VMEM((2 * SEG, LANES), f32), pltpu.VMEM((SEG, LANES), f32),
                          pltpu.VMEM((SEG, LANES), f32)],
                 sem=("arbitrary", "arbitrary"))(p_att, qw, kw)


def att_bwd(p_att, o, lse, do, dlse, qw, kw, d, name):
    S = p_att.shape[0]
    SEG = ATT_SEG
    nseg = S // SEG
    nblk = SEG // ATT_BLOCK

    def body(p_ref, pp_ref, o_ref, l_ref, do_ref, dl_ref, qw_ref, kw_ref, dp_ref, dqw_ref, dkw_ref,
             q_s, k_ext, v_ext, dq_s, dk_ext, dv_ext):
        hh, i = pl.program_id(0), pl.program_id(1)
        seg = nseg - 1 - i

        @pl.when(i == 0)
        def _():
            dk_ext[...] = jnp.zeros_like(dk_ext)
            dv_ext[...] = jnp.zeros_like(dv_ext)

        @pl.when((i == 0) & (hh == 0))
        def _():
            dqw_ref[...] = jnp.zeros_like(dqw_ref)
            dkw_ref[...] = jnp.zeros_like(dkw_ref)

        dk_ext[SEG:, :] = dk_ext[:SEG, :]
        dv_ext[SEG:, :] = dv_ext[:SEG, :]
        dk_ext[:SEG, :] = jnp.zeros((SEG, LANES), f32)
        dv_ext[:SEG, :] = jnp.zeros((SEG, LANES), f32)
        q_s[...] = p_ref[:, 0:128]
        k_ext[SEG:, :] = p_ref[:, 128:256]
        v_ext[SEG:, :] = p_ref[:, 256:384]
        k_ext[:SEG, :] = pp_ref[:, 128:256]
        v_ext[:SEG, :] = pp_ref[:, 256:384]
        qw_v, kw_v = qw_ref[...], kw_ref[...]

        def blk_pair(i2, carry):
            dqw, dkw = carry
            done = []
            for u in range(2):
                b = 2 * i2 + u
                j, r = b // d, b % d
                qs = j * (ATT_BLOCK * d) + r
                ks = SEG + qs - ATT_BLOCK * d
                has_prev = (seg > 0) | (j > 0)
                qrows, krows = _rows(qs, ATT_BLOCK, d), _rows(ks, 2 * ATT_BLOCK, d)
                dq, dk, dv, dqw_b, dkw_b = _att_block_bwd(
                    q_s[qrows, :], k_ext[krows, :], v_ext[krows, :], o_ref[qrows, :], l_ref[qrows, :],
                    do_ref[qrows, :], dl_ref[qrows, :], qw_v, kw_v, has_prev)
                dqw, dkw = dqw + dqw_b, dkw + dkw_b
                done.append((qrows, krows, dq, dk, dv))
            for qrows, krows, dq, dk, dv in done:
                dq_s[qrows, :] = dq
                dk_ext[krows, :] = dk_ext[krows, :] + dk
                dv_ext[krows, :] = dv_ext[krows, :] + dv
            return dqw, dkw

        zero = jnp.zeros((1, LANES), f32)
        dqw, dkw = lax.fori_loop(0, nblk // 2, blk_pair, (zero, zero))
        dqw_ref[...] += dqw
        dkw_ref[...] += dkw
        dp_ref[:, 0:128] = dq_s[...].astype(bf16)
        dp_ref[:, 128:256] = dk_ext[SEG:, :].astype(bf16)
        dp_ref[:, 256:384] = dv_ext[SEG:, :].astype(bf16)

    rev = lambda i: nseg - 1 - i
    vec = pl.BlockSpec((1, LANES), lambda hh, i: (0, 0))
    cur = pl.BlockSpec((SEG, 384), lambda hh, i: (rev(i), hh))
    prev = pl.BlockSpec((SEG, 384), lambda hh, i: (jnp.maximum(rev(i) - 1, 0), hh))
    col = pl.BlockSpec((SEG, LANES), lambda hh, i: (rev(i), hh))
    big = pltpu.VMEM((2 * SEG, LANES), f32)
    one = pltpu.VMEM((SEG, LANES), f32)
    return _call(body, name, (2, nseg), [cur, prev, col, col, col, col, vec, vec], [cur, vec, vec],
                 [jax.ShapeDtypeStruct((S, 768), bf16), jax.ShapeDtypeStruct((1, LANES), f32),
                  jax.ShapeDtypeStruct((1, LANES), f32)],
                 scratch=[one, big, big, one, big, big],
                 sem=("arbitrary", "arbitrary"))(p_att, p_att, o, lse, do, dlse, qw, kw)


def conv_fwd(p_ssd, conv_w, conv_b, name):
    S = p_ssd.shape[0]
    tm, C = CONV_ROWS, SSD_XBC

    def body(x_ref, xp_ref, w_ref, b_ref, o_ref, ext):
        first = (pl.program_id(0) == 0)
        ext[0:8, :] = jnp.where(first, 0.0, xp_ref[:, 0:C])
        ext[8:, :] = x_ref[:, 0:C]
        acc = b_ref[...] + w_ref[3:4, :] * ext[pl.ds(8, tm), :]
        for k in range(1, 4):
            acc = acc + w_ref[3 - k:4 - k, :] * ext[pl.ds(8 - k, tm), :]
        o_ref[...] = jax.nn.silu(acc)

    return _call(body, name, (S // tm,),
                 [pl.BlockSpec((tm, 1536), lambda i: (i, 0)),
                  pl.BlockSpec((8, 1536), lambda i: (jnp.maximum(i * (tm // 8) - 1, 0), 0)),
                  pl.BlockSpec((4, C), lambda i: (0, 0)), pl.BlockSpec((1, C), lambda i: (0, 0))],
                 pl.BlockSpec((tm, C), lambda i: (i, 0)), jax.ShapeDtypeStruct((S, C), f32),
                 scratch=[pltpu.VMEM((tm + 8, C), f32)], sem=("parallel",))(p_ssd, p_ssd, conv_w, conv_b)


def conv_bwd(p_ssd, dact, ddt, conv_w, conv_b, name):
    S = p_ssd.shape[0]
    tm, C = CONV_ROWS, SSD_XBC
    nblk = S // tm

    def body(x_ref, xp_ref, xn_ref, da_ref, dan_ref, ddt_ref, w_ref, b_ref, dp_ref, dw_ref, db_ref, ext, dpre):
        i = pl.program_id(0)
        ext[0:8, :] = jnp.where(i == 0, 0.0, xp_ref[:, 0:C])
        ext[8:tm + 8, :] = x_ref[:, 0:C]
        ext[tm + 8:, :] = xn_ref[:, 0:C]
        pre = b_ref[...] + w_ref[3:4, :] * ext[pl.ds(8, tm + 8), :]
        for k in range(1, 4):
            pre = pre + w_ref[3 - k:4 - k, :] * ext[pl.ds(8 - k, tm + 8), :]
        sg = jax.nn.sigmoid(pre)
        dsilu = sg * (1.0 + pre * (1.0 - sg))
        dpre[0:tm, :] = da_ref[...] * dsilu[0:tm, :]
        dpre[tm:, :] = jnp.where(i == nblk - 1, 0.0, dan_ref[...] * dsilu[tm:, :])
        dx = w_ref[3:4, :] * dpre[pl.ds(0, tm), :]
        for k in range(1, 4):
            dx = dx + w_ref[3 - k:4 - k, :] * dpre[pl.ds(k, tm), :]
        dp_ref[:, 0:C] = dx.astype(bf16)
        dp_ref[:, C:C + 128] = ddt_ref[...].astype(bf16)
        dp_ref[:, C + 128:] = jnp.zeros((tm, 128), bf16)
        dcur = dpre[pl.ds(0, tm), :]
        dws = [jnp.sum(dcur * ext[pl.ds(8 - (3 - j), tm), :], axis=0, keepdims=True) for j in range(4)]
        dbs = jnp.sum(dcur, axis=0, keepdims=True)

        @pl.when(i == 0)
        def _():
            dw_ref[...] = jnp.zeros_like(dw_ref)
            db_ref[...] = jnp.zeros_like(db_ref)

        for j in range(4):
            dw_ref[j:j + 1, :] += dws[j]
        db_ref[...] += dbs

    t8 = tm // 8
    return _call(body, name, (nblk,),
                 [pl.BlockSpec((tm, 1536), lambda i: (i, 0)),
                  pl.BlockSpec((8, 1536), lambda i: (jnp.maximum(i * t8 - 1, 0), 0)),
                  pl.BlockSpec((8, 1536), lambda i: (jnp.minimum((i + 1) * t8, S // 8 - 1), 0)),
                  pl.BlockSpec((tm, C), lambda i: (i, 0)),
                  pl.BlockSpec((8, C), lambda i: (jnp.minimum((i + 1) * t8, S // 8 - 1), 0)),
                  pl.BlockSpec((tm, 128), lambda i: (i, 0)),
                  pl.BlockSpec((4, C), lambda i: (0, 0)), pl.BlockSpec((1, C), lambda i: (0, 0))],
                 [pl.BlockSpec((tm, 1536), lambda i: (i, 0)), pl.BlockSpec((4, C), lambda i: (0, 0)),
                  pl.BlockSpec((1, C), lambda i: (0, 0))],
                 [jax.ShapeDtypeStruct((S, 1536), bf16), jax.ShapeDtypeStruct((4, C), f32),
                  jax.ShapeDtypeStruct((1, C), f32)],
                 scratch=[pltpu.VMEM((tm + 16, C), f32), pltpu.VMEM((tm + 8, C), f32)],
                 sem=("arbitrary",))(p_ssd, p_ssd, p_ssd, dact, dact, ddt, conv_w, conv_b)


def _ssd_chunk(xbc, dtr, state, dt_bias, a_log, d_full):
    T = SSD_CHUNK
    r_i = lax.broadcasted_iota(jnp.int32, (T, T), 0)
    c_i = lax.broadcasted_iota(jnp.int32, (T, T), 1)
    tril = c_i <= r_i
    tri = tril.astype(bf16)
    lane = lax.broadcasted_iota(jnp.int32, (1, LANES), 1)
    hm = [(lane < 64).astype(f32), (lane >= 64).astype(f32)]
    column = lambda v, h: jnp.broadcast_to(v[:, h:h + 1], (T, LANES))

    def per_head_lanes(v):
        return jnp.concatenate([jnp.where(lane < 64, column(v, 2 * pp), column(v, 2 * pp + 1)) for pp in range(6)],
                               axis=1)

    xs, bm, cm = xbc[:, :768], xbc[:, 768:1024], xbc[:, 1024:1280]
    dt = _softplus(dtr + dt_bias)
    a_dt = dt * (-jnp.exp(a_log))
    a_cs = _xdot_l(tri, a_dt)
    dt_full = per_head_lanes(dt)
    acs_full = per_head_lanes(a_cs)
    last = lax.broadcasted_iota(jnp.int32, (T, SSD_WIDTH), 0) == T - 1
    tot_full = jnp.sum(jnp.where(last, acs_full, 0.0), axis=0, keepdims=True)
    xdt = xs * dt_full
    xw = xdt * jnp.exp(tot_full - acs_full)
    eacs = jnp.exp(acs_full)
    st_parts, off_parts, diag_parts = [], [], []
    for g in range(2):
        bg, cg = bm[:, 128 * g:128 * (g + 1)], cm[:, 128 * g:128 * (g + 1)]
        cols = slice(384 * g, 384 * (g + 1))
        st_parts.append(_bdot(bg, xw[:, cols], TN))
        off_parts.append(_bdot(cg, state[:, cols], NN))
        cb = _bdot(cg, bg, NT)
        for pp in range(3 * g, 3 * g + 3):
            xp = xdt[:, 128 * pp:128 * (pp + 1)]
            acc = jnp.zeros((T, LANES), f32)
            for hh in range(2):
                a_col = column(a_cs, 2 * pp + hh)
                decay = jnp.where(tril, jnp.exp(jnp.minimum(a_col - a_col.T, 0.0)), 0.0)
                acc = acc + _bdot(cb * decay, xp * hm[hh], NN)
            diag_parts.append(acc)
    new_state = state * jnp.exp(tot_full) + jnp.concatenate(st_parts, axis=1)
    y = jnp.concatenate(diag_parts, axis=1) + jnp.concatenate(off_parts, axis=1) * eacs + xs * d_full
    return y, new_state


def ssd_fwd(xact, p_ssd, dt_bias, a_log, d_full, name):
    S = xact.shape[0]
    T = SSD_CHUNK

    def body(x_ref, p_ref, b_ref, a_ref, d_ref, y_ref, s_ref, state):
        @pl.when(pl.program_id(0) == 0)
        def _():
            state[...] = jnp.zeros_like(state)

        st = state[...]
        s_ref[0] = st
        y, new = _ssd_chunk(x_ref[...], p_ref[...], st, b_ref[...], a_ref[...], d_ref[...])
        y_ref[...] = y
        state[...] = new

    vec = lambda n: pl.BlockSpec((1, n), lambda i: (0, 0))
    return _call(body, name, (S // T,),
                 [pl.BlockSpec((T, SSD_XBC), lambda i: (i, 0)), pl.BlockSpec((T, 128), lambda i: (i, 10)),
                  vec(128), vec(128), vec(768)],
                 [pl.BlockSpec((T, 768), lambda i: (i, 0)), pl.BlockSpec((1, T, 768), lambda i: (i, 0, 0))],
                 [jax.ShapeDtypeStruct((S, 768), f32), jax.ShapeDtypeStruct((S // T, T, 768), f32)],
                 scratch=[pltpu.VMEM((T, 768), f32)], sem=("arbitrary",))(xact, p_ssd, dt_bias, a_log, d_full)


def ssd_bwd(xact, p_ssd, states, dy, dt_bias, a_log, d_full, name):
    S = xact.shape[0]
    T = SSD_CHUNK
    nc = S // T

    def body(x_ref, p_ref, s_ref, dy_ref, b_ref, a_ref, d_ref, dx_ref, ddt_ref, db_ref, da_ref, dd_ref, dstate):
        i = pl.program_id(0)

        @pl.when(i == 0)
        def _():
            for ref in (dstate, db_ref, da_ref, dd_ref):
                ref[...] = jnp.zeros_like(ref)

        _, vjp = jax.vjp(_ssd_chunk, x_ref[...], p_ref[...], s_ref[0], b_ref[...], a_ref[...], d_ref[...])
        dx, ddt, dst, db, da, dd = vjp((dy_ref[...], dstate[...]))
        dx_ref[...] = dx
        ddt_ref[...] = ddt
        dstate[...] = dst
        db_ref[...] += db
        da_ref[...] += da
        dd_ref[...] += dd

    rev = lambda i: nc - 1 - i
    vec = lambda n: pl.BlockSpec((1, n), lambda i: (0, 0))
    return _call(body, name, (nc,),
                 [pl.BlockSpec((T, SSD_XBC), lambda i: (rev(i), 0)), pl.BlockSpec((T, 128), lambda i: (rev(i), 10)),
                  pl.BlockSpec((1, T, 768), lambda i: (rev(i), 0, 0)), pl.BlockSpec((T, 768), lambda i: (rev(i), 0)),
                  vec(128), vec(128), vec(768)],
                 [pl.BlockSpec((T, SSD_XBC), lambda i: (rev(i), 0)), pl.BlockSpec((T, 128), lambda i: (rev(i), 0)),
                  vec(128), vec(128), vec(768)],
                 [jax.ShapeDtypeStruct((S, SSD_XBC), f32), jax.ShapeDtypeStruct((S, 128), f32),
                  jax.ShapeDtypeStruct((1, 128), f32), jax.ShapeDtypeStruct((1, 128), f32),
                  jax.ShapeDtypeStruct((1, 768), f32)],
                 scratch=[pltpu.VMEM((T, 768), f32)],
                 sem=("arbitrary",))(xact, p_ssd, states, dy, dt_bias, a_log, d_full)


def _tail_fn(ys5, pt, o0, o1, o2, l0, l1, l2, yssd, glu_b, nw, pr_glu, pr_a, pr_b, pr_c, x, weights):
    glu_w, pa, pb, pc, wo = weights
    gates = jax.nn.sigmoid(pt[:, :3072])
    za, zb, zc = pt[:, 3072:3584], pt[:, 3584:3840], pt[:, 3840:4608]
    g = jax.nn.gelu(ys5)
    ya = g * jax.nn.sigmoid(_cdot(g, glu_w, NN) + glu_b + pr_glu) * jax.nn.silu(za)
    m = jnp.maximum(jnp.maximum(l0, l1), l2)
    e0, e1, e2 = jnp.exp(l0 - m), jnp.exp(l1 - m), jnp.exp(l2 - m)
    yb = (e0 * o0 + e1 * o1 + e2 * o2) / (e0 + e1 + e2) * jax.nn.silu(zb)
    yc = _rms(yssd * jax.nn.silu(zc), nw)
    merged = (gates[:, :1024] * (_cdot(ya, pa, NN) + pr_a) + gates[:, 1024:2048] * (_cdot(yb, pb, NN) + pr_b)
              + gates[:, 2048:] * (_cdot(yc, pc, NN) + pr_c))
    out = x + _cdot(merged, wo, NN)
    return out, (g, ya, yb, yc, merged)


def _tail_specs(tm):
    row = lambda n: pl.BlockSpec((tm, n), lambda i: (i, 0))
    full = lambda a, b: pl.BlockSpec((a, b), lambda i: (0, 0))
    acts = [row(512), row(4608)] + [row(256)] * 6 + [row(768), row(D_MODEL)]
    consts = [full(1, 512), full(1, 768), full(512, 512), full(512, D_MODEL), full(256, D_MODEL),
              full(768, D_MODEL), full(D_MODEL, D_MODEL)]
    return row, full, acts, consts


def tail_fwd(ys5, pt, os_, ls_, yssd, x, glu_b, nw, weights, name):
    S = x.shape[0]
    tm = 2 * TAIL_ROWS
    row, full, acts, consts = _tail_specs(tm)

    def body(ys5_ref, pt_ref, o0, o1, o2, l0, l1, l2, yssd_ref, x_ref, gb_ref, nw_ref, gw, pa, pb, pc, wo, out_ref):
        z = lambda n: jnp.zeros((tm, n), f32)
        out, _ = _tail_fn(ys5_ref[...], pt_ref[...], o0[...], o1[...], o2[...], l0[...], l1[...], l2[...],
                          yssd_ref[...], gb_ref[...], nw_ref[...], z(512), z(D_MODEL), z(D_MODEL), z(D_MODEL),
                          x_ref[...], (gw[...], pa[...], pb[...], pc[...], wo[...]))
        out_ref[...] = out

    return _call(body, name, (S // tm,), acts + consts, row(D_MODEL), jax.ShapeDtypeStruct((S, D_MODEL), f32),
                 sem=("parallel",))(ys5, pt, *os_, *ls_, yssd, x, glu_b, nw, *weights)


def tail_bwd(ys5, pt, os_, ls_, yssd, dout, glu_b, nw, weights, name):
    S = dout.shape[0]
    tm = TAIL_ROWS
    row, full, acts, consts = _tail_specs(tm)

    def body(ys5_ref, pt_ref, o0, o1, o2, l0, l1, l2, yssd_ref, dout_ref, gb_ref, nw_ref, gw, pa, pb, pc, wo,
             dys5_ref, dpt_ref, do0, do1, do2, dl0, dl1, dl2, dyssd_ref, dgb_ref, dnw_ref,
             g_ref, ya_ref, yb_ref, yc_ref, mg_ref, dglu_ref, dpa_ref, dpb_ref, dpc_ref):
        z = lambda n: jnp.zeros((tm, n), f32)
        w = (gw[...], pa[...], pb[...], pc[...], wo[...])
        fn = lambda *a: _tail_fn(*a, z(D_MODEL), w)
        _, vjp, aux = jax.vjp(fn, ys5_ref[...], pt_ref[...], o0[...], o1[...], o2[...], l0[...], l1[...], l2[...],
                              yssd_ref[...], gb_ref[...], nw_ref[...], z(512), z(D_MODEL), z(D_MODEL), z(D_MODEL),
                              has_aux=True)
        (dys5, dpt, d0, d1, d2, e0, e1, e2, dyssd, dgb, dnw, dglu, dpa, dpb, dpc) = vjp(dout_ref[...])
        dys5_ref[...] = dys5
        dpt_ref[...] = dpt.astype(bf16)
        for ref, val in ((do0, d0), (do1, d1), (do2, d2), (dl0, e0), (dl1, e1), (dl2, e2)):
            ref[...] = val
        dyssd_ref[...] = dyssd
        g, ya, yb, yc, merged = aux
        for ref, val in ((g_ref, g), (ya_ref, ya), (yb_ref, yb), (yc_ref, yc), (mg_ref, merged),
                         (dglu_ref, dglu), (dpa_ref, dpa), (dpb_ref, dpb), (dpc_ref, dpc)):
            ref[...] = val.astype(bf16)

        @pl.when(pl.program_id(0) == 0)
        def _():
            dgb_ref[...] = dgb
            dnw_ref[...] = dnw

        @pl.when(pl.program_id(0) > 0)
        def _():
            dgb_ref[...] += dgb
            dnw_ref[...] += dnw

    sd = lambda n, dt=f32: jax.ShapeDtypeStruct((S, n), dt)
    out_specs = ([row(512), row(4608)] + [row(256)] * 6 + [row(768), full(1, 512), full(1, 768)]
                 + [row(512), row(512), row(256), row(768), row(D_MODEL), row(512)] + [row(D_MODEL)] * 3)
    out_shape = ([sd(512), sd(4608, bf16)] + [sd(256)] * 6 + [sd(768), jax.ShapeDtypeStruct((1, 512), f32),
                                                          jax.ShapeDtypeStruct((1, 768), f32)]
                 + [sd(512, bf16), sd(512, bf16), sd(256, bf16), sd(768, bf16), sd(D_MODEL, bf16), sd(512, bf16)]
                 + [sd(D_MODEL, bf16)] * 3)
    return _call(body, name, (S // tm,), acts + consts, out_specs, out_shape,
                 sem=("arbitrary",))(ys5, pt, *os_, *ls_, yssd, dout, glu_b, nw, *weights)


def _in_proj_segments(shards):
    dtype = shards[0].dtype

    def c(a, b):
        parts = []
        for k, sh in enumerate(shards):
            lo, hi = max(a, W_IN_SHARD * k), min(b, W_IN_SHARD * (k + 1))
            if lo < hi:
                parts.append(sh[:, lo - W_IN_SHARD * k:hi - W_IN_SHARD * k])
        return parts[0] if len(parts) == 1 else jnp.concatenate(parts, axis=1)

    atts = []
    for g in range(3):
        parts = []
        for hh in range(2):
            o = 64 * (4 * g + 2 * hh)
            parts += [c(_C_Q + o, _C_Q + o + 128), c(_C_K + o, _C_K + o + 128), c(_C_V + o, _C_V + o + 128)]
        atts.append(jnp.concatenate(parts, axis=1))
    ssd = jnp.concatenate([c(_C_XBC, _C_ZC), jnp.zeros((D_MODEL, 1536 - (_C_ZC - _C_XBC)), dtype)], axis=1)
    tail = jnp.concatenate([c(_C_GATE, _C_END), c(_C_ZA, _C_Q), c(_C_ZB, _C_XBC), c(_C_ZC, _C_GATE)], axis=1)
    return [c(_C_UA, _C_ZA)] + atts + [ssd, tail]


def _in_proj_grad(ds5, datts, dssd, dtail):
    pick = lambda off: [datts[g][:, 384 * hh + off:384 * hh + off + 128] for g in range(3) for hh in range(2)]
    pieces = ([ds5, dtail[:, 3072:3584]] + pick(0) + pick(128) + pick(256)
              + [dtail[:, 3584:3840], dssd[:, :_C_ZC - _C_XBC], dtail[:, 3840:4608], dtail[:, :3072]])
    shards, start = [[] for _ in range(4)], 0
    for piece in pieces:
        width = piece.shape[1]
        for k in range(4):
            lo, hi = max(start, W_IN_SHARD * k), min(start + width, W_IN_SHARD * (k + 1))
            if lo < hi:
                shards[k].append(piece[:, lo - start:hi - start])
        start += width
    return jnp.stack([jnp.concatenate(s, axis=1) for s in shards])


def _prep_layer(p):
    q = {}
    q["segs"] = [s.astype(bf16) for s in _in_proj_segments(p["w_in"])]
    disc = _s5_discretize(p["s5_a_re"], p["s5_a_im"], p["s5_log_step"], p["s5_b_re"], p["s5_b_im"],
                          p["s5_c_re"], p["s5_c_im"])
    q["s5"] = disc
    q["pw"] = _lam_powers(disc[0], disc[1])
    q["s5_d"] = p["s5_d"].reshape(1, 512)
    q["qw"] = jnp.tile(p["q_norm_w"], 2).reshape(1, LANES)
    q["kw"] = jnp.tile(p["k_norm_w"], 2).reshape(1, LANES)
    q["conv_w"] = p["conv_w"]
    q["conv_b"] = p["conv_b"].reshape(1, SSD_XBC)
    pad = lambda v: jnp.pad(v, (0, LANES - v.shape[0])).reshape(1, LANES)
    q["dt_bias"], q["a_log"] = pad(p["dt_bias"]), pad(p["ssd_a_log"])
    q["d_full"] = jnp.repeat(p["ssd_d"], 64).reshape(1, SSD_WIDTH)
    q["glu_b"] = p["s5_glu_b"].reshape(1, 512)
    q["nw"] = p["ssd_norm_w"].reshape(1, SSD_WIDTH)
    q["norm_w"] = p["norm_w"].reshape(1, D_MODEL)
    q["tailw"] = tuple(p[n].astype(bf16) for n in ("s5_glu_w", "proj_a", "proj_b", "proj_c", "w_out"))
    return q


_DILATIONS = (1, 4, 16)


def layer_fwd(x, q, tag):
    h = rms_fwd(x, q["norm_w"], f"rms_fwd{tag}")
    p_s5, p_a0, p_a1, p_a2, p_ssd, p_tail = [mm_nn(h, w, f"inproj{k}{tag}") for k, w in enumerate(q["segs"])]
    _, _, w_re, w_im, c_re, c_im = q["s5"]
    ys5, h_re, h_im = s5_fwd(p_s5, *q["pw"], w_re, w_im, c_re, c_im, q["s5_d"], f"s5_fwd{tag}")
    p_atts = (p_a0, p_a1, p_a2)
    os_, ls_ = [], []
    for g, d in enumerate(_DILATIONS):
        o, l = att_fwd(p_atts[g], q["qw"], q["kw"], d, f"att_fwd{g}{tag}")
        os_.append(o)
        ls_.append(l)
    xact = conv_fwd(p_ssd, q["conv_w"], q["conv_b"], f"conv_fwd{tag}")
    yssd, states = ssd_fwd(xact, p_ssd, q["dt_bias"], q["a_log"], q["d_full"], f"ssd_fwd{tag}")
    out = tail_fwd(ys5, p_tail, os_, ls_, yssd, x, q["glu_b"], q["nw"], q["tailw"], f"tail_fwd{tag}")
    saved = dict(x=x, h=h, p_s5=p_s5, p_atts=p_atts, p_ssd=p_ssd, p_tail=p_tail, ys5=ys5, h_re=h_re, h_im=h_im,
                 os=os_, ls=ls_, xact=xact, yssd=yssd, states=states)
    return out, saved


def layer_bwd(dout, sv, q, p, tag):
    S = dout.shape[0]
    (dys5, dp_tail, do0, do1, do2, dl0, dl1, dl2, dyssd, dglu_b, dnw, g_b, ya_b, yb_b, yc_b, mg_b, dglu_b16,
     dpa_b, dpb_b, dpc_b) = tail_bwd(sv["ys5"], sv["p_tail"], sv["os"], sv["ls"], sv["yssd"], dout, q["glu_b"],
                                     q["nw"], q["tailw"], f"tail_bwd{tag}")
    grads = {}
    grads["s5_glu_w"] = mm_tn(g_b, dglu_b16, f"dglu_w{tag}")
    grads["proj_a"] = mm_tn(ya_b, dpa_b, f"dproj_a{tag}")
    grads["proj_b"] = mm_tn(yb_b, dpb_b, f"dproj_b{tag}")
    grads["proj_c"] = mm_tn(yc_b, dpc_b, f"dproj_c{tag}")
    grads["w_out"] = mm_tn(mg_b, dout, f"dw_out{tag}")
    grads["s5_glu_b"] = dglu_b.reshape(512)
    grads["ssd_norm_w"] = dnw.reshape(SSD_WIDTH)

    dxact, ddt, ddt_bias, da_log, dd_full = ssd_bwd(sv["xact"], sv["p_ssd"], sv["states"], dyssd, q["dt_bias"],
                                                    q["a_log"], q["d_full"], f"ssd_bwd{tag}")
    dp_ssd, dconv_w, dconv_b = conv_bwd(sv["p_ssd"], dxact, ddt, q["conv_w"], q["conv_b"], f"conv_bwd{tag}")
    grads["dt_bias"] = ddt_bias[0, :12]
    grads["ssd_a_log"] = da_log[0, :12]
    grads["ssd_d"] = dd_full.reshape(12, 64).sum(axis=1)
    grads["conv_w"] = dconv_w
    grads["conv_b"] = dconv_b.reshape(SSD_XBC)

    dp_atts, dqw, dkw = [], 0.0, 0.0
    for g, d in enumerate(_DILATIONS):
        dp, a, b = att_bwd(sv["p_atts"][g], sv["os"][g], sv["ls"][g], (do0, do1, do2)[g], (dl0, dl1, dl2)[g],
                           q["qw"], q["kw"], d,
                           f"att_bwd{g}{tag}")
        dp_atts.append(dp)
        dqw, dkw = dqw + a, dkw + b
    grads["q_norm_w"] = dqw.reshape(2, 64).sum(axis=0)
    grads["k_norm_w"] = dkw.reshape(2, 64).sum(axis=0)

    _, _, w_re, w_im, c_re, c_im = q["s5"]
    dp_s5, dwre, dwim, dcre, dcim, dlam_re, dlam_im, dd = s5_bwd(
        dys5, sv["p_s5"], sv["h_re"], sv["h_im"], *q["pw"], w_re, w_im, c_re, c_im, q["s5_d"], f"s5_bwd{tag}")
    s5_names = ("s5_a_re", "s5_a_im", "s5_log_step", "s5_b_re", "s5_b_im", "s5_c_re", "s5_c_im")
    _, disc_vjp = jax.vjp(_s5_discretize, *[p[n] for n in s5_names])
    for n, gr in zip(s5_names, disc_vjp((dlam_re, dlam_im, dwre, dwim, dcre, dcim))):
        grads[n] = gr
    grads["s5_d"] = dd.reshape(512)

    dsegs = [dp_s5] + dp_atts + [dp_ssd, dp_tail]
    dws = [mm_tn(sv["h"], ds, f"dw_in{k}{tag}") for k, ds in enumerate(dsegs)]
    grads["w_in"] = _in_proj_grad(dws[0], dws[1:4], dws[4], dws[5])
    dh = None
    for k, (ds, w) in enumerate(zip(dsegs, q["segs"])):
        dh = mm_nt(ds, w, f"dh{k}{tag}", acc=dh)
    dx, dnorm_w = rms_bwd(sv["x"], q["norm_w"], dh, dout, f"rms_bwd{tag}")
    grads["norm_w"] = dnorm_w.reshape(D_MODEL)
    return dx, grads


_ANY = pl.BlockSpec(memory_space=pl.ANY)


def _chip_exchange(x, name, broadcast):
    shape = tuple(x.shape) if broadcast else tuple(x.shape[1:])

    def body(x_ref, o_ref, send_sems, recv_sems):
        mx, my, mc = lax.axis_index("x"), lax.axis_index("y"), lax.axis_index("c")
        me = 2 * mx + my
        copies = []
        for t, (px, py) in enumerate(((1 - mx, my), (mx, 1 - my), (1 - mx, 1 - my))):
            src = x_ref if broadcast else x_ref.at[2 * px + py]
            cp = pltpu.make_async_remote_copy(src_ref=src, dst_ref=o_ref.at[me], send_sem=send_sems.at[t],
                                              recv_sem=recv_sems.at[t], device_id=(px, py, mc),
                                              device_id_type=pl.DeviceIdType.MESH)
            cp.start()
            copies.append(cp)
        for cp in copies:
            cp.wait()

    landed = pl.pallas_call(
        body, name=name, in_specs=[_ANY], out_specs=_ANY, out_shape=jax.ShapeDtypeStruct((4,) + shape, x.dtype),
        scratch_shapes=[pltpu.SemaphoreType.DMA((3,)), pltpu.SemaphoreType.DMA((3,))],
    )(x)
    me = 2 * lax.axis_index("x") + lax.axis_index("y")
    own = x[None] if broadcast else lax.dynamic_index_in_dim(x, me, 0, keepdims=True)
    return lax.dynamic_update_index_in_dim(landed, own, me, 0)


def _sibling_exchange(xs, name, both=False):
    n = len(xs)

    def body(*refs):
        x_refs, o_refs, send_sems, recv_sems = refs[:n], refs[n:2 * n], refs[2 * n], refs[2 * n + 1]
        mc = lax.axis_index("c")
        peer = (lax.axis_index("x"), lax.axis_index("y"), 1 - mc)
        copies = []
        for t in range(n):
            cp = pltpu.make_async_remote_copy(src_ref=x_refs[t], dst_ref=o_refs[t].at[mc] if both else o_refs[t],
                                              send_sem=send_sems.at[t], recv_sem=recv_sems.at[t], device_id=peer,
                                              device_id_type=pl.DeviceIdType.MESH)
            cp.start()
            copies.append(cp)
        for cp in copies:
            cp.wait()

    lead = (2,) if both else ()
    outs = pl.pallas_call(
        body, name=name, in_specs=[_ANY] * n, out_specs=[_ANY] * n,
        out_shape=[jax.ShapeDtypeStruct(lead + tuple(x.shape), x.dtype) for x in xs],
        scratch_shapes=[pltpu.SemaphoreType.DMA((n,)), pltpu.SemaphoreType.DMA((n,))],
    )(*xs)
    if both:
        c = lax.axis_index("c")
        outs = [lax.dynamic_update_index_in_dim(o, x[None], c, 0) for o, x in zip(outs, xs)]
    return outs


def _rows_tile(rows, row_bytes, budget=1 << 20):
    return next(t for t in (512, 256, 128, 64, 32, 16, 8) if rows % t == 0 and t * row_bytes <= budget)


def _padded_row_bytes(cols):
    return -(-cols // LANES) * LANES * 4


def _add2(a, b, name, out_dtype=f32):
    R, C = a.shape
    tr = _rows_tile(R, _padded_row_bytes(C))

    def body(a_ref, b_ref, o_ref):
        o_ref[...] = (a_ref[...] + b_ref[...]).astype(out_dtype)

    spec = pl.BlockSpec((tr, C), lambda i: (i, 0))
    return _call(body, name, (R // tr,), [spec, spec], spec, jax.ShapeDtypeStruct((R, C), out_dtype),
                 sem=("parallel",))(a, b)


def _sum4(x, name):
    R = x.shape[1]
    tr = _tile(R, (512, 256, 128))

    def body(x_ref, o_ref):
        p = [x_ref[j].astype(f32) for j in range(4)]
        o_ref[...] = ((p[0] + p[1]) + p[2]) + p[3]

    return _call(body, name, (R // tr,), [pl.BlockSpec((4, tr, LANES), lambda i: (0, i, 0))],
                 pl.BlockSpec((tr, LANES), lambda i: (i, 0)), jax.ShapeDtypeStruct((R, LANES), f32),
                 sem=("parallel",))(x)


def _adamw(g_parts, w, m, v, name):
    stacked = not isinstance(g_parts, (tuple, list))
    k = g_parts.shape[0] if stacked else len(g_parts)
    R, C = w.shape
    tr = _rows_tile(R, _padded_row_bytes(C))
    c1 = 1.0 - ADAM_B1 ** ADAM_STEP
    c2 = 1.0 - ADAM_B2 ** ADAM_STEP

    def body(*refs):
        w_ref, m_ref, v_ref, g_ref, d_ref, nm_ref, nv_ref = refs[-7:]
        if stacked:
            g = refs[0][0].astype(f32)
            for j in range(1, k):
                g = g + refs[0][j].astype(f32)
        else:
            g = refs[0][...]
            for r in refs[1:k]:
                g = g + r[...]
        m = ADAM_B1 * m_ref[...] + (1.0 - ADAM_B1) * g
        v = ADAM_B2 * v_ref[...] + (1.0 - ADAM_B2) * (g * g)
        g_ref[...] = g
        nm_ref[...] = m
        nv_ref[...] = v
        d_ref[...] = -ADAM_LR * ((m / c1) / (jnp.sqrt(v / c2) + ADAM_EPS) + ADAM_WD * w_ref[...])

    spec = pl.BlockSpec((tr, C), lambda i: (i, 0))
    sd = jax.ShapeDtypeStruct((R, C), f32)
    g_specs = [pl.BlockSpec((k, tr, C), lambda i: (0, i, 0))] if stacked else [spec] * k
    g_args = [g_parts] if stacked else list(g_parts)
    return _call(body, name, (R // tr,), g_specs + [spec] * 3, [spec] * 4, [sd] * 4,
                 sem=("parallel",))(*g_args, w, m, v)


def _pack(arrays):
    flat = jnp.concatenate([a.reshape(-1) for a in arrays])
    unit = PACK_ROWS * LANES
    n = -(-flat.shape[0] // unit) * unit
    return jnp.pad(flat, (0, n - flat.shape[0])).reshape(n // LANES, LANES)


def _unpack(buf, shapes):
    flat = buf.reshape(-1)
    out, off = [], 0
    for s in shapes:
        n = 1
        for dim in s:
            n *= dim
        out.append(flat[off:off + n].reshape(s))
        off += n
    return out


def _to_shards(full, axis):
    s = full.shape
    t = full.reshape(s[:axis] + (4, s[axis] // 4) + s[axis + 1:])
    return jnp.moveaxis(t, axis, 0)


def _from_shards(sh, axis):
    t = jnp.moveaxis(sh, 0, axis)
    s = t.shape
    return t.reshape(s[:axis] + (s[axis] * s[axis + 1],) + s[axis + 2:])


def kernel(x, norm_w, w_in, s5_a_re, s5_a_im, s5_log_step, s5_b_re, s5_b_im, s5_c_re, s5_c_im, s5_d, s5_glu_w, s5_glu_b, q_norm_w, k_norm_w, conv_w, conv_b, dt_bias, ssd_a_log, ssd_d, ssd_norm_w, proj_a, proj_b, proj_c, w_out, loss_target, m_norm_w, m_w_in, m_s5_a_re, m_s5_a_im, m_s5_log_step, m_s5_b_re, m_s5_b_im, m_s5_c_re, m_s5_c_im, m_s5_d, m_s5_glu_w, m_s5_glu_b, m_q_norm_w, m_k_norm_w, m_conv_w, m_conv_b, m_dt_bias, m_ssd_a_log, m_ssd_d, m_ssd_norm_w, m_proj_a, m_proj_b, m_proj_c, m_w_out, v_norm_w, v_w_in, v_s5_a_re, v_s5_a_im, v_s5_log_step, v_s5_b_re, v_s5_b_im, v_s5_c_re, v_s5_c_im, v_s5_d, v_s5_glu_w, v_s5_glu_b, v_q_norm_w, v_k_norm_w, v_conv_w, v_conv_b, v_dt_bias, v_ssd_a_log, v_ssd_d, v_ssd_norm_w, v_proj_a, v_proj_b, v_proj_c, v_w_out):
    given = dict(locals())
    W = {n: given[n] for n in _WEIGHTS}
    M = {n: given["m_" + n] for n in _WEIGHTS}
    V = {n: given["v_" + n] for n in _WEIGHTS}
    n_layers = norm_w.shape[0]
    assert n_layers == 2
    c = lax.axis_index("c")

    mine_of = lambda t: lax.dynamic_index_in_dim(t, c, 0, keepdims=False)
    as_payload = lambda n: lax.bitcast_convert_type(W[n], bf16) if n == "conv_w" else W[n].astype(bf16)
    payload_shapes = [W[n].shape + ((2,) if n == "conv_w" else ()) for n, _ in _SHARDED]
    gathered = _chip_exchange(_pack([as_payload(n) for n, _ in _SHARDED]), "gather_weights", broadcast=True)
    w_in_layers, = _sibling_exchange(
        [_chip_exchange(mine_of(w_in).astype(bf16), "gather_w_in", broadcast=True)], "share_w_in", both=True)
    full = dict(W)
    pieces = [_unpack(gathered[j], payload_shapes) for j in range(4)]
    for k, (n, axis) in enumerate(_SHARDED):
        sh = jnp.stack([pieces[j][k] for j in range(4)])
        full[n] = _from_shards(lax.bitcast_convert_type(sh, f32) if n == "conv_w" else sh, axis)

    xs = x[0]
    qs, saves = [], []
    act = xs
    for l in range(n_layers):
        p = {n: full[n][l] for n in _WEIGHTS if n != "w_in"}
        p["w_in"] = [w_in_layers[l, k] for k in range(4)]
        q = _prep_layer(p)
        act, sv = layer_fwd(act, q, f"_l{l}")
        qs.append((q, p))
        saves.append(sv)
    dact, lsum = loss_and_grad(act, loss_target[0], "loss")
    loss = lax.psum(lsum[0, 0], ("x", "y", "c"))
    layer_grads = [None] * n_layers
    for l in reversed(range(n_layers)):
        q, p = qs[l]
        dact, layer_grads[l] = layer_bwd(dact, saves[l], q, p, f"_l{l}")
    grad_x = dact[None]
    G = {n: jnp.stack([layer_grads[l][n] for l in range(n_layers)]) for n in _WEIGHTS if n != "w_in"}

    g0, g1 = layer_grads[0]["w_in"], layer_grads[1]["w_in"]
    from_sibling, = _sibling_exchange([jnp.where(c == 0, g1, g0)], "swap_w_in_grads")
    flat = lambda t: t.reshape(4 * D_MODEL, W_IN_SHARD)
    shards = _add2(flat(jnp.where(c == 0, g0, g1)), flat(from_sibling), "sum_cores_w_in", out_dtype=bf16)
    landed = _chip_exchange(shards.reshape(4, D_MODEL, W_IN_SHARD), "scatter_w_in_grads", broadcast=False)
    w_in_mine = _adamw(landed, mine_of(w_in), mine_of(m_w_in), mine_of(v_w_in), "adamw_w_in")
    w_in_out = _sibling_exchange(w_in_mine, "share_w_in_updates", both=True)

    repl_shapes = [W[n].shape for n in _REPL]
    small = _pack([G[n] for n in _REPL])
    quarter = small.shape[0] // 4
    big = [_to_shards(G[n], axis).reshape(4, -1) for n, axis in _SHARDED]
    big = jnp.concatenate(big, axis=1)
    unit = PACK_ROWS * LANES
    nbig = -(-big.shape[1] // unit) * unit
    big = jnp.pad(big, ((0, 0), (0, nbig - big.shape[1]))).reshape(4, nbig // LANES, LANES)
    gpack = jnp.concatenate([big, small.reshape(4, quarter, LANES)], axis=1)
    mine = _sum4(_chip_exchange(gpack.astype(bf16), "scatter_grads", broadcast=False), "sum_chips")
    other, = _sibling_exchange([mine], "swap_cores")
    rbig = nbig // LANES

    wp, mp, vp = (_pack([T[n] for n, _ in _SHARDED]) for T in (W, M, V))
    outs_big = _adamw((mine[:rbig], other[:rbig]), wp, mp, vp, "adamw_sharded")
    big_out = [_unpack(o, [W[n].shape for n, _ in _SHARDED]) for o in outs_big]

    gq = _add2(mine[rbig:], other[rbig:], "sum_cores_small")
    gsmall = _chip_exchange(gq, "gather_small", broadcast=True).reshape(4 * quarter, LANES)
    ws, ms, vs = (_pack([T[n] for n in _REPL]) for T in (W, M, V))
    outs_small = _adamw((gsmall,), ws, ms, vs, "adamw_replicated")
    small_out = [_unpack(o, repl_shapes) for o in outs_small]

    res = [dict(), dict(), dict(), dict()]
    for kind in range(4):
        res[kind]["w_in"] = w_in_out[kind]
        for k, (n, _) in enumerate(_SHARDED):
            res[kind][n] = big_out[kind][k]
        for k, n in enumerate(_REPL):
            res[kind][n] = small_out[kind][k]
    return (loss, grad_x, *[res[0][n] for n in _WEIGHTS], *[res[1][n] for n in _WEIGHTS],
            *[res[2][n] for n in _WEIGHTS], *[res[3][n] for n in _WEIGHTS])
```

```python
import functools

import jax
import jax.numpy as jnp
from jax import lax
from jax.experimental import pallas as pl
from jax.experimental.pallas import tpu as pltpu

f32 = jnp.float32
bf16 = jnp.bfloat16

D_MODEL = 1024
RMS_EPS = 1e-6
V7X_VMEM_LIMIT = 60 * 1024 * 1024
LANES = 128
NN, NT, TN = ((1,), (0,)), ((1,), (1,)), ((0,), (0,))

S5_STATES = 2048
S5_ROWS = 256
ATT_SEG = 2048
ATT_BLOCK = 128
SSD_CHUNK = 128
SSD_WIDTH = 768
SSD_XBC = 1280
CONV_ROWS = 512
TAIL_ROWS = 128

ADAM_LR, ADAM_B1, ADAM_B2, ADAM_EPS, ADAM_WD, ADAM_STEP = 0.001, 0.9, 0.999, 1e-08, 0.01, 10

_C_UA, _C_ZA, _C_Q, _C_K, _C_V, _C_ZB, _C_XBC, _C_DT, _C_ZC, _C_GATE, _C_END = (
    0, 512, 1024, 1792, 2560, 3328, 3584, 4864, 4876, 5644, 8716)

_SHARDED = (("s5_glu_w", 1), ("conv_w", 2), ("proj_a", 2), ("proj_b", 2), ("proj_c", 2), ("w_out", 1))
W_IN_SHARD = 2179
_REPL = ("norm_w", "s5_a_re", "s5_a_im", "s5_log_step", "s5_b_re", "s5_b_im", "s5_c_re", "s5_c_im", "s5_d",
         "s5_glu_b", "q_norm_w", "k_norm_w", "conv_b", "dt_bias", "ssd_a_log", "ssd_d", "ssd_norm_w")
_WEIGHTS = ("norm_w", "w_in", "s5_a_re", "s5_a_im", "s5_log_step", "s5_b_re", "s5_b_im", "s5_c_re", "s5_c_im",
            "s5_d", "s5_glu_w", "s5_glu_b", "q_norm_w", "k_norm_w", "conv_w", "conv_b", "dt_bias", "ssd_a_log",
            "ssd_d", "ssd_norm_w", "proj_a", "proj_b", "proj_c", "w_out")
PACK_ROWS = 512


def _dot(a, b, dims):
    return lax.dot_general(a.astype(bf16), b.astype(bf16), (dims, ((), ())), preferred_element_type=f32)


def _call(body, name, grid, in_specs, out_specs, out_shape, scratch=(), sem=None):
    return pl.pallas_call(
        body, name=name, grid=grid, in_specs=in_specs, out_specs=out_specs, out_shape=out_shape,
        scratch_shapes=list(scratch),
        compiler_params=pltpu.CompilerParams(dimension_semantics=sem, vmem_limit_bytes=V7X_VMEM_LIMIT))


def _tile(n, options=(1024, 768, 512, 384, 256, 128)):
    return next(t for t in options if n % t == 0)


@functools.partial(jax.custom_vjp, nondiff_argnums=(2,))
def _bdot(a, b, dims):
    return _dot(a, b, dims)


def _bdot_fwd(a, b, dims):
    return _dot(a, b, dims), (a, b)


def _bdot_bwd(dims, res, g):
    a, b = res
    if dims == NN:
        da, db = _dot(g, b, NT), _dot(a, g, TN)
    elif dims == NT:
        da, db = _dot(g, b, NN), _dot(g, a, TN)
    else:
        da, db = _dot(b, g, NT), _dot(a, g, NN)
    return da.astype(a.dtype), db.astype(b.dtype)


_bdot.defvjp(_bdot_fwd, _bdot_bwd)


@functools.partial(jax.custom_vjp, nondiff_argnums=(2,))
def _cdot(a, w, dims):
    return _dot(a, w, dims)


def _cdot_fwd(a, w, dims):
    return _dot(a, w, dims), w


def _cdot_bwd(dims, w, g):
    da = _dot(g, w, NT) if dims == NN else _dot(g, w, NN)
    return da, jnp.zeros_like(w)


_cdot.defvjp(_cdot_fwd, _cdot_bwd)


def _split3(x):
    hi = x.astype(bf16)
    r = x - hi.astype(f32)
    mid = r.astype(bf16)
    lo = (r - mid.astype(f32)).astype(bf16)
    return hi, mid, lo


@jax.custom_vjp
def _xdot_l(m, x):
    return sum(_dot(m, p, NN) for p in _split3(x))


def _xdot_l_fwd(m, x):
    return _xdot_l(m, x), m


def _xdot_l_bwd(m, g):
    return jnp.zeros_like(m), sum(_dot(m, p, TN) for p in _split3(g))


_xdot_l.defvjp(_xdot_l_fwd, _xdot_l_bwd)


@jax.custom_vjp
def _softplus(x):
    e = jnp.exp(-jnp.abs(x))
    u = 1.0 + e
    log1p = jnp.where(u == 1.0, e, jnp.log(u) * (e / jnp.where(u == 1.0, 1.0, u - 1.0)))
    return jnp.maximum(x, 0.0) + log1p


def _softplus_fwd(x):
    return _softplus(x), x


def _softplus_bwd(x, g):
    return (g * jax.nn.sigmoid(x),)


_softplus.defvjp(_softplus_fwd, _softplus_bwd)


def _rms(x, w):
    return x * lax.rsqrt(jnp.mean(x * x, axis=-1, keepdims=True) + RMS_EPS) * w


def mm_nn(a, b, name, tm=2048):
    M, K = a.shape
    N = b.shape[1]
    tn = _tile(N)

    def body(a_ref, b_ref, o_ref):
        o_ref[...] = _dot(a_ref[...], b_ref[...], NN)

    return _call(body, name, (M // tm, N // tn),
                 [pl.BlockSpec((tm, K), lambda i, j: (i, 0)), pl.BlockSpec((K, tn), lambda i, j: (0, j))],
                 pl.BlockSpec((tm, tn), lambda i, j: (i, j)), jax.ShapeDtypeStruct((M, N), f32),
                 sem=("parallel", "parallel"))(a, b)


def mm_nt(a, b, name, acc=None, tm=1024):
    M, K = a.shape
    N = b.shape[0]
    tk = _tile(K)
    has_acc = acc is not None

    def body(*refs):
        a_ref, b_ref = refs[0], refs[1]
        o_ref = refs[-1]
        k = pl.program_id(1)
        p = _dot(a_ref[...], b_ref[...], NT)

        @pl.when(k == 0)
        def _():
            o_ref[...] = p + refs[2][...] if has_acc else p

        @pl.when(k > 0)
        def _():
            o_ref[...] += p

    specs = [pl.BlockSpec((tm, tk), lambda i, k: (i, k)), pl.BlockSpec((N, tk), lambda i, k: (0, k))]
    args = [a, b]
    if has_acc:
        specs.append(pl.BlockSpec((tm, N), lambda i, k: (i, 0)))
        args.append(acc)
    return _call(body, name, (M // tm, K // tk), specs, pl.BlockSpec((tm, N), lambda i, k: (i, 0)),
                 jax.ShapeDtypeStruct((M, N), f32), sem=("parallel", "arbitrary"))(*args)


def mm_tn(a, b, name, tk=1024):
    K, M = a.shape
    N = b.shape[1]
    tn = _tile(N)

    def body(a_ref, b_ref, o_ref):
        k = pl.program_id(1)
        p = _dot(a_ref[...], b_ref[...], TN)

        @pl.when(k == 0)
        def _():
            o_ref[...] = p

        @pl.when(k > 0)
        def _():
            o_ref[...] += p

    return _call(body, name, (N // tn, K // tk),
                 [pl.BlockSpec((tk, M), lambda j, k: (k, 0)), pl.BlockSpec((tk, tn), lambda j, k: (k, j))],
                 pl.BlockSpec((M, tn), lambda j, k: (0, j)), jax.ShapeDtypeStruct((M, N), f32),
                 sem=("parallel", "arbitrary"))(a, b)


def rms_fwd(x, w, name, tm=512):
    S = x.shape[0]

    def body(x_ref, w_ref, o_ref):
        o_ref[...] = _rms(x_ref[...], w_ref[...]).astype(bf16)

    return _call(body, name, (S // tm,),
                 [pl.BlockSpec((tm, D_MODEL), lambda i: (i, 0)), pl.BlockSpec((1, D_MODEL), lambda i: (0, 0))],
                 pl.BlockSpec((tm, D_MODEL), lambda i: (i, 0)), jax.ShapeDtypeStruct((S, D_MODEL), bf16),
                 sem=("parallel",))(x, w)


def rms_bwd(x, w, dh, dres, name, tm=512):
    S = x.shape[0]

    def body(x_ref, w_ref, dh_ref, dr_ref, dx_ref, dw_ref):
        _, vjp = jax.vjp(_rms, x_ref[...], w_ref[...])
        dx, dw = vjp(dh_ref[...])
        dx_ref[...] = dx + dr_ref[...]

        @pl.when(pl.program_id(0) == 0)
        def _():
            dw_ref[...] = dw

        @pl.when(pl.program_id(0) > 0)
        def _():
            dw_ref[...] += dw

    row = pl.BlockSpec((tm, D_MODEL), lambda i: (i, 0))
    vec = pl.BlockSpec((1, D_MODEL), lambda i: (0, 0))
    return _call(body, name, (S // tm,), [row, vec, row, row], [row, vec],
                 [jax.ShapeDtypeStruct((S, D_MODEL), f32), jax.ShapeDtypeStruct((1, D_MODEL), f32)],
                 sem=("arbitrary",))(x, w, dh, dres)


def loss_and_grad(y, target, name, tm=512):
    S = y.shape[0]

    def body(y_ref, t_ref, dy_ref, l_ref):
        diff = y_ref[...] - t_ref[...]
        dy_ref[...] = diff * (1.0 / D_MODEL)
        part = jnp.full((8, LANES), 0.5 / D_MODEL * jnp.sum(diff * diff), f32)

        @pl.when(pl.program_id(0) == 0)
        def _():
            l_ref[...] = part

        @pl.when(pl.program_id(0) > 0)
        def _():
            l_ref[...] += part

    row = pl.BlockSpec((tm, D_MODEL), lambda i: (i, 0))
    return _call(body, name, (S // tm,), [row, row], [row, pl.BlockSpec((8, LANES), lambda i: (0, 0))],
                 [jax.ShapeDtypeStruct((S, D_MODEL), f32), jax.ShapeDtypeStruct((8, LANES), f32)],
                 sem=("arbitrary",))(y, target)


def _s5_discretize(a_re, a_im, log_step, b_re, b_im, c_re, c_im):
    step = jnp.exp(log_step)[:, None]
    mag = jnp.exp(a_re * step)
    ang = a_im * step
    lam_re, lam_im = mag * jnp.cos(ang), mag * jnp.sin(ang)
    num_re, num_im = lam_re - 1.0, lam_im
    den = a_re * a_re + a_im * a_im
    f_re = (num_re * a_re + num_im * a_im) / den
    f_im = (num_im * a_re - num_re * a_im) / den
    bb_re = f_re[..., None] * b_re - f_im[..., None] * b_im
    bb_im = f_re[..., None] * b_im + f_im[..., None] * b_re
    eye = jnp.eye(8, dtype=f32)

    def block_in(bb):
        t = bb.transpose(0, 2, 1).reshape(4, 8, 16, 1, 64)
        return (t * eye[None, :, None, :, None]).reshape(4, 128, 512)

    def block_out(c):
        t = c.transpose(0, 2, 1).reshape(4, 8, 64, 1, 16)
        return (t * eye[None, :, None, :, None]).reshape(4, 512, 128)

    return (lam_re.reshape(1, S5_STATES), lam_im.reshape(1, S5_STATES), block_in(bb_re), block_in(bb_im),
            block_out(c_re), block_out(c_im))


def _lam_powers(lam_re, lam_im):
    rows_re, rows_im = [lam_re], [lam_im]
    for _ in range(7):
        pr, pi = rows_re[-1], rows_im[-1]
        rows_re.append(pr * lam_re - pi * lam_im)
        rows_im.append(pr * lam_im + pi * lam_re)
    return jnp.concatenate(rows_re, 0), jnp.concatenate(rows_im, 0)


def s5_fwd(u, pw_re, pw_im, w_re, w_im, c_re, c_im, dvec, name):
    S = u.shape[0]
    R, NS = S5_ROWS, S5_STATES
    nb = R // 8

    def body(u_ref, pwr_ref, pwi_ref, wre_ref, wim_ref, cre_ref, cim_ref, d_ref, y_ref, hr_ref, hi_ref,
             car_re, car_im, cin_re, cin_im, up, yp):
        @pl.when(pl.program_id(0) == 0)
        def _():
            car_re[...] = jnp.zeros_like(car_re)
            car_im[...] = jnp.zeros_like(car_im)

        slab = lambda r: pl.ds(r * nb, nb)
        for r in range(8):
            up[slab(r), :] = u_ref[:, r, :]
        u = up[...]
        for j in range(4):
            uj = u[:, 128 * j:128 * (j + 1)]
            hr_ref[:, 512 * j:512 * (j + 1)] = _dot(uj, wre_ref[j], NN)
            hi_ref[:, 512 * j:512 * (j + 1)] = _dot(uj, wim_ref[j], NN)
        lr, li = pwr_ref[0:1, :], pwi_ref[0:1, :]
        for r in range(1, 8):
            pr, pi = hr_ref[slab(r - 1), :], hi_ref[slab(r - 1), :]
            hr_ref[slab(r), :] = lr * pr - li * pi + hr_ref[slab(r), :]
            hi_ref[slab(r), :] = lr * pi + li * pr + hi_ref[slab(r), :]
        l8r, l8i = pwr_ref[7:8, :], pwi_ref[7:8, :]

        def across(c, carry):
            gr, gi = carry
            cin_re[pl.ds(c, 1), :] = gr
            cin_im[pl.ds(c, 1), :] = gi
            er, ei = hr_ref[pl.ds(7 * nb + c, 1), :], hi_ref[pl.ds(7 * nb + c, 1), :]
            return l8r * gr - l8i * gi + er, l8r * gi + l8i * gr + ei

        gr, gi = lax.fori_loop(0, nb, across, (car_re[...], car_im[...]))
        car_re[...] = gr
        car_im[...] = gi
        cr, ci = cin_re[...], cin_im[...]
        for r in range(8):
            pr, pi = pwr_ref[r:r + 1, :], pwi_ref[r:r + 1, :]
            hr_ref[slab(r), :] = hr_ref[slab(r), :] + pr * cr - pi * ci
            hi_ref[slab(r), :] = hi_ref[slab(r), :] + pr * ci + pi * cr
        for j in range(4):
            sl = slice(512 * j, 512 * (j + 1))
            cs = slice(128 * j, 128 * (j + 1))
            yp[:, cs] = (_dot(hr_ref[:, sl], cre_ref[j], NN) - _dot(hi_ref[:, sl], cim_ref[j], NN)
                         + d_ref[:, cs] * u[:, cs])
        for r in range(8):
            y_ref[:, r, :] = yp[slab(r), :]

    full = lambda shape: pl.BlockSpec(shape, lambda i: (0,) * len(shape))
    hspec = pl.BlockSpec((R, NS), lambda i: (i, 0))
    uspec = pl.BlockSpec((nb, 8, 512), lambda i: (i, 0, 0))
    y, h_re, h_im = _call(
        body, name, (S // R,),
        [uspec, full((8, NS)), full((8, NS)), full((4, 128, 512)),
         full((4, 128, 512)), full((4, 512, 128)), full((4, 512, 128)), full((1, 512))],
        [uspec, hspec, hspec],
        [jax.ShapeDtypeStruct((S // 8, 8, 512), f32), jax.ShapeDtypeStruct((S, NS), f32),
         jax.ShapeDtypeStruct((S, NS), f32)],
        scratch=[pltpu.VMEM((1, NS), f32), pltpu.VMEM((1, NS), f32), pltpu.VMEM((nb, NS), f32),
                 pltpu.VMEM((nb, NS), f32), pltpu.VMEM((R, 512), f32), pltpu.VMEM((R, 512), f32)],
        sem=("arbitrary",))(u.reshape(S // 8, 8, 512), pw_re, pw_im, w_re.astype(bf16), w_im.astype(bf16),
                            c_re.astype(bf16), c_im.astype(bf16), dvec)
    return y.reshape(S, 512), h_re, h_im


def s5_bwd(dy, u, h_re, h_im, pw_re, pw_im, w_re, w_im, c_re, c_im, dvec, name):
    S = u.shape[0]
    R, NS = S5_ROWS, S5_STATES
    nb = R // 8
    nchunk = S // R

    def body(dy_ref, u_ref, hr_ref, hi_ref, hpr_ref, hpi_ref, pwr_ref, pwi_ref, wre_ref, wim_ref, cre_ref, cim_ref,
             d_ref, du_ref, dwre_ref, dwim_ref, dcre_ref, dcim_ref, dlr_ref, dli_ref, dd_ref,
             ar, ai, car_re, car_im, cin_re, cin_im, up, dyp, dup):
        i = pl.program_id(0)

        @pl.when(i == 0)
        def _():
            for ref in (car_re, car_im, dwre_ref, dwim_ref, dcre_ref, dcim_ref, dlr_ref, dli_ref, dd_ref):
                ref[...] = jnp.zeros_like(ref)

        slab = lambda r: pl.ds(r * nb, nb)
        for r in range(8):
            up[slab(r), :] = u_ref[:, r, :]
            dyp[slab(r), :] = dy_ref[:, r, :]
        dy = dyp[...]
        u = up[...]
        for j in range(4):
            dyj = dy[:, 128 * j:128 * (j + 1)]
            ar[:, 512 * j:512 * (j + 1)] = _dot(dyj, cre_ref[j], NT)
            ai[:, 512 * j:512 * (j + 1)] = -_dot(dyj, cim_ref[j], NT)
        lr, li = pwr_ref[0:1, :], pwi_ref[0:1, :]
        for r in range(6, -1, -1):
            nr, ni = ar[slab(r + 1), :], ai[slab(r + 1), :]
            ar[slab(r), :] = lr * nr + li * ni + ar[slab(r), :]
            ai[slab(r), :] = lr * ni - li * nr + ai[slab(r), :]
        l8r, l8i = pwr_ref[7:8, :], pwi_ref[7:8, :]

        def across(k, carry):
            c = nb - 1 - k
            gr, gi = carry
            cin_re[pl.ds(c, 1), :] = gr
            cin_im[pl.ds(c, 1), :] = gi
            er, ei = ar[pl.ds(c, 1), :], ai[pl.ds(c, 1), :]
            return l8r * gr + l8i * gi + er, l8r * gi - l8i * gr + ei

        gr, gi = lax.fori_loop(0, nb, across, (car_re[...], car_im[...]))
        car_re[...] = gr
        car_im[...] = gi
        cr, ci = cin_re[...], cin_im[...]
        for r in range(8):
            pr, pi = pwr_ref[7 - r:8 - r, :], pwi_ref[7 - r:8 - r, :]
            ar[slab(r), :] = ar[slab(r), :] + pr * cr + pi * ci
            ai[slab(r), :] = ai[slab(r), :] + pr * ci - pi * cr

        acc_r = jnp.zeros((1, NS), f32)
        acc_i = jnp.zeros((1, NS), f32)
        has_prev = (i < nchunk - 1).astype(f32)
        top = lax.broadcasted_iota(jnp.int32, (nb, NS), 0) == 0
        for r in range(8):
            if r == 0:
                xr = jnp.where(top, hpr_ref[7:8, :] * has_prev, pltpu.roll(hr_ref[slab(7), :], 1, 0))
                xi = jnp.where(top, hpi_ref[7:8, :] * has_prev, pltpu.roll(hi_ref[slab(7), :], 1, 0))
            else:
                xr, xi = hr_ref[slab(r - 1), :], hi_ref[slab(r - 1), :]
            br, bi = ar[slab(r), :], ai[slab(r), :]
            acc_r += jnp.sum(br * xr + bi * xi, axis=0, keepdims=True)
            acc_i += jnp.sum(bi * xr - br * xi, axis=0, keepdims=True)
        dlr_ref[...] += acc_r
        dli_ref[...] += acc_i
        dd_ref[...] += jnp.sum(dy * u, axis=0, keepdims=True)

        for j in range(4):
            sl = slice(512 * j, 512 * (j + 1))
            cs = slice(128 * j, 128 * (j + 1))
            arj, aij = ar[:, sl], ai[:, sl]
            uj, dyj = u[:, cs], dy[:, cs]
            dup[:, cs] = _dot(arj, wre_ref[j], NT) + _dot(aij, wim_ref[j], NT) + d_ref[:, cs] * dyj
            dwre_ref[j] += _dot(uj, arj, TN)
            dwim_ref[j] += _dot(uj, aij, TN)
            dcre_ref[j] += _dot(hr_ref[:, sl], dyj, TN)
            dcim_ref[j] -= _dot(hi_ref[:, sl], dyj, TN)
        for r in range(8):
            du_ref[:, r, :] = dup[slab(r), :]

    rev = lambda i: nchunk - 1 - i
    full = lambda shape: pl.BlockSpec(shape, lambda i: (0,) * len(shape))
    row = pl.BlockSpec((nb, 8, 512), lambda i: (rev(i), 0, 0))
    hspec = pl.BlockSpec((R, NS), lambda i: (rev(i), 0))
    hprev = pl.BlockSpec((8, NS), lambda i: (jnp.maximum(rev(i) * nb - 1, 0), 0))
    outs = _call(
        body, name, (nchunk,),
        [row, row, hspec, hspec, hprev, hprev, full((8, NS)), full((8, NS)), full((4, 128, 512)), full((4, 128, 512)),
         full((4, 512, 128)), full((4, 512, 128)), full((1, 512))],
        [row, full((4, 128, 512)), full((4, 128, 512)), full((4, 512, 128)), full((4, 512, 128)),
         full((1, NS)), full((1, NS)), full((1, 512))],
        [jax.ShapeDtypeStruct((S // 8, 8, 512), f32), jax.ShapeDtypeStruct((4, 128, 512), f32),
         jax.ShapeDtypeStruct((4, 128, 512), f32), jax.ShapeDtypeStruct((4, 512, 128), f32),
         jax.ShapeDtypeStruct((4, 512, 128), f32), jax.ShapeDtypeStruct((1, NS), f32),
         jax.ShapeDtypeStruct((1, NS), f32), jax.ShapeDtypeStruct((1, 512), f32)],
        scratch=[pltpu.VMEM((R, NS), f32), pltpu.VMEM((R, NS), f32), pltpu.VMEM((1, NS), f32),
                 pltpu.VMEM((1, NS), f32), pltpu.VMEM((nb, NS), f32), pltpu.VMEM((nb, NS), f32),
                 pltpu.VMEM((R, 512), f32), pltpu.VMEM((R, 512), f32), pltpu.VMEM((R, 512), f32)],
        sem=("arbitrary",))(dy.reshape(S // 8, 8, 512), u.reshape(S // 8, 8, 512), h_re, h_im, h_re, h_im, pw_re,
                            pw_im, w_re.astype(bf16), w_im.astype(bf16), c_re.astype(bf16), c_im.astype(bf16), dvec)
    return (outs[0].reshape(S, 512),) + tuple(outs[1:])


def _rows(start, n, d):
    return pl.ds(pl.multiple_of(start, ATT_BLOCK), n) if d == 1 else pl.ds(start, n, stride=d)


def _head_masks():
    lane = lax.broadcasted_iota(jnp.int32, (1, LANES), 1)
    return [(lane < 64).astype(f32), (lane >= 64).astype(f32)]


def _head_norm(x, w, hm):
    x2 = x * x
    r = [lax.rsqrt(jnp.sum(x2 * hm[h], axis=-1, keepdims=True) * (1.0 / 64) + RMS_EPS) for h in range(2)]
    sc = hm[0] * r[0] + hm[1] * r[1]
    return x * sc * w, sc, r


def _head_norm_bwd(x, w, sc, r, dxn, hm):
    dw = jnp.sum(dxn * x * sc, axis=0, keepdims=True)
    t = dxn * w
    tx = t * x
    corr = sum(hm[h] * (r[h] * r[h] * r[h]) * jnp.sum(tx * hm[h], axis=-1, keepdims=True) for h in range(2))
    return t * sc - x * corr * (1.0 / 64), dw


def _att_mask(has_prev):
    qi = lax.broadcasted_iota(jnp.int32, (ATT_BLOCK, 2 * ATT_BLOCK), 0) + ATT_BLOCK
    kj = lax.broadcasted_iota(jnp.int32, (ATT_BLOCK, 2 * ATT_BLOCK), 1)
    return (qi - kj >= 0) & (qi - kj <= ATT_BLOCK) & (has_prev | (kj >= ATT_BLOCK))


def _att_block_bwd(q, k, v, o, lse, do, dlse, qw, kw, has_prev):
    hm = _head_masks()
    mask = _att_mask(has_prev)
    qn, qsc, qr = _head_norm(q, qw, hm)
    kn, ksc, kr = _head_norm(k, kw, hm)
    dqn = jnp.zeros((ATT_BLOCK, LANES), f32)
    dkn = jnp.zeros((2 * ATT_BLOCK, LANES), f32)
    dv = jnp.zeros((2 * ATT_BLOCK, LANES), f32)
    for h in range(2):
        qh, do_h = qn * hm[h], do * hm[h]
        s = _dot(qh, kn, NT) * 0.125
        p = jnp.exp(jnp.where(mask, s - lse[:, 64 * h:64 * h + 1], -jnp.inf))
        dp = _dot(do_h, v, NT)
        delta = jnp.sum(do_h * o, axis=-1, keepdims=True)
        dl = jnp.sum(dlse * hm[h], axis=-1, keepdims=True)
        ds = p * (dp - delta + dl) * 0.125
        dqn = dqn + hm[h] * _dot(ds, kn, NN)
        dkn = dkn + _dot(ds, qh, TN)
        dv = dv + _dot(p, do_h, TN)
    dq, dqw = _head_norm_bwd(q, qw, qsc, qr, dqn, hm)
    dk, dkw = _head_norm_bwd(k, kw, ksc, kr, dkn, hm)
    return dq, dk, dv, dqw, dkw


def _att_block(q, k, v, qw, kw, has_prev):
    hm = _head_masks()
    qn, kn = _head_norm(q, qw, hm)[0], _head_norm(k, kw, hm)[0]
    mask = _att_mask(has_prev)
    o = jnp.zeros((ATT_BLOCK, LANES), f32)
    lse = jnp.zeros((ATT_BLOCK, LANES), f32)
    for h in range(2):
        s = _bdot(qn * hm[h], kn, NT) * 0.125
        s = jnp.where(mask, s, -jnp.inf)
        m = jnp.max(s, axis=-1, keepdims=True)
        p = jnp.exp(s - m)
        l = jnp.sum(p, axis=-1, keepdims=True)
        o = o + hm[h] * _bdot(p / l, v, NN)
        lse = lse + hm[h] * (m + jnp.log(l))
    return o, lse


def att_fwd(p_att, qw, kw, d, name):
    S = p_att.shape[0]
    SEG = ATT_SEG
    nblk = SEG // ATT_BLOCK

    def body(p_ref, qw_ref, kw_ref, o_ref, l_ref, q_s, k_ext, v_ext, o_s, l_s):
        seg = pl.program_id(1)

        @pl.when(seg == 0)
        def _():
            k_ext[SEG:, :] = jnp.zeros((SEG, LANES), f32)
            v_ext[SEG:, :] = jnp.zeros((SEG, LANES), f32)

        k_ext[:SEG, :] = k_ext[SEG:, :]
        v_ext[:SEG, :] = v_ext[SEG:, :]
        q_s[...] = p_ref[:, 0:128]
        k_ext[SEG:, :] = p_ref[:, 128:256]
        v_ext[SEG:, :] = p_ref[:, 256:384]
        qw_v, kw_v = qw_ref[...], kw_ref[...]

        def blk(b, carry):
            j, r = b // d, b % d
            qs = j * (ATT_BLOCK * d) + r
            ks = SEG + qs - ATT_BLOCK * d
            o, lse = _att_block(q_s[_rows(qs, ATT_BLOCK, d), :], k_ext[_rows(ks, 2 * ATT_BLOCK, d), :],
                                v_ext[_rows(ks, 2 * ATT_BLOCK, d), :], qw_v, kw_v, (seg > 0) | (j > 0))
            o_s[_rows(qs, ATT_BLOCK, d), :] = o
            l_s[_rows(qs, ATT_BLOCK, d), :] = lse
            return carry

        lax.fori_loop(0, nblk, blk, 0, unroll=4)
        o_ref[...] = o_s[...]
        l_ref[...] = l_s[...]

    vec = pl.BlockSpec((1, LANES), lambda hh, s: (0, 0))
    out = pl.BlockSpec((SEG, LANES), lambda hh, s: (s, hh))
    return _call(body, name, (2, S // SEG), [pl.BlockSpec((SEG, 384), lambda hh, s: (s, hh)), vec, vec],
                 [out, out], [jax.ShapeDtypeStruct((S, 256), f32), jax.ShapeDtypeStruct((S, 256), f32)],
                 scratch=[pltpu.VMEM((SEG, LANES), f32), pltpu.VMEM((2 * SEG, LANES), f32),
                          pltpu.VMEM((2 * SEG, LANES), f32), pltpu.VMEM((SEG, LANES), f32),
                          pltpu.VMEM((SEG, LANES), f32)],
                 sem=("arbitrary", "arbitrary"))(p_att, qw, kw)


def att_bwd(p_att, o, lse, do, dlse, qw, kw, d, name):
    S = p_att.shape[0]
    SEG = ATT_SEG
    nseg = S // SEG
    nblk = SEG // ATT_BLOCK

    def body(p_ref, pp_ref, o_ref, l_ref, do_ref, dl_ref, qw_ref, kw_ref, dp_ref, dqw_ref, dkw_ref,
             q_s, k_ext, v_ext, dq_s, dk_ext, dv_ext):
        hh, i = pl.program_id(0), pl.program_id(1)
        seg = nseg - 1 - i

        @pl.when(i == 0)
        def _():
            dk_ext[...] = jnp.zeros_like(dk_ext)
            dv_ext[...] = jnp.zeros_like(dv_ext)

        @pl.when((i == 0) & (hh == 0))
        def _():
            dqw_ref[...] = jnp.zeros_like(dqw_ref)
            dkw_ref[...] = jnp.zeros_like(dkw_ref)

        dk_ext[SEG:, :] = dk_ext[:SEG, :]
        dv_ext[SEG:, :] = dv_ext[:SEG, :]
        dk_ext[:SEG, :] = jnp.zeros((SEG, LANES), f32)
        dv_ext[:SEG, :] = jnp.zeros((SEG, LANES), f32)
        q_s[...] = p_ref[:, 0:128]
        k_ext[SEG:, :] = p_ref[:, 128:256]
        v_ext[SEG:, :] = p_ref[:, 256:384]
        k_ext[:SEG, :] = pp_ref[:, 128:256]
        v_ext[:SEG, :] = pp_ref[:, 256:384]
        qw_v, kw_v = qw_ref[...], kw_ref[...]

        def blk_pair(i2, carry):
            dqw, dkw = carry
            done = []
            for u in range(2):
                b = 2 * i2 + u
                j, r = b // d, b % d
                qs = j * (ATT_BLOCK * d) + r
                ks = SEG + qs - ATT_BLOCK * d
                has_prev = (seg > 0) | (j > 0)
                qrows, krows = _rows(qs, ATT_BLOCK, d), _rows(ks, 2 * ATT_BLOCK, d)
                dq, dk, dv, dqw_b, dkw_b = _att_block_bwd(
                    q_s[qrows, :], k_ext[krows, :], v_ext[krows, :], o_ref[qrows, :], l_ref[qrows, :],
                    do_ref[qrows, :], dl_ref[qrows, :], qw_v, kw_v, has_prev)
                dqw, dkw = dqw + dqw_b, dkw + dkw_b
                done.append((qrows, krows, dq, dk, dv))
            for qrows, krows, dq, dk, dv in done:
                dq_s[qrows, :] = dq
                dk_ext[krows, :] = dk_ext[krows, :] + dk
                dv_ext[krows, :] = dv_ext[krows, :] + dv
            return dqw, dkw

        zero = jnp.zeros((1, LANES), f32)
        dqw, dkw = lax.fori_loop(0, nblk // 2, blk_pair, (zero, zero))
        dqw_ref[...] += dqw
        dkw_ref[...] += dkw
        dp_ref[:, 0:128] = dq_s[...].astype(bf16)
        dp_ref[:, 128:256] = dk_ext[SEG:, :].astype(bf16)
        dp_ref[:, 256:384] = dv_ext[SEG:, :].astype(bf16)

    rev = lambda i: nseg - 1 - i
    vec = pl.BlockSpec((1, LANES), lambda hh, i: (0, 0))
    cur = pl.BlockSpec((SEG, 384), lambda hh, i: (rev(i), hh))
    prev = pl.BlockSpec((SEG, 384), lambda hh, i: (jnp.maximum(rev(i) - 1, 0), hh))
    col = pl.BlockSpec((SEG, LANES), lambda hh, i: (rev(i), hh))
    big = pltpu.VMEM((2 * SEG, LANES), f32)
    one = pltpu.VMEM((SEG, LANES), f32)
    return _call(body, name, (2, nseg), [cur, prev, col, col, col, col, vec, vec], [cur, vec, vec],
                 [jax.ShapeDtypeStruct((S, 768), bf16), jax.ShapeDtypeStruct((1, LANES), f32),
                  jax.ShapeDtypeStruct((1, LANES), f32)],
                 scratch=[one, big, big, one, big, big],
                 sem=("arbitrary", "arbitrary"))(p_att, p_att, o, lse, do, dlse, qw, kw)


def conv_fwd(p_ssd, conv_w, conv_b, name):
    S = p_ssd.shape[0]
    tm, C = CONV_ROWS, SSD_XBC

    def body(x_ref, xp_ref, w_ref, b_ref, o_ref):
        first = (pl.program_id(0) == 0)
        ext = jnp.concatenate([jnp.where(first, 0.0, xp_ref[:, 0:C]), x_ref[:, 0:C]], axis=0)
        acc = b_ref[...] + w_ref[3:4, :] * ext[8:, :]
        for k in range(1, 4):
            acc = acc + w_ref[3 - k:4 - k, :] * pltpu.roll(ext, k, 0)[8:, :]
        o_ref[...] = jax.nn.silu(acc)

    return _call(body, name, (S // tm,),
                 [pl.BlockSpec((tm, 1536), lambda i: (i, 0)),
                  pl.BlockSpec((8, 1536), lambda i: (jnp.maximum(i * (tm // 8) - 1, 0), 0)),
                  pl.BlockSpec((4, C), lambda i: (0, 0)), pl.BlockSpec((1, C), lambda i: (0, 0))],
                 pl.BlockSpec((tm, C), lambda i: (i, 0)), jax.ShapeDtypeStruct((S, C), f32),
                 sem=("parallel",))(p_ssd, p_ssd, conv_w, conv_b)


def conv_bwd(p_ssd, dact, ddt, conv_w, conv_b, name):
    S = p_ssd.shape[0]
    tm, C = CONV_ROWS, SSD_XBC
    nblk = S // tm

    def body(x_ref, xp_ref, xn_ref, da_ref, dan_ref, ddt_ref, w_ref, b_ref, dp_ref, dw_ref, db_ref):
        i = pl.program_id(0)
        rows = tm + 8
        ext = jnp.concatenate([jnp.where(i == 0, 0.0, xp_ref[:, 0:C]), x_ref[:, 0:C], xn_ref[:, 0:C]], axis=0)
        shifted = [ext[8:, :]] + [pltpu.roll(ext, k, 0)[8:, :] for k in range(1, 4)]
        pre = b_ref[...] + w_ref[3:4, :] * shifted[0]
        for k in range(1, 4):
            pre = pre + w_ref[3 - k:4 - k, :] * shifted[k]
        sg = jax.nn.sigmoid(pre)
        dact = jnp.concatenate([da_ref[...], jnp.where(i == nblk - 1, 0.0, dan_ref[...])], axis=0)
        dpre = dact * (sg * (1.0 + pre * (1.0 - sg)))
        dx = w_ref[3:4, :] * dpre[0:tm, :]
        for k in range(1, 4):
            dx = dx + w_ref[3 - k:4 - k, :] * pltpu.roll(dpre, rows - k, 0)[0:tm, :]
        dp_ref[:, 0:C] = dx.astype(bf16)
        dp_ref[:, C:C + 128] = ddt_ref[...].astype(bf16)
        dp_ref[:, C + 128:] = jnp.zeros((tm, 128), bf16)
        dcur = dpre[0:tm, :]
        dws = [jnp.sum(dcur * shifted[3 - j][0:tm, :], axis=0, keepdims=True) for j in range(4)]
        dbs = jnp.sum(dcur, axis=0, keepdims=True)

        @pl.when(i == 0)
        def _():
            dw_ref[...] = jnp.zeros_like(dw_ref)
            db_ref[...] = jnp.zeros_like(db_ref)

        for j in range(4):
            dw_ref[j:j + 1, :] += dws[j]
        db_ref[...] += dbs

    t8 = tm // 8
    return _call(body, name, (nblk,),
                 [pl.BlockSpec((tm, 1536), lambda i: (i, 0)),
                  pl.BlockSpec((8, 1536), lambda i: (jnp.maximum(i * t8 - 1, 0), 0)),
                  pl.BlockSpec((8, 1536), lambda i: (jnp.minimum((i + 1) * t8, S // 8 - 1), 0)),
                  pl.BlockSpec((tm, C), lambda i: (i, 0)),
                  pl.BlockSpec((8, C), lambda i: (jnp.minimum((i + 1) * t8, S // 8 - 1), 0)),
                  pl.BlockSpec((tm, 128), lambda i: (i, 0)),
                  pl.BlockSpec((4, C), lambda i: (0, 0)), pl.BlockSpec((1, C), lambda i: (0, 0))],
                 [pl.BlockSpec((tm, 1536), lambda i: (i, 0)), pl.BlockSpec((4, C), lambda i: (0, 0)),
                  pl.BlockSpec((1, C), lambda i: (0, 0))],
                 [jax.ShapeDtypeStruct((S, 1536), bf16), jax.ShapeDtypeStruct((4, C), f32),
                  jax.ShapeDtypeStruct((1, C), f32)],
                 sem=("arbitrary",))(p_ssd, p_ssd, p_ssd, dact, dact, ddt, conv_w, conv_b)


def _ssd_chunk(xbc, dtr, state, dt_bias, a_log, d_full):
    T = SSD_CHUNK
    r_i = lax.broadcasted_iota(jnp.int32, (T, T), 0)
    c_i = lax.broadcasted_iota(jnp.int32, (T, T), 1)
    tril = c_i <= r_i
    tri = tril.astype(bf16)
    lane = lax.broadcasted_iota(jnp.int32, (1, LANES), 1)
    hm = [(lane < 64).astype(f32), (lane >= 64).astype(f32)]
    column = lambda v, h: jnp.broadcast_to(v[:, h:h + 1], (T, LANES))

    def per_head_lanes(v):
        return jnp.concatenate([jnp.where(lane < 64, column(v, 2 * pp), column(v, 2 * pp + 1)) for pp in range(6)],
                               axis=1)

    xs, bm, cm = xbc[:, :768], xbc[:, 768:1024], xbc[:, 1024:1280]
    dt = _softplus(dtr + dt_bias)
    a_dt = dt * (-jnp.exp(a_log))
    a_cs = _xdot_l(tri, a_dt)
    dt_full = per_head_lanes(dt)
    acs_full = per_head_lanes(a_cs)
    last = lax.broadcasted_iota(jnp.int32, (T, SSD_WIDTH), 0) == T - 1
    tot_full = jnp.sum(jnp.where(last, acs_full, 0.0), axis=0, keepdims=True)
    xdt = xs * dt_full
    xw = xdt * jnp.exp(tot_full - acs_full)
    eacs = jnp.exp(acs_full)
    st_parts, off_parts, diag_parts = [], [], []
    for g in range(2):
        bg, cg = bm[:, 128 * g:128 * (g + 1)], cm[:, 128 * g:128 * (g + 1)]
        cols = slice(384 * g, 384 * (g + 1))
        st_parts.append(_bdot(bg, xw[:, cols], TN))
        off_parts.append(_bdot(cg, state[:, cols], NN))
        cb = _bdot(cg, bg, NT)
        for pp in range(3 * g, 3 * g + 3):
            xp = xdt[:, 128 * pp:128 * (pp + 1)]
            acc = jnp.zeros((T, LANES), f32)
            for hh in range(2):
                a_col = column(a_cs, 2 * pp + hh)
                decay = jnp.where(tril, jnp.exp(jnp.minimum(a_col - a_col.T, 0.0)), 0.0)
                acc = acc + _bdot(cb * decay, xp * hm[hh], NN)
            diag_parts.append(acc)
    new_state = state * jnp.exp(tot_full) + jnp.concatenate(st_parts, axis=1)
    y = jnp.concatenate(diag_parts, axis=1) + jnp.concatenate(off_parts, axis=1) * eacs + xs * d_full
    return y, new_state


def ssd_fwd(xact, p_ssd, dt_bias, a_log, d_full, name):
    S = xact.shape[0]
    T = SSD_CHUNK

    def body(x_ref, p_ref, b_ref, a_ref, d_ref, y_ref, s_ref, state):
        @pl.when(pl.program_id(0) == 0)
        def _():
            state[...] = jnp.zeros_like(state)

        st = state[...]
        s_ref[0] = st
        y, new = _ssd_chunk(x_ref[...], p_ref[...], st, b_ref[...], a_ref[...], d_ref[...])
        y_ref[...] = y
        state[...] = new

    vec = lambda n: pl.BlockSpec((1, n), lambda i: (0, 0))
    return _call(body, name, (S // T,),
                 [pl.BlockSpec((T, SSD_XBC), lambda i: (i, 0)), pl.BlockSpec((T, 128), lambda i: (i, 10)),
                  vec(128), vec(128), vec(768)],
                 [pl.BlockSpec((T, 768), lambda i: (i, 0)), pl.BlockSpec((1, T, 768), lambda i: (i, 0, 0))],
                 [jax.ShapeDtypeStruct((S, 768), f32), jax.ShapeDtypeStruct((S // T, T, 768), f32)],
                 scratch=[pltpu.VMEM((T, 768), f32)], sem=("arbitrary",))(xact, p_ssd, dt_bias, a_log, d_full)


def ssd_bwd(xact, p_ssd, states, dy, dt_bias, a_log, d_full, name):
    S = xact.shape[0]
    T = SSD_CHUNK
    nc = S // T

    def body(x_ref, p_ref, s_ref, dy_ref, b_ref, a_ref, d_ref, dx_ref, ddt_ref, db_ref, da_ref, dd_ref, dstate):
        i = pl.program_id(0)

        @pl.when(i == 0)
        def _():
            for ref in (dstate, db_ref, da_ref, dd_ref):
                ref[...] = jnp.zeros_like(ref)

        _, vjp = jax.vjp(_ssd_chunk, x_ref[...], p_ref[...], s_ref[0], b_ref[...], a_ref[...], d_ref[...])
        dx, ddt, dst, db, da, dd = vjp((dy_ref[...], dstate[...]))
        dx_ref[...] = dx
        ddt_ref[...] = ddt
        dstate[...] = dst
        db_ref[...] += db
        da_ref[...] += da
        dd_ref[...] += dd

    rev = lambda i: nc - 1 - i
    vec = lambda n: pl.BlockSpec((1, n), lambda i: (0, 0))
    return _call(body, name, (nc,),
                 [pl.BlockSpec((T, SSD_XBC), lambda i: (rev(i), 0)), pl.BlockSpec((T, 128), lambda i: (rev(i), 10)),
                  pl.BlockSpec((1, T, 768), lambda i: (rev(i), 0, 0)), pl.BlockSpec((T, 768), lambda i: (rev(i), 0)),
                  vec(128), vec(128), vec(768)],
                 [pl.BlockSpec((T, SSD_XBC), lambda i: (rev(i), 0)), pl.BlockSpec((T, 128), lambda i: (rev(i), 0)),
                  vec(128), vec(128), vec(768)],
                 [jax.ShapeDtypeStruct((S, SSD_XBC), f32), jax.ShapeDtypeStruct((S, 128), f32),
                  jax.ShapeDtypeStruct((1, 128), f32), jax.ShapeDtypeStruct((1, 128), f32),
                  jax.ShapeDtypeStruct((1, 768), f32)],
                 scratch=[pltpu.VMEM((T, 768), f32)],
                 sem=("arbitrary",))(xact, p_ssd, states, dy, dt_bias, a_log, d_full)


def _tail_fn(ys5, pt, o0, o1, o2, l0, l1, l2, yssd, glu_b, nw, pr_glu, pr_a, pr_b, pr_c, x, weights):
    glu_w, pa, pb, pc, wo = weights
    gates = jax.nn.sigmoid(pt[:, :3072])
    za, zb, zc = pt[:, 3072:3584], pt[:, 3584:3840], pt[:, 3840:4608]
    g = jax.nn.gelu(ys5)
    ya = g * jax.nn.sigmoid(_cdot(g, glu_w, NN) + glu_b + pr_glu) * jax.nn.silu(za)
    m = jnp.maximum(jnp.maximum(l0, l1), l2)
    e0, e1, e2 = jnp.exp(l0 - m), jnp.exp(l1 - m), jnp.exp(l2 - m)
    yb = (e0 * o0 + e1 * o1 + e2 * o2) / (e0 + e1 + e2) * jax.nn.silu(zb)
    yc = _rms(yssd * jax.nn.silu(zc), nw)
    merged = (gates[:, :1024] * (_cdot(ya, pa, NN) + pr_a) + gates[:, 1024:2048] * (_cdot(yb, pb, NN) + pr_b)
              + gates[:, 2048:] * (_cdot(yc, pc, NN) + pr_c))
    out = x + _cdot(merged, wo, NN)
    return out, (g, ya, yb, yc, merged)


def _tail_specs(tm):
    row = lambda n: pl.BlockSpec((tm, n), lambda i: (i, 0))
    full = lambda a, b: pl.BlockSpec((a, b), lambda i: (0, 0))
    acts = [row(512), row(4608)] + [row(256)] * 6 + [row(768), row(D_MODEL)]
    consts = [full(1, 512), full(1, 768), full(512, 512), full(512, D_MODEL), full(256, D_MODEL),
              full(768, D_MODEL), full(D_MODEL, D_MODEL)]
    return row, full, acts, consts


def tail_fwd(ys5, pt, os_, ls_, yssd, x, glu_b, nw, weights, name):
    S = x.shape[0]
    tm = 2 * TAIL_ROWS
    row, full, acts, consts = _tail_specs(tm)

    def body(ys5_ref, pt_ref, o0, o1, o2, l0, l1, l2, yssd_ref, x_ref, gb_ref, nw_ref, gw, pa, pb, pc, wo, out_ref):
        z = lambda n: jnp.zeros((tm, n), f32)
        out, _ = _tail_fn(ys5_ref[...], pt_ref[...], o0[...], o1[...], o2[...], l0[...], l1[...], l2[...],
                          yssd_ref[...], gb_ref[...], nw_ref[...], z(512), z(D_MODEL), z(D_MODEL), z(D_MODEL),
                          x_ref[...], (gw[...], pa[...], pb[...], pc[...], wo[...]))
        out_ref[...] = out

    return _call(body, name, (S // tm,), acts + consts, row(D_MODEL), jax.ShapeDtypeStruct((S, D_MODEL), f32),
                 sem=("parallel",))(ys5, pt, *os_, *ls_, yssd, x, glu_b, nw, *weights)


def tail_bwd(ys5, pt, os_, ls_, yssd, dout, glu_b, nw, weights, name):
    S = dout.shape[0]
    tm = TAIL_ROWS
    row, full, acts, consts = _tail_specs(tm)

    def body(ys5_ref, pt_ref, o0, o1, o2, l0, l1, l2, yssd_ref, dout_ref, gb_ref, nw_ref, gw, pa, pb, pc, wo,
             dys5_ref, dpt_ref, do0, do1, do2, dl0, dl1, dl2, dyssd_ref, dgb_ref, dnw_ref,
             g_ref, ya_ref, yb_ref, yc_ref, mg_ref, dglu_ref, dpa_ref, dpb_ref, dpc_ref):
        z = lambda n: jnp.zeros((tm, n), f32)
        w = (gw[...], pa[...], pb[...], pc[...], wo[...])
        fn = lambda *a: _tail_fn(*a, z(D_MODEL), w)
        _, vjp, aux = jax.vjp(fn, ys5_ref[...], pt_ref[...], o0[...], o1[...], o2[...], l0[...], l1[...], l2[...],
                              yssd_ref[...], gb_ref[...], nw_ref[...], z(512), z(D_MODEL), z(D_MODEL), z(D_MODEL),
                              has_aux=True)
        (dys5, dpt, d0, d1, d2, e0, e1, e2, dyssd, dgb, dnw, dglu, dpa, dpb, dpc) = vjp(dout_ref[...])
        dys5_ref[...] = dys5
        dpt_ref[...] = dpt.astype(bf16)
        for ref, val in ((do0, d0), (do1, d1), (do2, d2), (dl0, e0), (dl1, e1), (dl2, e2)):
            ref[...] = val
        dyssd_ref[...] = dyssd
        g, ya, yb, yc, merged = aux
        for ref, val in ((g_ref, g), (ya_ref, ya), (yb_ref, yb), (yc_ref, yc), (mg_ref, merged),
                         (dglu_ref, dglu), (dpa_ref, dpa), (dpb_ref, dpb), (dpc_ref, dpc)):
            ref[...] = val.astype(bf16)

        @pl.when(pl.program_id(0) == 0)
        def _():
            dgb_ref[...] = dgb
            dnw_ref[...] = dnw

        @pl.when(pl.program_id(0) > 0)
        def _():
            dgb_ref[...] += dgb
            dnw_ref[...] += dnw

    sd = lambda n, dt=f32: jax.ShapeDtypeStruct((S, n), dt)
    out_specs = ([row(512), row(4608)] + [row(256)] * 6 + [row(768), full(1, 512), full(1, 768)]
                 + [row(512), row(512), row(256), row(768), row(D_MODEL), row(512)] + [row(D_MODEL)] * 3)
    out_shape = ([sd(512), sd(4608, bf16)] + [sd(256)] * 6 + [sd(768), jax.ShapeDtypeStruct((1, 512), f32),
                                                          jax.ShapeDtypeStruct((1, 768), f32)]
                 + [sd(512, bf16), sd(512, bf16), sd(256, bf16), sd(768, bf16), sd(D_MODEL, bf16), sd(512, bf16)]
                 + [sd(D_MODEL, bf16)] * 3)
    return _call(body, name, (S // tm,), acts + consts, out_specs, out_shape,
                 sem=("arbitrary",))(ys5, pt, *os_, *ls_, yssd, dout, glu_b, nw, *weights)


def _in_proj_segments(shards):
    dtype = shards[0].dtype

    def c(a, b):
        parts = []
        for k, sh in enumerate(shards):
            lo, hi = max(a, W_IN_SHARD * k), min(b, W_IN_SHARD * (k + 1))
            if lo < hi:
                parts.append(sh[:, lo - W_IN_SHARD * k:hi - W_IN_SHARD * k])
        return parts[0] if len(parts) == 1 else jnp.concatenate(parts, axis=1)

    atts = []
    for g in range(3):
        parts = []
        for hh in range(2):
            o = 64 * (4 * g + 2 * hh)
            parts += [c(_C_Q + o, _C_Q + o + 128), c(_C_K + o, _C_K + o + 128), c(_C_V + o, _C_V + o + 128)]
        atts.append(jnp.concatenate(parts, axis=1))
    ssd = jnp.concatenate([c(_C_XBC, _C_ZC), jnp.zeros((D_MODEL, 1536 - (_C_ZC - _C_XBC)), dtype)], axis=1)
    tail = jnp.concatenate([c(_C_GATE, _C_END), c(_C_ZA, _C_Q), c(_C_ZB, _C_XBC), c(_C_ZC, _C_GATE)], axis=1)
    return [c(_C_UA, _C_ZA)] + atts + [ssd, tail]


def _in_proj_grad(ds5, datts, dssd, dtail):
    pick = lambda off: [datts[g][:, 384 * hh + off:384 * hh + off + 128] for g in range(3) for hh in range(2)]
    pieces = ([ds5, dtail[:, 3072:3584]] + pick(0) + pick(128) + pick(256)
              + [dtail[:, 3584:3840], dssd[:, :_C_ZC - _C_XBC], dtail[:, 3840:4608], dtail[:, :3072]])
    shards, start = [[] for _ in range(4)], 0
    for piece in pieces:
        width = piece.shape[1]
        for k in range(4):
            lo, hi = max(start, W_IN_SHARD * k), min(start + width, W_IN_SHARD * (k + 1))
            if lo < hi:
                shards[k].append(piece[:, lo - start:hi - start])
        start += width
    return jnp.stack([jnp.concatenate(s, axis=1) for s in shards])


def _prep_layer(p):
    q = {}
    q["segs"] = [s.astype(bf16) for s in _in_proj_segments(p["w_in"])]
    disc = _s5_discretize(p["s5_a_re"], p["s5_a_im"], p["s5_log_step"], p["s5_b_re"], p["s5_b_im"],
                          p["s5_c_re"], p["s5_c_im"])
    q["s5"] = disc
    q["pw"] = _lam_powers(disc[0], disc[1])
    q["s5_d"] = p["s5_d"].reshape(1, 512)
    q["qw"] = jnp.tile(p["q_norm_w"], 2).reshape(1, LANES)
    q["kw"] = jnp.tile(p["k_norm_w"], 2).reshape(1, LANES)
    q["conv_w"] = p["conv_w"]
    q["conv_b"] = p["conv_b"].reshape(1, SSD_XBC)
    pad = lambda v: jnp.pad(v, (0, LANES - v.shape[0])).reshape(1, LANES)
    q["dt_bias"], q["a_log"] = pad(p["dt_bias"]), pad(p["ssd_a_log"])
    q["d_full"] = jnp.repeat(p["ssd_d"], 64).reshape(1, SSD_WIDTH)
    q["glu_b"] = p["s5_glu_b"].reshape(1, 512)
    q["nw"] = p["ssd_norm_w"].reshape(1, SSD_WIDTH)
    q["norm_w"] = p["norm_w"].reshape(1, D_MODEL)
    q["tailw"] = tuple(p[n].astype(bf16) for n in ("s5_glu_w", "proj_a", "proj_b", "proj_c", "w_out"))
    return q


_DILATIONS = (1, 4, 16)


def layer_fwd(x, q, tag):
    h = rms_fwd(x, q["norm_w"], f"rms_fwd{tag}")
    p_s5, p_a0, p_a1, p_a2, p_ssd, p_tail = [mm_nn(h, w, f"inproj{k}{tag}") for k, w in enumerate(q["segs"])]
    _, _, w_re, w_im, c_re, c_im = q["s5"]
    ys5, h_re, h_im = s5_fwd(p_s5, *q["pw"], w_re, w_im, c_re, c_im, q["s5_d"], f"s5_fwd{tag}")
    p_atts = (p_a0, p_a1, p_a2)
    os_, ls_ = [], []
    for g, d in enumerate(_DILATIONS):
        o, l = att_fwd(p_atts[g], q["qw"], q["kw"], d, f"att_fwd{g}{tag}")
        os_.append(o)
        ls_.append(l)
    xact = conv_fwd(p_ssd, q["conv_w"], q["conv_b"], f"conv_fwd{tag}")
    yssd, states = ssd_fwd(xact, p_ssd, q["dt_bias"], q["a_log"], q["d_full"], f"ssd_fwd{tag}")
    out = tail_fwd(ys5, p_tail, os_, ls_, yssd, x, q["glu_b"], q["nw"], q["tailw"], f"tail_fwd{tag}")
    saved = dict(x=x, h=h, p_s5=p_s5, p_atts=p_atts, p_ssd=p_ssd, p_tail=p_tail, ys5=ys5, h_re=h_re, h_im=h_im,
                 os=os_, ls=ls_, xact=xact, yssd=yssd, states=states)
    return out, saved


def layer_bwd(dout, sv, q, p, tag):
    S = dout.shape[0]
    (dys5, dp_tail, do0, do1, do2, dl0, dl1, dl2, dyssd, dglu_b, dnw, g_b, ya_b, yb_b, yc_b, mg_b, dglu_b16,
     dpa_b, dpb_b, dpc_b) = tail_bwd(sv["ys5"], sv["p_tail"], sv["os"], sv["ls"], sv["yssd"], dout, q["glu_b"],
                                     q["nw"], q["tailw"], f"tail_bwd{tag}")
    grads = {}
    grads["s5_glu_w"] = mm_tn(g_b, dglu_b16, f"dglu_w{tag}")
    grads["proj_a"] = mm_tn(ya_b, dpa_b, f"dproj_a{tag}")
    grads["proj_b"] = mm_tn(yb_b, dpb_b, f"dproj_b{tag}")
    grads["proj_c"] = mm_tn(yc_b, dpc_b, f"dproj_c{tag}")
    grads["w_out"] = mm_tn(mg_b, dout, f"dw_out{tag}")
    grads["s5_glu_b"] = dglu_b.reshape(512)
    grads["ssd_norm_w"] = dnw.reshape(SSD_WIDTH)

    dxact, ddt, ddt_bias, da_log, dd_full = ssd_bwd(sv["xact"], sv["p_ssd"], sv["states"], dyssd, q["dt_bias"],
                                                    q["a_log"], q["d_full"], f"ssd_bwd{tag}")
    dp_ssd, dconv_w, dconv_b = conv_bwd(sv["p_ssd"], dxact, ddt, q["conv_w"], q["conv_b"], f"conv_bwd{tag}")
    grads["dt_bias"] = ddt_bias[0, :12]
    grads["ssd_a_log"] = da_log[0, :12]
    grads["ssd_d"] = dd_full.reshape(12, 64).sum(axis=1)
    grads["conv_w"] = dconv_w
    grads["conv_b"] = dconv_b.reshape(SSD_XBC)

    dp_atts, dqw, dkw = [], 0.0, 0.0
    for g, d in enumerate(_DILATIONS):
        dp, a, b = att_bwd(sv["p_atts"][g], sv["os"][g], sv["ls"][g], (do0, do1, do2)[g], (dl0, dl1, dl2)[g],
                           q["qw"], q["kw"], d,
                           f"att_bwd{g}{tag}")
        dp_atts.append(dp)
        dqw, dkw = dqw + a, dkw + b
    grads["q_norm_w"] = dqw.reshape(2, 64).sum(axis=0)
    grads["k_norm_w"] = dkw.reshape(2, 64).sum(axis=0)

    _, _, w_re, w_im, c_re, c_im = q["s5"]
    dp_s5, dwre, dwim, dcre, dcim, dlam_re, dlam_im, dd = s5_bwd(
        dys5, sv["p_s5"], sv["h_re"], sv["h_im"], *q["pw"], w_re, w_im, c_re, c_im, q["s5_d"], f"s5_bwd{tag}")
    s5_names = ("s5_a_re", "s5_a_im", "s5_log_step", "s5_b_re", "s5_b_im", "s5_c_re", "s5_c_im")
    _, disc_vjp = jax.vjp(_s5_discretize, *[p[n] for n in s5_names])
    for n, gr in zip(s5_names, disc_vjp((dlam_re, dlam_im, dwre, dwim, dcre, dcim))):
        grads[n] = gr
    grads["s5_d"] = dd.reshape(512)

    dsegs = [dp_s5] + dp_atts + [dp_ssd, dp_tail]
    dws = [mm_tn(sv["h"], ds, f"dw_in{k}{tag}") for k, ds in enumerate(dsegs)]
    grads["w_in"] = _in_proj_grad(dws[0], dws[1:4], dws[4], dws[5])
    dh = None
    for k, (ds, w) in enumerate(zip(dsegs, q["segs"])):
        dh = mm_nt(ds, w, f"dh{k}{tag}", acc=dh)
    dx, dnorm_w = rms_bwd(sv["x"], q["norm_w"], dh, dout, f"rms_bwd{tag}")
    grads["norm_w"] = dnorm_w.reshape(D_MODEL)
    return dx, grads


_ANY = pl.BlockSpec(memory_space=pl.ANY)


def _exchange(name, scatter=(), gather=(), sibling=(), sibling_both=False):
    scatter, gather, sibling = list(scatter), list(gather), list(sibling)
    chip_xs = scatter + gather
    ns, nc, nb = len(scatter), len(chip_xs), len(sibling)
    n = nc + nb

    def body(*refs):
        x_refs, o_refs, send_sems, recv_sems = refs[:n], refs[n:2 * n], refs[2 * n], refs[2 * n + 1]
        mx, my, mc = lax.axis_index("x"), lax.axis_index("y"), lax.axis_index("c")
        me = 2 * mx + my
        copies = []
        for a in range(nc):
            for t, (px, py) in enumerate(((1 - mx, my), (mx, 1 - my), (1 - mx, 1 - my))):
                src = x_refs[a].at[2 * px + py] if a < ns else x_refs[a]
                copies.append(pltpu.make_async_remote_copy(
                    src_ref=src, dst_ref=o_refs[a].at[me], send_sem=send_sems.at[3 * a + t],
                    recv_sem=recv_sems.at[3 * a + t], device_id=(px, py, mc), device_id_type=pl.DeviceIdType.MESH))
        for b in range(nc, n):
            k = 3 * nc + b - nc
            copies.append(pltpu.make_async_remote_copy(
                src_ref=x_refs[b], dst_ref=o_refs[b].at[mc] if sibling_both else o_refs[b], send_sem=send_sems.at[k],
                recv_sem=recv_sems.at[k], device_id=(mx, my, 1 - mc), device_id_type=pl.DeviceIdType.MESH))
        for cp in copies:
            cp.start()
        for cp in copies:
            cp.wait()

    shapes = ([(4,) + tuple(x.shape[1:]) for x in scatter] + [(4,) + tuple(x.shape) for x in gather]
              + [((2,) if sibling_both else ()) + tuple(x.shape) for x in sibling])
    xs = chip_xs + sibling
    outs = pl.pallas_call(
        body, name=name, in_specs=[_ANY] * n, out_specs=[_ANY] * n,
        out_shape=[jax.ShapeDtypeStruct(s, x.dtype) for s, x in zip(shapes, xs)],
        scratch_shapes=[pltpu.SemaphoreType.DMA((3 * nc + nb,)), pltpu.SemaphoreType.DMA((3 * nc + nb,))],
    )(*xs)
    me, c = 2 * lax.axis_index("x") + lax.axis_index("y"), lax.axis_index("c")
    fixed = []
    for a, (o, x) in enumerate(zip(outs, xs)):
        if a < ns:
            o = lax.dynamic_update_index_in_dim(o, lax.dynamic_index_in_dim(x, me, 0, keepdims=True), me, 0)
        elif a < nc:
            o = lax.dynamic_update_index_in_dim(o, x[None], me, 0)
        elif sibling_both:
            o = lax.dynamic_update_index_in_dim(o, x[None], c, 0)
        fixed.append(o)
    return fixed[:ns], fixed[ns:nc], fixed[nc:]


def _rows_tile(rows, row_bytes, budget=1 << 20):
    return next(t for t in (512, 256, 128, 64, 32, 16, 8) if rows % t == 0 and t * row_bytes <= budget)


def _padded_row_bytes(cols):
    return -(-cols // LANES) * LANES * 4


def _add2(a, b, name, out_dtype=f32):
    R, C = a.shape
    tr = _rows_tile(R, _padded_row_bytes(C))

    def body(a_ref, b_ref, o_ref):
        o_ref[...] = (a_ref[...] + b_ref[...]).astype(out_dtype)

    spec = pl.BlockSpec((tr, C), lambda i: (i, 0))
    return _call(body, name, (R // tr,), [spec, spec], spec, jax.ShapeDtypeStruct((R, C), out_dtype),
                 sem=("parallel",))(a, b)


def _sum4(x, name):
    R = x.shape[1]
    tr = _tile(R, (512, 256, 128))

    def body(x_ref, o_ref):
        p = [x_ref[j].astype(f32) for j in range(4)]
        o_ref[...] = ((p[0] + p[1]) + p[2]) + p[3]

    return _call(body, name, (R // tr,), [pl.BlockSpec((4, tr, LANES), lambda i: (0, i, 0))],
                 pl.BlockSpec((tr, LANES), lambda i: (i, 0)), jax.ShapeDtypeStruct((R, LANES), f32),
                 sem=("parallel",))(x)


def _adamw(g_parts, w, m, v, name):
    stacked = not isinstance(g_parts, (tuple, list))
    k = g_parts.shape[0] if stacked else len(g_parts)
    R, C = w.shape
    tr = _rows_tile(R, _padded_row_bytes(C))
    c1 = 1.0 - ADAM_B1 ** ADAM_STEP
    c2 = 1.0 - ADAM_B2 ** ADAM_STEP

    def body(*refs):
        w_ref, m_ref, v_ref, g_ref, d_ref, nm_ref, nv_ref = refs[-7:]
        if stacked:
            g = refs[0][0].astype(f32)
            for j in range(1, k):
                g = g + refs[0][j].astype(f32)
        else:
            g = refs[0][...]
            for r in refs[1:k]:
                g = g + r[...]
        m = ADAM_B1 * m_ref[...] + (1.0 - ADAM_B1) * g
        v = ADAM_B2 * v_ref[...] + (1.0 - ADAM_B2) * (g * g)
        g_ref[...] = g
        nm_ref[...] = m
        nv_ref[...] = v
        d_ref[...] = -ADAM_LR * ((m / c1) / (jnp.sqrt(v / c2) + ADAM_EPS) + ADAM_WD * w_ref[...])

    spec = pl.BlockSpec((tr, C), lambda i: (i, 0))
    sd = jax.ShapeDtypeStruct((R, C), f32)
    g_specs = [pl.BlockSpec((k, tr, C), lambda i: (0, i, 0))] if stacked else [spec] * k
    g_args = [g_parts] if stacked else list(g_parts)
    return _call(body, name, (R // tr,), g_specs + [spec] * 3, [spec] * 4, [sd] * 4,
                 sem=("parallel",))(*g_args, w, m, v)


def _pack(arrays):
    flat = jnp.concatenate([a.reshape(-1) for a in arrays])
    unit = PACK_ROWS * LANES
    n = -(-flat.shape[0] // unit) * unit
    return jnp.pad(flat, (0, n - flat.shape[0])).reshape(n // LANES, LANES)


def _unpack(buf, shapes):
    flat = buf.reshape(-1)
    out, off = [], 0
    for s in shapes:
        n = 1
        for dim in s:
            n *= dim
        out.append(flat[off:off + n].reshape(s))
        off += n
    return out


def _to_shards(full, axis):
    s = full.shape
    t = full.reshape(s[:axis] + (4, s[axis] // 4) + s[axis + 1:])
    return jnp.moveaxis(t, axis, 0)


def _from_shards(sh, axis):
    t = jnp.moveaxis(sh, 0, axis)
    s = t.shape
    return t.reshape(s[:axis] + (s[axis] * s[axis + 1],) + s[axis + 2:])


def kernel(x, norm_w, w_in, s5_a_re, s5_a_im, s5_log_step, s5_b_re, s5_b_im, s5_c_re, s5_c_im, s5_d, s5_glu_w, s5_glu_b, q_norm_w, k_norm_w, conv_w, conv_b, dt_bias, ssd_a_log, ssd_d, ssd_norm_w, proj_a, proj_b, proj_c, w_out, loss_target, m_norm_w, m_w_in, m_s5_a_re, m_s5_a_im, m_s5_log_step, m_s5_b_re, m_s5_b_im, m_s5_c_re, m_s5_c_im, m_s5_d, m_s5_glu_w, m_s5_glu_b, m_q_norm_w, m_k_norm_w, m_conv_w, m_conv_b, m_dt_bias, m_ssd_a_log, m_ssd_d, m_ssd_norm_w, m_proj_a, m_proj_b, m_proj_c, m_w_out, v_norm_w, v_w_in, v_s5_a_re, v_s5_a_im, v_s5_log_step, v_s5_b_re, v_s5_b_im, v_s5_c_re, v_s5_c_im, v_s5_d, v_s5_glu_w, v_s5_glu_b, v_q_norm_w, v_k_norm_w, v_conv_w, v_conv_b, v_dt_bias, v_ssd_a_log, v_ssd_d, v_ssd_norm_w, v_proj_a, v_proj_b, v_proj_c, v_w_out):
    given = dict(locals())
    W = {n: given[n] for n in _WEIGHTS}
    M = {n: given["m_" + n] for n in _WEIGHTS}
    V = {n: given["v_" + n] for n in _WEIGHTS}
    n_layers = norm_w.shape[0]
    assert n_layers == 2
    c = lax.axis_index("c")

    mine_of = lambda t: lax.dynamic_index_in_dim(t, c, 0, keepdims=False)
    as_payload = lambda n: lax.bitcast_convert_type(W[n], bf16) if n == "conv_w" else W[n].astype(bf16)
    payload_shapes = [W[n].shape + ((2,) if n == "conv_w" else ()) for n, _ in _SHARDED]
    _, (gathered, w_in_mine_layer), _ = _exchange(
        "gather_weights", gather=[_pack([as_payload(n) for n, _ in _SHARDED]), mine_of(w_in).astype(bf16)])
    _, _, (w_in_layers,) = _exchange("share_w_in", sibling=[w_in_mine_layer], sibling_both=True)
    full = dict(W)
    pieces = [_unpack(gathered[j], payload_shapes) for j in range(4)]
    for k, (n, axis) in enumerate(_SHARDED):
        sh = jnp.stack([pieces[j][k] for j in range(4)])
        full[n] = _from_shards(lax.bitcast_convert_type(sh, f32) if n == "conv_w" else sh, axis)

    xs = x[0]
    qs, saves = [], []
    act = xs
    for l in range(n_layers):
        p = {n: full[n][l] for n in _WEIGHTS if n != "w_in"}
        p["w_in"] = [w_in_layers[l, k] for k in range(4)]
        q = _prep_layer(p)
        act, sv = layer_fwd(act, q, f"_l{l}")
        qs.append((q, p))
        saves.append(sv)
    dact, lsum = loss_and_grad(act, loss_target[0], "loss")
    loss = lax.psum(lsum[0, 0], ("x", "y", "c"))
    layer_grads = [None] * n_layers
    for l in reversed(range(n_layers)):
        q, p = qs[l]
        dact, layer_grads[l] = layer_bwd(dact, saves[l], q, p, f"_l{l}")
    grad_x = dact[None]
    G = {n: jnp.stack([layer_grads[l][n] for l in range(n_layers)]) for n in _WEIGHTS if n != "w_in"}

    repl_shapes = [W[n].shape for n in _REPL]
    small = _pack([G[n] for n in _REPL])
    quarter = small.shape[0] // 4
    big = [_to_shards(G[n], axis).reshape(4, -1) for n, axis in _SHARDED]
    big = jnp.concatenate(big, axis=1)
    unit = PACK_ROWS * LANES
    nbig = -(-big.shape[1] // unit) * unit
    big = jnp.pad(big, ((0, 0), (0, nbig - big.shape[1]))).reshape(4, nbig // LANES, LANES)
    gpack = jnp.concatenate([big, small.reshape(4, quarter, LANES)], axis=1)
    rbig = nbig // LANES
    g0, g1 = layer_grads[0]["w_in"], layer_grads[1]["w_in"]

    (landed_pack,), _, (from_sibling,) = _exchange(
        "swap_w_in_grads_and_scatter_grads", scatter=[gpack.astype(bf16)], sibling=[jnp.where(c == 0, g1, g0)])
    flat = lambda t: t.reshape(4 * D_MODEL, W_IN_SHARD)
    shards = _add2(flat(jnp.where(c == 0, g0, g1)), flat(from_sibling), "sum_cores_w_in", out_dtype=bf16)
    mine = _sum4(landed_pack, "sum_chips")

    (landed,), _, (other,) = _exchange(
        "scatter_w_in_grads_and_swap_cores", scatter=[shards.reshape(4, D_MODEL, W_IN_SHARD)], sibling=[mine])
    w_in_mine = _adamw(landed, mine_of(w_in), mine_of(m_w_in), mine_of(v_w_in), "adamw_w_in")
    gq = _add2(mine[rbig:], other[rbig:], "sum_cores_small")

    _, (gsmall,), w_in_out = _exchange(
        "share_w_in_updates_and_gather_small", gather=[gq], sibling=w_in_mine, sibling_both=True)
    gsmall = gsmall.reshape(4 * quarter, LANES)

    wp, mp, vp = (_pack([T[n] for n, _ in _SHARDED]) for T in (W, M, V))
    outs_big = _adamw((mine[:rbig], other[:rbig]), wp, mp, vp, "adamw_sharded")
    big_out = [_unpack(o, [W[n].shape for n, _ in _SHARDED]) for o in outs_big]
    ws, ms, vs = (_pack([T[n] for n in _REPL]) for T in (W, M, V))
    outs_small = _adamw((gsmall,), ws, ms, vs, "adamw_replicated")
    small_out = [_unpack(o, repl_shapes) for o in outs_small]

    res = [dict(), dict(), dict(), dict()]
    for kind in range(4):
        res[kind]["w_in"] = w_in_out[kind]
        for k, (n, _) in enumerate(_SHARDED):
            res[kind][n] = big_out[kind][k]
        for k, n in enumerate(_REPL):
            res[kind][n] = small_out[kind][k]
    return (loss, grad_x, *[res[0][n] for n in _WEIGHTS], *[res[1][n] for n in _WEIGHTS],
            *[res[2][n] for n in _WEIGHTS], *[res[3][n] for n in _WEIGHTS])
```

```python
import functools

import jax
import jax.numpy as jnp
from jax import lax
from jax.experimental import pallas as pl
from jax.experimental.pallas import tpu as pltpu

f32 = jnp.float32
bf16 = jnp.bfloat16

D_MODEL = 1024
RMS_EPS = 1e-6
V7X_VMEM_LIMIT = 60 * 1024 * 1024
LANES = 128
NN, NT, TN = ((1,), (0,)), ((1,), (1,)), ((0,), (0,))

S5_STATES = 2048
S5_ROWS = 256
ATT_SEG = 2048
ATT_BLOCK = 128
SSD_CHUNK = 128
SSD_WIDTH = 768
SSD_XBC = 1280
CONV_ROWS = 512
TAIL_ROWS = 128

ADAM_LR, ADAM_B1, ADAM_B2, ADAM_EPS, ADAM_WD, ADAM_STEP = 0.001, 0.9, 0.999, 1e-08, 0.01, 10

_C_UA, _C_ZA, _C_Q, _C_K, _C_V, _C_ZB, _C_XBC, _C_DT, _C_ZC, _C_GATE, _C_END = (
    0, 512, 1024, 1792, 2560, 3328, 3584, 4864, 4876, 5644, 8716)

_SHARDED = (("s5_glu_w", 1), ("conv_w", 2), ("proj_a", 2), ("proj_b", 2), ("proj_c", 2), ("w_out", 1))
W_IN_SHARD = 2179
_REPL = ("norm_w", "s5_a_re", "s5_a_im", "s5_log_step", "s5_b_re", "s5_b_im", "s5_c_re", "s5_c_im", "s5_d",
         "s5_glu_b", "q_norm_w", "k_norm_w", "conv_b", "dt_bias", "ssd_a_log", "ssd_d", "ssd_norm_w")
_WEIGHTS = ("norm_w", "w_in", "s5_a_re", "s5_a_im", "s5_log_step", "s5_b_re", "s5_b_im", "s5_c_re", "s5_c_im",
            "s5_d", "s5_glu_w", "s5_glu_b", "q_norm_w", "k_norm_w", "conv_w", "conv_b", "dt_bias", "ssd_a_log",
            "ssd_d", "ssd_norm_w", "proj_a", "proj_b", "proj_c", "w_out")
PACK_ROWS = 512


def _dot(a, b, dims):
    return lax.dot_general(a.astype(bf16), b.astype(bf16), (dims, ((), ())), preferred_element_type=f32)


_ANY = pl.BlockSpec(memory_space=pl.ANY)

MAIN_WIDTH = 8448
MAIN_SSD_BLOCK = 3
MAIN_DT_BLOCK = 46
MAIN_ATT_BLOCK = 16


def _call(body, name, grid, in_specs, out_specs, out_shape, scratch=(), sem=None, aliases=None):
    return pl.pallas_call(
        body, name=name, grid=grid, in_specs=in_specs, out_specs=out_specs, out_shape=out_shape,
        scratch_shapes=list(scratch), input_output_aliases=aliases or {},
        compiler_params=pltpu.CompilerParams(dimension_semantics=sem, vmem_limit_bytes=V7X_VMEM_LIMIT))


def _tile(n, options=(1024, 768, 512, 384, 256, 128)):
    return next(t for t in options if n % t == 0)


@functools.partial(jax.custom_vjp, nondiff_argnums=(2,))
def _bdot(a, b, dims):
    return _dot(a, b, dims)


def _bdot_fwd(a, b, dims):
    return _dot(a, b, dims), (a, b)


def _bdot_bwd(dims, res, g):
    a, b = res
    if dims == NN:
        da, db = _dot(g, b, NT), _dot(a, g, TN)
    elif dims == NT:
        da, db = _dot(g, b, NN), _dot(g, a, TN)
    else:
        da, db = _dot(b, g, NT), _dot(a, g, NN)
    return da.astype(a.dtype), db.astype(b.dtype)


_bdot.defvjp(_bdot_fwd, _bdot_bwd)


@functools.partial(jax.custom_vjp, nondiff_argnums=(2,))
def _cdot(a, w, dims):
    return _dot(a, w, dims)


def _cdot_fwd(a, w, dims):
    return _dot(a, w, dims), w


def _cdot_bwd(dims, w, g):
    da = _dot(g, w, NT) if dims == NN else _dot(g, w, NN)
    return da, jnp.zeros_like(w)


_cdot.defvjp(_cdot_fwd, _cdot_bwd)


def _split3(x):
    hi = x.astype(bf16)
    r = x - hi.astype(f32)
    mid = r.astype(bf16)
    lo = (r - mid.astype(f32)).astype(bf16)
    return hi, mid, lo


@jax.custom_vjp
def _xdot_l(m, x):
    return sum(_dot(m, p, NN) for p in _split3(x))


def _xdot_l_fwd(m, x):
    return _xdot_l(m, x), m


def _xdot_l_bwd(m, g):
    return jnp.zeros_like(m), sum(_dot(m, p, TN) for p in _split3(g))


_xdot_l.defvjp(_xdot_l_fwd, _xdot_l_bwd)


@jax.custom_vjp
def _softplus(x):
    e = jnp.exp(-jnp.abs(x))
    u = 1.0 + e
    log1p = jnp.where(u == 1.0, e, jnp.log(u) * (e / jnp.where(u == 1.0, 1.0, u - 1.0)))
    return jnp.maximum(x, 0.0) + log1p


def _softplus_fwd(x):
    return _softplus(x), x


def _softplus_bwd(x, g):
    return (g * jax.nn.sigmoid(x),)


_softplus.defvjp(_softplus_fwd, _softplus_bwd)


def _rms(x, w):
    return x * lax.rsqrt(jnp.mean(x * x, axis=-1, keepdims=True) + RMS_EPS) * w


def mm_nn(a, b, name, tm=2048):
    M, K = a.shape
    N = b.shape[1]
    tn = _tile(N)

    def body(a_ref, b_ref, o_ref):
        o_ref[...] = _dot(a_ref[...], b_ref[...], NN)

    return _call(body, name, (M // tm, N // tn),
                 [pl.BlockSpec((tm, K), lambda i, j: (i, 0)), pl.BlockSpec((K, tn), lambda i, j: (0, j))],
                 pl.BlockSpec((tm, tn), lambda i, j: (i, j)), jax.ShapeDtypeStruct((M, N), f32),
                 sem=("parallel", "parallel"))(a, b)


def mm_nt(a, b, name, acc=None, tm=1024):
    M, K = a.shape
    N = b.shape[0]
    tk = _tile(K)
    has_acc = acc is not None

    def body(*refs):
        a_ref, b_ref = refs[0], refs[1]
        o_ref = refs[-1]
        k = pl.program_id(1)
        p = _dot(a_ref[...], b_ref[...], NT)

        @pl.when(k == 0)
        def _():
            o_ref[...] = p + refs[2][...] if has_acc else p

        @pl.when(k > 0)
        def _():
            o_ref[...] += p

    specs = [pl.BlockSpec((tm, tk), lambda i, k: (i, k)), pl.BlockSpec((N, tk), lambda i, k: (0, k))]
    args = [a, b]
    if has_acc:
        specs.append(pl.BlockSpec((tm, N), lambda i, k: (i, 0)))
        args.append(acc)
    return _call(body, name, (M // tm, K // tk), specs, pl.BlockSpec((tm, N), lambda i, k: (i, 0)),
                 jax.ShapeDtypeStruct((M, N), f32), sem=("parallel", "arbitrary"))(*args)


def mm_tn(a, b, name, tk=1024):
    K, M = a.shape
    N = b.shape[1]
    tn = _tile(N)

    def body(a_ref, b_ref, o_ref):
        k = pl.program_id(1)
        p = _dot(a_ref[...], b_ref[...], TN)

        @pl.when(k == 0)
        def _():
            o_ref[...] = p

        @pl.when(k > 0)
        def _():
            o_ref[...] += p

    return _call(body, name, (N // tn, K // tk),
                 [pl.BlockSpec((tk, M), lambda j, k: (k, 0)), pl.BlockSpec((tk, tn), lambda j, k: (k, j))],
                 pl.BlockSpec((M, tn), lambda j, k: (0, j)), jax.ShapeDtypeStruct((M, N), f32),
                 sem=("parallel", "arbitrary"))(a, b)


def rms_fwd(x, w, name, tm=512):
    S = x.shape[0]

    def body(x_ref, w_ref, o_ref):
        o_ref[...] = _rms(x_ref[...], w_ref[...]).astype(bf16)

    return _call(body, name, (S // tm,),
                 [pl.BlockSpec((tm, D_MODEL), lambda i: (i, 0)), pl.BlockSpec((1, D_MODEL), lambda i: (0, 0))],
                 pl.BlockSpec((tm, D_MODEL), lambda i: (i, 0)), jax.ShapeDtypeStruct((S, D_MODEL), bf16),
                 sem=("parallel",))(x, w)


def rms_bwd(x, w, dh, dres, name, tm=512):
    S = x.shape[0]

    def body(x_ref, w_ref, dh_ref, dr_ref, dx_ref, dw_ref):
        _, vjp = jax.vjp(_rms, x_ref[...], w_ref[...])
        dx, dw = vjp(dh_ref[...])
        dx_ref[...] = dx + dr_ref[...]

        @pl.when(pl.program_id(0) == 0)
        def _():
            dw_ref[...] = dw

        @pl.when(pl.program_id(0) > 0)
        def _():
            dw_ref[...] += dw

    row = pl.BlockSpec((tm, D_MODEL), lambda i: (i, 0))
    vec = pl.BlockSpec((1, D_MODEL), lambda i: (0, 0))
    return _call(body, name, (S // tm,), [row, vec, row, row], [row, vec],
                 [jax.ShapeDtypeStruct((S, D_MODEL), f32), jax.ShapeDtypeStruct((1, D_MODEL), f32)],
                 sem=("arbitrary",))(x, w, dh, dres)


def loss_and_grad(y, target, name, tm=512):
    S = y.shape[0]

    def body(y_ref, t_ref, dy_ref, l_ref):
        diff = y_ref[...] - t_ref[...]
        dy_ref[...] = diff * (1.0 / D_MODEL)
        part = jnp.full((8, LANES), 0.5 / D_MODEL * jnp.sum(diff * diff), f32)

        @pl.when(pl.program_id(0) == 0)
        def _():
            l_ref[...] = part

        @pl.when(pl.program_id(0) > 0)
        def _():
            l_ref[...] += part

    row = pl.BlockSpec((tm, D_MODEL), lambda i: (i, 0))
    return _call(body, name, (S // tm,), [row, row], [row, pl.BlockSpec((8, LANES), lambda i: (0, 0))],
                 [jax.ShapeDtypeStruct((S, D_MODEL), f32), jax.ShapeDtypeStruct((8, LANES), f32)],
                 sem=("arbitrary",))(y, target)


def _s5_discretize(a_re, a_im, log_step, b_re, b_im, c_re, c_im):
    step = jnp.exp(log_step)[:, None]
    mag = jnp.exp(a_re * step)
    ang = a_im * step
    lam_re, lam_im = mag * jnp.cos(ang), mag * jnp.sin(ang)
    num_re, num_im = lam_re - 1.0, lam_im
    den = a_re * a_re + a_im * a_im
    f_re = (num_re * a_re + num_im * a_im) / den
    f_im = (num_im * a_re - num_re * a_im) / den
    bb_re = f_re[..., None] * b_re - f_im[..., None] * b_im
    bb_im = f_re[..., None] * b_im + f_im[..., None] * b_re
    eye = jnp.eye(8, dtype=f32)

    def block_in(bb):
        t = bb.transpose(0, 2, 1).reshape(4, 8, 16, 1, 64)
        return (t * eye[None, :, None, :, None]).reshape(4, 128, 512)

    def block_out(c):
        t = c.transpose(0, 2, 1).reshape(4, 8, 64, 1, 16)
        return (t * eye[None, :, None, :, None]).reshape(4, 512, 128)

    return (lam_re.reshape(1, S5_STATES), lam_im.reshape(1, S5_STATES), block_in(bb_re), block_in(bb_im),
            block_out(c_re), block_out(c_im))


def _lam_powers(lam_re, lam_im):
    rows_re, rows_im = [lam_re], [lam_im]
    for _ in range(7):
        pr, pi = rows_re[-1], rows_im[-1]
        rows_re.append(pr * lam_re - pi * lam_im)
        rows_im.append(pr * lam_im + pi * lam_re)
    return jnp.concatenate(rows_re, 0), jnp.concatenate(rows_im, 0)


def s5_fwd(u, pw_re, pw_im, w_re, w_im, c_re, c_im, dvec, name):
    S = u.shape[0]
    R, NS = S5_ROWS, S5_STATES
    nb = R // 8

    def body(u_ref, pwr_ref, pwi_ref, wre_ref, wim_ref, cre_ref, cim_ref, d_ref, y_ref, hr_ref, hi_ref,
             car_re, car_im, cin_re, cin_im, up, yp):
        @pl.when(pl.program_id(0) == 0)
        def _():
            car_re[...] = jnp.zeros_like(car_re)
            car_im[...] = jnp.zeros_like(car_im)

        slab = lambda r: pl.ds(r * nb, nb)
        for r in range(8):
            up[slab(r), :] = u_ref[:, r, :]
        u = up[...]
        for j in range(4):
            uj = u[:, 128 * j:128 * (j + 1)]
            hr_ref[:, 512 * j:512 * (j + 1)] = _dot(uj, wre_ref[j], NN)
            hi_ref[:, 512 * j:512 * (j + 1)] = _dot(uj, wim_ref[j], NN)
        lr, li = pwr_ref[0:1, :], pwi_ref[0:1, :]
        for r in range(1, 8):
            pr, pi = hr_ref[slab(r - 1), :], hi_ref[slab(r - 1), :]
            hr_ref[slab(r), :] = lr * pr - li * pi + hr_ref[slab(r), :]
            hi_ref[slab(r), :] = lr * pi + li * pr + hi_ref[slab(r), :]
        l8r, l8i = pwr_ref[7:8, :], pwi_ref[7:8, :]

        def across(c, carry):
            gr, gi = carry
            cin_re[pl.ds(c, 1), :] = gr
            cin_im[pl.ds(c, 1), :] = gi
            er, ei = hr_ref[pl.ds(7 * nb + c, 1), :], hi_ref[pl.ds(7 * nb + c, 1), :]
            return l8r * gr - l8i * gi + er, l8r * gi + l8i * gr + ei

        gr, gi = lax.fori_loop(0, nb, across, (car_re[...], car_im[...]))
        car_re[...] = gr
        car_im[...] = gi
        cr, ci = cin_re[...], cin_im[...]
        for r in range(8):
            pr, pi = pwr_ref[r:r + 1, :], pwi_ref[r:r + 1, :]
            hr_ref[slab(r), :] = hr_ref[slab(r), :] + pr * cr - pi * ci
            hi_ref[slab(r), :] = hi_ref[slab(r), :] + pr * ci + pi * cr
        for j in range(4):
            sl = slice(512 * j, 512 * (j + 1))
            cs = slice(128 * j, 128 * (j + 1))
            yp[:, cs] = (_dot(hr_ref[:, sl], cre_ref[j], NN) - _dot(hi_ref[:, sl], cim_ref[j], NN)
                         + d_ref[:, cs] * u[:, cs])
        for r in range(8):
            y_ref[:, r, :] = yp[slab(r), :]

    full = lambda shape: pl.BlockSpec(shape, lambda i: (0,) * len(shape))
    hspec = pl.BlockSpec((R, NS), lambda i: (i, 0))
    uspec = pl.BlockSpec((nb, 8, 512), lambda i: (i, 0, 0))
    y, h_re, h_im = _call(
        body, name, (S // R,),
        [uspec, full((8, NS)), full((8, NS)), full((4, 128, 512)),
         full((4, 128, 512)), full((4, 512, 128)), full((4, 512, 128)), full((1, 512))],
        [uspec, hspec, hspec],
        [jax.ShapeDtypeStruct((S // 8, 8, 512), f32), jax.ShapeDtypeStruct((S, NS), f32),
         jax.ShapeDtypeStruct((S, NS), f32)],
        scratch=[pltpu.VMEM((1, NS), f32), pltpu.VMEM((1, NS), f32), pltpu.VMEM((nb, NS), f32),
                 pltpu.VMEM((nb, NS), f32), pltpu.VMEM((R, 512), f32), pltpu.VMEM((R, 512), f32)],
        sem=("arbitrary",))(u.reshape(S // 8, 8, 512), pw_re, pw_im, w_re.astype(bf16), w_im.astype(bf16),
                            c_re.astype(bf16), c_im.astype(bf16), dvec)
    return y.reshape(S, 512), h_re, h_im


def s5_bwd(dy, u, h_re, h_im, pw_re, pw_im, w_re, w_im, c_re, c_im, dvec, name):
    S = u.shape[0]
    R, NS = S5_ROWS, S5_STATES
    nb = R // 8
    nchunk = S // R

    def body(dy_ref, u_ref, hr_ref, hi_ref, hpr_ref, hpi_ref, pwr_ref, pwi_ref, wre_ref, wim_ref, cre_ref, cim_ref,
             d_ref, du_ref, dwre_ref, dwim_ref, dcre_ref, dcim_ref, dlr_ref, dli_ref, dd_ref,
             ar, ai, car_re, car_im, cin_re, cin_im, up, dyp, dup):
        i = pl.program_id(0)

        @pl.when(i == 0)
        def _():
            for ref in (car_re, car_im, dwre_ref, dwim_ref, dcre_ref, dcim_ref, dlr_ref, dli_ref, dd_ref):
                ref[...] = jnp.zeros_like(ref)

        slab = lambda r: pl.ds(r * nb, nb)
        for r in range(8):
            up[slab(r), :] = u_ref[:, r, :]
            dyp[slab(r), :] = dy_ref[:, r, :]
        dy = dyp[...]
        u = up[...]
        for j in range(4):
            dyj = dy[:, 128 * j:128 * (j + 1)]
            ar[:, 512 * j:512 * (j + 1)] = _dot(dyj, cre_ref[j], NT)
            ai[:, 512 * j:512 * (j + 1)] = -_dot(dyj, cim_ref[j], NT)
        lr, li = pwr_ref[0:1, :], pwi_ref[0:1, :]
        for r in range(6, -1, -1):
            nr, ni = ar[slab(r + 1), :], ai[slab(r + 1), :]
            ar[slab(r), :] = lr * nr + li * ni + ar[slab(r), :]
            ai[slab(r), :] = lr * ni - li * nr + ai[slab(r), :]
        l8r, l8i = pwr_ref[7:8, :], pwi_ref[7:8, :]

        def across(k, carry):
            c = nb - 1 - k
            gr, gi = carry
            cin_re[pl.ds(c, 1), :] = gr
            cin_im[pl.ds(c, 1), :] = gi
            er, ei = ar[pl.ds(c, 1), :], ai[pl.ds(c, 1), :]
            return l8r * gr + l8i * gi + er, l8r * gi - l8i * gr + ei

        gr, gi = lax.fori_loop(0, nb, across, (car_re[...], car_im[...]))
        car_re[...] = gr
        car_im[...] = gi
        cr, ci = cin_re[...], cin_im[...]
        for r in range(8):
            pr, pi = pwr_ref[7 - r:8 - r, :], pwi_ref[7 - r:8 - r, :]
            ar[slab(r), :] = ar[slab(r), :] + pr * cr + pi * ci
            ai[slab(r), :] = ai[slab(r), :] + pr * ci - pi * cr

        acc_r = jnp.zeros((1, NS), f32)
        acc_i = jnp.zeros((1, NS), f32)
        has_prev = (i < nchunk - 1).astype(f32)
        top = lax.broadcasted_iota(jnp.int32, (nb, NS), 0) == 0
        for r in range(8):
            if r == 0:
                xr = jnp.where(top, hpr_ref[7:8, :] * has_prev, pltpu.roll(hr_ref[slab(7), :], 1, 0))
                xi = jnp.where(top, hpi_ref[7:8, :] * has_prev, pltpu.roll(hi_ref[slab(7), :], 1, 0))
            else:
                xr, xi = hr_ref[slab(r - 1), :], hi_ref[slab(r - 1), :]
            br, bi = ar[slab(r), :], ai[slab(r), :]
            acc_r += jnp.sum(br * xr + bi * xi, axis=0, keepdims=True)
            acc_i += jnp.sum(bi * xr - br * xi, axis=0, keepdims=True)
        dlr_ref[...] += acc_r
        dli_ref[...] += acc_i
        dd_ref[...] += jnp.sum(dy * u, axis=0, keepdims=True)

        for j in range(4):
            sl = slice(512 * j, 512 * (j + 1))
            cs = slice(128 * j, 128 * (j + 1))
            arj, aij = ar[:, sl], ai[:, sl]
            uj, dyj = u[:, cs], dy[:, cs]
            dup[:, cs] = _dot(arj, wre_ref[j], NT) + _dot(aij, wim_ref[j], NT) + d_ref[:, cs] * dyj
            dwre_ref[j] += _dot(uj, arj, TN)
            dwim_ref[j] += _dot(uj, aij, TN)
            dcre_ref[j] += _dot(hr_ref[:, sl], dyj, TN)
            dcim_ref[j] -= _dot(hi_ref[:, sl], dyj, TN)
        for r in range(8):
            du_ref[:, r, :] = dup[slab(r), :]

    rev = lambda i: nchunk - 1 - i
    full = lambda shape: pl.BlockSpec(shape, lambda i: (0,) * len(shape))
    row = pl.BlockSpec((nb, 8, 512), lambda i: (rev(i), 0, 0))
    hspec = pl.BlockSpec((R, NS), lambda i: (rev(i), 0))
    hprev = pl.BlockSpec((8, NS), lambda i: (jnp.maximum(rev(i) * nb - 1, 0), 0))
    outs = _call(
        body, name, (nchunk,),
        [row, row, hspec, hspec, hprev, hprev, full((8, NS)), full((8, NS)), full((4, 128, 512)), full((4, 128, 512)),
         full((4, 512, 128)), full((4, 512, 128)), full((1, 512))],
        [row, full((4, 128, 512)), full((4, 128, 512)), full((4, 512, 128)), full((4, 512, 128)),
         full((1, NS)), full((1, NS)), full((1, 512))],
        [jax.ShapeDtypeStruct((S // 8, 8, 512), f32), jax.ShapeDtypeStruct((4, 128, 512), f32),
         jax.ShapeDtypeStruct((4, 128, 512), f32), jax.ShapeDtypeStruct((4, 512, 128), f32),
         jax.ShapeDtypeStruct((4, 512, 128), f32), jax.ShapeDtypeStruct((1, NS), f32),
         jax.ShapeDtypeStruct((1, NS), f32), jax.ShapeDtypeStruct((1, 512), f32)],
        scratch=[pltpu.VMEM((R, NS), f32), pltpu.VMEM((R, NS), f32), pltpu.VMEM((1, NS), f32),
                 pltpu.VMEM((1, NS), f32), pltpu.VMEM((nb, NS), f32), pltpu.VMEM((nb, NS), f32),
                 pltpu.VMEM((R, 512), f32), pltpu.VMEM((R, 512), f32), pltpu.VMEM((R, 512), f32)],
        sem=("arbitrary",))(dy.reshape(S // 8, 8, 512), u.reshape(S // 8, 8, 512), h_re, h_im, h_re, h_im, pw_re,
                            pw_im, w_re.astype(bf16), w_im.astype(bf16), c_re.astype(bf16), c_im.astype(bf16), dvec)
    return (outs[0].reshape(S, 512),) + tuple(outs[1:])


def _rows(start, n, d):
    return pl.ds(pl.multiple_of(start, ATT_BLOCK), n) if d == 1 else pl.ds(start, n, stride=d)


def _head_masks():
    lane = lax.broadcasted_iota(jnp.int32, (1, LANES), 1)
    return [(lane < 64).astype(f32), (lane >= 64).astype(f32)]


def _head_norm(x, w, hm):
    x2 = x * x
    r = [lax.rsqrt(jnp.sum(x2 * hm[h], axis=-1, keepdims=True) * (1.0 / 64) + RMS_EPS) for h in range(2)]
    sc = hm[0] * r[0] + hm[1] * r[1]
    return x * sc * w, sc, r


def _head_norm_bwd(x, w, sc, r, dxn, hm):
    dw = jnp.sum(dxn * x * sc, axis=0, keepdims=True)
    t = dxn * w
    tx = t * x
    corr = sum(hm[h] * (r[h] * r[h] * r[h]) * jnp.sum(tx * hm[h], axis=-1, keepdims=True) for h in range(2))
    return t * sc - x * corr * (1.0 / 64), dw


def _att_mask(has_prev):
    qi = lax.broadcasted_iota(jnp.int32, (ATT_BLOCK, 2 * ATT_BLOCK), 0) + ATT_BLOCK
    kj = lax.broadcasted_iota(jnp.int32, (ATT_BLOCK, 2 * ATT_BLOCK), 1)
    return (qi - kj >= 0) & (qi - kj <= ATT_BLOCK) & (has_prev | (kj >= ATT_BLOCK))


def _att_block_bwd(q, k, v, o, lse, do, dlse, qw, kw, has_prev):
    hm = _head_masks()
    mask = _att_mask(has_prev)
    qn, qsc, qr = _head_norm(q, qw, hm)
    kn, ksc, kr = _head_norm(k, kw, hm)
    dqn = jnp.zeros((ATT_BLOCK, LANES), f32)
    dkn = jnp.zeros((2 * ATT_BLOCK, LANES), f32)
    dv = jnp.zeros((2 * ATT_BLOCK, LANES), f32)
    for h in range(2):
        qh, do_h = qn * hm[h], do * hm[h]
        s = _dot(qh, kn, NT) * 0.125
        p = jnp.exp(jnp.where(mask, s - lse[:, 64 * h:64 * h + 1], -jnp.inf))
        dp = _dot(do_h, v, NT)
        delta = jnp.sum(do_h * o, axis=-1, keepdims=True)
        dl = jnp.sum(dlse * hm[h], axis=-1, keepdims=True)
        ds = p * (dp - delta + dl) * 0.125
        dqn = dqn + hm[h] * _dot(ds, kn, NN)
        dkn = dkn + _dot(ds, qh, TN)
        dv = dv + _dot(p, do_h, TN)
    dq, dqw = _head_norm_bwd(q, qw, qsc, qr, dqn, hm)
    dk, dkw = _head_norm_bwd(k, kw, ksc, kr, dkn, hm)
    return dq, dk, dv, dqw, dkw


def _att_block(q, k, v, qw, kw, has_prev):
    hm = _head_masks()
    qn, kn = _head_norm(q, qw, hm)[0], _head_norm(k, kw, hm)[0]
    mask = _att_mask(has_prev)
    o = jnp.zeros((ATT_BLOCK, LANES), f32)
    lse = jnp.zeros((ATT_BLOCK, LANES), f32)
    for h in range(2):
        s = _bdot(qn * hm[h], kn, NT) * 0.125
        s = jnp.where(mask, s, -jnp.inf)
        m = jnp.max(s, axis=-1, keepdims=True)
        p = jnp.exp(s - m)
        l = jnp.sum(p, axis=-1, keepdims=True)
        o = o + hm[h] * _bdot(p / l, v, NN)
        lse = lse + hm[h] * (m + jnp.log(l))
    return o, lse


def att_fwd(p_att, qw, kw, d, g, name):
    S = p_att.shape[0]
    SEG = ATT_SEG
    nblk = SEG // ATT_BLOCK

    def body(p_ref, qw_ref, kw_ref, o_ref, l_ref, q_s, k_ext, v_ext, o_s, l_s):
        seg = pl.program_id(1)

        @pl.when(seg == 0)
        def _():
            k_ext[SEG:, :] = jnp.zeros((SEG, LANES), f32)
            v_ext[SEG:, :] = jnp.zeros((SEG, LANES), f32)

        k_ext[:SEG, :] = k_ext[SEG:, :]
        v_ext[:SEG, :] = v_ext[SEG:, :]
        q_s[...] = p_ref[:, 0:128]
        k_ext[SEG:, :] = p_ref[:, 128:256]
        v_ext[SEG:, :] = p_ref[:, 256:384]
        qw_v, kw_v = qw_ref[...], kw_ref[...]

        def blk(b, carry):
            j, r = b // d, b % d
            qs = j * (ATT_BLOCK * d) + r
            ks = SEG + qs - ATT_BLOCK * d
            o, lse = _att_block(q_s[_rows(qs, ATT_BLOCK, d), :], k_ext[_rows(ks, 2 * ATT_BLOCK, d), :],
                                v_ext[_rows(ks, 2 * ATT_BLOCK, d), :], qw_v, kw_v, (seg > 0) | (j > 0))
            o_s[_rows(qs, ATT_BLOCK, d), :] = o
            l_s[_rows(qs, ATT_BLOCK, d), :] = lse
            return carry

        lax.fori_loop(0, nblk, blk, 0, unroll=4)
        o_ref[...] = o_s[...]
        l_ref[...] = l_s[...]

    vec = pl.BlockSpec((1, LANES), lambda hh, s: (0, 0))
    out = pl.BlockSpec((SEG, LANES), lambda hh, s: (s, hh))
    return _call(body, name, (2, S // SEG), [pl.BlockSpec((SEG, 384), lambda hh, s: (s, MAIN_ATT_BLOCK + 2 * g + hh)), vec, vec],
                 [out, out], [jax.ShapeDtypeStruct((S, 256), f32), jax.ShapeDtypeStruct((S, 256), f32)],
                 scratch=[pltpu.VMEM((SEG, LANES), f32), pltpu.VMEM((2 * SEG, LANES), f32),
                          pltpu.VMEM((2 * SEG, LANES), f32), pltpu.VMEM((SEG, LANES), f32),
                          pltpu.VMEM((SEG, LANES), f32)],
                 sem=("arbitrary", "arbitrary"))(p_att, qw, kw)


def att_bwd(p_att, o, lse, do, dlse, qw, kw, d, g, dp_main, name):
    S = p_att.shape[0]
    SEG = ATT_SEG
    nseg = S // SEG
    nblk = SEG // ATT_BLOCK

    def body(p_ref, pp_ref, o_ref, l_ref, do_ref, dl_ref, qw_ref, kw_ref, _, dp_ref, dqw_ref, dkw_ref,
             q_s, k_ext, v_ext, dq_s, dk_ext, dv_ext):
        hh, i = pl.program_id(0), pl.program_id(1)
        seg = nseg - 1 - i

        @pl.when(i == 0)
        def _():
            dk_ext[...] = jnp.zeros_like(dk_ext)
            dv_ext[...] = jnp.zeros_like(dv_ext)

        @pl.when((i == 0) & (hh == 0))
        def _():
            dqw_ref[...] = jnp.zeros_like(dqw_ref)
            dkw_ref[...] = jnp.zeros_like(dkw_ref)

        dk_ext[SEG:, :] = dk_ext[:SEG, :]
        dv_ext[SEG:, :] = dv_ext[:SEG, :]
        dk_ext[:SEG, :] = jnp.zeros((SEG, LANES), f32)
        dv_ext[:SEG, :] = jnp.zeros((SEG, LANES), f32)
        q_s[...] = p_ref[:, 0:128]
        k_ext[SEG:, :] = p_ref[:, 128:256]
        v_ext[SEG:, :] = p_ref[:, 256:384]
        k_ext[:SEG, :] = pp_ref[:, 128:256]
        v_ext[:SEG, :] = pp_ref[:, 256:384]
        qw_v, kw_v = qw_ref[...], kw_ref[...]

        def blk_pair(i2, carry):
            dqw, dkw = carry
            done = []
            for u in range(2):
                b = 2 * i2 + u
                j, r = b // d, b % d
                qs = j * (ATT_BLOCK * d) + r
                ks = SEG + qs - ATT_BLOCK * d
                has_prev = (seg > 0) | (j > 0)
                qrows, krows = _rows(qs, ATT_BLOCK, d), _rows(ks, 2 * ATT_BLOCK, d)
                dq, dk, dv, dqw_b, dkw_b = _att_block_bwd(
                    q_s[qrows, :], k_ext[krows, :], v_ext[krows, :], o_ref[qrows, :], l_ref[qrows, :],
                    do_ref[qrows, :], dl_ref[qrows, :], qw_v, kw_v, has_prev)
                dqw, dkw = dqw + dqw_b, dkw + dkw_b
                done.append((qrows, krows, dq, dk, dv))
            for qrows, krows, dq, dk, dv in done:
                dq_s[qrows, :] = dq
                dk_ext[krows, :] = dk_ext[krows, :] + dk
                dv_ext[krows, :] = dv_ext[krows, :] + dv
            return dqw, dkw

        zero = jnp.zeros((1, LANES), f32)
        dqw, dkw = lax.fori_loop(0, nblk // 2, blk_pair, (zero, zero))
        dqw_ref[...] += dqw
        dkw_ref[...] += dkw
        dp_ref[:, 0:128] = dq_s[...].astype(bf16)
        dp_ref[:, 128:256] = dk_ext[SEG:, :].astype(bf16)
        dp_ref[:, 256:384] = dv_ext[SEG:, :].astype(bf16)

    rev = lambda i: nseg - 1 - i
    vec = pl.BlockSpec((1, LANES), lambda hh, i: (0, 0))
    blk = MAIN_ATT_BLOCK + 2 * g
    cur = pl.BlockSpec((SEG, 384), lambda hh, i: (rev(i), blk + hh))
    prev = pl.BlockSpec((SEG, 384), lambda hh, i: (jnp.maximum(rev(i) - 1, 0), blk + hh))
    col = pl.BlockSpec((SEG, LANES), lambda hh, i: (rev(i), hh))
    big = pltpu.VMEM((2 * SEG, LANES), f32)
    one = pltpu.VMEM((SEG, LANES), f32)
    return _call(body, name, (2, nseg), [cur, prev, col, col, col, col, vec, vec, _ANY], [cur, vec, vec],
                 [jax.ShapeDtypeStruct((S, MAIN_WIDTH), bf16), jax.ShapeDtypeStruct((1, LANES), f32),
                  jax.ShapeDtypeStruct((1, LANES), f32)],
                 scratch=[one, big, big, one, big, big], sem=("arbitrary", "arbitrary"),
                 aliases={8: 0})(p_att, p_att, o, lse, do, dlse, qw, kw, dp_main)


def conv_fwd(p_ssd, conv_w, conv_b, name):
    S = p_ssd.shape[0]
    tm, C = CONV_ROWS, SSD_XBC

    def body(x_ref, xp_ref, w_ref, b_ref, o_ref):
        first = (pl.program_id(0) == 0)
        ext = jnp.concatenate([jnp.where(first, 0.0, xp_ref[:, 0:C]), x_ref[:, 0:C]], axis=0)
        acc = b_ref[...] + w_ref[3:4, :] * ext[8:, :]
        for k in range(1, 4):
            acc = acc + w_ref[3 - k:4 - k, :] * pltpu.roll(ext, k, 0)[8:, :]
        o_ref[...] = jax.nn.silu(acc)

    return _call(body, name, (S // tm,),
                 [pl.BlockSpec((tm, 1536), lambda i: (i, MAIN_SSD_BLOCK)),
                  pl.BlockSpec((8, 1536), lambda i: (jnp.maximum(i * (tm // 8) - 1, 0), MAIN_SSD_BLOCK)),
                  pl.BlockSpec((4, C), lambda i: (0, 0)), pl.BlockSpec((1, C), lambda i: (0, 0))],
                 pl.BlockSpec((tm, C), lambda i: (i, 0)), jax.ShapeDtypeStruct((S, C), f32),
                 sem=("parallel",))(p_ssd, p_ssd, conv_w, conv_b)


def conv_bwd(p_ssd, dact, ddt, conv_w, conv_b, dp_main, name):
    S = p_ssd.shape[0]
    tm, C = CONV_ROWS, SSD_XBC
    nblk = S // tm

    def body(x_ref, xp_ref, xn_ref, da_ref, dan_ref, ddt_ref, w_ref, b_ref, _, dp_ref, dw_ref, db_ref):
        i = pl.program_id(0)
        rows = tm + 8
        ext = jnp.concatenate([jnp.where(i == 0, 0.0, xp_ref[:, 0:C]), x_ref[:, 0:C], xn_ref[:, 0:C]], axis=0)
        shifted = [ext[8:, :]] + [pltpu.roll(ext, k, 0)[8:, :] for k in range(1, 4)]
        pre = b_ref[...] + w_ref[3:4, :] * shifted[0]
        for k in range(1, 4):
            pre = pre + w_ref[3 - k:4 - k, :] * shifted[k]
        sg = jax.nn.sigmoid(pre)
        dact = jnp.concatenate([da_ref[...], jnp.where(i == nblk - 1, 0.0, dan_ref[...])], axis=0)
        dpre = dact * (sg * (1.0 + pre * (1.0 - sg)))
        dx = w_ref[3:4, :] * dpre[0:tm, :]
        for k in range(1, 4):
            dx = dx + w_ref[3 - k:4 - k, :] * pltpu.roll(dpre, rows - k, 0)[0:tm, :]
        dp_ref[:, 0:C] = dx.astype(bf16)
        dp_ref[:, C:C + 128] = ddt_ref[...].astype(bf16)
        dp_ref[:, C + 128:] = jnp.zeros((tm, 128), bf16)
        dcur = dpre[0:tm, :]
        dws = [jnp.sum(dcur * shifted[3 - j][0:tm, :], axis=0, keepdims=True) for j in range(4)]
        dbs = jnp.sum(dcur, axis=0, keepdims=True)

        @pl.when(i == 0)
        def _():
            dw_ref[...] = jnp.zeros_like(dw_ref)
            db_ref[...] = jnp.zeros_like(db_ref)

        for j in range(4):
            dw_ref[j:j + 1, :] += dws[j]
        db_ref[...] += dbs

    t8 = tm // 8
    blk = MAIN_SSD_BLOCK
    return _call(body, name, (nblk,),
                 [pl.BlockSpec((tm, 1536), lambda i: (i, blk)),
                  pl.BlockSpec((8, 1536), lambda i: (jnp.maximum(i * t8 - 1, 0), blk)),
                  pl.BlockSpec((8, 1536), lambda i: (jnp.minimum((i + 1) * t8, S // 8 - 1), blk)),
                  pl.BlockSpec((tm, C), lambda i: (i, 0)),
                  pl.BlockSpec((8, C), lambda i: (jnp.minimum((i + 1) * t8, S // 8 - 1), 0)),
                  pl.BlockSpec((tm, 128), lambda i: (i, 0)),
                  pl.BlockSpec((4, C), lambda i: (0, 0)), pl.BlockSpec((1, C), lambda i: (0, 0)), _ANY],
                 [pl.BlockSpec((tm, 1536), lambda i: (i, blk)), pl.BlockSpec((4, C), lambda i: (0, 0)),
                  pl.BlockSpec((1, C), lambda i: (0, 0))],
                 [jax.ShapeDtypeStruct((S, MAIN_WIDTH), bf16), jax.ShapeDtypeStruct((4, C), f32),
                  jax.ShapeDtypeStruct((1, C), f32)],
                 sem=("arbitrary",), aliases={8: 0})(p_ssd, p_ssd, p_ssd, dact, dact, ddt, conv_w, conv_b, dp_main)


def _ssd_chunk(xbc, dtr, state, dt_bias, a_log, d_full):
    T = SSD_CHUNK
    r_i = lax.broadcasted_iota(jnp.int32, (T, T), 0)
    c_i = lax.broadcasted_iota(jnp.int32, (T, T), 1)
    tril = c_i <= r_i
    tri = tril.astype(bf16)
    lane = lax.broadcasted_iota(jnp.int32, (1, LANES), 1)
    hm = [(lane < 64).astype(f32), (lane >= 64).astype(f32)]
    column = lambda v, h: jnp.broadcast_to(v[:, h:h + 1], (T, LANES))

    def per_head_lanes(v):
        return jnp.concatenate([jnp.where(lane < 64, column(v, 2 * pp), column(v, 2 * pp + 1)) for pp in range(6)],
                               axis=1)

    xs, bm, cm = xbc[:, :768], xbc[:, 768:1024], xbc[:, 1024:1280]
    dt = _softplus(dtr + dt_bias)
    a_dt = dt * (-jnp.exp(a_log))
    a_cs = _xdot_l(tri, a_dt)
    dt_full = per_head_lanes(dt)
    acs_full = per_head_lanes(a_cs)
    last = lax.broadcasted_iota(jnp.int32, (T, SSD_WIDTH), 0) == T - 1
    tot_full = jnp.sum(jnp.where(last, acs_full, 0.0), axis=0, keepdims=True)
    xdt = xs * dt_full
    xw = xdt * jnp.exp(tot_full - acs_full)
    eacs = jnp.exp(acs_full)
    st_parts, off_parts, diag_parts = [], [], []
    for g in range(2):
        bg, cg = bm[:, 128 * g:128 * (g + 1)], cm[:, 128 * g:128 * (g + 1)]
        cols = slice(384 * g, 384 * (g + 1))
        st_parts.append(_bdot(bg, xw[:, cols], TN))
        off_parts.append(_bdot(cg, state[:, cols], NN))
        cb = _bdot(cg, bg, NT)
        for pp in range(3 * g, 3 * g + 3):
            xp = xdt[:, 128 * pp:128 * (pp + 1)]
            acc = jnp.zeros((T, LANES), f32)
            for hh in range(2):
                a_col = column(a_cs, 2 * pp + hh)
                decay = jnp.where(tril, jnp.exp(jnp.minimum(a_col - a_col.T, 0.0)), 0.0)
                acc = acc + _bdot(cb * decay, xp * hm[hh], NN)
            diag_parts.append(acc)
    new_state = state * jnp.exp(tot_full) + jnp.concatenate(st_parts, axis=1)
    y = jnp.concatenate(diag_parts, axis=1) + jnp.concatenate(off_parts, axis=1) * eacs + xs * d_full
    return y, new_state


def ssd_fwd(xact, p_ssd, dt_bias, a_log, d_full, name):
    S = xact.shape[0]
    T = SSD_CHUNK

    def body(x_ref, p_ref, b_ref, a_ref, d_ref, y_ref, s_ref, state):
        @pl.when(pl.program_id(0) == 0)
        def _():
            state[...] = jnp.zeros_like(state)

        st = state[...]
        s_ref[0] = st
        y, new = _ssd_chunk(x_ref[...], p_ref[...], st, b_ref[...], a_ref[...], d_ref[...])
        y_ref[...] = y
        state[...] = new

    vec = lambda n: pl.BlockSpec((1, n), lambda i: (0, 0))
    return _call(body, name, (S // T,),
                 [pl.BlockSpec((T, SSD_XBC), lambda i: (i, 0)), pl.BlockSpec((T, 128), lambda i: (i, MAIN_DT_BLOCK)),
                  vec(128), vec(128), vec(768)],
                 [pl.BlockSpec((T, 768), lambda i: (i, 0)), pl.BlockSpec((1, T, 768), lambda i: (i, 0, 0))],
                 [jax.ShapeDtypeStruct((S, 768), f32), jax.ShapeDtypeStruct((S // T, T, 768), f32)],
                 scratch=[pltpu.VMEM((T, 768), f32)], sem=("arbitrary",))(xact, p_ssd, dt_bias, a_log, d_full)


def ssd_bwd(xact, p_ssd, states, dy, dt_bias, a_log, d_full, name):
    S = xact.shape[0]
    T = SSD_CHUNK
    nc = S // T

    def body(x_ref, p_ref, s_ref, dy_ref, b_ref, a_ref, d_ref, dx_ref, ddt_ref, db_ref, da_ref, dd_ref, dstate):
        i = pl.program_id(0)

        @pl.when(i == 0)
        def _():
            for ref in (dstate, db_ref, da_ref, dd_ref):
                ref[...] = jnp.zeros_like(ref)

        _, vjp = jax.vjp(_ssd_chunk, x_ref[...], p_ref[...], s_ref[0], b_ref[...], a_ref[...], d_ref[...])
        dx, ddt, dst, db, da, dd = vjp((dy_ref[...], dstate[...]))
        dx_ref[...] = dx
        ddt_ref[...] = ddt
        dstate[...] = dst
        db_ref[...] += db
        da_ref[...] += da
        dd_ref[...] += dd

    rev = lambda i: nc - 1 - i
    vec = lambda n: pl.BlockSpec((1, n), lambda i: (0, 0))
    return _call(body, name, (nc,),
                 [pl.BlockSpec((T, SSD_XBC), lambda i: (rev(i), 0)), pl.BlockSpec((T, 128), lambda i: (rev(i), MAIN_DT_BLOCK)),
                  pl.BlockSpec((1, T, 768), lambda i: (rev(i), 0, 0)), pl.BlockSpec((T, 768), lambda i: (rev(i), 0)),
                  vec(128), vec(128), vec(768)],
                 [pl.BlockSpec((T, SSD_XBC), lambda i: (rev(i), 0)), pl.BlockSpec((T, 128), lambda i: (rev(i), 0)),
                  vec(128), vec(128), vec(768)],
                 [jax.ShapeDtypeStruct((S, SSD_XBC), f32), jax.ShapeDtypeStruct((S, 128), f32),
                  jax.ShapeDtypeStruct((1, 128), f32), jax.ShapeDtypeStruct((1, 128), f32),
                  jax.ShapeDtypeStruct((1, 768), f32)],
                 scratch=[pltpu.VMEM((T, 768), f32)],
                 sem=("arbitrary",))(xact, p_ssd, states, dy, dt_bias, a_log, d_full)


def _tail_fn(ys5, pt, o0, o1, o2, l0, l1, l2, yssd, glu_b, nw, pr_glu, pr_a, pr_b, pr_c, x, weights):
    glu_w, pa, pb, pc, wo = weights
    gates = jax.nn.sigmoid(pt[:, :3072])
    za, zb, zc = pt[:, 3072:3584], pt[:, 3584:3840], pt[:, 3840:4608]
    g = jax.nn.gelu(ys5)
    ya = g * jax.nn.sigmoid(_cdot(g, glu_w, NN) + glu_b + pr_glu) * jax.nn.silu(za)
    m = jnp.maximum(jnp.maximum(l0, l1), l2)
    e0, e1, e2 = jnp.exp(l0 - m), jnp.exp(l1 - m), jnp.exp(l2 - m)
    yb = (e0 * o0 + e1 * o1 + e2 * o2) / (e0 + e1 + e2) * jax.nn.silu(zb)
    yc = _rms(yssd * jax.nn.silu(zc), nw)
    merged = (gates[:, :1024] * (_cdot(ya, pa, NN) + pr_a) + gates[:, 1024:2048] * (_cdot(yb, pb, NN) + pr_b)
              + gates[:, 2048:] * (_cdot(yc, pc, NN) + pr_c))
    out = x + _cdot(merged, wo, NN)
    return out, (g, ya, yb, yc, merged)


def _tail_specs(tm):
    row = lambda n: pl.BlockSpec((tm, n), lambda i: (i, 0))
    full = lambda a, b: pl.BlockSpec((a, b), lambda i: (0, 0))
    acts = [row(512), row(4608)] + [row(256)] * 6 + [row(768), row(D_MODEL)]
    consts = [full(1, 512), full(1, 768), full(512, 512), full(512, D_MODEL), full(256, D_MODEL),
              full(768, D_MODEL), full(D_MODEL, D_MODEL)]
    return row, full, acts, consts


def tail_fwd(ys5, pt, os_, ls_, yssd, x, glu_b, nw, weights, name):
    S = x.shape[0]
    tm = 2 * TAIL_ROWS
    row, full, acts, consts = _tail_specs(tm)

    def body(ys5_ref, pt_ref, o0, o1, o2, l0, l1, l2, yssd_ref, x_ref, gb_ref, nw_ref, gw, pa, pb, pc, wo, out_ref):
        z = lambda n: jnp.zeros((tm, n), f32)
        out, _ = _tail_fn(ys5_ref[...], pt_ref[...], o0[...], o1[...], o2[...], l0[...], l1[...], l2[...],
                          yssd_ref[...], gb_ref[...], nw_ref[...], z(512), z(D_MODEL), z(D_MODEL), z(D_MODEL),
                          x_ref[...], (gw[...], pa[...], pb[...], pc[...], wo[...]))
        out_ref[...] = out

    return _call(body, name, (S // tm,), acts + consts, row(D_MODEL), jax.ShapeDtypeStruct((S, D_MODEL), f32),
                 sem=("parallel",))(ys5, pt, *os_, *ls_, yssd, x, glu_b, nw, *weights)


def tail_bwd(ys5, pt, os_, ls_, yssd, dout, glu_b, nw, weights, name):
    S = dout.shape[0]
    tm = TAIL_ROWS
    row, full, acts, consts = _tail_specs(tm)

    def body(ys5_ref, pt_ref, o0, o1, o2, l0, l1, l2, yssd_ref, dout_ref, gb_ref, nw_ref, gw, pa, pb, pc, wo,
             dys5_ref, dpt_ref, do0, do1, do2, dl0, dl1, dl2, dyssd_ref, dgb_ref, dnw_ref,
             g_ref, ya_ref, yb_ref, yc_ref, mg_ref, dglu_ref, dpa_ref, dpb_ref, dpc_ref):
        z = lambda n: jnp.zeros((tm, n), f32)
        w = (gw[...], pa[...], pb[...], pc[...], wo[...])
        fn = lambda *a: _tail_fn(*a, z(D_MODEL), w)
        _, vjp, aux = jax.vjp(fn, ys5_ref[...], pt_ref[...], o0[...], o1[...], o2[...], l0[...], l1[...], l2[...],
                              yssd_ref[...], gb_ref[...], nw_ref[...], z(512), z(D_MODEL), z(D_MODEL), z(D_MODEL),
                              has_aux=True)
        (dys5, dpt, d0, d1, d2, e0, e1, e2, dyssd, dgb, dnw, dglu, dpa, dpb, dpc) = vjp(dout_ref[...])
        dys5_ref[...] = dys5
        dpt_ref[...] = dpt.astype(bf16)
        for ref, val in ((do0, d0), (do1, d1), (do2, d2), (dl0, e0), (dl1, e1), (dl2, e2)):
            ref[...] = val
        dyssd_ref[...] = dyssd
        g, ya, yb, yc, merged = aux
        for ref, val in ((g_ref, g), (ya_ref, ya), (yb_ref, yb), (yc_ref, yc), (mg_ref, merged),
                         (dglu_ref, dglu), (dpa_ref, dpa), (dpb_ref, dpb), (dpc_ref, dpc)):
            ref[...] = val.astype(bf16)

        @pl.when(pl.program_id(0) == 0)
        def _():
            dgb_ref[...] = dgb
            dnw_ref[...] = dnw

        @pl.when(pl.program_id(0) > 0)
        def _():
            dgb_ref[...] += dgb
            dnw_ref[...] += dnw

    sd = lambda n, dt=f32: jax.ShapeDtypeStruct((S, n), dt)
    out_specs = ([row(512), row(4608)] + [row(256)] * 6 + [row(768), full(1, 512), full(1, 768)]
                 + [row(512), row(512), row(256), row(768), row(D_MODEL), row(512)] + [row(D_MODEL)] * 3)
    out_shape = ([sd(512), sd(MAIN_WIDTH, bf16)] + [sd(256)] * 6 + [sd(768), jax.ShapeDtypeStruct((1, 512), f32),
                                                          jax.ShapeDtypeStruct((1, 768), f32)]
                 + [sd(512, bf16), sd(512, bf16), sd(256, bf16), sd(768, bf16), sd(D_MODEL, bf16), sd(512, bf16)]
                 + [sd(D_MODEL, bf16)] * 3)
    return _call(body, name, (S // tm,), acts + consts, out_specs, out_shape,
                 sem=("arbitrary",))(ys5, pt, *os_, *ls_, yssd, dout, glu_b, nw, *weights)


def _in_proj_segments(shards):
    dtype = shards[0].dtype

    def c(a, b):
        parts = []
        for k, sh in enumerate(shards):
            lo, hi = max(a, W_IN_SHARD * k), min(b, W_IN_SHARD * (k + 1))
            if lo < hi:
                parts.append(sh[:, lo - W_IN_SHARD * k:hi - W_IN_SHARD * k])
        return parts[0] if len(parts) == 1 else jnp.concatenate(parts, axis=1)

    atts = []
    for g in range(3):
        parts = []
        for hh in range(2):
            o = 64 * (4 * g + 2 * hh)
            parts += [c(_C_Q + o, _C_Q + o + 128), c(_C_K + o, _C_K + o + 128), c(_C_V + o, _C_V + o + 128)]
        atts.append(jnp.concatenate(parts, axis=1))
    ssd = jnp.concatenate([c(_C_XBC, _C_ZC), jnp.zeros((D_MODEL, 1536 - (_C_ZC - _C_XBC)), dtype)], axis=1)
    tail = jnp.concatenate([c(_C_GATE, _C_END), c(_C_ZA, _C_Q), c(_C_ZB, _C_XBC), c(_C_ZC, _C_GATE)], axis=1)
    return [c(_C_UA, _C_ZA), jnp.concatenate([tail, ssd] + atts, axis=1)]


def _in_proj_grad(ds5, dmain):
    dtail, dssd = dmain[:, :4608], dmain[:, 4608:6144]
    datts = [dmain[:, 6144 + 768 * g:6144 + 768 * (g + 1)] for g in range(3)]
    pick = lambda off: [datts[g][:, 384 * hh + off:384 * hh + off + 128] for g in range(3) for hh in range(2)]
    pieces = ([ds5, dtail[:, 3072:3584]] + pick(0) + pick(128) + pick(256)
              + [dtail[:, 3584:3840], dssd[:, :_C_ZC - _C_XBC], dtail[:, 3840:4608], dtail[:, :3072]])
    shards, start = [[] for _ in range(4)], 0
    for piece in pieces:
        width = piece.shape[1]
        for k in range(4):
            lo, hi = max(start, W_IN_SHARD * k), min(start + width, W_IN_SHARD * (k + 1))
            if lo < hi:
                shards[k].append(piece[:, lo - start:hi - start])
        start += width
    return jnp.stack([jnp.concatenate(s, axis=1) for s in shards])


def _prep_layer(p):
    q = {}
    q["segs"] = [s.astype(bf16) for s in _in_proj_segments(p["w_in"])]
    disc = _s5_discretize(p["s5_a_re"], p["s5_a_im"], p["s5_log_step"], p["s5_b_re"], p["s5_b_im"],
                          p["s5_c_re"], p["s5_c_im"])
    q["s5"] = disc
    q["pw"] = _lam_powers(disc[0], disc[1])
    q["s5_d"] = p["s5_d"].reshape(1, 512)
    q["qw"] = jnp.tile(p["q_norm_w"], 2).reshape(1, LANES)
    q["kw"] = jnp.tile(p["k_norm_w"], 2).reshape(1, LANES)
    q["conv_w"] = p["conv_w"]
    q["conv_b"] = p["conv_b"].reshape(1, SSD_XBC)
    pad = lambda v: jnp.pad(v, (0, LANES - v.shape[0])).reshape(1, LANES)
    q["dt_bias"], q["a_log"] = pad(p["dt_bias"]), pad(p["ssd_a_log"])
    q["d_full"] = jnp.repeat(p["ssd_d"], 64).reshape(1, SSD_WIDTH)
    q["glu_b"] = p["s5_glu_b"].reshape(1, 512)
    q["nw"] = p["ssd_norm_w"].reshape(1, SSD_WIDTH)
    q["norm_w"] = p["norm_w"].reshape(1, D_MODEL)
    q["tailw"] = tuple(p[n].astype(bf16) for n in ("s5_glu_w", "proj_a", "proj_b", "proj_c", "w_out"))
    return q


_DILATIONS = (1, 4, 16)


def layer_fwd(x, q, tag):
    h = rms_fwd(x, q["norm_w"], f"rms_fwd{tag}")
    p_s5, p_main = [mm_nn(h, w, f"inproj{k}{tag}") for k, w in enumerate(q["segs"])]
    _, _, w_re, w_im, c_re, c_im = q["s5"]
    ys5, h_re, h_im = s5_fwd(p_s5, *q["pw"], w_re, w_im, c_re, c_im, q["s5_d"], f"s5_fwd{tag}")
    os_, ls_ = [], []
    for g, d in enumerate(_DILATIONS):
        o, l = att_fwd(p_main, q["qw"], q["kw"], d, g, f"att_fwd{g}{tag}")
        os_.append(o)
        ls_.append(l)
    xact = conv_fwd(p_main, q["conv_w"], q["conv_b"], f"conv_fwd{tag}")
    yssd, states = ssd_fwd(xact, p_main, q["dt_bias"], q["a_log"], q["d_full"], f"ssd_fwd{tag}")
    out = tail_fwd(ys5, p_main, os_, ls_, yssd, x, q["glu_b"], q["nw"], q["tailw"], f"tail_fwd{tag}")
    saved = dict(x=x, h=h, p_s5=p_s5, p_main=p_main, ys5=ys5, h_re=h_re, h_im=h_im,
                 os=os_, ls=ls_, xact=xact, yssd=yssd, states=states)
    return out, saved


def layer_bwd(dout, sv, q, p, tag):
    S = dout.shape[0]
    (dys5, dp_main, do0, do1, do2, dl0, dl1, dl2, dyssd, dglu_b, dnw, g_b, ya_b, yb_b, yc_b, mg_b, dglu_b16,
     dpa_b, dpb_b, dpc_b) = tail_bwd(sv["ys5"], sv["p_main"], sv["os"], sv["ls"], sv["yssd"], dout, q["glu_b"],
                                     q["nw"], q["tailw"], f"tail_bwd{tag}")
    grads = {}
    grads["s5_glu_w"] = mm_tn(g_b, dglu_b16, f"dglu_w{tag}")
    grads["proj_a"] = mm_tn(ya_b, dpa_b, f"dproj_a{tag}")
    grads["proj_b"] = mm_tn(yb_b, dpb_b, f"dproj_b{tag}")
    grads["proj_c"] = mm_tn(yc_b, dpc_b, f"dproj_c{tag}")
    grads["w_out"] = mm_tn(mg_b, dout, f"dw_out{tag}")
    grads["s5_glu_b"] = dglu_b.reshape(512)
    grads["ssd_norm_w"] = dnw.reshape(SSD_WIDTH)

    dxact, ddt, ddt_bias, da_log, dd_full = ssd_bwd(sv["xact"], sv["p_main"], sv["states"], dyssd, q["dt_bias"],
                                                    q["a_log"], q["d_full"], f"ssd_bwd{tag}")
    dp_main, dconv_w, dconv_b = conv_bwd(sv["p_main"], dxact, ddt, q["conv_w"], q["conv_b"], dp_main,
                                         f"conv_bwd{tag}")
    grads["dt_bias"] = ddt_bias[0, :12]
    grads["ssd_a_log"] = da_log[0, :12]
    grads["ssd_d"] = dd_full.reshape(12, 64).sum(axis=1)
    grads["conv_w"] = dconv_w
    grads["conv_b"] = dconv_b.reshape(SSD_XBC)

    dqw, dkw = 0.0, 0.0
    for g, d in enumerate(_DILATIONS):
        dp_main, a, b = att_bwd(sv["p_main"], sv["os"][g], sv["ls"][g], (do0, do1, do2)[g], (dl0, dl1, dl2)[g],
                                q["qw"], q["kw"], d, g, dp_main, f"att_bwd{g}{tag}")
        dqw, dkw = dqw + a, dkw + b
    grads["q_norm_w"] = dqw.reshape(2, 64).sum(axis=0)
    grads["k_norm_w"] = dkw.reshape(2, 64).sum(axis=0)

    _, _, w_re, w_im, c_re, c_im = q["s5"]
    dp_s5, dwre, dwim, dcre, dcim, dlam_re, dlam_im, dd = s5_bwd(
        dys5, sv["p_s5"], sv["h_re"], sv["h_im"], *q["pw"], w_re, w_im, c_re, c_im, q["s5_d"], f"s5_bwd{tag}")
    s5_names = ("s5_a_re", "s5_a_im", "s5_log_step", "s5_b_re", "s5_b_im", "s5_c_re", "s5_c_im")
    _, disc_vjp = jax.vjp(_s5_discretize, *[p[n] for n in s5_names])
    for n, gr in zip(s5_names, disc_vjp((dlam_re, dlam_im, dwre, dwim, dcre, dcim))):
        grads[n] = gr
    grads["s5_d"] = dd.reshape(512)

    dsegs = [dp_s5, dp_main]
    dws = [mm_tn(sv["h"], ds, f"dw_in{k}{tag}") for k, ds in enumerate(dsegs)]
    grads["w_in"] = _in_proj_grad(*dws)
    dh = None
    for k, (ds, w) in enumerate(zip(dsegs, q["segs"])):
        dh = mm_nt(ds, w, f"dh{k}{tag}", acc=dh)
    dx, dnorm_w = rms_bwd(sv["x"], q["norm_w"], dh, dout, f"rms_bwd{tag}")
    grads["norm_w"] = dnorm_w.reshape(D_MODEL)
    return dx, grads


def _exchange(name, scatter=(), gather=(), sibling=(), sibling_both=False):
    scatter, gather, sibling = list(scatter), list(gather), list(sibling)
    chip_xs = scatter + gather
    ns, nc, nb = len(scatter), len(chip_xs), len(sibling)
    n = nc + nb

    def body(*refs):
        x_refs, o_refs, send_sems, recv_sems = refs[:n], refs[n:2 * n], refs[2 * n], refs[2 * n + 1]
        mx, my, mc = lax.axis_index("x"), lax.axis_index("y"), lax.axis_index("c")
        me = 2 * mx + my
        copies = []
        for a in range(nc):
            for t, (px, py) in enumerate(((1 - mx, my), (mx, 1 - my), (1 - mx, 1 - my))):
                src = x_refs[a].at[2 * px + py] if a < ns else x_refs[a]
                copies.append(pltpu.make_async_remote_copy(
                    src_ref=src, dst_ref=o_refs[a].at[me], send_sem=send_sems.at[3 * a + t],
                    recv_sem=recv_sems.at[3 * a + t], device_id=(px, py, mc), device_id_type=pl.DeviceIdType.MESH))
        for b in range(nc, n):
            k = 3 * nc + b - nc
            copies.append(pltpu.make_async_remote_copy(
                src_ref=x_refs[b], dst_ref=o_refs[b].at[mc] if sibling_both else o_refs[b], send_sem=send_sems.at[k],
                recv_sem=recv_sems.at[k], device_id=(mx, my, 1 - mc), device_id_type=pl.DeviceIdType.MESH))
        for cp in copies:
            cp.start()
        for cp in copies:
            cp.wait()

    shapes = ([(4,) + tuple(x.shape[1:]) for x in scatter] + [(4,) + tuple(x.shape) for x in gather]
              + [((2,) if sibling_both else ()) + tuple(x.shape) for x in sibling])
    xs = chip_xs + sibling
    outs = pl.pallas_call(
        body, name=name, in_specs=[_ANY] * n, out_specs=[_ANY] * n,
        out_shape=[jax.ShapeDtypeStruct(s, x.dtype) for s, x in zip(shapes, xs)],
        scratch_shapes=[pltpu.SemaphoreType.DMA((3 * nc + nb,)), pltpu.SemaphoreType.DMA((3 * nc + nb,))],
    )(*xs)
    me, c = 2 * lax.axis_index("x") + lax.axis_index("y"), lax.axis_index("c")
    fixed = []
    for a, (o, x) in enumerate(zip(outs, xs)):
        if a < ns:
            o = lax.dynamic_update_index_in_dim(o, lax.dynamic_index_in_dim(x, me, 0, keepdims=True), me, 0)
        elif a < nc:
            o = lax.dynamic_update_index_in_dim(o, x[None], me, 0)
        elif sibling_both:
            o = lax.dynamic_update_index_in_dim(o, x[None], c, 0)
        fixed.append(o)
    return fixed[:ns], fixed[ns:nc], fixed[nc:]


def _rows_tile(rows, row_bytes, budget=1 << 20):
    return next(t for t in (512, 256, 128, 64, 32, 16, 8) if rows % t == 0 and t * row_bytes <= budget)


def _padded_row_bytes(cols):
    return -(-cols // LANES) * LANES * 4


def _add2(a, b, name, out_dtype=f32):
    R, C = a.shape
    tr = _rows_tile(R, _padded_row_bytes(C))

    def body(a_ref, b_ref, o_ref):
        o_ref[...] = (a_ref[...] + b_ref[...]).astype(out_dtype)

    spec = pl.BlockSpec((tr, C), lambda i: (i, 0))
    return _call(body, name, (R // tr,), [spec, spec], spec, jax.ShapeDtypeStruct((R, C), out_dtype),
                 sem=("parallel",))(a, b)


def _sum4(x, name):
    R = x.shape[1]
    tr = _tile(R, (512, 256, 128))

    def body(x_ref, o_ref):
        p = [x_ref[j].astype(f32) for j in range(4)]
        o_ref[...] = ((p[0] + p[1]) + p[2]) + p[3]

    return _call(body, name, (R // tr,), [pl.BlockSpec((4, tr, LANES), lambda i: (0, i, 0))],
                 pl.BlockSpec((tr, LANES), lambda i: (i, 0)), jax.ShapeDtypeStruct((R, LANES), f32),
                 sem=("parallel",))(x)


def _adamw(g_parts, w, m, v, name):
    stacked = not isinstance(g_parts, (tuple, list))
    k = g_parts.shape[0] if stacked else len(g_parts)
    R, C = w.shape
    tr = _rows_tile(R, _padded_row_bytes(C))
    c1 = 1.0 - ADAM_B1 ** ADAM_STEP
    c2 = 1.0 - ADAM_B2 ** ADAM_STEP

    def body(*refs):
        w_ref, m_ref, v_ref, g_ref, d_ref, nm_ref, nv_ref = refs[-7:]
        if stacked:
            g = refs[0][0].astype(f32)
            for j in range(1, k):
                g = g + refs[0][j].astype(f32)
        else:
            g = refs[0][...]
            for r in refs[1:k]:
                g = g + r[...]
        m = ADAM_B1 * m_ref[...] + (1.0 - ADAM_B1) * g
        v = ADAM_B2 * v_ref[...] + (1.0 - ADAM_B2) * (g * g)
        g_ref[...] = g
        nm_ref[...] = m
        nv_ref[...] = v
        d_ref[...] = -ADAM_LR * ((m / c1) / (jnp.sqrt(v / c2) + ADAM_EPS) + ADAM_WD * w_ref[...])

    spec = pl.BlockSpec((tr, C), lambda i: (i, 0))
    sd = jax.ShapeDtypeStruct((R, C), f32)
    g_specs = [pl.BlockSpec((k, tr, C), lambda i: (0, i, 0))] if stacked else [spec] * k
    g_args = [g_parts] if stacked else list(g_parts)
    return _call(body, name, (R // tr,), g_specs + [spec] * 3, [spec] * 4, [sd] * 4,
                 sem=("parallel",))(*g_args, w, m, v)


def _pack(arrays):
    flat = jnp.concatenate([a.reshape(-1) for a in arrays])
    unit = PACK_ROWS * LANES
    n = -(-flat.shape[0] // unit) * unit
    return jnp.pad(flat, (0, n - flat.shape[0])).reshape(n // LANES, LANES)


def _unpack(buf, shapes):
    flat = buf.reshape(-1)
    out, off = [], 0
    for s in shapes:
        n = 1
        for dim in s:
            n *= dim
        out.append(flat[off:off + n].reshape(s))
        off += n
    return out


def _to_shards(full, axis):
    s = full.shape
    t = full.reshape(s[:axis] + (4, s[axis] // 4) + s[axis + 1:])
    return jnp.moveaxis(t, axis, 0)


def _from_shards(sh, axis):
    t = jnp.moveaxis(sh, 0, axis)
    s = t.shape
    return t.reshape(s[:axis] + (s[axis] * s[axis + 1],) + s[axis + 2:])


def kernel(x, norm_w, w_in, s5_a_re, s5_a_im, s5_log_step, s5_b_re, s5_b_im, s5_c_re, s5_c_im, s5_d, s5_glu_w, s5_glu_b, q_norm_w, k_norm_w, conv_w, conv_b, dt_bias, ssd_a_log, ssd_d, ssd_norm_w, proj_a, proj_b, proj_c, w_out, loss_target, m_norm_w, m_w_in, m_s5_a_re, m_s5_a_im, m_s5_log_step, m_s5_b_re, m_s5_b_im, m_s5_c_re, m_s5_c_im, m_s5_d, m_s5_glu_w, m_s5_glu_b, m_q_norm_w, m_k_norm_w, m_conv_w, m_conv_b, m_dt_bias, m_ssd_a_log, m_ssd_d, m_ssd_norm_w, m_proj_a, m_proj_b, m_proj_c, m_w_out, v_norm_w, v_w_in, v_s5_a_re, v_s5_a_im, v_s5_log_step, v_s5_b_re, v_s5_b_im, v_s5_c_re, v_s5_c_im, v_s5_d, v_s5_glu_w, v_s5_glu_b, v_q_norm_w, v_k_norm_w, v_conv_w, v_conv_b, v_dt_bias, v_ssd_a_log, v_ssd_d, v_ssd_norm_w, v_proj_a, v_proj_b, v_proj_c, v_w_out):
    given = dict(locals())
    W = {n: given[n] for n in _WEIGHTS}
    M = {n: given["m_" + n] for n in _WEIGHTS}
    V = {n: given["v_" + n] for n in _WEIGHTS}
    n_layers = norm_w.shape[0]
    assert n_layers == 2
    c = lax.axis_index("c")

    mine_of = lambda t: lax.dynamic_index_in_dim(t, c, 0, keepdims=False)
    as_payload = lambda n: lax.bitcast_convert_type(W[n], bf16) if n == "conv_w" else W[n].astype(bf16)
    payload_shapes = [W[n].shape + ((2,) if n == "conv_w" else ()) for n, _ in _SHARDED]
    _, (gathered, w_in_mine_layer), _ = _exchange(
        "gather_weights", gather=[_pack([as_payload(n) for n, _ in _SHARDED]), mine_of(w_in).astype(bf16)])
    _, _, (w_in_layers,) = _exchange("share_w_in", sibling=[w_in_mine_layer], sibling_both=True)
    full = dict(W)
    pieces = [_unpack(gathered[j], payload_shapes) for j in range(4)]
    for k, (n, axis) in enumerate(_SHARDED):
        sh = jnp.stack([pieces[j][k] for j in range(4)])
        full[n] = _from_shards(lax.bitcast_convert_type(sh, f32) if n == "conv_w" else sh, axis)

    xs = x[0]
    qs, saves = [], []
    act = xs
    for l in range(n_layers):
        p = {n: full[n][l] for n in _WEIGHTS if n != "w_in"}
        p["w_in"] = [w_in_layers[l, k] for k in range(4)]
        q = _prep_layer(p)
        act, sv = layer_fwd(act, q, f"_l{l}")
        qs.append((q, p))
        saves.append(sv)
    dact, lsum = loss_and_grad(act, loss_target[0], "loss")
    loss = lax.psum(lsum[0, 0], ("x", "y", "c"))
    layer_grads = [None] * n_layers
    for l in reversed(range(n_layers)):
        q, p = qs[l]
        dact, layer_grads[l] = layer_bwd(dact, saves[l], q, p, f"_l{l}")
    grad_x = dact[None]
    G = {n: jnp.stack([layer_grads[l][n] for l in range(n_layers)]) for n in _WEIGHTS if n != "w_in"}

    repl_shapes = [W[n].shape for n in _REPL]
    small = _pack([G[n] for n in _REPL])
    quarter = small.shape[0] // 4
    big = [_to_shards(G[n], axis).reshape(4, -1) for n, axis in _SHARDED]
    big = jnp.concatenate(big, axis=1)
    unit = PACK_ROWS * LANES
    nbig = -(-big.shape[1] // unit) * unit
    big = jnp.pad(big, ((0, 0), (0, nbig - big.shape[1]))).reshape(4, nbig // LANES, LANES)
    gpack = jnp.concatenate([big, small.reshape(4, quarter, LANES)], axis=1)
    rbig = nbig // LANES
    g0, g1 = layer_grads[0]["w_in"], layer_grads[1]["w_in"]

    (landed_pack,), _, (from_sibling,) = _exchange(
        "swap_w_in_grads_and_scatter_grads", scatter=[gpack.astype(bf16)], sibling=[jnp.where(c == 0, g1, g0)])
    flat = lambda t: t.reshape(4 * D_MODEL, W_IN_SHARD)
    shards = _add2(flat(jnp.where(c == 0, g0, g1)), flat(from_sibling), "sum_cores_w_in", out_dtype=bf16)
    mine = _sum4(landed_pack, "sum_chips")

    (landed,), _, (other,) = _exchange(
        "scatter_w_in_grads_and_swap_cores", scatter=[shards.reshape(4, D_MODEL, W_IN_SHARD)], sibling=[mine])
    w_in_mine = _adamw(landed, mine_of(w_in), mine_of(m_w_in), mine_of(v_w_in), "adamw_w_in")
    gq = _add2(mine[rbig:], other[rbig:], "sum_cores_small")

    _, (gsmall,), w_in_out = _exchange(
        "share_w_in_updates_and_gather_small", gather=[gq], sibling=w_in_mine, sibling_both=True)
    gsmall = gsmall.reshape(4 * quarter, LANES)

    wp, mp, vp = (_pack([T[n] for n, _ in _SHARDED]) for T in (W, M, V))
    outs_big = _adamw((mine[:rbig], other[:rbig]), wp, mp, vp, "adamw_sharded")
    big_out = [_unpack(o, [W[n].shape for n, _ in _SHARDED]) for o in outs_big]
    ws, ms, vs = (_pack([T[n] for n in _REPL]) for T in (W, M, V))
    outs_small = _adamw((gsmall,), ws, ms, vs, "adamw_replicated")
    small_out = [_unpack(o, repl_shapes) for o in outs_small]

    res = [dict(), dict(), dict(), dict()]
    for kind in range(4):
        res[kind]["w_in"] = w_in_out[kind]
        for k, (n, _) in enumerate(_SHARDED):
            res[kind][n] = big_out[kind][k]
        for k, n in enumerate(_REPL):
            res[kind][n] = small_out[kind][k]
    return (loss, grad_x, *[res[0][n] for n in _WEIGHTS], *[res[1][n] for n in _WEIGHTS],
            *[res[2][n] for n in _WEIGHTS], *[res[3][n] for n in _WEIGHTS])
```

```python
import functools

import jax
import jax.numpy as jnp
from jax import lax
from jax.experimental import pallas as pl
from jax.experimental.pallas import tpu as pltpu

f32 = jnp.float32
bf16 = jnp.bfloat16

D_MODEL = 1024
RMS_EPS = 1e-6
V7X_VMEM_LIMIT = 60 * 1024 * 1024
LANES = 128
NN, NT, TN = ((1,), (0,)), ((1,), (1,)), ((0,), (0,))

S5_STATES = 2048
S5_ROWS = 256
ATT_SEG = 2048
ATT_BLOCK = 128
SSD_CHUNK = 128
SSD_WIDTH = 768
SSD_XBC = 1280
CONV_ROWS = 512
TAIL_ROWS = 128

ADAM_LR, ADAM_B1, ADAM_B2, ADAM_EPS, ADAM_WD, ADAM_STEP = 0.001, 0.9, 0.999, 1e-08, 0.01, 10

_C_UA, _C_ZA, _C_Q, _C_K, _C_V, _C_ZB, _C_XBC, _C_DT, _C_ZC, _C_GATE, _C_END = (
    0, 512, 1024, 1792, 2560, 3328, 3584, 4864, 4876, 5644, 8716)

_SHARDED = (("s5_glu_w", 1), ("conv_w", 2), ("proj_a", 2), ("proj_b", 2), ("proj_c", 2), ("w_out", 1))
W_IN_SHARD = 2179
_REPL = ("norm_w", "s5_a_re", "s5_a_im", "s5_log_step", "s5_b_re", "s5_b_im", "s5_c_re", "s5_c_im", "s5_d",
         "s5_glu_b", "q_norm_w", "k_norm_w", "conv_b", "dt_bias", "ssd_a_log", "ssd_d", "ssd_norm_w")
_WEIGHTS = ("norm_w", "w_in", "s5_a_re", "s5_a_im", "s5_log_step", "s5_b_re", "s5_b_im", "s5_c_re", "s5_c_im",
            "s5_d", "s5_glu_w", "s5_glu_b", "q_norm_w", "k_norm_w", "conv_w", "conv_b", "dt_bias", "ssd_a_log",
            "ssd_d", "ssd_norm_w", "proj_a", "proj_b", "proj_c", "w_out")
PACK_ROWS = 512


def _dot(a, b, dims):
    return lax.dot_general(a.astype(bf16), b.astype(bf16), (dims, ((), ())), preferred_element_type=f32)


_ANY = pl.BlockSpec(memory_space=pl.ANY)

MAIN_WIDTH = 8448
MAIN_SSD_BLOCK = 3
MAIN_DT_BLOCK = 46
MAIN_ATT_BLOCK = 16


def _call(body, name, grid, in_specs, out_specs, out_shape, scratch=(), sem=None, aliases=None):
    return pl.pallas_call(
        body, name=name, grid=grid, in_specs=in_specs, out_specs=out_specs, out_shape=out_shape,
        scratch_shapes=list(scratch), input_output_aliases=aliases or {},
        compiler_params=pltpu.CompilerParams(dimension_semantics=sem, vmem_limit_bytes=V7X_VMEM_LIMIT))


def _tile(n, options=(1024, 768, 512, 384, 256, 128)):
    return next(t for t in options if n % t == 0)


@functools.partial(jax.custom_vjp, nondiff_argnums=(2,))
def _bdot(a, b, dims):
    return _dot(a, b, dims)


def _bdot_fwd(a, b, dims):
    return _dot(a, b, dims), (a, b)


def _bdot_bwd(dims, res, g):
    a, b = res
    if dims == NN:
        da, db = _dot(g, b, NT), _dot(a, g, TN)
    elif dims == NT:
        da, db = _dot(g, b, NN), _dot(g, a, TN)
    else:
        da, db = _dot(b, g, NT), _dot(a, g, NN)
    return da.astype(a.dtype), db.astype(b.dtype)


_bdot.defvjp(_bdot_fwd, _bdot_bwd)


@functools.partial(jax.custom_vjp, nondiff_argnums=(2,))
def _cdot(a, w, dims):
    return _dot(a, w, dims)


def _cdot_fwd(a, w, dims):
    return _dot(a, w, dims), w


def _cdot_bwd(dims, w, g):
    da = _dot(g, w, NT) if dims == NN else _dot(g, w, NN)
    return da, jnp.zeros_like(w)


_cdot.defvjp(_cdot_fwd, _cdot_bwd)


def _split3(x):
    hi = x.astype(bf16)
    r = x - hi.astype(f32)
    mid = r.astype(bf16)
    lo = (r - mid.astype(f32)).astype(bf16)
    return hi, mid, lo


@jax.custom_vjp
def _xdot_l(m, x):
    return sum(_dot(m, p, NN) for p in _split3(x))


def _xdot_l_fwd(m, x):
    return _xdot_l(m, x), m


def _xdot_l_bwd(m, g):
    return jnp.zeros_like(m), sum(_dot(m, p, TN) for p in _split3(g))


_xdot_l.defvjp(_xdot_l_fwd, _xdot_l_bwd)


@jax.custom_vjp
def _softplus(x):
    e = jnp.exp(-jnp.abs(x))
    u = 1.0 + e
    log1p = jnp.where(u == 1.0, e, jnp.log(u) * (e / jnp.where(u == 1.0, 1.0, u - 1.0)))
    return jnp.maximum(x, 0.0) + log1p


def _softplus_fwd(x):
    return _softplus(x), x


def _softplus_bwd(x, g):
    return (g * jax.nn.sigmoid(x),)


_softplus.defvjp(_softplus_fwd, _softplus_bwd)


def _rms(x, w):
    return x * lax.rsqrt(jnp.mean(x * x, axis=-1, keepdims=True) + RMS_EPS) * w


def mm_nn(a, b, name, tm=2048):
    M, K = a.shape
    N = b.shape[1]
    tn = _tile(N)

    def body(a_ref, b_ref, o_ref):
        o_ref[...] = _dot(a_ref[...], b_ref[...], NN)

    return _call(body, name, (M // tm, N // tn),
                 [pl.BlockSpec((tm, K), lambda i, j: (i, 0)), pl.BlockSpec((K, tn), lambda i, j: (0, j))],
                 pl.BlockSpec((tm, tn), lambda i, j: (i, j)), jax.ShapeDtypeStruct((M, N), f32),
                 sem=("parallel", "parallel"))(a, b)


def mm_nt(a, b, name, acc=None, tm=1024):
    M, K = a.shape
    N = b.shape[0]
    tk = _tile(K, (2816, 1024, 768, 512, 256, 128))
    has_acc = acc is not None

    def body(*refs):
        a_ref, b_ref = refs[0], refs[1]
        o_ref = refs[-1]
        k = pl.program_id(1)
        p = _dot(a_ref[...], b_ref[...], NT)

        @pl.when(k == 0)
        def _():
            o_ref[...] = p + refs[2][...] if has_acc else p

        @pl.when(k > 0)
        def _():
            o_ref[...] += p

    specs = [pl.BlockSpec((tm, tk), lambda i, k: (i, k)), pl.BlockSpec((N, tk), lambda i, k: (0, k))]
    args = [a, b]
    if has_acc:
        specs.append(pl.BlockSpec((tm, N), lambda i, k: (i, 0)))
        args.append(acc)
    return _call(body, name, (M // tm, K // tk), specs, pl.BlockSpec((tm, N), lambda i, k: (i, 0)),
                 jax.ShapeDtypeStruct((M, N), f32), sem=("parallel", "arbitrary"))(*args)


def mm_tn(a, b, name, tk=2048):
    K, M = a.shape
    N = b.shape[1]
    tn = _tile(N)

    def body(a_ref, b_ref, o_ref):
        k = pl.program_id(1)
        p = _dot(a_ref[...], b_ref[...], TN)

        @pl.when(k == 0)
        def _():
            o_ref[...] = p

        @pl.when(k > 0)
        def _():
            o_ref[...] += p

    return _call(body, name, (N // tn, K // tk),
                 [pl.BlockSpec((tk, M), lambda j, k: (k, 0)), pl.BlockSpec((tk, tn), lambda j, k: (k, j))],
                 pl.BlockSpec((M, tn), lambda j, k: (0, j)), jax.ShapeDtypeStruct((M, N), f32),
                 sem=("parallel", "arbitrary"))(a, b)


def rms_fwd(x, w, name, tm=512):
    S = x.shape[0]

    def body(x_ref, w_ref, o_ref):
        o_ref[...] = _rms(x_ref[...], w_ref[...]).astype(bf16)

    return _call(body, name, (S // tm,),
                 [pl.BlockSpec((tm, D_MODEL), lambda i: (i, 0)), pl.BlockSpec((1, D_MODEL), lambda i: (0, 0))],
                 pl.BlockSpec((tm, D_MODEL), lambda i: (i, 0)), jax.ShapeDtypeStruct((S, D_MODEL), bf16),
                 sem=("parallel",))(x, w)


def rms_bwd(x, w, dh, dres, name, tm=512):
    S = x.shape[0]

    def body(x_ref, w_ref, dh_ref, dr_ref, dx_ref, dw_ref):
        _, vjp = jax.vjp(_rms, x_ref[...], w_ref[...])
        dx, dw = vjp(dh_ref[...])
        dx_ref[...] = dx + dr_ref[...]

        @pl.when(pl.program_id(0) == 0)
        def _():
            dw_ref[...] = dw

        @pl.when(pl.program_id(0) > 0)
        def _():
            dw_ref[...] += dw

    row = pl.BlockSpec((tm, D_MODEL), lambda i: (i, 0))
    vec = pl.BlockSpec((1, D_MODEL), lambda i: (0, 0))
    return _call(body, name, (S // tm,), [row, vec, row, row], [row, vec],
                 [jax.ShapeDtypeStruct((S, D_MODEL), f32), jax.ShapeDtypeStruct((1, D_MODEL), f32)],
                 sem=("arbitrary",))(x, w, dh, dres)


def loss_and_grad(y, target, name, tm=512):
    S = y.shape[0]

    def body(y_ref, t_ref, dy_ref, l_ref):
        diff = y_ref[...] - t_ref[...]
        dy_ref[...] = diff * (1.0 / D_MODEL)
        part = jnp.full((8, LANES), 0.5 / D_MODEL * jnp.sum(diff * diff), f32)

        @pl.when(pl.program_id(0) == 0)
        def _():
            l_ref[...] = part

        @pl.when(pl.program_id(0) > 0)
        def _():
            l_ref[...] += part

    row = pl.BlockSpec((tm, D_MODEL), lambda i: (i, 0))
    return _call(body, name, (S // tm,), [row, row], [row, pl.BlockSpec((8, LANES), lambda i: (0, 0))],
                 [jax.ShapeDtypeStruct((S, D_MODEL), f32), jax.ShapeDtypeStruct((8, LANES), f32)],
                 sem=("arbitrary",))(y, target)


def _s5_discretize(a_re, a_im, log_step, b_re, b_im, c_re, c_im):
    step = jnp.exp(log_step)[:, None]
    mag = jnp.exp(a_re * step)
    ang = a_im * step
    lam_re, lam_im = mag * jnp.cos(ang), mag * jnp.sin(ang)
    num_re, num_im = lam_re - 1.0, lam_im
    den = a_re * a_re + a_im * a_im
    f_re = (num_re * a_re + num_im * a_im) / den
    f_im = (num_im * a_re - num_re * a_im) / den
    bb_re = f_re[..., None] * b_re - f_im[..., None] * b_im
    bb_im = f_re[..., None] * b_im + f_im[..., None] * b_re
    eye = jnp.eye(8, dtype=f32)

    def block_in(bb):
        t = bb.transpose(0, 2, 1).reshape(4, 8, 16, 1, 64)
        return (t * eye[None, :, None, :, None]).reshape(4, 128, 512)

    def block_out(c):
        t = c.transpose(0, 2, 1).reshape(4, 8, 64, 1, 16)
        return (t * eye[None, :, None, :, None]).reshape(4, 512, 128)

    return (lam_re.reshape(1, S5_STATES), lam_im.reshape(1, S5_STATES), block_in(bb_re), block_in(bb_im),
            block_out(c_re), block_out(c_im))


def _lam_powers(lam_re, lam_im):
    rows_re, rows_im = [lam_re], [lam_im]
    for _ in range(7):
        pr, pi = rows_re[-1], rows_im[-1]
        rows_re.append(pr * lam_re - pi * lam_im)
        rows_im.append(pr * lam_im + pi * lam_re)
    return jnp.concatenate(rows_re, 0), jnp.concatenate(rows_im, 0)


def s5_fwd(u, pw_re, pw_im, w_re, w_im, c_re, c_im, dvec, name):
    S = u.shape[0]
    R, NS = S5_ROWS, S5_STATES
    nb = R // 8

    def body(u_ref, pwr_ref, pwi_ref, wre_ref, wim_ref, cre_ref, cim_ref, d_ref, y_ref, hr_ref, hi_ref,
             car_re, car_im, cin_re, cin_im, up, yp):
        @pl.when(pl.program_id(0) == 0)
        def _():
            car_re[...] = jnp.zeros_like(car_re)
            car_im[...] = jnp.zeros_like(car_im)

        slab = lambda r: pl.ds(r * nb, nb)
        for r in range(8):
            up[slab(r), :] = u_ref[:, r, :]
        u = up[...]
        for j in range(4):
            uj = u[:, 128 * j:128 * (j + 1)]
            hr_ref[:, 512 * j:512 * (j + 1)] = _dot(uj, wre_ref[j], NN)
            hi_ref[:, 512 * j:512 * (j + 1)] = _dot(uj, wim_ref[j], NN)
        lr, li = pwr_ref[0:1, :], pwi_ref[0:1, :]
        for r in range(1, 8):
            pr, pi = hr_ref[slab(r - 1), :], hi_ref[slab(r - 1), :]
            hr_ref[slab(r), :] = lr * pr - li * pi + hr_ref[slab(r), :]
            hi_ref[slab(r), :] = lr * pi + li * pr + hi_ref[slab(r), :]
        l8r, l8i = pwr_ref[7:8, :], pwi_ref[7:8, :]

        def across(c, carry):
            gr, gi = carry
            cin_re[pl.ds(c, 1), :] = gr
            cin_im[pl.ds(c, 1), :] = gi
            er, ei = hr_ref[pl.ds(7 * nb + c, 1), :], hi_ref[pl.ds(7 * nb + c, 1), :]
            return l8r * gr - l8i * gi + er, l8r * gi + l8i * gr + ei

        gr, gi = lax.fori_loop(0, nb, across, (car_re[...], car_im[...]))
        car_re[...] = gr
        car_im[...] = gi
        cr, ci = cin_re[...], cin_im[...]
        for r in range(8):
            pr, pi = pwr_ref[r:r + 1, :], pwi_ref[r:r + 1, :]
            hr_ref[slab(r), :] = hr_ref[slab(r), :] + pr * cr - pi * ci
            hi_ref[slab(r), :] = hi_ref[slab(r), :] + pr * ci + pi * cr
        for j in range(4):
            sl = slice(512 * j, 512 * (j + 1))
            cs = slice(128 * j, 128 * (j + 1))
            yp[:, cs] = (_dot(hr_ref[:, sl], cre_ref[j], NN) - _dot(hi_ref[:, sl], cim_ref[j], NN)
                         + d_ref[:, cs] * u[:, cs])
        for r in range(8):
            y_ref[:, r, :] = yp[slab(r), :]

    full = lambda shape: pl.BlockSpec(shape, lambda i: (0,) * len(shape))
    hspec = pl.BlockSpec((R, NS), lambda i: (i, 0))
    uspec = pl.BlockSpec((nb, 8, 512), lambda i: (i, 0, 0))
    y, h_re, h_im = _call(
        body, name, (S // R,),
        [uspec, full((8, NS)), full((8, NS)), full((4, 128, 512)),
         full((4, 128, 512)), full((4, 512, 128)), full((4, 512, 128)), full((1, 512))],
        [uspec, hspec, hspec],
        [jax.ShapeDtypeStruct((S // 8, 8, 512), f32), jax.ShapeDtypeStruct((S, NS), f32),
         jax.ShapeDtypeStruct((S, NS), f32)],
        scratch=[pltpu.VMEM((1, NS), f32), pltpu.VMEM((1, NS), f32), pltpu.VMEM((nb, NS), f32),
                 pltpu.VMEM((nb, NS), f32), pltpu.VMEM((R, 512), f32), pltpu.VMEM((R, 512), f32)],
        sem=("arbitrary",))(u.reshape(S // 8, 8, 512), pw_re, pw_im, w_re.astype(bf16), w_im.astype(bf16),
                            c_re.astype(bf16), c_im.astype(bf16), dvec)
    return y.reshape(S, 512), h_re, h_im


def s5_bwd(dy, u, h_re, h_im, pw_re, pw_im, w_re, w_im, c_re, c_im, dvec, name):
    S = u.shape[0]
    R, NS = S5_ROWS, S5_STATES
    nb = R // 8
    nchunk = S // R

    def body(dy_ref, u_ref, hr_ref, hi_ref, hpr_ref, hpi_ref, pwr_ref, pwi_ref, wre_ref, wim_ref, cre_ref, cim_ref,
             d_ref, du_ref, dwre_ref, dwim_ref, dcre_ref, dcim_ref, dlr_ref, dli_ref, dd_ref,
             ar, ai, car_re, car_im, cin_re, cin_im, up, dyp, dup):
        i = pl.program_id(0)

        @pl.when(i == 0)
        def _():
            for ref in (car_re, car_im, dwre_ref, dwim_ref, dcre_ref, dcim_ref, dlr_ref, dli_ref, dd_ref):
                ref[...] = jnp.zeros_like(ref)

        slab = lambda r: pl.ds(r * nb, nb)
        for r in range(8):
            up[slab(r), :] = u_ref[:, r, :]
            dyp[slab(r), :] = dy_ref[:, r, :]
        dy = dyp[...]
        u = up[...]
        for j in range(4):
            dyj = dy[:, 128 * j:128 * (j + 1)]
            ar[:, 512 * j:512 * (j + 1)] = _dot(dyj, cre_ref[j], NT)
            ai[:, 512 * j:512 * (j + 1)] = -_dot(dyj, cim_ref[j], NT)
        lr, li = pwr_ref[0:1, :], pwi_ref[0:1, :]
        for r in range(6, -1, -1):
            nr, ni = ar[slab(r + 1), :], ai[slab(r + 1), :]
            ar[slab(r), :] = lr * nr + li * ni + ar[slab(r), :]
            ai[slab(r), :] = lr * ni - li * nr + ai[slab(r), :]
        l8r, l8i = pwr_ref[7:8, :], pwi_ref[7:8, :]

        def across(k, carry):
            c = nb - 1 - k
            gr, gi = carry
            cin_re[pl.ds(c, 1), :] = gr
            cin_im[pl.ds(c, 1), :] = gi
            er, ei = ar[pl.ds(c, 1), :], ai[pl.ds(c, 1), :]
            return l8r * gr + l8i * gi + er, l8r * gi - l8i * gr + ei

        gr, gi = lax.fori_loop(0, nb, across, (car_re[...], car_im[...]))
        car_re[...] = gr
        car_im[...] = gi
        cr, ci = cin_re[...], cin_im[...]
        for r in range(8):
            pr, pi = pwr_ref[7 - r:8 - r, :], pwi_ref[7 - r:8 - r, :]
            ar[slab(r), :] = ar[slab(r), :] + pr * cr + pi * ci
            ai[slab(r), :] = ai[slab(r), :] + pr * ci - pi * cr

        acc_r = jnp.zeros((1, NS), f32)
        acc_i = jnp.zeros((1, NS), f32)
        has_prev = (i < nchunk - 1).astype(f32)
        top = lax.broadcasted_iota(jnp.int32, (nb, NS), 0) == 0
        for r in range(8):
            if r == 0:
                xr = jnp.where(top, hpr_ref[7:8, :] * has_prev, pltpu.roll(hr_ref[slab(7), :], 1, 0))
                xi = jnp.where(top, hpi_ref[7:8, :] * has_prev, pltpu.roll(hi_ref[slab(7), :], 1, 0))
            else:
                xr, xi = hr_ref[slab(r - 1), :], hi_ref[slab(r - 1), :]
            br, bi = ar[slab(r), :], ai[slab(r), :]
            acc_r += jnp.sum(br * xr + bi * xi, axis=0, keepdims=True)
            acc_i += jnp.sum(bi * xr - br * xi, axis=0, keepdims=True)
        dlr_ref[...] += acc_r
        dli_ref[...] += acc_i
        dd_ref[...] += jnp.sum(dy * u, axis=0, keepdims=True)

        for j in range(4):
            sl = slice(512 * j, 512 * (j + 1))
            cs = slice(128 * j, 128 * (j + 1))
            arj, aij = ar[:, sl], ai[:, sl]
            uj, dyj = u[:, cs], dy[:, cs]
            dup[:, cs] = _dot(arj, wre_ref[j], NT) + _dot(aij, wim_ref[j], NT) + d_ref[:, cs] * dyj
            dwre_ref[j] += _dot(uj, arj, TN)
            dwim_ref[j] += _dot(uj, aij, TN)
            dcre_ref[j] += _dot(hr_ref[:, sl], dyj, TN)
            dcim_ref[j] -= _dot(hi_ref[:, sl], dyj, TN)
        for r in range(8):
            du_ref[:, r, :] = dup[slab(r), :]

    rev = lambda i: nchunk - 1 - i
    full = lambda shape: pl.BlockSpec(shape, lambda i: (0,) * len(shape))
    row = pl.BlockSpec((nb, 8, 512), lambda i: (rev(i), 0, 0))
    hspec = pl.BlockSpec((R, NS), lambda i: (rev(i), 0))
    hprev = pl.BlockSpec((8, NS), lambda i: (jnp.maximum(rev(i) * nb - 1, 0), 0))
    outs = _call(
        body, name, (nchunk,),
        [row, row, hspec, hspec, hprev, hprev, full((8, NS)), full((8, NS)), full((4, 128, 512)), full((4, 128, 512)),
         full((4, 512, 128)), full((4, 512, 128)), full((1, 512))],
        [row, full((4, 128, 512)), full((4, 128, 512)), full((4, 512, 128)), full((4, 512, 128)),
         full((1, NS)), full((1, NS)), full((1, 512))],
        [jax.ShapeDtypeStruct((S // 8, 8, 512), f32), jax.ShapeDtypeStruct((4, 128, 512), f32),
         jax.ShapeDtypeStruct((4, 128, 512), f32), jax.ShapeDtypeStruct((4, 512, 128), f32),
         jax.ShapeDtypeStruct((4, 512, 128), f32), jax.ShapeDtypeStruct((1, NS), f32),
         jax.ShapeDtypeStruct((1, NS), f32), jax.ShapeDtypeStruct((1, 512), f32)],
        scratch=[pltpu.VMEM((R, NS), f32), pltpu.VMEM((R, NS), f32), pltpu.VMEM((1, NS), f32),
                 pltpu.VMEM((1, NS), f32), pltpu.VMEM((nb, NS), f32), pltpu.VMEM((nb, NS), f32),
                 pltpu.VMEM((R, 512), f32), pltpu.VMEM((R, 512), f32), pltpu.VMEM((R, 512), f32)],
        sem=("arbitrary",))(dy.reshape(S // 8, 8, 512), u.reshape(S // 8, 8, 512), h_re, h_im, h_re, h_im, pw_re,
                            pw_im, w_re.astype(bf16), w_im.astype(bf16), c_re.astype(bf16), c_im.astype(bf16), dvec)
    return (outs[0].reshape(S, 512),) + tuple(outs[1:])


def _rows(start, n, d):
    return pl.ds(pl.multiple_of(start, ATT_BLOCK), n) if d == 1 else pl.ds(start, n, stride=d)


def _head_masks():
    lane = lax.broadcasted_iota(jnp.int32, (1, LANES), 1)
    return [(lane < 64).astype(f32), (lane >= 64).astype(f32)]


def _head_norm(x, w, hm):
    x2 = x * x
    r = [lax.rsqrt(jnp.sum(x2 * hm[h], axis=-1, keepdims=True) * (1.0 / 64) + RMS_EPS) for h in range(2)]
    sc = hm[0] * r[0] + hm[1] * r[1]
    return x * sc * w, sc, r


def _head_norm_bwd(x, w, sc, r, dxn, hm):
    dw = jnp.sum(dxn * x * sc, axis=0, keepdims=True)
    t = dxn * w
    tx = t * x
    corr = sum(hm[h] * (r[h] * r[h] * r[h]) * jnp.sum(tx * hm[h], axis=-1, keepdims=True) for h in range(2))
    return t * sc - x * corr * (1.0 / 64), dw


def _att_mask(has_prev):
    qi = lax.broadcasted_iota(jnp.int32, (ATT_BLOCK, 2 * ATT_BLOCK), 0) + ATT_BLOCK
    kj = lax.broadcasted_iota(jnp.int32, (ATT_BLOCK, 2 * ATT_BLOCK), 1)
    return (qi - kj >= 0) & (qi - kj <= ATT_BLOCK) & (has_prev | (kj >= ATT_BLOCK))


def _att_block_bwd(q, k, v, o, lse, do, dlse, qw, kw, has_prev):
    hm = _head_masks()
    mask = _att_mask(has_prev)
    qn, qsc, qr = _head_norm(q, qw, hm)
    kn, ksc, kr = _head_norm(k, kw, hm)
    dqn = jnp.zeros((ATT_BLOCK, LANES), f32)
    dkn = jnp.zeros((2 * ATT_BLOCK, LANES), f32)
    dv = jnp.zeros((2 * ATT_BLOCK, LANES), f32)
    for h in range(2):
        qh, do_h = qn * hm[h], do * hm[h]
        s = _dot(qh, kn, NT) * 0.125
        p = jnp.exp(jnp.where(mask, s - lse[:, 64 * h:64 * h + 1], -jnp.inf))
        dp = _dot(do_h, v, NT)
        delta = jnp.sum(do_h * o, axis=-1, keepdims=True)
        dl = jnp.sum(dlse * hm[h], axis=-1, keepdims=True)
        ds = p * (dp - delta + dl) * 0.125
        dqn = dqn + hm[h] * _dot(ds, kn, NN)
        dkn = dkn + _dot(ds, qh, TN)
        dv = dv + _dot(p, do_h, TN)
    dq, dqw = _head_norm_bwd(q, qw, qsc, qr, dqn, hm)
    dk, dkw = _head_norm_bwd(k, kw, ksc, kr, dkn, hm)
    return dq, dk, dv, dqw, dkw


def _att_block(q, k, v, qw, kw, has_prev):
    hm = _head_masks()
    qn, kn = _head_norm(q, qw, hm)[0], _head_norm(k, kw, hm)[0]
    mask = _att_mask(has_prev)
    o = jnp.zeros((ATT_BLOCK, LANES), f32)
    lse = jnp.zeros((ATT_BLOCK, LANES), f32)
    for h in range(2):
        s = _bdot(qn * hm[h], kn, NT) * 0.125
        s = jnp.where(mask, s, -jnp.inf)
        m = jnp.max(s, axis=-1, keepdims=True)
        p = jnp.exp(s - m)
        l = jnp.sum(p, axis=-1, keepdims=True)
        o = o + hm[h] * _bdot(p / l, v, NN)
        lse = lse + hm[h] * (m + jnp.log(l))
    return o, lse


def att_fwd(p_att, qw, kw, d, g, name):
    S = p_att.shape[0]
    SEG = ATT_SEG
    nblk = SEG // ATT_BLOCK

    def body(p_ref, qw_ref, kw_ref, o_ref, l_ref, q_s, k_ext, v_ext, o_s, l_s):
        seg = pl.program_id(1)

        @pl.when(seg == 0)
        def _():
            k_ext[SEG:, :] = jnp.zeros((SEG, LANES), f32)
            v_ext[SEG:, :] = jnp.zeros((SEG, LANES), f32)

        k_ext[:SEG, :] = k_ext[SEG:, :]
        v_ext[:SEG, :] = v_ext[SEG:, :]
        q_s[...] = p_ref[:, 0:128]
        k_ext[SEG:, :] = p_ref[:, 128:256]
        v_ext[SEG:, :] = p_ref[:, 256:384]
        qw_v, kw_v = qw_ref[...], kw_ref[...]

        def blk(b, carry):
            j, r = b // d, b % d
            qs = j * (ATT_BLOCK * d) + r
            ks = SEG + qs - ATT_BLOCK * d
            o, lse = _att_block(q_s[_rows(qs, ATT_BLOCK, d), :], k_ext[_rows(ks, 2 * ATT_BLOCK, d), :],
                                v_ext[_rows(ks, 2 * ATT_BLOCK, d), :], qw_v, kw_v, (seg > 0) | (j > 0))
            o_s[_rows(qs, ATT_BLOCK, d), :] = o
            l_s[_rows(qs, ATT_BLOCK, d), :] = lse
            return carry

        lax.fori_loop(0, nblk, blk, 0, unroll=4)
        o_ref[...] = o_s[...]
        l_ref[...] = l_s[...]

    vec = pl.BlockSpec((1, LANES), lambda hh, s: (0, 0))
    out = pl.BlockSpec((SEG, LANES), lambda hh, s: (s, hh))
    return _call(body, name, (2, S // SEG), [pl.BlockSpec((SEG, 384), lambda hh, s: (s, MAIN_ATT_BLOCK + 2 * g + hh)), vec, vec],
                 [out, out], [jax.ShapeDtypeStruct((S, 256), f32), jax.ShapeDtypeStruct((S, 256), f32)],
                 scratch=[pltpu.VMEM((SEG, LANES), f32), pltpu.VMEM((2 * SEG, LANES), f32),
                          pltpu.VMEM((2 * SEG, LANES), f32), pltpu.VMEM((SEG, LANES), f32),
                          pltpu.VMEM((SEG, LANES), f32)],
                 sem=("arbitrary", "arbitrary"))(p_att, qw, kw)


def att_bwd(p_att, o, lse, do, dlse, qw, kw, d, g, dp_main, name):
    S = p_att.shape[0]
    SEG = ATT_SEG
    nseg = S // SEG
    nblk = SEG // ATT_BLOCK

    def body(p_ref, pp_ref, o_ref, l_ref, do_ref, dl_ref, qw_ref, kw_ref, _, dp_ref, dqw_ref, dkw_ref,
             q_s, k_ext, v_ext, dq_s, dk_ext, dv_ext):
        hh, i = pl.program_id(0), pl.program_id(1)
        seg = nseg - 1 - i

        @pl.when(i == 0)
        def _():
            dk_ext[...] = jnp.zeros_like(dk_ext)
            dv_ext[...] = jnp.zeros_like(dv_ext)

        @pl.when((i == 0) & (hh == 0))
        def _():
            dqw_ref[...] = jnp.zeros_like(dqw_ref)
            dkw_ref[...] = jnp.zeros_like(dkw_ref)

        dk_ext[SEG:, :] = dk_ext[:SEG, :]
        dv_ext[SEG:, :] = dv_ext[:SEG, :]
        dk_ext[:SEG, :] = jnp.zeros((SEG, LANES), f32)
        dv_ext[:SEG, :] = jnp.zeros((SEG, LANES), f32)
        q_s[...] = p_ref[:, 0:128]
        k_ext[SEG:, :] = p_ref[:, 128:256]
        v_ext[SEG:, :] = p_ref[:, 256:384]
        k_ext[:SEG, :] = pp_ref[:, 128:256]
        v_ext[:SEG, :] = pp_ref[:, 256:384]
        qw_v, kw_v = qw_ref[...], kw_ref[...]

        def blk_pair(i2, carry):
            dqw, dkw = carry
            done = []
            for u in range(2):
                b = 2 * i2 + u
                j, r = b // d, b % d
                qs = j * (ATT_BLOCK * d) + r
                ks = SEG + qs - ATT_BLOCK * d
                has_prev = (seg > 0) | (j > 0)
                qrows, krows = _rows(qs, ATT_BLOCK, d), _rows(ks, 2 * ATT_BLOCK, d)
                dq, dk, dv, dqw_b, dkw_b = _att_block_bwd(
                    q_s[qrows, :], k_ext[krows, :], v_ext[krows, :], o_ref[qrows, :], l_ref[qrows, :],
                    do_ref[qrows, :], dl_ref[qrows, :], qw_v, kw_v, has_prev)
                dqw, dkw = dqw + dqw_b, dkw + dkw_b
                done.append((qrows, krows, dq, dk, dv))
            for qrows, krows, dq, dk, dv in done:
                dq_s[qrows, :] = dq
                dk_ext[krows, :] = dk_ext[krows, :] + dk
                dv_ext[krows, :] = dv_ext[krows, :] + dv
            return dqw, dkw

        zero = jnp.zeros((1, LANES), f32)
        dqw, dkw = lax.fori_loop(0, nblk // 2, blk_pair, (zero, zero))
        dqw_ref[...] += dqw
        dkw_ref[...] += dkw
        dp_ref[:, 0:128] = dq_s[...].astype(bf16)
        dp_ref[:, 128:256] = dk_ext[SEG:, :].astype(bf16)
        dp_ref[:, 256:384] = dv_ext[SEG:, :].astype(bf16)

    rev = lambda i: nseg - 1 - i
    vec = pl.BlockSpec((1, LANES), lambda hh, i: (0, 0))
    blk = MAIN_ATT_BLOCK + 2 * g
    cur = pl.BlockSpec((SEG, 384), lambda hh, i: (rev(i), blk + hh))
    prev = pl.BlockSpec((SEG, 384), lambda hh, i: (jnp.maximum(rev(i) - 1, 0), blk + hh))
    col = pl.BlockSpec((SEG, LANES), lambda hh, i: (rev(i), hh))
    big = pltpu.VMEM((2 * SEG, LANES), f32)
    one = pltpu.VMEM((SEG, LANES), f32)
    return _call(body, name, (2, nseg), [cur, prev, col, col, col, col, vec, vec, _ANY], [cur, vec, vec],
                 [jax.ShapeDtypeStruct((S, MAIN_WIDTH), bf16), jax.ShapeDtypeStruct((1, LANES), f32),
                  jax.ShapeDtypeStruct((1, LANES), f32)],
                 scratch=[one, big, big, one, big, big], sem=("arbitrary", "arbitrary"),
                 aliases={8: 0})(p_att, p_att, o, lse, do, dlse, qw, kw, dp_main)


def conv_fwd(p_ssd, conv_w, conv_b, name):
    S = p_ssd.shape[0]
    tm, C = CONV_ROWS, SSD_XBC

    def body(x_ref, xp_ref, w_ref, b_ref, o_ref):
        first = (pl.program_id(0) == 0)
        ext = jnp.concatenate([jnp.where(first, 0.0, xp_ref[:, 0:C]), x_ref[:, 0:C]], axis=0)
        acc = b_ref[...] + w_ref[3:4, :] * ext[8:, :]
        for k in range(1, 4):
            acc = acc + w_ref[3 - k:4 - k, :] * pltpu.roll(ext, k, 0)[8:, :]
        o_ref[...] = jax.nn.silu(acc)

    return _call(body, name, (S // tm,),
                 [pl.BlockSpec((tm, 1536), lambda i: (i, MAIN_SSD_BLOCK)),
                  pl.BlockSpec((8, 1536), lambda i: (jnp.maximum(i * (tm // 8) - 1, 0), MAIN_SSD_BLOCK)),
                  pl.BlockSpec((4, C), lambda i: (0, 0)), pl.BlockSpec((1, C), lambda i: (0, 0))],
                 pl.BlockSpec((tm, C), lambda i: (i, 0)), jax.ShapeDtypeStruct((S, C), f32),
                 sem=("parallel",))(p_ssd, p_ssd, conv_w, conv_b)


def conv_bwd(p_ssd, dact, ddt, conv_w, conv_b, dp_main, name):
    S = p_ssd.shape[0]
    tm, C = CONV_ROWS, SSD_XBC
    nblk = S // tm

    def body(x_ref, xp_ref, xn_ref, da_ref, dan_ref, ddt_ref, w_ref, b_ref, _, dp_ref, dw_ref, db_ref):
        i = pl.program_id(0)
        rows = tm + 8
        ext = jnp.concatenate([jnp.where(i == 0, 0.0, xp_ref[:, 0:C]), x_ref[:, 0:C], xn_ref[:, 0:C]], axis=0)
        shifted = [ext[8:, :]] + [pltpu.roll(ext, k, 0)[8:, :] for k in range(1, 4)]
        pre = b_ref[...] + w_ref[3:4, :] * shifted[0]
        for k in range(1, 4):
            pre = pre + w_ref[3 - k:4 - k, :] * shifted[k]
        sg = jax.nn.sigmoid(pre)
        dact = jnp.concatenate([da_ref[...], jnp.where(i == nblk - 1, 0.0, dan_ref[...])], axis=0)
        dpre = dact * (sg * (1.0 + pre * (1.0 - sg)))
        dx = w_ref[3:4, :] * dpre[0:tm, :]
        for k in range(1, 4):
            dx = dx + w_ref[3 - k:4 - k, :] * pltpu.roll(dpre, rows - k, 0)[0:tm, :]
        dp_ref[:, 0:C] = dx.astype(bf16)
        dp_ref[:, C:C + 128] = ddt_ref[...].astype(bf16)
        dp_ref[:, C + 128:] = jnp.zeros((tm, 128), bf16)
        dcur = dpre[0:tm, :]
        dws = [jnp.sum(dcur * shifted[3 - j][0:tm, :], axis=0, keepdims=True) for j in range(4)]
        dbs = jnp.sum(dcur, axis=0, keepdims=True)

        @pl.when(i == 0)
        def _():
            dw_ref[...] = jnp.zeros_like(dw_ref)
            db_ref[...] = jnp.zeros_like(db_ref)

        for j in range(4):
            dw_ref[j:j + 1, :] += dws[j]
        db_ref[...] += dbs

    t8 = tm // 8
    blk = MAIN_SSD_BLOCK
    return _call(body, name, (nblk,),
                 [pl.BlockSpec((tm, 1536), lambda i: (i, blk)),
                  pl.BlockSpec((8, 1536), lambda i: (jnp.maximum(i * t8 - 1, 0), blk)),
                  pl.BlockSpec((8, 1536), lambda i: (jnp.minimum((i + 1) * t8, S // 8 - 1), blk)),
                  pl.BlockSpec((tm, C), lambda i: (i, 0)),
                  pl.BlockSpec((8, C), lambda i: (jnp.minimum((i + 1) * t8, S // 8 - 1), 0)),
                  pl.BlockSpec((tm, 128), lambda i: (i, 0)),
                  pl.BlockSpec((4, C), lambda i: (0, 0)), pl.BlockSpec((1, C), lambda i: (0, 0)), _ANY],
                 [pl.BlockSpec((tm, 1536), lambda i: (i, blk)), pl.BlockSpec((4, C), lambda i: (0, 0)),
                  pl.BlockSpec((1, C), lambda i: (0, 0))],
                 [jax.ShapeDtypeStruct((S, MAIN_WIDTH), bf16), jax.ShapeDtypeStruct((4, C), f32),
                  jax.ShapeDtypeStruct((1, C), f32)],
                 sem=("arbitrary",), aliases={8: 0})(p_ssd, p_ssd, p_ssd, dact, dact, ddt, conv_w, conv_b, dp_main)


def _ssd_chunk(xbc, dtr, state, dt_bias, a_log, d_full):
    T = SSD_CHUNK
    r_i = lax.broadcasted_iota(jnp.int32, (T, T), 0)
    c_i = lax.broadcasted_iota(jnp.int32, (T, T), 1)
    tril = c_i <= r_i
    tri = tril.astype(bf16)
    lane = lax.broadcasted_iota(jnp.int32, (1, LANES), 1)
    hm = [(lane < 64).astype(f32), (lane >= 64).astype(f32)]
    column = lambda v, h: jnp.broadcast_to(v[:, h:h + 1], (T, LANES))

    def per_head_lanes(v):
        return jnp.concatenate([jnp.where(lane < 64, column(v, 2 * pp), column(v, 2 * pp + 1)) for pp in range(6)],
                               axis=1)

    xs, bm, cm = xbc[:, :768], xbc[:, 768:1024], xbc[:, 1024:1280]
    dt = _softplus(dtr + dt_bias)
    a_dt = dt * (-jnp.exp(a_log))
    a_cs = _xdot_l(tri, a_dt)
    dt_full = per_head_lanes(dt)
    acs_full = per_head_lanes(a_cs)
    last = lax.broadcasted_iota(jnp.int32, (T, SSD_WIDTH), 0) == T - 1
    tot_full = jnp.sum(jnp.where(last, acs_full, 0.0), axis=0, keepdims=True)
    xdt = xs * dt_full
    xw = xdt * jnp.exp(tot_full - acs_full)
    eacs = jnp.exp(acs_full)
    st_parts, off_parts, diag_parts = [], [], []
    for g in range(2):
        bg, cg = bm[:, 128 * g:128 * (g + 1)], cm[:, 128 * g:128 * (g + 1)]
        cols = slice(384 * g, 384 * (g + 1))
        st_parts.append(_bdot(bg, xw[:, cols], TN))
        off_parts.append(_bdot(cg, state[:, cols], NN))
        cb = _bdot(cg, bg, NT)
        for pp in range(3 * g, 3 * g + 3):
            xp = xdt[:, 128 * pp:128 * (pp + 1)]
            acc = jnp.zeros((T, LANES), f32)
            for hh in range(2):
                a_col = column(a_cs, 2 * pp + hh)
                decay = jnp.where(tril, jnp.exp(jnp.minimum(a_col - a_col.T, 0.0)), 0.0)
                acc = acc + _bdot(cb * decay, xp * hm[hh], NN)
            diag_parts.append(acc)
    new_state = state * jnp.exp(tot_full) + jnp.concatenate(st_parts, axis=1)
    y = jnp.concatenate(diag_parts, axis=1) + jnp.concatenate(off_parts, axis=1) * eacs + xs * d_full
    return y, new_state


def ssd_fwd(xact, p_ssd, dt_bias, a_log, d_full, name):
    S = xact.shape[0]
    T = SSD_CHUNK

    def body(x_ref, p_ref, b_ref, a_ref, d_ref, y_ref, s_ref, state):
        @pl.when(pl.program_id(0) == 0)
        def _():
            state[...] = jnp.zeros_like(state)

        st = state[...]
        s_ref[0] = st
        y, new = _ssd_chunk(x_ref[...], p_ref[...], st, b_ref[...], a_ref[...], d_ref[...])
        y_ref[...] = y
        state[...] = new

    vec = lambda n: pl.BlockSpec((1, n), lambda i: (0, 0))
    return _call(body, name, (S // T,),
                 [pl.BlockSpec((T, SSD_XBC), lambda i: (i, 0)), pl.BlockSpec((T, 128), lambda i: (i, MAIN_DT_BLOCK)),
                  vec(128), vec(128), vec(768)],
                 [pl.BlockSpec((T, 768), lambda i: (i, 0)), pl.BlockSpec((1, T, 768), lambda i: (i, 0, 0))],
                 [jax.ShapeDtypeStruct((S, 768), f32), jax.ShapeDtypeStruct((S // T, T, 768), f32)],
                 scratch=[pltpu.VMEM((T, 768), f32)], sem=("arbitrary",))(xact, p_ssd, dt_bias, a_log, d_full)


def ssd_bwd(xact, p_ssd, states, dy, dt_bias, a_log, d_full, name):
    S = xact.shape[0]
    T = SSD_CHUNK
    nc = S // T

    def body(x_ref, p_ref, s_ref, dy_ref, b_ref, a_ref, d_ref, dx_ref, ddt_ref, db_ref, da_ref, dd_ref, dstate):
        i = pl.program_id(0)

        @pl.when(i == 0)
        def _():
            for ref in (dstate, db_ref, da_ref, dd_ref):
                ref[...] = jnp.zeros_like(ref)

        _, vjp = jax.vjp(_ssd_chunk, x_ref[...], p_ref[...], s_ref[0], b_ref[...], a_ref[...], d_ref[...])
        dx, ddt, dst, db, da, dd = vjp((dy_ref[...], dstate[...]))
        dx_ref[...] = dx
        ddt_ref[...] = ddt
        dstate[...] = dst
        db_ref[...] += db
        da_ref[...] += da
        dd_ref[...] += dd

    rev = lambda i: nc - 1 - i
    vec = lambda n: pl.BlockSpec((1, n), lambda i: (0, 0))
    return _call(body, name, (nc,),
                 [pl.BlockSpec((T, SSD_XBC), lambda i: (rev(i), 0)), pl.BlockSpec((T, 128), lambda i: (rev(i), MAIN_DT_BLOCK)),
                  pl.BlockSpec((1, T, 768), lambda i: (rev(i), 0, 0)), pl.BlockSpec((T, 768), lambda i: (rev(i), 0)),
                  vec(128), vec(128), vec(768)],
                 [pl.BlockSpec((T, SSD_XBC), lambda i: (rev(i), 0)), pl.BlockSpec((T, 128), lambda i: (rev(i), 0)),
                  vec(128), vec(128), vec(768)],
                 [jax.ShapeDtypeStruct((S, SSD_XBC), f32), jax.ShapeDtypeStruct((S, 128), f32),
                  jax.ShapeDtypeStruct((1, 128), f32), jax.ShapeDtypeStruct((1, 128), f32),
                  jax.ShapeDtypeStruct((1, 768), f32)],
                 scratch=[pltpu.VMEM((T, 768), f32)],
                 sem=("arbitrary",))(xact, p_ssd, states, dy, dt_bias, a_log, d_full)


def _tail_fn(ys5, pt, o0, o1, o2, l0, l1, l2, yssd, glu_b, nw, pr_glu, pr_a, pr_b, pr_c, x, weights):
    glu_w, pa, pb, pc, wo = weights
    gates = jax.nn.sigmoid(pt[:, :3072])
    za, zb, zc = pt[:, 3072:3584], pt[:, 3584:3840], pt[:, 3840:4608]
    g = jax.nn.gelu(ys5)
    ya = g * jax.nn.sigmoid(_cdot(g, glu_w, NN) + glu_b + pr_glu) * jax.nn.silu(za)
    m = jnp.maximum(jnp.maximum(l0, l1), l2)
    e0, e1, e2 = jnp.exp(l0 - m), jnp.exp(l1 - m), jnp.exp(l2 - m)
    yb = (e0 * o0 + e1 * o1 + e2 * o2) / (e0 + e1 + e2) * jax.nn.silu(zb)
    yc = _rms(yssd * jax.nn.silu(zc), nw)
    merged = (gates[:, :1024] * (_cdot(ya, pa, NN) + pr_a) + gates[:, 1024:2048] * (_cdot(yb, pb, NN) + pr_b)
              + gates[:, 2048:] * (_cdot(yc, pc, NN) + pr_c))
    out = x + _cdot(merged, wo, NN)
    return out, (g, ya, yb, yc, merged)


def _tail_specs(tm):
    row = lambda n: pl.BlockSpec((tm, n), lambda i: (i, 0))
    full = lambda a, b: pl.BlockSpec((a, b), lambda i: (0, 0))
    acts = [row(512), row(4608)] + [row(256)] * 6 + [row(768), row(D_MODEL)]
    consts = [full(1, 512), full(1, 768), full(512, 512), full(512, D_MODEL), full(256, D_MODEL),
              full(768, D_MODEL), full(D_MODEL, D_MODEL)]
    return row, full, acts, consts


def tail_fwd(ys5, pt, os_, ls_, yssd, x, glu_b, nw, weights, name):
    S = x.shape[0]
    tm = 2 * TAIL_ROWS
    row, full, acts, consts = _tail_specs(tm)

    def body(ys5_ref, pt_ref, o0, o1, o2, l0, l1, l2, yssd_ref, x_ref, gb_ref, nw_ref, gw, pa, pb, pc, wo, out_ref):
        z = lambda n: jnp.zeros((tm, n), f32)
        out, _ = _tail_fn(ys5_ref[...], pt_ref[...], o0[...], o1[...], o2[...], l0[...], l1[...], l2[...],
                          yssd_ref[...], gb_ref[...], nw_ref[...], z(512), z(D_MODEL), z(D_MODEL), z(D_MODEL),
                          x_ref[...], (gw[...], pa[...], pb[...], pc[...], wo[...]))
        out_ref[...] = out

    return _call(body, name, (S // tm,), acts + consts, row(D_MODEL), jax.ShapeDtypeStruct((S, D_MODEL), f32),
                 sem=("parallel",))(ys5, pt, *os_, *ls_, yssd, x, glu_b, nw, *weights)


def tail_bwd(ys5, pt, os_, ls_, yssd, dout, glu_b, nw, weights, name):
    S = dout.shape[0]
    tm = TAIL_ROWS
    row, full, acts, consts = _tail_specs(tm)

    def body(ys5_ref, pt_ref, o0, o1, o2, l0, l1, l2, yssd_ref, dout_ref, gb_ref, nw_ref, gw, pa, pb, pc, wo,
             dys5_ref, dpt_ref, do0, do1, do2, dl0, dl1, dl2, dyssd_ref, dgb_ref, dnw_ref,
             g_ref, ya_ref, yb_ref, yc_ref, mg_ref, dglu_ref, dpa_ref, dpb_ref, dpc_ref):
        z = lambda n: jnp.zeros((tm, n), f32)
        w = (gw[...], pa[...], pb[...], pc[...], wo[...])
        fn = lambda *a: _tail_fn(*a, z(D_MODEL), w)
        _, vjp, aux = jax.vjp(fn, ys5_ref[...], pt_ref[...], o0[...], o1[...], o2[...], l0[...], l1[...], l2[...],
                              yssd_ref[...], gb_ref[...], nw_ref[...], z(512), z(D_MODEL), z(D_MODEL), z(D_MODEL),
                              has_aux=True)
        (dys5, dpt, d0, d1, d2, e0, e1, e2, dyssd, dgb, dnw, dglu, dpa, dpb, dpc) = vjp(dout_ref[...])
        dys5_ref[...] = dys5
        dpt_ref[...] = dpt.astype(bf16)
        for ref, val in ((do0, d0), (do1, d1), (do2, d2), (dl0, e0), (dl1, e1), (dl2, e2)):
            ref[...] = val
        dyssd_ref[...] = dyssd
        g, ya, yb, yc, merged = aux
        for ref, val in ((g_ref, g), (ya_ref, ya), (yb_ref, yb), (yc_ref, yc), (mg_ref, merged),
                         (dglu_ref, dglu), (dpa_ref, dpa), (dpb_ref, dpb), (dpc_ref, dpc)):
            ref[...] = val.astype(bf16)

        @pl.when(pl.program_id(0) == 0)
        def _():
            dgb_ref[...] = dgb
            dnw_ref[...] = dnw

        @pl.when(pl.program_id(0) > 0)
        def _():
            dgb_ref[...] += dgb
            dnw_ref[...] += dnw

    sd = lambda n, dt=f32: jax.ShapeDtypeStruct((S, n), dt)
    out_specs = ([row(512), row(4608)] + [row(256)] * 6 + [row(768), full(1, 512), full(1, 768)]
                 + [row(512), row(512), row(256), row(768), row(D_MODEL), row(512)] + [row(D_MODEL)] * 3)
    out_shape = ([sd(512), sd(MAIN_WIDTH, bf16)] + [sd(256)] * 6 + [sd(768), jax.ShapeDtypeStruct((1, 512), f32),
                                                          jax.ShapeDtypeStruct((1, 768), f32)]
                 + [sd(512, bf16), sd(512, bf16), sd(256, bf16), sd(768, bf16), sd(D_MODEL, bf16), sd(512, bf16)]
                 + [sd(D_MODEL, bf16)] * 3)
    return _call(body, name, (S // tm,), acts + consts, out_specs, out_shape,
                 sem=("arbitrary",))(ys5, pt, *os_, *ls_, yssd, dout, glu_b, nw, *weights)


def _in_proj_segments(shards):
    dtype = shards[0].dtype

    def c(a, b):
        parts = []
        for k, sh in enumerate(shards):
            lo, hi = max(a, W_IN_SHARD * k), min(b, W_IN_SHARD * (k + 1))
            if lo < hi:
                parts.append(sh[:, lo - W_IN_SHARD * k:hi - W_IN_SHARD * k])
        return parts[0] if len(parts) == 1 else jnp.concatenate(parts, axis=1)

    atts = []
    for g in range(3):
        parts = []
        for hh in range(2):
            o = 64 * (4 * g + 2 * hh)
            parts += [c(_C_Q + o, _C_Q + o + 128), c(_C_K + o, _C_K + o + 128), c(_C_V + o, _C_V + o + 128)]
        atts.append(jnp.concatenate(parts, axis=1))
    ssd = jnp.concatenate([c(_C_XBC, _C_ZC), jnp.zeros((D_MODEL, 1536 - (_C_ZC - _C_XBC)), dtype)], axis=1)
    tail = jnp.concatenate([c(_C_GATE, _C_END), c(_C_ZA, _C_Q), c(_C_ZB, _C_XBC), c(_C_ZC, _C_GATE)], axis=1)
    return [c(_C_UA, _C_ZA), jnp.concatenate([tail, ssd] + atts, axis=1)]


def _in_proj_grad(ds5, dmain):
    dtail, dssd = dmain[:, :4608], dmain[:, 4608:6144]
    datts = [dmain[:, 6144 + 768 * g:6144 + 768 * (g + 1)] for g in range(3)]
    pick = lambda off: [datts[g][:, 384 * hh + off:384 * hh + off + 128] for g in range(3) for hh in range(2)]
    pieces = ([ds5, dtail[:, 3072:3584]] + pick(0) + pick(128) + pick(256)
              + [dtail[:, 3584:3840], dssd[:, :_C_ZC - _C_XBC], dtail[:, 3840:4608], dtail[:, :3072]])
    shards, start = [[] for _ in range(4)], 0
    for piece in pieces:
        width = piece.shape[1]
        for k in range(4):
            lo, hi = max(start, W_IN_SHARD * k), min(start + width, W_IN_SHARD * (k + 1))
            if lo < hi:
                shards[k].append(piece[:, lo - start:hi - start])
        start += width
    return jnp.stack([jnp.concatenate(s, axis=1) for s in shards])


def _prep_layer(p):
    q = {}
    q["segs"] = [s.astype(bf16) for s in _in_proj_segments(p["w_in"])]
    disc = _s5_discretize(p["s5_a_re"], p["s5_a_im"], p["s5_log_step"], p["s5_b_re"], p["s5_b_im"],
                          p["s5_c_re"], p["s5_c_im"])
    q["s5"] = disc
    q["pw"] = _lam_powers(disc[0], disc[1])
    q["s5_d"] = p["s5_d"].reshape(1, 512)
    q["qw"] = jnp.tile(p["q_norm_w"], 2).reshape(1, LANES)
    q["kw"] = jnp.tile(p["k_norm_w"], 2).reshape(1, LANES)
    q["conv_w"] = p["conv_w"]
    q["conv_b"] = p["conv_b"].reshape(1, SSD_XBC)
    pad = lambda v: jnp.pad(v, (0, LANES - v.shape[0])).reshape(1, LANES)
    q["dt_bias"], q["a_log"] = pad(p["dt_bias"]), pad(p["ssd_a_log"])
    q["d_full"] = jnp.repeat(p["ssd_d"], 64).reshape(1, SSD_WIDTH)
    q["glu_b"] = p["s5_glu_b"].reshape(1, 512)
    q["nw"] = p["ssd_norm_w"].reshape(1, SSD_WIDTH)
    q["norm_w"] = p["norm_w"].reshape(1, D_MODEL)
    q["tailw"] = tuple(p[n].astype(bf16) for n in ("s5_glu_w", "proj_a", "proj_b", "proj_c", "w_out"))
    return q


_DILATIONS = (1, 4, 16)


def layer_fwd(x, q, tag):
    h = rms_fwd(x, q["norm_w"], f"rms_fwd{tag}")
    p_s5, p_main = [mm_nn(h, w, f"inproj{k}{tag}") for k, w in enumerate(q["segs"])]
    _, _, w_re, w_im, c_re, c_im = q["s5"]
    ys5, h_re, h_im = s5_fwd(p_s5, *q["pw"], w_re, w_im, c_re, c_im, q["s5_d"], f"s5_fwd{tag}")
    os_, ls_ = [], []
    for g, d in enumerate(_DILATIONS):
        o, l = att_fwd(p_main, q["qw"], q["kw"], d, g, f"att_fwd{g}{tag}")
        os_.append(o)
        ls_.append(l)
    xact = conv_fwd(p_main, q["conv_w"], q["conv_b"], f"conv_fwd{tag}")
    yssd, states = ssd_fwd(xact, p_main, q["dt_bias"], q["a_log"], q["d_full"], f"ssd_fwd{tag}")
    out = tail_fwd(ys5, p_main, os_, ls_, yssd, x, q["glu_b"], q["nw"], q["tailw"], f"tail_fwd{tag}")
    saved = dict(x=x, h=h, p_s5=p_s5, p_main=p_main, ys5=ys5, h_re=h_re, h_im=h_im,
                 os=os_, ls=ls_, xact=xact, yssd=yssd, states=states)
    return out, saved


def layer_bwd(dout, sv, q, p, tag):
    S = dout.shape[0]
    (dys5, dp_main, do0, do1, do2, dl0, dl1, dl2, dyssd, dglu_b, dnw, g_b, ya_b, yb_b, yc_b, mg_b, dglu_b16,
     dpa_b, dpb_b, dpc_b) = tail_bwd(sv["ys5"], sv["p_main"], sv["os"], sv["ls"], sv["yssd"], dout, q["glu_b"],
                                     q["nw"], q["tailw"], f"tail_bwd{tag}")
    grads = {}
    grads["s5_glu_w"] = mm_tn(g_b, dglu_b16, f"dglu_w{tag}")
    grads["proj_a"] = mm_tn(ya_b, dpa_b, f"dproj_a{tag}")
    grads["proj_b"] = mm_tn(yb_b, dpb_b, f"dproj_b{tag}")
    grads["proj_c"] = mm_tn(yc_b, dpc_b, f"dproj_c{tag}")
    grads["w_out"] = mm_tn(mg_b, dout, f"dw_out{tag}")
    grads["s5_glu_b"] = dglu_b.reshape(512)
    grads["ssd_norm_w"] = dnw.reshape(SSD_WIDTH)

    dxact, ddt, ddt_bias, da_log, dd_full = ssd_bwd(sv["xact"], sv["p_main"], sv["states"], dyssd, q["dt_bias"],
                                                    q["a_log"], q["d_full"], f"ssd_bwd{tag}")
    dp_main, dconv_w, dconv_b = conv_bwd(sv["p_main"], dxact, ddt, q["conv_w"], q["conv_b"], dp_main,
                                         f"conv_bwd{tag}")
    grads["dt_bias"] = ddt_bias[0, :12]
    grads["ssd_a_log"] = da_log[0, :12]
    grads["ssd_d"] = dd_full.reshape(12, 64).sum(axis=1)
    grads["conv_w"] = dconv_w
    grads["conv_b"] = dconv_b.reshape(SSD_XBC)

    dqw, dkw = 0.0, 0.0
    for g, d in enumerate(_DILATIONS):
        dp_main, a, b = att_bwd(sv["p_main"], sv["os"][g], sv["ls"][g], (do0, do1, do2)[g], (dl0, dl1, dl2)[g],
                                q["qw"], q["kw"], d, g, dp_main, f"att_bwd{g}{tag}")
        dqw, dkw = dqw + a, dkw + b
    grads["q_norm_w"] = dqw.reshape(2, 64).sum(axis=0)
    grads["k_norm_w"] = dkw.reshape(2, 64).sum(axis=0)

    _, _, w_re, w_im, c_re, c_im = q["s5"]
    dp_s5, dwre, dwim, dcre, dcim, dlam_re, dlam_im, dd = s5_bwd(
        dys5, sv["p_s5"], sv["h_re"], sv["h_im"], *q["pw"], w_re, w_im, c_re, c_im, q["s5_d"], f"s5_bwd{tag}")
    s5_names = ("s5_a_re", "s5_a_im", "s5_log_step", "s5_b_re", "s5_b_im", "s5_c_re", "s5_c_im")
    _, disc_vjp = jax.vjp(_s5_discretize, *[p[n] for n in s5_names])
    for n, gr in zip(s5_names, disc_vjp((dlam_re, dlam_im, dwre, dwim, dcre, dcim))):
        grads[n] = gr
    grads["s5_d"] = dd.reshape(512)

    dsegs = [dp_s5, dp_main]
    dws = [mm_tn(sv["h"], ds, f"dw_in{k}{tag}") for k, ds in enumerate(dsegs)]
    grads["w_in"] = _in_proj_grad(*dws)
    dh = None
    for k, (ds, w) in enumerate(zip(dsegs, q["segs"])):
        dh = mm_nt(ds, w, f"dh{k}{tag}", acc=dh)
    dx, dnorm_w = rms_bwd(sv["x"], q["norm_w"], dh, dout, f"rms_bwd{tag}")
    grads["norm_w"] = dnorm_w.reshape(D_MODEL)
    return dx, grads


def _exchange(name, scatter=(), gather=(), sibling=(), sibling_both=False):
    scatter, gather, sibling = list(scatter), list(gather), list(sibling)
    chip_xs = scatter + gather
    ns, nc, nb = len(scatter), len(chip_xs), len(sibling)
    n = nc + nb

    def body(*refs):
        x_refs, o_refs, send_sems, recv_sems = refs[:n], refs[n:2 * n], refs[2 * n], refs[2 * n + 1]
        mx, my, mc = lax.axis_index("x"), lax.axis_index("y"), lax.axis_index("c")
        me = 2 * mx + my
        copies = []
        for a in range(nc):
            for t, (px, py) in enumerate(((1 - mx, my), (mx, 1 - my), (1 - mx, 1 - my))):
                src = x_refs[a].at[2 * px + py] if a < ns else x_refs[a]
                copies.append(pltpu.make_async_remote_copy(
                    src_ref=src, dst_ref=o_refs[a].at[me], send_sem=send_sems.at[3 * a + t],
                    recv_sem=recv_sems.at[3 * a + t], device_id=(px, py, mc), device_id_type=pl.DeviceIdType.MESH))
        for b in range(nc, n):
            k = 3 * nc + b - nc
            copies.append(pltpu.make_async_remote_copy(
                src_ref=x_refs[b], dst_ref=o_refs[b].at[mc] if sibling_both else o_refs[b], send_sem=send_sems.at[k],
                recv_sem=recv_sems.at[k], device_id=(mx, my, 1 - mc), device_id_type=pl.DeviceIdType.MESH))
        for cp in copies:
            cp.start()
        for cp in copies:
            cp.wait()

    shapes = ([(4,) + tuple(x.shape[1:]) for x in scatter] + [(4,) + tuple(x.shape) for x in gather]
              + [((2,) if sibling_both else ()) + tuple(x.shape) for x in sibling])
    xs = chip_xs + sibling
    outs = pl.pallas_call(
        body, name=name, in_specs=[_ANY] * n, out_specs=[_ANY] * n,
        out_shape=[jax.ShapeDtypeStruct(s, x.dtype) for s, x in zip(shapes, xs)],
        scratch_shapes=[pltpu.SemaphoreType.DMA((3 * nc + nb,)), pltpu.SemaphoreType.DMA((3 * nc + nb,))],
    )(*xs)
    me, c = 2 * lax.axis_index("x") + lax.axis_index("y"), lax.axis_index("c")
    fixed = []
    for a, (o, x) in enumerate(zip(outs, xs)):
        if a < ns:
            o = lax.dynamic_update_index_in_dim(o, lax.dynamic_index_in_dim(x, me, 0, keepdims=True), me, 0)
        elif a < nc:
            o = lax.dynamic_update_index_in_dim(o, x[None], me, 0)
        elif sibling_both:
            o = lax.dynamic_update_index_in_dim(o, x[None], c, 0)
        fixed.append(o)
    return fixed[:ns], fixed[ns:nc], fixed[nc:]


def _rows_tile(rows, row_bytes, budget=1 << 20):
    return next(t for t in (512, 256, 128, 64, 32, 16, 8) if rows % t == 0 and t * row_bytes <= budget)


def _padded_row_bytes(cols):
    return -(-cols // LANES) * LANES * 4


def _add2(a, b, name, out_dtype=f32):
    R, C = a.shape
    tr = _rows_tile(R, _padded_row_bytes(C))

    def body(a_ref, b_ref, o_ref):
        o_ref[...] = (a_ref[...] + b_ref[...]).astype(out_dtype)

    spec = pl.BlockSpec((tr, C), lambda i: (i, 0))
    return _call(body, name, (R // tr,), [spec, spec], spec, jax.ShapeDtypeStruct((R, C), out_dtype),
                 sem=("parallel",))(a, b)


def _sum4(x, name):
    R = x.shape[1]
    tr = _tile(R, (512, 256, 128))

    def body(x_ref, o_ref):
        p = [x_ref[j].astype(f32) for j in range(4)]
        o_ref[...] = ((p[0] + p[1]) + p[2]) + p[3]

    return _call(body, name, (R // tr,), [pl.BlockSpec((4, tr, LANES), lambda i: (0, i, 0))],
                 pl.BlockSpec((tr, LANES), lambda i: (i, 0)), jax.ShapeDtypeStruct((R, LANES), f32),
                 sem=("parallel",))(x)


def _adamw(g_parts, w, m, v, name):
    stacked = not isinstance(g_parts, (tuple, list))
    k = g_parts.shape[0] if stacked else len(g_parts)
    R, C = w.shape
    tr = _rows_tile(R, _padded_row_bytes(C))
    c1 = 1.0 - ADAM_B1 ** ADAM_STEP
    c2 = 1.0 - ADAM_B2 ** ADAM_STEP

    def body(*refs):
        w_ref, m_ref, v_ref, g_ref, d_ref, nm_ref, nv_ref = refs[-7:]
        if stacked:
            g = refs[0][0].astype(f32)
            for j in range(1, k):
                g = g + refs[0][j].astype(f32)
        else:
            g = refs[0][...]
            for r in refs[1:k]:
                g = g + r[...]
        m = ADAM_B1 * m_ref[...] + (1.0 - ADAM_B1) * g
        v = ADAM_B2 * v_ref[...] + (1.0 - ADAM_B2) * (g * g)
        g_ref[...] = g
        nm_ref[...] = m
        nv_ref[...] = v
        d_ref[...] = -ADAM_LR * ((m / c1) / (jnp.sqrt(v / c2) + ADAM_EPS) + ADAM_WD * w_ref[...])

    spec = pl.BlockSpec((tr, C), lambda i: (i, 0))
    sd = jax.ShapeDtypeStruct((R, C), f32)
    g_specs = [pl.BlockSpec((k, tr, C), lambda i: (0, i, 0))] if stacked else [spec] * k
    g_args = [g_parts] if stacked else list(g_parts)
    return _call(body, name, (R // tr,), g_specs + [spec] * 3, [spec] * 4, [sd] * 4,
                 sem=("parallel",))(*g_args, w, m, v)


def _pack(arrays):
    flat = jnp.concatenate([a.reshape(-1) for a in arrays])
    unit = PACK_ROWS * LANES
    n = -(-flat.shape[0] // unit) * unit
    return jnp.pad(flat, (0, n - flat.shape[0])).reshape(n // LANES, LANES)


def _unpack(buf, shapes):
    flat = buf.reshape(-1)
    out, off = [], 0
    for s in shapes:
        n = 1
        for dim in s:
            n *= dim
        out.append(flat[off:off + n].reshape(s))
        off += n
    return out


def _to_shards(full, axis):
    s = full.shape
    t = full.reshape(s[:axis] + (4, s[axis] // 4) + s[axis + 1:])
    return jnp.moveaxis(t, axis, 0)


def _from_shards(sh, axis):
    t = jnp.moveaxis(sh, 0, axis)
    s = t.shape
    return t.reshape(s[:axis] + (s[axis] * s[axis + 1],) + s[axis + 2:])


def kernel(x, norm_w, w_in, s5_a_re, s5_a_im, s5_log_step, s5_b_re, s5_b_im, s5_c_re, s5_c_im, s5_d, s5_glu_w, s5_glu_b, q_norm_w, k_norm_w, conv_w, conv_b, dt_bias, ssd_a_log, ssd_d, ssd_norm_w, proj_a, proj_b, proj_c, w_out, loss_target, m_norm_w, m_w_in, m_s5_a_re, m_s5_a_im, m_s5_log_step, m_s5_b_re, m_s5_b_im, m_s5_c_re, m_s5_c_im, m_s5_d, m_s5_glu_w, m_s5_glu_b, m_q_norm_w, m_k_norm_w, m_conv_w, m_conv_b, m_dt_bias, m_ssd_a_log, m_ssd_d, m_ssd_norm_w, m_proj_a, m_proj_b, m_proj_c, m_w_out, v_norm_w, v_w_in, v_s5_a_re, v_s5_a_im, v_s5_log_step, v_s5_b_re, v_s5_b_im, v_s5_c_re, v_s5_c_im, v_s5_d, v_s5_glu_w, v_s5_glu_b, v_q_norm_w, v_k_norm_w, v_conv_w, v_conv_b, v_dt_bias, v_ssd_a_log, v_ssd_d, v_ssd_norm_w, v_proj_a, v_proj_b, v_proj_c, v_w_out):
    given = dict(locals())
    W = {n: given[n] for n in _WEIGHTS}
    M = {n: given["m_" + n] for n in _WEIGHTS}
    V = {n: given["v_" + n] for n in _WEIGHTS}
    n_layers = norm_w.shape[0]
    assert n_layers == 2
    c = lax.axis_index("c")

    mine_of = lambda t: lax.dynamic_index_in_dim(t, c, 0, keepdims=False)
    as_payload = lambda n: lax.bitcast_convert_type(W[n], bf16) if n == "conv_w" else W[n].astype(bf16)
    payload_shapes = [W[n].shape + ((2,) if n == "conv_w" else ()) for n, _ in _SHARDED]
    wpack = _pack([as_payload(n) for n, _ in _SHARDED])
    half_rows = wpack.shape[0] // 2
    _, (pack_half, w_in_mine_layer), _ = _exchange(
        "gather_weights", gather=[lax.dynamic_slice_in_dim(wpack, c * half_rows, half_rows),
                                  mine_of(w_in).astype(bf16)])
    _, _, (w_in_layers, pack_halves) = _exchange("share_weights", sibling=[w_in_mine_layer, pack_half],
                                                 sibling_both=True)
    gathered = jnp.moveaxis(pack_halves, 0, 1).reshape(4, 2 * half_rows, LANES)
    full = dict(W)
    pieces = [_unpack(gathered[j], payload_shapes) for j in range(4)]
    for k, (n, axis) in enumerate(_SHARDED):
        sh = jnp.stack([pieces[j][k] for j in range(4)])
        full[n] = _from_shards(lax.bitcast_convert_type(sh, f32) if n == "conv_w" else sh, axis)

    xs = x[0]
    qs, saves = [], []
    act = xs
    for l in range(n_layers):
        p = {n: full[n][l] for n in _WEIGHTS if n != "w_in"}
        p["w_in"] = [w_in_layers[l, k] for k in range(4)]
        q = _prep_layer(p)
        act, sv = layer_fwd(act, q, f"_l{l}")
        qs.append((q, p))
        saves.append(sv)
    dact, lsum = loss_and_grad(act, loss_target[0], "loss")
    loss = lax.psum(lsum[0, 0], ("x", "y", "c"))
    layer_grads = [None] * n_layers
    for l in reversed(range(n_layers)):
        q, p = qs[l]
        dact, layer_grads[l] = layer_bwd(dact, saves[l], q, p, f"_l{l}")
    grad_x = dact[None]
    G = {n: jnp.stack([layer_grads[l][n] for l in range(n_layers)]) for n in _WEIGHTS if n != "w_in"}

    repl_shapes = [W[n].shape for n in _REPL]
    small = _pack([G[n] for n in _REPL])
    quarter = small.shape[0] // 4
    big = [_to_shards(G[n], axis).reshape(4, -1) for n, axis in _SHARDED]
    big = jnp.concatenate(big, axis=1)
    unit = PACK_ROWS * LANES
    nbig = -(-big.shape[1] // unit) * unit
    big = jnp.pad(big, ((0, 0), (0, nbig - big.shape[1]))).reshape(4, nbig // LANES, LANES)
    gpack = jnp.concatenate([big, small.reshape(4, quarter, LANES)], axis=1)
    rbig = nbig // LANES
    g0, g1 = layer_grads[0]["w_in"], layer_grads[1]["w_in"]

    (landed_pack,), _, (from_sibling,) = _exchange(
        "swap_w_in_grads_and_scatter_grads", scatter=[gpack.astype(bf16)], sibling=[jnp.where(c == 0, g1, g0)])
    flat = lambda t: t.reshape(4 * D_MODEL, W_IN_SHARD)
    shards = _add2(flat(jnp.where(c == 0, g0, g1)), flat(from_sibling), "sum_cores_w_in", out_dtype=bf16)
    mine = _sum4(landed_pack, "sum_chips")

    (landed,), _, (other,) = _exchange(
        "scatter_w_in_grads_and_swap_cores", scatter=[shards.reshape(4, D_MODEL, W_IN_SHARD)], sibling=[mine])
    w_in_mine = _adamw(landed, mine_of(w_in), mine_of(m_w_in), mine_of(v_w_in), "adamw_w_in")
    gq = _add2(mine[rbig:], other[rbig:], "sum_cores_small")

    _, (gsmall,), w_in_out = _exchange(
        "share_w_in_updates_and_gather_small", gather=[gq], sibling=w_in_mine, sibling_both=True)
    gsmall = gsmall.reshape(4 * quarter, LANES)

    wp, mp, vp = (_pack([T[n] for n, _ in _SHARDED]) for T in (W, M, V))
    outs_big = _adamw((mine[:rbig], other[:rbig]), wp, mp, vp, "adamw_sharded")
    big_out = [_unpack(o, [W[n].shape for n, _ in _SHARDED]) for o in outs_big]
    ws, ms, vs = (_pack([T[n] for n in _REPL]) for T in (W, M, V))
    outs_small = _adamw((gsmall,), ws, ms, vs, "adamw_replicated")
    small_out = [_unpack(o, repl_shapes) for o in outs_small]

    res = [dict(), dict(), dict(), dict()]
    for kind in range(4):
        res[kind]["w_in"] = w_in_out[kind]
        for k, (n, _) in enumerate(_SHARDED):
            res[kind][n] = big_out[kind][k]
        for k, n in enumerate(_REPL):
            res[kind][n] = small_out[kind][k]
    return (loss, grad_x, *[res[0][n] for n in _WEIGHTS], *[res[1][n] for n in _WEIGHTS],
            *[res[2][n] for n in _WEIGHTS], *[res[3][n] for n in _WEIGHTS])
```

```python
import functools

import jax
import jax.numpy as jnp
from jax import lax
from jax.experimental import pallas as pl
from jax.experimental.pallas import tpu as pltpu

f32 = jnp.float32
bf16 = jnp.bfloat16

D_MODEL = 1024
RMS_EPS = 1e-6
V7X_VMEM_LIMIT = 60 * 1024 * 1024
LANES = 128
NN, NT, TN = ((1,), (0,)), ((1,), (1,)), ((0,), (0,))

S5_STATES = 2048
S5_ROWS = 256
ATT_SEG = 2048
ATT_BLOCK = 128
SSD_CHUNK = 128
SSD_WIDTH = 768
SSD_XBC = 1280
CONV_ROWS = 512
TAIL_ROWS = 256

ADAM_LR, ADAM_B1, ADAM_B2, ADAM_EPS, ADAM_WD, ADAM_STEP = 0.001, 0.9, 0.999, 1e-08, 0.01, 10

_C_UA, _C_ZA, _C_Q, _C_K, _C_V, _C_ZB, _C_XBC, _C_DT, _C_ZC, _C_GATE, _C_END = (
    0, 512, 1024, 1792, 2560, 3328, 3584, 4864, 4876, 5644, 8716)

_SHARDED = (("s5_glu_w", 1), ("conv_w", 2), ("proj_a", 2), ("proj_b", 2), ("proj_c", 2), ("w_out", 1))
W_IN_SHARD = 2179
_REPL = ("norm_w", "s5_a_re", "s5_a_im", "s5_log_step", "s5_b_re", "s5_b_im", "s5_c_re", "s5_c_im", "s5_d",
         "s5_glu_b", "q_norm_w", "k_norm_w", "conv_b", "dt_bias", "ssd_a_log", "ssd_d", "ssd_norm_w")
_WEIGHTS = ("norm_w", "w_in", "s5_a_re", "s5_a_im", "s5_log_step", "s5_b_re", "s5_b_im", "s5_c_re", "s5_c_im",
            "s5_d", "s5_glu_w", "s5_glu_b", "q_norm_w", "k_norm_w", "conv_w", "conv_b", "dt_bias", "ssd_a_log",
            "ssd_d", "ssd_norm_w", "proj_a", "proj_b", "proj_c", "w_out")
PACK_ROWS = 512


def _dot(a, b, dims):
    return lax.dot_general(a.astype(bf16), b.astype(bf16), (dims, ((), ())), preferred_element_type=f32)


_ANY = pl.BlockSpec(memory_space=pl.ANY)

MAIN_WIDTH = 8448
MAIN_SSD_BLOCK = 3
MAIN_DT_BLOCK = 46
MAIN_ATT_BLOCK = 16


def _call(body, name, grid, in_specs, out_specs, out_shape, scratch=(), sem=None, aliases=None):
    return pl.pallas_call(
        body, name=name, grid=grid, in_specs=in_specs, out_specs=out_specs, out_shape=out_shape,
        scratch_shapes=list(scratch), input_output_aliases=aliases or {},
        compiler_params=pltpu.CompilerParams(dimension_semantics=sem, vmem_limit_bytes=V7X_VMEM_LIMIT))


def _tile(n, options=(1024, 768, 512, 384, 256, 128)):
    return next(t for t in options if n % t == 0)


@functools.partial(jax.custom_vjp, nondiff_argnums=(2,))
def _bdot(a, b, dims):
    return _dot(a, b, dims)


def _bdot_fwd(a, b, dims):
    return _dot(a, b, dims), (a, b)


def _bdot_bwd(dims, res, g):
    a, b = res
    if dims == NN:
        da, db = _dot(g, b, NT), _dot(a, g, TN)
    elif dims == NT:
        da, db = _dot(g, b, NN), _dot(g, a, TN)
    else:
        da, db = _dot(b, g, NT), _dot(a, g, NN)
    return da.astype(a.dtype), db.astype(b.dtype)


_bdot.defvjp(_bdot_fwd, _bdot_bwd)


@functools.partial(jax.custom_vjp, nondiff_argnums=(2,))
def _cdot(a, w, dims):
    return _dot(a, w, dims)


def _cdot_fwd(a, w, dims):
    return _dot(a, w, dims), w


def _cdot_bwd(dims, w, g):
    da = _dot(g, w, NT) if dims == NN else _dot(g, w, NN)
    return da, jnp.zeros_like(w)


_cdot.defvjp(_cdot_fwd, _cdot_bwd)


def _split3(x):
    hi = x.astype(bf16)
    r = x - hi.astype(f32)
    mid = r.astype(bf16)
    lo = (r - mid.astype(f32)).astype(bf16)
    return hi, mid, lo


@jax.custom_vjp
def _xdot_l(m, x):
    return sum(_dot(m, p, NN) for p in _split3(x))


def _xdot_l_fwd(m, x):
    return _xdot_l(m, x), m


def _xdot_l_bwd(m, g):
    return jnp.zeros_like(m), sum(_dot(m, p, TN) for p in _split3(g))


_xdot_l.defvjp(_xdot_l_fwd, _xdot_l_bwd)


@jax.custom_vjp
def _softplus(x):
    e = jnp.exp(-jnp.abs(x))
    u = 1.0 + e
    log1p = jnp.where(u == 1.0, e, jnp.log(u) * (e / jnp.where(u == 1.0, 1.0, u - 1.0)))
    return jnp.maximum(x, 0.0) + log1p


def _softplus_fwd(x):
    return _softplus(x), x


def _softplus_bwd(x, g):
    return (g * jax.nn.sigmoid(x),)


_softplus.defvjp(_softplus_fwd, _softplus_bwd)


def _rms(x, w):
    return x * lax.rsqrt(jnp.mean(x * x, axis=-1, keepdims=True) + RMS_EPS) * w


def mm_nn(a, b, name, tm=2048):
    M, K = a.shape
    N = b.shape[1]
    tn = _tile(N)

    def body(a_ref, b_ref, o_ref):
        o_ref[...] = _dot(a_ref[...], b_ref[...], NN)

    return _call(body, name, (M // tm, N // tn),
                 [pl.BlockSpec((tm, K), lambda i, j: (i, 0)), pl.BlockSpec((K, tn), lambda i, j: (0, j))],
                 pl.BlockSpec((tm, tn), lambda i, j: (i, j)), jax.ShapeDtypeStruct((M, N), f32),
                 sem=("parallel", "parallel"))(a, b)


def mm_nt(a, b, name, acc=None, tm=1024):
    M, K = a.shape
    N = b.shape[0]
    tk = _tile(K, (2816, 1024, 768, 512, 256, 128))
    has_acc = acc is not None

    def body(*refs):
        a_ref, b_ref = refs[0], refs[1]
        o_ref = refs[-1]
        k = pl.program_id(1)
        p = _dot(a_ref[...], b_ref[...], NT)

        @pl.when(k == 0)
        def _():
            o_ref[...] = p + refs[2][...] if has_acc else p

        @pl.when(k > 0)
        def _():
            o_ref[...] += p

    specs = [pl.BlockSpec((tm, tk), lambda i, k: (i, k)), pl.BlockSpec((N, tk), lambda i, k: (0, k))]
    args = [a, b]
    if has_acc:
        specs.append(pl.BlockSpec((tm, N), lambda i, k: (i, 0)))
        args.append(acc)
    return _call(body, name, (M // tm, K // tk), specs, pl.BlockSpec((tm, N), lambda i, k: (i, 0)),
                 jax.ShapeDtypeStruct((M, N), f32), sem=("parallel", "arbitrary"))(*args)


def mm_tn(a, b, name, tk=2048):
    K, M = a.shape
    N = b.shape[1]
    tn = _tile(N)

    def body(a_ref, b_ref, o_ref):
        k = pl.program_id(1)
        p = _dot(a_ref[...], b_ref[...], TN)

        @pl.when(k == 0)
        def _():
            o_ref[...] = p

        @pl.when(k > 0)
        def _():
            o_ref[...] += p

    return _call(body, name, (N // tn, K // tk),
                 [pl.BlockSpec((tk, M), lambda j, k: (k, 0)), pl.BlockSpec((tk, tn), lambda j, k: (k, j))],
                 pl.BlockSpec((M, tn), lambda j, k: (0, j)), jax.ShapeDtypeStruct((M, N), f32),
                 sem=("parallel", "arbitrary"))(a, b)


def rms_fwd(x, w, name, tm=512):
    S = x.shape[0]

    def body(x_ref, w_ref, o_ref):
        o_ref[...] = _rms(x_ref[...], w_ref[...]).astype(bf16)

    return _call(body, name, (S // tm,),
                 [pl.BlockSpec((tm, D_MODEL), lambda i: (i, 0)), pl.BlockSpec((1, D_MODEL), lambda i: (0, 0))],
                 pl.BlockSpec((tm, D_MODEL), lambda i: (i, 0)), jax.ShapeDtypeStruct((S, D_MODEL), bf16),
                 sem=("parallel",))(x, w)


def rms_bwd(x, w, dh, dres, name, tm=512):
    S = x.shape[0]

    def body(x_ref, w_ref, dh_ref, dr_ref, dx_ref, dw_ref):
        _, vjp = jax.vjp(_rms, x_ref[...], w_ref[...])
        dx, dw = vjp(dh_ref[...])
        dx_ref[...] = dx + dr_ref[...]

        @pl.when(pl.program_id(0) == 0)
        def _():
            dw_ref[...] = dw

        @pl.when(pl.program_id(0) > 0)
        def _():
            dw_ref[...] += dw

    row = pl.BlockSpec((tm, D_MODEL), lambda i: (i, 0))
    vec = pl.BlockSpec((1, D_MODEL), lambda i: (0, 0))
    return _call(body, name, (S // tm,), [row, vec, row, row], [row, vec],
                 [jax.ShapeDtypeStruct((S, D_MODEL), f32), jax.ShapeDtypeStruct((1, D_MODEL), f32)],
                 sem=("arbitrary",))(x, w, dh, dres)


def loss_and_grad(y, target, name, tm=512):
    S = y.shape[0]

    def body(y_ref, t_ref, dy_ref, l_ref):
        diff = y_ref[...] - t_ref[...]
        dy_ref[...] = diff * (1.0 / D_MODEL)
        part = jnp.full((8, LANES), 0.5 / D_MODEL * jnp.sum(diff * diff), f32)

        @pl.when(pl.program_id(0) == 0)
        def _():
            l_ref[...] = part

        @pl.when(pl.program_id(0) > 0)
        def _():
            l_ref[...] += part

    row = pl.BlockSpec((tm, D_MODEL), lambda i: (i, 0))
    return _call(body, name, (S // tm,), [row, row], [row, pl.BlockSpec((8, LANES), lambda i: (0, 0))],
                 [jax.ShapeDtypeStruct((S, D_MODEL), f32), jax.ShapeDtypeStruct((8, LANES), f32)],
                 sem=("arbitrary",))(y, target)


def _s5_discretize(a_re, a_im, log_step, b_re, b_im, c_re, c_im):
    step = jnp.exp(log_step)[:, None]
    mag = jnp.exp(a_re * step)
    ang = a_im * step
    lam_re, lam_im = mag * jnp.cos(ang), mag * jnp.sin(ang)
    num_re, num_im = lam_re - 1.0, lam_im
    den = a_re * a_re + a_im * a_im
    f_re = (num_re * a_re + num_im * a_im) / den
    f_im = (num_im * a_re - num_re * a_im) / den
    bb_re = f_re[..., None] * b_re - f_im[..., None] * b_im
    bb_im = f_re[..., None] * b_im + f_im[..., None] * b_re
    eye = jnp.eye(8, dtype=f32)

    def block_in(bb):
        t = bb.transpose(0, 2, 1).reshape(4, 8, 16, 1, 64)
        return (t * eye[None, :, None, :, None]).reshape(4, 128, 512)

    def block_out(c):
        t = c.transpose(0, 2, 1).reshape(4, 8, 64, 1, 16)
        return (t * eye[None, :, None, :, None]).reshape(4, 512, 128)

    return (lam_re.reshape(1, S5_STATES), lam_im.reshape(1, S5_STATES), block_in(bb_re), block_in(bb_im),
            block_out(c_re), block_out(c_im))


def _lam_powers(lam_re, lam_im):
    rows_re, rows_im = [lam_re], [lam_im]
    for _ in range(7):
        pr, pi = rows_re[-1], rows_im[-1]
        rows_re.append(pr * lam_re - pi * lam_im)
        rows_im.append(pr * lam_im + pi * lam_re)
    return jnp.concatenate(rows_re, 0), jnp.concatenate(rows_im, 0)


def s5_fwd(u, pw_re, pw_im, w_re, w_im, c_re, c_im, dvec, name):
    S = u.shape[0]
    R, NS = S5_ROWS, S5_STATES
    nb = R // 8

    def body(u_ref, pwr_ref, pwi_ref, wre_ref, wim_ref, cre_ref, cim_ref, d_ref, y_ref, hr_ref, hi_ref,
             car_re, car_im, cin_re, cin_im, up, yp):
        @pl.when(pl.program_id(0) == 0)
        def _():
            car_re[...] = jnp.zeros_like(car_re)
            car_im[...] = jnp.zeros_like(car_im)

        slab = lambda r: pl.ds(r * nb, nb)
        for r in range(8):
            up[slab(r), :] = u_ref[:, r, :]
        u = up[...]
        for j in range(4):
            uj = u[:, 128 * j:128 * (j + 1)]
            hr_ref[:, 512 * j:512 * (j + 1)] = _dot(uj, wre_ref[j], NN)
            hi_ref[:, 512 * j:512 * (j + 1)] = _dot(uj, wim_ref[j], NN)
        lr, li = pwr_ref[0:1, :], pwi_ref[0:1, :]
        for r in range(1, 8):
            pr, pi = hr_ref[slab(r - 1), :], hi_ref[slab(r - 1), :]
            hr_ref[slab(r), :] = lr * pr - li * pi + hr_ref[slab(r), :]
            hi_ref[slab(r), :] = lr * pi + li * pr + hi_ref[slab(r), :]
        l8r, l8i = pwr_ref[7:8, :], pwi_ref[7:8, :]

        def across(c, carry):
            gr, gi = carry
            cin_re[pl.ds(c, 1), :] = gr
            cin_im[pl.ds(c, 1), :] = gi
            er, ei = hr_ref[pl.ds(7 * nb + c, 1), :], hi_ref[pl.ds(7 * nb + c, 1), :]
            return l8r * gr - l8i * gi + er, l8r * gi + l8i * gr + ei

        gr, gi = lax.fori_loop(0, nb, across, (car_re[...], car_im[...]))
        car_re[...] = gr
        car_im[...] = gi
        cr, ci = cin_re[...], cin_im[...]
        for r in range(8):
            pr, pi = pwr_ref[r:r + 1, :], pwi_ref[r:r + 1, :]
            hr_ref[slab(r), :] = hr_ref[slab(r), :] + pr * cr - pi * ci
            hi_ref[slab(r), :] = hi_ref[slab(r), :] + pr * ci + pi * cr
        for j in range(4):
            sl = slice(512 * j, 512 * (j + 1))
            cs = slice(128 * j, 128 * (j + 1))
            yp[:, cs] = (_dot(hr_ref[:, sl], cre_ref[j], NN) - _dot(hi_ref[:, sl], cim_ref[j], NN)
                         + d_ref[:, cs] * u[:, cs])
        for r in range(8):
            y_ref[:, r, :] = yp[slab(r), :]

    full = lambda shape: pl.BlockSpec(shape, lambda i: (0,) * len(shape))
    hspec = pl.BlockSpec((R, NS), lambda i: (i, 0))
    uspec = pl.BlockSpec((nb, 8, 512), lambda i: (i, 0, 0))
    y, h_re, h_im = _call(
        body, name, (S // R,),
        [uspec, full((8, NS)), full((8, NS)), full((4, 128, 512)),
         full((4, 128, 512)), full((4, 512, 128)), full((4, 512, 128)), full((1, 512))],
        [uspec, hspec, hspec],
        [jax.ShapeDtypeStruct((S // 8, 8, 512), f32), jax.ShapeDtypeStruct((S, NS), f32),
         jax.ShapeDtypeStruct((S, NS), f32)],
        scratch=[pltpu.VMEM((1, NS), f32), pltpu.VMEM((1, NS), f32), pltpu.VMEM((nb, NS), f32),
                 pltpu.VMEM((nb, NS), f32), pltpu.VMEM((R, 512), f32), pltpu.VMEM((R, 512), f32)],
        sem=("arbitrary",))(u.reshape(S // 8, 8, 512), pw_re, pw_im, w_re.astype(bf16), w_im.astype(bf16),
                            c_re.astype(bf16), c_im.astype(bf16), dvec)
    return y.reshape(S, 512), h_re, h_im


def s5_bwd(dy, u, h_re, h_im, pw_re, pw_im, w_re, w_im, c_re, c_im, dvec, name):
    S = u.shape[0]
    R, NS = S5_ROWS, S5_STATES
    nb = R // 8
    nchunk = S // R

    def body(dy_ref, u_ref, hr_ref, hi_ref, hpr_ref, hpi_ref, pwr_ref, pwi_ref, wre_ref, wim_ref, cre_ref, cim_ref,
             d_ref, du_ref, dwre_ref, dwim_ref, dcre_ref, dcim_ref, dlr_ref, dli_ref, dd_ref,
             ar, ai, car_re, car_im, cin_re, cin_im, up, dyp, dup):
        i = pl.program_id(0)

        @pl.when(i == 0)
        def _():
            for ref in (car_re, car_im, dwre_ref, dwim_ref, dcre_ref, dcim_ref, dlr_ref, dli_ref, dd_ref):
                ref[...] = jnp.zeros_like(ref)

        slab = lambda r: pl.ds(r * nb, nb)
        for r in range(8):
            up[slab(r), :] = u_ref[:, r, :]
            dyp[slab(r), :] = dy_ref[:, r, :]
        dy = dyp[...]
        u = up[...]
        for j in range(4):
            dyj = dy[:, 128 * j:128 * (j + 1)]
            ar[:, 512 * j:512 * (j + 1)] = _dot(dyj, cre_ref[j], NT)
            ai[:, 512 * j:512 * (j + 1)] = -_dot(dyj, cim_ref[j], NT)
        lr, li = pwr_ref[0:1, :], pwi_ref[0:1, :]
        for r in range(6, -1, -1):
            nr, ni = ar[slab(r + 1), :], ai[slab(r + 1), :]
            ar[slab(r), :] = lr * nr + li * ni + ar[slab(r), :]
            ai[slab(r), :] = lr * ni - li * nr + ai[slab(r), :]
        l8r, l8i = pwr_ref[7:8, :], pwi_ref[7:8, :]

        def across(k, carry):
            c = nb - 1 - k
            gr, gi = carry
            cin_re[pl.ds(c, 1), :] = gr
            cin_im[pl.ds(c, 1), :] = gi
            er, ei = ar[pl.ds(c, 1), :], ai[pl.ds(c, 1), :]
            return l8r * gr + l8i * gi + er, l8r * gi - l8i * gr + ei

        gr, gi = lax.fori_loop(0, nb, across, (car_re[...], car_im[...]))
        car_re[...] = gr
        car_im[...] = gi
        cr, ci = cin_re[...], cin_im[...]
        for r in range(8):
            pr, pi = pwr_ref[7 - r:8 - r, :], pwi_ref[7 - r:8 - r, :]
            ar[slab(r), :] = ar[slab(r), :] + pr * cr + pi * ci
            ai[slab(r), :] = ai[slab(r), :] + pr * ci - pi * cr

        acc_r = jnp.zeros((1, NS), f32)
        acc_i = jnp.zeros((1, NS), f32)
        has_prev = (i < nchunk - 1).astype(f32)
        top = lax.broadcasted_iota(jnp.int32, (nb, NS), 0) == 0
        for r in range(8):
            if r == 0:
                xr = jnp.where(top, hpr_ref[7:8, :] * has_prev, pltpu.roll(hr_ref[slab(7), :], 1, 0))
                xi = jnp.where(top, hpi_ref[7:8, :] * has_prev, pltpu.roll(hi_ref[slab(7), :], 1, 0))
            else:
                xr, xi = hr_ref[slab(r - 1), :], hi_ref[slab(r - 1), :]
            br, bi = ar[slab(r), :], ai[slab(r), :]
            acc_r += jnp.sum(br * xr + bi * xi, axis=0, keepdims=True)
            acc_i += jnp.sum(bi * xr - br * xi, axis=0, keepdims=True)
        dlr_ref[...] += acc_r
        dli_ref[...] += acc_i
        dd_ref[...] += jnp.sum(dy * u, axis=0, keepdims=True)

        for j in range(4):
            sl = slice(512 * j, 512 * (j + 1))
            cs = slice(128 * j, 128 * (j + 1))
            arj, aij = ar[:, sl], ai[:, sl]
            uj, dyj = u[:, cs], dy[:, cs]
            dup[:, cs] = _dot(arj, wre_ref[j], NT) + _dot(aij, wim_ref[j], NT) + d_ref[:, cs] * dyj
            dwre_ref[j] += _dot(uj, arj, TN)
            dwim_ref[j] += _dot(uj, aij, TN)
            dcre_ref[j] += _dot(hr_ref[:, sl], dyj, TN)
            dcim_ref[j] -= _dot(hi_ref[:, sl], dyj, TN)
        for r in range(8):
            du_ref[:, r, :] = dup[slab(r), :]

    rev = lambda i: nchunk - 1 - i
    full = lambda shape: pl.BlockSpec(shape, lambda i: (0,) * len(shape))
    row = pl.BlockSpec((nb, 8, 512), lambda i: (rev(i), 0, 0))
    hspec = pl.BlockSpec((R, NS), lambda i: (rev(i), 0))
    hprev = pl.BlockSpec((8, NS), lambda i: (jnp.maximum(rev(i) * nb - 1, 0), 0))
    outs = _call(
        body, name, (nchunk,),
        [row, row, hspec, hspec, hprev, hprev, full((8, NS)), full((8, NS)), full((4, 128, 512)), full((4, 128, 512)),
         full((4, 512, 128)), full((4, 512, 128)), full((1, 512))],
        [row, full((4, 128, 512)), full((4, 128, 512)), full((4, 512, 128)), full((4, 512, 128)),
         full((1, NS)), full((1, NS)), full((1, 512))],
        [jax.ShapeDtypeStruct((S // 8, 8, 512), f32), jax.ShapeDtypeStruct((4, 128, 512), f32),
         jax.ShapeDtypeStruct((4, 128, 512), f32), jax.ShapeDtypeStruct((4, 512, 128), f32),
         jax.ShapeDtypeStruct((4, 512, 128), f32), jax.ShapeDtypeStruct((1, NS), f32),
         jax.ShapeDtypeStruct((1, NS), f32), jax.ShapeDtypeStruct((1, 512), f32)],
        scratch=[pltpu.VMEM((R, NS), f32), pltpu.VMEM((R, NS), f32), pltpu.VMEM((1, NS), f32),
                 pltpu.VMEM((1, NS), f32), pltpu.VMEM((nb, NS), f32), pltpu.VMEM((nb, NS), f32),
                 pltpu.VMEM((R, 512), f32), pltpu.VMEM((R, 512), f32), pltpu.VMEM((R, 512), f32)],
        sem=("arbitrary",))(dy.reshape(S // 8, 8, 512), u.reshape(S // 8, 8, 512), h_re, h_im, h_re, h_im, pw_re,
                            pw_im, w_re.astype(bf16), w_im.astype(bf16), c_re.astype(bf16), c_im.astype(bf16), dvec)
    return (outs[0].reshape(S, 512),) + tuple(outs[1:])


def _rows(start, n, d):
    return pl.ds(pl.multiple_of(start, ATT_BLOCK), n) if d == 1 else pl.ds(start, n, stride=d)


def _head_masks():
    lane = lax.broadcasted_iota(jnp.int32, (1, LANES), 1)
    return [(lane < 64).astype(f32), (lane >= 64).astype(f32)]


def _head_norm(x, w, hm):
    x2 = x * x
    r = [lax.rsqrt(jnp.sum(x2 * hm[h], axis=-1, keepdims=True) * (1.0 / 64) + RMS_EPS) for h in range(2)]
    sc = hm[0] * r[0] + hm[1] * r[1]
    return x * sc * w, sc, r


def _head_norm_bwd(x, w, sc, r, dxn, hm):
    dw = jnp.sum(dxn * x * sc, axis=0, keepdims=True)
    t = dxn * w
    tx = t * x
    corr = sum(hm[h] * (r[h] * r[h] * r[h]) * jnp.sum(tx * hm[h], axis=-1, keepdims=True) for h in range(2))
    return t * sc - x * corr * (1.0 / 64), dw


def _att_mask(has_prev):
    qi = lax.broadcasted_iota(jnp.int32, (ATT_BLOCK, 2 * ATT_BLOCK), 0) + ATT_BLOCK
    kj = lax.broadcasted_iota(jnp.int32, (ATT_BLOCK, 2 * ATT_BLOCK), 1)
    return (qi - kj >= 0) & (qi - kj <= ATT_BLOCK) & (has_prev | (kj >= ATT_BLOCK))


def _att_block_bwd(q, k, v, o, lse, do, dlse, qw, kw, has_prev):
    hm = _head_masks()
    mask = _att_mask(has_prev)
    qn, qsc, qr = _head_norm(q, qw, hm)
    kn, ksc, kr = _head_norm(k, kw, hm)
    dqn = jnp.zeros((ATT_BLOCK, LANES), f32)
    dkn = jnp.zeros((2 * ATT_BLOCK, LANES), f32)
    dv = jnp.zeros((2 * ATT_BLOCK, LANES), f32)
    for h in range(2):
        qh, do_h = qn * hm[h], do * hm[h]
        s = _dot(qh, kn, NT) * 0.125
        p = jnp.exp(jnp.where(mask, s - lse[:, 64 * h:64 * h + 1], -jnp.inf))
        dp = _dot(do_h, v, NT)
        delta = jnp.sum(do_h * o, axis=-1, keepdims=True)
        dl = jnp.sum(dlse * hm[h], axis=-1, keepdims=True)
        ds = p * (dp - delta + dl) * 0.125
        dqn = dqn + hm[h] * _dot(ds, kn, NN)
        dkn = dkn + _dot(ds, qh, TN)
        dv = dv + _dot(p, do_h, TN)
    dq, dqw = _head_norm_bwd(q, qw, qsc, qr, dqn, hm)
    dk, dkw = _head_norm_bwd(k, kw, ksc, kr, dkn, hm)
    return dq, dk, dv, dqw, dkw


def _att_block(q, k, v, qw, kw, has_prev):
    hm = _head_masks()
    qn, kn = _head_norm(q, qw, hm)[0], _head_norm(k, kw, hm)[0]
    mask = _att_mask(has_prev)
    o = jnp.zeros((ATT_BLOCK, LANES), f32)
    lse = jnp.zeros((ATT_BLOCK, LANES), f32)
    for h in range(2):
        s = _bdot(qn * hm[h], kn, NT) * 0.125
        s = jnp.where(mask, s, -jnp.inf)
        m = jnp.max(s, axis=-1, keepdims=True)
        p = jnp.exp(s - m)
        l = jnp.sum(p, axis=-1, keepdims=True)
        o = o + hm[h] * _bdot(p / l, v, NN)
        lse = lse + hm[h] * (m + jnp.log(l))
    return o, lse


def att_fwd(p_att, qw, kw, d, g, name):
    S = p_att.shape[0]
    SEG = ATT_SEG
    nblk = SEG // ATT_BLOCK

    def body(p_ref, qw_ref, kw_ref, o_ref, l_ref, q_s, k_ext, v_ext, o_s, l_s):
        seg = pl.program_id(1)

        @pl.when(seg == 0)
        def _():
            k_ext[SEG:, :] = jnp.zeros((SEG, LANES), f32)
            v_ext[SEG:, :] = jnp.zeros((SEG, LANES), f32)

        k_ext[:SEG, :] = k_ext[SEG:, :]
        v_ext[:SEG, :] = v_ext[SEG:, :]
        q_s[...] = p_ref[:, 0:128]
        k_ext[SEG:, :] = p_ref[:, 128:256]
        v_ext[SEG:, :] = p_ref[:, 256:384]
        qw_v, kw_v = qw_ref[...], kw_ref[...]

        def blk(b, carry):
            j, r = b // d, b % d
            qs = j * (ATT_BLOCK * d) + r
            ks = SEG + qs - ATT_BLOCK * d
            o, lse = _att_block(q_s[_rows(qs, ATT_BLOCK, d), :], k_ext[_rows(ks, 2 * ATT_BLOCK, d), :],
                                v_ext[_rows(ks, 2 * ATT_BLOCK, d), :], qw_v, kw_v, (seg > 0) | (j > 0))
            o_s[_rows(qs, ATT_BLOCK, d), :] = o
            l_s[_rows(qs, ATT_BLOCK, d), :] = lse
            return carry

        lax.fori_loop(0, nblk, blk, 0, unroll=4)
        o_ref[...] = o_s[...]
        l_ref[...] = l_s[...]

    vec = pl.BlockSpec((1, LANES), lambda hh, s: (0, 0))
    out = pl.BlockSpec((SEG, LANES), lambda hh, s: (s, hh))
    return _call(body, name, (2, S // SEG), [pl.BlockSpec((SEG, 384), lambda hh, s: (s, MAIN_ATT_BLOCK + 2 * g + hh)), vec, vec],
                 [out, out], [jax.ShapeDtypeStruct((S, 256), f32), jax.ShapeDtypeStruct((S, 256), f32)],
                 scratch=[pltpu.VMEM((SEG, LANES), f32), pltpu.VMEM((2 * SEG, LANES), f32),
                          pltpu.VMEM((2 * SEG, LANES), f32), pltpu.VMEM((SEG, LANES), f32),
                          pltpu.VMEM((SEG, LANES), f32)],
                 sem=("arbitrary", "arbitrary"))(p_att, qw, kw)


def att_bwd(p_att, o, lse, do, dlse, qw, kw, d, g, dp_main, name):
    S = p_att.shape[0]
    SEG = ATT_SEG
    nseg = S // SEG
    nblk = SEG // ATT_BLOCK

    def body(p_ref, pp_ref, o_ref, l_ref, do_ref, dl_ref, qw_ref, kw_ref, _, dp_ref, dqw_ref, dkw_ref,
             q_s, k_ext, v_ext, dq_s, dk_ext, dv_ext):
        hh, i = pl.program_id(0), pl.program_id(1)
        seg = nseg - 1 - i

        @pl.when(i == 0)
        def _():
            dk_ext[...] = jnp.zeros_like(dk_ext)
            dv_ext[...] = jnp.zeros_like(dv_ext)

        @pl.when((i == 0) & (hh == 0))
        def _():
            dqw_ref[...] = jnp.zeros_like(dqw_ref)
            dkw_ref[...] = jnp.zeros_like(dkw_ref)

        dk_ext[SEG:, :] = dk_ext[:SEG, :]
        dv_ext[SEG:, :] = dv_ext[:SEG, :]
        dk_ext[:SEG, :] = jnp.zeros((SEG, LANES), f32)
        dv_ext[:SEG, :] = jnp.zeros((SEG, LANES), f32)
        q_s[...] = p_ref[:, 0:128]
        k_ext[SEG:, :] = p_ref[:, 128:256]
        v_ext[SEG:, :] = p_ref[:, 256:384]
        k_ext[:SEG, :] = pp_ref[:, 128:256]
        v_ext[:SEG, :] = pp_ref[:, 256:384]
        qw_v, kw_v = qw_ref[...], kw_ref[...]

        def blk_pair(i2, carry):
            dqw, dkw = carry
            done = []
            for u in range(2):
                b = 2 * i2 + u
                j, r = b // d, b % d
                qs = j * (ATT_BLOCK * d) + r
                ks = SEG + qs - ATT_BLOCK * d
                has_prev = (seg > 0) | (j > 0)
                qrows, krows = _rows(qs, ATT_BLOCK, d), _rows(ks, 2 * ATT_BLOCK, d)
                dq, dk, dv, dqw_b, dkw_b = _att_block_bwd(
                    q_s[qrows, :], k_ext[krows, :], v_ext[krows, :], o_ref[qrows, :], l_ref[qrows, :],
                    do_ref[qrows, :], dl_ref[qrows, :], qw_v, kw_v, has_prev)
                dqw, dkw = dqw + dqw_b, dkw + dkw_b
                done.append((qrows, krows, dq, dk, dv))
            for qrows, krows, dq, dk, dv in done:
                dq_s[qrows, :] = dq
                dk_ext[krows, :] = dk_ext[krows, :] + dk
                dv_ext[krows, :] = dv_ext[krows, :] + dv
            return dqw, dkw

        zero = jnp.zeros((1, LANES), f32)
        dqw, dkw = lax.fori_loop(0, nblk // 2, blk_pair, (zero, zero))
        dqw_ref[...] += dqw
        dkw_ref[...] += dkw
        dp_ref[:, 0:128] = dq_s[...].astype(bf16)
        dp_ref[:, 128:256] = dk_ext[SEG:, :].astype(bf16)
        dp_ref[:, 256:384] = dv_ext[SEG:, :].astype(bf16)

    rev = lambda i: nseg - 1 - i
    vec = pl.BlockSpec((1, LANES), lambda hh, i: (0, 0))
    blk = MAIN_ATT_BLOCK + 2 * g
    cur = pl.BlockSpec((SEG, 384), lambda hh, i: (rev(i), blk + hh))
    prev = pl.BlockSpec((SEG, 384), lambda hh, i: (jnp.maximum(rev(i) - 1, 0), blk + hh))
    col = pl.BlockSpec((SEG, LANES), lambda hh, i: (rev(i), hh))
    big = pltpu.VMEM((2 * SEG, LANES), f32)
    one = pltpu.VMEM((SEG, LANES), f32)
    return _call(body, name, (2, nseg), [cur, prev, col, col, col, col, vec, vec, _ANY], [cur, vec, vec],
                 [jax.ShapeDtypeStruct((S, MAIN_WIDTH), bf16), jax.ShapeDtypeStruct((1, LANES), f32),
                  jax.ShapeDtypeStruct((1, LANES), f32)],
                 scratch=[one, big, big, one, big, big], sem=("arbitrary", "arbitrary"),
                 aliases={8: 0})(p_att, p_att, o, lse, do, dlse, qw, kw, dp_main)


def conv_fwd(p_ssd, conv_w, conv_b, name):
    S = p_ssd.shape[0]
    tm, C = CONV_ROWS, SSD_XBC

    def body(x_ref, xp_ref, w_ref, b_ref, o_ref):
        first = (pl.program_id(0) == 0)
        ext = jnp.concatenate([jnp.where(first, 0.0, xp_ref[:, 0:C]), x_ref[:, 0:C]], axis=0)
        acc = b_ref[...] + w_ref[3:4, :] * ext[8:, :]
        for k in range(1, 4):
            acc = acc + w_ref[3 - k:4 - k, :] * pltpu.roll(ext, k, 0)[8:, :]
        o_ref[...] = jax.nn.silu(acc)

    return _call(body, name, (S // tm,),
                 [pl.BlockSpec((tm, 1536), lambda i: (i, MAIN_SSD_BLOCK)),
                  pl.BlockSpec((8, 1536), lambda i: (jnp.maximum(i * (tm // 8) - 1, 0), MAIN_SSD_BLOCK)),
                  pl.BlockSpec((4, C), lambda i: (0, 0)), pl.BlockSpec((1, C), lambda i: (0, 0))],
                 pl.BlockSpec((tm, C), lambda i: (i, 0)), jax.ShapeDtypeStruct((S, C), f32),
                 sem=("parallel",))(p_ssd, p_ssd, conv_w, conv_b)


def conv_bwd(p_ssd, dact, ddt, conv_w, conv_b, dp_main, name):
    S = p_ssd.shape[0]
    tm, C = CONV_ROWS, SSD_XBC
    nblk = S // tm

    def body(x_ref, xp_ref, xn_ref, da_ref, dan_ref, ddt_ref, w_ref, b_ref, _, dp_ref, dw_ref, db_ref):
        i = pl.program_id(0)
        rows = tm + 8
        ext = jnp.concatenate([jnp.where(i == 0, 0.0, xp_ref[:, 0:C]), x_ref[:, 0:C], xn_ref[:, 0:C]], axis=0)
        shifted = [ext[8:, :]] + [pltpu.roll(ext, k, 0)[8:, :] for k in range(1, 4)]
        pre = b_ref[...] + w_ref[3:4, :] * shifted[0]
        for k in range(1, 4):
            pre = pre + w_ref[3 - k:4 - k, :] * shifted[k]
        sg = jax.nn.sigmoid(pre)
        dact = jnp.concatenate([da_ref[...], jnp.where(i == nblk - 1, 0.0, dan_ref[...])], axis=0)
        dpre = dact * (sg * (1.0 + pre * (1.0 - sg)))
        dx = w_ref[3:4, :] * dpre[0:tm, :]
        for k in range(1, 4):
            dx = dx + w_ref[3 - k:4 - k, :] * pltpu.roll(dpre, rows - k, 0)[0:tm, :]
        dp_ref[:, 0:C] = dx.astype(bf16)
        dp_ref[:, C:C + 128] = ddt_ref[...].astype(bf16)
        dp_ref[:, C + 128:] = jnp.zeros((tm, 128), bf16)
        dcur = dpre[0:tm, :]
        dws = [jnp.sum(dcur * shifted[3 - j][0:tm, :], axis=0, keepdims=True) for j in range(4)]
        dbs = jnp.sum(dcur, axis=0, keepdims=True)

        @pl.when(i == 0)
        def _():
            dw_ref[...] = jnp.zeros_like(dw_ref)
            db_ref[...] = jnp.zeros_like(db_ref)

        for j in range(4):
            dw_ref[j:j + 1, :] += dws[j]
        db_ref[...] += dbs

    t8 = tm // 8
    blk = MAIN_SSD_BLOCK
    return _call(body, name, (nblk,),
                 [pl.BlockSpec((tm, 1536), lambda i: (i, blk)),
                  pl.BlockSpec((8, 1536), lambda i: (jnp.maximum(i * t8 - 1, 0), blk)),
                  pl.BlockSpec((8, 1536), lambda i: (jnp.minimum((i + 1) * t8, S // 8 - 1), blk)),
                  pl.BlockSpec((tm, C), lambda i: (i, 0)),
                  pl.BlockSpec((8, C), lambda i: (jnp.minimum((i + 1) * t8, S // 8 - 1), 0)),
                  pl.BlockSpec((tm, 128), lambda i: (i, 0)),
                  pl.BlockSpec((4, C), lambda i: (0, 0)), pl.BlockSpec((1, C), lambda i: (0, 0)), _ANY],
                 [pl.BlockSpec((tm, 1536), lambda i: (i, blk)), pl.BlockSpec((4, C), lambda i: (0, 0)),
                  pl.BlockSpec((1, C), lambda i: (0, 0))],
                 [jax.ShapeDtypeStruct((S, MAIN_WIDTH), bf16), jax.ShapeDtypeStruct((4, C), f32),
                  jax.ShapeDtypeStruct((1, C), f32)],
                 sem=("arbitrary",), aliases={8: 0})(p_ssd, p_ssd, p_ssd, dact, dact, ddt, conv_w, conv_b, dp_main)


def _ssd_chunk(xbc, dtr, state, dt_bias, a_log, d_full):
    T = SSD_CHUNK
    r_i = lax.broadcasted_iota(jnp.int32, (T, T), 0)
    c_i = lax.broadcasted_iota(jnp.int32, (T, T), 1)
    tril = c_i <= r_i
    tri = tril.astype(bf16)
    lane = lax.broadcasted_iota(jnp.int32, (1, LANES), 1)
    hm = [(lane < 64).astype(f32), (lane >= 64).astype(f32)]
    column = lambda v, h: jnp.broadcast_to(v[:, h:h + 1], (T, LANES))

    def per_head_lanes(v):
        return jnp.concatenate([jnp.where(lane < 64, column(v, 2 * pp), column(v, 2 * pp + 1)) for pp in range(6)],
                               axis=1)

    xs, bm, cm = xbc[:, :768], xbc[:, 768:1024], xbc[:, 1024:1280]
    dt = _softplus(dtr + dt_bias)
    a_dt = dt * (-jnp.exp(a_log))
    a_cs = _xdot_l(tri, a_dt)
    dt_full = per_head_lanes(dt)
    acs_full = per_head_lanes(a_cs)
    last = lax.broadcasted_iota(jnp.int32, (T, SSD_WIDTH), 0) == T - 1
    tot_full = jnp.sum(jnp.where(last, acs_full, 0.0), axis=0, keepdims=True)
    xdt = xs * dt_full
    xw = xdt * jnp.exp(tot_full - acs_full)
    eacs = jnp.exp(acs_full)
    st_parts, off_parts, diag_parts = [], [], []
    for g in range(2):
        bg, cg = bm[:, 128 * g:128 * (g + 1)], cm[:, 128 * g:128 * (g + 1)]
        cols = slice(384 * g, 384 * (g + 1))
        st_parts.append(_bdot(bg, xw[:, cols], TN))
        off_parts.append(_bdot(cg, state[:, cols], NN))
        cb = _bdot(cg, bg, NT)
        for pp in range(3 * g, 3 * g + 3):
            xp = xdt[:, 128 * pp:128 * (pp + 1)]
            acc = jnp.zeros((T, LANES), f32)
            for hh in range(2):
                a_col = column(a_cs, 2 * pp + hh)
                decay = jnp.where(tril, jnp.exp(jnp.minimum(a_col - a_col.T, 0.0)), 0.0)
                acc = acc + _bdot(cb * decay, xp * hm[hh], NN)
            diag_parts.append(acc)
    new_state = state * jnp.exp(tot_full) + jnp.concatenate(st_parts, axis=1)
    y = jnp.concatenate(diag_parts, axis=1) + jnp.concatenate(off_parts, axis=1) * eacs + xs * d_full
    return y, new_state


def ssd_fwd(xact, p_ssd, dt_bias, a_log, d_full, name):
    S = xact.shape[0]
    T = SSD_CHUNK

    def body(x_ref, p_ref, b_ref, a_ref, d_ref, y_ref, s_ref, state):
        @pl.when(pl.program_id(0) == 0)
        def _():
            state[...] = jnp.zeros_like(state)

        st = state[...]
        s_ref[0] = st
        y, new = _ssd_chunk(x_ref[...], p_ref[...], st, b_ref[...], a_ref[...], d_ref[...])
        y_ref[...] = y
        state[...] = new

    vec = lambda n: pl.BlockSpec((1, n), lambda i: (0, 0))
    return _call(body, name, (S // T,),
                 [pl.BlockSpec((T, SSD_XBC), lambda i: (i, 0)), pl.BlockSpec((T, 128), lambda i: (i, MAIN_DT_BLOCK)),
                  vec(128), vec(128), vec(768)],
                 [pl.BlockSpec((T, 768), lambda i: (i, 0)), pl.BlockSpec((1, T, 768), lambda i: (i, 0, 0))],
                 [jax.ShapeDtypeStruct((S, 768), f32), jax.ShapeDtypeStruct((S // T, T, 768), f32)],
                 scratch=[pltpu.VMEM((T, 768), f32)], sem=("arbitrary",))(xact, p_ssd, dt_bias, a_log, d_full)


def ssd_bwd(xact, p_ssd, states, dy, dt_bias, a_log, d_full, name):
    S = xact.shape[0]
    T = SSD_CHUNK
    nc = S // T

    def body(x_ref, p_ref, s_ref, dy_ref, b_ref, a_ref, d_ref, dx_ref, ddt_ref, db_ref, da_ref, dd_ref, dstate):
        i = pl.program_id(0)

        @pl.when(i == 0)
        def _():
            for ref in (dstate, db_ref, da_ref, dd_ref):
                ref[...] = jnp.zeros_like(ref)

        _, vjp = jax.vjp(_ssd_chunk, x_ref[...], p_ref[...], s_ref[0], b_ref[...], a_ref[...], d_ref[...])
        dx, ddt, dst, db, da, dd = vjp((dy_ref[...], dstate[...]))
        dx_ref[...] = dx
        ddt_ref[...] = ddt
        dstate[...] = dst
        db_ref[...] += db
        da_ref[...] += da
        dd_ref[...] += dd

    rev = lambda i: nc - 1 - i
    vec = lambda n: pl.BlockSpec((1, n), lambda i: (0, 0))
    return _call(body, name, (nc,),
                 [pl.BlockSpec((T, SSD_XBC), lambda i: (rev(i), 0)), pl.BlockSpec((T, 128), lambda i: (rev(i), MAIN_DT_BLOCK)),
                  pl.BlockSpec((1, T, 768), lambda i: (rev(i), 0, 0)), pl.BlockSpec((T, 768), lambda i: (rev(i), 0)),
                  vec(128), vec(128), vec(768)],
                 [pl.BlockSpec((T, SSD_XBC), lambda i: (rev(i), 0)), pl.BlockSpec((T, 128), lambda i: (rev(i), 0)),
                  vec(128), vec(128), vec(768)],
                 [jax.ShapeDtypeStruct((S, SSD_XBC), f32), jax.ShapeDtypeStruct((S, 128), f32),
                  jax.ShapeDtypeStruct((1, 128), f32), jax.ShapeDtypeStruct((1, 128), f32),
                  jax.ShapeDtypeStruct((1, 768), f32)],
                 scratch=[pltpu.VMEM((T, 768), f32)],
                 sem=("arbitrary",))(xact, p_ssd, states, dy, dt_bias, a_log, d_full)


def _tail_fn(ys5, pt, o0, o1, o2, l0, l1, l2, yssd, glu_b, nw, pr_glu, pr_a, pr_b, pr_c, x, weights):
    glu_w, pa, pb, pc, wo = weights
    gates = jax.nn.sigmoid(pt[:, :3072])
    za, zb, zc = pt[:, 3072:3584], pt[:, 3584:3840], pt[:, 3840:4608]
    g = jax.nn.gelu(ys5)
    ya = g * jax.nn.sigmoid(_cdot(g, glu_w, NN) + glu_b + pr_glu) * jax.nn.silu(za)
    m = jnp.maximum(jnp.maximum(l0, l1), l2)
    e0, e1, e2 = jnp.exp(l0 - m), jnp.exp(l1 - m), jnp.exp(l2 - m)
    yb = (e0 * o0 + e1 * o1 + e2 * o2) / (e0 + e1 + e2) * jax.nn.silu(zb)
    yc = _rms(yssd * jax.nn.silu(zc), nw)
    merged = (gates[:, :1024] * (_cdot(ya, pa, NN) + pr_a) + gates[:, 1024:2048] * (_cdot(yb, pb, NN) + pr_b)
              + gates[:, 2048:] * (_cdot(yc, pc, NN) + pr_c))
    out = x + _cdot(merged, wo, NN)
    return out, (g, ya, yb, yc, merged)


def _tail_specs(tm):
    row = lambda n: pl.BlockSpec((tm, n), lambda i: (i, 0))
    full = lambda a, b: pl.BlockSpec((a, b), lambda i: (0, 0))
    acts = [row(512), row(4608)] + [row(256)] * 6 + [row(768), row(D_MODEL)]
    consts = [full(1, 512), full(1, 768), full(512, 512), full(512, D_MODEL), full(256, D_MODEL),
              full(768, D_MODEL), full(D_MODEL, D_MODEL)]
    return row, full, acts, consts


def tail_fwd(ys5, pt, os_, ls_, yssd, x, glu_b, nw, weights, name):
    S = x.shape[0]
    tm = TAIL_ROWS
    row, full, acts, consts = _tail_specs(tm)

    def body(ys5_ref, pt_ref, o0, o1, o2, l0, l1, l2, yssd_ref, x_ref, gb_ref, nw_ref, gw, pa, pb, pc, wo, out_ref):
        z = lambda n: jnp.zeros((tm, n), f32)
        out, _ = _tail_fn(ys5_ref[...], pt_ref[...], o0[...], o1[...], o2[...], l0[...], l1[...], l2[...],
                          yssd_ref[...], gb_ref[...], nw_ref[...], z(512), z(D_MODEL), z(D_MODEL), z(D_MODEL),
                          x_ref[...], (gw[...], pa[...], pb[...], pc[...], wo[...]))
        out_ref[...] = out

    return _call(body, name, (S // tm,), acts + consts, row(D_MODEL), jax.ShapeDtypeStruct((S, D_MODEL), f32),
                 sem=("parallel",))(ys5, pt, *os_, *ls_, yssd, x, glu_b, nw, *weights)


def tail_bwd(ys5, pt, os_, ls_, yssd, dout, glu_b, nw, weights, name):
    S = dout.shape[0]
    tm = TAIL_ROWS
    row, full, acts, consts = _tail_specs(tm)

    def body(ys5_ref, pt_ref, o0, o1, o2, l0, l1, l2, yssd_ref, dout_ref, gb_ref, nw_ref, gw, pa, pb, pc, wo,
             dys5_ref, dpt_ref, do0, do1, do2, dl0, dl1, dl2, dyssd_ref, dgb_ref, dnw_ref,
             g_ref, ya_ref, yb_ref, yc_ref, mg_ref, dglu_ref, dpa_ref, dpb_ref, dpc_ref):
        z = lambda n: jnp.zeros((tm, n), f32)
        w = (gw[...], pa[...], pb[...], pc[...], wo[...])
        fn = lambda *a: _tail_fn(*a, z(D_MODEL), w)
        _, vjp, aux = jax.vjp(fn, ys5_ref[...], pt_ref[...], o0[...], o1[...], o2[...], l0[...], l1[...], l2[...],
                              yssd_ref[...], gb_ref[...], nw_ref[...], z(512), z(D_MODEL), z(D_MODEL), z(D_MODEL),
                              has_aux=True)
        (dys5, dpt, d0, d1, d2, e0, e1, e2, dyssd, dgb, dnw, dglu, dpa, dpb, dpc) = vjp(dout_ref[...])
        dys5_ref[...] = dys5
        dpt_ref[...] = dpt.astype(bf16)
        for ref, val in ((do0, d0), (do1, d1), (do2, d2), (dl0, e0), (dl1, e1), (dl2, e2)):
            ref[...] = val
        dyssd_ref[...] = dyssd
        g, ya, yb, yc, merged = aux
        for ref, val in ((g_ref, g), (ya_ref, ya), (yb_ref, yb), (yc_ref, yc), (mg_ref, merged),
                         (dglu_ref, dglu), (dpa_ref, dpa), (dpb_ref, dpb), (dpc_ref, dpc)):
            ref[...] = val.astype(bf16)

        @pl.when(pl.program_id(0) == 0)
        def _():
            dgb_ref[...] = dgb
            dnw_ref[...] = dnw

        @pl.when(pl.program_id(0) > 0)
        def _():
            dgb_ref[...] += dgb
            dnw_ref[...] += dnw

    sd = lambda n, dt=f32: jax.ShapeDtypeStruct((S, n), dt)
    out_specs = ([row(512), row(4608)] + [row(256)] * 6 + [row(768), full(1, 512), full(1, 768)]
                 + [row(512), row(512), row(256), row(768), row(D_MODEL), row(512)] + [row(D_MODEL)] * 3)
    out_shape = ([sd(512), sd(MAIN_WIDTH, bf16)] + [sd(256)] * 6 + [sd(768), jax.ShapeDtypeStruct((1, 512), f32),
                                                          jax.ShapeDtypeStruct((1, 768), f32)]
                 + [sd(512, bf16), sd(512, bf16), sd(256, bf16), sd(768, bf16), sd(D_MODEL, bf16), sd(512, bf16)]
                 + [sd(D_MODEL, bf16)] * 3)
    return _call(body, name, (S // tm,), acts + consts, out_specs, out_shape,
                 sem=("arbitrary",))(ys5, pt, *os_, *ls_, yssd, dout, glu_b, nw, *weights)


def _in_proj_segments(shards):
    dtype = shards[0].dtype

    def c(a, b):
        parts = []
        for k, sh in enumerate(shards):
            lo, hi = max(a, W_IN_SHARD * k), min(b, W_IN_SHARD * (k + 1))
            if lo < hi:
                parts.append(sh[:, lo - W_IN_SHARD * k:hi - W_IN_SHARD * k])
        return parts[0] if len(parts) == 1 else jnp.concatenate(parts, axis=1)

    atts = []
    for g in range(3):
        parts = []
        for hh in range(2):
            o = 64 * (4 * g + 2 * hh)
            parts += [c(_C_Q + o, _C_Q + o + 128), c(_C_K + o, _C_K + o + 128), c(_C_V + o, _C_V + o + 128)]
        atts.append(jnp.concatenate(parts, axis=1))
    ssd = jnp.concatenate([c(_C_XBC, _C_ZC), jnp.zeros((D_MODEL, 1536 - (_C_ZC - _C_XBC)), dtype)], axis=1)
    tail = jnp.concatenate([c(_C_GATE, _C_END), c(_C_ZA, _C_Q), c(_C_ZB, _C_XBC), c(_C_ZC, _C_GATE)], axis=1)
    return [c(_C_UA, _C_ZA), jnp.concatenate([tail, ssd] + atts, axis=1)]


def _in_proj_grad(ds5, dmain):
    dtail, dssd = dmain[:, :4608], dmain[:, 4608:6144]
    datts = [dmain[:, 6144 + 768 * g:6144 + 768 * (g + 1)] for g in range(3)]
    pick = lambda off: [datts[g][:, 384 * hh + off:384 * hh + off + 128] for g in range(3) for hh in range(2)]
    pieces = ([ds5, dtail[:, 3072:3584]] + pick(0) + pick(128) + pick(256)
              + [dtail[:, 3584:3840], dssd[:, :_C_ZC - _C_XBC], dtail[:, 3840:4608], dtail[:, :3072]])
    shards, start = [[] for _ in range(4)], 0
    for piece in pieces:
        width = piece.shape[1]
        for k in range(4):
            lo, hi = max(start, W_IN_SHARD * k), min(start + width, W_IN_SHARD * (k + 1))
            if lo < hi:
                shards[k].append(piece[:, lo - start:hi - start])
        start += width
    return jnp.stack([jnp.concatenate(s, axis=1) for s in shards])


def _prep_layer(p):
    q = {}
    q["segs"] = [s.astype(bf16) for s in _in_proj_segments(p["w_in"])]
    disc = _s5_discretize(p["s5_a_re"], p["s5_a_im"], p["s5_log_step"], p["s5_b_re"], p["s5_b_im"],
                          p["s5_c_re"], p["s5_c_im"])
    q["s5"] = disc
    q["pw"] = _lam_powers(disc[0], disc[1])
    q["s5_d"] = p["s5_d"].reshape(1, 512)
    q["qw"] = jnp.tile(p["q_norm_w"], 2).reshape(1, LANES)
    q["kw"] = jnp.tile(p["k_norm_w"], 2).reshape(1, LANES)
    q["conv_w"] = p["conv_w"]
    q["conv_b"] = p["conv_b"].reshape(1, SSD_XBC)
    pad = lambda v: jnp.pad(v, (0, LANES - v.shape[0])).reshape(1, LANES)
    q["dt_bias"], q["a_log"] = pad(p["dt_bias"]), pad(p["ssd_a_log"])
    q["d_full"] = jnp.repeat(p["ssd_d"], 64).reshape(1, SSD_WIDTH)
    q["glu_b"] = p["s5_glu_b"].reshape(1, 512)
    q["nw"] = p["ssd_norm_w"].reshape(1, SSD_WIDTH)
    q["norm_w"] = p["norm_w"].reshape(1, D_MODEL)
    q["tailw"] = tuple(p[n].astype(bf16) for n in ("s5_glu_w", "proj_a", "proj_b", "proj_c", "w_out"))
    return q


_DILATIONS = (1, 4, 16)


def layer_fwd(x, q, tag):
    h = rms_fwd(x, q["norm_w"], f"rms_fwd{tag}")
    p_s5, p_main = [mm_nn(h, w, f"inproj{k}{tag}") for k, w in enumerate(q["segs"])]
    _, _, w_re, w_im, c_re, c_im = q["s5"]
    ys5, h_re, h_im = s5_fwd(p_s5, *q["pw"], w_re, w_im, c_re, c_im, q["s5_d"], f"s5_fwd{tag}")
    os_, ls_ = [], []
    for g, d in enumerate(_DILATIONS):
        o, l = att_fwd(p_main, q["qw"], q["kw"], d, g, f"att_fwd{g}{tag}")
        os_.append(o)
        ls_.append(l)
    xact = conv_fwd(p_main, q["conv_w"], q["conv_b"], f"conv_fwd{tag}")
    yssd, states = ssd_fwd(xact, p_main, q["dt_bias"], q["a_log"], q["d_full"], f"ssd_fwd{tag}")
    out = tail_fwd(ys5, p_main, os_, ls_, yssd, x, q["glu_b"], q["nw"], q["tailw"], f"tail_fwd{tag}")
    saved = dict(x=x, h=h, p_s5=p_s5, p_main=p_main, ys5=ys5, h_re=h_re, h_im=h_im,
                 os=os_, ls=ls_, xact=xact, yssd=yssd, states=states)
    return out, saved


def layer_bwd(dout, sv, q, p, tag):
    S = dout.shape[0]
    (dys5, dp_main, do0, do1, do2, dl0, dl1, dl2, dyssd, dglu_b, dnw, g_b, ya_b, yb_b, yc_b, mg_b, dglu_b16,
     dpa_b, dpb_b, dpc_b) = tail_bwd(sv["ys5"], sv["p_main"], sv["os"], sv["ls"], sv["yssd"], dout, q["glu_b"],
                                     q["nw"], q["tailw"], f"tail_bwd{tag}")
    grads = {}
    grads["s5_glu_w"] = mm_tn(g_b, dglu_b16, f"dglu_w{tag}")
    grads["proj_a"] = mm_tn(ya_b, dpa_b, f"dproj_a{tag}")
    grads["proj_b"] = mm_tn(yb_b, dpb_b, f"dproj_b{tag}")
    grads["proj_c"] = mm_tn(yc_b, dpc_b, f"dproj_c{tag}")
    grads["w_out"] = mm_tn(mg_b, dout, f"dw_out{tag}")
    grads["s5_glu_b"] = dglu_b.reshape(512)
    grads["ssd_norm_w"] = dnw.reshape(SSD_WIDTH)

    dxact, ddt, ddt_bias, da_log, dd_full = ssd_bwd(sv["xact"], sv["p_main"], sv["states"], dyssd, q["dt_bias"],
                                                    q["a_log"], q["d_full"], f"ssd_bwd{tag}")
    dp_main, dconv_w, dconv_b = conv_bwd(sv["p_main"], dxact, ddt, q["conv_w"], q["conv_b"], dp_main,
                                         f"conv_bwd{tag}")
    grads["dt_bias"] = ddt_bias[0, :12]
    grads["ssd_a_log"] = da_log[0, :12]
    grads["ssd_d"] = dd_full.reshape(12, 64).sum(axis=1)
    grads["conv_w"] = dconv_w
    grads["conv_b"] = dconv_b.reshape(SSD_XBC)

    dqw, dkw = 0.0, 0.0
    for g, d in enumerate(_DILATIONS):
        dp_main, a, b = att_bwd(sv["p_main"], sv["os"][g], sv["ls"][g], (do0, do1, do2)[g], (dl0, dl1, dl2)[g],
                                q["qw"], q["kw"], d, g, dp_main, f"att_bwd{g}{tag}")
        dqw, dkw = dqw + a, dkw + b
    grads["q_norm_w"] = dqw.reshape(2, 64).sum(axis=0)
    grads["k_norm_w"] = dkw.reshape(2, 64).sum(axis=0)

    _, _, w_re, w_im, c_re, c_im = q["s5"]
    dp_s5, dwre, dwim, dcre, dcim, dlam_re, dlam_im, dd = s5_bwd(
        dys5, sv["p_s5"], sv["h_re"], sv["h_im"], *q["pw"], w_re, w_im, c_re, c_im, q["s5_d"], f"s5_bwd{tag}")
    s5_names = ("s5_a_re", "s5_a_im", "s5_log_step", "s5_b_re", "s5_b_im", "s5_c_re", "s5_c_im")
    _, disc_vjp = jax.vjp(_s5_discretize, *[p[n] for n in s5_names])
    for n, gr in zip(s5_names, disc_vjp((dlam_re, dlam_im, dwre, dwim, dcre, dcim))):
        grads[n] = gr
    grads["s5_d"] = dd.reshape(512)

    dsegs = [dp_s5, dp_main]
    dws = [mm_tn(sv["h"], ds, f"dw_in{k}{tag}") for k, ds in enumerate(dsegs)]
    grads["w_in"] = _in_proj_grad(*dws)
    dh = None
    for k, (ds, w) in enumerate(zip(dsegs, q["segs"])):
        dh = mm_nt(ds, w, f"dh{k}{tag}", acc=dh)
    dx, dnorm_w = rms_bwd(sv["x"], q["norm_w"], dh, dout, f"rms_bwd{tag}")
    grads["norm_w"] = dnorm_w.reshape(D_MODEL)
    return dx, grads


def _exchange(name, scatter=(), gather=(), sibling=(), sibling_both=False):
    scatter, gather, sibling = list(scatter), list(gather), list(sibling)
    chip_xs = scatter + gather
    ns, nc, nb = len(scatter), len(chip_xs), len(sibling)
    n = nc + nb

    def body(*refs):
        x_refs, o_refs, send_sems, recv_sems = refs[:n], refs[n:2 * n], refs[2 * n], refs[2 * n + 1]
        mx, my, mc = lax.axis_index("x"), lax.axis_index("y"), lax.axis_index("c")
        me = 2 * mx + my
        copies = []
        for a in range(nc):
            for t, (px, py) in enumerate(((1 - mx, my), (mx, 1 - my), (1 - mx, 1 - my))):
                src = x_refs[a].at[2 * px + py] if a < ns else x_refs[a]
                copies.append(pltpu.make_async_remote_copy(
                    src_ref=src, dst_ref=o_refs[a].at[me], send_sem=send_sems.at[3 * a + t],
                    recv_sem=recv_sems.at[3 * a + t], device_id=(px, py, mc), device_id_type=pl.DeviceIdType.MESH))
        for b in range(nc, n):
            k = 3 * nc + b - nc
            copies.append(pltpu.make_async_remote_copy(
                src_ref=x_refs[b], dst_ref=o_refs[b].at[mc] if sibling_both else o_refs[b], send_sem=send_sems.at[k],
                recv_sem=recv_sems.at[k], device_id=(mx, my, 1 - mc), device_id_type=pl.DeviceIdType.MESH))
        for cp in copies:
            cp.start()
        for cp in copies:
            cp.wait()

    shapes = ([(4,) + tuple(x.shape[1:]) for x in scatter] + [(4,) + tuple(x.shape) for x in gather]
              + [((2,) if sibling_both else ()) + tuple(x.shape) for x in sibling])
    xs = chip_xs + sibling
    outs = pl.pallas_call(
        body, name=name, in_specs=[_ANY] * n, out_specs=[_ANY] * n,
        out_shape=[jax.ShapeDtypeStruct(s, x.dtype) for s, x in zip(shapes, xs)],
        scratch_shapes=[pltpu.SemaphoreType.DMA((3 * nc + nb,)), pltpu.SemaphoreType.DMA((3 * nc + nb,))],
    )(*xs)
    me, c = 2 * lax.axis_index("x") + lax.axis_index("y"), lax.axis_index("c")
    fixed = []
    for a, (o, x) in enumerate(zip(outs, xs)):
        if a < ns:
            o = lax.dynamic_update_index_in_dim(o, lax.dynamic_index_in_dim(x, me, 0, keepdims=True), me, 0)
        elif a < nc:
            o = lax.dynamic_update_index_in_dim(o, x[None], me, 0)
        elif sibling_both:
            o = lax.dynamic_update_index_in_dim(o, x[None], c, 0)
        fixed.append(o)
    return fixed[:ns], fixed[ns:nc], fixed[nc:]


def _rows_tile(rows, row_bytes, budget=1 << 20):
    return next(t for t in (512, 256, 128, 64, 32, 16, 8) if rows % t == 0 and t * row_bytes <= budget)


def _padded_row_bytes(cols):
    return -(-cols // LANES) * LANES * 4


def _add2(a, b, name, out_dtype=f32):
    R, C = a.shape
    tr = _rows_tile(R, _padded_row_bytes(C))

    def body(a_ref, b_ref, o_ref):
        o_ref[...] = (a_ref[...] + b_ref[...]).astype(out_dtype)

    spec = pl.BlockSpec((tr, C), lambda i: (i, 0))
    return _call(body, name, (R // tr,), [spec, spec], spec, jax.ShapeDtypeStruct((R, C), out_dtype),
                 sem=("parallel",))(a, b)


def _sum4(x, name):
    R = x.shape[1]
    tr = _tile(R, (512, 256, 128))

    def body(x_ref, o_ref):
        p = [x_ref[j].astype(f32) for j in range(4)]
        o_ref[...] = ((p[0] + p[1]) + p[2]) + p[3]

    return _call(body, name, (R // tr,), [pl.BlockSpec((4, tr, LANES), lambda i: (0, i, 0))],
                 pl.BlockSpec((tr, LANES), lambda i: (i, 0)), jax.ShapeDtypeStruct((R, LANES), f32),
                 sem=("parallel",))(x)


def _adamw(g_parts, w, m, v, name):
    stacked = not isinstance(g_parts, (tuple, list))
    k = g_parts.shape[0] if stacked else len(g_parts)
    R, C = w.shape
    tr = _rows_tile(R, _padded_row_bytes(C))
    c1 = 1.0 - ADAM_B1 ** ADAM_STEP
    c2 = 1.0 - ADAM_B2 ** ADAM_STEP

    def body(*refs):
        w_ref, m_ref, v_ref, g_ref, d_ref, nm_ref, nv_ref = refs[-7:]
        if stacked:
            g = refs[0][0].astype(f32)
            for j in range(1, k):
                g = g + refs[0][j].astype(f32)
        else:
            g = refs[0][...]
            for r in refs[1:k]:
                g = g + r[...]
        m = ADAM_B1 * m_ref[...] + (1.0 - ADAM_B1) * g
        v = ADAM_B2 * v_ref[...] + (1.0 - ADAM_B2) * (g * g)
        g_ref[...] = g
        nm_ref[...] = m
        nv_ref[...] = v
        d_ref[...] = -ADAM_LR * ((m / c1) / (jnp.sqrt(v / c2) + ADAM_EPS) + ADAM_WD * w_ref[...])

    spec = pl.BlockSpec((tr, C), lambda i: (i, 0))
    sd = jax.ShapeDtypeStruct((R, C), f32)
    g_specs = [pl.BlockSpec((k, tr, C), lambda i: (0, i, 0))] if stacked else [spec] * k
    g_args = [g_parts] if stacked else list(g_parts)
    return _call(body, name, (R // tr,), g_specs + [spec] * 3, [spec] * 4, [sd] * 4,
                 sem=("parallel",))(*g_args, w, m, v)


def _pack(arrays):
    flat = jnp.concatenate([a.reshape(-1) for a in arrays])
    unit = PACK_ROWS * LANES
    n = -(-flat.shape[0] // unit) * unit
    return jnp.pad(flat, (0, n - flat.shape[0])).reshape(n // LANES, LANES)


def _unpack(buf, shapes):
    flat = buf.reshape(-1)
    out, off = [], 0
    for s in shapes:
        n = 1
        for dim in s:
            n *= dim
        out.append(flat[off:off + n].reshape(s))
        off += n
    return out


def _to_shards(full, axis):
    s = full.shape
    t = full.reshape(s[:axis] + (4, s[axis] // 4) + s[axis + 1:])
    return jnp.moveaxis(t, axis, 0)


def _from_shards(sh, axis):
    t = jnp.moveaxis(sh, 0, axis)
    s = t.shape
    return t.reshape(s[:axis] + (s[axis] * s[axis + 1],) + s[axis + 2:])


def kernel(x, norm_w, w_in, s5_a_re, s5_a_im, s5_log_step, s5_b_re, s5_b_im, s5_c_re, s5_c_im, s5_d, s5_glu_w, s5_glu_b, q_norm_w, k_norm_w, conv_w, conv_b, dt_bias, ssd_a_log, ssd_d, ssd_norm_w, proj_a, proj_b, proj_c, w_out, loss_target, m_norm_w, m_w_in, m_s5_a_re, m_s5_a_im, m_s5_log_step, m_s5_b_re, m_s5_b_im, m_s5_c_re, m_s5_c_im, m_s5_d, m_s5_glu_w, m_s5_glu_b, m_q_norm_w, m_k_norm_w, m_conv_w, m_conv_b, m_dt_bias, m_ssd_a_log, m_ssd_d, m_ssd_norm_w, m_proj_a, m_proj_b, m_proj_c, m_w_out, v_norm_w, v_w_in, v_s5_a_re, v_s5_a_im, v_s5_log_step, v_s5_b_re, v_s5_b_im, v_s5_c_re, v_s5_c_im, v_s5_d, v_s5_glu_w, v_s5_glu_b, v_q_norm_w, v_k_norm_w, v_conv_w, v_conv_b, v_dt_bias, v_ssd_a_log, v_ssd_d, v_ssd_norm_w, v_proj_a, v_proj_b, v_proj_c, v_w_out):
    given = dict(locals())
    W = {n: given[n] for n in _WEIGHTS}
    M = {n: given["m_" + n] for n in _WEIGHTS}
    V = {n: given["v_" + n] for n in _WEIGHTS}
    n_layers = norm_w.shape[0]
    assert n_layers == 2
    c = lax.axis_index("c")

    mine_of = lambda t: lax.dynamic_index_in_dim(t, c, 0, keepdims=False)
    as_payload = lambda n: lax.bitcast_convert_type(W[n], bf16) if n == "conv_w" else W[n].astype(bf16)
    payload_shapes = [W[n].shape + ((2,) if n == "conv_w" else ()) for n, _ in _SHARDED]
    wpack = _pack([as_payload(n) for n, _ in _SHARDED])
    half_rows = wpack.shape[0] // 2
    _, (pack_half, w_in_mine_layer), _ = _exchange(
        "gather_weights", gather=[lax.dynamic_slice_in_dim(wpack, c * half_rows, half_rows),
                                  mine_of(w_in).astype(bf16)])
    _, _, (w_in_layers, pack_halves) = _exchange("share_weights", sibling=[w_in_mine_layer, pack_half],
                                                 sibling_both=True)
    gathered = jnp.moveaxis(pack_halves, 0, 1).reshape(4, 2 * half_rows, LANES)
    full = dict(W)
    pieces = [_unpack(gathered[j], payload_shapes) for j in range(4)]
    for k, (n, axis) in enumerate(_SHARDED):
        sh = jnp.stack([pieces[j][k] for j in range(4)])
        full[n] = _from_shards(lax.bitcast_convert_type(sh, f32) if n == "conv_w" else sh, axis)

    xs = x[0]
    qs, saves = [], []
    act = xs
    for l in range(n_layers):
        p = {n: full[n][l] for n in _WEIGHTS if n != "w_in"}
        p["w_in"] = [w_in_layers[l, k] for k in range(4)]
        q = _prep_layer(p)
        act, sv = layer_fwd(act, q, f"_l{l}")
        qs.append((q, p))
        saves.append(sv)
    dact, lsum = loss_and_grad(act, loss_target[0], "loss")
    loss = lax.psum(lsum[0, 0], ("x", "y", "c"))
    layer_grads = [None] * n_layers
    for l in reversed(range(n_layers)):
        q, p = qs[l]
        dact, layer_grads[l] = layer_bwd(dact, saves[l], q, p, f"_l{l}")
    grad_x = dact[None]
    G = {n: jnp.stack([layer_grads[l][n] for l in range(n_layers)]) for n in _WEIGHTS if n != "w_in"}

    repl_shapes = [W[n].shape for n in _REPL]
    small = _pack([G[n] for n in _REPL])
    quarter = small.shape[0] // 4
    big = [_to_shards(G[n], axis).reshape(4, -1) for n, axis in _SHARDED]
    big = jnp.concatenate(big, axis=1)
    unit = PACK_ROWS * LANES
    nbig = -(-big.shape[1] // unit) * unit
    big = jnp.pad(big, ((0, 0), (0, nbig - big.shape[1]))).reshape(4, nbig // LANES, LANES)
    gpack = jnp.concatenate([big, small.reshape(4, quarter, LANES)], axis=1)
    rbig = nbig // LANES
    g0, g1 = layer_grads[0]["w_in"], layer_grads[1]["w_in"]

    (landed_pack,), _, (from_sibling,) = _exchange(
        "swap_w_in_grads_and_scatter_grads", scatter=[gpack.astype(bf16)], sibling=[jnp.where(c == 0, g1, g0)])
    flat = lambda t: t.reshape(4 * D_MODEL, W_IN_SHARD)
    shards = _add2(flat(jnp.where(c == 0, g0, g1)), flat(from_sibling), "sum_cores_w_in", out_dtype=bf16)
    mine = _sum4(landed_pack, "sum_chips")

    (landed,), _, (other,) = _exchange(
        "scatter_w_in_grads_and_swap_cores", scatter=[shards.reshape(4, D_MODEL, W_IN_SHARD)], sibling=[mine])
    w_in_mine = _adamw(landed, mine_of(w_in), mine_of(m_w_in), mine_of(v_w_in), "adamw_w_in")
    gq = _add2(mine[rbig:], other[rbig:], "sum_cores_small")

    _, (gsmall,), w_in_out = _exchange(
        "share_w_in_updates_and_gather_small", gather=[gq], sibling=w_in_mine, sibling_both=True)
    gsmall = gsmall.reshape(4 * quarter, LANES)

    wp, mp, vp = (_pack([T[n] for n, _ in _SHARDED]) for T in (W, M, V))
    outs_big = _adamw((mine[:rbig], other[:rbig]), wp, mp, vp, "adamw_sharded")
    big_out = [_unpack(o, [W[n].shape for n, _ in _SHARDED]) for o in outs_big]
    ws, ms, vs = (_pack([T[n] for n in _REPL]) for T in (W, M, V))
    outs_small = _adamw((gsmall,), ws, ms, vs, "adamw_replicated")
    small_out = [_unpack(o, repl_shapes) for o in outs_small]

    res = [dict(), dict(), dict(), dict()]
    for kind in range(4):
        res[kind]["w_in"] = w_in_out[kind]
        for k, (n, _) in enumerate(_SHARDED):
            res[kind][n] = big_out[kind][k]
        for k, n in enumerate(_REPL):
            res[kind][n] = small_out[kind][k]
    return (loss, grad_x, *[res[0][n] for n in _WEIGHTS], *[res[1][n] for n in _WEIGHTS],
            *[res[2][n] for n in _WEIGHTS], *[res[3][n] for n in _WEIGHTS])
```

```python
import functools

import jax
import jax.numpy as jnp
from jax import lax
from jax.experimental import pallas as pl
from jax.experimental.pallas import tpu as pltpu

f32 = jnp.float32
bf16 = jnp.bfloat16

D_MODEL = 1024
RMS_EPS = 1e-6
V7X_VMEM_LIMIT = 60 * 1024 * 1024
LANES = 128
NN, NT, TN = ((1,), (0,)), ((1,), (1,)), ((0,), (0,))

S5_STATES = 2048
S5_ROWS = 512
ATT_SEG = 2048
ATT_BLOCK = 128
SSD_CHUNK = 128
SSD_CHUNKS_PER_STEP = 2
SSD_WIDTH = 768
SSD_XBC = 1280
CONV_ROWS = 512
TAIL_ROWS = 256

ADAM_LR, ADAM_B1, ADAM_B2, ADAM_EPS, ADAM_WD, ADAM_STEP = 0.001, 0.9, 0.999, 1e-08, 0.01, 10

_C_UA, _C_ZA, _C_Q, _C_K, _C_V, _C_ZB, _C_XBC, _C_DT, _C_ZC, _C_GATE, _C_END = (
    0, 512, 1024, 1792, 2560, 3328, 3584, 4864, 4876, 5644, 8716)

_SHARDED = (("s5_glu_w", 1), ("conv_w", 2), ("proj_a", 2), ("proj_b", 2), ("proj_c", 2), ("w_out", 1))
W_IN_SHARD = 2179
_REPL = ("norm_w", "s5_a_re", "s5_a_im", "s5_log_step", "s5_b_re", "s5_b_im", "s5_c_re", "s5_c_im", "s5_d",
         "s5_glu_b", "q_norm_w", "k_norm_w", "conv_b", "dt_bias", "ssd_a_log", "ssd_d", "ssd_norm_w")
_WEIGHTS = ("norm_w", "w_in", "s5_a_re", "s5_a_im", "s5_log_step", "s5_b_re", "s5_b_im", "s5_c_re", "s5_c_im",
            "s5_d", "s5_glu_w", "s5_glu_b", "q_norm_w", "k_norm_w", "conv_w", "conv_b", "dt_bias", "ssd_a_log",
            "ssd_d", "ssd_norm_w", "proj_a", "proj_b", "proj_c", "w_out")
PACK_ROWS = 512


def _dot(a, b, dims):
    return lax.dot_general(a.astype(bf16), b.astype(bf16), (dims, ((), ())), preferred_element_type=f32)


_ANY = pl.BlockSpec(memory_space=pl.ANY)

MAIN_WIDTH = 8448
MAIN_SSD_BLOCK = 3
MAIN_DT_BLOCK = 46
MAIN_ATT_BLOCK = 16


def _call(body, name, grid, in_specs, out_specs, out_shape, scratch=(), sem=None, aliases=None):
    return pl.pallas_call(
        body, name=name, grid=grid, in_specs=in_specs, out_specs=out_specs, out_shape=out_shape,
        scratch_shapes=list(scratch), input_output_aliases=aliases or {},
        compiler_params=pltpu.CompilerParams(dimension_semantics=sem, vmem_limit_bytes=V7X_VMEM_LIMIT))


def _tile(n, options=(1024, 768, 512, 384, 256, 128)):
    return next(t for t in options if n % t == 0)


@functools.partial(jax.custom_vjp, nondiff_argnums=(2,))
def _bdot(a, b, dims):
    return _dot(a, b, dims)


def _bdot_fwd(a, b, dims):
    return _dot(a, b, dims), (a, b)


def _bdot_bwd(dims, res, g):
    a, b = res
    if dims == NN:
        da, db = _dot(g, b, NT), _dot(a, g, TN)
    elif dims == NT:
        da, db = _dot(g, b, NN), _dot(g, a, TN)
    else:
        da, db = _dot(b, g, NT), _dot(a, g, NN)
    return da.astype(a.dtype), db.astype(b.dtype)


_bdot.defvjp(_bdot_fwd, _bdot_bwd)


@functools.partial(jax.custom_vjp, nondiff_argnums=(2,))
def _cdot(a, w, dims):
    return _dot(a, w, dims)


def _cdot_fwd(a, w, dims):
    return _dot(a, w, dims), w


def _cdot_bwd(dims, w, g):
    da = _dot(g, w, NT) if dims == NN else _dot(g, w, NN)
    return da, jnp.zeros_like(w)


_cdot.defvjp(_cdot_fwd, _cdot_bwd)


def _split3(x):
    hi = x.astype(bf16)
    r = x - hi.astype(f32)
    mid = r.astype(bf16)
    lo = (r - mid.astype(f32)).astype(bf16)
    return hi, mid, lo


@jax.custom_vjp
def _xdot_l(m, x):
    return sum(_dot(m, p, NN) for p in _split3(x))


def _xdot_l_fwd(m, x):
    return _xdot_l(m, x), m


def _xdot_l_bwd(m, g):
    return jnp.zeros_like(m), sum(_dot(m, p, TN) for p in _split3(g))


_xdot_l.defvjp(_xdot_l_fwd, _xdot_l_bwd)


@jax.custom_vjp
def _softplus(x):
    e = jnp.exp(-jnp.abs(x))
    u = 1.0 + e
    log1p = jnp.where(u == 1.0, e, jnp.log(u) * (e / jnp.where(u == 1.0, 1.0, u - 1.0)))
    return jnp.maximum(x, 0.0) + log1p


def _softplus_fwd(x):
    return _softplus(x), x


def _softplus_bwd(x, g):
    return (g * jax.nn.sigmoid(x),)


_softplus.defvjp(_softplus_fwd, _softplus_bwd)


def _rms(x, w):
    return x * lax.rsqrt(jnp.mean(x * x, axis=-1, keepdims=True) + RMS_EPS) * w


def mm_nn(a, b, name, tm=2048):
    M, K = a.shape
    N = b.shape[1]
    tn = _tile(N)

    def body(a_ref, b_ref, o_ref):
        o_ref[...] = _dot(a_ref[...], b_ref[...], NN)

    return _call(body, name, (M // tm, N // tn),
                 [pl.BlockSpec((tm, K), lambda i, j: (i, 0)), pl.BlockSpec((K, tn), lambda i, j: (0, j))],
                 pl.BlockSpec((tm, tn), lambda i, j: (i, j)), jax.ShapeDtypeStruct((M, N), f32),
                 sem=("parallel", "parallel"))(a, b)


def mm_nt(a, b, name, acc=None, tm=1024):
    M, K = a.shape
    N = b.shape[0]
    tk = _tile(K, (2816, 1024, 768, 512, 256, 128))
    has_acc = acc is not None

    def body(*refs):
        a_ref, b_ref = refs[0], refs[1]
        o_ref = refs[-1]
        k = pl.program_id(1)
        p = _dot(a_ref[...], b_ref[...], NT)

        @pl.when(k == 0)
        def _():
            o_ref[...] = p + refs[2][...] if has_acc else p

        @pl.when(k > 0)
        def _():
            o_ref[...] += p

    specs = [pl.BlockSpec((tm, tk), lambda i, k: (i, k)), pl.BlockSpec((N, tk), lambda i, k: (0, k))]
    args = [a, b]
    if has_acc:
        specs.append(pl.BlockSpec((tm, N), lambda i, k: (i, 0)))
        args.append(acc)
    return _call(body, name, (M // tm, K // tk), specs, pl.BlockSpec((tm, N), lambda i, k: (i, 0)),
                 jax.ShapeDtypeStruct((M, N), f32), sem=("parallel", "arbitrary"))(*args)


def mm_tn(a, b, name, tk=2048):
    K, M = a.shape
    N = b.shape[1]
    tn = _tile(N)

    def body(a_ref, b_ref, o_ref):
        k = pl.program_id(1)
        p = _dot(a_ref[...], b_ref[...], TN)

        @pl.when(k == 0)
        def _():
            o_ref[...] = p

        @pl.when(k > 0)
        def _():
            o_ref[...] += p

    return _call(body, name, (N // tn, K // tk),
                 [pl.BlockSpec((tk, M), lambda j, k: (k, 0)), pl.BlockSpec((tk, tn), lambda j, k: (k, j))],
                 pl.BlockSpec((M, tn), lambda j, k: (0, j)), jax.ShapeDtypeStruct((M, N), f32),
                 sem=("parallel", "arbitrary"))(a, b)


def rms_fwd(x, w, name, tm=512):
    S = x.shape[0]

    def body(x_ref, w_ref, o_ref):
        o_ref[...] = _rms(x_ref[...], w_ref[...]).astype(bf16)

    return _call(body, name, (S // tm,),
                 [pl.BlockSpec((tm, D_MODEL), lambda i: (i, 0)), pl.BlockSpec((1, D_MODEL), lambda i: (0, 0))],
                 pl.BlockSpec((tm, D_MODEL), lambda i: (i, 0)), jax.ShapeDtypeStruct((S, D_MODEL), bf16),
                 sem=("parallel",))(x, w)


def rms_bwd(x, w, dh, dres, name, tm=512):
    S = x.shape[0]

    def body(x_ref, w_ref, dh_ref, dr_ref, dx_ref, dw_ref):
        _, vjp = jax.vjp(_rms, x_ref[...], w_ref[...])
        dx, dw = vjp(dh_ref[...])
        dx_ref[...] = dx + dr_ref[...]

        @pl.when(pl.program_id(0) == 0)
        def _():
            dw_ref[...] = dw

        @pl.when(pl.program_id(0) > 0)
        def _():
            dw_ref[...] += dw

    row = pl.BlockSpec((tm, D_MODEL), lambda i: (i, 0))
    vec = pl.BlockSpec((1, D_MODEL), lambda i: (0, 0))
    return _call(body, name, (S // tm,), [row, vec, row, row], [row, vec],
                 [jax.ShapeDtypeStruct((S, D_MODEL), f32), jax.ShapeDtypeStruct((1, D_MODEL), f32)],
                 sem=("arbitrary",))(x, w, dh, dres)


def loss_and_grad(y, target, name, tm=512):
    S = y.shape[0]

    def body(y_ref, t_ref, dy_ref, l_ref):
        diff = y_ref[...] - t_ref[...]
        dy_ref[...] = diff * (1.0 / D_MODEL)
        part = jnp.full((8, LANES), 0.5 / D_MODEL * jnp.sum(diff * diff), f32)

        @pl.when(pl.program_id(0) == 0)
        def _():
            l_ref[...] = part

        @pl.when(pl.program_id(0) > 0)
        def _():
            l_ref[...] += part

    row = pl.BlockSpec((tm, D_MODEL), lambda i: (i, 0))
    return _call(body, name, (S // tm,), [row, row], [row, pl.BlockSpec((8, LANES), lambda i: (0, 0))],
                 [jax.ShapeDtypeStruct((S, D_MODEL), f32), jax.ShapeDtypeStruct((8, LANES), f32)],
                 sem=("arbitrary",))(y, target)


def _s5_discretize(a_re, a_im, log_step, b_re, b_im, c_re, c_im):
    step = jnp.exp(log_step)[:, None]
    mag = jnp.exp(a_re * step)
    ang = a_im * step
    lam_re, lam_im = mag * jnp.cos(ang), mag * jnp.sin(ang)
    num_re, num_im = lam_re - 1.0, lam_im
    den = a_re * a_re + a_im * a_im
    f_re = (num_re * a_re + num_im * a_im) / den
    f_im = (num_im * a_re - num_re * a_im) / den
    bb_re = f_re[..., None] * b_re - f_im[..., None] * b_im
    bb_im = f_re[..., None] * b_im + f_im[..., None] * b_re
    eye = jnp.eye(8, dtype=f32)

    def block_in(bb):
        t = bb.transpose(0, 2, 1).reshape(4, 8, 16, 1, 64)
        return (t * eye[None, :, None, :, None]).reshape(4, 128, 512)

    def block_out(c):
        t = c.transpose(0, 2, 1).reshape(4, 8, 64, 1, 16)
        return (t * eye[None, :, None, :, None]).reshape(4, 512, 128)

    return (lam_re.reshape(1, S5_STATES), lam_im.reshape(1, S5_STATES), block_in(bb_re), block_in(bb_im),
            block_out(c_re), block_out(c_im))


def _lam_powers(lam_re, lam_im):
    rows_re, rows_im = [lam_re], [lam_im]
    for _ in range(7):
        pr, pi = rows_re[-1], rows_im[-1]
        rows_re.append(pr * lam_re - pi * lam_im)
        rows_im.append(pr * lam_im + pi * lam_re)
    return jnp.concatenate(rows_re, 0), jnp.concatenate(rows_im, 0)


def s5_fwd(u, pw_re, pw_im, w_re, w_im, c_re, c_im, dvec, name):
    S = u.shape[0]
    R, NS = S5_ROWS, S5_STATES
    nb = R // 8

    def body(u_ref, pwr_ref, pwi_ref, wre_ref, wim_ref, cre_ref, cim_ref, d_ref, y_ref, hr_ref, hi_ref,
             car_re, car_im, cin_re, cin_im, up, yp):
        @pl.when(pl.program_id(0) == 0)
        def _():
            car_re[...] = jnp.zeros_like(car_re)
            car_im[...] = jnp.zeros_like(car_im)

        slab = lambda r: pl.ds(r * nb, nb)
        for r in range(8):
            up[slab(r), :] = u_ref[:, r, :]
        u = up[...]
        for j in range(4):
            uj = u[:, 128 * j:128 * (j + 1)]
            hr_ref[:, 512 * j:512 * (j + 1)] = _dot(uj, wre_ref[j], NN)
            hi_ref[:, 512 * j:512 * (j + 1)] = _dot(uj, wim_ref[j], NN)
        lr, li = pwr_ref[0:1, :], pwi_ref[0:1, :]
        for r in range(1, 8):
            pr, pi = hr_ref[slab(r - 1), :], hi_ref[slab(r - 1), :]
            hr_ref[slab(r), :] = lr * pr - li * pi + hr_ref[slab(r), :]
            hi_ref[slab(r), :] = lr * pi + li * pr + hi_ref[slab(r), :]
        l8r, l8i = pwr_ref[7:8, :], pwi_ref[7:8, :]

        def across(c, carry):
            gr, gi = carry
            cin_re[pl.ds(c, 1), :] = gr
            cin_im[pl.ds(c, 1), :] = gi
            er, ei = hr_ref[pl.ds(7 * nb + c, 1), :], hi_ref[pl.ds(7 * nb + c, 1), :]
            return l8r * gr - l8i * gi + er, l8r * gi + l8i * gr + ei

        gr, gi = lax.fori_loop(0, nb, across, (car_re[...], car_im[...]))
        car_re[...] = gr
        car_im[...] = gi
        cr, ci = cin_re[...], cin_im[...]
        for r in range(8):
            pr, pi = pwr_ref[r:r + 1, :], pwi_ref[r:r + 1, :]
            hr_ref[slab(r), :] = hr_ref[slab(r), :] + pr * cr - pi * ci
            hi_ref[slab(r), :] = hi_ref[slab(r), :] + pr * ci + pi * cr
        for j in range(4):
            sl = slice(512 * j, 512 * (j + 1))
            cs = slice(128 * j, 128 * (j + 1))
            yp[:, cs] = (_dot(hr_ref[:, sl], cre_ref[j], NN) - _dot(hi_ref[:, sl], cim_ref[j], NN)
                         + d_ref[:, cs] * u[:, cs])
        for r in range(8):
            y_ref[:, r, :] = yp[slab(r), :]

    full = lambda shape: pl.BlockSpec(shape, lambda i: (0,) * len(shape))
    hspec = pl.BlockSpec((R, NS), lambda i: (i, 0))
    uspec = pl.BlockSpec((nb, 8, 512), lambda i: (i, 0, 0))
    y, h_re, h_im = _call(
        body, name, (S // R,),
        [uspec, full((8, NS)), full((8, NS)), full((4, 128, 512)),
         full((4, 128, 512)), full((4, 512, 128)), full((4, 512, 128)), full((1, 512))],
        [uspec, hspec, hspec],
        [jax.ShapeDtypeStruct((S // 8, 8, 512), f32), jax.ShapeDtypeStruct((S, NS), f32),
         jax.ShapeDtypeStruct((S, NS), f32)],
        scratch=[pltpu.VMEM((1, NS), f32), pltpu.VMEM((1, NS), f32), pltpu.VMEM((nb, NS), f32),
                 pltpu.VMEM((nb, NS), f32), pltpu.VMEM((R, 512), f32), pltpu.VMEM((R, 512), f32)],
        sem=("arbitrary",))(u.reshape(S // 8, 8, 512), pw_re, pw_im, w_re.astype(bf16), w_im.astype(bf16),
                            c_re.astype(bf16), c_im.astype(bf16), dvec)
    return y.reshape(S, 512), h_re, h_im


def s5_bwd(dy, u, h_re, h_im, pw_re, pw_im, w_re, w_im, c_re, c_im, dvec, name):
    S = u.shape[0]
    R, NS = S5_ROWS, S5_STATES
    nb = R // 8
    nchunk = S // R

    def body(dy_ref, u_ref, hr_ref, hi_ref, hpr_ref, hpi_ref, pwr_ref, pwi_ref, wre_ref, wim_ref, cre_ref, cim_ref,
             d_ref, du_ref, dwre_ref, dwim_ref, dcre_ref, dcim_ref, dlr_ref, dli_ref, dd_ref,
             ar, ai, car_re, car_im, cin_re, cin_im, up, dyp, dup):
        i = pl.program_id(0)

        @pl.when(i == 0)
        def _():
            for ref in (car_re, car_im, dwre_ref, dwim_ref, dcre_ref, dcim_ref, dlr_ref, dli_ref, dd_ref):
                ref[...] = jnp.zeros_like(ref)

        slab = lambda r: pl.ds(r * nb, nb)
        for r in range(8):
            up[slab(r), :] = u_ref[:, r, :]
            dyp[slab(r), :] = dy_ref[:, r, :]
        dy = dyp[...]
        u = up[...]
        for j in range(4):
            dyj = dy[:, 128 * j:128 * (j + 1)]
            ar[:, 512 * j:512 * (j + 1)] = _dot(dyj, cre_ref[j], NT)
            ai[:, 512 * j:512 * (j + 1)] = -_dot(dyj, cim_ref[j], NT)
        lr, li = pwr_ref[0:1, :], pwi_ref[0:1, :]
        for r in range(6, -1, -1):
            nr, ni = ar[slab(r + 1), :], ai[slab(r + 1), :]
            ar[slab(r), :] = lr * nr + li * ni + ar[slab(r), :]
            ai[slab(r), :] = lr * ni - li * nr + ai[slab(r), :]
        l8r, l8i = pwr_ref[7:8, :], pwi_ref[7:8, :]

        def across(k, carry):
            c = nb - 1 - k
            gr, gi = carry
            cin_re[pl.ds(c, 1), :] = gr
            cin_im[pl.ds(c, 1), :] = gi
            er, ei = ar[pl.ds(c, 1), :], ai[pl.ds(c, 1), :]
            return l8r * gr + l8i * gi + er, l8r * gi - l8i * gr + ei

        gr, gi = lax.fori_loop(0, nb, across, (car_re[...], car_im[...]))
        car_re[...] = gr
        car_im[...] = gi
        cr, ci = cin_re[...], cin_im[...]
        for r in range(8):
            pr, pi = pwr_ref[7 - r:8 - r, :], pwi_ref[7 - r:8 - r, :]
            ar[slab(r), :] = ar[slab(r), :] + pr * cr + pi * ci
            ai[slab(r), :] = ai[slab(r), :] + pr * ci - pi * cr

        acc_r = jnp.zeros((1, NS), f32)
        acc_i = jnp.zeros((1, NS), f32)
        has_prev = (i < nchunk - 1).astype(f32)
        top = lax.broadcasted_iota(jnp.int32, (nb, NS), 0) == 0
        for r in range(8):
            if r == 0:
                xr = jnp.where(top, hpr_ref[7:8, :] * has_prev, pltpu.roll(hr_ref[slab(7), :], 1, 0))
                xi = jnp.where(top, hpi_ref[7:8, :] * has_prev, pltpu.roll(hi_ref[slab(7), :], 1, 0))
            else:
                xr, xi = hr_ref[slab(r - 1), :], hi_ref[slab(r - 1), :]
            br, bi = ar[slab(r), :], ai[slab(r), :]
            acc_r += jnp.sum(br * xr + bi * xi, axis=0, keepdims=True)
            acc_i += jnp.sum(bi * xr - br * xi, axis=0, keepdims=True)
        dlr_ref[...] += acc_r
        dli_ref[...] += acc_i
        dd_ref[...] += jnp.sum(dy * u, axis=0, keepdims=True)

        for j in range(4):
            sl = slice(512 * j, 512 * (j + 1))
            cs = slice(128 * j, 128 * (j + 1))
            arj, aij = ar[:, sl], ai[:, sl]
            uj, dyj = u[:, cs], dy[:, cs]
            dup[:, cs] = _dot(arj, wre_ref[j], NT) + _dot(aij, wim_ref[j], NT) + d_ref[:, cs] * dyj
            dwre_ref[j] += _dot(uj, arj, TN)
            dwim_ref[j] += _dot(uj, aij, TN)
            dcre_ref[j] += _dot(hr_ref[:, sl], dyj, TN)
            dcim_ref[j] -= _dot(hi_ref[:, sl], dyj, TN)
        for r in range(8):
            du_ref[:, r, :] = dup[slab(r), :]

    rev = lambda i: nchunk - 1 - i
    full = lambda shape: pl.BlockSpec(shape, lambda i: (0,) * len(shape))
    row = pl.BlockSpec((nb, 8, 512), lambda i: (rev(i), 0, 0))
    hspec = pl.BlockSpec((R, NS), lambda i: (rev(i), 0))
    hprev = pl.BlockSpec((8, NS), lambda i: (jnp.maximum(rev(i) * nb - 1, 0), 0))
    outs = _call(
        body, name, (nchunk,),
        [row, row, hspec, hspec, hprev, hprev, full((8, NS)), full((8, NS)), full((4, 128, 512)), full((4, 128, 512)),
         full((4, 512, 128)), full((4, 512, 128)), full((1, 512))],
        [row, full((4, 128, 512)), full((4, 128, 512)), full((4, 512, 128)), full((4, 512, 128)),
         full((1, NS)), full((1, NS)), full((1, 512))],
        [jax.ShapeDtypeStruct((S // 8, 8, 512), f32), jax.ShapeDtypeStruct((4, 128, 512), f32),
         jax.ShapeDtypeStruct((4, 128, 512), f32), jax.ShapeDtypeStruct((4, 512, 128), f32),
         jax.ShapeDtypeStruct((4, 512, 128), f32), jax.ShapeDtypeStruct((1, NS), f32),
         jax.ShapeDtypeStruct((1, NS), f32), jax.ShapeDtypeStruct((1, 512), f32)],
        scratch=[pltpu.VMEM((R, NS), f32), pltpu.VMEM((R, NS), f32), pltpu.VMEM((1, NS), f32),
                 pltpu.VMEM((1, NS), f32), pltpu.VMEM((nb, NS), f32), pltpu.VMEM((nb, NS), f32),
                 pltpu.VMEM((R, 512), f32), pltpu.VMEM((R, 512), f32), pltpu.VMEM((R, 512), f32)],
        sem=("arbitrary",))(dy.reshape(S // 8, 8, 512), u.reshape(S // 8, 8, 512), h_re, h_im, h_re, h_im, pw_re,
                            pw_im, w_re.astype(bf16), w_im.astype(bf16), c_re.astype(bf16), c_im.astype(bf16), dvec)
    return (outs[0].reshape(S, 512),) + tuple(outs[1:])


def _rows(start, n, d):
    return pl.ds(pl.multiple_of(start, ATT_BLOCK), n) if d == 1 else pl.ds(start, n, stride=d)


def _head_masks():
    lane = lax.broadcasted_iota(jnp.int32, (1, LANES), 1)
    return [(lane < 64).astype(f32), (lane >= 64).astype(f32)]


def _head_norm(x, w, hm):
    x2 = x * x
    r = [lax.rsqrt(jnp.sum(x2 * hm[h], axis=-1, keepdims=True) * (1.0 / 64) + RMS_EPS) for h in range(2)]
    sc = hm[0] * r[0] + hm[1] * r[1]
    return x * sc * w, sc, r


def _head_norm_bwd(x, w, sc, r, dxn, hm):
    dw = jnp.sum(dxn * x * sc, axis=0, keepdims=True)
    t = dxn * w
    tx = t * x
    corr = sum(hm[h] * (r[h] * r[h] * r[h]) * jnp.sum(tx * hm[h], axis=-1, keepdims=True) for h in range(2))
    return t * sc - x * corr * (1.0 / 64), dw


def _att_mask(has_prev):
    qi = lax.broadcasted_iota(jnp.int32, (ATT_BLOCK, 2 * ATT_BLOCK), 0) + ATT_BLOCK
    kj = lax.broadcasted_iota(jnp.int32, (ATT_BLOCK, 2 * ATT_BLOCK), 1)
    return (qi - kj >= 0) & (qi - kj <= ATT_BLOCK) & (has_prev | (kj >= ATT_BLOCK))


def _att_block_bwd(q, k, v, o, lse, do, dlse, qw, kw, has_prev):
    hm = _head_masks()
    mask = _att_mask(has_prev)
    qn, qsc, qr = _head_norm(q, qw, hm)
    kn, ksc, kr = _head_norm(k, kw, hm)
    dqn = jnp.zeros((ATT_BLOCK, LANES), f32)
    dkn = jnp.zeros((2 * ATT_BLOCK, LANES), f32)
    dv = jnp.zeros((2 * ATT_BLOCK, LANES), f32)
    for h in range(2):
        qh, do_h = qn * hm[h], do * hm[h]
        s = _dot(qh, kn, NT) * 0.125
        p = jnp.exp(jnp.where(mask, s - lse[:, 64 * h:64 * h + 1], -jnp.inf))
        dp = _dot(do_h, v, NT)
        delta = jnp.sum(do_h * o, axis=-1, keepdims=True)
        dl = jnp.sum(dlse * hm[h], axis=-1, keepdims=True)
        ds = p * (dp - delta + dl) * 0.125
        dqn = dqn + hm[h] * _dot(ds, kn, NN)
        dkn = dkn + _dot(ds, qh, TN)
        dv = dv + _dot(p, do_h, TN)
    dq, dqw = _head_norm_bwd(q, qw, qsc, qr, dqn, hm)
    dk, dkw = _head_norm_bwd(k, kw, ksc, kr, dkn, hm)
    return dq, dk, dv, dqw, dkw


def _att_block(q, k, v, qw, kw, has_prev):
    hm = _head_masks()
    qn, kn = _head_norm(q, qw, hm)[0], _head_norm(k, kw, hm)[0]
    mask = _att_mask(has_prev)
    o = jnp.zeros((ATT_BLOCK, LANES), f32)
    lse = jnp.zeros((ATT_BLOCK, LANES), f32)
    for h in range(2):
        s = _bdot(qn * hm[h], kn, NT) * 0.125
        s = jnp.where(mask, s, -jnp.inf)
        m = jnp.max(s, axis=-1, keepdims=True)
        p = jnp.exp(s - m)
        l = jnp.sum(p, axis=-1, keepdims=True)
        o = o + hm[h] * _bdot(p / l, v, NN)
        lse = lse + hm[h] * (m + jnp.log(l))
    return o, lse


def att_fwd(p_att, qw, kw, d, g, name):
    S = p_att.shape[0]
    SEG = ATT_SEG
    nblk = SEG // ATT_BLOCK

    def body(p_ref, qw_ref, kw_ref, o_ref, l_ref, q_s, k_ext, v_ext, o_s, l_s):
        seg = pl.program_id(1)

        @pl.when(seg == 0)
        def _():
            k_ext[SEG:, :] = jnp.zeros((SEG, LANES), f32)
            v_ext[SEG:, :] = jnp.zeros((SEG, LANES), f32)

        k_ext[:SEG, :] = k_ext[SEG:, :]
        v_ext[:SEG, :] = v_ext[SEG:, :]
        q_s[...] = p_ref[:, 0:128]
        k_ext[SEG:, :] = p_ref[:, 128:256]
        v_ext[SEG:, :] = p_ref[:, 256:384]
        qw_v, kw_v = qw_ref[...], kw_ref[...]

        def blk(b, carry):
            j, r = b // d, b % d
            qs = j * (ATT_BLOCK * d) + r
            ks = SEG + qs - ATT_BLOCK * d
            o, lse = _att_block(q_s[_rows(qs, ATT_BLOCK, d), :], k_ext[_rows(ks, 2 * ATT_BLOCK, d), :],
                                v_ext[_rows(ks, 2 * ATT_BLOCK, d), :], qw_v, kw_v, (seg > 0) | (j > 0))
            o_s[_rows(qs, ATT_BLOCK, d), :] = o
            l_s[_rows(qs, ATT_BLOCK, d), :] = lse
            return carry

        lax.fori_loop(0, nblk, blk, 0, unroll=4)
        o_ref[...] = o_s[...]
        l_ref[...] = l_s[...]

    vec = pl.BlockSpec((1, LANES), lambda hh, s: (0, 0))
    out = pl.BlockSpec((SEG, LANES), lambda hh, s: (s, hh))
    return _call(body, name, (2, S // SEG), [pl.BlockSpec((SEG, 384), lambda hh, s: (s, MAIN_ATT_BLOCK + 2 * g + hh)), vec, vec],
                 [out, out], [jax.ShapeDtypeStruct((S, 256), f32), jax.ShapeDtypeStruct((S, 256), f32)],
                 scratch=[pltpu.VMEM((SEG, LANES), f32), pltpu.VMEM((2 * SEG, LANES), f32),
                          pltpu.VMEM((2 * SEG, LANES), f32), pltpu.VMEM((SEG, LANES), f32),
                          pltpu.VMEM((SEG, LANES), f32)],
                 sem=("arbitrary", "arbitrary"))(p_att, qw, kw)


def att_bwd(p_att, o, lse, do, dlse, qw, kw, d, g, dp_main, name):
    S = p_att.shape[0]
    SEG = ATT_SEG
    nseg = S // SEG
    nblk = SEG // ATT_BLOCK

    def body(p_ref, pp_ref, o_ref, l_ref, do_ref, dl_ref, qw_ref, kw_ref, _, dp_ref, dqw_ref, dkw_ref,
             q_s, k_ext, v_ext, dq_s, dk_ext, dv_ext):
        hh, i = pl.program_id(0), pl.program_id(1)
        seg = nseg - 1 - i

        @pl.when(i == 0)
        def _():
            dk_ext[...] = jnp.zeros_like(dk_ext)
            dv_ext[...] = jnp.zeros_like(dv_ext)

        @pl.when((i == 0) & (hh == 0))
        def _():
            dqw_ref[...] = jnp.zeros_like(dqw_ref)
            dkw_ref[...] = jnp.zeros_like(dkw_ref)

        dk_ext[SEG:, :] = dk_ext[:SEG, :]
        dv_ext[SEG:, :] = dv_ext[:SEG, :]
        dk_ext[:SEG, :] = jnp.zeros((SEG, LANES), f32)
        dv_ext[:SEG, :] = jnp.zeros((SEG, LANES), f32)
        q_s[...] = p_ref[:, 0:128]
        k_ext[SEG:, :] = p_ref[:, 128:256]
        v_ext[SEG:, :] = p_ref[:, 256:384]
        k_ext[:SEG, :] = pp_ref[:, 128:256]
        v_ext[:SEG, :] = pp_ref[:, 256:384]
        qw_v, kw_v = qw_ref[...], kw_ref[...]

        def blk_pair(i2, carry):
            dqw, dkw = carry
            done = []
            for u in range(2):
                b = 2 * i2 + u
                j, r = b // d, b % d
                qs = j * (ATT_BLOCK * d) + r
                ks = SEG + qs - ATT_BLOCK * d
                has_prev = (seg > 0) | (j > 0)
                qrows, krows = _rows(qs, ATT_BLOCK, d), _rows(ks, 2 * ATT_BLOCK, d)
                dq, dk, dv, dqw_b, dkw_b = _att_block_bwd(
                    q_s[qrows, :], k_ext[krows, :], v_ext[krows, :], o_ref[qrows, :], l_ref[qrows, :],
                    do_ref[qrows, :], dl_ref[qrows, :], qw_v, kw_v, has_prev)
                dqw, dkw = dqw + dqw_b, dkw + dkw_b
                done.append((qrows, krows, dq, dk, dv))
            for qrows, krows, dq, dk, dv in done:
                dq_s[qrows, :] = dq
                dk_ext[krows, :] = dk_ext[krows, :] + dk
                dv_ext[krows, :] = dv_ext[krows, :] + dv
            return dqw, dkw

        zero = jnp.zeros((1, LANES), f32)
        dqw, dkw = lax.fori_loop(0, nblk // 2, blk_pair, (zero, zero))
        dqw_ref[...] += dqw
        dkw_ref[...] += dkw
        dp_ref[:, 0:128] = dq_s[...].astype(bf16)
        dp_ref[:, 128:256] = dk_ext[SEG:, :].astype(bf16)
        dp_ref[:, 256:384] = dv_ext[SEG:, :].astype(bf16)

    rev = lambda i: nseg - 1 - i
    vec = pl.BlockSpec((1, LANES), lambda hh, i: (0, 0))
    blk = MAIN_ATT_BLOCK + 2 * g
    cur = pl.BlockSpec((SEG, 384), lambda hh, i: (rev(i), blk + hh))
    prev = pl.BlockSpec((SEG, 384), lambda hh, i: (jnp.maximum(rev(i) - 1, 0), blk + hh))
    col = pl.BlockSpec((SEG, LANES), lambda hh, i: (rev(i), hh))
    big = pltpu.VMEM((2 * SEG, LANES), f32)
    one = pltpu.VMEM((SEG, LANES), f32)
    return _call(body, name, (2, nseg), [cur, prev, col, col, col, col, vec, vec, _ANY], [cur, vec, vec],
                 [jax.ShapeDtypeStruct((S, MAIN_WIDTH), bf16), jax.ShapeDtypeStruct((1, LANES), f32),
                  jax.ShapeDtypeStruct((1, LANES), f32)],
                 scratch=[one, big, big, one, big, big], sem=("arbitrary", "arbitrary"),
                 aliases={8: 0})(p_att, p_att, o, lse, do, dlse, qw, kw, dp_main)


def conv_fwd(p_ssd, conv_w, conv_b, name):
    S = p_ssd.shape[0]
    tm, C = CONV_ROWS, SSD_XBC

    def body(x_ref, xp_ref, w_ref, b_ref, o_ref):
        first = (pl.program_id(0) == 0)
        ext = jnp.concatenate([jnp.where(first, 0.0, xp_ref[:, 0:C]), x_ref[:, 0:C]], axis=0)
        acc = b_ref[...] + w_ref[3:4, :] * ext[8:, :]
        for k in range(1, 4):
            acc = acc + w_ref[3 - k:4 - k, :] * pltpu.roll(ext, k, 0)[8:, :]
        o_ref[...] = jax.nn.silu(acc)

    return _call(body, name, (S // tm,),
                 [pl.BlockSpec((tm, 1536), lambda i: (i, MAIN_SSD_BLOCK)),
                  pl.BlockSpec((8, 1536), lambda i: (jnp.maximum(i * (tm // 8) - 1, 0), MAIN_SSD_BLOCK)),
                  pl.BlockSpec((4, C), lambda i: (0, 0)), pl.BlockSpec((1, C), lambda i: (0, 0))],
                 pl.BlockSpec((tm, C), lambda i: (i, 0)), jax.ShapeDtypeStruct((S, C), f32),
                 sem=("parallel",))(p_ssd, p_ssd, conv_w, conv_b)


def conv_bwd(p_ssd, dact, ddt, conv_w, conv_b, dp_main, name):
    S = p_ssd.shape[0]
    tm, C = CONV_ROWS, SSD_XBC
    nblk = S // tm

    def body(x_ref, xp_ref, xn_ref, da_ref, dan_ref, ddt_ref, w_ref, b_ref, _, dp_ref, dw_ref, db_ref):
        i = pl.program_id(0)
        rows = tm + 8
        ext = jnp.concatenate([jnp.where(i == 0, 0.0, xp_ref[:, 0:C]), x_ref[:, 0:C], xn_ref[:, 0:C]], axis=0)
        shifted = [ext[8:, :]] + [pltpu.roll(ext, k, 0)[8:, :] for k in range(1, 4)]
        pre = b_ref[...] + w_ref[3:4, :] * shifted[0]
        for k in range(1, 4):
            pre = pre + w_ref[3 - k:4 - k, :] * shifted[k]
        sg = jax.nn.sigmoid(pre)
        dact = jnp.concatenate([da_ref[...], jnp.where(i == nblk - 1, 0.0, dan_ref[...])], axis=0)
        dpre = dact * (sg * (1.0 + pre * (1.0 - sg)))
        dx = w_ref[3:4, :] * dpre[0:tm, :]
        for k in range(1, 4):
            dx = dx + w_ref[3 - k:4 - k, :] * pltpu.roll(dpre, rows - k, 0)[0:tm, :]
        dp_ref[:, 0:C] = dx.astype(bf16)
        dp_ref[:, C:C + 128] = ddt_ref[...].astype(bf16)
        dp_ref[:, C + 128:] = jnp.zeros((tm, 128), bf16)
        dcur = dpre[0:tm, :]
        dws = [jnp.sum(dcur * shifted[3 - j][0:tm, :], axis=0, keepdims=True) for j in range(4)]
        dbs = jnp.sum(dcur, axis=0, keepdims=True)

        @pl.when(i == 0)
        def _():
            dw_ref[...] = jnp.zeros_like(dw_ref)
            db_ref[...] = jnp.zeros_like(db_ref)

        for j in range(4):
            dw_ref[j:j + 1, :] += dws[j]
        db_ref[...] += dbs

    t8 = tm // 8
    blk = MAIN_SSD_BLOCK
    return _call(body, name, (nblk,),
                 [pl.BlockSpec((tm, 1536), lambda i: (i, blk)),
                  pl.BlockSpec((8, 1536), lambda i: (jnp.maximum(i * t8 - 1, 0), blk)),
                  pl.BlockSpec((8, 1536), lambda i: (jnp.minimum((i + 1) * t8, S // 8 - 1), blk)),
                  pl.BlockSpec((tm, C), lambda i: (i, 0)),
                  pl.BlockSpec((8, C), lambda i: (jnp.minimum((i + 1) * t8, S // 8 - 1), 0)),
                  pl.BlockSpec((tm, 128), lambda i: (i, 0)),
                  pl.BlockSpec((4, C), lambda i: (0, 0)), pl.BlockSpec((1, C), lambda i: (0, 0)), _ANY],
                 [pl.BlockSpec((tm, 1536), lambda i: (i, blk)), pl.BlockSpec((4, C), lambda i: (0, 0)),
                  pl.BlockSpec((1, C), lambda i: (0, 0))],
                 [jax.ShapeDtypeStruct((S, MAIN_WIDTH), bf16), jax.ShapeDtypeStruct((4, C), f32),
                  jax.ShapeDtypeStruct((1, C), f32)],
                 sem=("arbitrary",), aliases={8: 0})(p_ssd, p_ssd, p_ssd, dact, dact, ddt, conv_w, conv_b, dp_main)


def _ssd_chunk(xbc, dtr, state, dt_bias, a_log, d_full):
    T = SSD_CHUNK
    r_i = lax.broadcasted_iota(jnp.int32, (T, T), 0)
    c_i = lax.broadcasted_iota(jnp.int32, (T, T), 1)
    tril = c_i <= r_i
    tri = tril.astype(bf16)
    lane = lax.broadcasted_iota(jnp.int32, (1, LANES), 1)
    hm = [(lane < 64).astype(f32), (lane >= 64).astype(f32)]
    column = lambda v, h: jnp.broadcast_to(v[:, h:h + 1], (T, LANES))

    def per_head_lanes(v):
        return jnp.concatenate([jnp.where(lane < 64, column(v, 2 * pp), column(v, 2 * pp + 1)) for pp in range(6)],
                               axis=1)

    xs, bm, cm = xbc[:, :768], xbc[:, 768:1024], xbc[:, 1024:1280]
    dt = _softplus(dtr + dt_bias)
    a_dt = dt * (-jnp.exp(a_log))
    a_cs = _xdot_l(tri, a_dt)
    dt_full = per_head_lanes(dt)
    acs_full = per_head_lanes(a_cs)
    last = lax.broadcasted_iota(jnp.int32, (T, SSD_WIDTH), 0) == T - 1
    tot_full = jnp.sum(jnp.where(last, acs_full, 0.0), axis=0, keepdims=True)
    xdt = xs * dt_full
    xw = xdt * jnp.exp(tot_full - acs_full)
    eacs = jnp.exp(acs_full)
    st_parts, off_parts, diag_parts = [], [], []
    for g in range(2):
        bg, cg = bm[:, 128 * g:128 * (g + 1)], cm[:, 128 * g:128 * (g + 1)]
        cols = slice(384 * g, 384 * (g + 1))
        st_parts.append(_bdot(bg, xw[:, cols], TN))
        off_parts.append(_bdot(cg, state[:, cols], NN))
        cb = _bdot(cg, bg, NT)
        for pp in range(3 * g, 3 * g + 3):
            xp = xdt[:, 128 * pp:128 * (pp + 1)]
            acc = jnp.zeros((T, LANES), f32)
            for hh in range(2):
                a_col = column(a_cs, 2 * pp + hh)
                decay = jnp.where(tril, jnp.exp(jnp.minimum(a_col - a_col.T, 0.0)), 0.0)
                acc = acc + _bdot(cb * decay, xp * hm[hh], NN)
            diag_parts.append(acc)
    new_state = state * jnp.exp(tot_full) + jnp.concatenate(st_parts, axis=1)
    y = jnp.concatenate(diag_parts, axis=1) + jnp.concatenate(off_parts, axis=1) * eacs + xs * d_full
    return y, new_state


def ssd_fwd(xact, p_ssd, dt_bias, a_log, d_full, name):
    S = xact.shape[0]
    T = SSD_CHUNK

    U = SSD_CHUNKS_PER_STEP

    def body(x_ref, p_ref, b_ref, a_ref, d_ref, y_ref, s_ref, state):
        @pl.when(pl.program_id(0) == 0)
        def _():
            state[...] = jnp.zeros_like(state)

        st = state[...]
        for u in range(U):
            rows = slice(T * u, T * (u + 1))
            s_ref[u] = st
            y, st = _ssd_chunk(x_ref[rows, :], p_ref[rows, :], st, b_ref[...], a_ref[...], d_ref[...])
            y_ref[rows, :] = y
        state[...] = st

    vec = lambda n: pl.BlockSpec((1, n), lambda i: (0, 0))
    return _call(body, name, (S // (U * T),),
                 [pl.BlockSpec((U * T, SSD_XBC), lambda i: (i, 0)),
                  pl.BlockSpec((U * T, 128), lambda i: (i, MAIN_DT_BLOCK)), vec(128), vec(128), vec(768)],
                 [pl.BlockSpec((U * T, 768), lambda i: (i, 0)), pl.BlockSpec((U, T, 768), lambda i: (i, 0, 0))],
                 [jax.ShapeDtypeStruct((S, 768), f32), jax.ShapeDtypeStruct((S // T, T, 768), f32)],
                 scratch=[pltpu.VMEM((T, 768), f32)], sem=("arbitrary",))(xact, p_ssd, dt_bias, a_log, d_full)


def ssd_bwd(xact, p_ssd, states, dy, dt_bias, a_log, d_full, name):
    S = xact.shape[0]
    T = SSD_CHUNK
    U = SSD_CHUNKS_PER_STEP
    nc = S // (U * T)

    def body(x_ref, p_ref, s_ref, dy_ref, b_ref, a_ref, d_ref, dx_ref, ddt_ref, db_ref, da_ref, dd_ref, dstate):
        i = pl.program_id(0)

        @pl.when(i == 0)
        def _():
            for ref in (dstate, db_ref, da_ref, dd_ref):
                ref[...] = jnp.zeros_like(ref)

        dst = dstate[...]
        for u in reversed(range(U)):
            rows = slice(T * u, T * (u + 1))
            _, vjp = jax.vjp(_ssd_chunk, x_ref[rows, :], p_ref[rows, :], s_ref[u], b_ref[...], a_ref[...], d_ref[...])
            dx, ddt, dst, db, da, dd = vjp((dy_ref[rows, :], dst))
            dx_ref[rows, :] = dx
            ddt_ref[rows, :] = ddt
            db_ref[...] += db
            da_ref[...] += da
            dd_ref[...] += dd
        dstate[...] = dst

    rev = lambda i: nc - 1 - i
    vec = lambda n: pl.BlockSpec((1, n), lambda i: (0, 0))
    return _call(body, name, (nc,),
                 [pl.BlockSpec((U * T, SSD_XBC), lambda i: (rev(i), 0)),
                  pl.BlockSpec((U * T, 128), lambda i: (rev(i), MAIN_DT_BLOCK)),
                  pl.BlockSpec((U, T, 768), lambda i: (rev(i), 0, 0)), pl.BlockSpec((U * T, 768), lambda i: (rev(i), 0)),
                  vec(128), vec(128), vec(768)],
                 [pl.BlockSpec((U * T, SSD_XBC), lambda i: (rev(i), 0)), pl.BlockSpec((U * T, 128), lambda i: (rev(i), 0)),
                  vec(128), vec(128), vec(768)],
                 [jax.ShapeDtypeStruct((S, SSD_XBC), f32), jax.ShapeDtypeStruct((S, 128), f32),
                  jax.ShapeDtypeStruct((1, 128), f32), jax.ShapeDtypeStruct((1, 128), f32),
                  jax.ShapeDtypeStruct((1, 768), f32)],
                 scratch=[pltpu.VMEM((T, 768), f32)],
                 sem=("arbitrary",))(xact, p_ssd, states, dy, dt_bias, a_log, d_full)


def _tail_fn(ys5, pt, o0, o1, o2, l0, l1, l2, yssd, glu_b, nw, pr_glu, pr_a, pr_b, pr_c, x, weights):
    glu_w, pa, pb, pc, wo = weights
    gates = jax.nn.sigmoid(pt[:, :3072])
    za, zb, zc = pt[:, 3072:3584], pt[:, 3584:3840], pt[:, 3840:4608]
    g = jax.nn.gelu(ys5)
    ya = g * jax.nn.sigmoid(_cdot(g, glu_w, NN) + glu_b + pr_glu) * jax.nn.silu(za)
    m = jnp.maximum(jnp.maximum(l0, l1), l2)
    e0, e1, e2 = jnp.exp(l0 - m), jnp.exp(l1 - m), jnp.exp(l2 - m)
    yb = (e0 * o0 + e1 * o1 + e2 * o2) / (e0 + e1 + e2) * jax.nn.silu(zb)
    yc = _rms(yssd * jax.nn.silu(zc), nw)
    merged = (gates[:, :1024] * (_cdot(ya, pa, NN) + pr_a) + gates[:, 1024:2048] * (_cdot(yb, pb, NN) + pr_b)
              + gates[:, 2048:] * (_cdot(yc, pc, NN) + pr_c))
    out = x + _cdot(merged, wo, NN)
    return out, (g, ya, yb, yc, merged)


def _tail_specs(tm):
    row = lambda n: pl.BlockSpec((tm, n), lambda i: (i, 0))
    full = lambda a, b: pl.BlockSpec((a, b), lambda i: (0, 0))
    acts = [row(512), row(4608)] + [row(256)] * 6 + [row(768), row(D_MODEL)]
    consts = [full(1, 512), full(1, 768), full(512, 512), full(512, D_MODEL), full(256, D_MODEL),
              full(768, D_MODEL), full(D_MODEL, D_MODEL)]
    return row, full, acts, consts


def tail_fwd(ys5, pt, os_, ls_, yssd, x, glu_b, nw, weights, name):
    S = x.shape[0]
    tm = TAIL_ROWS
    row, full, acts, consts = _tail_specs(tm)

    def body(ys5_ref, pt_ref, o0, o1, o2, l0, l1, l2, yssd_ref, x_ref, gb_ref, nw_ref, gw, pa, pb, pc, wo, out_ref):
        z = lambda n: jnp.zeros((tm, n), f32)
        out, _ = _tail_fn(ys5_ref[...], pt_ref[...], o0[...], o1[...], o2[...], l0[...], l1[...], l2[...],
                          yssd_ref[...], gb_ref[...], nw_ref[...], z(512), z(D_MODEL), z(D_MODEL), z(D_MODEL),
                          x_ref[...], (gw[...], pa[...], pb[...], pc[...], wo[...]))
        out_ref[...] = out

    return _call(body, name, (S // tm,), acts + consts, row(D_MODEL), jax.ShapeDtypeStruct((S, D_MODEL), f32),
                 sem=("parallel",))(ys5, pt, *os_, *ls_, yssd, x, glu_b, nw, *weights)


def tail_bwd(ys5, pt, os_, ls_, yssd, dout, glu_b, nw, weights, name):
    S = dout.shape[0]
    tm = TAIL_ROWS
    row, full, acts, consts = _tail_specs(tm)

    def body(ys5_ref, pt_ref, o0, o1, o2, l0, l1, l2, yssd_ref, dout_ref, gb_ref, nw_ref, gw, pa, pb, pc, wo,
             dys5_ref, dpt_ref, do0, do1, do2, dl0, dl1, dl2, dyssd_ref, dgb_ref, dnw_ref,
             g_ref, ya_ref, yb_ref, yc_ref, mg_ref, dglu_ref, dpa_ref, dpb_ref, dpc_ref):
        z = lambda n: jnp.zeros((tm, n), f32)
        w = (gw[...], pa[...], pb[...], pc[...], wo[...])
        fn = lambda *a: _tail_fn(*a, z(D_MODEL), w)
        _, vjp, aux = jax.vjp(fn, ys5_ref[...], pt_ref[...], o0[...], o1[...], o2[...], l0[...], l1[...], l2[...],
                              yssd_ref[...], gb_ref[...], nw_ref[...], z(512), z(D_MODEL), z(D_MODEL), z(D_MODEL),
                              has_aux=True)
        (dys5, dpt, d0, d1, d2, e0, e1, e2, dyssd, dgb, dnw, dglu, dpa, dpb, dpc) = vjp(dout_ref[...])
        dys5_ref[...] = dys5
        dpt_ref[...] = dpt.astype(bf16)
        for ref, val in ((do0, d0), (do1, d1), (do2, d2), (dl0, e0), (dl1, e1), (dl2, e2)):
            ref[...] = val
        dyssd_ref[...] = dyssd
        g, ya, yb, yc, merged = aux
        for ref, val in ((g_ref, g), (ya_ref, ya), (yb_ref, yb), (yc_ref, yc), (mg_ref, merged),
                         (dglu_ref, dglu), (dpa_ref, dpa), (dpb_ref, dpb), (dpc_ref, dpc)):
            ref[...] = val.astype(bf16)

        @pl.when(pl.program_id(0) == 0)
        def _():
            dgb_ref[...] = dgb
            dnw_ref[...] = dnw

        @pl.when(pl.program_id(0) > 0)
        def _():
            dgb_ref[...] += dgb
            dnw_ref[...] += dnw

    sd = lambda n, dt=f32: jax.ShapeDtypeStruct((S, n), dt)
    out_specs = ([row(512), row(4608)] + [row(256)] * 6 + [row(768), full(1, 512), full(1, 768)]
                 + [row(512), row(512), row(256), row(768), row(D_MODEL), row(512)] + [row(D_MODEL)] * 3)
    out_shape = ([sd(512), sd(MAIN_WIDTH, bf16)] + [sd(256)] * 6 + [sd(768), jax.ShapeDtypeStruct((1, 512), f32),
                                                          jax.ShapeDtypeStruct((1, 768), f32)]
                 + [sd(512, bf16), sd(512, bf16), sd(256, bf16), sd(768, bf16), sd(D_MODEL, bf16), sd(512, bf16)]
                 + [sd(D_MODEL, bf16)] * 3)
    return _call(body, name, (S // tm,), acts + consts, out_specs, out_shape,
                 sem=("arbitrary",))(ys5, pt, *os_, *ls_, yssd, dout, glu_b, nw, *weights)


def _in_proj_segments(shards):
    dtype = shards[0].dtype

    def c(a, b):
        parts = []
        for k, sh in enumerate(shards):
            lo, hi = max(a, W_IN_SHARD * k), min(b, W_IN_SHARD * (k + 1))
            if lo < hi:
                parts.append(sh[:, lo - W_IN_SHARD * k:hi - W_IN_SHARD * k])
        return parts[0] if len(parts) == 1 else jnp.concatenate(parts, axis=1)

    atts = []
    for g in range(3):
        parts = []
        for hh in range(2):
            o = 64 * (4 * g + 2 * hh)
            parts += [c(_C_Q + o, _C_Q + o + 128), c(_C_K + o, _C_K + o + 128), c(_C_V + o, _C_V + o + 128)]
        atts.append(jnp.concatenate(parts, axis=1))
    ssd = jnp.concatenate([c(_C_XBC, _C_ZC), jnp.zeros((D_MODEL, 1536 - (_C_ZC - _C_XBC)), dtype)], axis=1)
    tail = jnp.concatenate([c(_C_GATE, _C_END), c(_C_ZA, _C_Q), c(_C_ZB, _C_XBC), c(_C_ZC, _C_GATE)], axis=1)
    return [c(_C_UA, _C_ZA), jnp.concatenate([tail, ssd] + atts, axis=1)]


def _in_proj_grad(ds5, dmain):
    dtail, dssd = dmain[:, :4608], dmain[:, 4608:6144]
    datts = [dmain[:, 6144 + 768 * g:6144 + 768 * (g + 1)] for g in range(3)]
    pick = lambda off: [datts[g][:, 384 * hh + off:384 * hh + off + 128] for g in range(3) for hh in range(2)]
    pieces = ([ds5, dtail[:, 3072:3584]] + pick(0) + pick(128) + pick(256)
              + [dtail[:, 3584:3840], dssd[:, :_C_ZC - _C_XBC], dtail[:, 3840:4608], dtail[:, :3072]])
    shards, start = [[] for _ in range(4)], 0
    for piece in pieces:
        width = piece.shape[1]
        for k in range(4):
            lo, hi = max(start, W_IN_SHARD * k), min(start + width, W_IN_SHARD * (k + 1))
            if lo < hi:
                shards[k].append(piece[:, lo - start:hi - start])
        start += width
    return jnp.stack([jnp.concatenate(s, axis=1) for s in shards])


def _prep_layer(p):
    q = {}
    q["segs"] = [s.astype(bf16) for s in _in_proj_segments(p["w_in"])]
    disc = _s5_discretize(p["s5_a_re"], p["s5_a_im"], p["s5_log_step"], p["s5_b_re"], p["s5_b_im"],
                          p["s5_c_re"], p["s5_c_im"])
    q["s5"] = disc
    q["pw"] = _lam_powers(disc[0], disc[1])
    q["s5_d"] = p["s5_d"].reshape(1, 512)
    q["qw"] = jnp.tile(p["q_norm_w"], 2).reshape(1, LANES)
    q["kw"] = jnp.tile(p["k_norm_w"], 2).reshape(1, LANES)
    q["conv_w"] = p["conv_w"]
    q["conv_b"] = p["conv_b"].reshape(1, SSD_XBC)
    pad = lambda v: jnp.pad(v, (0, LANES - v.shape[0])).reshape(1, LANES)
    q["dt_bias"], q["a_log"] = pad(p["dt_bias"]), pad(p["ssd_a_log"])
    q["d_full"] = jnp.repeat(p["ssd_d"], 64).reshape(1, SSD_WIDTH)
    q["glu_b"] = p["s5_glu_b"].reshape(1, 512)
    q["nw"] = p["ssd_norm_w"].reshape(1, SSD_WIDTH)
    q["norm_w"] = p["norm_w"].reshape(1, D_MODEL)
    q["tailw"] = tuple(p[n].astype(bf16) for n in ("s5_glu_w", "proj_a", "proj_b", "proj_c", "w_out"))
    return q


_DILATIONS = (1, 4, 16)


def layer_fwd(x, q, tag):
    h = rms_fwd(x, q["norm_w"], f"rms_fwd{tag}")
    p_s5, p_main = [mm_nn(h, w, f"inproj{k}{tag}") for k, w in enumerate(q["segs"])]
    _, _, w_re, w_im, c_re, c_im = q["s5"]
    ys5, h_re, h_im = s5_fwd(p_s5, *q["pw"], w_re, w_im, c_re, c_im, q["s5_d"], f"s5_fwd{tag}")
    os_, ls_ = [], []
    for g, d in enumerate(_DILATIONS):
        o, l = att_fwd(p_main, q["qw"], q["kw"], d, g, f"att_fwd{g}{tag}")
        os_.append(o)
        ls_.append(l)
    xact = conv_fwd(p_main, q["conv_w"], q["conv_b"], f"conv_fwd{tag}")
    yssd, states = ssd_fwd(xact, p_main, q["dt_bias"], q["a_log"], q["d_full"], f"ssd_fwd{tag}")
    out = tail_fwd(ys5, p_main, os_, ls_, yssd, x, q["glu_b"], q["nw"], q["tailw"], f"tail_fwd{tag}")
    saved = dict(x=x, h=h, p_s5=p_s5, p_main=p_main, ys5=ys5, h_re=h_re, h_im=h_im,
                 os=os_, ls=ls_, xact=xact, yssd=yssd, states=states)
    return out, saved


def layer_bwd(dout, sv, q, p, tag):
    S = dout.shape[0]
    (dys5, dp_main, do0, do1, do2, dl0, dl1, dl2, dyssd, dglu_b, dnw, g_b, ya_b, yb_b, yc_b, mg_b, dglu_b16,
     dpa_b, dpb_b, dpc_b) = tail_bwd(sv["ys5"], sv["p_main"], sv["os"], sv["ls"], sv["yssd"], dout, q["glu_b"],
                                     q["nw"], q["tailw"], f"tail_bwd{tag}")
    grads = {}
    grads["s5_glu_w"] = mm_tn(g_b, dglu_b16, f"dglu_w{tag}")
    grads["proj_a"] = mm_tn(ya_b, dpa_b, f"dproj_a{tag}")
    grads["proj_b"] = mm_tn(yb_b, dpb_b, f"dproj_b{tag}")
    grads["proj_c"] = mm_tn(yc_b, dpc_b, f"dproj_c{tag}")
    grads["w_out"] = mm_tn(mg_b, dout, f"dw_out{tag}")
    grads["s5_glu_b"] = dglu_b.reshape(512)
    grads["ssd_norm_w"] = dnw.reshape(SSD_WIDTH)

    dxact, ddt, ddt_bias, da_log, dd_full = ssd_bwd(sv["xact"], sv["p_main"], sv["states"], dyssd, q["dt_bias"],
                                                    q["a_log"], q["d_full"], f"ssd_bwd{tag}")
    dp_main, dconv_w, dconv_b = conv_bwd(sv["p_main"], dxact, ddt, q["conv_w"], q["conv_b"], dp_main,
                                         f"conv_bwd{tag}")
    grads["dt_bias"] = ddt_bias[0, :12]
    grads["ssd_a_log"] = da_log[0, :12]
    grads["ssd_d"] = dd_full.reshape(12, 64).sum(axis=1)
    grads["conv_w"] = dconv_w
    grads["conv_b"] = dconv_b.reshape(SSD_XBC)

    dqw, dkw = 0.0, 0.0
    for g, d in enumerate(_DILATIONS):
        dp_main, a, b = att_bwd(sv["p_main"], sv["os"][g], sv["ls"][g], (do0, do1, do2)[g], (dl0, dl1, dl2)[g],
                                q["qw"], q["kw"], d, g, dp_main, f"att_bwd{g}{tag}")
        dqw, dkw = dqw + a, dkw + b
    grads["q_norm_w"] = dqw.reshape(2, 64).sum(axis=0)
    grads["k_norm_w"] = dkw.reshape(2, 64).sum(axis=0)

    _, _, w_re, w_im, c_re, c_im = q["s5"]
    dp_s5, dwre, dwim, dcre, dcim, dlam_re, dlam_im, dd = s5_bwd(
        dys5, sv["p_s5"], sv["h_re"], sv["h_im"], *q["pw"], w_re, w_im, c_re, c_im, q["s5_d"], f"s5_bwd{tag}")
    s5_names = ("s5_a_re", "s5_a_im", "s5_log_step", "s5_b_re", "s5_b_im", "s5_c_re", "s5_c_im")
    _, disc_vjp = jax.vjp(_s5_discretize, *[p[n] for n in s5_names])
    for n, gr in zip(s5_names, disc_vjp((dlam_re, dlam_im, dwre, dwim, dcre, dcim))):
        grads[n] = gr
    grads["s5_d"] = dd.reshape(512)

    dsegs = [dp_s5, dp_main]
    dws = [mm_tn(sv["h"], ds, f"dw_in{k}{tag}") for k, ds in enumerate(dsegs)]
    grads["w_in"] = _in_proj_grad(*dws)
    dh = None
    for k, (ds, w) in enumerate(zip(dsegs, q["segs"])):
        dh = mm_nt(ds, w, f"dh{k}{tag}", acc=dh)
    dx, dnorm_w = rms_bwd(sv["x"], q["norm_w"], dh, dout, f"rms_bwd{tag}")
    grads["norm_w"] = dnorm_w.reshape(D_MODEL)
    return dx, grads


def _exchange(name, scatter=(), gather=(), sibling=(), sibling_both=False):
    scatter, gather, sibling = list(scatter), list(gather), list(sibling)
    chip_xs = scatter + gather
    ns, nc, nb = len(scatter), len(chip_xs), len(sibling)
    n = nc + nb

    def body(*refs):
        x_refs, o_refs, send_sems, recv_sems = refs[:n], refs[n:2 * n], refs[2 * n], refs[2 * n + 1]
        mx, my, mc = lax.axis_index("x"), lax.axis_index("y"), lax.axis_index("c")
        me = 2 * mx + my
        copies = []
        for a in range(nc):
            for t, (px, py) in enumerate(((1 - mx, my), (mx, 1 - my), (1 - mx, 1 - my))):
                src = x_refs[a].at[2 * px + py] if a < ns else x_refs[a]
                copies.append(pltpu.make_async_remote_copy(
                    src_ref=src, dst_ref=o_refs[a].at[me], send_sem=send_sems.at[3 * a + t],
                    recv_sem=recv_sems.at[3 * a + t], device_id=(px, py, mc), device_id_type=pl.DeviceIdType.MESH))
        for b in range(nc, n):
            k = 3 * nc + b - nc
            copies.append(pltpu.make_async_remote_copy(
                src_ref=x_refs[b], dst_ref=o_refs[b].at[mc] if sibling_both else o_refs[b], send_sem=send_sems.at[k],
                recv_sem=recv_sems.at[k], device_id=(mx, my, 1 - mc), device_id_type=pl.DeviceIdType.MESH))
        for cp in copies:
            cp.start()
        for cp in copies:
            cp.wait()

    shapes = ([(4,) + tuple(x.shape[1:]) for x in scatter] + [(4,) + tuple(x.shape) for x in gather]
              + [((2,) if sibling_both else ()) + tuple(x.shape) for x in sibling])
    xs = chip_xs + sibling
    outs = pl.pallas_call(
        body, name=name, in_specs=[_ANY] * n, out_specs=[_ANY] * n,
        out_shape=[jax.ShapeDtypeStruct(s, x.dtype) for s, x in zip(shapes, xs)],
        scratch_shapes=[pltpu.SemaphoreType.DMA((3 * nc + nb,)), pltpu.SemaphoreType.DMA((3 * nc + nb,))],
    )(*xs)
    me, c = 2 * lax.axis_index("x") + lax.axis_index("y"), lax.axis_index("c")
    fixed = []
    for a, (o, x) in enumerate(zip(outs, xs)):
        if a < ns:
            o = lax.dynamic_update_index_in_dim(o, lax.dynamic_index_in_dim(x, me, 0, keepdims=True), me, 0)
        elif a < nc:
            o = lax.dynamic_update_index_in_dim(o, x[None], me, 0)
        elif sibling_both:
            o = lax.dynamic_update_index_in_dim(o, x[None], c, 0)
        fixed.append(o)
    return fixed[:ns], fixed[ns:nc], fixed[nc:]


def _rows_tile(rows, row_bytes, budget=1 << 20):
    return next(t for t in (512, 256, 128, 64, 32, 16, 8) if rows % t == 0 and t * row_bytes <= budget)


def _padded_row_bytes(cols):
    return -(-cols // LANES) * LANES * 4


def _add2(a, b, name, out_dtype=f32):
    R, C = a.shape
    tr = _rows_tile(R, _padded_row_bytes(C))

    def body(a_ref, b_ref, o_ref):
        o_ref[...] = (a_ref[...] + b_ref[...]).astype(out_dtype)

    spec = pl.BlockSpec((tr, C), lambda i: (i, 0))
    return _call(body, name, (R // tr,), [spec, spec], spec, jax.ShapeDtypeStruct((R, C), out_dtype),
                 sem=("parallel",))(a, b)


def _sum4(x, name):
    R = x.shape[1]
    tr = _tile(R, (512, 256, 128))

    def body(x_ref, o_ref):
        p = [x_ref[j].astype(f32) for j in range(4)]
        o_ref[...] = ((p[0] + p[1]) + p[2]) + p[3]

    return _call(body, name, (R // tr,), [pl.BlockSpec((4, tr, LANES), lambda i: (0, i, 0))],
                 pl.BlockSpec((tr, LANES), lambda i: (i, 0)), jax.ShapeDtypeStruct((R, LANES), f32),
                 sem=("parallel",))(x)


def _adamw(g_parts, w, m, v, name):
    stacked = not isinstance(g_parts, (tuple, list))
    k = g_parts.shape[0] if stacked else len(g_parts)
    R, C = w.shape
    tr = _rows_tile(R, _padded_row_bytes(C))
    c1 = 1.0 - ADAM_B1 ** ADAM_STEP
    c2 = 1.0 - ADAM_B2 ** ADAM_STEP

    def body(*refs):
        w_ref, m_ref, v_ref, g_ref, d_ref, nm_ref, nv_ref = refs[-7:]
        if stacked:
            g = refs[0][0].astype(f32)
            for j in range(1, k):
                g = g + refs[0][j].astype(f32)
        else:
            g = refs[0][...]
            for r in refs[1:k]:
                g = g + r[...]
        m = ADAM_B1 * m_ref[...] + (1.0 - ADAM_B1) * g
        v = ADAM_B2 * v_ref[...] + (1.0 - ADAM_B2) * (g * g)
        g_ref[...] = g
        nm_ref[...] = m
        nv_ref[...] = v
        d_ref[...] = -ADAM_LR * ((m / c1) / (jnp.sqrt(v / c2) + ADAM_EPS) + ADAM_WD * w_ref[...])

    spec = pl.BlockSpec((tr, C), lambda i: (i, 0))
    sd = jax.ShapeDtypeStruct((R, C), f32)
    g_specs = [pl.BlockSpec((k, tr, C), lambda i: (0, i, 0))] if stacked else [spec] * k
    g_args = [g_parts] if stacked else list(g_parts)
    return _call(body, name, (R // tr,), g_specs + [spec] * 3, [spec] * 4, [sd] * 4,
                 sem=("parallel",))(*g_args, w, m, v)


def _pack(arrays):
    flat = jnp.concatenate([a.reshape(-1) for a in arrays])
    unit = PACK_ROWS * LANES
    n = -(-flat.shape[0] // unit) * unit
    return jnp.pad(flat, (0, n - flat.shape[0])).reshape(n // LANES, LANES)


def _unpack(buf, shapes):
    flat = buf.reshape(-1)
    out, off = [], 0
    for s in shapes:
        n = 1
        for dim in s:
            n *= dim
        out.append(flat[off:off + n].reshape(s))
        off += n
    return out


def _to_shards(full, axis):
    s = full.shape
    t = full.reshape(s[:axis] + (4, s[axis] // 4) + s[axis + 1:])
    return jnp.moveaxis(t, axis, 0)


def _from_shards(sh, axis):
    t = jnp.moveaxis(sh, 0, axis)
    s = t.shape
    return t.reshape(s[:axis] + (s[axis] * s[axis + 1],) + s[axis + 2:])


def kernel(x, norm_w, w_in, s5_a_re, s5_a_im, s5_log_step, s5_b_re, s5_b_im, s5_c_re, s5_c_im, s5_d, s5_glu_w, s5_glu_b, q_norm_w, k_norm_w, conv_w, conv_b, dt_bias, ssd_a_log, ssd_d, ssd_norm_w, proj_a, proj_b, proj_c, w_out, loss_target, m_norm_w, m_w_in, m_s5_a_re, m_s5_a_im, m_s5_log_step, m_s5_b_re, m_s5_b_im, m_s5_c_re, m_s5_c_im, m_s5_d, m_s5_glu_w, m_s5_glu_b, m_q_norm_w, m_k_norm_w, m_conv_w, m_conv_b, m_dt_bias, m_ssd_a_log, m_ssd_d, m_ssd_norm_w, m_proj_a, m_proj_b, m_proj_c, m_w_out, v_norm_w, v_w_in, v_s5_a_re, v_s5_a_im, v_s5_log_step, v_s5_b_re, v_s5_b_im, v_s5_c_re, v_s5_c_im, v_s5_d, v_s5_glu_w, v_s5_glu_b, v_q_norm_w, v_k_norm_w, v_conv_w, v_conv_b, v_dt_bias, v_ssd_a_log, v_ssd_d, v_ssd_norm_w, v_proj_a, v_proj_b, v_proj_c, v_w_out):
    given = dict(locals())
    W = {n: given[n] for n in _WEIGHTS}
    M = {n: given["m_" + n] for n in _WEIGHTS}
    V = {n: given["v_" + n] for n in _WEIGHTS}
    n_layers = norm_w.shape[0]
    assert n_layers == 2
    c = lax.axis_index("c")

    mine_of = lambda t: lax.dynamic_index_in_dim(t, c, 0, keepdims=False)
    as_payload = lambda n: lax.bitcast_convert_type(W[n], bf16) if n == "conv_w" else W[n].astype(bf16)
    payload_shapes = [W[n].shape + ((2,) if n == "conv_w" else ()) for n, _ in _SHARDED]
    wpack = _pack([as_payload(n) for n, _ in _SHARDED])
    half_rows = wpack.shape[0] // 2
    _, (pack_half, w_in_mine_layer), _ = _exchange(
        "gather_weights", gather=[lax.dynamic_slice_in_dim(wpack, c * half_rows, half_rows),
                                  mine_of(w_in).astype(bf16)])
    _, _, (w_in_layers, pack_halves) = _exchange("share_weights", sibling=[w_in_mine_layer, pack_half],
                                                 sibling_both=True)
    gathered = jnp.moveaxis(pack_halves, 0, 1).reshape(4, 2 * half_rows, LANES)
    full = dict(W)
    pieces = [_unpack(gathered[j], payload_shapes) for j in range(4)]
    for k, (n, axis) in enumerate(_SHARDED):
        sh = jnp.stack([pieces[j][k] for j in range(4)])
        full[n] = _from_shards(lax.bitcast_convert_type(sh, f32) if n == "conv_w" else sh, axis)

    xs = x[0]
    qs, saves = [], []
    act = xs
    for l in range(n_layers):
        p = {n: full[n][l] for n in _WEIGHTS if n != "w_in"}
        p["w_in"] = [w_in_layers[l, k] for k in range(4)]
        q = _prep_layer(p)
        act, sv = layer_fwd(act, q, f"_l{l}")
        qs.append((q, p))
        saves.append(sv)
    dact, lsum = loss_and_grad(act, loss_target[0], "loss")
    loss = lax.psum(lsum[0, 0], ("x", "y", "c"))
    layer_grads = [None] * n_layers
    for l in reversed(range(n_layers)):
        q, p = qs[l]
        dact, layer_grads[l] = layer_bwd(dact, saves[l], q, p, f"_l{l}")
    grad_x = dact[None]
    G = {n: jnp.stack([layer_grads[l][n] for l in range(n_layers)]) for n in _WEIGHTS if n != "w_in"}

    repl_shapes = [W[n].shape for n in _REPL]
    small = _pack([G[n] for n in _REPL])
    quarter = small.shape[0] // 4
    big = [_to_shards(G[n], axis).reshape(4, -1) for n, axis in _SHARDED]
    big = jnp.concatenate(big, axis=1)
    unit = PACK_ROWS * LANES
    nbig = -(-big.shape[1] // unit) * unit
    big = jnp.pad(big, ((0, 0), (0, nbig - big.shape[1]))).reshape(4, nbig // LANES, LANES)
    gpack = jnp.concatenate([big, small.reshape(4, quarter, LANES)], axis=1)
    rbig = nbig // LANES
    g0, g1 = layer_grads[0]["w_in"], layer_grads[1]["w_in"]

    (landed_pack,), _, (from_sibling,) = _exchange(
        "swap_w_in_grads_and_scatter_grads", scatter=[gpack.astype(bf16)], sibling=[jnp.where(c == 0, g1, g0)])
    flat = lambda t: t.reshape(4 * D_MODEL, W_IN_SHARD)
    shards = _add2(flat(jnp.where(c == 0, g0, g1)), flat(from_sibling), "sum_cores_w_in", out_dtype=bf16)
    mine = _sum4(landed_pack, "sum_chips")

    (landed,), _, (other,) = _exchange(
        "scatter_w_in_grads_and_swap_cores", scatter=[shards.reshape(4, D_MODEL, W_IN_SHARD)], sibling=[mine])
    w_in_mine = _adamw(landed, mine_of(w_in), mine_of(m_w_in), mine_of(v_w_in), "adamw_w_in")
    gq = _add2(mine[rbig:], other[rbig:], "sum_cores_small")

    _, (gsmall,), w_in_out = _exchange(
        "share_w_in_updates_and_gather_small", gather=[gq], sibling=w_in_mine, sibling_both=True)
    gsmall = gsmall.reshape(4 * quarter, LANES)

    wp, mp, vp = (_pack([T[n] for n, _ in _SHARDED]) for T in (W, M, V))
    outs_big = _adamw((mine[:rbig], other[:rbig]), wp, mp, vp, "adamw_sharded")
    big_out = [_unpack(o, [W[n].shape for n, _ in _SHARDED]) for o in outs_big]
    ws, ms, vs = (_pack([T[n] for n in _REPL]) for T in (W, M, V))
    outs_small = _adamw((gsmall,), ws, ms, vs, "adamw_replicated")
    small_out = [_unpack(o, repl_shapes) for o in outs_small]

    res = [dict(), dict(), dict(), dict()]
    for kind in range(4):
        res[kind]["w_in"] = w_in_out[kind]
        for k, (n, _) in enumerate(_SHARDED):
            res[kind][n] = big_out[kind][k]
        for k, n in enumerate(_REPL):
            res[kind][n] = small_out[kind][k]
    return (loss, grad_x, *[res[0][n] for n in _WEIGHTS], *[res[1][n] for n in _WEIGHTS],
            *[res[2][n] for n in _WEIGHTS], *[res[3][n] for n in _WEIGHTS])
```

```python
import functools

import jax
import jax.numpy as jnp
from jax import lax
from jax.experimental import pallas as pl
from jax.experimental.pallas import tpu as pltpu

f32 = jnp.float32
bf16 = jnp.bfloat16

D_MODEL = 1024
RMS_EPS = 1e-6
V7X_VMEM_LIMIT = 60 * 1024 * 1024
LANES = 128
NN, NT, TN = ((1,), (0,)), ((1,), (1,)), ((0,), (0,))

S5_STATES = 2048
S5_ROWS = 512
ATT_SEG = 2048
ATT_BLOCK = 128
SSD_CHUNK = 128
SSD_CHUNKS_PER_STEP = 2
SSD_WIDTH = 768
SSD_XBC = 1280
CONV_ROWS = 512
TAIL_ROWS = 256

ADAM_LR, ADAM_B1, ADAM_B2, ADAM_EPS, ADAM_WD, ADAM_STEP = 0.001, 0.9, 0.999, 1e-08, 0.01, 10

_C_UA, _C_ZA, _C_Q, _C_K, _C_V, _C_ZB, _C_XBC, _C_DT, _C_ZC, _C_GATE, _C_END = (
    0, 512, 1024, 1792, 2560, 3328, 3584, 4864, 4876, 5644, 8716)

_SHARDED = (("s5_glu_w", 1), ("conv_w", 2), ("proj_a", 2), ("proj_b", 2), ("proj_c", 2), ("w_out", 1))
W_IN_SHARD = 2179
_REPL = ("norm_w", "s5_a_re", "s5_a_im", "s5_log_step", "s5_b_re", "s5_b_im", "s5_c_re", "s5_c_im", "s5_d",
         "s5_glu_b", "q_norm_w", "k_norm_w", "conv_b", "dt_bias", "ssd_a_log", "ssd_d", "ssd_norm_w")
_WEIGHTS = ("norm_w", "w_in", "s5_a_re", "s5_a_im", "s5_log_step", "s5_b_re", "s5_b_im", "s5_c_re", "s5_c_im",
            "s5_d", "s5_glu_w", "s5_glu_b", "q_norm_w", "k_norm_w", "conv_w", "conv_b", "dt_bias", "ssd_a_log",
            "ssd_d", "ssd_norm_w", "proj_a", "proj_b", "proj_c", "w_out")
PACK_ROWS = 512


def _dot(a, b, dims):
    return lax.dot_general(a.astype(bf16), b.astype(bf16), (dims, ((), ())), preferred_element_type=f32)


_ANY = pl.BlockSpec(memory_space=pl.ANY)

MAIN_WIDTH = 8448
MAIN_SSD_BLOCK = 3
MAIN_DT_BLOCK = 46
MAIN_ATT_BLOCK = 16


def _call(body, name, grid, in_specs, out_specs, out_shape, scratch=(), sem=None, aliases=None):
    return pl.pallas_call(
        body, name=name, grid=grid, in_specs=in_specs, out_specs=out_specs, out_shape=out_shape,
        scratch_shapes=list(scratch), input_output_aliases=aliases or {},
        compiler_params=pltpu.CompilerParams(dimension_semantics=sem, vmem_limit_bytes=V7X_VMEM_LIMIT))


def _tile(n, options=(1024, 768, 512, 384, 256, 128)):
    return next(t for t in options if n % t == 0)


@functools.partial(jax.custom_vjp, nondiff_argnums=(2,))
def _bdot(a, b, dims):
    return _dot(a, b, dims)


def _bdot_fwd(a, b, dims):
    return _dot(a, b, dims), (a, b)


def _bdot_bwd(dims, res, g):
    a, b = res
    if dims == NN:
        da, db = _dot(g, b, NT), _dot(a, g, TN)
    elif dims == NT:
        da, db = _dot(g, b, NN), _dot(g, a, TN)
    else:
        da, db = _dot(b, g, NT), _dot(a, g, NN)
    return da.astype(a.dtype), db.astype(b.dtype)


_bdot.defvjp(_bdot_fwd, _bdot_bwd)


@functools.partial(jax.custom_vjp, nondiff_argnums=(2,))
def _cdot(a, w, dims):
    return _dot(a, w, dims)


def _cdot_fwd(a, w, dims):
    return _dot(a, w, dims), w


def _cdot_bwd(dims, w, g):
    da = _dot(g, w, NT) if dims == NN else _dot(g, w, NN)
    return da, jnp.zeros_like(w)


_cdot.defvjp(_cdot_fwd, _cdot_bwd)


def _split3(x):
    hi = x.astype(bf16)
    r = x - hi.astype(f32)
    mid = r.astype(bf16)
    lo = (r - mid.astype(f32)).astype(bf16)
    return hi, mid, lo


@jax.custom_vjp
def _xdot_l(m, x):
    return sum(_dot(m, p, NN) for p in _split3(x))


def _xdot_l_fwd(m, x):
    return _xdot_l(m, x), m


def _xdot_l_bwd(m, g):
    return jnp.zeros_like(m), sum(_dot(m, p, TN) for p in _split3(g))


_xdot_l.defvjp(_xdot_l_fwd, _xdot_l_bwd)


@jax.custom_vjp
def _softplus(x):
    e = jnp.exp(-jnp.abs(x))
    u = 1.0 + e
    log1p = jnp.where(u == 1.0, e, jnp.log(u) * (e / jnp.where(u == 1.0, 1.0, u - 1.0)))
    return jnp.maximum(x, 0.0) + log1p


def _softplus_fwd(x):
    return _softplus(x), x


def _softplus_bwd(x, g):
    return (g * jax.nn.sigmoid(x),)


_softplus.defvjp(_softplus_fwd, _softplus_bwd)


def _rms(x, w):
    return x * lax.rsqrt(jnp.mean(x * x, axis=-1, keepdims=True) + RMS_EPS) * w


def mm_nn(a, b, name, tm=2048):
    M, K = a.shape
    N = b.shape[1]
    tn = _tile(N)

    def body(a_ref, b_ref, o_ref):
        o_ref[...] = _dot(a_ref[...], b_ref[...], NN)

    return _call(body, name, (M // tm, N // tn),
                 [pl.BlockSpec((tm, K), lambda i, j: (i, 0)), pl.BlockSpec((K, tn), lambda i, j: (0, j))],
                 pl.BlockSpec((tm, tn), lambda i, j: (i, j)), jax.ShapeDtypeStruct((M, N), f32),
                 sem=("parallel", "parallel"))(a, b)


def mm_nt(a, b, name, acc=None, tm=1024):
    M, K = a.shape
    N = b.shape[0]
    tk = _tile(K, (2816, 1024, 768, 512, 256, 128))
    has_acc = acc is not None

    def body(*refs):
        a_ref, b_ref = refs[0], refs[1]
        o_ref = refs[-1]
        k = pl.program_id(1)
        p = _dot(a_ref[...], b_ref[...], NT)

        @pl.when(k == 0)
        def _():
            o_ref[...] = p + refs[2][...] if has_acc else p

        @pl.when(k > 0)
        def _():
            o_ref[...] += p

    specs = [pl.BlockSpec((tm, tk), lambda i, k: (i, k)), pl.BlockSpec((N, tk), lambda i, k: (0, k))]
    args = [a, b]
    if has_acc:
        specs.append(pl.BlockSpec((tm, N), lambda i, k: (i, 0)))
        args.append(acc)
    return _call(body, name, (M // tm, K // tk), specs, pl.BlockSpec((tm, N), lambda i, k: (i, 0)),
                 jax.ShapeDtypeStruct((M, N), f32), sem=("parallel", "arbitrary"))(*args)


def mm_tn(a, b, name, tk=2048):
    K, M = a.shape
    N = b.shape[1]
    tn = _tile(N)

    def body(a_ref, b_ref, o_ref):
        k = pl.program_id(1)
        p = _dot(a_ref[...], b_ref[...], TN)

        @pl.when(k == 0)
        def _():
            o_ref[...] = p

        @pl.when(k > 0)
        def _():
            o_ref[...] += p

    return _call(body, name, (N // tn, K // tk),
                 [pl.BlockSpec((tk, M), lambda j, k: (k, 0)), pl.BlockSpec((tk, tn), lambda j, k: (k, j))],
                 pl.BlockSpec((M, tn), lambda j, k: (0, j)), jax.ShapeDtypeStruct((M, N), f32),
                 sem=("parallel", "arbitrary"))(a, b)


def rms_fwd(x, w, name, tm=512):
    S = x.shape[0]

    def body(x_ref, w_ref, o_ref):
        o_ref[...] = _rms(x_ref[...], w_ref[...]).astype(bf16)

    return _call(body, name, (S // tm,),
                 [pl.BlockSpec((tm, D_MODEL), lambda i: (i, 0)), pl.BlockSpec((1, D_MODEL), lambda i: (0, 0))],
                 pl.BlockSpec((tm, D_MODEL), lambda i: (i, 0)), jax.ShapeDtypeStruct((S, D_MODEL), bf16),
                 sem=("parallel",))(x, w)


def rms_bwd(x, w, dh, dres, name, tm=512):
    S = x.shape[0]

    def body(x_ref, w_ref, dh_ref, dr_ref, dx_ref, dw_ref):
        _, vjp = jax.vjp(_rms, x_ref[...], w_ref[...])
        dx, dw = vjp(dh_ref[...])
        dx_ref[...] = dx + dr_ref[...]

        @pl.when(pl.program_id(0) == 0)
        def _():
            dw_ref[...] = dw

        @pl.when(pl.program_id(0) > 0)
        def _():
            dw_ref[...] += dw

    row = pl.BlockSpec((tm, D_MODEL), lambda i: (i, 0))
    vec = pl.BlockSpec((1, D_MODEL), lambda i: (0, 0))
    return _call(body, name, (S // tm,), [row, vec, row, row], [row, vec],
                 [jax.ShapeDtypeStruct((S, D_MODEL), f32), jax.ShapeDtypeStruct((1, D_MODEL), f32)],
                 sem=("arbitrary",))(x, w, dh, dres)


def loss_and_grad(y, target, name, tm=512):
    S = y.shape[0]

    def body(y_ref, t_ref, dy_ref, l_ref):
        diff = y_ref[...] - t_ref[...]
        dy_ref[...] = diff * (1.0 / D_MODEL)
        part = jnp.full((8, LANES), 0.5 / D_MODEL * jnp.sum(diff * diff), f32)

        @pl.when(pl.program_id(0) == 0)
        def _():
            l_ref[...] = part

        @pl.when(pl.program_id(0) > 0)
        def _():
            l_ref[...] += part

    row = pl.BlockSpec((tm, D_MODEL), lambda i: (i, 0))
    return _call(body, name, (S // tm,), [row, row], [row, pl.BlockSpec((8, LANES), lambda i: (0, 0))],
                 [jax.ShapeDtypeStruct((S, D_MODEL), f32), jax.ShapeDtypeStruct((8, LANES), f32)],
                 sem=("arbitrary",))(y, target)


def _s5_discretize(a_re, a_im, log_step, b_re, b_im, c_re, c_im):
    step = jnp.exp(log_step)[:, None]
    mag = jnp.exp(a_re * step)
    ang = a_im * step
    lam_re, lam_im = mag * jnp.cos(ang), mag * jnp.sin(ang)
    num_re, num_im = lam_re - 1.0, lam_im
    den = a_re * a_re + a_im * a_im
    f_re = (num_re * a_re + num_im * a_im) / den
    f_im = (num_im * a_re - num_re * a_im) / den
    bb_re = f_re[..., None] * b_re - f_im[..., None] * b_im
    bb_im = f_re[..., None] * b_im + f_im[..., None] * b_re
    eye = jnp.eye(8, dtype=f32)

    def block_in(bb):
        t = bb.transpose(0, 2, 1).reshape(4, 8, 16, 1, 64)
        return (t * eye[None, :, None, :, None]).reshape(4, 128, 512)

    def block_out(c):
        t = c.transpose(0, 2, 1).reshape(4, 8, 64, 1, 16)
        return (t * eye[None, :, None, :, None]).reshape(4, 512, 128)

    return (lam_re.reshape(1, S5_STATES), lam_im.reshape(1, S5_STATES), block_in(bb_re), block_in(bb_im),
            block_out(c_re), block_out(c_im))


def _lam_powers(lam_re, lam_im):
    rows_re, rows_im = [lam_re], [lam_im]
    for _ in range(7):
        pr, pi = rows_re[-1], rows_im[-1]
        rows_re.append(pr * lam_re - pi * lam_im)
        rows_im.append(pr * lam_im + pi * lam_re)
    return jnp.concatenate(rows_re, 0), jnp.concatenate(rows_im, 0)


def s5_fwd(u, pw_re, pw_im, w_re, w_im, c_re, c_im, dvec, name):
    S = u.shape[0]
    R, NS = S5_ROWS, S5_STATES
    nb = R // 8

    def body(u_ref, pwr_ref, pwi_ref, wre_ref, wim_ref, cre_ref, cim_ref, d_ref, y_ref, hr_ref, hi_ref,
             car_re, car_im, cin_re, cin_im, up, yp):
        @pl.when(pl.program_id(0) == 0)
        def _():
            car_re[...] = jnp.zeros_like(car_re)
            car_im[...] = jnp.zeros_like(car_im)

        slab = lambda r: pl.ds(r * nb, nb)
        for r in range(8):
            up[slab(r), :] = u_ref[:, r, :]
        u = up[...]
        for j in range(4):
            uj = u[:, 128 * j:128 * (j + 1)]
            hr_ref[:, 512 * j:512 * (j + 1)] = _dot(uj, wre_ref[j], NN)
            hi_ref[:, 512 * j:512 * (j + 1)] = _dot(uj, wim_ref[j], NN)
        lr, li = pwr_ref[0:1, :], pwi_ref[0:1, :]
        for r in range(1, 8):
            pr, pi = hr_ref[slab(r - 1), :], hi_ref[slab(r - 1), :]
            hr_ref[slab(r), :] = lr * pr - li * pi + hr_ref[slab(r), :]
            hi_ref[slab(r), :] = lr * pi + li * pr + hi_ref[slab(r), :]
        l8r, l8i = pwr_ref[7:8, :], pwi_ref[7:8, :]

        def across(c, carry):
            gr, gi = carry
            cin_re[pl.ds(c, 1), :] = gr
            cin_im[pl.ds(c, 1), :] = gi
            er, ei = hr_ref[pl.ds(7 * nb + c, 1), :], hi_ref[pl.ds(7 * nb + c, 1), :]
            return l8r * gr - l8i * gi + er, l8r * gi + l8i * gr + ei

        gr, gi = lax.fori_loop(0, nb, across, (car_re[...], car_im[...]))
        car_re[...] = gr
        car_im[...] = gi
        cr, ci = cin_re[...], cin_im[...]
        for r in range(8):
            pr, pi = pwr_ref[r:r + 1, :], pwi_ref[r:r + 1, :]
            hr_ref[slab(r), :] = hr_ref[slab(r), :] + pr * cr - pi * ci
            hi_ref[slab(r), :] = hi_ref[slab(r), :] + pr * ci + pi * cr
        for j in range(4):
            sl = slice(512 * j, 512 * (j + 1))
            cs = slice(128 * j, 128 * (j + 1))
            yp[:, cs] = (_dot(hr_ref[:, sl], cre_ref[j], NN) - _dot(hi_ref[:, sl], cim_ref[j], NN)
                         + d_ref[:, cs] * u[:, cs])
        for r in range(8):
            y_ref[:, r, :] = yp[slab(r), :]

    full = lambda shape: pl.BlockSpec(shape, lambda i: (0,) * len(shape))
    hspec = pl.BlockSpec((R, NS), lambda i: (i, 0))
    uspec = pl.BlockSpec((nb, 8, 512), lambda i: (i, 0, 0))
    y, h_re, h_im = _call(
        body, name, (S // R,),
        [uspec, full((8, NS)), full((8, NS)), full((4, 128, 512)),
         full((4, 128, 512)), full((4, 512, 128)), full((4, 512, 128)), full((1, 512))],
        [uspec, hspec, hspec],
        [jax.ShapeDtypeStruct((S // 8, 8, 512), f32), jax.ShapeDtypeStruct((S, NS), f32),
         jax.ShapeDtypeStruct((S, NS), f32)],
        scratch=[pltpu.VMEM((1, NS), f32), pltpu.VMEM((1, NS), f32), pltpu.VMEM((nb, NS), f32),
                 pltpu.VMEM((nb, NS), f32), pltpu.VMEM((R, 512), f32), pltpu.VMEM((R, 512), f32)],
        sem=("arbitrary",))(u.reshape(S // 8, 8, 512), pw_re, pw_im, w_re.astype(bf16), w_im.astype(bf16),
                            c_re.astype(bf16), c_im.astype(bf16), dvec)
    return y.reshape(S, 512), h_re, h_im


def s5_bwd(dy, u, h_re, h_im, pw_re, pw_im, w_re, w_im, c_re, c_im, dvec, name):
    S = u.shape[0]
    R, NS = S5_ROWS, S5_STATES
    nb = R // 8
    nchunk = S // R

    def body(dy_ref, u_ref, hr_ref, hi_ref, hpr_ref, hpi_ref, pwr_ref, pwi_ref, wre_ref, wim_ref, cre_ref, cim_ref,
             d_ref, du_ref, dwre_ref, dwim_ref, dcre_ref, dcim_ref, dlr_ref, dli_ref, dd_ref,
             ar, ai, car_re, car_im, cin_re, cin_im, up, dyp, dup):
        i = pl.program_id(0)

        @pl.when(i == 0)
        def _():
            for ref in (car_re, car_im, dwre_ref, dwim_ref, dcre_ref, dcim_ref, dlr_ref, dli_ref, dd_ref):
                ref[...] = jnp.zeros_like(ref)

        slab = lambda r: pl.ds(r * nb, nb)
        for r in range(8):
            up[slab(r), :] = u_ref[:, r, :]
            dyp[slab(r), :] = dy_ref[:, r, :]
        dy = dyp[...]
        u = up[...]
        for j in range(4):
            dyj = dy[:, 128 * j:128 * (j + 1)]
            ar[:, 512 * j:512 * (j + 1)] = _dot(dyj, cre_ref[j], NT)
            ai[:, 512 * j:512 * (j + 1)] = -_dot(dyj, cim_ref[j], NT)
        lr, li = pwr_ref[0:1, :], pwi_ref[0:1, :]
        for r in range(6, -1, -1):
            nr, ni = ar[slab(r + 1), :], ai[slab(r + 1), :]
            ar[slab(r), :] = lr * nr + li * ni + ar[slab(r), :]
            ai[slab(r), :] = lr * ni - li * nr + ai[slab(r), :]
        l8r, l8i = pwr_ref[7:8, :], pwi_ref[7:8, :]

        def across(k, carry):
            c = nb - 1 - k
            gr, gi = carry
            cin_re[pl.ds(c, 1), :] = gr
            cin_im[pl.ds(c, 1), :] = gi
            er, ei = ar[pl.ds(c, 1), :], ai[pl.ds(c, 1), :]
            return l8r * gr + l8i * gi + er, l8r * gi - l8i * gr + ei

        gr, gi = lax.fori_loop(0, nb, across, (car_re[...], car_im[...]))
        car_re[...] = gr
        car_im[...] = gi
        cr, ci = cin_re[...], cin_im[...]
        for r in range(8):
            pr, pi = pwr_ref[7 - r:8 - r, :], pwi_ref[7 - r:8 - r, :]
            ar[slab(r), :] = ar[slab(r), :] + pr * cr + pi * ci
            ai[slab(r), :] = ai[slab(r), :] + pr * ci - pi * cr

        acc_r = jnp.zeros((1, NS), f32)
        acc_i = jnp.zeros((1, NS), f32)
        has_prev = (i < nchunk - 1).astype(f32)
        top = lax.broadcasted_iota(jnp.int32, (nb, NS), 0) == 0
        for r in range(8):
            if r == 0:
                xr = jnp.where(top, hpr_ref[7:8, :] * has_prev, pltpu.roll(hr_ref[slab(7), :], 1, 0))
                xi = jnp.where(top, hpi_ref[7:8, :] * has_prev, pltpu.roll(hi_ref[slab(7), :], 1, 0))
            else:
                xr, xi = hr_ref[slab(r - 1), :], hi_ref[slab(r - 1), :]
            br, bi = ar[slab(r), :], ai[slab(r), :]
            acc_r += jnp.sum(br * xr + bi * xi, axis=0, keepdims=True)
            acc_i += jnp.sum(bi * xr - br * xi, axis=0, keepdims=True)
        dlr_ref[...] += acc_r
        dli_ref[...] += acc_i
        dd_ref[...] += jnp.sum(dy * u, axis=0, keepdims=True)

        for j in range(4):
            sl = slice(512 * j, 512 * (j + 1))
            cs = slice(128 * j, 128 * (j + 1))
            arj, aij = ar[:, sl], ai[:, sl]
            uj, dyj = u[:, cs], dy[:, cs]
            dup[:, cs] = _dot(arj, wre_ref[j], NT) + _dot(aij, wim_ref[j], NT) + d_ref[:, cs] * dyj
            dwre_ref[j] += _dot(uj, arj, TN)
            dwim_ref[j] += _dot(uj, aij, TN)
            dcre_ref[j] += _dot(hr_ref[:, sl], dyj, TN)
            dcim_ref[j] -= _dot(hi_ref[:, sl], dyj, TN)
        for r in range(8):
            du_ref[:, r, :] = dup[slab(r), :]

    rev = lambda i: nchunk - 1 - i
    full = lambda shape: pl.BlockSpec(shape, lambda i: (0,) * len(shape))
    row = pl.BlockSpec((nb, 8, 512), lambda i: (rev(i), 0, 0))
    hspec = pl.BlockSpec((R, NS), lambda i: (rev(i), 0))
    hprev = pl.BlockSpec((8, NS), lambda i: (jnp.maximum(rev(i) * nb - 1, 0), 0))
    outs = _call(
        body, name, (nchunk,),
        [row, row, hspec, hspec, hprev, hprev, full((8, NS)), full((8, NS)), full((4, 128, 512)), full((4, 128, 512)),
         full((4, 512, 128)), full((4, 512, 128)), full((1, 512))],
        [row, full((4, 128, 512)), full((4, 128, 512)), full((4, 512, 128)), full((4, 512, 128)),
         full((1, NS)), full((1, NS)), full((1, 512))],
        [jax.ShapeDtypeStruct((S // 8, 8, 512), f32), jax.ShapeDtypeStruct((4, 128, 512), f32),
         jax.ShapeDtypeStruct((4, 128, 512), f32), jax.ShapeDtypeStruct((4, 512, 128), f32),
         jax.ShapeDtypeStruct((4, 512, 128), f32), jax.ShapeDtypeStruct((1, NS), f32),
         jax.ShapeDtypeStruct((1, NS), f32), jax.ShapeDtypeStruct((1, 512), f32)],
        scratch=[pltpu.VMEM((R, NS), f32), pltpu.VMEM((R, NS), f32), pltpu.VMEM((1, NS), f32),
                 pltpu.VMEM((1, NS), f32), pltpu.VMEM((nb, NS), f32), pltpu.VMEM((nb, NS), f32),
                 pltpu.VMEM((R, 512), f32), pltpu.VMEM((R, 512), f32), pltpu.VMEM((R, 512), f32)],
        sem=("arbitrary",))(dy.reshape(S // 8, 8, 512), u.reshape(S // 8, 8, 512), h_re, h_im, h_re, h_im, pw_re,
                            pw_im, w_re.astype(bf16), w_im.astype(bf16), c_re.astype(bf16), c_im.astype(bf16), dvec)
    return (outs[0].reshape(S, 512),) + tuple(outs[1:])


def _rows(start, n, d):
    return pl.ds(pl.multiple_of(start, ATT_BLOCK), n) if d == 1 else pl.ds(start, n, stride=d)


def _head_masks():
    lane = lax.broadcasted_iota(jnp.int32, (1, LANES), 1)
    return [(lane < 64).astype(f32), (lane >= 64).astype(f32)]


def _head_norm(x, w, hm):
    x2 = x * x
    r = [lax.rsqrt(jnp.sum(x2 * hm[h], axis=-1, keepdims=True) * (1.0 / 64) + RMS_EPS) for h in range(2)]
    sc = hm[0] * r[0] + hm[1] * r[1]
    return x * sc * w, sc, r


def _head_norm_bwd(x, w, sc, r, dxn, hm):
    dw = jnp.sum(dxn * x * sc, axis=0, keepdims=True)
    t = dxn * w
    tx = t * x
    corr = sum(hm[h] * (r[h] * r[h] * r[h]) * jnp.sum(tx * hm[h], axis=-1, keepdims=True) for h in range(2))
    return t * sc - x * corr * (1.0 / 64), dw


def _att_mask(has_prev):
    qi = lax.broadcasted_iota(jnp.int32, (ATT_BLOCK, 2 * ATT_BLOCK), 0) + ATT_BLOCK
    kj = lax.broadcasted_iota(jnp.int32, (ATT_BLOCK, 2 * ATT_BLOCK), 1)
    return (qi - kj >= 0) & (qi - kj <= ATT_BLOCK) & (has_prev | (kj >= ATT_BLOCK))


def _att_block_bwd(q, k, v, o, lse, do, dlse, qw, kw, has_prev):
    hm = _head_masks()
    mask = _att_mask(has_prev)
    qn, qsc, qr = _head_norm(q, qw, hm)
    kn, ksc, kr = _head_norm(k, kw, hm)
    dqn = jnp.zeros((ATT_BLOCK, LANES), f32)
    dkn = jnp.zeros((2 * ATT_BLOCK, LANES), f32)
    dv = jnp.zeros((2 * ATT_BLOCK, LANES), f32)
    for h in range(2):
        qh, do_h = qn * hm[h], do * hm[h]
        s = _dot(qh, kn, NT) * 0.125
        p = jnp.exp(jnp.where(mask, s - lse[:, 64 * h:64 * h + 1], -jnp.inf))
        dp = _dot(do_h, v, NT)
        delta = jnp.sum(do_h * o, axis=-1, keepdims=True)
        dl = jnp.sum(dlse * hm[h], axis=-1, keepdims=True)
        ds = p * (dp - delta + dl) * 0.125
        dqn = dqn + hm[h] * _dot(ds, kn, NN)
        dkn = dkn + _dot(ds, qh, TN)
        dv = dv + _dot(p, do_h, TN)
    dq, dqw = _head_norm_bwd(q, qw, qsc, qr, dqn, hm)
    dk, dkw = _head_norm_bwd(k, kw, ksc, kr, dkn, hm)
    return dq, dk, dv, dqw, dkw


def _att_block(q, k, v, qw, kw, has_prev):
    hm = _head_masks()
    qn, kn = _head_norm(q, qw, hm)[0], _head_norm(k, kw, hm)[0]
    mask = _att_mask(has_prev)
    o = jnp.zeros((ATT_BLOCK, LANES), f32)
    lse = jnp.zeros((ATT_BLOCK, LANES), f32)
    for h in range(2):
        s = _bdot(qn * hm[h], kn, NT) * 0.125
        s = jnp.where(mask, s, -jnp.inf)
        m = jnp.max(s, axis=-1, keepdims=True)
        p = jnp.exp(s - m)
        l = jnp.sum(p, axis=-1, keepdims=True)
        o = o + hm[h] * _bdot(p / l, v, NN)
        lse = lse + hm[h] * (m + jnp.log(l))
    return o, lse


def att_fwd(p_att, qw, kw, d, g, name):
    S = p_att.shape[0]
    SEG = ATT_SEG
    nblk = SEG // ATT_BLOCK

    def body(p_ref, qw_ref, kw_ref, o_ref, l_ref, q_s, k_ext, v_ext, o_s, l_s):
        seg = pl.program_id(1)

        @pl.when(seg == 0)
        def _():
            k_ext[SEG:, :] = jnp.zeros((SEG, LANES), f32)
            v_ext[SEG:, :] = jnp.zeros((SEG, LANES), f32)

        k_ext[:SEG, :] = k_ext[SEG:, :]
        v_ext[:SEG, :] = v_ext[SEG:, :]
        q_s[...] = p_ref[:, 0:128]
        k_ext[SEG:, :] = p_ref[:, 128:256]
        v_ext[SEG:, :] = p_ref[:, 256:384]
        qw_v, kw_v = qw_ref[...], kw_ref[...]

        def blk(b, carry):
            j, r = b // d, b % d
            qs = j * (ATT_BLOCK * d) + r
            ks = SEG + qs - ATT_BLOCK * d
            o, lse = _att_block(q_s[_rows(qs, ATT_BLOCK, d), :], k_ext[_rows(ks, 2 * ATT_BLOCK, d), :],
                                v_ext[_rows(ks, 2 * ATT_BLOCK, d), :], qw_v, kw_v, (seg > 0) | (j > 0))
            o_s[_rows(qs, ATT_BLOCK, d), :] = o
            l_s[_rows(qs, ATT_BLOCK, d), :] = lse
            return carry

        lax.fori_loop(0, nblk, blk, 0, unroll=4)
        o_ref[...] = o_s[...]
        l_ref[...] = l_s[...]

    vec = pl.BlockSpec((1, LANES), lambda hh, s: (0, 0))
    out = pl.BlockSpec((SEG, LANES), lambda hh, s: (s, hh))
    return _call(body, name, (2, S // SEG), [pl.BlockSpec((SEG, 384), lambda hh, s: (s, MAIN_ATT_BLOCK + 2 * g + hh)), vec, vec],
                 [out, out], [jax.ShapeDtypeStruct((S, 256), f32), jax.ShapeDtypeStruct((S, 256), f32)],
                 scratch=[pltpu.VMEM((SEG, LANES), f32), pltpu.VMEM((2 * SEG, LANES), f32),
                          pltpu.VMEM((2 * SEG, LANES), f32), pltpu.VMEM((SEG, LANES), f32),
                          pltpu.VMEM((SEG, LANES), f32)],
                 sem=("arbitrary", "arbitrary"))(p_att, qw, kw)


def att_bwd(p_att, o, lse, do, dlse, qw, kw, d, g, dp_main, name):
    S = p_att.shape[0]
    SEG = ATT_SEG
    nseg = S // SEG
    nblk = SEG // ATT_BLOCK

    def body(p_ref, pp_ref, o_ref, l_ref, do_ref, dl_ref, qw_ref, kw_ref, _, dp_ref, dqw_ref, dkw_ref,
             q_s, k_ext, v_ext, dq_s, dk_ext, dv_ext):
        hh, i = pl.program_id(0), pl.program_id(1)
        seg = nseg - 1 - i

        @pl.when(i == 0)
        def _():
            dk_ext[...] = jnp.zeros_like(dk_ext)
            dv_ext[...] = jnp.zeros_like(dv_ext)

        @pl.when((i == 0) & (hh == 0))
        def _():
            dqw_ref[...] = jnp.zeros_like(dqw_ref)
            dkw_ref[...] = jnp.zeros_like(dkw_ref)

        dk_ext[SEG:, :] = dk_ext[:SEG, :]
        dv_ext[SEG:, :] = dv_ext[:SEG, :]
        dk_ext[:SEG, :] = jnp.zeros((SEG, LANES), f32)
        dv_ext[:SEG, :] = jnp.zeros((SEG, LANES), f32)
        q_s[...] = p_ref[:, 0:128]
        k_ext[SEG:, :] = p_ref[:, 128:256]
        v_ext[SEG:, :] = p_ref[:, 256:384]
        k_ext[:SEG, :] = pp_ref[:, 128:256]
        v_ext[:SEG, :] = pp_ref[:, 256:384]
        qw_v, kw_v = qw_ref[...], kw_ref[...]

        def blk_pair(i2, carry):
            dqw, dkw = carry
            done = []
            for u in range(2):
                b = 2 * i2 + u
                j, r = b // d, b % d
                qs = j * (ATT_BLOCK * d) + r
                ks = SEG + qs - ATT_BLOCK * d
                has_prev = (seg > 0) | (j > 0)
                qrows, krows = _rows(qs, ATT_BLOCK, d), _rows(ks, 2 * ATT_BLOCK, d)
                dq, dk, dv, dqw_b, dkw_b = _att_block_bwd(
                    q_s[qrows, :], k_ext[krows, :], v_ext[krows, :], o_ref[qrows, :], l_ref[qrows, :],
                    do_ref[qrows, :], dl_ref[qrows, :], qw_v, kw_v, has_prev)
                dqw, dkw = dqw + dqw_b, dkw + dkw_b
                done.append((qrows, krows, dq, dk, dv))
            for qrows, krows, dq, dk, dv in done:
                dq_s[qrows, :] = dq
                dk_ext[krows, :] = dk_ext[krows, :] + dk
                dv_ext[krows, :] = dv_ext[krows, :] + dv
            return dqw, dkw

        zero = jnp.zeros((1, LANES), f32)
        dqw, dkw = lax.fori_loop(0, nblk // 2, blk_pair, (zero, zero))
        dqw_ref[...] += dqw
        dkw_ref[...] += dkw
        dp_ref[:, 0:128] = dq_s[...].astype(bf16)
        dp_ref[:, 128:256] = dk_ext[SEG:, :].astype(bf16)
        dp_ref[:, 256:384] = dv_ext[SEG:, :].astype(bf16)

    rev = lambda i: nseg - 1 - i
    vec = pl.BlockSpec((1, LANES), lambda hh, i: (0, 0))
    blk = MAIN_ATT_BLOCK + 2 * g
    cur = pl.BlockSpec((SEG, 384), lambda hh, i: (rev(i), blk + hh))
    prev = pl.BlockSpec((SEG, 384), lambda hh, i: (jnp.maximum(rev(i) - 1, 0), blk + hh))
    col = pl.BlockSpec((SEG, LANES), lambda hh, i: (rev(i), hh))
    big = pltpu.VMEM((2 * SEG, LANES), f32)
    one = pltpu.VMEM((SEG, LANES), f32)
    return _call(body, name, (2, nseg), [cur, prev, col, col, col, col, vec, vec, _ANY], [cur, vec, vec],
                 [jax.ShapeDtypeStruct((S, MAIN_WIDTH), bf16), jax.ShapeDtypeStruct((1, LANES), f32),
                  jax.ShapeDtypeStruct((1, LANES), f32)],
                 scratch=[one, big, big, one, big, big], sem=("arbitrary", "arbitrary"),
                 aliases={8: 0})(p_att, p_att, o, lse, do, dlse, qw, kw, dp_main)


def conv_fwd(p_ssd, conv_w, conv_b, name):
    S = p_ssd.shape[0]
    tm, C = CONV_ROWS, SSD_XBC

    def body(x_ref, xp_ref, w_ref, b_ref, o_ref):
        first = (pl.program_id(0) == 0)
        ext = jnp.concatenate([jnp.where(first, 0.0, xp_ref[:, 0:C]), x_ref[:, 0:C]], axis=0)
        acc = b_ref[...] + w_ref[3:4, :] * ext[8:, :]
        for k in range(1, 4):
            acc = acc + w_ref[3 - k:4 - k, :] * pltpu.roll(ext, k, 0)[8:, :]
        o_ref[...] = jax.nn.silu(acc)

    return _call(body, name, (S // tm,),
                 [pl.BlockSpec((tm, 1536), lambda i: (i, MAIN_SSD_BLOCK)),
                  pl.BlockSpec((8, 1536), lambda i: (jnp.maximum(i * (tm // 8) - 1, 0), MAIN_SSD_BLOCK)),
                  pl.BlockSpec((4, C), lambda i: (0, 0)), pl.BlockSpec((1, C), lambda i: (0, 0))],
                 pl.BlockSpec((tm, C), lambda i: (i, 0)), jax.ShapeDtypeStruct((S, C), f32),
                 sem=("parallel",))(p_ssd, p_ssd, conv_w, conv_b)


def conv_bwd(p_ssd, dact, ddt, conv_w, conv_b, dp_main, name):
    S = p_ssd.shape[0]
    tm, C = CONV_ROWS, SSD_XBC
    nblk = S // tm

    def body(x_ref, xp_ref, xn_ref, da_ref, dan_ref, ddt_ref, w_ref, b_ref, _, dp_ref, dw_ref, db_ref):
        i = pl.program_id(0)
        rows = tm + 8
        ext = jnp.concatenate([jnp.where(i == 0, 0.0, xp_ref[:, 0:C]), x_ref[:, 0:C], xn_ref[:, 0:C]], axis=0)
        shifted = [ext[8:, :]] + [pltpu.roll(ext, k, 0)[8:, :] for k in range(1, 4)]
        pre = b_ref[...] + w_ref[3:4, :] * shifted[0]
        for k in range(1, 4):
            pre = pre + w_ref[3 - k:4 - k, :] * shifted[k]
        sg = jax.nn.sigmoid(pre)
        dact = jnp.concatenate([da_ref[...], jnp.where(i == nblk - 1, 0.0, dan_ref[...])], axis=0)
        dpre = dact * (sg * (1.0 + pre * (1.0 - sg)))
        dx = w_ref[3:4, :] * dpre[0:tm, :]
        for k in range(1, 4):
            dx = dx + w_ref[3 - k:4 - k, :] * pltpu.roll(dpre, rows - k, 0)[0:tm, :]
        dp_ref[:, 0:C] = dx.astype(bf16)
        dp_ref[:, C:C + 128] = ddt_ref[...].astype(bf16)
        dp_ref[:, C + 128:] = jnp.zeros((tm, 128), bf16)
        dcur = dpre[0:tm, :]
        dws = [jnp.sum(dcur * shifted[3 - j][0:tm, :], axis=0, keepdims=True) for j in range(4)]
        dbs = jnp.sum(dcur, axis=0, keepdims=True)

        @pl.when(i == 0)
        def _():
            dw_ref[...] = jnp.zeros_like(dw_ref)
            db_ref[...] = jnp.zeros_like(db_ref)

        for j in range(4):
            dw_ref[j:j + 1, :] += dws[j]
        db_ref[...] += dbs

    t8 = tm // 8
    blk = MAIN_SSD_BLOCK
    return _call(body, name, (nblk,),
                 [pl.BlockSpec((tm, 1536), lambda i: (i, blk)),
                  pl.BlockSpec((8, 1536), lambda i: (jnp.maximum(i * t8 - 1, 0), blk)),
                  pl.BlockSpec((8, 1536), lambda i: (jnp.minimum((i + 1) * t8, S // 8 - 1), blk)),
                  pl.BlockSpec((tm, C), lambda i: (i, 0)),
                  pl.BlockSpec((8, C), lambda i: (jnp.minimum((i + 1) * t8, S // 8 - 1), 0)),
                  pl.BlockSpec((tm, 128), lambda i: (i, 0)),
                  pl.BlockSpec((4, C), lambda i: (0, 0)), pl.BlockSpec((1, C), lambda i: (0, 0)), _ANY],
                 [pl.BlockSpec((tm, 1536), lambda i: (i, blk)), pl.BlockSpec((4, C), lambda i: (0, 0)),
                  pl.BlockSpec((1, C), lambda i: (0, 0))],
                 [jax.ShapeDtypeStruct((S, MAIN_WIDTH), bf16), jax.ShapeDtypeStruct((4, C), f32),
                  jax.ShapeDtypeStruct((1, C), f32)],
                 sem=("arbitrary",), aliases={8: 0})(p_ssd, p_ssd, p_ssd, dact, dact, ddt, conv_w, conv_b, dp_main)


def _ssd_chunk(xbc, dtr, state, dt_bias, a_log, d_full):
    T = SSD_CHUNK
    r_i = lax.broadcasted_iota(jnp.int32, (T, T), 0)
    c_i = lax.broadcasted_iota(jnp.int32, (T, T), 1)
    tril = c_i <= r_i
    tri = tril.astype(bf16)
    lane = lax.broadcasted_iota(jnp.int32, (1, LANES), 1)
    hm = [(lane < 64).astype(f32), (lane >= 64).astype(f32)]
    column = lambda v, h: jnp.broadcast_to(v[:, h:h + 1], (T, LANES))

    def per_head_lanes(v):
        return jnp.concatenate([jnp.where(lane < 64, column(v, 2 * pp), column(v, 2 * pp + 1)) for pp in range(6)],
                               axis=1)

    xs, bm, cm = xbc[:, :768], xbc[:, 768:1024], xbc[:, 1024:1280]
    dt = _softplus(dtr + dt_bias)
    a_dt = dt * (-jnp.exp(a_log))
    a_cs = _xdot_l(tri, a_dt)
    dt_full = per_head_lanes(dt)
    acs_full = per_head_lanes(a_cs)
    last = lax.broadcasted_iota(jnp.int32, (T, SSD_WIDTH), 0) == T - 1
    tot_full = jnp.sum(jnp.where(last, acs_full, 0.0), axis=0, keepdims=True)
    xdt = xs * dt_full
    xw = xdt * jnp.exp(tot_full - acs_full)
    eacs = jnp.exp(acs_full)
    st_parts, off_parts, diag_parts = [], [], []
    for g in range(2):
        bg, cg = bm[:, 128 * g:128 * (g + 1)], cm[:, 128 * g:128 * (g + 1)]
        cols = slice(384 * g, 384 * (g + 1))
        st_parts.append(_bdot(bg, xw[:, cols], TN))
        off_parts.append(_bdot(cg, state[:, cols], NN))
        cb = _bdot(cg, bg, NT)
        for pp in range(3 * g, 3 * g + 3):
            xp = xdt[:, 128 * pp:128 * (pp + 1)]
            acc = jnp.zeros((T, LANES), f32)
            for hh in range(2):
                a_col = column(a_cs, 2 * pp + hh)
                decay = jnp.where(tril, jnp.exp(jnp.minimum(a_col - a_col.T, 0.0)), 0.0)
                acc = acc + _bdot(cb * decay, xp * hm[hh], NN)
            diag_parts.append(acc)
    new_state = state * jnp.exp(tot_full) + jnp.concatenate(st_parts, axis=1)
    y = jnp.concatenate(diag_parts, axis=1) + jnp.concatenate(off_parts, axis=1) * eacs + xs * d_full
    return y, new_state


def ssd_fwd(xact, p_ssd, dt_bias, a_log, d_full, name):
    S = xact.shape[0]
    T = SSD_CHUNK

    U = SSD_CHUNKS_PER_STEP

    def body(x_ref, p_ref, b_ref, a_ref, d_ref, y_ref, s_ref, state):
        @pl.when(pl.program_id(0) == 0)
        def _():
            state[...] = jnp.zeros_like(state)

        st = state[...]
        for u in range(U):
            rows = slice(T * u, T * (u + 1))
            s_ref[u] = st
            y, st = _ssd_chunk(x_ref[rows, :], p_ref[rows, :], st, b_ref[...], a_ref[...], d_ref[...])
            y_ref[rows, :] = y
        state[...] = st

    vec = lambda n: pl.BlockSpec((1, n), lambda i: (0, 0))
    return _call(body, name, (S // (U * T),),
                 [pl.BlockSpec((U * T, SSD_XBC), lambda i: (i, 0)),
                  pl.BlockSpec((U * T, 128), lambda i: (i, MAIN_DT_BLOCK)), vec(128), vec(128), vec(768)],
                 [pl.BlockSpec((U * T, 768), lambda i: (i, 0)), pl.BlockSpec((U, T, 768), lambda i: (i, 0, 0))],
                 [jax.ShapeDtypeStruct((S, 768), f32), jax.ShapeDtypeStruct((S // T, T, 768), f32)],
                 scratch=[pltpu.VMEM((T, 768), f32)], sem=("arbitrary",))(xact, p_ssd, dt_bias, a_log, d_full)


def ssd_bwd(xact, p_ssd, states, dy, dt_bias, a_log, d_full, name):
    S = xact.shape[0]
    T = SSD_CHUNK
    U = 1
    nc = S // (U * T)

    def body(x_ref, p_ref, s_ref, dy_ref, b_ref, a_ref, d_ref, dx_ref, ddt_ref, db_ref, da_ref, dd_ref, dstate):
        i = pl.program_id(0)

        @pl.when(i == 0)
        def _():
            for ref in (dstate, db_ref, da_ref, dd_ref):
                ref[...] = jnp.zeros_like(ref)

        dst = dstate[...]
        for u in reversed(range(U)):
            rows = slice(T * u, T * (u + 1))
            _, vjp = jax.vjp(_ssd_chunk, x_ref[rows, :], p_ref[rows, :], s_ref[u], b_ref[...], a_ref[...], d_ref[...])
            dx, ddt, dst, db, da, dd = vjp((dy_ref[rows, :], dst))
            dx_ref[rows, :] = dx
            ddt_ref[rows, :] = ddt
            db_ref[...] += db
            da_ref[...] += da
            dd_ref[...] += dd
        dstate[...] = dst

    rev = lambda i: nc - 1 - i
    vec = lambda n: pl.BlockSpec((1, n), lambda i: (0, 0))
    return _call(body, name, (nc,),
                 [pl.BlockSpec((U * T, SSD_XBC), lambda i: (rev(i), 0)),
                  pl.BlockSpec((U * T, 128), lambda i: (rev(i), MAIN_DT_BLOCK)),
                  pl.BlockSpec((U, T, 768), lambda i: (rev(i), 0, 0)), pl.BlockSpec((U * T, 768), lambda i: (rev(i), 0)),
                  vec(128), vec(128), vec(768)],
                 [pl.BlockSpec((U * T, SSD_XBC), lambda i: (rev(i), 0)), pl.BlockSpec((U * T, 128), lambda i: (rev(i), 0)),
                  vec(128), vec(128), vec(768)],
                 [jax.ShapeDtypeStruct((S, SSD_XBC), f32), jax.ShapeDtypeStruct((S, 128), f32),
                  jax.ShapeDtypeStruct((1, 128), f32), jax.ShapeDtypeStruct((1, 128), f32),
                  jax.ShapeDtypeStruct((1, 768), f32)],
                 scratch=[pltpu.VMEM((T, 768), f32)],
                 sem=("arbitrary",))(xact, p_ssd, states, dy, dt_bias, a_log, d_full)


def _tail_fn(ys5, pt, o0, o1, o2, l0, l1, l2, yssd, glu_b, nw, pr_glu, pr_a, pr_b, pr_c, x, weights):
    glu_w, pa, pb, pc, wo = weights
    gates = jax.nn.sigmoid(pt[:, :3072])
    za, zb, zc = pt[:, 3072:3584], pt[:, 3584:3840], pt[:, 3840:4608]
    g = jax.nn.gelu(ys5)
    ya = g * jax.nn.sigmoid(_cdot(g, glu_w, NN) + glu_b + pr_glu) * jax.nn.silu(za)
    m = jnp.maximum(jnp.maximum(l0, l1), l2)
    e0, e1, e2 = jnp.exp(l0 - m), jnp.exp(l1 - m), jnp.exp(l2 - m)
    yb = (e0 * o0 + e1 * o1 + e2 * o2) / (e0 + e1 + e2) * jax.nn.silu(zb)
    yc = _rms(yssd * jax.nn.silu(zc), nw)
    merged = (gates[:, :1024] * (_cdot(ya, pa, NN) + pr_a) + gates[:, 1024:2048] * (_cdot(yb, pb, NN) + pr_b)
              + gates[:, 2048:] * (_cdot(yc, pc, NN) + pr_c))
    out = x + _cdot(merged, wo, NN)
    return out, (g, ya, yb, yc, merged)


def _tail_specs(tm):
    row = lambda n: pl.BlockSpec((tm, n), lambda i: (i, 0))
    full = lambda a, b: pl.BlockSpec((a, b), lambda i: (0, 0))
    acts = [row(512), row(4608)] + [row(256)] * 6 + [row(768), row(D_MODEL)]
    consts = [full(1, 512), full(1, 768), full(512, 512), full(512, D_MODEL), full(256, D_MODEL),
              full(768, D_MODEL), full(D_MODEL, D_MODEL)]
    return row, full, acts, consts


def tail_fwd(ys5, pt, os_, ls_, yssd, x, glu_b, nw, weights, name):
    S = x.shape[0]
    tm = TAIL_ROWS
    row, full, acts, consts = _tail_specs(tm)

    def body(ys5_ref, pt_ref, o0, o1, o2, l0, l1, l2, yssd_ref, x_ref, gb_ref, nw_ref, gw, pa, pb, pc, wo, out_ref):
        z = lambda n: jnp.zeros((tm, n), f32)
        out, _ = _tail_fn(ys5_ref[...], pt_ref[...], o0[...], o1[...], o2[...], l0[...], l1[...], l2[...],
                          yssd_ref[...], gb_ref[...], nw_ref[...], z(512), z(D_MODEL), z(D_MODEL), z(D_MODEL),
                          x_ref[...], (gw[...], pa[...], pb[...], pc[...], wo[...]))
        out_ref[...] = out

    return _call(body, name, (S // tm,), acts + consts, row(D_MODEL), jax.ShapeDtypeStruct((S, D_MODEL), f32),
                 sem=("parallel",))(ys5, pt, *os_, *ls_, yssd, x, glu_b, nw, *weights)


def tail_bwd(ys5, pt, os_, ls_, yssd, dout, glu_b, nw, weights, name):
    S = dout.shape[0]
    tm = TAIL_ROWS
    row, full, acts, consts = _tail_specs(tm)

    def body(ys5_ref, pt_ref, o0, o1, o2, l0, l1, l2, yssd_ref, dout_ref, gb_ref, nw_ref, gw, pa, pb, pc, wo,
             dys5_ref, dpt_ref, do0, do1, do2, dl0, dl1, dl2, dyssd_ref, dgb_ref, dnw_ref,
             g_ref, ya_ref, yb_ref, yc_ref, mg_ref, dglu_ref, dpa_ref, dpb_ref, dpc_ref):
        z = lambda n: jnp.zeros((tm, n), f32)
        w = (gw[...], pa[...], pb[...], pc[...], wo[...])
        fn = lambda *a: _tail_fn(*a, z(D_MODEL), w)
        _, vjp, aux = jax.vjp(fn, ys5_ref[...], pt_ref[...], o0[...], o1[...], o2[...], l0[...], l1[...], l2[...],
                              yssd_ref[...], gb_ref[...], nw_ref[...], z(512), z(D_MODEL), z(D_MODEL), z(D_MODEL),
                              has_aux=True)
        (dys5, dpt, d0, d1, d2, e0, e1, e2, dyssd, dgb, dnw, dglu, dpa, dpb, dpc) = vjp(dout_ref[...])
        dys5_ref[...] = dys5
        dpt_ref[...] = dpt.astype(bf16)
        for ref, val in ((do0, d0), (do1, d1), (do2, d2), (dl0, e0), (dl1, e1), (dl2, e2)):
            ref[...] = val
        dyssd_ref[...] = dyssd
        g, ya, yb, yc, merged = aux
        for ref, val in ((g_ref, g), (ya_ref, ya), (yb_ref, yb), (yc_ref, yc), (mg_ref, merged),
                         (dglu_ref, dglu), (dpa_ref, dpa), (dpb_ref, dpb), (dpc_ref, dpc)):
            ref[...] = val.astype(bf16)

        @pl.when(pl.program_id(0) == 0)
        def _():
            dgb_ref[...] = dgb
            dnw_ref[...] = dnw

        @pl.when(pl.program_id(0) > 0)
        def _():
            dgb_ref[...] += dgb
            dnw_ref[...] += dnw

    sd = lambda n, dt=f32: jax.ShapeDtypeStruct((S, n), dt)
    out_specs = ([row(512), row(4608)] + [row(256)] * 6 + [row(768), full(1, 512), full(1, 768)]
                 + [row(512), row(512), row(256), row(768), row(D_MODEL), row(512)] + [row(D_MODEL)] * 3)
    out_shape = ([sd(512), sd(MAIN_WIDTH, bf16)] + [sd(256)] * 6 + [sd(768), jax.ShapeDtypeStruct((1, 512), f32),
                                                          jax.ShapeDtypeStruct((1, 768), f32)]
                 + [sd(512, bf16), sd(512, bf16), sd(256, bf16), sd(768, bf16), sd(D_MODEL, bf16), sd(512, bf16)]
                 + [sd(D_MODEL, bf16)] * 3)
    return _call(body, name, (S // tm,), acts + consts, out_specs, out_shape,
                 sem=("arbitrary",))(ys5, pt, *os_, *ls_, yssd, dout, glu_b, nw, *weights)


def _in_proj_segments(shards):
    dtype = shards[0].dtype

    def c(a, b):
        parts = []
        for k, sh in enumerate(shards):
            lo, hi = max(a, W_IN_SHARD * k), min(b, W_IN_SHARD * (k + 1))
            if lo < hi:
                parts.append(sh[:, lo - W_IN_SHARD * k:hi - W_IN_SHARD * k])
        return parts[0] if len(parts) == 1 else jnp.concatenate(parts, axis=1)

    atts = []
    for g in range(3):
        parts = []
        for hh in range(2):
            o = 64 * (4 * g + 2 * hh)
            parts += [c(_C_Q + o, _C_Q + o + 128), c(_C_K + o, _C_K + o + 128), c(_C_V + o, _C_V + o + 128)]
        atts.append(jnp.concatenate(parts, axis=1))
    ssd = jnp.concatenate([c(_C_XBC, _C_ZC), jnp.zeros((D_MODEL, 1536 - (_C_ZC - _C_XBC)), dtype)], axis=1)
    tail = jnp.concatenate([c(_C_GATE, _C_END), c(_C_ZA, _C_Q), c(_C_ZB, _C_XBC), c(_C_ZC, _C_GATE)], axis=1)
    return [c(_C_UA, _C_ZA), jnp.concatenate([tail, ssd] + atts, axis=1)]


def _in_proj_grad(ds5, dmain):
    dtail, dssd = dmain[:, :4608], dmain[:, 4608:6144]
    datts = [dmain[:, 6144 + 768 * g:6144 + 768 * (g + 1)] for g in range(3)]
    pick = lambda off: [datts[g][:, 384 * hh + off:384 * hh + off + 128] for g in range(3) for hh in range(2)]
    pieces = ([ds5, dtail[:, 3072:3584]] + pick(0) + pick(128) + pick(256)
              + [dtail[:, 3584:3840], dssd[:, :_C_ZC - _C_XBC], dtail[:, 3840:4608], dtail[:, :3072]])
    shards, start = [[] for _ in range(4)], 0
    for piece in pieces:
        width = piece.shape[1]
        for k in range(4):
            lo, hi = max(start, W_IN_SHARD * k), min(start + width, W_IN_SHARD * (k + 1))
            if lo < hi:
                shards[k].append(piece[:, lo - start:hi - start])
        start += width
    return jnp.stack([jnp.concatenate(s, axis=1) for s in shards])


def _prep_layer(p):
    q = {}
    q["segs"] = [s.astype(bf16) for s in _in_proj_segments(p["w_in"])]
    disc = _s5_discretize(p["s5_a_re"], p["s5_a_im"], p["s5_log_step"], p["s5_b_re"], p["s5_b_im"],
                          p["s5_c_re"], p["s5_c_im"])
    q["s5"] = disc
    q["pw"] = _lam_powers(disc[0], disc[1])
    q["s5_d"] = p["s5_d"].reshape(1, 512)
    q["qw"] = jnp.tile(p["q_norm_w"], 2).reshape(1, LANES)
    q["kw"] = jnp.tile(p["k_norm_w"], 2).reshape(1, LANES)
    q["conv_w"] = p["conv_w"]
    q["conv_b"] = p["conv_b"].reshape(1, SSD_XBC)
    pad = lambda v: jnp.pad(v, (0, LANES - v.shape[0])).reshape(1, LANES)
    q["dt_bias"], q["a_log"] = pad(p["dt_bias"]), pad(p["ssd_a_log"])
    q["d_full"] = jnp.repeat(p["ssd_d"], 64).reshape(1, SSD_WIDTH)
    q["glu_b"] = p["s5_glu_b"].reshape(1, 512)
    q["nw"] = p["ssd_norm_w"].reshape(1, SSD_WIDTH)
    q["norm_w"] = p["norm_w"].reshape(1, D_MODEL)
    q["tailw"] = tuple(p[n].astype(bf16) for n in ("s5_glu_w", "proj_a", "proj_b", "proj_c", "w_out"))
    return q


_DILATIONS = (1, 4, 16)


def layer_fwd(x, q, tag):
    h = rms_fwd(x, q["norm_w"], f"rms_fwd{tag}")
    p_s5, p_main = [mm_nn(h, w, f"inproj{k}{tag}") for k, w in enumerate(q["segs"])]
    _, _, w_re, w_im, c_re, c_im = q["s5"]
    ys5, h_re, h_im = s5_fwd(p_s5, *q["pw"], w_re, w_im, c_re, c_im, q["s5_d"], f"s5_fwd{tag}")
    os_, ls_ = [], []
    for g, d in enumerate(_DILATIONS):
        o, l = att_fwd(p_main, q["qw"], q["kw"], d, g, f"att_fwd{g}{tag}")
        os_.append(o)
        ls_.append(l)
    xact = conv_fwd(p_main, q["conv_w"], q["conv_b"], f"conv_fwd{tag}")
    yssd, states = ssd_fwd(xact, p_main, q["dt_bias"], q["a_log"], q["d_full"], f"ssd_fwd{tag}")
    out = tail_fwd(ys5, p_main, os_, ls_, yssd, x, q["glu_b"], q["nw"], q["tailw"], f"tail_fwd{tag}")
    saved = dict(x=x, h=h, p_s5=p_s5, p_main=p_main, ys5=ys5, h_re=h_re, h_im=h_im,
                 os=os_, ls=ls_, xact=xact, yssd=yssd, states=states)
    return out, saved


def layer_bwd(dout, sv, q, p, tag):
    S = dout.shape[0]
    (dys5, dp_main, do0, do1, do2, dl0, dl1, dl2, dyssd, dglu_b, dnw, g_b, ya_b, yb_b, yc_b, mg_b, dglu_b16,
     dpa_b, dpb_b, dpc_b) = tail_bwd(sv["ys5"], sv["p_main"], sv["os"], sv["ls"], sv["yssd"], dout, q["glu_b"],
                                     q["nw"], q["tailw"], f"tail_bwd{tag}")
    grads = {}
    grads["s5_glu_w"] = mm_tn(g_b, dglu_b16, f"dglu_w{tag}")
    grads["proj_a"] = mm_tn(ya_b, dpa_b, f"dproj_a{tag}")
    grads["proj_b"] = mm_tn(yb_b, dpb_b, f"dproj_b{tag}")
    grads["proj_c"] = mm_tn(yc_b, dpc_b, f"dproj_c{tag}")
    grads["w_out"] = mm_tn(mg_b, dout, f"dw_out{tag}")
    grads["s5_glu_b"] = dglu_b.reshape(512)
    grads["ssd_norm_w"] = dnw.reshape(SSD_WIDTH)

    dxact, ddt, ddt_bias, da_log, dd_full = ssd_bwd(sv["xact"], sv["p_main"], sv["states"], dyssd, q["dt_bias"],
                                                    q["a_log"], q["d_full"], f"ssd_bwd{tag}")
    dp_main, dconv_w, dconv_b = conv_bwd(sv["p_main"], dxact, ddt, q["conv_w"], q["conv_b"], dp_main,
                                         f"conv_bwd{tag}")
    grads["dt_bias"] = ddt_bias[0, :12]
    grads["ssd_a_log"] = da_log[0, :12]
    grads["ssd_d"] = dd_full.reshape(12, 64).sum(axis=1)
    grads["conv_w"] = dconv_w
    grads["conv_b"] = dconv_b.reshape(SSD_XBC)

    dqw, dkw = 0.0, 0.0
    for g, d in enumerate(_DILATIONS):
        dp_main, a, b = att_bwd(sv["p_main"], sv["os"][g], sv["ls"][g], (do0, do1, do2)[g], (dl0, dl1, dl2)[g],
                                q["qw"], q["kw"], d, g, dp_main, f"att_bwd{g}{tag}")
        dqw, dkw = dqw + a, dkw + b
    grads["q_norm_w"] = dqw.reshape(2, 64).sum(axis=0)
    grads["k_norm_w"] = dkw.reshape(2, 64).sum(axis=0)

    _, _, w_re, w_im, c_re, c_im = q["s5"]
    dp_s5, dwre, dwim, dcre, dcim, dlam_re, dlam_im, dd = s5_bwd(
        dys5, sv["p_s5"], sv["h_re"], sv["h_im"], *q["pw"], w_re, w_im, c_re, c_im, q["s5_d"], f"s5_bwd{tag}")
    s5_names = ("s5_a_re", "s5_a_im", "s5_log_step", "s5_b_re", "s5_b_im", "s5_c_re", "s5_c_im")
    _, disc_vjp = jax.vjp(_s5_discretize, *[p[n] for n in s5_names])
    for n, gr in zip(s5_names, disc_vjp((dlam_re, dlam_im, dwre, dwim, dcre, dcim))):
        grads[n] = gr
    grads["s5_d"] = dd.reshape(512)

    dsegs = [dp_s5, dp_main]
    dws = [mm_tn(sv["h"], ds, f"dw_in{k}{tag}") for k, ds in enumerate(dsegs)]
    grads["w_in"] = _in_proj_grad(*dws)
    dh = None
    for k, (ds, w) in enumerate(zip(dsegs, q["segs"])):
        dh = mm_nt(ds, w, f"dh{k}{tag}", acc=dh)
    dx, dnorm_w = rms_bwd(sv["x"], q["norm_w"], dh, dout, f"rms_bwd{tag}")
    grads["norm_w"] = dnorm_w.reshape(D_MODEL)
    return dx, grads


def _exchange(name, scatter=(), gather=(), sibling=(), sibling_both=False):
    scatter, gather, sibling = list(scatter), list(gather), list(sibling)
    chip_xs = scatter + gather
    ns, nc, nb = len(scatter), len(chip_xs), len(sibling)
    n = nc + nb

    def body(*refs):
        x_refs, o_refs, send_sems, recv_sems = refs[:n], refs[n:2 * n], refs[2 * n], refs[2 * n + 1]
        mx, my, mc = lax.axis_index("x"), lax.axis_index("y"), lax.axis_index("c")
        me = 2 * mx + my
        copies = []
        for a in range(nc):
            for t, (px, py) in enumerate(((1 - mx, my), (mx, 1 - my), (1 - mx, 1 - my))):
                src = x_refs[a].at[2 * px + py] if a < ns else x_refs[a]
                copies.append(pltpu.make_async_remote_copy(
                    src_ref=src, dst_ref=o_refs[a].at[me], send_sem=send_sems.at[3 * a + t],
                    recv_sem=recv_sems.at[3 * a + t], device_id=(px, py, mc), device_id_type=pl.DeviceIdType.MESH))
        for b in range(nc, n):
            k = 3 * nc + b - nc
            copies.append(pltpu.make_async_remote_copy(
                src_ref=x_refs[b], dst_ref=o_refs[b].at[mc] if sibling_both else o_refs[b], send_sem=send_sems.at[k],
                recv_sem=recv_sems.at[k], device_id=(mx, my, 1 - mc), device_id_type=pl.DeviceIdType.MESH))
        for cp in copies:
            cp.start()
        for cp in copies:
            cp.wait()

    shapes = ([(4,) + tuple(x.shape[1:]) for x in scatter] + [(4,) + tuple(x.shape) for x in gather]
              + [((2,) if sibling_both else ()) + tuple(x.shape) for x in sibling])
    xs = chip_xs + sibling
    outs = pl.pallas_call(
        body, name=name, in_specs=[_ANY] * n, out_specs=[_ANY] * n,
        out_shape=[jax.ShapeDtypeStruct(s, x.dtype) for s, x in zip(shapes, xs)],
        scratch_shapes=[pltpu.SemaphoreType.DMA((3 * nc + nb,)), pltpu.SemaphoreType.DMA((3 * nc + nb,))],
    )(*xs)
    me, c = 2 * lax.axis_index("x") + lax.axis_index("y"), lax.axis_index("c")
    fixed = []
    for a, (o, x) in enumerate(zip(outs, xs)):
        if a < ns:
            o = lax.dynamic_update_index_in_dim(o, lax.dynamic_index_in_dim(x, me, 0, keepdims=True), me, 0)
        elif a < nc:
            o = lax.dynamic_update_index_in_dim(o, x[None], me, 0)
        elif sibling_both:
            o = lax.dynamic_update_index_in_dim(o, x[None], c, 0)
        fixed.append(o)
    return fixed[:ns], fixed[ns:nc], fixed[nc:]


def _rows_tile(rows, row_bytes, budget=5 << 19):
    return next(t for t in (512, 256, 128, 64, 32, 16, 8) if rows % t == 0 and t * row_bytes <= budget)


def _padded_row_bytes(cols):
    return -(-cols // LANES) * LANES * 4


def _add2(a, b, name, out_dtype=f32):
    R, C = a.shape
    tr = _rows_tile(R, _padded_row_bytes(C))

    def body(a_ref, b_ref, o_ref):
        o_ref[...] = (a_ref[...] + b_ref[...]).astype(out_dtype)

    spec = pl.BlockSpec((tr, C), lambda i: (i, 0))
    return _call(body, name, (R // tr,), [spec, spec], spec, jax.ShapeDtypeStruct((R, C), out_dtype),
                 sem=("parallel",))(a, b)


def _sum4(x, name):
    R = x.shape[1]
    tr = _tile(R, (2560, 1024, 512, 256, 128))

    def body(x_ref, o_ref):
        p = [x_ref[j].astype(f32) for j in range(4)]
        o_ref[...] = ((p[0] + p[1]) + p[2]) + p[3]

    return _call(body, name, (R // tr,), [pl.BlockSpec((4, tr, LANES), lambda i: (0, i, 0))],
                 pl.BlockSpec((tr, LANES), lambda i: (i, 0)), jax.ShapeDtypeStruct((R, LANES), f32),
                 sem=("parallel",))(x)


def _adamw(g_parts, w, m, v, name):
    stacked = not isinstance(g_parts, (tuple, list))
    k = g_parts.shape[0] if stacked else len(g_parts)
    R, C = w.shape
    tr = _rows_tile(R, _padded_row_bytes(C))
    c1 = 1.0 - ADAM_B1 ** ADAM_STEP
    c2 = 1.0 - ADAM_B2 ** ADAM_STEP

    def body(*refs):
        w_ref, m_ref, v_ref, g_ref, d_ref, nm_ref, nv_ref = refs[-7:]
        if stacked:
            g = refs[0][0].astype(f32)
            for j in range(1, k):
                g = g + refs[0][j].astype(f32)
        else:
            g = refs[0][...]
            for r in refs[1:k]:
                g = g + r[...]
        m = ADAM_B1 * m_ref[...] + (1.0 - ADAM_B1) * g
        v = ADAM_B2 * v_ref[...] + (1.0 - ADAM_B2) * (g * g)
        g_ref[...] = g
        nm_ref[...] = m
        nv_ref[...] = v
        d_ref[...] = -ADAM_LR * ((m / c1) / (jnp.sqrt(v / c2) + ADAM_EPS) + ADAM_WD * w_ref[...])

    spec = pl.BlockSpec((tr, C), lambda i: (i, 0))
    sd = jax.ShapeDtypeStruct((R, C), f32)
    g_specs = [pl.BlockSpec((k, tr, C), lambda i: (0, i, 0))] if stacked else [spec] * k
    g_args = [g_parts] if stacked else list(g_parts)
    return _call(body, name, (R // tr,), g_specs + [spec] * 3, [spec] * 4, [sd] * 4,
                 sem=("parallel",))(*g_args, w, m, v)


def _pack(arrays, row_multiple=PACK_ROWS):
    flat = jnp.concatenate([a.reshape(-1) for a in arrays])
    unit = row_multiple * LANES
    n = -(-flat.shape[0] // unit) * unit
    return jnp.pad(flat, (0, n - flat.shape[0])).reshape(n // LANES, LANES)


def _unpack(buf, shapes):
    flat = buf.reshape(-1)
    out, off = [], 0
    for s in shapes:
        n = 1
        for dim in s:
            n *= dim
        out.append(flat[off:off + n].reshape(s))
        off += n
    return out


def _to_shards(full, axis):
    s = full.shape
    t = full.reshape(s[:axis] + (4, s[axis] // 4) + s[axis + 1:])
    return jnp.moveaxis(t, axis, 0)


def _from_shards(sh, axis):
    t = jnp.moveaxis(sh, 0, axis)
    s = t.shape
    return t.reshape(s[:axis] + (s[axis] * s[axis + 1],) + s[axis + 2:])


def kernel(x, norm_w, w_in, s5_a_re, s5_a_im, s5_log_step, s5_b_re, s5_b_im, s5_c_re, s5_c_im, s5_d, s5_glu_w, s5_glu_b, q_norm_w, k_norm_w, conv_w, conv_b, dt_bias, ssd_a_log, ssd_d, ssd_norm_w, proj_a, proj_b, proj_c, w_out, loss_target, m_norm_w, m_w_in, m_s5_a_re, m_s5_a_im, m_s5_log_step, m_s5_b_re, m_s5_b_im, m_s5_c_re, m_s5_c_im, m_s5_d, m_s5_glu_w, m_s5_glu_b, m_q_norm_w, m_k_norm_w, m_conv_w, m_conv_b, m_dt_bias, m_ssd_a_log, m_ssd_d, m_ssd_norm_w, m_proj_a, m_proj_b, m_proj_c, m_w_out, v_norm_w, v_w_in, v_s5_a_re, v_s5_a_im, v_s5_log_step, v_s5_b_re, v_s5_b_im, v_s5_c_re, v_s5_c_im, v_s5_d, v_s5_glu_w, v_s5_glu_b, v_q_norm_w, v_k_norm_w, v_conv_w, v_conv_b, v_dt_bias, v_ssd_a_log, v_ssd_d, v_ssd_norm_w, v_proj_a, v_proj_b, v_proj_c, v_w_out):
    given = dict(locals())
    W = {n: given[n] for n in _WEIGHTS}
    M = {n: given["m_" + n] for n in _WEIGHTS}
    V = {n: given["v_" + n] for n in _WEIGHTS}
    n_layers = norm_w.shape[0]
    assert n_layers == 2
    c = lax.axis_index("c")

    mine_of = lambda t: lax.dynamic_index_in_dim(t, c, 0, keepdims=False)
    as_payload = lambda n: lax.bitcast_convert_type(W[n], bf16) if n == "conv_w" else W[n].astype(bf16)
    payload_shapes = [W[n].shape + ((2,) if n == "conv_w" else ()) for n, _ in _SHARDED]
    wpack = _pack([as_payload(n) for n, _ in _SHARDED])
    half_rows = wpack.shape[0] // 2
    _, (pack_half, w_in_mine_layer), _ = _exchange(
        "gather_weights", gather=[lax.dynamic_slice_in_dim(wpack, c * half_rows, half_rows),
                                  mine_of(w_in).astype(bf16)])
    _, _, (w_in_layers, pack_halves) = _exchange("share_weights", sibling=[w_in_mine_layer, pack_half],
                                                 sibling_both=True)
    gathered = jnp.moveaxis(pack_halves, 0, 1).reshape(4, 2 * half_rows, LANES)
    full = dict(W)
    pieces = [_unpack(gathered[j], payload_shapes) for j in range(4)]
    for k, (n, axis) in enumerate(_SHARDED):
        sh = jnp.stack([pieces[j][k] for j in range(4)])
        full[n] = _from_shards(lax.bitcast_convert_type(sh, f32) if n == "conv_w" else sh, axis)

    xs = x[0]
    qs, saves = [], []
    act = xs
    for l in range(n_layers):
        p = {n: full[n][l] for n in _WEIGHTS if n != "w_in"}
        p["w_in"] = [w_in_layers[l, k] for k in range(4)]
        q = _prep_layer(p)
        act, sv = layer_fwd(act, q, f"_l{l}")
        qs.append((q, p))
        saves.append(sv)
    dact, lsum = loss_and_grad(act, loss_target[0], "loss")
    loss = lax.psum(lsum[0, 0], ("x", "y", "c"))
    layer_grads = [None] * n_layers
    for l in reversed(range(n_layers)):
        q, p = qs[l]
        dact, layer_grads[l] = layer_bwd(dact, saves[l], q, p, f"_l{l}")
    grad_x = dact[None]
    G = {n: jnp.stack([layer_grads[l][n] for l in range(n_layers)]) for n in _WEIGHTS if n != "w_in"}

    repl_shapes = [W[n].shape for n in _REPL]
    small = _pack([G[n] for n in _REPL], 4 * PACK_ROWS)
    quarter = small.shape[0] // 4
    big = [_to_shards(G[n], axis).reshape(4, -1) for n, axis in _SHARDED]
    big = jnp.concatenate(big, axis=1)
    unit = PACK_ROWS * LANES
    nbig = -(-big.shape[1] // unit) * unit
    big = jnp.pad(big, ((0, 0), (0, nbig - big.shape[1]))).reshape(4, nbig // LANES, LANES)
    gpack = jnp.concatenate([big, small.reshape(4, quarter, LANES)], axis=1)
    rbig = nbig // LANES
    g0, g1 = layer_grads[0]["w_in"], layer_grads[1]["w_in"]

    (landed_pack,), _, (from_sibling,) = _exchange(
        "swap_w_in_grads_and_scatter_grads", scatter=[gpack.astype(bf16)], sibling=[jnp.where(c == 0, g1, g0)])
    flat = lambda t: t.reshape(4 * D_MODEL, W_IN_SHARD)
    shards = _add2(flat(jnp.where(c == 0, g0, g1)), flat(from_sibling), "sum_cores_w_in", out_dtype=bf16)
    mine = _sum4(landed_pack, "sum_chips")

    (landed,), _, (other,) = _exchange(
        "scatter_w_in_grads_and_swap_cores", scatter=[shards.reshape(4, D_MODEL, W_IN_SHARD)], sibling=[mine])
    w_in_mine = _adamw(landed, mine_of(w_in), mine_of(m_w_in), mine_of(v_w_in), "adamw_w_in")
    gq = _add2(mine[rbig:], other[rbig:], "sum_cores_small")

    _, (gsmall,), w_in_out = _exchange(
        "share_w_in_updates_and_gather_small", gather=[gq], sibling=w_in_mine, sibling_both=True)
    gsmall = gsmall.reshape(4 * quarter, LANES)

    wp, mp, vp = (_pack([T[n] for n, _ in _SHARDED]) for T in (W, M, V))
    outs_big = _adamw((mine[:rbig], other[:rbig]), wp, mp, vp, "adamw_sharded")
    big_out = [_unpack(o, [W[n].shape for n, _ in _SHARDED]) for o in outs_big]
    ws, ms, vs = (_pack([T[n] for n in _REPL], 4 * PACK_ROWS) for T in (W, M, V))
    outs_small = _adamw((gsmall,), ws, ms, vs, "adamw_replicated")
    small_out = [_unpack(o, repl_shapes) for o in outs_small]

    res = [dict(), dict(), dict(), dict()]
    for kind in range(4):
        res[kind]["w_in"] = w_in_out[kind]
        for k, (n, _) in enumerate(_SHARDED):
            res[kind][n] = big_out[kind][k]
        for k, n in enumerate(_REPL):
            res[kind][n] = small_out[kind][k]
    return (loss, grad_x, *[res[0][n] for n in _WEIGHTS], *[res[1][n] for n in _WEIGHTS],
            *[res[2][n] for n in _WEIGHTS], *[res[3][n] for n in _WEIGHTS])
```

```python
import functools

import jax
import jax.numpy as jnp
from jax import lax
from jax.experimental import pallas as pl
from jax.experimental.pallas import tpu as pltpu

f32 = jnp.float32
bf16 = jnp.bfloat16

D_MODEL = 1024
RMS_EPS = 1e-6
V7X_VMEM_LIMIT = 60 * 1024 * 1024
LANES = 128
NN, NT, TN = ((1,), (0,)), ((1,), (1,)), ((0,), (0,))

S5_STATES = 2048
S5_ROWS = 512
ATT_SEG = 2048
ATT_BLOCK = 128
SSD_CHUNK = 128
SSD_CHUNKS_PER_STEP = 2
SSD_WIDTH = 768
SSD_XBC = 1280
CONV_ROWS = 512
TAIL_ROWS = 256

ADAM_LR, ADAM_B1, ADAM_B2, ADAM_EPS, ADAM_WD, ADAM_STEP = 0.001, 0.9, 0.999, 1e-08, 0.01, 10

_C_UA, _C_ZA, _C_Q, _C_K, _C_V, _C_ZB, _C_XBC, _C_DT, _C_ZC, _C_GATE, _C_END = (
    0, 512, 1024, 1792, 2560, 3328, 3584, 4864, 4876, 5644, 8716)

_SHARDED = (("s5_glu_w", 1), ("conv_w", 2), ("proj_a", 2), ("proj_b", 2), ("proj_c", 2), ("w_out", 1))
W_IN_SHARD = 2179
_REPL = ("norm_w", "s5_a_re", "s5_a_im", "s5_log_step", "s5_b_re", "s5_b_im", "s5_c_re", "s5_c_im", "s5_d",
         "s5_glu_b", "q_norm_w", "k_norm_w", "conv_b", "dt_bias", "ssd_a_log", "ssd_d", "ssd_norm_w")
_WEIGHTS = ("norm_w", "w_in", "s5_a_re", "s5_a_im", "s5_log_step", "s5_b_re", "s5_b_im", "s5_c_re", "s5_c_im",
            "s5_d", "s5_glu_w", "s5_glu_b", "q_norm_w", "k_norm_w", "conv_w", "conv_b", "dt_bias", "ssd_a_log",
            "ssd_d", "ssd_norm_w", "proj_a", "proj_b", "proj_c", "w_out")
PACK_ROWS = 512


def _dot(a, b, dims):
    return lax.dot_general(a.astype(bf16), b.astype(bf16), (dims, ((), ())), preferred_element_type=f32)


_ANY = pl.BlockSpec(memory_space=pl.ANY)

MAIN_WIDTH = 8448
MAIN_SSD_BLOCK = 3
MAIN_DT_BLOCK = 46
MAIN_ATT_BLOCK = 16


def _call(body, name, grid, in_specs, out_specs, out_shape, scratch=(), sem=None, aliases=None):
    return pl.pallas_call(
        body, name=name, grid=grid, in_specs=in_specs, out_specs=out_specs, out_shape=out_shape,
        scratch_shapes=list(scratch), input_output_aliases=aliases or {},
        compiler_params=pltpu.CompilerParams(dimension_semantics=sem, vmem_limit_bytes=V7X_VMEM_LIMIT))


def _tile(n, options=(1024, 768, 512, 384, 256, 128)):
    return next(t for t in options if n % t == 0)


@functools.partial(jax.custom_vjp, nondiff_argnums=(2,))
def _bdot(a, b, dims):
    return _dot(a, b, dims)


def _bdot_fwd(a, b, dims):
    return _dot(a, b, dims), (a, b)


def _bdot_bwd(dims, res, g):
    a, b = res
    if dims == NN:
        da, db = _dot(g, b, NT), _dot(a, g, TN)
    elif dims == NT:
        da, db = _dot(g, b, NN), _dot(g, a, TN)
    else:
        da, db = _dot(b, g, NT), _dot(a, g, NN)
    return da.astype(a.dtype), db.astype(b.dtype)


_bdot.defvjp(_bdot_fwd, _bdot_bwd)


@functools.partial(jax.custom_vjp, nondiff_argnums=(2,))
def _cdot(a, w, dims):
    return _dot(a, w, dims)


def _cdot_fwd(a, w, dims):
    return _dot(a, w, dims), w


def _cdot_bwd(dims, w, g):
    da = _dot(g, w, NT) if dims == NN else _dot(g, w, NN)
    return da, jnp.zeros_like(w)


_cdot.defvjp(_cdot_fwd, _cdot_bwd)


def _split3(x):
    hi = x.astype(bf16)
    r = x - hi.astype(f32)
    mid = r.astype(bf16)
    lo = (r - mid.astype(f32)).astype(bf16)
    return hi, mid, lo


@jax.custom_vjp
def _xdot_l(m, x):
    return sum(_dot(m, p, NN) for p in _split3(x))


def _xdot_l_fwd(m, x):
    return _xdot_l(m, x), m


def _xdot_l_bwd(m, g):
    return jnp.zeros_like(m), sum(_dot(m, p, TN) for p in _split3(g))


_xdot_l.defvjp(_xdot_l_fwd, _xdot_l_bwd)


@jax.custom_vjp
def _softplus(x):
    e = jnp.exp(-jnp.abs(x))
    u = 1.0 + e
    log1p = jnp.where(u == 1.0, e, jnp.log(u) * (e / jnp.where(u == 1.0, 1.0, u - 1.0)))
    return jnp.maximum(x, 0.0) + log1p


def _softplus_fwd(x):
    return _softplus(x), x


def _softplus_bwd(x, g):
    return (g * jax.nn.sigmoid(x),)


_softplus.defvjp(_softplus_fwd, _softplus_bwd)


def _rms(x, w):
    return x * lax.rsqrt(jnp.mean(x * x, axis=-1, keepdims=True) + RMS_EPS) * w


def mm_nn(a, b, name, tm=2048):
    M, K = a.shape
    N = b.shape[1]
    tn = _tile(N)

    def body(a_ref, b_ref, o_ref):
        o_ref[...] = _dot(a_ref[...], b_ref[...], NN)

    return _call(body, name, (M // tm, N // tn),
                 [pl.BlockSpec((tm, K), lambda i, j: (i, 0)), pl.BlockSpec((K, tn), lambda i, j: (0, j))],
                 pl.BlockSpec((tm, tn), lambda i, j: (i, j)), jax.ShapeDtypeStruct((M, N), f32),
                 sem=("parallel", "parallel"))(a, b)


def mm_nt(a, b, name, acc=None, tm=1024):
    M, K = a.shape
    N = b.shape[0]
    tk = _tile(K, (2816, 1024, 768, 512, 256, 128))
    has_acc = acc is not None

    def body(*refs):
        a_ref, b_ref = refs[0], refs[1]
        o_ref = refs[-1]
        k = pl.program_id(1)
        p = _dot(a_ref[...], b_ref[...], NT)

        @pl.when(k == 0)
        def _():
            o_ref[...] = p + refs[2][...] if has_acc else p

        @pl.when(k > 0)
        def _():
            o_ref[...] += p

    specs = [pl.BlockSpec((tm, tk), lambda i, k: (i, k)), pl.BlockSpec((N, tk), lambda i, k: (0, k))]
    args = [a, b]
    if has_acc:
        specs.append(pl.BlockSpec((tm, N), lambda i, k: (i, 0)))
        args.append(acc)
    return _call(body, name, (M // tm, K // tk), specs, pl.BlockSpec((tm, N), lambda i, k: (i, 0)),
                 jax.ShapeDtypeStruct((M, N), f32), sem=("parallel", "arbitrary"))(*args)


def mm_tn(a, b, name, tk=2048):
    K, M = a.shape
    N = b.shape[1]
    tn = _tile(N)

    def body(a_ref, b_ref, o_ref):
        k = pl.program_id(1)
        p = _dot(a_ref[...], b_ref[...], TN)

        @pl.when(k == 0)
        def _():
            o_ref[...] = p

        @pl.when(k > 0)
        def _():
            o_ref[...] += p

    return _call(body, name, (N // tn, K // tk),
                 [pl.BlockSpec((tk, M), lambda j, k: (k, 0)), pl.BlockSpec((tk, tn), lambda j, k: (k, j))],
                 pl.BlockSpec((M, tn), lambda j, k: (0, j)), jax.ShapeDtypeStruct((M, N), f32),
                 sem=("parallel", "arbitrary"))(a, b)


def rms_fwd(x, w, name, tm=512):
    S = x.shape[0]

    def body(x_ref, w_ref, o_ref):
        o_ref[...] = _rms(x_ref[...], w_ref[...]).astype(bf16)

    return _call(body, name, (S // tm,),
                 [pl.BlockSpec((tm, D_MODEL), lambda i: (i, 0)), pl.BlockSpec((1, D_MODEL), lambda i: (0, 0))],
                 pl.BlockSpec((tm, D_MODEL), lambda i: (i, 0)), jax.ShapeDtypeStruct((S, D_MODEL), bf16),
                 sem=("parallel",))(x, w)


def rms_bwd(x, w, dh, dres, name, tm=512):
    S = x.shape[0]

    def body(x_ref, w_ref, dh_ref, dr_ref, dx_ref, dw_ref):
        _, vjp = jax.vjp(_rms, x_ref[...], w_ref[...])
        dx, dw = vjp(dh_ref[...])
        dx_ref[...] = dx + dr_ref[...]

        @pl.when(pl.program_id(0) == 0)
        def _():
            dw_ref[...] = dw

        @pl.when(pl.program_id(0) > 0)
        def _():
            dw_ref[...] += dw

    row = pl.BlockSpec((tm, D_MODEL), lambda i: (i, 0))
    vec = pl.BlockSpec((1, D_MODEL), lambda i: (0, 0))
    return _call(body, name, (S // tm,), [row, vec, row, row], [row, vec],
                 [jax.ShapeDtypeStruct((S, D_MODEL), f32), jax.ShapeDtypeStruct((1, D_MODEL), f32)],
                 sem=("arbitrary",))(x, w, dh, dres)


def loss_and_grad(y, target, name, tm=512):
    S = y.shape[0]

    def body(y_ref, t_ref, dy_ref, l_ref):
        diff = y_ref[...] - t_ref[...]
        dy_ref[...] = diff * (1.0 / D_MODEL)
        part = jnp.full((8, LANES), 0.5 / D_MODEL * jnp.sum(diff * diff), f32)

        @pl.when(pl.program_id(0) == 0)
        def _():
            l_ref[...] = part

        @pl.when(pl.program_id(0) > 0)
        def _():
            l_ref[...] += part

    row = pl.BlockSpec((tm, D_MODEL), lambda i: (i, 0))
    return _call(body, name, (S // tm,), [row, row], [row, pl.BlockSpec((8, LANES), lambda i: (0, 0))],
                 [jax.ShapeDtypeStruct((S, D_MODEL), f32), jax.ShapeDtypeStruct((8, LANES), f32)],
                 sem=("arbitrary",))(y, target)


def _s5_discretize(a_re, a_im, log_step, b_re, b_im, c_re, c_im):
    step = jnp.exp(log_step)[:, None]
    mag = jnp.exp(a_re * step)
    ang = a_im * step
    lam_re, lam_im = mag * jnp.cos(ang), mag * jnp.sin(ang)
    num_re, num_im = lam_re - 1.0, lam_im
    den = a_re * a_re + a_im * a_im
    f_re = (num_re * a_re + num_im * a_im) / den
    f_im = (num_im * a_re - num_re * a_im) / den
    bb_re = f_re[..., None] * b_re - f_im[..., None] * b_im
    bb_im = f_re[..., None] * b_im + f_im[..., None] * b_re
    eye = jnp.eye(8, dtype=f32)

    def block_in(bb):
        t = bb.transpose(0, 2, 1).reshape(4, 8, 16, 1, 64)
        return (t * eye[None, :, None, :, None]).reshape(4, 128, 512)

    def block_out(c):
        t = c.transpose(0, 2, 1).reshape(4, 8, 64, 1, 16)
        return (t * eye[None, :, None, :, None]).reshape(4, 512, 128)

    return (lam_re.reshape(1, S5_STATES), lam_im.reshape(1, S5_STATES), block_in(bb_re), block_in(bb_im),
            block_out(c_re), block_out(c_im))


def _lam_powers(lam_re, lam_im):
    rows_re, rows_im = [lam_re], [lam_im]
    for _ in range(7):
        pr, pi = rows_re[-1], rows_im[-1]
        rows_re.append(pr * lam_re - pi * lam_im)
        rows_im.append(pr * lam_im + pi * lam_re)
    return jnp.concatenate(rows_re, 0), jnp.concatenate(rows_im, 0)


def s5_fwd(u, pw_re, pw_im, w_re, w_im, c_re, c_im, dvec, name):
    S = u.shape[0]
    R, NS = S5_ROWS, S5_STATES
    nb = R // 8

    def body(u_ref, pwr_ref, pwi_ref, wre_ref, wim_ref, cre_ref, cim_ref, d_ref, y_ref, hr_ref, hi_ref,
             car_re, car_im, cin_re, cin_im, up, yp):
        @pl.when(pl.program_id(0) == 0)
        def _():
            car_re[...] = jnp.zeros_like(car_re)
            car_im[...] = jnp.zeros_like(car_im)

        slab = lambda r: pl.ds(r * nb, nb)
        for r in range(8):
            up[slab(r), :] = u_ref[:, r, :]
        u = up[...]
        for j in range(4):
            uj = u[:, 128 * j:128 * (j + 1)]
            hr_ref[:, 512 * j:512 * (j + 1)] = _dot(uj, wre_ref[j], NN)
            hi_ref[:, 512 * j:512 * (j + 1)] = _dot(uj, wim_ref[j], NN)
        lr, li = pwr_ref[0:1, :], pwi_ref[0:1, :]
        for r in range(1, 8):
            pr, pi = hr_ref[slab(r - 1), :], hi_ref[slab(r - 1), :]
            hr_ref[slab(r), :] = lr * pr - li * pi + hr_ref[slab(r), :]
            hi_ref[slab(r), :] = lr * pi + li * pr + hi_ref[slab(r), :]
        l8r, l8i = pwr_ref[7:8, :], pwi_ref[7:8, :]

        def across(c, carry):
            gr, gi = carry
            cin_re[pl.ds(c, 1), :] = gr
            cin_im[pl.ds(c, 1), :] = gi
            er, ei = hr_ref[pl.ds(7 * nb + c, 1), :], hi_ref[pl.ds(7 * nb + c, 1), :]
            return l8r * gr - l8i * gi + er, l8r * gi + l8i * gr + ei

        gr, gi = lax.fori_loop(0, nb, across, (car_re[...], car_im[...]))
        car_re[...] = gr
        car_im[...] = gi
        cr, ci = cin_re[...], cin_im[...]
        for r in range(8):
            pr, pi = pwr_ref[r:r + 1, :], pwi_ref[r:r + 1, :]
            hr_ref[slab(r), :] = hr_ref[slab(r), :] + pr * cr - pi * ci
            hi_ref[slab(r), :] = hi_ref[slab(r), :] + pr * ci + pi * cr
        for j in range(4):
            sl = slice(512 * j, 512 * (j + 1))
            cs = slice(128 * j, 128 * (j + 1))
            yp[:, cs] = (_dot(hr_ref[:, sl], cre_ref[j], NN) - _dot(hi_ref[:, sl], cim_ref[j], NN)
                         + d_ref[:, cs] * u[:, cs])
        for r in range(8):
            y_ref[:, r, :] = yp[slab(r), :]

    full = lambda shape: pl.BlockSpec(shape, lambda i: (0,) * len(shape))
    hspec = pl.BlockSpec((R, NS), lambda i: (i, 0))
    uspec = pl.BlockSpec((nb, 8, 512), lambda i: (i, 0, 0))
    y, h_re, h_im = _call(
        body, name, (S // R,),
        [uspec, full((8, NS)), full((8, NS)), full((4, 128, 512)),
         full((4, 128, 512)), full((4, 512, 128)), full((4, 512, 128)), full((1, 512))],
        [uspec, hspec, hspec],
        [jax.ShapeDtypeStruct((S // 8, 8, 512), f32), jax.ShapeDtypeStruct((S, NS), f32),
         jax.ShapeDtypeStruct((S, NS), f32)],
        scratch=[pltpu.VMEM((1, NS), f32), pltpu.VMEM((1, NS), f32), pltpu.VMEM((nb, NS), f32),
                 pltpu.VMEM((nb, NS), f32), pltpu.VMEM((R, 512), f32), pltpu.VMEM((R, 512), f32)],
        sem=("arbitrary",))(u.reshape(S // 8, 8, 512), pw_re, pw_im, w_re.astype(bf16), w_im.astype(bf16),
                            c_re.astype(bf16), c_im.astype(bf16), dvec)
    return y.reshape(S, 512), h_re, h_im


def s5_bwd(dy, u, h_re, h_im, pw_re, pw_im, w_re, w_im, c_re, c_im, dvec, name):
    S = u.shape[0]
    R, NS = S5_ROWS, S5_STATES
    nb = R // 8
    nchunk = S // R

    def body(dy_ref, u_ref, hr_ref, hi_ref, hpr_ref, hpi_ref, pwr_ref, pwi_ref, wre_ref, wim_ref, cre_ref, cim_ref,
             d_ref, du_ref, dwre_ref, dwim_ref, dcre_ref, dcim_ref, dlr_ref, dli_ref, dd_ref,
             ar, ai, car_re, car_im, cin_re, cin_im, up, dyp, dup):
        i = pl.program_id(0)

        @pl.when(i == 0)
        def _():
            for ref in (car_re, car_im, dwre_ref, dwim_ref, dcre_ref, dcim_ref, dlr_ref, dli_ref, dd_ref):
                ref[...] = jnp.zeros_like(ref)

        slab = lambda r: pl.ds(r * nb, nb)
        for r in range(8):
            up[slab(r), :] = u_ref[:, r, :]
            dyp[slab(r), :] = dy_ref[:, r, :]
        dy = dyp[...]
        u = up[...]
        for j in range(4):
            dyj = dy[:, 128 * j:128 * (j + 1)]
            ar[:, 512 * j:512 * (j + 1)] = _dot(dyj, cre_ref[j], NT)
            ai[:, 512 * j:512 * (j + 1)] = -_dot(dyj, cim_ref[j], NT)
        lr, li = pwr_ref[0:1, :], pwi_ref[0:1, :]
        for r in range(6, -1, -1):
            nr, ni = ar[slab(r + 1), :], ai[slab(r + 1), :]
            ar[slab(r), :] = lr * nr + li * ni + ar[slab(r), :]
            ai[slab(r), :] = lr * ni - li * nr + ai[slab(r), :]
        l8r, l8i = pwr_ref[7:8, :], pwi_ref[7:8, :]

        def across(k, carry):
            c = nb - 1 - k
            gr, gi = carry
            cin_re[pl.ds(c, 1), :] = gr
            cin_im[pl.ds(c, 1), :] = gi
            er, ei = ar[pl.ds(c, 1), :], ai[pl.ds(c, 1), :]
            return l8r * gr + l8i * gi + er, l8r * gi - l8i * gr + ei

        gr, gi = lax.fori_loop(0, nb, across, (car_re[...], car_im[...]))
        car_re[...] = gr
        car_im[...] = gi
        cr, ci = cin_re[...], cin_im[...]
        for r in range(8):
            pr, pi = pwr_ref[7 - r:8 - r, :], pwi_ref[7 - r:8 - r, :]
            ar[slab(r), :] = ar[slab(r), :] + pr * cr + pi * ci
            ai[slab(r), :] = ai[slab(r), :] + pr * ci - pi * cr

        acc_r = jnp.zeros((1, NS), f32)
        acc_i = jnp.zeros((1, NS), f32)
        has_prev = (i < nchunk - 1).astype(f32)
        top = lax.broadcasted_iota(jnp.int32, (nb, NS), 0) == 0
        for r in range(8):
            if r == 0:
                xr = jnp.where(top, hpr_ref[7:8, :] * has_prev, pltpu.roll(hr_ref[slab(7), :], 1, 0))
                xi = jnp.where(top, hpi_ref[7:8, :] * has_prev, pltpu.roll(hi_ref[slab(7), :], 1, 0))
            else:
                xr, xi = hr_ref[slab(r - 1), :], hi_ref[slab(r - 1), :]
            br, bi = ar[slab(r), :], ai[slab(r), :]
            acc_r += jnp.sum(br * xr + bi * xi, axis=0, keepdims=True)
            acc_i += jnp.sum(bi * xr - br * xi, axis=0, keepdims=True)
        dlr_ref[...] += acc_r
        dli_ref[...] += acc_i
        dd_ref[...] += jnp.sum(dy * u, axis=0, keepdims=True)

        for j in range(4):
            sl = slice(512 * j, 512 * (j + 1))
            cs = slice(128 * j, 128 * (j + 1))
            arj, aij = ar[:, sl], ai[:, sl]
            uj, dyj = u[:, cs], dy[:, cs]
            dup[:, cs] = _dot(arj, wre_ref[j], NT) + _dot(aij, wim_ref[j], NT) + d_ref[:, cs] * dyj
            dwre_ref[j] += _dot(uj, arj, TN)
            dwim_ref[j] += _dot(uj, aij, TN)
            dcre_ref[j] += _dot(hr_ref[:, sl], dyj, TN)
            dcim_ref[j] -= _dot(hi_ref[:, sl], dyj, TN)
        for r in range(8):
            du_ref[:, r, :] = dup[slab(r), :]

    rev = lambda i: nchunk - 1 - i
    full = lambda shape: pl.BlockSpec(shape, lambda i: (0,) * len(shape))
    row = pl.BlockSpec((nb, 8, 512), lambda i: (rev(i), 0, 0))
    hspec = pl.BlockSpec((R, NS), lambda i: (rev(i), 0))
    hprev = pl.BlockSpec((8, NS), lambda i: (jnp.maximum(rev(i) * nb - 1, 0), 0))
    outs = _call(
        body, name, (nchunk,),
        [row, row, hspec, hspec, hprev, hprev, full((8, NS)), full((8, NS)), full((4, 128, 512)), full((4, 128, 512)),
         full((4, 512, 128)), full((4, 512, 128)), full((1, 512))],
        [row, full((4, 128, 512)), full((4, 128, 512)), full((4, 512, 128)), full((4, 512, 128)),
         full((1, NS)), full((1, NS)), full((1, 512))],
        [jax.ShapeDtypeStruct((S // 8, 8, 512), f32), jax.ShapeDtypeStruct((4, 128, 512), f32),
         jax.ShapeDtypeStruct((4, 128, 512), f32), jax.ShapeDtypeStruct((4, 512, 128), f32),
         jax.ShapeDtypeStruct((4, 512, 128), f32), jax.ShapeDtypeStruct((1, NS), f32),
         jax.ShapeDtypeStruct((1, NS), f32), jax.ShapeDtypeStruct((1, 512), f32)],
        scratch=[pltpu.VMEM((R, NS), f32), pltpu.VMEM((R, NS), f32), pltpu.VMEM((1, NS), f32),
                 pltpu.VMEM((1, NS), f32), pltpu.VMEM((nb, NS), f32), pltpu.VMEM((nb, NS), f32),
                 pltpu.VMEM((R, 512), f32), pltpu.VMEM((R, 512), f32), pltpu.VMEM((R, 512), f32)],
        sem=("arbitrary",))(dy.reshape(S // 8, 8, 512), u.reshape(S // 8, 8, 512), h_re, h_im, h_re, h_im, pw_re,
                            pw_im, w_re.astype(bf16), w_im.astype(bf16), c_re.astype(bf16), c_im.astype(bf16), dvec)
    return (outs[0].reshape(S, 512),) + tuple(outs[1:])


def _rows(start, n, d):
    return pl.ds(pl.multiple_of(start, ATT_BLOCK), n) if d == 1 else pl.ds(start, n, stride=d)


def _head_masks():
    lane = lax.broadcasted_iota(jnp.int32, (1, LANES), 1)
    return [(lane < 64).astype(f32), (lane >= 64).astype(f32)]


def _head_norm(x, w, hm):
    x2 = x * x
    r = [lax.rsqrt(jnp.sum(x2 * hm[h], axis=-1, keepdims=True) * (1.0 / 64) + RMS_EPS) for h in range(2)]
    sc = hm[0] * r[0] + hm[1] * r[1]
    return x * sc * w, sc, r


def _head_norm_bwd(x, w, sc, r, dxn, hm):
    dw = jnp.sum(dxn * x * sc, axis=0, keepdims=True)
    t = dxn * w
    tx = t * x
    corr = sum(hm[h] * (r[h] * r[h] * r[h]) * jnp.sum(tx * hm[h], axis=-1, keepdims=True) for h in range(2))
    return t * sc - x * corr * (1.0 / 64), dw


def _att_mask(has_prev):
    qi = lax.broadcasted_iota(jnp.int32, (ATT_BLOCK, 2 * ATT_BLOCK), 0) + ATT_BLOCK
    kj = lax.broadcasted_iota(jnp.int32, (ATT_BLOCK, 2 * ATT_BLOCK), 1)
    return (qi - kj >= 0) & (qi - kj <= ATT_BLOCK) & (has_prev | (kj >= ATT_BLOCK))


def _att_block_bwd(q, k, v, o, lse, do, dlse, qw, kw, has_prev):
    hm = _head_masks()
    mask = _att_mask(has_prev)
    qn, qsc, qr = _head_norm(q, qw, hm)
    kn, ksc, kr = _head_norm(k, kw, hm)
    dqn = jnp.zeros((ATT_BLOCK, LANES), f32)
    dkn = jnp.zeros((2 * ATT_BLOCK, LANES), f32)
    dv = jnp.zeros((2 * ATT_BLOCK, LANES), f32)
    for h in range(2):
        qh, do_h = qn * hm[h], do * hm[h]
        s = _dot(qh, kn, NT) * 0.125
        p = jnp.exp(jnp.where(mask, s - lse[:, 64 * h:64 * h + 1], -jnp.inf))
        dp = _dot(do_h, v, NT)
        delta = jnp.sum(do_h * o, axis=-1, keepdims=True)
        dl = jnp.sum(dlse * hm[h], axis=-1, keepdims=True)
        ds = p * (dp - delta + dl) * 0.125
        dqn = dqn + hm[h] * _dot(ds, kn, NN)
        dkn = dkn + _dot(ds, qh, TN)
        dv = dv + _dot(p, do_h, TN)
    dq, dqw = _head_norm_bwd(q, qw, qsc, qr, dqn, hm)
    dk, dkw = _head_norm_bwd(k, kw, ksc, kr, dkn, hm)
    return dq, dk, dv, dqw, dkw


def _att_block(q, k, v, qw, kw, has_prev):
    hm = _head_masks()
    qn, kn = _head_norm(q, qw, hm)[0], _head_norm(k, kw, hm)[0]
    mask = _att_mask(has_prev)
    o = jnp.zeros((ATT_BLOCK, LANES), f32)
    lse = jnp.zeros((ATT_BLOCK, LANES), f32)
    for h in range(2):
        s = _bdot(qn * hm[h], kn, NT) * 0.125
        s = jnp.where(mask, s, -jnp.inf)
        m = jnp.max(s, axis=-1, keepdims=True)
        p = jnp.exp(s - m)
        l = jnp.sum(p, axis=-1, keepdims=True)
        o = o + hm[h] * _bdot(p / l, v, NN)
        lse = lse + hm[h] * (m + jnp.log(l))
    return o, lse


def att_fwd(p_att, qw, kw, d, g, name):
    S = p_att.shape[0]
    SEG = ATT_SEG
    nblk = SEG // ATT_BLOCK

    def body(p_ref, qw_ref, kw_ref, o_ref, l_ref, q_s, k_ext, v_ext, o_s, l_s):
        seg = pl.program_id(1)

        @pl.when(seg == 0)
        def _():
            k_ext[SEG:, :] = jnp.zeros((SEG, LANES), f32)
            v_ext[SEG:, :] = jnp.zeros((SEG, LANES), f32)

        k_ext[:SEG, :] = k_ext[SEG:, :]
        v_ext[:SEG, :] = v_ext[SEG:, :]
        q_s[...] = p_ref[:, 0:128]
        k_ext[SEG:, :] = p_ref[:, 128:256]
        v_ext[SEG:, :] = p_ref[:, 256:384]
        qw_v, kw_v = qw_ref[...], kw_ref[...]

        def blk(b, carry):
            j, r = b // d, b % d
            qs = j * (ATT_BLOCK * d) + r
            ks = SEG + qs - ATT_BLOCK * d
            o, lse = _att_block(q_s[_rows(qs, ATT_BLOCK, d), :], k_ext[_rows(ks, 2 * ATT_BLOCK, d), :],
                                v_ext[_rows(ks, 2 * ATT_BLOCK, d), :], qw_v, kw_v, (seg > 0) | (j > 0))
            o_s[_rows(qs, ATT_BLOCK, d), :] = o
            l_s[_rows(qs, ATT_BLOCK, d), :] = lse
            return carry

        lax.fori_loop(0, nblk, blk, 0, unroll=4)
        o_ref[...] = o_s[...]
        l_ref[...] = l_s[...]

    vec = pl.BlockSpec((1, LANES), lambda hh, s: (0, 0))
    out = pl.BlockSpec((SEG, LANES), lambda hh, s: (s, hh))
    return _call(body, name, (2, S // SEG), [pl.BlockSpec((SEG, 384), lambda hh, s: (s, MAIN_ATT_BLOCK + 2 * g + hh)), vec, vec],
                 [out, out], [jax.ShapeDtypeStruct((S, 256), f32), jax.ShapeDtypeStruct((S, 256), f32)],
                 scratch=[pltpu.VMEM((SEG, LANES), f32), pltpu.VMEM((2 * SEG, LANES), f32),
                          pltpu.VMEM((2 * SEG, LANES), f32), pltpu.VMEM((SEG, LANES), f32),
                          pltpu.VMEM((SEG, LANES), f32)],
                 sem=("arbitrary", "arbitrary"))(p_att, qw, kw)


def att_bwd(p_att, o, lse, do, dlse, qw, kw, d, g, dp_main, name):
    S = p_att.shape[0]
    SEG = ATT_SEG
    nseg = S // SEG
    nblk = SEG // ATT_BLOCK

    def body(p_ref, pp_ref, o_ref, l_ref, do_ref, dl_ref, qw_ref, kw_ref, _, dp_ref, dqw_ref, dkw_ref,
             q_s, k_ext, v_ext, dq_s, dk_ext, dv_ext):
        hh, i = pl.program_id(0), pl.program_id(1)
        seg = nseg - 1 - i

        @pl.when(i == 0)
        def _():
            dk_ext[...] = jnp.zeros_like(dk_ext)
            dv_ext[...] = jnp.zeros_like(dv_ext)

        @pl.when((i == 0) & (hh == 0))
        def _():
            dqw_ref[...] = jnp.zeros_like(dqw_ref)
            dkw_ref[...] = jnp.zeros_like(dkw_ref)

        dk_ext[SEG:, :] = dk_ext[:SEG, :]
        dv_ext[SEG:, :] = dv_ext[:SEG, :]
        dk_ext[:SEG, :] = jnp.zeros((SEG, LANES), f32)
        dv_ext[:SEG, :] = jnp.zeros((SEG, LANES), f32)
        q_s[...] = p_ref[:, 0:128]
        k_ext[SEG:, :] = p_ref[:, 128:256]
        v_ext[SEG:, :] = p_ref[:, 256:384]
        k_ext[:SEG, :] = pp_ref[:, 128:256]
        v_ext[:SEG, :] = pp_ref[:, 256:384]
        qw_v, kw_v = qw_ref[...], kw_ref[...]

        def blk_pair(i2, carry):
            dqw, dkw = carry
            done = []
            for u in range(2):
                b = 2 * i2 + u
                j, r = b // d, b % d
                qs = j * (ATT_BLOCK * d) + r
                ks = SEG + qs - ATT_BLOCK * d
                has_prev = (seg > 0) | (j > 0)
                qrows, krows = _rows(qs, ATT_BLOCK, d), _rows(ks, 2 * ATT_BLOCK, d)
                dq, dk, dv, dqw_b, dkw_b = _att_block_bwd(
                    q_s[qrows, :], k_ext[krows, :], v_ext[krows, :], o_ref[qrows, :], l_ref[qrows, :],
                    do_ref[qrows, :], dl_ref[qrows, :], qw_v, kw_v, has_prev)
                dqw, dkw = dqw + dqw_b, dkw + dkw_b
                done.append((qrows, krows, dq, dk, dv))
            for qrows, krows, dq, dk, dv in done:
                dq_s[qrows, :] = dq
                dk_ext[krows, :] = dk_ext[krows, :] + dk
                dv_ext[krows, :] = dv_ext[krows, :] + dv
            return dqw, dkw

        zero = jnp.zeros((1, LANES), f32)
        dqw, dkw = lax.fori_loop(0, nblk // 2, blk_pair, (zero, zero))
        dqw_ref[...] += dqw
        dkw_ref[...] += dkw
        dp_ref[:, 0:128] = dq_s[...].astype(bf16)
        dp_ref[:, 128:256] = dk_ext[SEG:, :].astype(bf16)
        dp_ref[:, 256:384] = dv_ext[SEG:, :].astype(bf16)

    rev = lambda i: nseg - 1 - i
    vec = pl.BlockSpec((1, LANES), lambda hh, i: (0, 0))
    blk = MAIN_ATT_BLOCK + 2 * g
    cur = pl.BlockSpec((SEG, 384), lambda hh, i: (rev(i), blk + hh))
    prev = pl.BlockSpec((SEG, 384), lambda hh, i: (jnp.maximum(rev(i) - 1, 0), blk + hh))
    col = pl.BlockSpec((SEG, LANES), lambda hh, i: (rev(i), hh))
    big = pltpu.VMEM((2 * SEG, LANES), f32)
    one = pltpu.VMEM((SEG, LANES), f32)
    return _call(body, name, (2, nseg), [cur, prev, col, col, col, col, vec, vec, _ANY], [cur, vec, vec],
                 [jax.ShapeDtypeStruct((S, MAIN_WIDTH), bf16), jax.ShapeDtypeStruct((1, LANES), f32),
                  jax.ShapeDtypeStruct((1, LANES), f32)],
                 scratch=[one, big, big, one, big, big], sem=("arbitrary", "arbitrary"),
                 aliases={8: 0})(p_att, p_att, o, lse, do, dlse, qw, kw, dp_main)


def conv_fwd(p_ssd, conv_w, conv_b, name):
    S = p_ssd.shape[0]
    tm, C = CONV_ROWS, SSD_XBC

    def body(x_ref, xp_ref, w_ref, b_ref, o_ref):
        first = (pl.program_id(0) == 0)
        ext = jnp.concatenate([jnp.where(first, 0.0, xp_ref[:, 0:C]), x_ref[:, 0:C]], axis=0)
        acc = b_ref[...] + w_ref[3:4, :] * ext[8:, :]
        for k in range(1, 4):
            acc = acc + w_ref[3 - k:4 - k, :] * pltpu.roll(ext, k, 0)[8:, :]
        o_ref[...] = jax.nn.silu(acc)

    return _call(body, name, (S // tm,),
                 [pl.BlockSpec((tm, 1536), lambda i: (i, MAIN_SSD_BLOCK)),
                  pl.BlockSpec((8, 1536), lambda i: (jnp.maximum(i * (tm // 8) - 1, 0), MAIN_SSD_BLOCK)),
                  pl.BlockSpec((4, C), lambda i: (0, 0)), pl.BlockSpec((1, C), lambda i: (0, 0))],
                 pl.BlockSpec((tm, C), lambda i: (i, 0)), jax.ShapeDtypeStruct((S, C), f32),
                 sem=("parallel",))(p_ssd, p_ssd, conv_w, conv_b)


def conv_bwd(p_ssd, dact, ddt, conv_w, conv_b, dp_main, name):
    S = p_ssd.shape[0]
    tm, C = CONV_ROWS, SSD_XBC
    nblk = S // tm

    def body(x_ref, xp_ref, xn_ref, da_ref, dan_ref, ddt_ref, w_ref, b_ref, _, dp_ref, dw_ref, db_ref):
        i = pl.program_id(0)
        rows = tm + 8
        ext = jnp.concatenate([jnp.where(i == 0, 0.0, xp_ref[:, 0:C]), x_ref[:, 0:C], xn_ref[:, 0:C]], axis=0)
        shifted = [ext[8:, :]] + [pltpu.roll(ext, k, 0)[8:, :] for k in range(1, 4)]
        pre = b_ref[...] + w_ref[3:4, :] * shifted[0]
        for k in range(1, 4):
            pre = pre + w_ref[3 - k:4 - k, :] * shifted[k]
        sg = jax.nn.sigmoid(pre)
        dact = jnp.concatenate([da_ref[...], jnp.where(i == nblk - 1, 0.0, dan_ref[...])], axis=0)
        dpre = dact * (sg * (1.0 + pre * (1.0 - sg)))
        dx = w_ref[3:4, :] * dpre[0:tm, :]
        for k in range(1, 4):
            dx = dx + w_ref[3 - k:4 - k, :] * pltpu.roll(dpre, rows - k, 0)[0:tm, :]
        dp_ref[:, 0:C] = dx.astype(bf16)
        dp_ref[:, C:C + 128] = ddt_ref[...].astype(bf16)
        dp_ref[:, C + 128:] = jnp.zeros((tm, 128), bf16)
        dcur = dpre[0:tm, :]
        dws = [jnp.sum(dcur * shifted[3 - j][0:tm, :], axis=0, keepdims=True) for j in range(4)]
        dbs = jnp.sum(dcur, axis=0, keepdims=True)

        @pl.when(i == 0)
        def _():
            dw_ref[...] = jnp.zeros_like(dw_ref)
            db_ref[...] = jnp.zeros_like(db_ref)

        for j in range(4):
            dw_ref[j:j + 1, :] += dws[j]
        db_ref[...] += dbs

    t8 = tm // 8
    blk = MAIN_SSD_BLOCK
    return _call(body, name, (nblk,),
                 [pl.BlockSpec((tm, 1536), lambda i: (i, blk)),
                  pl.BlockSpec((8, 1536), lambda i: (jnp.maximum(i * t8 - 1, 0), blk)),
                  pl.BlockSpec((8, 1536), lambda i: (jnp.minimum((i + 1) * t8, S // 8 - 1), blk)),
                  pl.BlockSpec((tm, C), lambda i: (i, 0)),
                  pl.BlockSpec((8, C), lambda i: (jnp.minimum((i + 1) * t8, S // 8 - 1), 0)),
                  pl.BlockSpec((tm, 128), lambda i: (i, 0)),
                  pl.BlockSpec((4, C), lambda i: (0, 0)), pl.BlockSpec((1, C), lambda i: (0, 0)), _ANY],
                 [pl.BlockSpec((tm, 1536), lambda i: (i, blk)), pl.BlockSpec((4, C), lambda i: (0, 0)),
                  pl.BlockSpec((1, C), lambda i: (0, 0))],
                 [jax.ShapeDtypeStruct((S, MAIN_WIDTH), bf16), jax.ShapeDtypeStruct((4, C), f32),
                  jax.ShapeDtypeStruct((1, C), f32)],
                 sem=("arbitrary",), aliases={8: 0})(p_ssd, p_ssd, p_ssd, dact, dact, ddt, conv_w, conv_b, dp_main)


def _ssd_chunk(xbc, dtr, state, dt_bias, a_log, d_full):
    T = SSD_CHUNK
    r_i = lax.broadcasted_iota(jnp.int32, (T, T), 0)
    c_i = lax.broadcasted_iota(jnp.int32, (T, T), 1)
    tril = c_i <= r_i
    tri = tril.astype(bf16)
    lane = lax.broadcasted_iota(jnp.int32, (1, LANES), 1)
    hm = [(lane < 64).astype(f32), (lane >= 64).astype(f32)]
    column = lambda v, h: jnp.broadcast_to(v[:, h:h + 1], (T, LANES))

    def per_head_lanes(v):
        return jnp.concatenate([jnp.where(lane < 64, column(v, 2 * pp), column(v, 2 * pp + 1)) for pp in range(6)],
                               axis=1)

    xs, bm, cm = xbc[:, :768], xbc[:, 768:1024], xbc[:, 1024:1280]
    dt = _softplus(dtr + dt_bias)
    a_dt = dt * (-jnp.exp(a_log))
    a_cs = _xdot_l(tri, a_dt)
    dt_full = per_head_lanes(dt)
    acs_full = per_head_lanes(a_cs)
    last = lax.broadcasted_iota(jnp.int32, (T, SSD_WIDTH), 0) == T - 1
    tot_full = jnp.sum(jnp.where(last, acs_full, 0.0), axis=0, keepdims=True)
    xdt = xs * dt_full
    xw = xdt * jnp.exp(tot_full - acs_full)
    eacs = jnp.exp(acs_full)
    st_parts, off_parts, diag_parts = [], [], []
    for g in range(2):
        bg, cg = bm[:, 128 * g:128 * (g + 1)], cm[:, 128 * g:128 * (g + 1)]
        cols = slice(384 * g, 384 * (g + 1))
        st_parts.append(_bdot(bg, xw[:, cols], TN))
        off_parts.append(_bdot(cg, state[:, cols], NN))
        cb = _bdot(cg, bg, NT)
        for pp in range(3 * g, 3 * g + 3):
            xp = xdt[:, 128 * pp:128 * (pp + 1)]
            acc = jnp.zeros((T, LANES), f32)
            for hh in range(2):
                a_col = column(a_cs, 2 * pp + hh)
                decay = jnp.where(tril, jnp.exp(jnp.minimum(a_col - a_col.T, 0.0)), 0.0)
                acc = acc + _bdot(cb * decay, xp * hm[hh], NN)
            diag_parts.append(acc)
    new_state = state * jnp.exp(tot_full) + jnp.concatenate(st_parts, axis=1)
    y = jnp.concatenate(diag_parts, axis=1) + jnp.concatenate(off_parts, axis=1) * eacs + xs * d_full
    return y, new_state


def ssd_fwd(xact, p_ssd, dt_bias, a_log, d_full, name):
    S = xact.shape[0]
    T = SSD_CHUNK

    U = SSD_CHUNKS_PER_STEP

    def body(x_ref, p_ref, b_ref, a_ref, d_ref, y_ref, s_ref, state):
        @pl.when(pl.program_id(0) == 0)
        def _():
            state[...] = jnp.zeros_like(state)

        st = state[...]
        for u in range(U):
            rows = slice(T * u, T * (u + 1))
            s_ref[u] = st
            y, st = _ssd_chunk(x_ref[rows, :], p_ref[rows, :], st, b_ref[...], a_ref[...], d_ref[...])
            y_ref[rows, :] = y
        state[...] = st

    vec = lambda n: pl.BlockSpec((1, n), lambda i: (0, 0))
    return _call(body, name, (S // (U * T),),
                 [pl.BlockSpec((U * T, SSD_XBC), lambda i: (i, 0)),
                  pl.BlockSpec((U * T, 128), lambda i: (i, MAIN_DT_BLOCK)), vec(128), vec(128), vec(768)],
                 [pl.BlockSpec((U * T, 768), lambda i: (i, 0)), pl.BlockSpec((U, T, 768), lambda i: (i, 0, 0))],
                 [jax.ShapeDtypeStruct((S, 768), f32), jax.ShapeDtypeStruct((S // T, T, 768), f32)],
                 scratch=[pltpu.VMEM((T, 768), f32)], sem=("arbitrary",))(xact, p_ssd, dt_bias, a_log, d_full)


def ssd_bwd(xact, p_ssd, states, dy, dt_bias, a_log, d_full, name):
    S = xact.shape[0]
    T = SSD_CHUNK
    U = 1
    nc = S // (U * T)

    def body(x_ref, p_ref, s_ref, dy_ref, b_ref, a_ref, d_ref, dx_ref, ddt_ref, db_ref, da_ref, dd_ref, dstate):
        i = pl.program_id(0)

        @pl.when(i == 0)
        def _():
            for ref in (dstate, db_ref, da_ref, dd_ref):
                ref[...] = jnp.zeros_like(ref)

        dst = dstate[...]
        for u in reversed(range(U)):
            rows = slice(T * u, T * (u + 1))
            _, vjp = jax.vjp(_ssd_chunk, x_ref[rows, :], p_ref[rows, :], s_ref[u], b_ref[...], a_ref[...], d_ref[...])
            dx, ddt, dst, db, da, dd = vjp((dy_ref[rows, :], dst))
            dx_ref[rows, :] = dx
            ddt_ref[rows, :] = ddt
            db_ref[...] += db
            da_ref[...] += da
            dd_ref[...] += dd
        dstate[...] = dst

    rev = lambda i: nc - 1 - i
    vec = lambda n: pl.BlockSpec((1, n), lambda i: (0, 0))
    return _call(body, name, (nc,),
                 [pl.BlockSpec((U * T, SSD_XBC), lambda i: (rev(i), 0)),
                  pl.BlockSpec((U * T, 128), lambda i: (rev(i), MAIN_DT_BLOCK)),
                  pl.BlockSpec((U, T, 768), lambda i: (rev(i), 0, 0)), pl.BlockSpec((U * T, 768), lambda i: (rev(i), 0)),
                  vec(128), vec(128), vec(768)],
                 [pl.BlockSpec((U * T, SSD_XBC), lambda i: (rev(i), 0)), pl.BlockSpec((U * T, 128), lambda i: (rev(i), 0)),
                  vec(128), vec(128), vec(768)],
                 [jax.ShapeDtypeStruct((S, SSD_XBC), f32), jax.ShapeDtypeStruct((S, 128), f32),
                  jax.ShapeDtypeStruct((1, 128), f32), jax.ShapeDtypeStruct((1, 128), f32),
                  jax.ShapeDtypeStruct((1, 768), f32)],
                 scratch=[pltpu.VMEM((T, 768), f32)],
                 sem=("arbitrary",))(xact, p_ssd, states, dy, dt_bias, a_log, d_full)


def _tail_fn(ys5, pt, o0, o1, o2, l0, l1, l2, yssd, glu_b, nw, pr_glu, pr_a, pr_b, pr_c, x, weights):
    glu_w, pa, pb, pc, wo = weights
    gates = jax.nn.sigmoid(pt[:, :3072])
    za, zb, zc = pt[:, 3072:3584], pt[:, 3584:3840], pt[:, 3840:4608]
    g = jax.nn.gelu(ys5)
    ya = g * jax.nn.sigmoid(_cdot(g, glu_w, NN) + glu_b + pr_glu) * jax.nn.silu(za)
    m = jnp.maximum(jnp.maximum(l0, l1), l2)
    e0, e1, e2 = jnp.exp(l0 - m), jnp.exp(l1 - m), jnp.exp(l2 - m)
    yb = (e0 * o0 + e1 * o1 + e2 * o2) / (e0 + e1 + e2) * jax.nn.silu(zb)
    yc = _rms(yssd * jax.nn.silu(zc), nw)
    merged = (gates[:, :1024] * (_cdot(ya, pa, NN) + pr_a) + gates[:, 1024:2048] * (_cdot(yb, pb, NN) + pr_b)
              + gates[:, 2048:] * (_cdot(yc, pc, NN) + pr_c))
    out = x + _cdot(merged, wo, NN)
    return out, (g, ya, yb, yc, merged)


def _tail_specs(tm):
    row = lambda n: pl.BlockSpec((tm, n), lambda i: (i, 0))
    full = lambda a, b: pl.BlockSpec((a, b), lambda i: (0, 0))
    acts = [row(512), row(4608)] + [row(256)] * 6 + [row(768), row(D_MODEL)]
    consts = [full(1, 512), full(1, 768), full(512, 512), full(512, D_MODEL), full(256, D_MODEL),
              full(768, D_MODEL), full(D_MODEL, D_MODEL)]
    return row, full, acts, consts


def tail_fwd(ys5, pt, os_, ls_, yssd, x, glu_b, nw, weights, name):
    S = x.shape[0]
    tm = TAIL_ROWS
    row, full, acts, consts = _tail_specs(tm)

    def body(ys5_ref, pt_ref, o0, o1, o2, l0, l1, l2, yssd_ref, x_ref, gb_ref, nw_ref, gw, pa, pb, pc, wo, out_ref):
        z = lambda n: jnp.zeros((tm, n), f32)
        out, _ = _tail_fn(ys5_ref[...], pt_ref[...], o0[...], o1[...], o2[...], l0[...], l1[...], l2[...],
                          yssd_ref[...], gb_ref[...], nw_ref[...], z(512), z(D_MODEL), z(D_MODEL), z(D_MODEL),
                          x_ref[...], (gw[...], pa[...], pb[...], pc[...], wo[...]))
        out_ref[...] = out

    return _call(body, name, (S // tm,), acts + consts, row(D_MODEL), jax.ShapeDtypeStruct((S, D_MODEL), f32),
                 sem=("parallel",))(ys5, pt, *os_, *ls_, yssd, x, glu_b, nw, *weights)


def tail_bwd(ys5, pt, os_, ls_, yssd, dout, glu_b, nw, weights, name):
    S = dout.shape[0]
    tm = TAIL_ROWS
    row, full, acts, consts = _tail_specs(tm)

    def body(ys5_ref, pt_ref, o0, o1, o2, l0, l1, l2, yssd_ref, dout_ref, gb_ref, nw_ref, gw, pa, pb, pc, wo,
             dys5_ref, dpt_ref, do0, do1, do2, dl0, dl1, dl2, dyssd_ref, dgb_ref, dnw_ref,
             g_ref, ya_ref, yb_ref, yc_ref, mg_ref, dglu_ref, dpa_ref, dpb_ref, dpc_ref):
        z = lambda n: jnp.zeros((tm, n), f32)
        w = (gw[...], pa[...], pb[...], pc[...], wo[...])
        fn = lambda *a: _tail_fn(*a, z(D_MODEL), w)
        _, vjp, aux = jax.vjp(fn, ys5_ref[...], pt_ref[...], o0[...], o1[...], o2[...], l0[...], l1[...], l2[...],
                              yssd_ref[...], gb_ref[...], nw_ref[...], z(512), z(D_MODEL), z(D_MODEL), z(D_MODEL),
                              has_aux=True)
        (dys5, dpt, d0, d1, d2, e0, e1, e2, dyssd, dgb, dnw, dglu, dpa, dpb, dpc) = vjp(dout_ref[...])
        dys5_ref[...] = dys5
        dpt_ref[...] = dpt.astype(bf16)
        for ref, val in ((do0, d0), (do1, d1), (do2, d2), (dl0, e0), (dl1, e1), (dl2, e2)):
            ref[...] = val
        dyssd_ref[...] = dyssd
        g, ya, yb, yc, merged = aux
        for ref, val in ((g_ref, g), (ya_ref, ya), (yb_ref, yb), (yc_ref, yc), (mg_ref, merged),
                         (dglu_ref, dglu), (dpa_ref, dpa), (dpb_ref, dpb), (dpc_ref, dpc)):
            ref[...] = val.astype(bf16)

        @pl.when(pl.program_id(0) == 0)
        def _():
            dgb_ref[...] = dgb
            dnw_ref[...] = dnw

        @pl.when(pl.program_id(0) > 0)
        def _():
            dgb_ref[...] += dgb
            dnw_ref[...] += dnw

    sd = lambda n, dt=f32: jax.ShapeDtypeStruct((S, n), dt)
    out_specs = ([row(512), row(4608)] + [row(256)] * 6 + [row(768), full(1, 512), full(1, 768)]
                 + [row(512), row(512), row(256), row(768), row(D_MODEL), row(512)] + [row(D_MODEL)] * 3)
    out_shape = ([sd(512), sd(MAIN_WIDTH, bf16)] + [sd(256)] * 6 + [sd(768), jax.ShapeDtypeStruct((1, 512), f32),
                                                          jax.ShapeDtypeStruct((1, 768), f32)]
                 + [sd(512, bf16), sd(512, bf16), sd(256, bf16), sd(768, bf16), sd(D_MODEL, bf16), sd(512, bf16)]
                 + [sd(D_MODEL, bf16)] * 3)
    return _call(body, name, (S // tm,), acts + consts, out_specs, out_shape,
                 sem=("arbitrary",))(ys5, pt, *os_, *ls_, yssd, dout, glu_b, nw, *weights)


def _in_proj_segments(shards):
    dtype = shards[0].dtype

    def c(a, b):
        parts = []
        for k, sh in enumerate(shards):
            lo, hi = max(a, W_IN_SHARD * k), min(b, W_IN_SHARD * (k + 1))
            if lo < hi:
                parts.append(sh[:, lo - W_IN_SHARD * k:hi - W_IN_SHARD * k])
        return parts[0] if len(parts) == 1 else jnp.concatenate(parts, axis=1)

    atts = []
    for g in range(3):
        parts = []
        for hh in range(2):
            o = 64 * (4 * g + 2 * hh)
            parts += [c(_C_Q + o, _C_Q + o + 128), c(_C_K + o, _C_K + o + 128), c(_C_V + o, _C_V + o + 128)]
        atts.append(jnp.concatenate(parts, axis=1))
    ssd = jnp.concatenate([c(_C_XBC, _C_ZC), jnp.zeros((D_MODEL, 1536 - (_C_ZC - _C_XBC)), dtype)], axis=1)
    tail = jnp.concatenate([c(_C_GATE, _C_END), c(_C_ZA, _C_Q), c(_C_ZB, _C_XBC), c(_C_ZC, _C_GATE)], axis=1)
    return [c(_C_UA, _C_ZA), jnp.concatenate([tail, ssd] + atts, axis=1)]


def _in_proj_grad(ds5, dmain):
    dtail, dssd = dmain[:, :4608], dmain[:, 4608:6144]
    datts = [dmain[:, 6144 + 768 * g:6144 + 768 * (g + 1)] for g in range(3)]
    pick = lambda off: [datts[g][:, 384 * hh + off:384 * hh + off + 128] for g in range(3) for hh in range(2)]
    pieces = ([ds5, dtail[:, 3072:3584]] + pick(0) + pick(128) + pick(256)
              + [dtail[:, 3584:3840], dssd[:, :_C_ZC - _C_XBC], dtail[:, 3840:4608], dtail[:, :3072]])
    shards, start = [[] for _ in range(4)], 0
    for piece in pieces:
        width = piece.shape[1]
        for k in range(4):
            lo, hi = max(start, W_IN_SHARD * k), min(start + width, W_IN_SHARD * (k + 1))
            if lo < hi:
                shards[k].append(piece[:, lo - start:hi - start])
        start += width
    return jnp.stack([jnp.concatenate(s, axis=1) for s in shards])


def _prep_layer(p):
    q = {}
    q["segs"] = [s.astype(bf16) for s in _in_proj_segments(p["w_in"])]
    disc = _s5_discretize(p["s5_a_re"], p["s5_a_im"], p["s5_log_step"], p["s5_b_re"], p["s5_b_im"],
                          p["s5_c_re"], p["s5_c_im"])
    q["s5"] = disc
    q["pw"] = _lam_powers(disc[0], disc[1])
    q["s5_d"] = p["s5_d"].reshape(1, 512)
    q["qw"] = jnp.tile(p["q_norm_w"], 2).reshape(1, LANES)
    q["kw"] = jnp.tile(p["k_norm_w"], 2).reshape(1, LANES)
    q["conv_w"] = p["conv_w"]
    q["conv_b"] = p["conv_b"].reshape(1, SSD_XBC)
    pad = lambda v: jnp.pad(v, (0, LANES - v.shape[0])).reshape(1, LANES)
    q["dt_bias"], q["a_log"] = pad(p["dt_bias"]), pad(p["ssd_a_log"])
    q["d_full"] = jnp.repeat(p["ssd_d"], 64).reshape(1, SSD_WIDTH)
    q["glu_b"] = p["s5_glu_b"].reshape(1, 512)
    q["nw"] = p["ssd_norm_w"].reshape(1, SSD_WIDTH)
    q["norm_w"] = p["norm_w"].reshape(1, D_MODEL)
    q["tailw"] = tuple(p[n].astype(bf16) for n in ("s5_glu_w", "proj_a", "proj_b", "proj_c", "w_out"))
    return q


_DILATIONS = (1, 4, 16)


def layer_fwd(x, q, tag):
    h = rms_fwd(x, q["norm_w"], f"rms_fwd{tag}")
    p_s5, p_main = [mm_nn(h, w, f"inproj{k}{tag}") for k, w in enumerate(q["segs"])]
    _, _, w_re, w_im, c_re, c_im = q["s5"]
    ys5, h_re, h_im = s5_fwd(p_s5, *q["pw"], w_re, w_im, c_re, c_im, q["s5_d"], f"s5_fwd{tag}")
    os_, ls_ = [], []
    for g, d in enumerate(_DILATIONS):
        o, l = att_fwd(p_main, q["qw"], q["kw"], d, g, f"att_fwd{g}{tag}")
        os_.append(o)
        ls_.append(l)
    xact = conv_fwd(p_main, q["conv_w"], q["conv_b"], f"conv_fwd{tag}")
    yssd, states = ssd_fwd(xact, p_main, q["dt_bias"], q["a_log"], q["d_full"], f"ssd_fwd{tag}")
    out = tail_fwd(ys5, p_main, os_, ls_, yssd, x, q["glu_b"], q["nw"], q["tailw"], f"tail_fwd{tag}")
    saved = dict(x=x, h=h, p_s5=p_s5, p_main=p_main, ys5=ys5, h_re=h_re, h_im=h_im,
                 os=os_, ls=ls_, xact=xact, yssd=yssd, states=states)
    return out, saved


def layer_bwd(dout, sv, q, p, tag):
    S = dout.shape[0]
    (dys5, dp_main, do0, do1, do2, dl0, dl1, dl2, dyssd, dglu_b, dnw, g_b, ya_b, yb_b, yc_b, mg_b, dglu_b16,
     dpa_b, dpb_b, dpc_b) = tail_bwd(sv["ys5"], sv["p_main"], sv["os"], sv["ls"], sv["yssd"], dout, q["glu_b"],
                                     q["nw"], q["tailw"], f"tail_bwd{tag}")
    grads = {}
    grads["s5_glu_w"] = mm_tn(g_b, dglu_b16, f"dglu_w{tag}")
    grads["proj_a"] = mm_tn(ya_b, dpa_b, f"dproj_a{tag}")
    grads["proj_b"] = mm_tn(yb_b, dpb_b, f"dproj_b{tag}")
    grads["proj_c"] = mm_tn(yc_b, dpc_b, f"dproj_c{tag}")
    grads["w_out"] = mm_tn(mg_b, dout, f"dw_out{tag}")
    grads["s5_glu_b"] = dglu_b.reshape(512)
    grads["ssd_norm_w"] = dnw.reshape(SSD_WIDTH)

    dxact, ddt, ddt_bias, da_log, dd_full = ssd_bwd(sv["xact"], sv["p_main"], sv["states"], dyssd, q["dt_bias"],
                                                    q["a_log"], q["d_full"], f"ssd_bwd{tag}")
    dp_main, dconv_w, dconv_b = conv_bwd(sv["p_main"], dxact, ddt, q["conv_w"], q["conv_b"], dp_main,
                                         f"conv_bwd{tag}")
    grads["dt_bias"] = ddt_bias[0, :12]
    grads["ssd_a_log"] = da_log[0, :12]
    grads["ssd_d"] = dd_full.reshape(12, 64).sum(axis=1)
    grads["conv_w"] = dconv_w
    grads["conv_b"] = dconv_b.reshape(SSD_XBC)

    dqw, dkw = 0.0, 0.0
    for g, d in enumerate(_DILATIONS):
        dp_main, a, b = att_bwd(sv["p_main"], sv["os"][g], sv["ls"][g], (do0, do1, do2)[g], (dl0, dl1, dl2)[g],
                                q["qw"], q["kw"], d, g, dp_main, f"att_bwd{g}{tag}")
        dqw, dkw = dqw + a, dkw + b
    grads["q_norm_w"] = dqw.reshape(2, 64).sum(axis=0)
    grads["k_norm_w"] = dkw.reshape(2, 64).sum(axis=0)

    _, _, w_re, w_im, c_re, c_im = q["s5"]
    dp_s5, dwre, dwim, dcre, dcim, dlam_re, dlam_im, dd = s5_bwd(
        dys5, sv["p_s5"], sv["h_re"], sv["h_im"], *q["pw"], w_re, w_im, c_re, c_im, q["s5_d"], f"s5_bwd{tag}")
    s5_names = ("s5_a_re", "s5_a_im", "s5_log_step", "s5_b_re", "s5_b_im", "s5_c_re", "s5_c_im")
    _, disc_vjp = jax.vjp(_s5_discretize, *[p[n] for n in s5_names])
    for n, gr in zip(s5_names, disc_vjp((dlam_re, dlam_im, dwre, dwim, dcre, dcim))):
        grads[n] = gr
    grads["s5_d"] = dd.reshape(512)

    dsegs = [dp_s5, dp_main]
    dws = [mm_tn(sv["h"], ds, f"dw_in{k}{tag}") for k, ds in enumerate(dsegs)]
    grads["w_in"] = _in_proj_grad(*dws)
    dh = None
    for k, (ds, w) in enumerate(zip(dsegs, q["segs"])):
        dh = mm_nt(ds, w, f"dh{k}{tag}", acc=dh)
    dx, dnorm_w = rms_bwd(sv["x"], q["norm_w"], dh, dout, f"rms_bwd{tag}")
    grads["norm_w"] = dnorm_w.reshape(D_MODEL)
    return dx, grads


def _exchange(name, scatter=(), gather=(), sibling=(), sibling_both=False):
    scatter, gather, sibling = list(scatter), list(gather), list(sibling)
    chip_xs = scatter + gather
    ns, nc, nb = len(scatter), len(chip_xs), len(sibling)
    n = nc + nb

    def body(*refs):
        x_refs, o_refs, send_sems, recv_sems = refs[:n], refs[n:2 * n], refs[2 * n], refs[2 * n + 1]
        mx, my, mc = lax.axis_index("x"), lax.axis_index("y"), lax.axis_index("c")
        me = 2 * mx + my
        copies = []
        for a in range(nc):
            for t, (px, py) in enumerate(((1 - mx, my), (mx, 1 - my), (1 - mx, 1 - my))):
                src = x_refs[a].at[2 * px + py] if a < ns else x_refs[a]
                copies.append(pltpu.make_async_remote_copy(
                    src_ref=src, dst_ref=o_refs[a].at[me], send_sem=send_sems.at[3 * a + t],
                    recv_sem=recv_sems.at[3 * a + t], device_id=(px, py, mc), device_id_type=pl.DeviceIdType.MESH))
        for b in range(nc, n):
            k = 3 * nc + b - nc
            copies.append(pltpu.make_async_remote_copy(
                src_ref=x_refs[b], dst_ref=o_refs[b].at[mc] if sibling_both else o_refs[b], send_sem=send_sems.at[k],
                recv_sem=recv_sems.at[k], device_id=(mx, my, 1 - mc), device_id_type=pl.DeviceIdType.MESH))
        for cp in copies:
            cp.start()
        for cp in copies:
            cp.wait()

    shapes = ([(4,) + tuple(x.shape[1:]) for x in scatter] + [(4,) + tuple(x.shape) for x in gather]
              + [((2,) if sibling_both else ()) + tuple(x.shape) for x in sibling])
    xs = chip_xs + sibling
    outs = pl.pallas_call(
        body, name=name, in_specs=[_ANY] * n, out_specs=[_ANY] * n,
        out_shape=[jax.ShapeDtypeStruct(s, x.dtype) for s, x in zip(shapes, xs)],
        scratch_shapes=[pltpu.SemaphoreType.DMA((3 * nc + nb,)), pltpu.SemaphoreType.DMA((3 * nc + nb,))],
    )(*xs)
    me, c = 2 * lax.axis_index("x") + lax.axis_index("y"), lax.axis_index("c")
    fixed = []
    for a, (o, x) in enumerate(zip(outs, xs)):
        if a < ns:
            o = lax.dynamic_update_index_in_dim(o, lax.dynamic_index_in_dim(x, me, 0, keepdims=True), me, 0)
        elif a < nc:
            o = lax.dynamic_update_index_in_dim(o, x[None], me, 0)
        elif sibling_both:
            o = lax.dynamic_update_index_in_dim(o, x[None], c, 0)
        fixed.append(o)
    return fixed[:ns], fixed[ns:nc], fixed[nc:]


def _rows_tile(rows, row_bytes, budget=5 << 19):
    return next(t for t in (512, 256, 128, 64, 32, 16, 8) if rows % t == 0 and t * row_bytes <= budget)


def _padded_row_bytes(cols):
    return -(-cols // LANES) * LANES * 4


def _add2(a, b, name, out_dtype=f32):
    R, C = a.shape
    tr = _rows_tile(R, _padded_row_bytes(C))

    def body(a_ref, b_ref, o_ref):
        o_ref[...] = (a_ref[...] + b_ref[...]).astype(out_dtype)

    spec = pl.BlockSpec((tr, C), lambda i: (i, 0))
    return _call(body, name, (R // tr,), [spec, spec], spec, jax.ShapeDtypeStruct((R, C), out_dtype),
                 sem=("parallel",))(a, b)


def _sum4(x, name):
    R = x.shape[1]
    tr = _tile(R, (2560, 1024, 512, 256, 128))

    def body(x_ref, o_ref):
        p = [x_ref[j].astype(f32) for j in range(4)]
        o_ref[...] = ((p[0] + p[1]) + p[2]) + p[3]

    return _call(body, name, (R // tr,), [pl.BlockSpec((4, tr, LANES), lambda i: (0, i, 0))],
                 pl.BlockSpec((tr, LANES), lambda i: (i, 0)), jax.ShapeDtypeStruct((R, LANES), f32),
                 sem=("parallel",))(x)


def _adamw(g_parts, w, m, v, name):
    stacked = not isinstance(g_parts, (tuple, list))
    k = g_parts.shape[0] if stacked else len(g_parts)
    R, C = w.shape
    tr = _rows_tile(R, _padded_row_bytes(C))
    c1 = 1.0 - ADAM_B1 ** ADAM_STEP
    c2 = 1.0 - ADAM_B2 ** ADAM_STEP

    def body(*refs):
        w_ref, m_ref, v_ref, g_ref, d_ref, nm_ref, nv_ref = refs[-7:]
        if stacked:
            g = refs[0][0].astype(f32)
            for j in range(1, k):
                g = g + refs[0][j].astype(f32)
        else:
            g = refs[0][...]
            for r in refs[1:k]:
                g = g + r[...]
        m = ADAM_B1 * m_ref[...] + (1.0 - ADAM_B1) * g
        v = ADAM_B2 * v_ref[...] + (1.0 - ADAM_B2) * (g * g)
        g_ref[...] = g
        nm_ref[...] = m
        nv_ref[...] = v
        d_ref[...] = -ADAM_LR * ((m / c1) / (jnp.sqrt(v / c2) + ADAM_EPS) + ADAM_WD * w_ref[...])

    spec = pl.BlockSpec((tr, C), lambda i: (i, 0))
    sd = jax.ShapeDtypeStruct((R, C), f32)
    g_specs = [pl.BlockSpec((k, tr, C), lambda i: (0, i, 0))] if stacked else [spec] * k
    g_args = [g_parts] if stacked else list(g_parts)
    return _call(body, name, (R // tr,), g_specs + [spec] * 3, [spec] * 4, [sd] * 4,
                 sem=("parallel",))(*g_args, w, m, v)


def _pack(arrays, row_multiple=PACK_ROWS):
    flat = jnp.concatenate([a.reshape(-1) for a in arrays])
    unit = row_multiple * LANES
    n = -(-flat.shape[0] // unit) * unit
    return jnp.pad(flat, (0, n - flat.shape[0])).reshape(n // LANES, LANES)


def _unpack(buf, shapes):
    flat = buf.reshape(-1)
    out, off = [], 0
    for s in shapes:
        n = 1
        for dim in s:
            n *= dim
        out.append(flat[off:off + n].reshape(s))
        off += n
    return out


def _to_shards(full, axis):
    s = full.shape
    t = full.reshape(s[:axis] + (4, s[axis] // 4) + s[axis + 1:])
    return jnp.moveaxis(t, axis, 0)


def _from_shards(sh, axis):
    t = jnp.moveaxis(sh, 0, axis)
    s = t.shape
    return t.reshape(s[:axis] + (s[axis] * s[axis + 1],) + s[axis + 2:])


def kernel(x, norm_w, w_in, s5_a_re, s5_a_im, s5_log_step, s5_b_re, s5_b_im, s5_c_re, s5_c_im, s5_d, s5_glu_w, s5_glu_b, q_norm_w, k_norm_w, conv_w, conv_b, dt_bias, ssd_a_log, ssd_d, ssd_norm_w, proj_a, proj_b, proj_c, w_out, loss_target, m_norm_w, m_w_in, m_s5_a_re, m_s5_a_im, m_s5_log_step, m_s5_b_re, m_s5_b_im, m_s5_c_re, m_s5_c_im, m_s5_d, m_s5_glu_w, m_s5_glu_b, m_q_norm_w, m_k_norm_w, m_conv_w, m_conv_b, m_dt_bias, m_ssd_a_log, m_ssd_d, m_ssd_norm_w, m_proj_a, m_proj_b, m_proj_c, m_w_out, v_norm_w, v_w_in, v_s5_a_re, v_s5_a_im, v_s5_log_step, v_s5_b_re, v_s5_b_im, v_s5_c_re, v_s5_c_im, v_s5_d, v_s5_glu_w, v_s5_glu_b, v_q_norm_w, v_k_norm_w, v_conv_w, v_conv_b, v_dt_bias, v_ssd_a_log, v_ssd_d, v_ssd_norm_w, v_proj_a, v_proj_b, v_proj_c, v_w_out):
    given = dict(locals())
    W = {n: given[n] for n in _WEIGHTS}
    M = {n: given["m_" + n] for n in _WEIGHTS}
    V = {n: given["v_" + n] for n in _WEIGHTS}
    n_layers = norm_w.shape[0]
    assert n_layers == 2
    c = lax.axis_index("c")

    mine_of = lambda t: lax.dynamic_index_in_dim(t, c, 0, keepdims=False)
    as_payload = lambda n: lax.bitcast_convert_type(W[n], bf16) if n == "conv_w" else W[n].astype(bf16)
    payload_shapes = [W[n].shape + ((2,) if n == "conv_w" else ()) for n, _ in _SHARDED]
    wpack = _pack([as_payload(n) for n, _ in _SHARDED])
    half_rows = wpack.shape[0] // 2
    _, (pack_half, w_in_mine_layer), _ = _exchange(
        "gather_weights", gather=[lax.dynamic_slice_in_dim(wpack, c * half_rows, half_rows),
                                  mine_of(w_in).astype(bf16)])
    _, _, (w_in_layers, pack_halves) = _exchange("share_weights", sibling=[w_in_mine_layer, pack_half],
                                                 sibling_both=True)
    gathered = jnp.moveaxis(pack_halves, 0, 1).reshape(4, 2 * half_rows, LANES)
    full = dict(W)
    pieces = [_unpack(gathered[j], payload_shapes) for j in range(4)]
    for k, (n, axis) in enumerate(_SHARDED):
        sh = jnp.stack([pieces[j][k] for j in range(4)])
        full[n] = _from_shards(lax.bitcast_convert_type(sh, f32) if n == "conv_w" else sh, axis)

    xs = x[0]
    qs, saves = [], []
    act = xs
    for l in range(n_layers):
        p = {n: full[n][l] for n in _WEIGHTS if n != "w_in"}
        p["w_in"] = [w_in_layers[l, k] for k in range(4)]
        q = _prep_layer(p)
        act, sv = layer_fwd(act, q, f"_l{l}")
        qs.append((q, p))
        saves.append(sv)
    dact, lsum = loss_and_grad(act, loss_target[0], "loss")
    loss = lax.psum(lsum[0, 0], ("x", "y", "c"))
    layer_grads = [None] * n_layers
    for l in reversed(range(n_layers)):
        q, p = qs[l]
        dact, layer_grads[l] = layer_bwd(dact, saves[l], q, p, f"_l{l}")
    grad_x = dact[None]
    G = {n: jnp.stack([layer_grads[l][n] for l in range(n_layers)]) for n in _WEIGHTS if n != "w_in"}

    repl_shapes = [W[n].shape for n in _REPL]
    small = _pack([G[n] for n in _REPL], 4 * PACK_ROWS)
    quarter = small.shape[0] // 4
    big = [_to_shards(G[n], axis).reshape(4, -1) for n, axis in _SHARDED]
    big = jnp.concatenate(big, axis=1)
    unit = PACK_ROWS * LANES
    nbig = -(-big.shape[1] // unit) * unit
    big = jnp.pad(big, ((0, 0), (0, nbig - big.shape[1]))).reshape(4, nbig // LANES, LANES)
    gpack = jnp.concatenate([big, small.reshape(4, quarter, LANES)], axis=1)
    rbig = nbig // LANES
    g0, g1 = layer_grads[0]["w_in"], layer_grads[1]["w_in"]

    (landed_pack,), _, (from_sibling,) = _exchange(
        "swap_w_in_grads_and_scatter_grads", scatter=[gpack.astype(bf16)], sibling=[jnp.where(c == 0, g1, g0)])
    flat = lambda t: t.reshape(4 * D_MODEL, W_IN_SHARD)
    shards = _add2(flat(jnp.where(c == 0, g0, g1)), flat(from_sibling), "sum_cores_w_in", out_dtype=bf16)
    mine = _sum4(landed_pack, "sum_chips")

    (landed,), _, (other,) = _exchange(
        "scatter_w_in_grads_and_swap_cores", scatter=[shards.reshape(4, D_MODEL, W_IN_SHARD)], sibling=[mine])
    w_in_mine = _adamw(landed, mine_of(w_in), mine_of(m_w_in), mine_of(v_w_in), "adamw_w_in")
    gq = _add2(mine[rbig:], other[rbig:], "sum_cores_small")

    _, (gsmall,), w_in_out = _exchange(
        "share_w_in_updates_and_gather_small", gather=[gq], sibling=w_in_mine, sibling_both=True)
    gsmall = gsmall.reshape(4 * quarter, LANES)

    shard_shapes = [W[n].shape for n, _ in _SHARDED]
    g_mine, g_other = _unpack(mine[:rbig], shard_shapes), _unpack(other[:rbig], shard_shapes)
    rows_of = lambda t: t.reshape(-1, t.shape[-1])
    res = [dict(), dict(), dict(), dict()]
    for k, (n, _) in enumerate(_SHARDED):
        outs = _adamw((rows_of(g_mine[k]), rows_of(g_other[k])), rows_of(W[n]), rows_of(M[n]), rows_of(V[n]),
                      f"adamw_{n}")
        for kind in range(4):
            res[kind][n] = outs[kind].reshape(W[n].shape)
    ws, ms, vs = (_pack([T[n] for n in _REPL], 4 * PACK_ROWS) for T in (W, M, V))
    outs_small = _adamw((gsmall,), ws, ms, vs, "adamw_replicated")
    small_out = [_unpack(o, repl_shapes) for o in outs_small]

    for kind in range(4):
        res[kind]["w_in"] = w_in_out[kind]
        for k, n in enumerate(_REPL):
            res[kind][n] = small_out[kind][k]
    return (loss, grad_x, *[res[0][n] for n in _WEIGHTS], *[res[1][n] for n in _WEIGHTS],
            *[res[2][n] for n in _WEIGHTS], *[res[3][n] for n in _WEIGHTS])
```

```python
import functools

import jax
import jax.numpy as jnp
from jax import lax
from jax.experimental import pallas as pl
from jax.experimental.pallas import tpu as pltpu

f32 = jnp.float32
bf16 = jnp.bfloat16

D_MODEL = 1024
RMS_EPS = 1e-6
V7X_VMEM_LIMIT = 60 * 1024 * 1024
LANES = 128
NN, NT, TN = ((1,), (0,)), ((1,), (1,)), ((0,), (0,))

S5_STATES = 2048
S5_ROWS = 512
ATT_SEG = 2048
ATT_BLOCK = 128
SSD_CHUNK = 128
SSD_CHUNKS_PER_STEP = 2
SSD_WIDTH = 768
SSD_XBC = 1280
CONV_ROWS = 512
TAIL_ROWS = 256

ADAM_LR, ADAM_B1, ADAM_B2, ADAM_EPS, ADAM_WD, ADAM_STEP = 0.001, 0.9, 0.999, 1e-08, 0.01, 10

_C_UA, _C_ZA, _C_Q, _C_K, _C_V, _C_ZB, _C_XBC, _C_DT, _C_ZC, _C_GATE, _C_END = (
    0, 512, 1024, 1792, 2560, 3328, 3584, 4864, 4876, 5644, 8716)

_SHARDED = (("s5_glu_w", 1), ("conv_w", 2), ("proj_a", 2), ("proj_b", 2), ("proj_c", 2), ("w_out", 1))
W_IN_SHARD = 2179
_REPL = ("norm_w", "s5_a_re", "s5_a_im", "s5_log_step", "s5_b_re", "s5_b_im", "s5_c_re", "s5_c_im", "s5_d",
         "s5_glu_b", "q_norm_w", "k_norm_w", "conv_b", "dt_bias", "ssd_a_log", "ssd_d", "ssd_norm_w")
_WEIGHTS = ("norm_w", "w_in", "s5_a_re", "s5_a_im", "s5_log_step", "s5_b_re", "s5_b_im", "s5_c_re", "s5_c_im",
            "s5_d", "s5_glu_w", "s5_glu_b", "q_norm_w", "k_norm_w", "conv_w", "conv_b", "dt_bias", "ssd_a_log",
            "ssd_d", "ssd_norm_w", "proj_a", "proj_b", "proj_c", "w_out")
PACK_ROWS = 512


def _dot(a, b, dims):
    return lax.dot_general(a.astype(bf16), b.astype(bf16), (dims, ((), ())), preferred_element_type=f32)


_ANY = pl.BlockSpec(memory_space=pl.ANY)

MAIN_WIDTH = 8448
MAIN_SSD_BLOCK = 3
MAIN_DT_BLOCK = 46
MAIN_ATT_BLOCK = 16


def _call(body, name, grid, in_specs, out_specs, out_shape, scratch=(), sem=None, aliases=None):
    return pl.pallas_call(
        body, name=name, grid=grid, in_specs=in_specs, out_specs=out_specs, out_shape=out_shape,
        scratch_shapes=list(scratch), input_output_aliases=aliases or {},
        compiler_params=pltpu.CompilerParams(dimension_semantics=sem, vmem_limit_bytes=V7X_VMEM_LIMIT))


def _tile(n, options=(1024, 768, 512, 384, 256, 128)):
    return next(t for t in options if n % t == 0)


@functools.partial(jax.custom_vjp, nondiff_argnums=(2,))
def _bdot(a, b, dims):
    return _dot(a, b, dims)


def _bdot_fwd(a, b, dims):
    return _dot(a, b, dims), (a, b)


def _bdot_bwd(dims, res, g):
    a, b = res
    if dims == NN:
        da, db = _dot(g, b, NT), _dot(a, g, TN)
    elif dims == NT:
        da, db = _dot(g, b, NN), _dot(g, a, TN)
    else:
        da, db = _dot(b, g, NT), _dot(a, g, NN)
    return da.astype(a.dtype), db.astype(b.dtype)


_bdot.defvjp(_bdot_fwd, _bdot_bwd)


@functools.partial(jax.custom_vjp, nondiff_argnums=(2,))
def _cdot(a, w, dims):
    return _dot(a, w, dims)


def _cdot_fwd(a, w, dims):
    return _dot(a, w, dims), w


def _cdot_bwd(dims, w, g):
    da = _dot(g, w, NT) if dims == NN else _dot(g, w, NN)
    return da, jnp.zeros_like(w)


_cdot.defvjp(_cdot_fwd, _cdot_bwd)


def _split3(x):
    hi = x.astype(bf16)
    r = x - hi.astype(f32)
    mid = r.astype(bf16)
    lo = (r - mid.astype(f32)).astype(bf16)
    return hi, mid, lo


@jax.custom_vjp
def _xdot_l(m, x):
    return sum(_dot(m, p, NN) for p in _split3(x))


def _xdot_l_fwd(m, x):
    return _xdot_l(m, x), m


def _xdot_l_bwd(m, g):
    return jnp.zeros_like(m), sum(_dot(m, p, TN) for p in _split3(g))


_xdot_l.defvjp(_xdot_l_fwd, _xdot_l_bwd)


@jax.custom_vjp
def _softplus(x):
    e = jnp.exp(-jnp.abs(x))
    u = 1.0 + e
    log1p = jnp.where(u == 1.0, e, jnp.log(u) * (e / jnp.where(u == 1.0, 1.0, u - 1.0)))
    return jnp.maximum(x, 0.0) + log1p


def _softplus_fwd(x):
    return _softplus(x), x


def _softplus_bwd(x, g):
    return (g * jax.nn.sigmoid(x),)


_softplus.defvjp(_softplus_fwd, _softplus_bwd)


def _rms(x, w):
    return x * lax.rsqrt(jnp.mean(x * x, axis=-1, keepdims=True) + RMS_EPS) * w


def mm_nn(a, b, name, tm=2048):
    M, K = a.shape
    N = b.shape[1]
    tn = _tile(N)

    def body(a_ref, b_ref, o_ref):
        o_ref[...] = _dot(a_ref[...], b_ref[...], NN)

    return _call(body, name, (M // tm, N // tn),
                 [pl.BlockSpec((tm, K), lambda i, j: (i, 0)), pl.BlockSpec((K, tn), lambda i, j: (0, j))],
                 pl.BlockSpec((tm, tn), lambda i, j: (i, j)), jax.ShapeDtypeStruct((M, N), f32),
                 sem=("parallel", "parallel"))(a, b)


def mm_nt(a, b, name, acc=None, tm=1024):
    M, K = a.shape
    N = b.shape[0]
    tk = _tile(K, (2816, 1024, 768, 512, 256, 128))
    has_acc = acc is not None

    def body(*refs):
        a_ref, b_ref = refs[0], refs[1]
        o_ref = refs[-1]
        k = pl.program_id(1)
        p = _dot(a_ref[...], b_ref[...], NT)

        @pl.when(k == 0)
        def _():
            o_ref[...] = p + refs[2][...] if has_acc else p

        @pl.when(k > 0)
        def _():
            o_ref[...] += p

    specs = [pl.BlockSpec((tm, tk), lambda i, k: (i, k)), pl.BlockSpec((N, tk), lambda i, k: (0, k))]
    args = [a, b]
    if has_acc:
        specs.append(pl.BlockSpec((tm, N), lambda i, k: (i, 0)))
        args.append(acc)
    return _call(body, name, (M // tm, K // tk), specs, pl.BlockSpec((tm, N), lambda i, k: (i, 0)),
                 jax.ShapeDtypeStruct((M, N), f32), sem=("parallel", "arbitrary"))(*args)


def mm_tn(a, b, name, tk=2048):
    K, M = a.shape
    N = b.shape[1]
    tn = _tile(N)

    def body(a_ref, b_ref, o_ref):
        k = pl.program_id(1)
        p = _dot(a_ref[...], b_ref[...], TN)

        @pl.when(k == 0)
        def _():
            o_ref[...] = p

        @pl.when(k > 0)
        def _():
            o_ref[...] += p

    return _call(body, name, (N // tn, K // tk),
                 [pl.BlockSpec((tk, M), lambda j, k: (k, 0)), pl.BlockSpec((tk, tn), lambda j, k: (k, j))],
                 pl.BlockSpec((M, tn), lambda j, k: (0, j)), jax.ShapeDtypeStruct((M, N), f32),
                 sem=("parallel", "arbitrary"))(a, b)


def rms_fwd(x, w, name, tm=512):
    S = x.shape[0]

    def body(x_ref, w_ref, o_ref):
        o_ref[...] = _rms(x_ref[...], w_ref[...]).astype(bf16)

    return _call(body, name, (S // tm,),
                 [pl.BlockSpec((tm, D_MODEL), lambda i: (i, 0)), pl.BlockSpec((1, D_MODEL), lambda i: (0, 0))],
                 pl.BlockSpec((tm, D_MODEL), lambda i: (i, 0)), jax.ShapeDtypeStruct((S, D_MODEL), bf16),
                 sem=("parallel",))(x, w)


def rms_bwd(x, w, dh, dres, name, tm=512):
    S = x.shape[0]

    def body(x_ref, w_ref, dh_ref, dr_ref, dx_ref, dw_ref):
        _, vjp = jax.vjp(_rms, x_ref[...], w_ref[...])
        dx, dw = vjp(dh_ref[...])
        dx_ref[...] = dx + dr_ref[...]

        @pl.when(pl.program_id(0) == 0)
        def _():
            dw_ref[...] = dw

        @pl.when(pl.program_id(0) > 0)
        def _():
            dw_ref[...] += dw

    row = pl.BlockSpec((tm, D_MODEL), lambda i: (i, 0))
    vec = pl.BlockSpec((1, D_MODEL), lambda i: (0, 0))
    return _call(body, name, (S // tm,), [row, vec, row, row], [row, vec],
                 [jax.ShapeDtypeStruct((S, D_MODEL), f32), jax.ShapeDtypeStruct((1, D_MODEL), f32)],
                 sem=("arbitrary",))(x, w, dh, dres)


def loss_and_grad(y, target, name, tm=512):
    S = y.shape[0]

    def body(y_ref, t_ref, dy_ref, l_ref):
        diff = y_ref[...] - t_ref[...]
        dy_ref[...] = diff * (1.0 / D_MODEL)
        part = jnp.full((8, LANES), 0.5 / D_MODEL * jnp.sum(diff * diff), f32)

        @pl.when(pl.program_id(0) == 0)
        def _():
            l_ref[...] = part

        @pl.when(pl.program_id(0) > 0)
        def _():
            l_ref[...] += part

    row = pl.BlockSpec((tm, D_MODEL), lambda i: (i, 0))
    return _call(body, name, (S // tm,), [row, row], [row, pl.BlockSpec((8, LANES), lambda i: (0, 0))],
                 [jax.ShapeDtypeStruct((S, D_MODEL), f32), jax.ShapeDtypeStruct((8, LANES), f32)],
                 sem=("arbitrary",))(y, target)


def _s5_discretize(a_re, a_im, log_step, b_re, b_im, c_re, c_im):
    step = jnp.exp(log_step)[:, None]
    mag = jnp.exp(a_re * step)
    ang = a_im * step
    lam_re, lam_im = mag * jnp.cos(ang), mag * jnp.sin(ang)
    num_re, num_im = lam_re - 1.0, lam_im
    den = a_re * a_re + a_im * a_im
    f_re = (num_re * a_re + num_im * a_im) / den
    f_im = (num_im * a_re - num_re * a_im) / den
    bb_re = f_re[..., None] * b_re - f_im[..., None] * b_im
    bb_im = f_re[..., None] * b_im + f_im[..., None] * b_re
    eye = jnp.eye(8, dtype=f32)

    def block_in(bb):
        t = bb.transpose(0, 2, 1).reshape(4, 8, 16, 1, 64)
        return (t * eye[None, :, None, :, None]).reshape(4, 128, 512)

    def block_out(c):
        t = c.transpose(0, 2, 1).reshape(4, 8, 64, 1, 16)
        return (t * eye[None, :, None, :, None]).reshape(4, 512, 128)

    return (lam_re.reshape(1, S5_STATES), lam_im.reshape(1, S5_STATES), block_in(bb_re), block_in(bb_im),
            block_out(c_re), block_out(c_im))


def _lam_powers(lam_re, lam_im):
    rows_re, rows_im = [lam_re], [lam_im]
    for _ in range(7):
        pr, pi = rows_re[-1], rows_im[-1]
        rows_re.append(pr * lam_re - pi * lam_im)
        rows_im.append(pr * lam_im + pi * lam_re)
    return jnp.concatenate(rows_re, 0), jnp.concatenate(rows_im, 0)


def s5_fwd(u, pw_re, pw_im, w_re, w_im, c_re, c_im, dvec, name):
    S = u.shape[0]
    R, NS = S5_ROWS, S5_STATES
    nb = R // 8

    def body(u_ref, pwr_ref, pwi_ref, wre_ref, wim_ref, cre_ref, cim_ref, d_ref, y_ref, hr_ref, hi_ref,
             car_re, car_im, cin_re, cin_im, up, yp):
        @pl.when(pl.program_id(0) == 0)
        def _():
            car_re[...] = jnp.zeros_like(car_re)
            car_im[...] = jnp.zeros_like(car_im)

        slab = lambda r: pl.ds(r * nb, nb)
        for r in range(8):
            up[slab(r), :] = u_ref[:, r, :]
        u = up[...]
        for j in range(4):
            uj = u[:, 128 * j:128 * (j + 1)]
            hr_ref[:, 512 * j:512 * (j + 1)] = _dot(uj, wre_ref[j], NN)
            hi_ref[:, 512 * j:512 * (j + 1)] = _dot(uj, wim_ref[j], NN)
        lr, li = pwr_ref[0:1, :], pwi_ref[0:1, :]
        for r in range(1, 8):
            pr, pi = hr_ref[slab(r - 1), :], hi_ref[slab(r - 1), :]
            hr_ref[slab(r), :] = lr * pr - li * pi + hr_ref[slab(r), :]
            hi_ref[slab(r), :] = lr * pi + li * pr + hi_ref[slab(r), :]
        l8r, l8i = pwr_ref[7:8, :], pwi_ref[7:8, :]

        def across(c, carry):
            gr, gi = carry
            cin_re[pl.ds(c, 1), :] = gr
            cin_im[pl.ds(c, 1), :] = gi
            er, ei = hr_ref[pl.ds(7 * nb + c, 1), :], hi_ref[pl.ds(7 * nb + c, 1), :]
            return l8r * gr - l8i * gi + er, l8r * gi + l8i * gr + ei

        gr, gi = lax.fori_loop(0, nb, across, (car_re[...], car_im[...]))
        car_re[...] = gr
        car_im[...] = gi
        cr, ci = cin_re[...], cin_im[...]
        for r in range(8):
            pr, pi = pwr_ref[r:r + 1, :], pwi_ref[r:r + 1, :]
            hr_ref[slab(r), :] = hr_ref[slab(r), :] + pr * cr - pi * ci
            hi_ref[slab(r), :] = hi_ref[slab(r), :] + pr * ci + pi * cr
        for j in range(4):
            sl = slice(512 * j, 512 * (j + 1))
            cs = slice(128 * j, 128 * (j + 1))
            yp[:, cs] = (_dot(hr_ref[:, sl], cre_ref[j], NN) - _dot(hi_ref[:, sl], cim_ref[j], NN)
                         + d_ref[:, cs] * u[:, cs])
        for r in range(8):
            y_ref[:, r, :] = yp[slab(r), :]

    full = lambda shape: pl.BlockSpec(shape, lambda i: (0,) * len(shape))
    hspec = pl.BlockSpec((R, NS), lambda i: (i, 0))
    uspec = pl.BlockSpec((nb, 8, 512), lambda i: (i, 0, 0))
    y, h_re, h_im = _call(
        body, name, (S // R,),
        [uspec, full((8, NS)), full((8, NS)), full((4, 128, 512)),
         full((4, 128, 512)), full((4, 512, 128)), full((4, 512, 128)), full((1, 512))],
        [uspec, hspec, hspec],
        [jax.ShapeDtypeStruct((S // 8, 8, 512), f32), jax.ShapeDtypeStruct((S, NS), f32),
         jax.ShapeDtypeStruct((S, NS), f32)],
        scratch=[pltpu.VMEM((1, NS), f32), pltpu.VMEM((1, NS), f32), pltpu.VMEM((nb, NS), f32),
                 pltpu.VMEM((nb, NS), f32), pltpu.VMEM((R, 512), f32), pltpu.VMEM((R, 512), f32)],
        sem=("arbitrary",))(u.reshape(S // 8, 8, 512), pw_re, pw_im, w_re.astype(bf16), w_im.astype(bf16),
                            c_re.astype(bf16), c_im.astype(bf16), dvec)
    return y.reshape(S, 512), h_re, h_im


def s5_bwd(dy, u, h_re, h_im, pw_re, pw_im, w_re, w_im, c_re, c_im, dvec, name):
    S = u.shape[0]
    R, NS = S5_ROWS, S5_STATES
    nb = R // 8
    nchunk = S // R

    def body(dy_ref, u_ref, hr_ref, hi_ref, hpr_ref, hpi_ref, pwr_ref, pwi_ref, wre_ref, wim_ref, cre_ref, cim_ref,
             d_ref, du_ref, dwre_ref, dwim_ref, dcre_ref, dcim_ref, dlr_ref, dli_ref, dd_ref,
             ar, ai, car_re, car_im, cin_re, cin_im, up, dyp, dup):
        i = pl.program_id(0)

        @pl.when(i == 0)
        def _():
            for ref in (car_re, car_im, dwre_ref, dwim_ref, dcre_ref, dcim_ref, dlr_ref, dli_ref, dd_ref):
                ref[...] = jnp.zeros_like(ref)

        slab = lambda r: pl.ds(r * nb, nb)
        for r in range(8):
            up[slab(r), :] = u_ref[:, r, :]
            dyp[slab(r), :] = dy_ref[:, r, :]
        dy = dyp[...]
        u = up[...]
        for j in range(4):
            dyj = dy[:, 128 * j:128 * (j + 1)]
            ar[:, 512 * j:512 * (j + 1)] = _dot(dyj, cre_ref[j], NT)
            ai[:, 512 * j:512 * (j + 1)] = -_dot(dyj, cim_ref[j], NT)
        lr, li = pwr_ref[0:1, :], pwi_ref[0:1, :]
        for r in range(6, -1, -1):
            nr, ni = ar[slab(r + 1), :], ai[slab(r + 1), :]
            ar[slab(r), :] = lr * nr + li * ni + ar[slab(r), :]
            ai[slab(r), :] = lr * ni - li * nr + ai[slab(r), :]
        l8r, l8i = pwr_ref[7:8, :], pwi_ref[7:8, :]

        def across(k, carry):
            c = nb - 1 - k
            gr, gi = carry
            cin_re[pl.ds(c, 1), :] = gr
            cin_im[pl.ds(c, 1), :] = gi
            er, ei = ar[pl.ds(c, 1), :], ai[pl.ds(c, 1), :]
            return l8r * gr + l8i * gi + er, l8r * gi - l8i * gr + ei

        gr, gi = lax.fori_loop(0, nb, across, (car_re[...], car_im[...]))
        car_re[...] = gr
        car_im[...] = gi
        cr, ci = cin_re[...], cin_im[...]
        for r in range(8):
            pr, pi = pwr_ref[7 - r:8 - r, :], pwi_ref[7 - r:8 - r, :]
            ar[slab(r), :] = ar[slab(r), :] + pr * cr + pi * ci
            ai[slab(r), :] = ai[slab(r), :] + pr * ci - pi * cr

        acc_r = jnp.zeros((1, NS), f32)
        acc_i = jnp.zeros((1, NS), f32)
        has_prev = (i < nchunk - 1).astype(f32)
        top = lax.broadcasted_iota(jnp.int32, (nb, NS), 0) == 0
        for r in range(8):
            if r == 0:
                xr = jnp.where(top, hpr_ref[7:8, :] * has_prev, pltpu.roll(hr_ref[slab(7), :], 1, 0))
                xi = jnp.where(top, hpi_ref[7:8, :] * has_prev, pltpu.roll(hi_ref[slab(7), :], 1, 0))
            else:
                xr, xi = hr_ref[slab(r - 1), :], hi_ref[slab(r - 1), :]
            br, bi = ar[slab(r), :], ai[slab(r), :]
            acc_r += jnp.sum(br * xr + bi * xi, axis=0, keepdims=True)
            acc_i += jnp.sum(bi * xr - br * xi, axis=0, keepdims=True)
        dlr_ref[...] += acc_r
        dli_ref[...] += acc_i
        dd_ref[...] += jnp.sum(dy * u, axis=0, keepdims=True)

        for j in range(4):
            sl = slice(512 * j, 512 * (j + 1))
            cs = slice(128 * j, 128 * (j + 1))
            arj, aij = ar[:, sl], ai[:, sl]
            uj, dyj = u[:, cs], dy[:, cs]
            dup[:, cs] = _dot(arj, wre_ref[j], NT) + _dot(aij, wim_ref[j], NT) + d_ref[:, cs] * dyj
            dwre_ref[j] += _dot(uj, arj, TN)
            dwim_ref[j] += _dot(uj, aij, TN)
            dcre_ref[j] += _dot(hr_ref[:, sl], dyj, TN)
            dcim_ref[j] -= _dot(hi_ref[:, sl], dyj, TN)
        for r in range(8):
            du_ref[:, r, :] = dup[slab(r), :]

    rev = lambda i: nchunk - 1 - i
    full = lambda shape: pl.BlockSpec(shape, lambda i: (0,) * len(shape))
    row = pl.BlockSpec((nb, 8, 512), lambda i: (rev(i), 0, 0))
    hspec = pl.BlockSpec((R, NS), lambda i: (rev(i), 0))
    hprev = pl.BlockSpec((8, NS), lambda i: (jnp.maximum(rev(i) * nb - 1, 0), 0))
    outs = _call(
        body, name, (nchunk,),
        [row, row, hspec, hspec, hprev, hprev, full((8, NS)), full((8, NS)), full((4, 128, 512)), full((4, 128, 512)),
         full((4, 512, 128)), full((4, 512, 128)), full((1, 512))],
        [row, full((4, 128, 512)), full((4, 128, 512)), full((4, 512, 128)), full((4, 512, 128)),
         full((1, NS)), full((1, NS)), full((1, 512))],
        [jax.ShapeDtypeStruct((S // 8, 8, 512), f32), jax.ShapeDtypeStruct((4, 128, 512), f32),
         jax.ShapeDtypeStruct((4, 128, 512), f32), jax.ShapeDtypeStruct((4, 512, 128), f32),
         jax.ShapeDtypeStruct((4, 512, 128), f32), jax.ShapeDtypeStruct((1, NS), f32),
         jax.ShapeDtypeStruct((1, NS), f32), jax.ShapeDtypeStruct((1, 512), f32)],
        scratch=[pltpu.VMEM((R, NS), f32), pltpu.VMEM((R, NS), f32), pltpu.VMEM((1, NS), f32),
                 pltpu.VMEM((1, NS), f32), pltpu.VMEM((nb, NS), f32), pltpu.VMEM((nb, NS), f32),
                 pltpu.VMEM((R, 512), f32), pltpu.VMEM((R, 512), f32), pltpu.VMEM((R, 512), f32)],
        sem=("arbitrary",))(dy.reshape(S // 8, 8, 512), u.reshape(S // 8, 8, 512), h_re, h_im, h_re, h_im, pw_re,
                            pw_im, w_re.astype(bf16), w_im.astype(bf16), c_re.astype(bf16), c_im.astype(bf16), dvec)
    return (outs[0].reshape(S, 512),) + tuple(outs[1:])


def _rows(start, n, d):
    return pl.ds(pl.multiple_of(start, ATT_BLOCK), n) if d == 1 else pl.ds(start, n, stride=d)


def _head_masks():
    lane = lax.broadcasted_iota(jnp.int32, (1, LANES), 1)
    return [(lane < 64).astype(f32), (lane >= 64).astype(f32)]


def _head_norm(x, w, hm):
    x2 = x * x
    r = [lax.rsqrt(jnp.sum(x2 * hm[h], axis=-1, keepdims=True) * (1.0 / 64) + RMS_EPS) for h in range(2)]
    sc = hm[0] * r[0] + hm[1] * r[1]
    return x * sc * w, sc, r


def _head_norm_bwd(x, w, sc, r, dxn, hm):
    dw = jnp.sum(dxn * x * sc, axis=0, keepdims=True)
    t = dxn * w
    tx = t * x
    corr = sum(hm[h] * (r[h] * r[h] * r[h]) * jnp.sum(tx * hm[h], axis=-1, keepdims=True) for h in range(2))
    return t * sc - x * corr * (1.0 / 64), dw


def _att_mask(has_prev):
    qi = lax.broadcasted_iota(jnp.int32, (ATT_BLOCK, 2 * ATT_BLOCK), 0) + ATT_BLOCK
    kj = lax.broadcasted_iota(jnp.int32, (ATT_BLOCK, 2 * ATT_BLOCK), 1)
    return (qi - kj >= 0) & (qi - kj <= ATT_BLOCK) & (has_prev | (kj >= ATT_BLOCK))


def _att_block_bwd(q, k, v, o, lse, do, dlse, qw, kw, has_prev):
    hm = _head_masks()
    mask = _att_mask(has_prev)
    qn, qsc, qr = _head_norm(q, qw, hm)
    kn, ksc, kr = _head_norm(k, kw, hm)
    dqn = jnp.zeros((ATT_BLOCK, LANES), f32)
    dkn = jnp.zeros((2 * ATT_BLOCK, LANES), f32)
    dv = jnp.zeros((2 * ATT_BLOCK, LANES), f32)
    for h in range(2):
        qh, do_h = qn * hm[h], do * hm[h]
        s = _dot(qh, kn, NT) * 0.125
        p = jnp.exp(jnp.where(mask, s - lse[:, 64 * h:64 * h + 1], -jnp.inf))
        dp = _dot(do_h, v, NT)
        delta = jnp.sum(do_h * o, axis=-1, keepdims=True)
        dl = jnp.sum(dlse * hm[h], axis=-1, keepdims=True)
        ds = p * (dp - delta + dl) * 0.125
        dqn = dqn + hm[h] * _dot(ds, kn, NN)
        dkn = dkn + _dot(ds, qh, TN)
        dv = dv + _dot(p, do_h, TN)
    dq, dqw = _head_norm_bwd(q, qw, qsc, qr, dqn, hm)
    dk, dkw = _head_norm_bwd(k, kw, ksc, kr, dkn, hm)
    return dq, dk, dv, dqw, dkw


def _att_block(q, k, v, qw, kw, has_prev):
    hm = _head_masks()
    qn, kn = _head_norm(q, qw, hm)[0], _head_norm(k, kw, hm)[0]
    mask = _att_mask(has_prev)
    o = jnp.zeros((ATT_BLOCK, LANES), f32)
    lse = jnp.zeros((ATT_BLOCK, LANES), f32)
    for h in range(2):
        s = _bdot(qn * hm[h], kn, NT) * 0.125
        s = jnp.where(mask, s, -jnp.inf)
        m = jnp.max(s, axis=-1, keepdims=True)
        p = jnp.exp(s - m)
        l = jnp.sum(p, axis=-1, keepdims=True)
        o = o + hm[h] * _bdot(p / l, v, NN)
        lse = lse + hm[h] * (m + jnp.log(l))
    return o, lse


def att_fwd(p_att, qw, kw, d, g, name):
    S = p_att.shape[0]
    SEG = ATT_SEG
    nblk = SEG // ATT_BLOCK

    def body(p_ref, qw_ref, kw_ref, o_ref, l_ref, q_s, k_ext, v_ext, o_s, l_s):
        seg = pl.program_id(1)

        @pl.when(seg == 0)
        def _():
            k_ext[SEG:, :] = jnp.zeros((SEG, LANES), f32)
            v_ext[SEG:, :] = jnp.zeros((SEG, LANES), f32)

        k_ext[:SEG, :] = k_ext[SEG:, :]
        v_ext[:SEG, :] = v_ext[SEG:, :]
        q_s[...] = p_ref[:, 0:128]
        k_ext[SEG:, :] = p_ref[:, 128:256]
        v_ext[SEG:, :] = p_ref[:, 256:384]
        qw_v, kw_v = qw_ref[...], kw_ref[...]

        def blk(b, carry):
            j, r = b // d, b % d
            qs = j * (ATT_BLOCK * d) + r
            ks = SEG + qs - ATT_BLOCK * d
            o, lse = _att_block(q_s[_rows(qs, ATT_BLOCK, d), :], k_ext[_rows(ks, 2 * ATT_BLOCK, d), :],
                                v_ext[_rows(ks, 2 * ATT_BLOCK, d), :], qw_v, kw_v, (seg > 0) | (j > 0))
            o_s[_rows(qs, ATT_BLOCK, d), :] = o
            l_s[_rows(qs, ATT_BLOCK, d), :] = lse
            return carry

        lax.fori_loop(0, nblk, blk, 0, unroll=4)
        o_ref[...] = o_s[...]
        l_ref[...] = l_s[...]

    vec = pl.BlockSpec((1, LANES), lambda hh, s: (0, 0))
    out = pl.BlockSpec((SEG, LANES), lambda hh, s: (s, hh))
    return _call(body, name, (2, S // SEG), [pl.BlockSpec((SEG, 384), lambda hh, s: (s, MAIN_ATT_BLOCK + 2 * g + hh)), vec, vec],
                 [out, out], [jax.ShapeDtypeStruct((S, 256), f32), jax.ShapeDtypeStruct((S, 256), f32)],
                 scratch=[pltpu.VMEM((SEG, LANES), f32), pltpu.VMEM((2 * SEG, LANES), f32),
                          pltpu.VMEM((2 * SEG, LANES), f32), pltpu.VMEM((SEG, LANES), f32),
                          pltpu.VMEM((SEG, LANES), f32)],
                 sem=("arbitrary", "arbitrary"))(p_att, qw, kw)


def att_bwd(p_att, o, lse, do, dlse, qw, kw, d, g, dp_main, name):
    S = p_att.shape[0]
    SEG = ATT_SEG
    nseg = S // SEG
    nblk = SEG // ATT_BLOCK

    def body(p_ref, pp_ref, o_ref, l_ref, do_ref, dl_ref, qw_ref, kw_ref, _, dp_ref, dqw_ref, dkw_ref,
             q_s, k_ext, v_ext, dq_s, dk_ext, dv_ext):
        hh, i = pl.program_id(0), pl.program_id(1)
        seg = nseg - 1 - i

        @pl.when(i == 0)
        def _():
            dk_ext[...] = jnp.zeros_like(dk_ext)
            dv_ext[...] = jnp.zeros_like(dv_ext)

        @pl.when((i == 0) & (hh == 0))
        def _():
            dqw_ref[...] = jnp.zeros_like(dqw_ref)
            dkw_ref[...] = jnp.zeros_like(dkw_ref)

        dk_ext[SEG:, :] = dk_ext[:SEG, :]
        dv_ext[SEG:, :] = dv_ext[:SEG, :]
        dk_ext[:SEG, :] = jnp.zeros((SEG, LANES), f32)
        dv_ext[:SEG, :] = jnp.zeros((SEG, LANES), f32)
        q_s[...] = p_ref[:, 0:128]
        k_ext[SEG:, :] = p_ref[:, 128:256]
        v_ext[SEG:, :] = p_ref[:, 256:384]
        k_ext[:SEG, :] = pp_ref[:, 128:256]
        v_ext[:SEG, :] = pp_ref[:, 256:384]
        qw_v, kw_v = qw_ref[...], kw_ref[...]

        def blk_pair(i2, carry):
            dqw, dkw = carry
            done = []
            for u in range(2):
                b = 2 * i2 + u
                j, r = b // d, b % d
                qs = j * (ATT_BLOCK * d) + r
                ks = SEG + qs - ATT_BLOCK * d
                has_prev = (seg > 0) | (j > 0)
                qrows, krows = _rows(qs, ATT_BLOCK, d), _rows(ks, 2 * ATT_BLOCK, d)
                dq, dk, dv, dqw_b, dkw_b = _att_block_bwd(
                    q_s[qrows, :], k_ext[krows, :], v_ext[krows, :], o_ref[qrows, :], l_ref[qrows, :],
                    do_ref[qrows, :], dl_ref[qrows, :], qw_v, kw_v, has_prev)
                dqw, dkw = dqw + dqw_b, dkw + dkw_b
                done.append((qrows, krows, dq, dk, dv))
            for qrows, krows, dq, dk, dv in done:
                dq_s[qrows, :] = dq
                dk_ext[krows, :] = dk_ext[krows, :] + dk
                dv_ext[krows, :] = dv_ext[krows, :] + dv
            return dqw, dkw

        zero = jnp.zeros((1, LANES), f32)
        dqw, dkw = lax.fori_loop(0, nblk // 2, blk_pair, (zero, zero))
        dqw_ref[...] += dqw
        dkw_ref[...] += dkw
        dp_ref[:, 0:128] = dq_s[...].astype(bf16)
        dp_ref[:, 128:256] = dk_ext[SEG:, :].astype(bf16)
        dp_ref[:, 256:384] = dv_ext[SEG:, :].astype(bf16)

    rev = lambda i: nseg - 1 - i
    vec = pl.BlockSpec((1, LANES), lambda hh, i: (0, 0))
    blk = MAIN_ATT_BLOCK + 2 * g
    cur = pl.BlockSpec((SEG, 384), lambda hh, i: (rev(i), blk + hh))
    prev = pl.BlockSpec((SEG, 384), lambda hh, i: (jnp.maximum(rev(i) - 1, 0), blk + hh))
    col = pl.BlockSpec((SEG, LANES), lambda hh, i: (rev(i), hh))
    big = pltpu.VMEM((2 * SEG, LANES), f32)
    one = pltpu.VMEM((SEG, LANES), f32)
    return _call(body, name, (2, nseg), [cur, prev, col, col, col, col, vec, vec, _ANY], [cur, vec, vec],
                 [jax.ShapeDtypeStruct((S, MAIN_WIDTH), bf16), jax.ShapeDtypeStruct((1, LANES), f32),
                  jax.ShapeDtypeStruct((1, LANES), f32)],
                 scratch=[one, big, big, one, big, big], sem=("arbitrary", "arbitrary"),
                 aliases={8: 0})(p_att, p_att, o, lse, do, dlse, qw, kw, dp_main)


def conv_fwd(p_ssd, conv_w, conv_b, name):
    S = p_ssd.shape[0]
    tm, C = CONV_ROWS, SSD_XBC

    def body(x_ref, xp_ref, w_ref, b_ref, o_ref):
        first = (pl.program_id(0) == 0)
        ext = jnp.concatenate([jnp.where(first, 0.0, xp_ref[:, 0:C]), x_ref[:, 0:C]], axis=0)
        acc = b_ref[...] + w_ref[3:4, :] * ext[8:, :]
        for k in range(1, 4):
            acc = acc + w_ref[3 - k:4 - k, :] * pltpu.roll(ext, k, 0)[8:, :]
        o_ref[...] = jax.nn.silu(acc)

    return _call(body, name, (S // tm,),
                 [pl.BlockSpec((tm, 1536), lambda i: (i, MAIN_SSD_BLOCK)),
                  pl.BlockSpec((8, 1536), lambda i: (jnp.maximum(i * (tm // 8) - 1, 0), MAIN_SSD_BLOCK)),
                  pl.BlockSpec((4, C), lambda i: (0, 0)), pl.BlockSpec((1, C), lambda i: (0, 0))],
                 pl.BlockSpec((tm, C), lambda i: (i, 0)), jax.ShapeDtypeStruct((S, C), f32),
                 sem=("parallel",))(p_ssd, p_ssd, conv_w, conv_b)


def conv_bwd(p_ssd, dact, ddt, conv_w, conv_b, dp_main, name):
    S = p_ssd.shape[0]
    tm, C = CONV_ROWS, SSD_XBC
    nblk = S // tm

    def body(x_ref, xp_ref, xn_ref, da_ref, dan_ref, ddt_ref, w_ref, b_ref, _, dp_ref, dw_ref, db_ref):
        i = pl.program_id(0)
        rows = tm + 8
        ext = jnp.concatenate([jnp.where(i == 0, 0.0, xp_ref[:, 0:C]), x_ref[:, 0:C], xn_ref[:, 0:C]], axis=0)
        shifted = [ext[8:, :]] + [pltpu.roll(ext, k, 0)[8:, :] for k in range(1, 4)]
        pre = b_ref[...] + w_ref[3:4, :] * shifted[0]
        for k in range(1, 4):
            pre = pre + w_ref[3 - k:4 - k, :] * shifted[k]
        sg = jax.nn.sigmoid(pre)
        dact = jnp.concatenate([da_ref[...], jnp.where(i == nblk - 1, 0.0, dan_ref[...])], axis=0)
        dpre = dact * (sg * (1.0 + pre * (1.0 - sg)))
        dx = w_ref[3:4, :] * dpre[0:tm, :]
        for k in range(1, 4):
            dx = dx + w_ref[3 - k:4 - k, :] * pltpu.roll(dpre, rows - k, 0)[0:tm, :]
        dp_ref[:, 0:C] = dx.astype(bf16)
        dp_ref[:, C:C + 128] = ddt_ref[...].astype(bf16)
        dp_ref[:, C + 128:] = jnp.zeros((tm, 128), bf16)
        dcur = dpre[0:tm, :]
        dws = [jnp.sum(dcur * shifted[3 - j][0:tm, :], axis=0, keepdims=True) for j in range(4)]
        dbs = jnp.sum(dcur, axis=0, keepdims=True)

        @pl.when(i == 0)
        def _():
            dw_ref[...] = jnp.zeros_like(dw_ref)
            db_ref[...] = jnp.zeros_like(db_ref)

        for j in range(4):
            dw_ref[j:j + 1, :] += dws[j]
        db_ref[...] += dbs

    t8 = tm // 8
    blk = MAIN_SSD_BLOCK
    return _call(body, name, (nblk,),
                 [pl.BlockSpec((tm, 1536), lambda i: (i, blk)),
                  pl.BlockSpec((8, 1536), lambda i: (jnp.maximum(i * t8 - 1, 0), blk)),
                  pl.BlockSpec((8, 1536), lambda i: (jnp.minimum((i + 1) * t8, S // 8 - 1), blk)),
                  pl.BlockSpec((tm, C), lambda i: (i, 0)),
                  pl.BlockSpec((8, C), lambda i: (jnp.minimum((i + 1) * t8, S // 8 - 1), 0)),
                  pl.BlockSpec((tm, 128), lambda i: (i, 0)),
                  pl.BlockSpec((4, C), lambda i: (0, 0)), pl.BlockSpec((1, C), lambda i: (0, 0)), _ANY],
                 [pl.BlockSpec((tm, 1536), lambda i: (i, blk)), pl.BlockSpec((4, C), lambda i: (0, 0)),
                  pl.BlockSpec((1, C), lambda i: (0, 0))],
                 [jax.ShapeDtypeStruct((S, MAIN_WIDTH), bf16), jax.ShapeDtypeStruct((4, C), f32),
                  jax.ShapeDtypeStruct((1, C), f32)],
                 sem=("arbitrary",), aliases={8: 0})(p_ssd, p_ssd, p_ssd, dact, dact, ddt, conv_w, conv_b, dp_main)


def _ssd_chunk(xbc, dtr, state, dt_bias, a_log, d_full):
    T = SSD_CHUNK
    r_i = lax.broadcasted_iota(jnp.int32, (T, T), 0)
    c_i = lax.broadcasted_iota(jnp.int32, (T, T), 1)
    tril = c_i <= r_i
    tri = tril.astype(bf16)
    lane = lax.broadcasted_iota(jnp.int32, (1, LANES), 1)
    hm = [(lane < 64).astype(f32), (lane >= 64).astype(f32)]
    column = lambda v, h: jnp.broadcast_to(v[:, h:h + 1], (T, LANES))

    def per_head_lanes(v):
        return jnp.concatenate([jnp.where(lane < 64, column(v, 2 * pp), column(v, 2 * pp + 1)) for pp in range(6)],
                               axis=1)

    xs, bm, cm = xbc[:, :768], xbc[:, 768:1024], xbc[:, 1024:1280]
    dt = _softplus(dtr + dt_bias)
    a_dt = dt * (-jnp.exp(a_log))
    a_cs = _xdot_l(tri, a_dt)
    dt_full = per_head_lanes(dt)
    acs_full = per_head_lanes(a_cs)
    last = lax.broadcasted_iota(jnp.int32, (T, SSD_WIDTH), 0) == T - 1
    tot_full = jnp.sum(jnp.where(last, acs_full, 0.0), axis=0, keepdims=True)
    xdt = xs * dt_full
    xw = xdt * jnp.exp(tot_full - acs_full)
    eacs = jnp.exp(acs_full)
    st_parts, off_parts, diag_parts = [], [], []
    for g in range(2):
        bg, cg = bm[:, 128 * g:128 * (g + 1)], cm[:, 128 * g:128 * (g + 1)]
        cols = slice(384 * g, 384 * (g + 1))
        st_parts.append(_bdot(bg, xw[:, cols], TN))
        off_parts.append(_bdot(cg, state[:, cols], NN))
        cb = _bdot(cg, bg, NT)
        for pp in range(3 * g, 3 * g + 3):
            xp = xdt[:, 128 * pp:128 * (pp + 1)]
            acc = jnp.zeros((T, LANES), f32)
            for hh in range(2):
                a_col = column(a_cs, 2 * pp + hh)
                decay = jnp.where(tril, jnp.exp(jnp.minimum(a_col - a_col.T, 0.0)), 0.0)
                acc = acc + _bdot(cb * decay, xp * hm[hh], NN)
            diag_parts.append(acc)
    new_state = state * jnp.exp(tot_full) + jnp.concatenate(st_parts, axis=1)
    y = jnp.concatenate(diag_parts, axis=1) + jnp.concatenate(off_parts, axis=1) * eacs + xs * d_full
    return y, new_state


def ssd_fwd(xact, p_ssd, dt_bias, a_log, d_full, name):
    S = xact.shape[0]
    T = SSD_CHUNK

    U = SSD_CHUNKS_PER_STEP

    def body(x_ref, p_ref, b_ref, a_ref, d_ref, y_ref, s_ref, state):
        @pl.when(pl.program_id(0) == 0)
        def _():
            state[...] = jnp.zeros_like(state)

        st = state[...]
        for u in range(U):
            rows = slice(T * u, T * (u + 1))
            s_ref[u] = st
            y, st = _ssd_chunk(x_ref[rows, :], p_ref[rows, :], st, b_ref[...], a_ref[...], d_ref[...])
            y_ref[rows, :] = y
        state[...] = st

    vec = lambda n: pl.BlockSpec((1, n), lambda i: (0, 0))
    return _call(body, name, (S // (U * T),),
                 [pl.BlockSpec((U * T, SSD_XBC), lambda i: (i, 0)),
                  pl.BlockSpec((U * T, 128), lambda i: (i, MAIN_DT_BLOCK)), vec(128), vec(128), vec(768)],
                 [pl.BlockSpec((U * T, 768), lambda i: (i, 0)), pl.BlockSpec((U, T, 768), lambda i: (i, 0, 0))],
                 [jax.ShapeDtypeStruct((S, 768), f32), jax.ShapeDtypeStruct((S // T, T, 768), f32)],
                 scratch=[pltpu.VMEM((T, 768), f32)], sem=("arbitrary",))(xact, p_ssd, dt_bias, a_log, d_full)


def ssd_bwd(xact, p_ssd, states, dy, dt_bias, a_log, d_full, name):
    S = xact.shape[0]
    T = SSD_CHUNK
    U = 1
    nc = S // (U * T)

    def body(x_ref, p_ref, s_ref, dy_ref, b_ref, a_ref, d_ref, dx_ref, ddt_ref, db_ref, da_ref, dd_ref, dstate):
        i = pl.program_id(0)

        @pl.when(i == 0)
        def _():
            for ref in (dstate, db_ref, da_ref, dd_ref):
                ref[...] = jnp.zeros_like(ref)

        dst = dstate[...]
        for u in reversed(range(U)):
            rows = slice(T * u, T * (u + 1))
            _, vjp = jax.vjp(_ssd_chunk, x_ref[rows, :], p_ref[rows, :], s_ref[u], b_ref[...], a_ref[...], d_ref[...])
            dx, ddt, dst, db, da, dd = vjp((dy_ref[rows, :], dst))
            dx_ref[rows, :] = dx
            ddt_ref[rows, :] = ddt
            db_ref[...] += db
            da_ref[...] += da
            dd_ref[...] += dd
        dstate[...] = dst

    rev = lambda i: nc - 1 - i
    vec = lambda n: pl.BlockSpec((1, n), lambda i: (0, 0))
    return _call(body, name, (nc,),
                 [pl.BlockSpec((U * T, SSD_XBC), lambda i: (rev(i), 0)),
                  pl.BlockSpec((U * T, 128), lambda i: (rev(i), MAIN_DT_BLOCK)),
                  pl.BlockSpec((U, T, 768), lambda i: (rev(i), 0, 0)), pl.BlockSpec((U * T, 768), lambda i: (rev(i), 0)),
                  vec(128), vec(128), vec(768)],
                 [pl.BlockSpec((U * T, SSD_XBC), lambda i: (rev(i), 0)), pl.BlockSpec((U * T, 128), lambda i: (rev(i), 0)),
                  vec(128), vec(128), vec(768)],
                 [jax.ShapeDtypeStruct((S, SSD_XBC), f32), jax.ShapeDtypeStruct((S, 128), f32),
                  jax.ShapeDtypeStruct((1, 128), f32), jax.ShapeDtypeStruct((1, 128), f32),
                  jax.ShapeDtypeStruct((1, 768), f32)],
                 scratch=[pltpu.VMEM((T, 768), f32)],
                 sem=("arbitrary",))(xact, p_ssd, states, dy, dt_bias, a_log, d_full)


def _tail_fn(ys5, pt, o0, o1, o2, l0, l1, l2, yssd, glu_b, nw, pr_glu, pr_a, pr_b, pr_c, x, weights):
    glu_w, pa, pb, pc, wo = weights
    gates = jax.nn.sigmoid(pt[:, :3072])
    za, zb, zc = pt[:, 3072:3584], pt[:, 3584:3840], pt[:, 3840:4608]
    g = jax.nn.gelu(ys5)
    ya = g * jax.nn.sigmoid(_cdot(g, glu_w, NN) + glu_b + pr_glu) * jax.nn.silu(za)
    m = jnp.maximum(jnp.maximum(l0, l1), l2)
    e0, e1, e2 = jnp.exp(l0 - m), jnp.exp(l1 - m), jnp.exp(l2 - m)
    yb = (e0 * o0 + e1 * o1 + e2 * o2) / (e0 + e1 + e2) * jax.nn.silu(zb)
    yc = _rms(yssd * jax.nn.silu(zc), nw)
    merged = (gates[:, :1024] * (_cdot(ya, pa, NN) + pr_a) + gates[:, 1024:2048] * (_cdot(yb, pb, NN) + pr_b)
              + gates[:, 2048:] * (_cdot(yc, pc, NN) + pr_c))
    out = x + _cdot(merged, wo, NN)
    return out, (g, ya, yb, yc, merged)


def _tail_specs(tm):
    row = lambda n: pl.BlockSpec((tm, n), lambda i: (i, 0))
    full = lambda a, b: pl.BlockSpec((a, b), lambda i: (0, 0))
    acts = [row(512), row(4608)] + [row(256)] * 6 + [row(768), row(D_MODEL)]
    consts = [full(1, 512), full(1, 768), full(512, 512), full(512, D_MODEL), full(256, D_MODEL),
              full(768, D_MODEL), full(D_MODEL, D_MODEL)]
    return row, full, acts, consts


def tail_fwd(ys5, pt, os_, ls_, yssd, x, glu_b, nw, weights, name):
    S = x.shape[0]
    tm = TAIL_ROWS
    row, full, acts, consts = _tail_specs(tm)

    def body(ys5_ref, pt_ref, o0, o1, o2, l0, l1, l2, yssd_ref, x_ref, gb_ref, nw_ref, gw, pa, pb, pc, wo, out_ref):
        z = lambda n: jnp.zeros((tm, n), f32)
        out, _ = _tail_fn(ys5_ref[...], pt_ref[...], o0[...], o1[...], o2[...], l0[...], l1[...], l2[...],
                          yssd_ref[...], gb_ref[...], nw_ref[...], z(512), z(D_MODEL), z(D_MODEL), z(D_MODEL),
                          x_ref[...], (gw[...], pa[...], pb[...], pc[...], wo[...]))
        out_ref[...] = out

    return _call(body, name, (S // tm,), acts + consts, row(D_MODEL), jax.ShapeDtypeStruct((S, D_MODEL), f32),
                 sem=("parallel",))(ys5, pt, *os_, *ls_, yssd, x, glu_b, nw, *weights)


def tail_bwd(ys5, pt, os_, ls_, yssd, dout, glu_b, nw, weights, name):
    S = dout.shape[0]
    tm = TAIL_ROWS
    row, full, acts, consts = _tail_specs(tm)

    def body(ys5_ref, pt_ref, o0, o1, o2, l0, l1, l2, yssd_ref, dout_ref, gb_ref, nw_ref, gw, pa, pb, pc, wo,
             dys5_ref, dpt_ref, do0, do1, do2, dl0, dl1, dl2, dyssd_ref, dgb_ref, dnw_ref,
             g_ref, ya_ref, yb_ref, yc_ref, mg_ref, dglu_ref, dpa_ref, dpb_ref, dpc_ref):
        z = lambda n: jnp.zeros((tm, n), f32)
        w = (gw[...], pa[...], pb[...], pc[...], wo[...])
        fn = lambda *a: _tail_fn(*a, z(D_MODEL), w)
        _, vjp, aux = jax.vjp(fn, ys5_ref[...], pt_ref[...], o0[...], o1[...], o2[...], l0[...], l1[...], l2[...],
                              yssd_ref[...], gb_ref[...], nw_ref[...], z(512), z(D_MODEL), z(D_MODEL), z(D_MODEL),
                              has_aux=True)
        (dys5, dpt, d0, d1, d2, e0, e1, e2, dyssd, dgb, dnw, dglu, dpa, dpb, dpc) = vjp(dout_ref[...])
        dys5_ref[...] = dys5
        dpt_ref[...] = dpt.astype(bf16)
        for ref, val in ((do0, d0), (do1, d1), (do2, d2), (dl0, e0), (dl1, e1), (dl2, e2)):
            ref[...] = val
        dyssd_ref[...] = dyssd
        g, ya, yb, yc, merged = aux
        for ref, val in ((g_ref, g), (ya_ref, ya), (yb_ref, yb), (yc_ref, yc), (mg_ref, merged),
                         (dglu_ref, dglu), (dpa_ref, dpa), (dpb_ref, dpb), (dpc_ref, dpc)):
            ref[...] = val.astype(bf16)

        @pl.when(pl.program_id(0) == 0)
        def _():
            dgb_ref[...] = dgb
            dnw_ref[...] = dnw

        @pl.when(pl.program_id(0) > 0)
        def _():
            dgb_ref[...] += dgb
            dnw_ref[...] += dnw

    sd = lambda n, dt=f32: jax.ShapeDtypeStruct((S, n), dt)
    out_specs = ([row(512), row(4608)] + [row(256)] * 6 + [row(768), full(1, 512), full(1, 768)]
                 + [row(512), row(512), row(256), row(768), row(D_MODEL), row(512)] + [row(D_MODEL)] * 3)
    out_shape = ([sd(512), sd(MAIN_WIDTH, bf16)] + [sd(256)] * 6 + [sd(768), jax.ShapeDtypeStruct((1, 512), f32),
                                                          jax.ShapeDtypeStruct((1, 768), f32)]
                 + [sd(512, bf16), sd(512, bf16), sd(256, bf16), sd(768, bf16), sd(D_MODEL, bf16), sd(512, bf16)]
                 + [sd(D_MODEL, bf16)] * 3)
    return _call(body, name, (S // tm,), acts + consts, out_specs, out_shape,
                 sem=("arbitrary",))(ys5, pt, *os_, *ls_, yssd, dout, glu_b, nw, *weights)


def _in_proj_segments(shards):
    dtype = shards[0].dtype

    def c(a, b):
        parts = []
        for k, sh in enumerate(shards):
            lo, hi = max(a, W_IN_SHARD * k), min(b, W_IN_SHARD * (k + 1))
            if lo < hi:
                parts.append(sh[:, lo - W_IN_SHARD * k:hi - W_IN_SHARD * k])
        return parts[0] if len(parts) == 1 else jnp.concatenate(parts, axis=1)

    atts = []
    for g in range(3):
        parts = []
        for hh in range(2):
            o = 64 * (4 * g + 2 * hh)
            parts += [c(_C_Q + o, _C_Q + o + 128), c(_C_K + o, _C_K + o + 128), c(_C_V + o, _C_V + o + 128)]
        atts.append(jnp.concatenate(parts, axis=1))
    ssd = jnp.concatenate([c(_C_XBC, _C_ZC), jnp.zeros((D_MODEL, 1536 - (_C_ZC - _C_XBC)), dtype)], axis=1)
    tail = jnp.concatenate([c(_C_GATE, _C_END), c(_C_ZA, _C_Q), c(_C_ZB, _C_XBC), c(_C_ZC, _C_GATE)], axis=1)
    return [c(_C_UA, _C_ZA), jnp.concatenate([tail, ssd] + atts, axis=1)]


def _in_proj_grad(ds5, dmain):
    dtail, dssd = dmain[:, :4608], dmain[:, 4608:6144]
    datts = [dmain[:, 6144 + 768 * g:6144 + 768 * (g + 1)] for g in range(3)]
    pick = lambda off: [datts[g][:, 384 * hh + off:384 * hh + off + 128] for g in range(3) for hh in range(2)]
    pieces = ([ds5, dtail[:, 3072:3584]] + pick(0) + pick(128) + pick(256)
              + [dtail[:, 3584:3840], dssd[:, :_C_ZC - _C_XBC], dtail[:, 3840:4608], dtail[:, :3072]])
    shards, start = [[] for _ in range(4)], 0
    for piece in pieces:
        width = piece.shape[1]
        for k in range(4):
            lo, hi = max(start, W_IN_SHARD * k), min(start + width, W_IN_SHARD * (k + 1))
            if lo < hi:
                shards[k].append(piece[:, lo - start:hi - start])
        start += width
    return jnp.stack([jnp.concatenate(s, axis=1) for s in shards])


def _prep_layer(p):
    q = {}
    q["segs"] = [s.astype(bf16) for s in _in_proj_segments(p["w_in"])]
    disc = _s5_discretize(p["s5_a_re"], p["s5_a_im"], p["s5_log_step"], p["s5_b_re"], p["s5_b_im"],
                          p["s5_c_re"], p["s5_c_im"])
    q["s5"] = disc
    q["pw"] = _lam_powers(disc[0], disc[1])
    q["s5_d"] = p["s5_d"].reshape(1, 512)
    q["qw"] = jnp.tile(p["q_norm_w"], 2).reshape(1, LANES)
    q["kw"] = jnp.tile(p["k_norm_w"], 2).reshape(1, LANES)
    q["conv_w"] = p["conv_w"]
    q["conv_b"] = p["conv_b"].reshape(1, SSD_XBC)
    pad = lambda v: jnp.pad(v, (0, LANES - v.shape[0])).reshape(1, LANES)
    q["dt_bias"], q["a_log"] = pad(p["dt_bias"]), pad(p["ssd_a_log"])
    q["d_full"] = jnp.repeat(p["ssd_d"], 64).reshape(1, SSD_WIDTH)
    q["glu_b"] = p["s5_glu_b"].reshape(1, 512)
    q["nw"] = p["ssd_norm_w"].reshape(1, SSD_WIDTH)
    q["norm_w"] = p["norm_w"].reshape(1, D_MODEL)
    q["tailw"] = tuple(p[n].astype(bf16) for n in ("s5_glu_w", "proj_a", "proj_b", "proj_c", "w_out"))
    return q


_DILATIONS = (1, 4, 16)


def layer_fwd(x, q, tag):
    h = rms_fwd(x, q["norm_w"], f"rms_fwd{tag}")
    p_s5, p_main = [mm_nn(h, w, f"inproj{k}{tag}") for k, w in enumerate(q["segs"])]
    _, _, w_re, w_im, c_re, c_im = q["s5"]
    ys5, h_re, h_im = s5_fwd(p_s5, *q["pw"], w_re, w_im, c_re, c_im, q["s5_d"], f"s5_fwd{tag}")
    os_, ls_ = [], []
    for g, d in enumerate(_DILATIONS):
        o, l = att_fwd(p_main, q["qw"], q["kw"], d, g, f"att_fwd{g}{tag}")
        os_.append(o)
        ls_.append(l)
    xact = conv_fwd(p_main, q["conv_w"], q["conv_b"], f"conv_fwd{tag}")
    yssd, states = ssd_fwd(xact, p_main, q["dt_bias"], q["a_log"], q["d_full"], f"ssd_fwd{tag}")
    out = tail_fwd(ys5, p_main, os_, ls_, yssd, x, q["glu_b"], q["nw"], q["tailw"], f"tail_fwd{tag}")
    saved = dict(x=x, h=h, p_s5=p_s5, p_main=p_main, ys5=ys5, h_re=h_re, h_im=h_im,
                 os=os_, ls=ls_, xact=xact, yssd=yssd, states=states)
    return out, saved


def layer_bwd(dout, sv, q, p, tag):
    S = dout.shape[0]
    (dys5, dp_main, do0, do1, do2, dl0, dl1, dl2, dyssd, dglu_b, dnw, g_b, ya_b, yb_b, yc_b, mg_b, dglu_b16,
     dpa_b, dpb_b, dpc_b) = tail_bwd(sv["ys5"], sv["p_main"], sv["os"], sv["ls"], sv["yssd"], dout, q["glu_b"],
                                     q["nw"], q["tailw"], f"tail_bwd{tag}")
    grads = {}
    grads["s5_glu_w"] = mm_tn(g_b, dglu_b16, f"dglu_w{tag}")
    grads["proj_a"] = mm_tn(ya_b, dpa_b, f"dproj_a{tag}")
    grads["proj_b"] = mm_tn(yb_b, dpb_b, f"dproj_b{tag}")
    grads["proj_c"] = mm_tn(yc_b, dpc_b, f"dproj_c{tag}")
    grads["w_out"] = mm_tn(mg_b, dout, f"dw_out{tag}")
    grads["s5_glu_b"] = dglu_b.reshape(512)
    grads["ssd_norm_w"] = dnw.reshape(SSD_WIDTH)

    dxact, ddt, ddt_bias, da_log, dd_full = ssd_bwd(sv["xact"], sv["p_main"], sv["states"], dyssd, q["dt_bias"],
                                                    q["a_log"], q["d_full"], f"ssd_bwd{tag}")
    dp_main, dconv_w, dconv_b = conv_bwd(sv["p_main"], dxact, ddt, q["conv_w"], q["conv_b"], dp_main,
                                         f"conv_bwd{tag}")
    grads["dt_bias"] = ddt_bias[0, :12]
    grads["ssd_a_log"] = da_log[0, :12]
    grads["ssd_d"] = dd_full.reshape(12, 64).sum(axis=1)
    grads["conv_w"] = dconv_w
    grads["conv_b"] = dconv_b.reshape(SSD_XBC)

    dqw, dkw = 0.0, 0.0
    for g, d in enumerate(_DILATIONS):
        dp_main, a, b = att_bwd(sv["p_main"], sv["os"][g], sv["ls"][g], (do0, do1, do2)[g], (dl0, dl1, dl2)[g],
                                q["qw"], q["kw"], d, g, dp_main, f"att_bwd{g}{tag}")
        dqw, dkw = dqw + a, dkw + b
    grads["q_norm_w"] = dqw.reshape(2, 64).sum(axis=0)
    grads["k_norm_w"] = dkw.reshape(2, 64).sum(axis=0)

    _, _, w_re, w_im, c_re, c_im = q["s5"]
    dp_s5, dwre, dwim, dcre, dcim, dlam_re, dlam_im, dd = s5_bwd(
        dys5, sv["p_s5"], sv["h_re"], sv["h_im"], *q["pw"], w_re, w_im, c_re, c_im, q["s5_d"], f"s5_bwd{tag}")
    s5_names = ("s5_a_re", "s5_a_im", "s5_log_step", "s5_b_re", "s5_b_im", "s5_c_re", "s5_c_im")
    _, disc_vjp = jax.vjp(_s5_discretize, *[p[n] for n in s5_names])
    for n, gr in zip(s5_names, disc_vjp((dlam_re, dlam_im, dwre, dwim, dcre, dcim))):
        grads[n] = gr
    grads["s5_d"] = dd.reshape(512)

    dsegs = [dp_s5, dp_main]
    dws = [mm_tn(sv["h"], ds, f"dw_in{k}{tag}") for k, ds in enumerate(dsegs)]
    grads["w_in"] = _in_proj_grad(*dws)
    dh = None
    for k, (ds, w) in enumerate(zip(dsegs, q["segs"])):
        dh = mm_nt(ds, w, f"dh{k}{tag}", acc=dh)
    dx, dnorm_w = rms_bwd(sv["x"], q["norm_w"], dh, dout, f"rms_bwd{tag}")
    grads["norm_w"] = dnorm_w.reshape(D_MODEL)
    return dx, grads


def _exchange(name, scatter=(), gather=(), sibling=(), sibling_both=False):
    scatter, gather, sibling = list(scatter), list(gather), list(sibling)
    chip_xs = scatter + gather
    ns, nc, nb = len(scatter), len(chip_xs), len(sibling)
    n = nc + nb

    def body(*refs):
        x_refs, o_refs, send_sems, recv_sems = refs[:n], refs[n:2 * n], refs[2 * n], refs[2 * n + 1]
        mx, my, mc = lax.axis_index("x"), lax.axis_index("y"), lax.axis_index("c")
        me = 2 * mx + my
        copies = []
        for a in range(nc):
            for t, (px, py) in enumerate(((1 - mx, my), (mx, 1 - my), (1 - mx, 1 - my))):
                src = x_refs[a].at[2 * px + py] if a < ns else x_refs[a]
                copies.append(pltpu.make_async_remote_copy(
                    src_ref=src, dst_ref=o_refs[a].at[me], send_sem=send_sems.at[3 * a + t],
                    recv_sem=recv_sems.at[3 * a + t], device_id=(px, py, mc), device_id_type=pl.DeviceIdType.MESH))
        for b in range(nc, n):
            k = 3 * nc + b - nc
            copies.append(pltpu.make_async_remote_copy(
                src_ref=x_refs[b], dst_ref=o_refs[b].at[mc] if sibling_both else o_refs[b], send_sem=send_sems.at[k],
                recv_sem=recv_sems.at[k], device_id=(mx, my, 1 - mc), device_id_type=pl.DeviceIdType.MESH))
        for cp in copies:
            cp.start()
        for cp in copies:
            cp.wait()

    shapes = ([(4,) + tuple(x.shape[1:]) for x in scatter] + [(4,) + tuple(x.shape) for x in gather]
              + [((2,) if sibling_both else ()) + tuple(x.shape) for x in sibling])
    xs = chip_xs + sibling
    outs = pl.pallas_call(
        body, name=name, in_specs=[_ANY] * n, out_specs=[_ANY] * n,
        out_shape=[jax.ShapeDtypeStruct(s, x.dtype) for s, x in zip(shapes, xs)],
        scratch_shapes=[pltpu.SemaphoreType.DMA((3 * nc + nb,)), pltpu.SemaphoreType.DMA((3 * nc + nb,))],
    )(*xs)
    me, c = 2 * lax.axis_index("x") + lax.axis_index("y"), lax.axis_index("c")
    fixed = []
    for a, (o, x) in enumerate(zip(outs, xs)):
        if a < ns:
            o = lax.dynamic_update_index_in_dim(o, lax.dynamic_index_in_dim(x, me, 0, keepdims=True), me, 0)
        elif a < nc:
            o = lax.dynamic_update_index_in_dim(o, x[None], me, 0)
        elif sibling_both:
            o = lax.dynamic_update_index_in_dim(o, x[None], c, 0)
        fixed.append(o)
    return fixed[:ns], fixed[ns:nc], fixed[nc:]


def _rows_tile(rows, row_bytes, budget=5 << 19):
    return next(t for t in (512, 256, 128, 64, 32, 16, 8) if rows % t == 0 and t * row_bytes <= budget)


def _padded_row_bytes(cols):
    return -(-cols // LANES) * LANES * 4


def _add2(a, b, name, out_dtype=f32):
    R, C = a.shape
    tr = _rows_tile(R, _padded_row_bytes(C))

    def body(a_ref, b_ref, o_ref):
        o_ref[...] = (a_ref[...] + b_ref[...]).astype(out_dtype)

    spec = pl.BlockSpec((tr, C), lambda i: (i, 0))
    return _call(body, name, (R // tr,), [spec, spec], spec, jax.ShapeDtypeStruct((R, C), out_dtype),
                 sem=("parallel",))(a, b)


def _sum4(x, name):
    R = x.shape[1]
    tr = _tile(R, (2560, 1024, 512, 256, 128))

    def body(x_ref, o_ref):
        p = [x_ref[j].astype(f32) for j in range(4)]
        o_ref[...] = ((p[0] + p[1]) + p[2]) + p[3]

    return _call(body, name, (R // tr,), [pl.BlockSpec((4, tr, LANES), lambda i: (0, i, 0))],
                 pl.BlockSpec((tr, LANES), lambda i: (i, 0)), jax.ShapeDtypeStruct((R, LANES), f32),
                 sem=("parallel",))(x)


def _adamw(g_parts, w, m, v, name):
    stacked = not isinstance(g_parts, (tuple, list))
    k = g_parts.shape[0] if stacked else len(g_parts)
    R, C = w.shape
    tr = _rows_tile(R, _padded_row_bytes(C))

    def body(*refs):
        w_ref, m_ref, v_ref, g_ref, d_ref, nm_ref, nv_ref = refs[-7:]
        if stacked:
            g = refs[0][0].astype(f32)
            for j in range(1, k):
                g = g + refs[0][j].astype(f32)
        else:
            g = refs[0][...]
            for r in refs[1:k]:
                g = g + r[...]
        g_ref[...] = g
        d_ref[...], nm_ref[...], nv_ref[...] = _adamw_update(g, w_ref[...], m_ref[...], v_ref[...])

    spec = pl.BlockSpec((tr, C), lambda i: (i, 0))
    sd = jax.ShapeDtypeStruct((R, C), f32)
    g_specs = [pl.BlockSpec((k, tr, C), lambda i: (0, i, 0))] if stacked else [spec] * k
    g_args = [g_parts] if stacked else list(g_parts)
    return _call(body, name, (R // tr,), g_specs + [spec] * 3, [spec] * 4, [sd] * 4,
                 sem=("parallel",))(*g_args, w, m, v)


def _adamw_update(g, w, m, v):
    m = ADAM_B1 * m + (1.0 - ADAM_B1) * g
    v = ADAM_B2 * v + (1.0 - ADAM_B2) * (g * g)
    c1 = 1.0 - ADAM_B1 ** ADAM_STEP
    c2 = 1.0 - ADAM_B2 ** ADAM_STEP
    return -ADAM_LR * ((m / c1) / (jnp.sqrt(v / c2) + ADAM_EPS) + ADAM_WD * w), m, v


def _adamw_small(gs, ws, ms, vs, name):
    n = len(gs)

    def body(*refs):
        ins, outs = refs[:4 * n], refs[4 * n:]
        for t in range(n):
            d, m, v = _adamw_update(ins[t][...], ins[n + t][...], ins[2 * n + t][...], ins[3 * n + t][...])
            outs[t][...] = d
            outs[n + t][...] = m
            outs[2 * n + t][...] = v

    vmem = pl.BlockSpec(memory_space=pltpu.VMEM)
    outs = pl.pallas_call(
        body, name=name, in_specs=[vmem] * (4 * n), out_specs=[vmem] * (3 * n),
        out_shape=[jax.ShapeDtypeStruct(w.shape, f32) for w in ws] * 3,
        compiler_params=pltpu.CompilerParams(vmem_limit_bytes=V7X_VMEM_LIMIT))(*gs, *ws, *ms, *vs)
    return outs[:n], outs[n:2 * n], outs[2 * n:]


def _pack(arrays, row_multiple=PACK_ROWS):
    flat = jnp.concatenate([a.reshape(-1) for a in arrays])
    unit = row_multiple * LANES
    n = -(-flat.shape[0] // unit) * unit
    return jnp.pad(flat, (0, n - flat.shape[0])).reshape(n // LANES, LANES)


def _unpack(buf, shapes):
    flat = buf.reshape(-1)
    out, off = [], 0
    for s in shapes:
        n = 1
        for dim in s:
            n *= dim
        out.append(flat[off:off + n].reshape(s))
        off += n
    return out


def _to_shards(full, axis):
    s = full.shape
    t = full.reshape(s[:axis] + (4, s[axis] // 4) + s[axis + 1:])
    return jnp.moveaxis(t, axis, 0)


def _from_shards(sh, axis):
    t = jnp.moveaxis(sh, 0, axis)
    s = t.shape
    return t.reshape(s[:axis] + (s[axis] * s[axis + 1],) + s[axis + 2:])


def kernel(x, norm_w, w_in, s5_a_re, s5_a_im, s5_log_step, s5_b_re, s5_b_im, s5_c_re, s5_c_im, s5_d, s5_glu_w, s5_glu_b, q_norm_w, k_norm_w, conv_w, conv_b, dt_bias, ssd_a_log, ssd_d, ssd_norm_w, proj_a, proj_b, proj_c, w_out, loss_target, m_norm_w, m_w_in, m_s5_a_re, m_s5_a_im, m_s5_log_step, m_s5_b_re, m_s5_b_im, m_s5_c_re, m_s5_c_im, m_s5_d, m_s5_glu_w, m_s5_glu_b, m_q_norm_w, m_k_norm_w, m_conv_w, m_conv_b, m_dt_bias, m_ssd_a_log, m_ssd_d, m_ssd_norm_w, m_proj_a, m_proj_b, m_proj_c, m_w_out, v_norm_w, v_w_in, v_s5_a_re, v_s5_a_im, v_s5_log_step, v_s5_b_re, v_s5_b_im, v_s5_c_re, v_s5_c_im, v_s5_d, v_s5_glu_w, v_s5_glu_b, v_q_norm_w, v_k_norm_w, v_conv_w, v_conv_b, v_dt_bias, v_ssd_a_log, v_ssd_d, v_ssd_norm_w, v_proj_a, v_proj_b, v_proj_c, v_w_out):
    given = dict(locals())
    W = {n: given[n] for n in _WEIGHTS}
    M = {n: given["m_" + n] for n in _WEIGHTS}
    V = {n: given["v_" + n] for n in _WEIGHTS}
    n_layers = norm_w.shape[0]
    assert n_layers == 2
    c = lax.axis_index("c")

    mine_of = lambda t: lax.dynamic_index_in_dim(t, c, 0, keepdims=False)
    as_payload = lambda n: lax.bitcast_convert_type(W[n], bf16) if n == "conv_w" else W[n].astype(bf16)
    payload_shapes = [W[n].shape + ((2,) if n == "conv_w" else ()) for n, _ in _SHARDED]
    wpack = _pack([as_payload(n) for n, _ in _SHARDED])
    half_rows = wpack.shape[0] // 2
    _, (pack_half, w_in_mine_layer), _ = _exchange(
        "gather_weights", gather=[lax.dynamic_slice_in_dim(wpack, c * half_rows, half_rows),
                                  mine_of(w_in).astype(bf16)])
    _, _, (w_in_layers, pack_halves) = _exchange("share_weights", sibling=[w_in_mine_layer, pack_half],
                                                 sibling_both=True)
    gathered = jnp.moveaxis(pack_halves, 0, 1).reshape(4, 2 * half_rows, LANES)
    full = dict(W)
    pieces = [_unpack(gathered[j], payload_shapes) for j in range(4)]
    for k, (n, axis) in enumerate(_SHARDED):
        sh = jnp.stack([pieces[j][k] for j in range(4)])
        full[n] = _from_shards(lax.bitcast_convert_type(sh, f32) if n == "conv_w" else sh, axis)

    xs = x[0]
    qs, saves = [], []
    act = xs
    for l in range(n_layers):
        p = {n: full[n][l] for n in _WEIGHTS if n != "w_in"}
        p["w_in"] = [w_in_layers[l, k] for k in range(4)]
        q = _prep_layer(p)
        act, sv = layer_fwd(act, q, f"_l{l}")
        qs.append((q, p))
        saves.append(sv)
    dact, lsum = loss_and_grad(act, loss_target[0], "loss")
    loss = lax.psum(lsum[0, 0], ("x", "y", "c"))
    layer_grads = [None] * n_layers
    for l in reversed(range(n_layers)):
        q, p = qs[l]
        dact, layer_grads[l] = layer_bwd(dact, saves[l], q, p, f"_l{l}")
    grad_x = dact[None]
    G = {n: jnp.stack([layer_grads[l][n] for l in range(n_layers)]) for n in _WEIGHTS if n != "w_in"}

    repl_shapes = [W[n].shape for n in _REPL]
    small = _pack([G[n] for n in _REPL], 4 * PACK_ROWS)
    quarter = small.shape[0] // 4
    big = [_to_shards(G[n], axis).reshape(4, -1) for n, axis in _SHARDED]
    big = jnp.concatenate(big, axis=1)
    unit = PACK_ROWS * LANES
    nbig = -(-big.shape[1] // unit) * unit
    big = jnp.pad(big, ((0, 0), (0, nbig - big.shape[1]))).reshape(4, nbig // LANES, LANES)
    gpack = jnp.concatenate([big, small.reshape(4, quarter, LANES)], axis=1)
    rbig = nbig // LANES
    g0, g1 = layer_grads[0]["w_in"], layer_grads[1]["w_in"]

    (landed_pack,), _, (from_sibling,) = _exchange(
        "swap_w_in_grads_and_scatter_grads", scatter=[gpack.astype(bf16)], sibling=[jnp.where(c == 0, g1, g0)])
    flat = lambda t: t.reshape(4 * D_MODEL, W_IN_SHARD)
    shards = _add2(flat(jnp.where(c == 0, g0, g1)), flat(from_sibling), "sum_cores_w_in", out_dtype=bf16)
    mine = _sum4(landed_pack, "sum_chips")

    (landed,), _, (other,) = _exchange(
        "scatter_w_in_grads_and_swap_cores", scatter=[shards.reshape(4, D_MODEL, W_IN_SHARD)], sibling=[mine])
    w_in_mine = _adamw(landed, mine_of(w_in), mine_of(m_w_in), mine_of(v_w_in), "adamw_w_in")
    gq = _add2(mine[rbig:], other[rbig:], "sum_cores_small")

    _, (gsmall,), w_in_out = _exchange(
        "share_w_in_updates_and_gather_small", gather=[gq], sibling=w_in_mine, sibling_both=True)
    gsmall = gsmall.reshape(4 * quarter, LANES)

    shard_shapes = [W[n].shape for n, _ in _SHARDED]
    g_mine, g_other = _unpack(mine[:rbig], shard_shapes), _unpack(other[:rbig], shard_shapes)
    rows_of = lambda t: t.reshape(-1, t.shape[-1])
    res = [dict(), dict(), dict(), dict()]
    for k, (n, _) in enumerate(_SHARDED):
        outs = _adamw((rows_of(g_mine[k]), rows_of(g_other[k])), rows_of(W[n]), rows_of(M[n]), rows_of(V[n]),
                      f"adamw_{n}")
        for kind in range(4):
            res[kind][n] = outs[kind].reshape(W[n].shape)
    g_small = _unpack(gsmall, repl_shapes)
    small_out = _adamw_small([rows_of(g) for g in g_small], *([rows_of(T[n]) for n in _REPL] for T in (W, M, V)),
                             "adamw_replicated")
    for kind in range(4):
        res[kind]["w_in"] = w_in_out[kind]
        for k, n in enumerate(_REPL):
            res[kind][n] = g_small[k] if kind == 0 else small_out[kind - 1][k].reshape(W[n].shape)
    return (loss, grad_x, *[res[0][n] for n in _WEIGHTS], *[res[1][n] for n in _WEIGHTS],
            *[res[2][n] for n in _WEIGHTS], *[res[3][n] for n in _WEIGHTS])
```

```python
import functools

import jax
import jax.numpy as jnp
from jax import lax
from jax.experimental import pallas as pl
from jax.experimental.pallas import tpu as pltpu

f32 = jnp.float32
bf16 = jnp.bfloat16

D_MODEL = 1024
RMS_EPS = 1e-6
V7X_VMEM_LIMIT = 60 * 1024 * 1024
LANES = 128
NN, NT, TN = ((1,), (0,)), ((1,), (1,)), ((0,), (0,))

S5_STATES = 2048
S5_ROWS = 512
ATT_SEG = 2048
ATT_BLOCK = 128
SSD_CHUNK = 128
SSD_CHUNKS_PER_STEP = 2
SSD_WIDTH = 768
SSD_XBC = 1280
CONV_ROWS = 512
TAIL_ROWS = 256

ADAM_LR, ADAM_B1, ADAM_B2, ADAM_EPS, ADAM_WD, ADAM_STEP = 0.001, 0.9, 0.999, 1e-08, 0.01, 10

_C_UA, _C_ZA, _C_Q, _C_K, _C_V, _C_ZB, _C_XBC, _C_DT, _C_ZC, _C_GATE, _C_END = (
    0, 512, 1024, 1792, 2560, 3328, 3584, 4864, 4876, 5644, 8716)

_SHARDED = (("s5_glu_w", 1), ("conv_w", 2), ("proj_a", 2), ("proj_b", 2), ("proj_c", 2), ("w_out", 1))
W_IN_SHARD = 2179
_REPL = ("norm_w", "s5_a_re", "s5_a_im", "s5_log_step", "s5_b_re", "s5_b_im", "s5_c_re", "s5_c_im", "s5_d",
         "s5_glu_b", "q_norm_w", "k_norm_w", "conv_b", "dt_bias", "ssd_a_log", "ssd_d", "ssd_norm_w")
_WEIGHTS = ("norm_w", "w_in", "s5_a_re", "s5_a_im", "s5_log_step", "s5_b_re", "s5_b_im", "s5_c_re", "s5_c_im",
            "s5_d", "s5_glu_w", "s5_glu_b", "q_norm_w", "k_norm_w", "conv_w", "conv_b", "dt_bias", "ssd_a_log",
            "ssd_d", "ssd_norm_w", "proj_a", "proj_b", "proj_c", "w_out")
PACK_ROWS = 512


def _dot(a, b, dims):
    return lax.dot_general(a.astype(bf16), b.astype(bf16), (dims, ((), ())), preferred_element_type=f32)


_ANY = pl.BlockSpec(memory_space=pl.ANY)

MAIN_WIDTH = 8448
MAIN_SSD_BLOCK = 3
MAIN_DT_BLOCK = 46
MAIN_ATT_BLOCK = 16


def _call(body, name, grid, in_specs, out_specs, out_shape, scratch=(), sem=None, aliases=None):
    return pl.pallas_call(
        body, name=name, grid=grid, in_specs=in_specs, out_specs=out_specs, out_shape=out_shape,
        scratch_shapes=list(scratch), input_output_aliases=aliases or {},
        compiler_params=pltpu.CompilerParams(dimension_semantics=sem, vmem_limit_bytes=V7X_VMEM_LIMIT))


def _tile(n, options=(1024, 768, 512, 384, 256, 128)):
    return next(t for t in options if n % t == 0)


@functools.partial(jax.custom_vjp, nondiff_argnums=(2,))
def _bdot(a, b, dims):
    return _dot(a, b, dims)


def _bdot_fwd(a, b, dims):
    return _dot(a, b, dims), (a, b)


def _bdot_bwd(dims, res, g):
    a, b = res
    if dims == NN:
        da, db = _dot(g, b, NT), _dot(a, g, TN)
    elif dims == NT:
        da, db = _dot(g, b, NN), _dot(g, a, TN)
    else:
        da, db = _dot(b, g, NT), _dot(a, g, NN)
    return da.astype(a.dtype), db.astype(b.dtype)


_bdot.defvjp(_bdot_fwd, _bdot_bwd)


@functools.partial(jax.custom_vjp, nondiff_argnums=(2,))
def _cdot(a, w, dims):
    return _dot(a, w, dims)


def _cdot_fwd(a, w, dims):
    return _dot(a, w, dims), w


def _cdot_bwd(dims, w, g):
    da = _dot(g, w, NT) if dims == NN else _dot(g, w, NN)
    return da, jnp.zeros_like(w)


_cdot.defvjp(_cdot_fwd, _cdot_bwd)


def _split3(x):
    hi = x.astype(bf16)
    r = x - hi.astype(f32)
    mid = r.astype(bf16)
    lo = (r - mid.astype(f32)).astype(bf16)
    return hi, mid, lo


@jax.custom_vjp
def _xdot_l(m, x):
    return sum(_dot(m, p, NN) for p in _split3(x))


def _xdot_l_fwd(m, x):
    return _xdot_l(m, x), m


def _xdot_l_bwd(m, g):
    return jnp.zeros_like(m), sum(_dot(m, p, TN) for p in _split3(g))


_xdot_l.defvjp(_xdot_l_fwd, _xdot_l_bwd)


@jax.custom_vjp
def _softplus(x):
    e = jnp.exp(-jnp.abs(x))
    u = 1.0 + e
    log1p = jnp.where(u == 1.0, e, jnp.log(u) * (e / jnp.where(u == 1.0, 1.0, u - 1.0)))
    return jnp.maximum(x, 0.0) + log1p


def _softplus_fwd(x):
    return _softplus(x), x


def _softplus_bwd(x, g):
    return (g * jax.nn.sigmoid(x),)


_softplus.defvjp(_softplus_fwd, _softplus_bwd)


def _rms(x, w):
    return x * lax.rsqrt(jnp.mean(x * x, axis=-1, keepdims=True) + RMS_EPS) * w


def mm_nn(a, b, name, tm=2048):
    M, K = a.shape
    N = b.shape[1]
    tn = _tile(N)

    def body(a_ref, b_ref, o_ref):
        o_ref[...] = _dot(a_ref[...], b_ref[...], NN)

    return _call(body, name, (M // tm, N // tn),
                 [pl.BlockSpec((tm, K), lambda i, j: (i, 0)), pl.BlockSpec((K, tn), lambda i, j: (0, j))],
                 pl.BlockSpec((tm, tn), lambda i, j: (i, j)), jax.ShapeDtypeStruct((M, N), f32),
                 sem=("parallel", "parallel"))(a, b)


def mm_nt(a, b, name, tm=1024):
    M, K = a.shape
    N = b.shape[0]
    tk = _tile(K, (2816, 1024, 768, 512, 256, 128))

    def body(a_ref, b_ref, o_ref):
        k = pl.program_id(1)
        p = _dot(a_ref[...], b_ref[...], NT)

        @pl.when(k == 0)
        def _():
            o_ref[...] = p

        @pl.when(k > 0)
        def _():
            o_ref[...] += p

    return _call(body, name, (M // tm, K // tk),
                 [pl.BlockSpec((tm, tk), lambda i, k: (i, k)), pl.BlockSpec((N, tk), lambda i, k: (0, k))],
                 pl.BlockSpec((tm, N), lambda i, k: (i, 0)), jax.ShapeDtypeStruct((M, N), f32),
                 sem=("parallel", "arbitrary"))(a, b)


def mm_tn(a, b, name, tk=2048):
    K, M = a.shape
    N = b.shape[1]
    tn = _tile(N)

    def body(a_ref, b_ref, o_ref):
        k = pl.program_id(1)
        p = _dot(a_ref[...], b_ref[...], TN)

        @pl.when(k == 0)
        def _():
            o_ref[...] = p

        @pl.when(k > 0)
        def _():
            o_ref[...] += p

    return _call(body, name, (N // tn, K // tk),
                 [pl.BlockSpec((tk, M), lambda j, k: (k, 0)), pl.BlockSpec((tk, tn), lambda j, k: (k, j))],
                 pl.BlockSpec((M, tn), lambda j, k: (0, j)), jax.ShapeDtypeStruct((M, N), f32),
                 sem=("parallel", "arbitrary"))(a, b)


def rms_fwd(x, w, name, tm=512):
    S = x.shape[0]

    def body(x_ref, w_ref, o_ref):
        o_ref[...] = _rms(x_ref[...], w_ref[...]).astype(bf16)

    return _call(body, name, (S // tm,),
                 [pl.BlockSpec((tm, D_MODEL), lambda i: (i, 0)), pl.BlockSpec((1, D_MODEL), lambda i: (0, 0))],
                 pl.BlockSpec((tm, D_MODEL), lambda i: (i, 0)), jax.ShapeDtypeStruct((S, D_MODEL), bf16),
                 sem=("parallel",))(x, w)


def mm_nt_rms_bwd(a, b, acc, x, w, dres, name, tm=1024):
    S, K = a.shape

    def body(a_ref, b_ref, acc_ref, x_ref, w_ref, dr_ref, dx_ref, dw_ref):
        dh = _dot(a_ref[...], b_ref[...], NT) + acc_ref[...]
        _, vjp = jax.vjp(_rms, x_ref[...], w_ref[...])
        dx, dw = vjp(dh)
        dx_ref[...] = dx + dr_ref[...]

        @pl.when(pl.program_id(0) == 0)
        def _():
            dw_ref[...] = dw

        @pl.when(pl.program_id(0) > 0)
        def _():
            dw_ref[...] += dw

    row = pl.BlockSpec((tm, D_MODEL), lambda i: (i, 0))
    vec = pl.BlockSpec((1, D_MODEL), lambda i: (0, 0))
    return _call(body, name, (S // tm,),
                 [pl.BlockSpec((tm, K), lambda i: (i, 0)), pl.BlockSpec((D_MODEL, K), lambda i: (0, 0)), row, row, vec,
                  row], [row, vec],
                 [jax.ShapeDtypeStruct((S, D_MODEL), f32), jax.ShapeDtypeStruct((1, D_MODEL), f32)],
                 sem=("arbitrary",))(a, b, acc, x, w, dres)


def _s5_discretize(a_re, a_im, log_step, b_re, b_im, c_re, c_im):
    step = jnp.exp(log_step)[:, None]
    mag = jnp.exp(a_re * step)
    ang = a_im * step
    lam_re, lam_im = mag * jnp.cos(ang), mag * jnp.sin(ang)
    num_re, num_im = lam_re - 1.0, lam_im
    den = a_re * a_re + a_im * a_im
    f_re = (num_re * a_re + num_im * a_im) / den
    f_im = (num_im * a_re - num_re * a_im) / den
    bb_re = f_re[..., None] * b_re - f_im[..., None] * b_im
    bb_im = f_re[..., None] * b_im + f_im[..., None] * b_re
    eye = jnp.eye(8, dtype=f32)

    def block_in(bb):
        t = bb.transpose(0, 2, 1).reshape(4, 8, 16, 1, 64)
        return (t * eye[None, :, None, :, None]).reshape(4, 128, 512)

    def block_out(c):
        t = c.transpose(0, 2, 1).reshape(4, 8, 64, 1, 16)
        return (t * eye[None, :, None, :, None]).reshape(4, 512, 128)

    return (lam_re.reshape(1, S5_STATES), lam_im.reshape(1, S5_STATES), block_in(bb_re), block_in(bb_im),
            block_out(c_re), block_out(c_im))


def _lam_powers(lam_re, lam_im):
    rows_re, rows_im = [lam_re], [lam_im]
    for _ in range(7):
        pr, pi = rows_re[-1], rows_im[-1]
        rows_re.append(pr * lam_re - pi * lam_im)
        rows_im.append(pr * lam_im + pi * lam_re)
    return jnp.concatenate(rows_re, 0), jnp.concatenate(rows_im, 0)


def s5_fwd(u, pw_re, pw_im, w_re, w_im, c_re, c_im, dvec, name):
    S = u.shape[0]
    R, NS = S5_ROWS, S5_STATES
    nb = R // 8

    def body(u_ref, pwr_ref, pwi_ref, wre_ref, wim_ref, cre_ref, cim_ref, d_ref, y_ref, hr_ref, hi_ref,
             car_re, car_im, cin_re, cin_im, up, yp):
        @pl.when(pl.program_id(0) == 0)
        def _():
            car_re[...] = jnp.zeros_like(car_re)
            car_im[...] = jnp.zeros_like(car_im)

        slab = lambda r: pl.ds(r * nb, nb)
        for r in range(8):
            up[slab(r), :] = u_ref[:, r, :]
        u = up[...]
        for j in range(4):
            uj = u[:, 128 * j:128 * (j + 1)]
            hr_ref[:, 512 * j:512 * (j + 1)] = _dot(uj, wre_ref[j], NN)
            hi_ref[:, 512 * j:512 * (j + 1)] = _dot(uj, wim_ref[j], NN)
        lr, li = pwr_ref[0:1, :], pwi_ref[0:1, :]
        for r in range(1, 8):
            pr, pi = hr_ref[slab(r - 1), :], hi_ref[slab(r - 1), :]
            hr_ref[slab(r), :] = lr * pr - li * pi + hr_ref[slab(r), :]
            hi_ref[slab(r), :] = lr * pi + li * pr + hi_ref[slab(r), :]
        l8r, l8i = pwr_ref[7:8, :], pwi_ref[7:8, :]

        def across(c, carry):
            gr, gi = carry
            cin_re[pl.ds(c, 1), :] = gr
            cin_im[pl.ds(c, 1), :] = gi
            er, ei = hr_ref[pl.ds(7 * nb + c, 1), :], hi_ref[pl.ds(7 * nb + c, 1), :]
            return l8r * gr - l8i * gi + er, l8r * gi + l8i * gr + ei

        gr, gi = lax.fori_loop(0, nb, across, (car_re[...], car_im[...]))
        car_re[...] = gr
        car_im[...] = gi
        cr, ci = cin_re[...], cin_im[...]
        for r in range(8):
            pr, pi = pwr_ref[r:r + 1, :], pwi_ref[r:r + 1, :]
            hr_ref[slab(r), :] = hr_ref[slab(r), :] + pr * cr - pi * ci
            hi_ref[slab(r), :] = hi_ref[slab(r), :] + pr * ci + pi * cr
        for j in range(4):
            sl = slice(512 * j, 512 * (j + 1))
            cs = slice(128 * j, 128 * (j + 1))
            yp[:, cs] = (_dot(hr_ref[:, sl], cre_ref[j], NN) - _dot(hi_ref[:, sl], cim_ref[j], NN)
                         + d_ref[:, cs] * u[:, cs])
        for r in range(8):
            y_ref[:, r, :] = yp[slab(r), :]

    full = lambda shape: pl.BlockSpec(shape, lambda i: (0,) * len(shape))
    hspec = pl.BlockSpec((R, NS), lambda i: (i, 0))
    uspec = pl.BlockSpec((nb, 8, 512), lambda i: (i, 0, 0))
    y, h_re, h_im = _call(
        body, name, (S // R,),
        [uspec, full((8, NS)), full((8, NS)), full((4, 128, 512)),
         full((4, 128, 512)), full((4, 512, 128)), full((4, 512, 128)), full((1, 512))],
        [uspec, hspec, hspec],
        [jax.ShapeDtypeStruct((S // 8, 8, 512), f32), jax.ShapeDtypeStruct((S, NS), f32),
         jax.ShapeDtypeStruct((S, NS), f32)],
        scratch=[pltpu.VMEM((1, NS), f32), pltpu.VMEM((1, NS), f32), pltpu.VMEM((nb, NS), f32),
                 pltpu.VMEM((nb, NS), f32), pltpu.VMEM((R, 512), f32), pltpu.VMEM((R, 512), f32)],
        sem=("arbitrary",))(u.reshape(S // 8, 8, 512), pw_re, pw_im, w_re.astype(bf16), w_im.astype(bf16),
                            c_re.astype(bf16), c_im.astype(bf16), dvec)
    return y.reshape(S, 512), h_re, h_im


def s5_bwd(dy, u, h_re, h_im, pw_re, pw_im, w_re, w_im, c_re, c_im, dvec, name):
    S = u.shape[0]
    R, NS = S5_ROWS, S5_STATES
    nb = R // 8
    nchunk = S // R

    def body(dy_ref, u_ref, hr_ref, hi_ref, hpr_ref, hpi_ref, pwr_ref, pwi_ref, wre_ref, wim_ref, cre_ref, cim_ref,
             d_ref, du_ref, dwre_ref, dwim_ref, dcre_ref, dcim_ref, dlr_ref, dli_ref, dd_ref,
             ar, ai, car_re, car_im, cin_re, cin_im, up, dyp, dup):
        i = pl.program_id(0)

        @pl.when(i == 0)
        def _():
            for ref in (car_re, car_im, dwre_ref, dwim_ref, dcre_ref, dcim_ref, dlr_ref, dli_ref, dd_ref):
                ref[...] = jnp.zeros_like(ref)

        slab = lambda r: pl.ds(r * nb, nb)
        for r in range(8):
            up[slab(r), :] = u_ref[:, r, :]
            dyp[slab(r), :] = dy_ref[:, r, :]
        dy = dyp[...]
        u = up[...]
        for j in range(4):
            dyj = dy[:, 128 * j:128 * (j + 1)]
            ar[:, 512 * j:512 * (j + 1)] = _dot(dyj, cre_ref[j], NT)
            ai[:, 512 * j:512 * (j + 1)] = -_dot(dyj, cim_ref[j], NT)
        lr, li = pwr_ref[0:1, :], pwi_ref[0:1, :]
        for r in range(6, -1, -1):
            nr, ni = ar[slab(r + 1), :], ai[slab(r + 1), :]
            ar[slab(r), :] = lr * nr + li * ni + ar[slab(r), :]
            ai[slab(r), :] = lr * ni - li * nr + ai[slab(r), :]
        l8r, l8i = pwr_ref[7:8, :], pwi_ref[7:8, :]

        def across(k, carry):
            c = nb - 1 - k
            gr, gi = carry
            cin_re[pl.ds(c, 1), :] = gr
            cin_im[pl.ds(c, 1), :] = gi
            er, ei = ar[pl.ds(c, 1), :], ai[pl.ds(c, 1), :]
            return l8r * gr + l8i * gi + er, l8r * gi - l8i * gr + ei

        gr, gi = lax.fori_loop(0, nb, across, (car_re[...], car_im[...]))
        car_re[...] = gr
        car_im[...] = gi
        cr, ci = cin_re[...], cin_im[...]
        for r in range(8):
            pr, pi = pwr_ref[7 - r:8 - r, :], pwi_ref[7 - r:8 - r, :]
            ar[slab(r), :] = ar[slab(r), :] + pr * cr + pi * ci
            ai[slab(r), :] = ai[slab(r), :] + pr * ci - pi * cr

        acc_r = jnp.zeros((1, NS), f32)
        acc_i = jnp.zeros((1, NS), f32)
        has_prev = (i < nchunk - 1).astype(f32)
        top = lax.broadcasted_iota(jnp.int32, (nb, NS), 0) == 0
        for r in range(8):
            if r == 0:
                xr = jnp.where(top, hpr_ref[7:8, :] * has_prev, pltpu.roll(hr_ref[slab(7), :], 1, 0))
                xi = jnp.where(top, hpi_ref[7:8, :] * has_prev, pltpu.roll(hi_ref[slab(7), :], 1, 0))
            else:
                xr, xi = hr_ref[slab(r - 1), :], hi_ref[slab(r - 1), :]
            br, bi = ar[slab(r), :], ai[slab(r), :]
            acc_r += jnp.sum(br * xr + bi * xi, axis=0, keepdims=True)
            acc_i += jnp.sum(bi * xr - br * xi, axis=0, keepdims=True)
        dlr_ref[...] += acc_r
        dli_ref[...] += acc_i
        dd_ref[...] += jnp.sum(dy * u, axis=0, keepdims=True)

        for j in range(4):
            sl = slice(512 * j, 512 * (j + 1))
            cs = slice(128 * j, 128 * (j + 1))
            arj, aij = ar[:, sl], ai[:, sl]
            uj, dyj = u[:, cs], dy[:, cs]
            dup[:, cs] = _dot(arj, wre_ref[j], NT) + _dot(aij, wim_ref[j], NT) + d_ref[:, cs] * dyj
            dwre_ref[j] += _dot(uj, arj, TN)
            dwim_ref[j] += _dot(uj, aij, TN)
            dcre_ref[j] += _dot(hr_ref[:, sl], dyj, TN)
            dcim_ref[j] -= _dot(hi_ref[:, sl], dyj, TN)
        for r in range(8):
            du_ref[:, r, :] = dup[slab(r), :]

    rev = lambda i: nchunk - 1 - i
    full = lambda shape: pl.BlockSpec(shape, lambda i: (0,) * len(shape))
    row = pl.BlockSpec((nb, 8, 512), lambda i: (rev(i), 0, 0))
    hspec = pl.BlockSpec((R, NS), lambda i: (rev(i), 0))
    hprev = pl.BlockSpec((8, NS), lambda i: (jnp.maximum(rev(i) * nb - 1, 0), 0))
    outs = _call(
        body, name, (nchunk,),
        [row, row, hspec, hspec, hprev, hprev, full((8, NS)), full((8, NS)), full((4, 128, 512)), full((4, 128, 512)),
         full((4, 512, 128)), full((4, 512, 128)), full((1, 512))],
        [row, full((4, 128, 512)), full((4, 128, 512)), full((4, 512, 128)), full((4, 512, 128)),
         full((1, NS)), full((1, NS)), full((1, 512))],
        [jax.ShapeDtypeStruct((S // 8, 8, 512), f32), jax.ShapeDtypeStruct((4, 128, 512), f32),
         jax.ShapeDtypeStruct((4, 128, 512), f32), jax.ShapeDtypeStruct((4, 512, 128), f32),
         jax.ShapeDtypeStruct((4, 512, 128), f32), jax.ShapeDtypeStruct((1, NS), f32),
         jax.ShapeDtypeStruct((1, NS), f32), jax.ShapeDtypeStruct((1, 512), f32)],
        scratch=[pltpu.VMEM((R, NS), f32), pltpu.VMEM((R, NS), f32), pltpu.VMEM((1, NS), f32),
                 pltpu.VMEM((1, NS), f32), pltpu.VMEM((nb, NS), f32), pltpu.VMEM((nb, NS), f32),
                 pltpu.VMEM((R, 512), f32), pltpu.VMEM((R, 512), f32), pltpu.VMEM((R, 512), f32)],
        sem=("arbitrary",))(dy.reshape(S // 8, 8, 512), u.reshape(S // 8, 8, 512), h_re, h_im, h_re, h_im, pw_re,
                            pw_im, w_re.astype(bf16), w_im.astype(bf16), c_re.astype(bf16), c_im.astype(bf16), dvec)
    return (outs[0].reshape(S, 512),) + tuple(outs[1:])


def _rows(start, n, d):
    return pl.ds(pl.multiple_of(start, ATT_BLOCK), n) if d == 1 else pl.ds(start, n, stride=d)


def _head_masks():
    lane = lax.broadcasted_iota(jnp.int32, (1, LANES), 1)
    return [(lane < 64).astype(f32), (lane >= 64).astype(f32)]


def _head_norm(x, w, hm):
    x2 = x * x
    r = [lax.rsqrt(jnp.sum(x2 * hm[h], axis=-1, keepdims=True) * (1.0 / 64) + RMS_EPS) for h in range(2)]
    sc = hm[0] * r[0] + hm[1] * r[1]
    return x * sc * w, sc, r


def _head_norm_bwd(x, w, sc, r, dxn, hm):
    dw = jnp.sum(dxn * x * sc, axis=0, keepdims=True)
    t = dxn * w
    tx = t * x
    corr = sum(hm[h] * (r[h] * r[h] * r[h]) * jnp.sum(tx * hm[h], axis=-1, keepdims=True) for h in range(2))
    return t * sc - x * corr * (1.0 / 64), dw


def _att_mask(has_prev):
    qi = lax.broadcasted_iota(jnp.int32, (ATT_BLOCK, 2 * ATT_BLOCK), 0) + ATT_BLOCK
    kj = lax.broadcasted_iota(jnp.int32, (ATT_BLOCK, 2 * ATT_BLOCK), 1)
    return (qi - kj >= 0) & (qi - kj <= ATT_BLOCK) & (has_prev | (kj >= ATT_BLOCK))


def _att_block_bwd(q, k, v, o, lse, do, dlse, qw, kw, has_prev):
    hm = _head_masks()
    mask = _att_mask(has_prev)
    qn, qsc, qr = _head_norm(q, qw, hm)
    kn, ksc, kr = _head_norm(k, kw, hm)
    dqn = jnp.zeros((ATT_BLOCK, LANES), f32)
    dkn = jnp.zeros((2 * ATT_BLOCK, LANES), f32)
    dv = jnp.zeros((2 * ATT_BLOCK, LANES), f32)
    for h in range(2):
        qh, do_h = qn * hm[h], do * hm[h]
        s = _dot(qh, kn, NT) * 0.125
        p = jnp.exp(jnp.where(mask, s - lse[:, 64 * h:64 * h + 1], -jnp.inf))
        dp = _dot(do_h, v, NT)
        delta = jnp.sum(do_h * o, axis=-1, keepdims=True)
        dl = jnp.sum(dlse * hm[h], axis=-1, keepdims=True)
        ds = p * (dp - delta + dl) * 0.125
        dqn = dqn + hm[h] * _dot(ds, kn, NN)
        dkn = dkn + _dot(ds, qh, TN)
        dv = dv + _dot(p, do_h, TN)
    dq, dqw = _head_norm_bwd(q, qw, qsc, qr, dqn, hm)
    dk, dkw = _head_norm_bwd(k, kw, ksc, kr, dkn, hm)
    return dq, dk, dv, dqw, dkw


def _att_block(q, k, v, qw, kw, has_prev):
    hm = _head_masks()
    qn, kn = _head_norm(q, qw, hm)[0], _head_norm(k, kw, hm)[0]
    mask = _att_mask(has_prev)
    o = jnp.zeros((ATT_BLOCK, LANES), f32)
    lse = jnp.zeros((ATT_BLOCK, LANES), f32)
    for h in range(2):
        s = _bdot(qn * hm[h], kn, NT) * 0.125
        s = jnp.where(mask, s, -jnp.inf)
        m = jnp.max(s, axis=-1, keepdims=True)
        p = jnp.exp(s - m)
        l = jnp.sum(p, axis=-1, keepdims=True)
        o = o + hm[h] * _bdot(p / l, v, NN)
        lse = lse + hm[h] * (m + jnp.log(l))
    return o, lse


def att_fwd(p_att, qw, kw, d, g, name):
    S = p_att.shape[0]
    SEG = ATT_SEG
    nblk = SEG // ATT_BLOCK

    def body(p_ref, qw_ref, kw_ref, o_ref, l_ref, q_s, k_ext, v_ext, o_s, l_s):
        seg = pl.program_id(1)

        @pl.when(seg == 0)
        def _():
            k_ext[SEG:, :] = jnp.zeros((SEG, LANES), f32)
            v_ext[SEG:, :] = jnp.zeros((SEG, LANES), f32)

        k_ext[:SEG, :] = k_ext[SEG:, :]
        v_ext[:SEG, :] = v_ext[SEG:, :]
        q_s[...] = p_ref[:, 0:128]
        k_ext[SEG:, :] = p_ref[:, 128:256]
        v_ext[SEG:, :] = p_ref[:, 256:384]
        qw_v, kw_v = qw_ref[...], kw_ref[...]

        def blk(b, carry):
            j, r = b // d, b % d
            qs = j * (ATT_BLOCK * d) + r
            ks = SEG + qs - ATT_BLOCK * d
            o, lse = _att_block(q_s[_rows(qs, ATT_BLOCK, d), :], k_ext[_rows(ks, 2 * ATT_BLOCK, d), :],
                                v_ext[_rows(ks, 2 * ATT_BLOCK, d), :], qw_v, kw_v, (seg > 0) | (j > 0))
            o_s[_rows(qs, ATT_BLOCK, d), :] = o
            l_s[_rows(qs, ATT_BLOCK, d), :] = lse
            return carry

        lax.fori_loop(0, nblk, blk, 0, unroll=4)
        o_ref[...] = o_s[...]
        l_ref[...] = l_s[...]

    vec = pl.BlockSpec((1, LANES), lambda hh, s: (0, 0))
    out = pl.BlockSpec((SEG, LANES), lambda hh, s: (s, hh))
    return _call(body, name, (2, S // SEG), [pl.BlockSpec((SEG, 384), lambda hh, s: (s, MAIN_ATT_BLOCK + 2 * g + hh)), vec, vec],
                 [out, out], [jax.ShapeDtypeStruct((S, 256), f32), jax.ShapeDtypeStruct((S, 256), f32)],
                 scratch=[pltpu.VMEM((SEG, LANES), f32), pltpu.VMEM((2 * SEG, LANES), f32),
                          pltpu.VMEM((2 * SEG, LANES), f32), pltpu.VMEM((SEG, LANES), f32),
                          pltpu.VMEM((SEG, LANES), f32)],
                 sem=("arbitrary", "arbitrary"))(p_att, qw, kw)


def att_bwd(p_att, o, lse, do, dlse, qw, kw, d, g, dp_main, name):
    S = p_att.shape[0]
    SEG = ATT_SEG
    nseg = S // SEG
    nblk = SEG // ATT_BLOCK

    def body(p_ref, pp_ref, o_ref, l_ref, do_ref, dl_ref, qw_ref, kw_ref, _, dp_ref, dqw_ref, dkw_ref,
             q_s, k_ext, v_ext, dq_s, dk_ext, dv_ext):
        hh, i = pl.program_id(0), pl.program_id(1)
        seg = nseg - 1 - i

        @pl.when(i == 0)
        def _():
            dk_ext[...] = jnp.zeros_like(dk_ext)
            dv_ext[...] = jnp.zeros_like(dv_ext)

        @pl.when((i == 0) & (hh == 0))
        def _():
            dqw_ref[...] = jnp.zeros_like(dqw_ref)
            dkw_ref[...] = jnp.zeros_like(dkw_ref)

        dk_ext[SEG:, :] = dk_ext[:SEG, :]
        dv_ext[SEG:, :] = dv_ext[:SEG, :]
        dk_ext[:SEG, :] = jnp.zeros((SEG, LANES), f32)
        dv_ext[:SEG, :] = jnp.zeros((SEG, LANES), f32)
        q_s[...] = p_ref[:, 0:128]
        k_ext[SEG:, :] = p_ref[:, 128:256]
        v_ext[SEG:, :] = p_ref[:, 256:384]
        k_ext[:SEG, :] = pp_ref[:, 128:256]
        v_ext[:SEG, :] = pp_ref[:, 256:384]
        qw_v, kw_v = qw_ref[...], kw_ref[...]

        def blk_pair(i2, carry):
            dqw, dkw = carry
            done = []
            for u in range(2):
                b = 2 * i2 + u
                j, r = b // d, b % d
                qs = j * (ATT_BLOCK * d) + r
                ks = SEG + qs - ATT_BLOCK * d
                has_prev = (seg > 0) | (j > 0)
                qrows, krows = _rows(qs, ATT_BLOCK, d), _rows(ks, 2 * ATT_BLOCK, d)
                dq, dk, dv, dqw_b, dkw_b = _att_block_bwd(
                    q_s[qrows, :], k_ext[krows, :], v_ext[krows, :], o_ref[qrows, :], l_ref[qrows, :],
                    do_ref[qrows, :], dl_ref[qrows, :], qw_v, kw_v, has_prev)
                dqw, dkw = dqw + dqw_b, dkw + dkw_b
                done.append((qrows, krows, dq, dk, dv))
            for qrows, krows, dq, dk, dv in done:
                dq_s[qrows, :] = dq
                dk_ext[krows, :] = dk_ext[krows, :] + dk
                dv_ext[krows, :] = dv_ext[krows, :] + dv
            return dqw, dkw

        zero = jnp.zeros((1, LANES), f32)
        dqw, dkw = lax.fori_loop(0, nblk // 2, blk_pair, (zero, zero))
        dqw_ref[...] += dqw
        dkw_ref[...] += dkw
        dp_ref[:, 0:128] = dq_s[...].astype(bf16)
        dp_ref[:, 128:256] = dk_ext[SEG:, :].astype(bf16)
        dp_ref[:, 256:384] = dv_ext[SEG:, :].astype(bf16)

    rev = lambda i: nseg - 1 - i
    vec = pl.BlockSpec((1, LANES), lambda hh, i: (0, 0))
    blk = MAIN_ATT_BLOCK + 2 * g
    cur = pl.BlockSpec((SEG, 384), lambda hh, i: (rev(i), blk + hh))
    prev = pl.BlockSpec((SEG, 384), lambda hh, i: (jnp.maximum(rev(i) - 1, 0), blk + hh))
    col = pl.BlockSpec((SEG, LANES), lambda hh, i: (rev(i), hh))
    big = pltpu.VMEM((2 * SEG, LANES), f32)
    one = pltpu.VMEM((SEG, LANES), f32)
    return _call(body, name, (2, nseg), [cur, prev, col, col, col, col, vec, vec, _ANY], [cur, vec, vec],
                 [jax.ShapeDtypeStruct((S, MAIN_WIDTH), bf16), jax.ShapeDtypeStruct((1, LANES), f32),
                  jax.ShapeDtypeStruct((1, LANES), f32)],
                 scratch=[one, big, big, one, big, big], sem=("arbitrary", "arbitrary"),
                 aliases={8: 0})(p_att, p_att, o, lse, do, dlse, qw, kw, dp_main)


def conv_fwd(p_ssd, conv_w, conv_b, name):
    S = p_ssd.shape[0]
    tm, C = CONV_ROWS, SSD_XBC

    def body(x_ref, xp_ref, w_ref, b_ref, o_ref):
        first = (pl.program_id(0) == 0)
        ext = jnp.concatenate([jnp.where(first, 0.0, xp_ref[:, 0:C]), x_ref[:, 0:C]], axis=0)
        acc = b_ref[...] + w_ref[3:4, :] * ext[8:, :]
        for k in range(1, 4):
            acc = acc + w_ref[3 - k:4 - k, :] * pltpu.roll(ext, k, 0)[8:, :]
        o_ref[...] = jax.nn.silu(acc)

    return _call(body, name, (S // tm,),
                 [pl.BlockSpec((tm, 1536), lambda i: (i, MAIN_SSD_BLOCK)),
                  pl.BlockSpec((8, 1536), lambda i: (jnp.maximum(i * (tm // 8) - 1, 0), MAIN_SSD_BLOCK)),
                  pl.BlockSpec((4, C), lambda i: (0, 0)), pl.BlockSpec((1, C), lambda i: (0, 0))],
                 pl.BlockSpec((tm, C), lambda i: (i, 0)), jax.ShapeDtypeStruct((S, C), f32),
                 sem=("parallel",))(p_ssd, p_ssd, conv_w, conv_b)


def conv_bwd(p_ssd, dact, ddt, conv_w, conv_b, dp_main, name):
    S = p_ssd.shape[0]
    tm, C = CONV_ROWS, SSD_XBC
    nblk = S // tm

    def body(x_ref, xp_ref, xn_ref, da_ref, dan_ref, ddt_ref, w_ref, b_ref, _, dp_ref, dw_ref, db_ref):
        i = pl.program_id(0)
        rows = tm + 8
        ext = jnp.concatenate([jnp.where(i == 0, 0.0, xp_ref[:, 0:C]), x_ref[:, 0:C], xn_ref[:, 0:C]], axis=0)
        shifted = [ext[8:, :]] + [pltpu.roll(ext, k, 0)[8:, :] for k in range(1, 4)]
        pre = b_ref[...] + w_ref[3:4, :] * shifted[0]
        for k in range(1, 4):
            pre = pre + w_ref[3 - k:4 - k, :] * shifted[k]
        sg = jax.nn.sigmoid(pre)
        dact = jnp.concatenate([da_ref[...], jnp.where(i == nblk - 1, 0.0, dan_ref[...])], axis=0)
        dpre = dact * (sg * (1.0 + pre * (1.0 - sg)))
        dx = w_ref[3:4, :] * dpre[0:tm, :]
        for k in range(1, 4):
            dx = dx + w_ref[3 - k:4 - k, :] * pltpu.roll(dpre, rows - k, 0)[0:tm, :]
        dp_ref[:, 0:C] = dx.astype(bf16)
        dp_ref[:, C:C + 128] = ddt_ref[...].astype(bf16)
        dp_ref[:, C + 128:] = jnp.zeros((tm, 128), bf16)
        dcur = dpre[0:tm, :]
        dws = [jnp.sum(dcur * shifted[3 - j][0:tm, :], axis=0, keepdims=True) for j in range(4)]
        dbs = jnp.sum(dcur, axis=0, keepdims=True)

        @pl.when(i == 0)
        def _():
            dw_ref[...] = jnp.zeros_like(dw_ref)
            db_ref[...] = jnp.zeros_like(db_ref)

        for j in range(4):
            dw_ref[j:j + 1, :] += dws[j]
        db_ref[...] += dbs

    t8 = tm // 8
    blk = MAIN_SSD_BLOCK
    return _call(body, name, (nblk,),
                 [pl.BlockSpec((tm, 1536), lambda i: (i, blk)),
                  pl.BlockSpec((8, 1536), lambda i: (jnp.maximum(i * t8 - 1, 0), blk)),
                  pl.BlockSpec((8, 1536), lambda i: (jnp.minimum((i + 1) * t8, S // 8 - 1), blk)),
                  pl.BlockSpec((tm, C), lambda i: (i, 0)),
                  pl.BlockSpec((8, C), lambda i: (jnp.minimum((i + 1) * t8, S // 8 - 1), 0)),
                  pl.BlockSpec((tm, 128), lambda i: (i, 0)),
                  pl.BlockSpec((4, C), lambda i: (0, 0)), pl.BlockSpec((1, C), lambda i: (0, 0)), _ANY],
                 [pl.BlockSpec((tm, 1536), lambda i: (i, blk)), pl.BlockSpec((4, C), lambda i: (0, 0)),
                  pl.BlockSpec((1, C), lambda i: (0, 0))],
                 [jax.ShapeDtypeStruct((S, MAIN_WIDTH), bf16), jax.ShapeDtypeStruct((4, C), f32),
                  jax.ShapeDtypeStruct((1, C), f32)],
                 sem=("arbitrary",), aliases={8: 0})(p_ssd, p_ssd, p_ssd, dact, dact, ddt, conv_w, conv_b, dp_main)


def _ssd_chunk(xbc, dtr, state, dt_bias, a_log, d_full):
    T = SSD_CHUNK
    r_i = lax.broadcasted_iota(jnp.int32, (T, T), 0)
    c_i = lax.broadcasted_iota(jnp.int32, (T, T), 1)
    tril = c_i <= r_i
    tri = tril.astype(bf16)
    lane = lax.broadcasted_iota(jnp.int32, (1, LANES), 1)
    hm = [(lane < 64).astype(f32), (lane >= 64).astype(f32)]
    column = lambda v, h: jnp.broadcast_to(v[:, h:h + 1], (T, LANES))

    def per_head_lanes(v):
        return jnp.concatenate([jnp.where(lane < 64, column(v, 2 * pp), column(v, 2 * pp + 1)) for pp in range(6)],
                               axis=1)

    xs, bm, cm = xbc[:, :768], xbc[:, 768:1024], xbc[:, 1024:1280]
    dt = _softplus(dtr + dt_bias)
    a_dt = dt * (-jnp.exp(a_log))
    a_cs = _xdot_l(tri, a_dt)
    dt_full = per_head_lanes(dt)
    acs_full = per_head_lanes(a_cs)
    last = lax.broadcasted_iota(jnp.int32, (T, SSD_WIDTH), 0) == T - 1
    tot_full = jnp.sum(jnp.where(last, acs_full, 0.0), axis=0, keepdims=True)
    xdt = xs * dt_full
    xw = xdt * jnp.exp(tot_full - acs_full)
    eacs = jnp.exp(acs_full)
    st_parts, off_parts, diag_parts = [], [], []
    for g in range(2):
        bg, cg = bm[:, 128 * g:128 * (g + 1)], cm[:, 128 * g:128 * (g + 1)]
        cols = slice(384 * g, 384 * (g + 1))
        st_parts.append(_bdot(bg, xw[:, cols], TN))
        off_parts.append(_bdot(cg, state[:, cols], NN))
        cb = _bdot(cg, bg, NT)
        for pp in range(3 * g, 3 * g + 3):
            xp = xdt[:, 128 * pp:128 * (pp + 1)]
            acc = jnp.zeros((T, LANES), f32)
            for hh in range(2):
                a_col = column(a_cs, 2 * pp + hh)
                decay = jnp.where(tril, jnp.exp(jnp.minimum(a_col - a_col.T, 0.0)), 0.0)
                acc = acc + _bdot(cb * decay, xp * hm[hh], NN)
            diag_parts.append(acc)
    new_state = state * jnp.exp(tot_full) + jnp.concatenate(st_parts, axis=1)
    y = jnp.concatenate(diag_parts, axis=1) + jnp.concatenate(off_parts, axis=1) * eacs + xs * d_full
    return y, new_state


def ssd_fwd(xact, p_ssd, dt_bias, a_log, d_full, name):
    S = xact.shape[0]
    T = SSD_CHUNK

    U = SSD_CHUNKS_PER_STEP

    def body(x_ref, p_ref, b_ref, a_ref, d_ref, y_ref, s_ref, state):
        @pl.when(pl.program_id(0) == 0)
        def _():
            state[...] = jnp.zeros_like(state)

        st = state[...]
        for u in range(U):
            rows = slice(T * u, T * (u + 1))
            s_ref[u] = st
            y, st = _ssd_chunk(x_ref[rows, :], p_ref[rows, :], st, b_ref[...], a_ref[...], d_ref[...])
            y_ref[rows, :] = y
        state[...] = st

    vec = lambda n: pl.BlockSpec((1, n), lambda i: (0, 0))
    return _call(body, name, (S // (U * T),),
                 [pl.BlockSpec((U * T, SSD_XBC), lambda i: (i, 0)),
                  pl.BlockSpec((U * T, 128), lambda i: (i, MAIN_DT_BLOCK)), vec(128), vec(128), vec(768)],
                 [pl.BlockSpec((U * T, 768), lambda i: (i, 0)), pl.BlockSpec((U, T, 768), lambda i: (i, 0, 0))],
                 [jax.ShapeDtypeStruct((S, 768), f32), jax.ShapeDtypeStruct((S // T, T, 768), f32)],
                 scratch=[pltpu.VMEM((T, 768), f32)], sem=("arbitrary",))(xact, p_ssd, dt_bias, a_log, d_full)


def ssd_bwd(xact, p_ssd, states, dy, dt_bias, a_log, d_full, name):
    S = xact.shape[0]
    T = SSD_CHUNK
    U = 1
    nc = S // (U * T)

    def body(x_ref, p_ref, s_ref, dy_ref, b_ref, a_ref, d_ref, dx_ref, ddt_ref, db_ref, da_ref, dd_ref, dstate):
        i = pl.program_id(0)

        @pl.when(i == 0)
        def _():
            for ref in (dstate, db_ref, da_ref, dd_ref):
                ref[...] = jnp.zeros_like(ref)

        dst = dstate[...]
        for u in reversed(range(U)):
            rows = slice(T * u, T * (u + 1))
            _, vjp = jax.vjp(_ssd_chunk, x_ref[rows, :], p_ref[rows, :], s_ref[u], b_ref[...], a_ref[...], d_ref[...])
            dx, ddt, dst, db, da, dd = vjp((dy_ref[rows, :], dst))
            dx_ref[rows, :] = dx
            ddt_ref[rows, :] = ddt
            db_ref[...] += db
            da_ref[...] += da
            dd_ref[...] += dd
        dstate[...] = dst

    rev = lambda i: nc - 1 - i
    vec = lambda n: pl.BlockSpec((1, n), lambda i: (0, 0))
    return _call(body, name, (nc,),
                 [pl.BlockSpec((U * T, SSD_XBC), lambda i: (rev(i), 0)),
                  pl.BlockSpec((U * T, 128), lambda i: (rev(i), MAIN_DT_BLOCK)),
                  pl.BlockSpec((U, T, 768), lambda i: (rev(i), 0, 0)), pl.BlockSpec((U * T, 768), lambda i: (rev(i), 0)),
                  vec(128), vec(128), vec(768)],
                 [pl.BlockSpec((U * T, SSD_XBC), lambda i: (rev(i), 0)), pl.BlockSpec((U * T, 128), lambda i: (rev(i), 0)),
                  vec(128), vec(128), vec(768)],
                 [jax.ShapeDtypeStruct((S, SSD_XBC), f32), jax.ShapeDtypeStruct((S, 128), f32),
                  jax.ShapeDtypeStruct((1, 128), f32), jax.ShapeDtypeStruct((1, 128), f32),
                  jax.ShapeDtypeStruct((1, 768), f32)],
                 scratch=[pltpu.VMEM((T, 768), f32)],
                 sem=("arbitrary",))(xact, p_ssd, states, dy, dt_bias, a_log, d_full)


def _tail_fn(ys5, pt, o0, o1, o2, l0, l1, l2, yssd, glu_b, nw, pr_glu, pr_a, pr_b, pr_c, x, weights):
    glu_w, pa, pb, pc, wo = weights
    gates = jax.nn.sigmoid(pt[:, :3072])
    za, zb, zc = pt[:, 3072:3584], pt[:, 3584:3840], pt[:, 3840:4608]
    g = jax.nn.gelu(ys5)
    ya = g * jax.nn.sigmoid(_cdot(g, glu_w, NN) + glu_b + pr_glu) * jax.nn.silu(za)
    m = jnp.maximum(jnp.maximum(l0, l1), l2)
    e0, e1, e2 = jnp.exp(l0 - m), jnp.exp(l1 - m), jnp.exp(l2 - m)
    yb = (e0 * o0 + e1 * o1 + e2 * o2) / (e0 + e1 + e2) * jax.nn.silu(zb)
    yc = _rms(yssd * jax.nn.silu(zc), nw)
    merged = (gates[:, :1024] * (_cdot(ya, pa, NN) + pr_a) + gates[:, 1024:2048] * (_cdot(yb, pb, NN) + pr_b)
              + gates[:, 2048:] * (_cdot(yc, pc, NN) + pr_c))
    out = x + _cdot(merged, wo, NN)
    return out, (g, ya, yb, yc, merged)


def _tail_specs(tm):
    row = lambda n: pl.BlockSpec((tm, n), lambda i: (i, 0))
    full = lambda a, b: pl.BlockSpec((a, b), lambda i: (0, 0))
    acts = [row(512), row(4608)] + [row(256)] * 6 + [row(768), row(D_MODEL)]
    consts = [full(1, 512), full(1, 768), full(512, 512), full(512, D_MODEL), full(256, D_MODEL),
              full(768, D_MODEL), full(D_MODEL, D_MODEL)]
    return row, full, acts, consts


def tail_fwd(ys5, pt, os_, ls_, yssd, x, glu_b, nw, weights, name, next_norm_w=None, target=None):
    S = x.shape[0]
    tm = TAIL_ROWS
    row, full, acts, consts = _tail_specs(tm)

    def body(ys5_ref, pt_ref, o0, o1, o2, l0, l1, l2, yssd_ref, x_ref, gb_ref, nw_ref, gw, pa, pb, pc, wo, *rest):
        z = lambda n: jnp.zeros((tm, n), f32)
        out, _ = _tail_fn(ys5_ref[...], pt_ref[...], o0[...], o1[...], o2[...], l0[...], l1[...], l2[...],
                          yssd_ref[...], gb_ref[...], nw_ref[...], z(512), z(D_MODEL), z(D_MODEL), z(D_MODEL),
                          x_ref[...], (gw[...], pa[...], pb[...], pc[...], wo[...]))
        if target is not None:
            t_ref, dy_ref, l_ref = rest
            diff = out - t_ref[...]
            dy_ref[...] = diff * (1.0 / D_MODEL)
            part = jnp.full((8, LANES), 0.5 / D_MODEL * jnp.sum(diff * diff), f32)

            @pl.when(pl.program_id(0) == 0)
            def _():
                l_ref[...] = part

            @pl.when(pl.program_id(0) > 0)
            def _():
                l_ref[...] += part
        elif next_norm_w is not None:
            n_ref, out_ref, h_ref = rest
            out_ref[...] = out
            h_ref[...] = _rms(out, n_ref[...]).astype(bf16)
        else:
            rest[0][...] = out

    sd = jax.ShapeDtypeStruct((S, D_MODEL), f32)
    if target is not None:
        extra_in, extra_specs = [target], [row(D_MODEL)]
        out_specs = [row(D_MODEL), pl.BlockSpec((8, LANES), lambda i: (0, 0))]
        out_shape = [sd, jax.ShapeDtypeStruct((8, LANES), f32)]
    elif next_norm_w is not None:
        extra_in, extra_specs = [next_norm_w], [full(1, D_MODEL)]
        out_specs, out_shape = [row(D_MODEL), row(D_MODEL)], [sd, jax.ShapeDtypeStruct((S, D_MODEL), bf16)]
    else:
        extra_in, extra_specs, out_specs, out_shape = [], [], row(D_MODEL), sd
    return _call(body, name, (S // tm,), acts + consts + extra_specs, out_specs, out_shape,
                 sem=("arbitrary",))(ys5, pt, *os_, *ls_, yssd, x, glu_b, nw, *weights, *extra_in)


def tail_bwd(ys5, pt, os_, ls_, yssd, dout, glu_b, nw, weights, name):
    S = dout.shape[0]
    tm = TAIL_ROWS
    row, full, acts, consts = _tail_specs(tm)

    def body(ys5_ref, pt_ref, o0, o1, o2, l0, l1, l2, yssd_ref, dout_ref, gb_ref, nw_ref, gw, pa, pb, pc, wo,
             dys5_ref, dpt_ref, do0, do1, do2, dl0, dl1, dl2, dyssd_ref, dgb_ref, dnw_ref,
             g_ref, ya_ref, yb_ref, yc_ref, mg_ref, dglu_ref, dpa_ref, dpb_ref, dpc_ref):
        z = lambda n: jnp.zeros((tm, n), f32)
        w = (gw[...], pa[...], pb[...], pc[...], wo[...])
        fn = lambda *a: _tail_fn(*a, z(D_MODEL), w)
        _, vjp, aux = jax.vjp(fn, ys5_ref[...], pt_ref[...], o0[...], o1[...], o2[...], l0[...], l1[...], l2[...],
                              yssd_ref[...], gb_ref[...], nw_ref[...], z(512), z(D_MODEL), z(D_MODEL), z(D_MODEL),
                              has_aux=True)
        (dys5, dpt, d0, d1, d2, e0, e1, e2, dyssd, dgb, dnw, dglu, dpa, dpb, dpc) = vjp(dout_ref[...])
        dys5_ref[...] = dys5
        dpt_ref[...] = dpt.astype(bf16)
        for ref, val in ((do0, d0), (do1, d1), (do2, d2), (dl0, e0), (dl1, e1), (dl2, e2)):
            ref[...] = val
        dyssd_ref[...] = dyssd
        g, ya, yb, yc, merged = aux
        for ref, val in ((g_ref, g), (ya_ref, ya), (yb_ref, yb), (yc_ref, yc), (mg_ref, merged),
                         (dglu_ref, dglu), (dpa_ref, dpa), (dpb_ref, dpb), (dpc_ref, dpc)):
            ref[...] = val.astype(bf16)

        @pl.when(pl.program_id(0) == 0)
        def _():
            dgb_ref[...] = dgb
            dnw_ref[...] = dnw

        @pl.when(pl.program_id(0) > 0)
        def _():
            dgb_ref[...] += dgb
            dnw_ref[...] += dnw

    sd = lambda n, dt=f32: jax.ShapeDtypeStruct((S, n), dt)
    out_specs = ([row(512), row(4608)] + [row(256)] * 6 + [row(768), full(1, 512), full(1, 768)]
                 + [row(512), row(512), row(256), row(768), row(D_MODEL), row(512)] + [row(D_MODEL)] * 3)
    out_shape = ([sd(512), sd(MAIN_WIDTH, bf16)] + [sd(256)] * 6 + [sd(768), jax.ShapeDtypeStruct((1, 512), f32),
                                                          jax.ShapeDtypeStruct((1, 768), f32)]
                 + [sd(512, bf16), sd(512, bf16), sd(256, bf16), sd(768, bf16), sd(D_MODEL, bf16), sd(512, bf16)]
                 + [sd(D_MODEL, bf16)] * 3)
    return _call(body, name, (S // tm,), acts + consts, out_specs, out_shape,
                 sem=("arbitrary",))(ys5, pt, *os_, *ls_, yssd, dout, glu_b, nw, *weights)


def _in_proj_segments(shards):
    dtype = shards[0].dtype

    def c(a, b):
        parts = []
        for k, sh in enumerate(shards):
            lo, hi = max(a, W_IN_SHARD * k), min(b, W_IN_SHARD * (k + 1))
            if lo < hi:
                parts.append(sh[:, lo - W_IN_SHARD * k:hi - W_IN_SHARD * k])
        return parts[0] if len(parts) == 1 else jnp.concatenate(parts, axis=1)

    atts = []
    for g in range(3):
        parts = []
        for hh in range(2):
            o = 64 * (4 * g + 2 * hh)
            parts += [c(_C_Q + o, _C_Q + o + 128), c(_C_K + o, _C_K + o + 128), c(_C_V + o, _C_V + o + 128)]
        atts.append(jnp.concatenate(parts, axis=1))
    ssd = jnp.concatenate([c(_C_XBC, _C_ZC), jnp.zeros((D_MODEL, 1536 - (_C_ZC - _C_XBC)), dtype)], axis=1)
    tail = jnp.concatenate([c(_C_GATE, _C_END), c(_C_ZA, _C_Q), c(_C_ZB, _C_XBC), c(_C_ZC, _C_GATE)], axis=1)
    return [c(_C_UA, _C_ZA), jnp.concatenate([tail, ssd] + atts, axis=1)]


def _in_proj_grad(ds5, dmain):
    dtail, dssd = dmain[:, :4608], dmain[:, 4608:6144]
    datts = [dmain[:, 6144 + 768 * g:6144 + 768 * (g + 1)] for g in range(3)]
    pick = lambda off: [datts[g][:, 384 * hh + off:384 * hh + off + 128] for g in range(3) for hh in range(2)]
    pieces = ([ds5, dtail[:, 3072:3584]] + pick(0) + pick(128) + pick(256)
              + [dtail[:, 3584:3840], dssd[:, :_C_ZC - _C_XBC], dtail[:, 3840:4608], dtail[:, :3072]])
    shards, start = [[] for _ in range(4)], 0
    for piece in pieces:
        width = piece.shape[1]
        for k in range(4):
            lo, hi = max(start, W_IN_SHARD * k), min(start + width, W_IN_SHARD * (k + 1))
            if lo < hi:
                shards[k].append(piece[:, lo - start:hi - start])
        start += width
    return jnp.stack([jnp.concatenate(s, axis=1) for s in shards])


def _prep_layer(p):
    q = {}
    q["segs"] = [s.astype(bf16) for s in _in_proj_segments(p["w_in"])]
    disc = _s5_discretize(p["s5_a_re"], p["s5_a_im"], p["s5_log_step"], p["s5_b_re"], p["s5_b_im"],
                          p["s5_c_re"], p["s5_c_im"])
    q["s5"] = disc
    q["pw"] = _lam_powers(disc[0], disc[1])
    q["s5_d"] = p["s5_d"].reshape(1, 512)
    q["qw"] = jnp.tile(p["q_norm_w"], 2).reshape(1, LANES)
    q["kw"] = jnp.tile(p["k_norm_w"], 2).reshape(1, LANES)
    q["conv_w"] = p["conv_w"]
    q["conv_b"] = p["conv_b"].reshape(1, SSD_XBC)
    pad = lambda v: jnp.pad(v, (0, LANES - v.shape[0])).reshape(1, LANES)
    q["dt_bias"], q["a_log"] = pad(p["dt_bias"]), pad(p["ssd_a_log"])
    q["d_full"] = jnp.repeat(p["ssd_d"], 64).reshape(1, SSD_WIDTH)
    q["glu_b"] = p["s5_glu_b"].reshape(1, 512)
    q["nw"] = p["ssd_norm_w"].reshape(1, SSD_WIDTH)
    q["norm_w"] = p["norm_w"].reshape(1, D_MODEL)
    q["tailw"] = tuple(p[n].astype(bf16) for n in ("s5_glu_w", "proj_a", "proj_b", "proj_c", "w_out"))
    return q


_DILATIONS = (1, 4, 16)


def layer_fwd(x, q, tag, h=None, next_norm_w=None, target=None):
    if h is None:
        h = rms_fwd(x, q["norm_w"], f"rms_fwd{tag}")
    p_s5, p_main = [mm_nn(h, w, f"inproj{k}{tag}") for k, w in enumerate(q["segs"])]
    _, _, w_re, w_im, c_re, c_im = q["s5"]
    ys5, h_re, h_im = s5_fwd(p_s5, *q["pw"], w_re, w_im, c_re, c_im, q["s5_d"], f"s5_fwd{tag}")
    os_, ls_ = [], []
    for g, d in enumerate(_DILATIONS):
        o, l = att_fwd(p_main, q["qw"], q["kw"], d, g, f"att_fwd{g}{tag}")
        os_.append(o)
        ls_.append(l)
    xact = conv_fwd(p_main, q["conv_w"], q["conv_b"], f"conv_fwd{tag}")
    yssd, states = ssd_fwd(xact, p_main, q["dt_bias"], q["a_log"], q["d_full"], f"ssd_fwd{tag}")
    out = tail_fwd(ys5, p_main, os_, ls_, yssd, x, q["glu_b"], q["nw"], q["tailw"], f"tail_fwd{tag}",
                   next_norm_w=next_norm_w, target=target)
    saved = dict(x=x, h=h, p_s5=p_s5, p_main=p_main, ys5=ys5, h_re=h_re, h_im=h_im,
                 os=os_, ls=ls_, xact=xact, yssd=yssd, states=states)
    return out, saved


def layer_bwd(dout, sv, q, p, tag):
    S = dout.shape[0]
    (dys5, dp_main, do0, do1, do2, dl0, dl1, dl2, dyssd, dglu_b, dnw, g_b, ya_b, yb_b, yc_b, mg_b, dglu_b16,
     dpa_b, dpb_b, dpc_b) = tail_bwd(sv["ys5"], sv["p_main"], sv["os"], sv["ls"], sv["yssd"], dout, q["glu_b"],
                                     q["nw"], q["tailw"], f"tail_bwd{tag}")
    grads = {}
    grads["s5_glu_w"] = mm_tn(g_b, dglu_b16, f"dglu_w{tag}")
    grads["proj_a"] = mm_tn(ya_b, dpa_b, f"dproj_a{tag}")
    grads["proj_b"] = mm_tn(yb_b, dpb_b, f"dproj_b{tag}")
    grads["proj_c"] = mm_tn(yc_b, dpc_b, f"dproj_c{tag}")
    grads["w_out"] = mm_tn(mg_b, dout, f"dw_out{tag}")
    grads["s5_glu_b"] = dglu_b.reshape(512)
    grads["ssd_norm_w"] = dnw.reshape(SSD_WIDTH)

    dxact, ddt, ddt_bias, da_log, dd_full = ssd_bwd(sv["xact"], sv["p_main"], sv["states"], dyssd, q["dt_bias"],
                                                    q["a_log"], q["d_full"], f"ssd_bwd{tag}")
    dp_main, dconv_w, dconv_b = conv_bwd(sv["p_main"], dxact, ddt, q["conv_w"], q["conv_b"], dp_main,
                                         f"conv_bwd{tag}")
    grads["dt_bias"] = ddt_bias[0, :12]
    grads["ssd_a_log"] = da_log[0, :12]
    grads["ssd_d"] = dd_full.reshape(12, 64).sum(axis=1)
    grads["conv_w"] = dconv_w
    grads["conv_b"] = dconv_b.reshape(SSD_XBC)

    dqw, dkw = 0.0, 0.0
    for g, d in enumerate(_DILATIONS):
        dp_main, a, b = att_bwd(sv["p_main"], sv["os"][g], sv["ls"][g], (do0, do1, do2)[g], (dl0, dl1, dl2)[g],
                                q["qw"], q["kw"], d, g, dp_main, f"att_bwd{g}{tag}")
        dqw, dkw = dqw + a, dkw + b
    grads["q_norm_w"] = dqw.reshape(2, 64).sum(axis=0)
    grads["k_norm_w"] = dkw.reshape(2, 64).sum(axis=0)

    _, _, w_re, w_im, c_re, c_im = q["s5"]
    dp_s5, dwre, dwim, dcre, dcim, dlam_re, dlam_im, dd = s5_bwd(
        dys5, sv["p_s5"], sv["h_re"], sv["h_im"], *q["pw"], w_re, w_im, c_re, c_im, q["s5_d"], f"s5_bwd{tag}")
    s5_names = ("s5_a_re", "s5_a_im", "s5_log_step", "s5_b_re", "s5_b_im", "s5_c_re", "s5_c_im")
    _, disc_vjp = jax.vjp(_s5_discretize, *[p[n] for n in s5_names])
    for n, gr in zip(s5_names, disc_vjp((dlam_re, dlam_im, dwre, dwim, dcre, dcim))):
        grads[n] = gr
    grads["s5_d"] = dd.reshape(512)

    dsegs = [dp_s5, dp_main]
    dws = [mm_tn(sv["h"], ds, f"dw_in{k}{tag}") for k, ds in enumerate(dsegs)]
    grads["w_in"] = _in_proj_grad(*dws)
    dh_main = mm_nt(dp_main, q["segs"][1], f"dh1{tag}")
    dx, dnorm_w = mm_nt_rms_bwd(dp_s5, q["segs"][0], dh_main, sv["x"], q["norm_w"], dout, f"dh0_rms_bwd{tag}")
    grads["norm_w"] = dnorm_w.reshape(D_MODEL)
    return dx, grads


def _exchange(name, scatter=(), gather=(), sibling=(), sibling_both=False):
    scatter, gather, sibling = list(scatter), list(gather), list(sibling)
    chip_xs = scatter + gather
    ns, nc, nb = len(scatter), len(chip_xs), len(sibling)
    n = nc + nb

    def body(*refs):
        x_refs, o_refs, send_sems, recv_sems = refs[:n], refs[n:2 * n], refs[2 * n], refs[2 * n + 1]
        mx, my, mc = lax.axis_index("x"), lax.axis_index("y"), lax.axis_index("c")
        me = 2 * mx + my
        copies = []
        for a in range(nc):
            for t, (px, py) in enumerate(((1 - mx, my), (mx, 1 - my), (1 - mx, 1 - my))):
                src = x_refs[a].at[2 * px + py] if a < ns else x_refs[a]
                copies.append(pltpu.make_async_remote_copy(
                    src_ref=src, dst_ref=o_refs[a].at[me], send_sem=send_sems.at[3 * a + t],
                    recv_sem=recv_sems.at[3 * a + t], device_id=(px, py, mc), device_id_type=pl.DeviceIdType.MESH))
        for b in range(nc, n):
            k = 3 * nc + b - nc
            copies.append(pltpu.make_async_remote_copy(
                src_ref=x_refs[b], dst_ref=o_refs[b].at[mc] if sibling_both else o_refs[b], send_sem=send_sems.at[k],
                recv_sem=recv_sems.at[k], device_id=(mx, my, 1 - mc), device_id_type=pl.DeviceIdType.MESH))
        for cp in copies:
            cp.start()
        for cp in copies:
            cp.wait()

    shapes = ([(4,) + tuple(x.shape[1:]) for x in scatter] + [(4,) + tuple(x.shape) for x in gather]
              + [((2,) if sibling_both else ()) + tuple(x.shape) for x in sibling])
    xs = chip_xs + sibling
    outs = pl.pallas_call(
        body, name=name, in_specs=[_ANY] * n, out_specs=[_ANY] * n,
        out_shape=[jax.ShapeDtypeStruct(s, x.dtype) for s, x in zip(shapes, xs)],
        scratch_shapes=[pltpu.SemaphoreType.DMA((3 * nc + nb,)), pltpu.SemaphoreType.DMA((3 * nc + nb,))],
    )(*xs)
    me, c = 2 * lax.axis_index("x") + lax.axis_index("y"), lax.axis_index("c")
    fixed = []
    for a, (o, x) in enumerate(zip(outs, xs)):
        if a < ns:
            o = lax.dynamic_update_index_in_dim(o, lax.dynamic_index_in_dim(x, me, 0, keepdims=True), me, 0)
        elif a < nc:
            o = lax.dynamic_update_index_in_dim(o, x[None], me, 0)
        elif sibling_both:
            o = lax.dynamic_update_index_in_dim(o, x[None], c, 0)
        fixed.append(o)
    return fixed[:ns], fixed[ns:nc], fixed[nc:]


def _rows_tile(rows, row_bytes, budget=5 << 19):
    return next(t for t in (512, 256, 128, 64, 32, 16, 8) if rows % t == 0 and t * row_bytes <= budget)


def _padded_row_bytes(cols):
    return -(-cols // LANES) * LANES * 4


def _add2(a, b, name, out_dtype=f32):
    R, C = a.shape
    tr = _rows_tile(R, _padded_row_bytes(C))

    def body(a_ref, b_ref, o_ref):
        o_ref[...] = (a_ref[...] + b_ref[...]).astype(out_dtype)

    spec = pl.BlockSpec((tr, C), lambda i: (i, 0))
    return _call(body, name, (R // tr,), [spec, spec], spec, jax.ShapeDtypeStruct((R, C), out_dtype),
                 sem=("parallel",))(a, b)


def _sum4(x, name):
    R = x.shape[1]
    tr = _tile(R, (2560, 1024, 512, 256, 128))

    def body(x_ref, o_ref):
        p = [x_ref[j].astype(f32) for j in range(4)]
        o_ref[...] = ((p[0] + p[1]) + p[2]) + p[3]

    return _call(body, name, (R // tr,), [pl.BlockSpec((4, tr, LANES), lambda i: (0, i, 0))],
                 pl.BlockSpec((tr, LANES), lambda i: (i, 0)), jax.ShapeDtypeStruct((R, LANES), f32),
                 sem=("parallel",))(x)


def _adamw(g_parts, w, m, v, name):
    stacked = not isinstance(g_parts, (tuple, list))
    k = g_parts.shape[0] if stacked else len(g_parts)
    R, C = w.shape
    tr = _rows_tile(R, _padded_row_bytes(C))

    def body(*refs):
        w_ref, m_ref, v_ref, g_ref, d_ref, nm_ref, nv_ref = refs[-7:]
        if stacked:
            g = refs[0][0].astype(f32)
            for j in range(1, k):
                g = g + refs[0][j].astype(f32)
        else:
            g = refs[0][...]
            for r in refs[1:k]:
                g = g + r[...]
        g_ref[...] = g
        d_ref[...], nm_ref[...], nv_ref[...] = _adamw_update(g, w_ref[...], m_ref[...], v_ref[...])

    spec = pl.BlockSpec((tr, C), lambda i: (i, 0))
    sd = jax.ShapeDtypeStruct((R, C), f32)
    g_specs = [pl.BlockSpec((k, tr, C), lambda i: (0, i, 0))] if stacked else [spec] * k
    g_args = [g_parts] if stacked else list(g_parts)
    return _call(body, name, (R // tr,), g_specs + [spec] * 3, [spec] * 4, [sd] * 4,
                 sem=("parallel",))(*g_args, w, m, v)


def _adamw_update(g, w, m, v):
    m = ADAM_B1 * m + (1.0 - ADAM_B1) * g
    v = ADAM_B2 * v + (1.0 - ADAM_B2) * (g * g)
    c1 = 1.0 - ADAM_B1 ** ADAM_STEP
    c2 = 1.0 - ADAM_B2 ** ADAM_STEP
    return -ADAM_LR * ((m / c1) / (jnp.sqrt(v / c2) + ADAM_EPS) + ADAM_WD * w), m, v


def _adamw_small(gs, ws, ms, vs, name):
    n = len(gs)

    def body(*refs):
        ins, outs = refs[:4 * n], refs[4 * n:]
        for t in range(n):
            d, m, v = _adamw_update(ins[t][...], ins[n + t][...], ins[2 * n + t][...], ins[3 * n + t][...])
            outs[t][...] = d
            outs[n + t][...] = m
            outs[2 * n + t][...] = v

    vmem = pl.BlockSpec(memory_space=pltpu.VMEM)
    outs = pl.pallas_call(
        body, name=name, in_specs=[vmem] * (4 * n), out_specs=[vmem] * (3 * n),
        out_shape=[jax.ShapeDtypeStruct(w.shape, f32) for w in ws] * 3,
        compiler_params=pltpu.CompilerParams(vmem_limit_bytes=V7X_VMEM_LIMIT))(*gs, *ws, *ms, *vs)
    return outs[:n], outs[n:2 * n], outs[2 * n:]


def _pack(arrays, row_multiple=PACK_ROWS):
    flat = jnp.concatenate([a.reshape(-1) for a in arrays])
    unit = row_multiple * LANES
    n = -(-flat.shape[0] // unit) * unit
    return jnp.pad(flat, (0, n - flat.shape[0])).reshape(n // LANES, LANES)


def _unpack(buf, shapes):
    flat = buf.reshape(-1)
    out, off = [], 0
    for s in shapes:
        n = 1
        for dim in s:
            n *= dim
        out.append(flat[off:off + n].reshape(s))
        off += n
    return out


def _to_shards(full, axis):
    s = full.shape
    t = full.reshape(s[:axis] + (4, s[axis] // 4) + s[axis + 1:])
    return jnp.moveaxis(t, axis, 0)


def _from_shards(sh, axis):
    t = jnp.moveaxis(sh, 0, axis)
    s = t.shape
    return t.reshape(s[:axis] + (s[axis] * s[axis + 1],) + s[axis + 2:])


def kernel(x, norm_w, w_in, s5_a_re, s5_a_im, s5_log_step, s5_b_re, s5_b_im, s5_c_re, s5_c_im, s5_d, s5_glu_w, s5_glu_b, q_norm_w, k_norm_w, conv_w, conv_b, dt_bias, ssd_a_log, ssd_d, ssd_norm_w, proj_a, proj_b, proj_c, w_out, loss_target, m_norm_w, m_w_in, m_s5_a_re, m_s5_a_im, m_s5_log_step, m_s5_b_re, m_s5_b_im, m_s5_c_re, m_s5_c_im, m_s5_d, m_s5_glu_w, m_s5_glu_b, m_q_norm_w, m_k_norm_w, m_conv_w, m_conv_b, m_dt_bias, m_ssd_a_log, m_ssd_d, m_ssd_norm_w, m_proj_a, m_proj_b, m_proj_c, m_w_out, v_norm_w, v_w_in, v_s5_a_re, v_s5_a_im, v_s5_log_step, v_s5_b_re, v_s5_b_im, v_s5_c_re, v_s5_c_im, v_s5_d, v_s5_glu_w, v_s5_glu_b, v_q_norm_w, v_k_norm_w, v_conv_w, v_conv_b, v_dt_bias, v_ssd_a_log, v_ssd_d, v_ssd_norm_w, v_proj_a, v_proj_b, v_proj_c, v_w_out):
    given = dict(locals())
    W = {n: given[n] for n in _WEIGHTS}
    M = {n: given["m_" + n] for n in _WEIGHTS}
    V = {n: given["v_" + n] for n in _WEIGHTS}
    n_layers = norm_w.shape[0]
    assert n_layers == 2
    c = lax.axis_index("c")

    mine_of = lambda t: lax.dynamic_index_in_dim(t, c, 0, keepdims=False)
    as_payload = lambda n: lax.bitcast_convert_type(W[n], bf16) if n == "conv_w" else W[n].astype(bf16)
    payload_shapes = [W[n].shape + ((2,) if n == "conv_w" else ()) for n, _ in _SHARDED]
    wpack = _pack([as_payload(n) for n, _ in _SHARDED])
    half_rows = wpack.shape[0] // 2
    _, (pack_half, w_in_mine_layer), _ = _exchange(
        "gather_weights", gather=[lax.dynamic_slice_in_dim(wpack, c * half_rows, half_rows),
                                  mine_of(w_in).astype(bf16)])
    _, _, (w_in_layers, pack_halves) = _exchange("share_weights", sibling=[w_in_mine_layer, pack_half],
                                                 sibling_both=True)
    gathered = jnp.moveaxis(pack_halves, 0, 1).reshape(4, 2 * half_rows, LANES)
    full = dict(W)
    pieces = [_unpack(gathered[j], payload_shapes) for j in range(4)]
    for k, (n, axis) in enumerate(_SHARDED):
        sh = jnp.stack([pieces[j][k] for j in range(4)])
        full[n] = _from_shards(lax.bitcast_convert_type(sh, f32) if n == "conv_w" else sh, axis)

    qs, saves = [], []
    for l in range(n_layers):
        p = {n: full[n][l] for n in _WEIGHTS if n != "w_in"}
        p["w_in"] = [w_in_layers[l, k] for k in range(4)]
        qs.append((_prep_layer(p), p))
    (act, h), sv = layer_fwd(x[0], qs[0][0], "_l0", next_norm_w=qs[1][0]["norm_w"])
    saves.append(sv)
    (dact, lsum), sv = layer_fwd(act, qs[1][0], "_l1", h=h, target=loss_target[0])
    saves.append(sv)
    loss = lax.psum(lsum[0, 0], ("x", "y", "c"))
    layer_grads = [None] * n_layers
    for l in reversed(range(n_layers)):
        q, p = qs[l]
        dact, layer_grads[l] = layer_bwd(dact, saves[l], q, p, f"_l{l}")
    grad_x = dact[None]
    G = {n: jnp.stack([layer_grads[l][n] for l in range(n_layers)]) for n in _WEIGHTS if n != "w_in"}

    repl_shapes = [W[n].shape for n in _REPL]
    small = _pack([G[n] for n in _REPL], 4 * PACK_ROWS)
    quarter = small.shape[0] // 4
    big = [_to_shards(G[n], axis).reshape(4, -1) for n, axis in _SHARDED]
    big = jnp.concatenate(big, axis=1)
    unit = PACK_ROWS * LANES
    nbig = -(-big.shape[1] // unit) * unit
    big = jnp.pad(big, ((0, 0), (0, nbig - big.shape[1]))).reshape(4, nbig // LANES, LANES)
    gpack = jnp.concatenate([big, small.reshape(4, quarter, LANES)], axis=1)
    rbig = nbig // LANES
    g0, g1 = layer_grads[0]["w_in"], layer_grads[1]["w_in"]

    (landed_pack,), _, (from_sibling,) = _exchange(
        "swap_w_in_grads_and_scatter_grads", scatter=[gpack.astype(bf16)], sibling=[jnp.where(c == 0, g1, g0)])
    flat = lambda t: t.reshape(4 * D_MODEL, W_IN_SHARD)
    shards = _add2(flat(jnp.where(c == 0, g0, g1)), flat(from_sibling), "sum_cores_w_in", out_dtype=bf16)
    mine = _sum4(landed_pack, "sum_chips")

    (landed,), _, (other,) = _exchange(
        "scatter_w_in_grads_and_swap_cores", scatter=[shards.reshape(4, D_MODEL, W_IN_SHARD)], sibling=[mine])
    w_in_mine = _adamw(landed, mine_of(w_in), mine_of(m_w_in), mine_of(v_w_in), "adamw_w_in")
    gq = _add2(mine[rbig:], other[rbig:], "sum_cores_small")

    _, (gsmall,), w_in_out = _exchange(
        "share_w_in_updates_and_gather_small", gather=[gq], sibling=w_in_mine, sibling_both=True)
    gsmall = gsmall.reshape(4 * quarter, LANES)

    shard_shapes = [W[n].shape for n, _ in _SHARDED]
    g_mine, g_other = _unpack(mine[:rbig], shard_shapes), _unpack(other[:rbig], shard_shapes)
    rows_of = lambda t: t.reshape(-1, t.shape[-1])
    res = [dict(), dict(), dict(), dict()]
    for k, (n, _) in enumerate(_SHARDED):
        outs = _adamw((rows_of(g_mine[k]), rows_of(g_other[k])), rows_of(W[n]), rows_of(M[n]), rows_of(V[n]),
                      f"adamw_{n}")
        for kind in range(4):
            res[kind][n] = outs[kind].reshape(W[n].shape)
    g_small = _unpack(gsmall, repl_shapes)
    small_out = _adamw_small([rows_of(g) for g in g_small], *([rows_of(T[n]) for n in _REPL] for T in (W, M, V)),
                             "adamw_replicated")
    for kind in range(4):
        res[kind]["w_in"] = w_in_out[kind]
        for k, n in enumerate(_REPL):
            res[kind][n] = g_small[k] if kind == 0 else small_out[kind - 1][k].reshape(W[n].shape)
    return (loss, grad_x, *[res[0][n] for n in _WEIGHTS], *[res[1][n] for n in _WEIGHTS],
            *[res[2][n] for n in _WEIGHTS], *[res[3][n] for n in _WEIGHTS])
```

```python
import functools

import jax
import jax.numpy as jnp
from jax import lax
from jax.experimental import pallas as pl
from jax.experimental.pallas import tpu as pltpu

f32 = jnp.float32
bf16 = jnp.bfloat16

D_MODEL = 1024
RMS_EPS = 1e-6
V7X_VMEM_LIMIT = 60 * 1024 * 1024
LANES = 128
NN, NT, TN = ((1,), (0,)), ((1,), (1,)), ((0,), (0,))

S5_STATES = 2048
S5_ROWS = 512
ATT_SEG = 2048
ATT_BLOCK = 128
SSD_CHUNK = 128
SSD_CHUNKS_PER_STEP = 2
SSD_WIDTH = 768
SSD_XBC = 1280
CONV_ROWS = 512
TAIL_ROWS = 256

ADAM_LR, ADAM_B1, ADAM_B2, ADAM_EPS, ADAM_WD, ADAM_STEP = 0.001, 0.9, 0.999, 1e-08, 0.01, 10

_C_UA, _C_ZA, _C_Q, _C_K, _C_V, _C_ZB, _C_XBC, _C_DT, _C_ZC, _C_GATE, _C_END = (
    0, 512, 1024, 1792, 2560, 3328, 3584, 4864, 4876, 5644, 8716)

_SHARDED = (("s5_glu_w", 1), ("conv_w", 2), ("proj_a", 2), ("proj_b", 2), ("proj_c", 2), ("w_out", 1))
W_IN_SHARD = 2179
_REPL = ("norm_w", "s5_a_re", "s5_a_im", "s5_log_step", "s5_b_re", "s5_b_im", "s5_c_re", "s5_c_im", "s5_d",
         "s5_glu_b", "q_norm_w", "k_norm_w", "conv_b", "dt_bias", "ssd_a_log", "ssd_d", "ssd_norm_w")
_WEIGHTS = ("norm_w", "w_in", "s5_a_re", "s5_a_im", "s5_log_step", "s5_b_re", "s5_b_im", "s5_c_re", "s5_c_im",
            "s5_d", "s5_glu_w", "s5_glu_b", "q_norm_w", "k_norm_w", "conv_w", "conv_b", "dt_bias", "ssd_a_log",
            "ssd_d", "ssd_norm_w", "proj_a", "proj_b", "proj_c", "w_out")
PACK_ROWS = 512


def _dot(a, b, dims):
    return lax.dot_general(a.astype(bf16), b.astype(bf16), (dims, ((), ())), preferred_element_type=f32)


_ANY = pl.BlockSpec(memory_space=pl.ANY)

MAIN_WIDTH = 8448
MAIN_SSD_BLOCK = 3
MAIN_DT_BLOCK = 46
MAIN_ATT_BLOCK = 16


def _call(body, name, grid, in_specs, out_specs, out_shape, scratch=(), sem=None, aliases=None):
    return pl.pallas_call(
        body, name=name, grid=grid, in_specs=in_specs, out_specs=out_specs, out_shape=out_shape,
        scratch_shapes=list(scratch), input_output_aliases=aliases or {},
        compiler_params=pltpu.CompilerParams(dimension_semantics=sem, vmem_limit_bytes=V7X_VMEM_LIMIT))


def _tile(n, options=(1024, 768, 512, 384, 256, 128)):
    return next(t for t in options if n % t == 0)


@functools.partial(jax.custom_vjp, nondiff_argnums=(2,))
def _bdot(a, b, dims):
    return _dot(a, b, dims)


def _bdot_fwd(a, b, dims):
    return _dot(a, b, dims), (a, b)


def _bdot_bwd(dims, res, g):
    a, b = res
    if dims == NN:
        da, db = _dot(g, b, NT), _dot(a, g, TN)
    elif dims == NT:
        da, db = _dot(g, b, NN), _dot(g, a, TN)
    else:
        da, db = _dot(b, g, NT), _dot(a, g, NN)
    return da.astype(a.dtype), db.astype(b.dtype)


_bdot.defvjp(_bdot_fwd, _bdot_bwd)


@functools.partial(jax.custom_vjp, nondiff_argnums=(2,))
def _cdot(a, w, dims):
    return _dot(a, w, dims)


def _cdot_fwd(a, w, dims):
    return _dot(a, w, dims), w


def _cdot_bwd(dims, w, g):
    da = _dot(g, w, NT) if dims == NN else _dot(g, w, NN)
    return da, jnp.zeros_like(w)


_cdot.defvjp(_cdot_fwd, _cdot_bwd)


def _split3(x):
    hi = x.astype(bf16)
    r = x - hi.astype(f32)
    mid = r.astype(bf16)
    lo = (r - mid.astype(f32)).astype(bf16)
    return hi, mid, lo


@jax.custom_vjp
def _xdot_l(m, x):
    return sum(_dot(m, p, NN) for p in _split3(x))


def _xdot_l_fwd(m, x):
    return _xdot_l(m, x), m


def _xdot_l_bwd(m, g):
    return jnp.zeros_like(m), sum(_dot(m, p, TN) for p in _split3(g))


_xdot_l.defvjp(_xdot_l_fwd, _xdot_l_bwd)


@jax.custom_vjp
def _softplus(x):
    e = jnp.exp(-jnp.abs(x))
    u = 1.0 + e
    log1p = jnp.where(u == 1.0, e, jnp.log(u) * (e / jnp.where(u == 1.0, 1.0, u - 1.0)))
    return jnp.maximum(x, 0.0) + log1p


def _softplus_fwd(x):
    return _softplus(x), x


def _softplus_bwd(x, g):
    return (g * jax.nn.sigmoid(x),)


_softplus.defvjp(_softplus_fwd, _softplus_bwd)


def _rms(x, w):
    return x * lax.rsqrt(jnp.mean(x * x, axis=-1, keepdims=True) + RMS_EPS) * w


def mm_nn(a, b, name, tm=2048):
    M, K = a.shape
    N = b.shape[1]
    tn = _tile(N)

    def body(a_ref, b_ref, o_ref):
        o_ref[...] = _dot(a_ref[...], b_ref[...], NN)

    return _call(body, name, (M // tm, N // tn),
                 [pl.BlockSpec((tm, K), lambda i, j: (i, 0)), pl.BlockSpec((K, tn), lambda i, j: (0, j))],
                 pl.BlockSpec((tm, tn), lambda i, j: (i, j)), jax.ShapeDtypeStruct((M, N), f32),
                 sem=("parallel", "parallel"))(a, b)


def mm_nt(a, b, name, tm=1024):
    M, K = a.shape
    N = b.shape[0]
    tk = _tile(K, (2816, 1024, 768, 512, 256, 128))

    def body(a_ref, b_ref, o_ref):
        k = pl.program_id(1)
        p = _dot(a_ref[...], b_ref[...], NT)

        @pl.when(k == 0)
        def _():
            o_ref[...] = p

        @pl.when(k > 0)
        def _():
            o_ref[...] += p

    return _call(body, name, (M // tm, K // tk),
                 [pl.BlockSpec((tm, tk), lambda i, k: (i, k)), pl.BlockSpec((N, tk), lambda i, k: (0, k))],
                 pl.BlockSpec((tm, N), lambda i, k: (i, 0)), jax.ShapeDtypeStruct((M, N), f32),
                 sem=("parallel", "arbitrary"))(a, b)


def mm_tn(a, b, name, tk=2048):
    K, M = a.shape
    N = b.shape[1]
    tn = _tile(N)

    def body(a_ref, b_ref, o_ref):
        k = pl.program_id(1)
        p = _dot(a_ref[...], b_ref[...], TN)

        @pl.when(k == 0)
        def _():
            o_ref[...] = p

        @pl.when(k > 0)
        def _():
            o_ref[...] += p

    return _call(body, name, (N // tn, K // tk),
                 [pl.BlockSpec((tk, M), lambda j, k: (k, 0)), pl.BlockSpec((tk, tn), lambda j, k: (k, j))],
                 pl.BlockSpec((M, tn), lambda j, k: (0, j)), jax.ShapeDtypeStruct((M, N), f32),
                 sem=("parallel", "arbitrary"))(a, b)


def rms_fwd(x, w, name, tm=512):
    S = x.shape[0]

    def body(x_ref, w_ref, o_ref):
        o_ref[...] = _rms(x_ref[...], w_ref[...]).astype(bf16)

    return _call(body, name, (S // tm,),
                 [pl.BlockSpec((tm, D_MODEL), lambda i: (i, 0)), pl.BlockSpec((1, D_MODEL), lambda i: (0, 0))],
                 pl.BlockSpec((tm, D_MODEL), lambda i: (i, 0)), jax.ShapeDtypeStruct((S, D_MODEL), bf16),
                 sem=("parallel",))(x, w)


def mm_nt_rms_bwd(a, b, acc, x, w, dres, name, tm=1024):
    S, K = a.shape

    def body(a_ref, b_ref, acc_ref, x_ref, w_ref, dr_ref, dx_ref, dw_ref):
        dh = _dot(a_ref[...], b_ref[...], NT) + acc_ref[...]
        _, vjp = jax.vjp(_rms, x_ref[...], w_ref[...])
        dx, dw = vjp(dh)
        dx_ref[...] = dx + dr_ref[...]

        @pl.when(pl.program_id(0) == 0)
        def _():
            dw_ref[...] = dw

        @pl.when(pl.program_id(0) > 0)
        def _():
            dw_ref[...] += dw

    row = pl.BlockSpec((tm, D_MODEL), lambda i: (i, 0))
    vec = pl.BlockSpec((1, D_MODEL), lambda i: (0, 0))
    return _call(body, name, (S // tm,),
                 [pl.BlockSpec((tm, K), lambda i: (i, 0)), pl.BlockSpec((D_MODEL, K), lambda i: (0, 0)), row, row, vec,
                  row], [row, vec],
                 [jax.ShapeDtypeStruct((S, D_MODEL), f32), jax.ShapeDtypeStruct((1, D_MODEL), f32)],
                 sem=("arbitrary",))(a, b, acc, x, w, dres)


def _s5_discretize(a_re, a_im, log_step, b_re, b_im, c_re, c_im):
    step = jnp.exp(log_step)[:, None]
    mag = jnp.exp(a_re * step)
    ang = a_im * step
    lam_re, lam_im = mag * jnp.cos(ang), mag * jnp.sin(ang)
    num_re, num_im = lam_re - 1.0, lam_im
    den = a_re * a_re + a_im * a_im
    f_re = (num_re * a_re + num_im * a_im) / den
    f_im = (num_im * a_re - num_re * a_im) / den
    bb_re = f_re[..., None] * b_re - f_im[..., None] * b_im
    bb_im = f_re[..., None] * b_im + f_im[..., None] * b_re
    eye = jnp.eye(8, dtype=f32)

    def block_in(bb):
        t = bb.transpose(0, 2, 1).reshape(4, 8, 16, 1, 64)
        return (t * eye[None, :, None, :, None]).reshape(4, 128, 512)

    def block_out(c):
        t = c.transpose(0, 2, 1).reshape(4, 8, 64, 1, 16)
        return (t * eye[None, :, None, :, None]).reshape(4, 512, 128)

    return (lam_re.reshape(1, S5_STATES), lam_im.reshape(1, S5_STATES), block_in(bb_re), block_in(bb_im),
            block_out(c_re), block_out(c_im))


def _lam_powers(lam_re, lam_im):
    rows_re, rows_im = [lam_re], [lam_im]
    for _ in range(7):
        pr, pi = rows_re[-1], rows_im[-1]
        rows_re.append(pr * lam_re - pi * lam_im)
        rows_im.append(pr * lam_im + pi * lam_re)
    return jnp.concatenate(rows_re, 0), jnp.concatenate(rows_im, 0)


def s5_fwd(u, pw_re, pw_im, w_re, w_im, c_re, c_im, dvec, name):
    S = u.shape[0]
    R, NS = S5_ROWS, S5_STATES
    nb = R // 8

    def body(u_ref, pwr_ref, pwi_ref, wre_ref, wim_ref, cre_ref, cim_ref, d_ref, y_ref, hr_ref, hi_ref,
             car_re, car_im, cin_re, cin_im, up, yp):
        @pl.when(pl.program_id(0) == 0)
        def _():
            car_re[...] = jnp.zeros_like(car_re)
            car_im[...] = jnp.zeros_like(car_im)

        slab = lambda r: pl.ds(r * nb, nb)
        for r in range(8):
            up[slab(r), :] = u_ref[:, r, :]
        u = up[...]
        for j in range(4):
            uj = u[:, 128 * j:128 * (j + 1)]
            hr_ref[:, 512 * j:512 * (j + 1)] = _dot(uj, wre_ref[j], NN)
            hi_ref[:, 512 * j:512 * (j + 1)] = _dot(uj, wim_ref[j], NN)
        lr, li = pwr_ref[0:1, :], pwi_ref[0:1, :]
        for r in range(1, 8):
            pr, pi = hr_ref[slab(r - 1), :], hi_ref[slab(r - 1), :]
            hr_ref[slab(r), :] = lr * pr - li * pi + hr_ref[slab(r), :]
            hi_ref[slab(r), :] = lr * pi + li * pr + hi_ref[slab(r), :]
        l8r, l8i = pwr_ref[7:8, :], pwi_ref[7:8, :]

        def across(c, carry):
            gr, gi = carry
            cin_re[pl.ds(c, 1), :] = gr
            cin_im[pl.ds(c, 1), :] = gi
            er, ei = hr_ref[pl.ds(7 * nb + c, 1), :], hi_ref[pl.ds(7 * nb + c, 1), :]
            return l8r * gr - l8i * gi + er, l8r * gi + l8i * gr + ei

        gr, gi = lax.fori_loop(0, nb, across, (car_re[...], car_im[...]))
        car_re[...] = gr
        car_im[...] = gi
        cr, ci = cin_re[...], cin_im[...]
        for r in range(8):
            pr, pi = pwr_ref[r:r + 1, :], pwi_ref[r:r + 1, :]
            hr_ref[slab(r), :] = hr_ref[slab(r), :] + pr * cr - pi * ci
            hi_ref[slab(r), :] = hi_ref[slab(r), :] + pr * ci + pi * cr
        for j in range(4):
            sl = slice(512 * j, 512 * (j + 1))
            cs = slice(128 * j, 128 * (j + 1))
            yp[:, cs] = (_dot(hr_ref[:, sl], cre_ref[j], NN) - _dot(hi_ref[:, sl], cim_ref[j], NN)
                         + d_ref[:, cs] * u[:, cs])
        for r in range(8):
            y_ref[:, r, :] = yp[slab(r), :]

    full = lambda shape: pl.BlockSpec(shape, lambda i: (0,) * len(shape))
    hspec = pl.BlockSpec((R, NS), lambda i: (i, 0))
    uspec = pl.BlockSpec((nb, 8, 512), lambda i: (i, 0, 0))
    y, h_re, h_im = _call(
        body, name, (S // R,),
        [uspec, full((8, NS)), full((8, NS)), full((4, 128, 512)),
         full((4, 128, 512)), full((4, 512, 128)), full((4, 512, 128)), full((1, 512))],
        [uspec, hspec, hspec],
        [jax.ShapeDtypeStruct((S // 8, 8, 512), f32), jax.ShapeDtypeStruct((S, NS), f32),
         jax.ShapeDtypeStruct((S, NS), f32)],
        scratch=[pltpu.VMEM((1, NS), f32), pltpu.VMEM((1, NS), f32), pltpu.VMEM((nb, NS), f32),
                 pltpu.VMEM((nb, NS), f32), pltpu.VMEM((R, 512), f32), pltpu.VMEM((R, 512), f32)],
        sem=("arbitrary",))(u.reshape(S // 8, 8, 512), pw_re, pw_im, w_re.astype(bf16), w_im.astype(bf16),
                            c_re.astype(bf16), c_im.astype(bf16), dvec)
    return y.reshape(S, 512), h_re, h_im


def s5_bwd(dy, u, h_re, h_im, pw_re, pw_im, w_re, w_im, c_re, c_im, dvec, name):
    S = u.shape[0]
    R, NS = S5_ROWS, S5_STATES
    nb = R // 8
    nchunk = S // R

    def body(dy_ref, u_ref, hr_ref, hi_ref, hpr_ref, hpi_ref, pwr_ref, pwi_ref, wre_ref, wim_ref, cre_ref, cim_ref,
             d_ref, du_ref, dwre_ref, dwim_ref, dcre_ref, dcim_ref, dlr_ref, dli_ref, dd_ref,
             ar, ai, car_re, car_im, cin_re, cin_im, up, dyp, dup):
        i = pl.program_id(0)

        @pl.when(i == 0)
        def _():
            for ref in (car_re, car_im, dwre_ref, dwim_ref, dcre_ref, dcim_ref, dlr_ref, dli_ref, dd_ref):
                ref[...] = jnp.zeros_like(ref)

        slab = lambda r: pl.ds(r * nb, nb)
        for r in range(8):
            up[slab(r), :] = u_ref[:, r, :]
            dyp[slab(r), :] = dy_ref[:, r, :]
        dy = dyp[...]
        u = up[...]
        for j in range(4):
            dyj = dy[:, 128 * j:128 * (j + 1)]
            ar[:, 512 * j:512 * (j + 1)] = _dot(dyj, cre_ref[j], NT)
            ai[:, 512 * j:512 * (j + 1)] = -_dot(dyj, cim_ref[j], NT)
        lr, li = pwr_ref[0:1, :], pwi_ref[0:1, :]
        for r in range(6, -1, -1):
            nr, ni = ar[slab(r + 1), :], ai[slab(r + 1), :]
            ar[slab(r), :] = lr * nr + li * ni + ar[slab(r), :]
            ai[slab(r), :] = lr * ni - li * nr + ai[slab(r), :]
        l8r, l8i = pwr_ref[7:8, :], pwi_ref[7:8, :]

        def across(k, carry):
            c = nb - 1 - k
            gr, gi = carry
            cin_re[pl.ds(c, 1), :] = gr
            cin_im[pl.ds(c, 1), :] = gi
            er, ei = ar[pl.ds(c, 1), :], ai[pl.ds(c, 1), :]
            return l8r * gr + l8i * gi + er, l8r * gi - l8i * gr + ei

        gr, gi = lax.fori_loop(0, nb, across, (car_re[...], car_im[...]))
        car_re[...] = gr
        car_im[...] = gi
        cr, ci = cin_re[...], cin_im[...]
        for r in range(8):
            pr, pi = pwr_ref[7 - r:8 - r, :], pwi_ref[7 - r:8 - r, :]
            ar[slab(r), :] = ar[slab(r), :] + pr * cr + pi * ci
            ai[slab(r), :] = ai[slab(r), :] + pr * ci - pi * cr

        acc_r = jnp.zeros((1, NS), f32)
        acc_i = jnp.zeros((1, NS), f32)
        has_prev = (i < nchunk - 1).astype(f32)
        top = lax.broadcasted_iota(jnp.int32, (nb, NS), 0) == 0
        for r in range(8):
            if r == 0:
                xr = jnp.where(top, hpr_ref[7:8, :] * has_prev, pltpu.roll(hr_ref[slab(7), :], 1, 0))
                xi = jnp.where(top, hpi_ref[7:8, :] * has_prev, pltpu.roll(hi_ref[slab(7), :], 1, 0))
            else:
                xr, xi = hr_ref[slab(r - 1), :], hi_ref[slab(r - 1), :]
            br, bi = ar[slab(r), :], ai[slab(r), :]
            acc_r += jnp.sum(br * xr + bi * xi, axis=0, keepdims=True)
            acc_i += jnp.sum(bi * xr - br * xi, axis=0, keepdims=True)
        dlr_ref[...] += acc_r
        dli_ref[...] += acc_i
        dd_ref[...] += jnp.sum(dy * u, axis=0, keepdims=True)

        for j in range(4):
            sl = slice(512 * j, 512 * (j + 1))
            cs = slice(128 * j, 128 * (j + 1))
            arj, aij = ar[:, sl], ai[:, sl]
            uj, dyj = u[:, cs], dy[:, cs]
            dup[:, cs] = _dot(arj, wre_ref[j], NT) + _dot(aij, wim_ref[j], NT) + d_ref[:, cs] * dyj
            dwre_ref[j] += _dot(uj, arj, TN)
            dwim_ref[j] += _dot(uj, aij, TN)
            dcre_ref[j] += _dot(hr_ref[:, sl], dyj, TN)
            dcim_ref[j] -= _dot(hi_ref[:, sl], dyj, TN)
        for r in range(8):
            du_ref[:, r, :] = dup[slab(r), :]

    rev = lambda i: nchunk - 1 - i
    full = lambda shape: pl.BlockSpec(shape, lambda i: (0,) * len(shape))
    row = pl.BlockSpec((nb, 8, 512), lambda i: (rev(i), 0, 0))
    hspec = pl.BlockSpec((R, NS), lambda i: (rev(i), 0))
    hprev = pl.BlockSpec((8, NS), lambda i: (jnp.maximum(rev(i) * nb - 1, 0), 0))
    outs = _call(
        body, name, (nchunk,),
        [row, row, hspec, hspec, hprev, hprev, full((8, NS)), full((8, NS)), full((4, 128, 512)), full((4, 128, 512)),
         full((4, 512, 128)), full((4, 512, 128)), full((1, 512))],
        [row, full((4, 128, 512)), full((4, 128, 512)), full((4, 512, 128)), full((4, 512, 128)),
         full((1, NS)), full((1, NS)), full((1, 512))],
        [jax.ShapeDtypeStruct((S // 8, 8, 512), f32), jax.ShapeDtypeStruct((4, 128, 512), f32),
         jax.ShapeDtypeStruct((4, 128, 512), f32), jax.ShapeDtypeStruct((4, 512, 128), f32),
         jax.ShapeDtypeStruct((4, 512, 128), f32), jax.ShapeDtypeStruct((1, NS), f32),
         jax.ShapeDtypeStruct((1, NS), f32), jax.ShapeDtypeStruct((1, 512), f32)],
        scratch=[pltpu.VMEM((R, NS), f32), pltpu.VMEM((R, NS), f32), pltpu.VMEM((1, NS), f32),
                 pltpu.VMEM((1, NS), f32), pltpu.VMEM((nb, NS), f32), pltpu.VMEM((nb, NS), f32),
                 pltpu.VMEM((R, 512), f32), pltpu.VMEM((R, 512), f32), pltpu.VMEM((R, 512), f32)],
        sem=("arbitrary",))(dy.reshape(S // 8, 8, 512), u.reshape(S // 8, 8, 512), h_re, h_im, h_re, h_im, pw_re,
                            pw_im, w_re.astype(bf16), w_im.astype(bf16), c_re.astype(bf16), c_im.astype(bf16), dvec)
    return (outs[0].reshape(S, 512),) + tuple(outs[1:])


def _rows(start, n, d):
    return pl.ds(pl.multiple_of(start, ATT_BLOCK), n) if d == 1 else pl.ds(start, n, stride=d)


def _head_masks():
    lane = lax.broadcasted_iota(jnp.int32, (1, LANES), 1)
    return [(lane < 64).astype(f32), (lane >= 64).astype(f32)]


def _head_norm(x, w, hm):
    x2 = x * x
    r = [lax.rsqrt(jnp.sum(x2 * hm[h], axis=-1, keepdims=True) * (1.0 / 64) + RMS_EPS) for h in range(2)]
    sc = hm[0] * r[0] + hm[1] * r[1]
    return x * sc * w, sc, r


def _head_norm_bwd(x, w, sc, r, dxn, hm):
    dw = jnp.sum(dxn * x * sc, axis=0, keepdims=True)
    t = dxn * w
    tx = t * x
    corr = sum(hm[h] * (r[h] * r[h] * r[h]) * jnp.sum(tx * hm[h], axis=-1, keepdims=True) for h in range(2))
    return t * sc - x * corr * (1.0 / 64), dw


def _att_mask(has_prev):
    qi = lax.broadcasted_iota(jnp.int32, (ATT_BLOCK, 2 * ATT_BLOCK), 0) + ATT_BLOCK
    kj = lax.broadcasted_iota(jnp.int32, (ATT_BLOCK, 2 * ATT_BLOCK), 1)
    return (qi - kj >= 0) & (qi - kj <= ATT_BLOCK) & (has_prev | (kj >= ATT_BLOCK))


def _att_block_bwd(q, k, v, o, lse, do, dlse, qw, kw, has_prev):
    hm = _head_masks()
    mask = _att_mask(has_prev)
    qn, qsc, qr = _head_norm(q, qw, hm)
    kn, ksc, kr = _head_norm(k, kw, hm)
    dqn = jnp.zeros((ATT_BLOCK, LANES), f32)
    dkn = jnp.zeros((2 * ATT_BLOCK, LANES), f32)
    dv = jnp.zeros((2 * ATT_BLOCK, LANES), f32)
    for h in range(2):
        qh, do_h = qn * hm[h], do * hm[h]
        s = _dot(qh, kn, NT) * 0.125
        p = jnp.exp(jnp.where(mask, s - lse[:, 64 * h:64 * h + 1], -jnp.inf))
        dp = _dot(do_h, v, NT)
        delta = jnp.sum(do_h * o, axis=-1, keepdims=True)
        dl = jnp.sum(dlse * hm[h], axis=-1, keepdims=True)
        ds = p * (dp - delta + dl) * 0.125
        dqn = dqn + hm[h] * _dot(ds, kn, NN)
        dkn = dkn + _dot(ds, qh, TN)
        dv = dv + _dot(p, do_h, TN)
    dq, dqw = _head_norm_bwd(q, qw, qsc, qr, dqn, hm)
    dk, dkw = _head_norm_bwd(k, kw, ksc, kr, dkn, hm)
    return dq, dk, dv, dqw, dkw


def _att_block(q, k, v, qw, kw, has_prev):
    hm = _head_masks()
    qn, kn = _head_norm(q, qw, hm)[0], _head_norm(k, kw, hm)[0]
    mask = _att_mask(has_prev)
    o = jnp.zeros((ATT_BLOCK, LANES), f32)
    lse = jnp.zeros((ATT_BLOCK, LANES), f32)
    for h in range(2):
        s = _bdot(qn * hm[h], kn, NT) * 0.125
        s = jnp.where(mask, s, -jnp.inf)
        m = jnp.max(s, axis=-1, keepdims=True)
        p = jnp.exp(s - m)
        l = jnp.sum(p, axis=-1, keepdims=True)
        o = o + hm[h] * _bdot(p / l, v, NN)
        lse = lse + hm[h] * (m + jnp.log(l))
    return o, lse


def att_fwd(p_att, qw, kw, d, g, name):
    S = p_att.shape[0]
    SEG = ATT_SEG
    nblk = SEG // ATT_BLOCK

    def body(p_ref, qw_ref, kw_ref, o_ref, l_ref, q_s, k_ext, v_ext, o_s, l_s):
        seg = pl.program_id(1)

        @pl.when(seg == 0)
        def _():
            k_ext[SEG:, :] = jnp.zeros((SEG, LANES), f32)
            v_ext[SEG:, :] = jnp.zeros((SEG, LANES), f32)

        k_ext[:SEG, :] = k_ext[SEG:, :]
        v_ext[:SEG, :] = v_ext[SEG:, :]
        q_s[...] = p_ref[:, 0:128]
        k_ext[SEG:, :] = p_ref[:, 128:256]
        v_ext[SEG:, :] = p_ref[:, 256:384]
        qw_v, kw_v = qw_ref[...], kw_ref[...]

        def blk(b, carry):
            j, r = b // d, b % d
            qs = j * (ATT_BLOCK * d) + r
            ks = SEG + qs - ATT_BLOCK * d
            o, lse = _att_block(q_s[_rows(qs, ATT_BLOCK, d), :], k_ext[_rows(ks, 2 * ATT_BLOCK, d), :],
                                v_ext[_rows(ks, 2 * ATT_BLOCK, d), :], qw_v, kw_v, (seg > 0) | (j > 0))
            o_s[_rows(qs, ATT_BLOCK, d), :] = o
            l_s[_rows(qs, ATT_BLOCK, d), :] = lse
            return carry

        lax.fori_loop(0, nblk, blk, 0, unroll=4)
        o_ref[...] = o_s[...]
        l_ref[...] = l_s[...]

    vec = pl.BlockSpec((1, LANES), lambda hh, s: (0, 0))
    out = pl.BlockSpec((SEG, LANES), lambda hh, s: (s, hh))
    return _call(body, name, (2, S // SEG), [pl.BlockSpec((SEG, 384), lambda hh, s: (s, MAIN_ATT_BLOCK + 2 * g + hh)), vec, vec],
                 [out, out], [jax.ShapeDtypeStruct((S, 256), f32), jax.ShapeDtypeStruct((S, 256), f32)],
                 scratch=[pltpu.VMEM((SEG, LANES), f32), pltpu.VMEM((2 * SEG, LANES), f32),
                          pltpu.VMEM((2 * SEG, LANES), f32), pltpu.VMEM((SEG, LANES), f32),
                          pltpu.VMEM((SEG, LANES), f32)],
                 sem=("arbitrary", "arbitrary"))(p_att, qw, kw)


def att_bwd(p_att, o, lse, do, dlse, qw, kw, d, g, dp_main, name):
    S = p_att.shape[0]
    SEG = ATT_SEG
    nseg = S // SEG
    nblk = SEG // ATT_BLOCK

    def body(p_ref, pp_ref, o_ref, l_ref, do_ref, dl_ref, qw_ref, kw_ref, _, dp_ref, dqw_ref, dkw_ref,
             q_s, k_ext, v_ext, dq_s, dk_ext, dv_ext):
        hh, i = pl.program_id(0), pl.program_id(1)
        seg = nseg - 1 - i

        @pl.when(i == 0)
        def _():
            dk_ext[...] = jnp.zeros_like(dk_ext)
            dv_ext[...] = jnp.zeros_like(dv_ext)

        @pl.when((i == 0) & (hh == 0))
        def _():
            dqw_ref[...] = jnp.zeros_like(dqw_ref)
            dkw_ref[...] = jnp.zeros_like(dkw_ref)

        dk_ext[SEG:, :] = dk_ext[:SEG, :]
        dv_ext[SEG:, :] = dv_ext[:SEG, :]
        dk_ext[:SEG, :] = jnp.zeros((SEG, LANES), f32)
        dv_ext[:SEG, :] = jnp.zeros((SEG, LANES), f32)
        q_s[...] = p_ref[:, 0:128]
        k_ext[SEG:, :] = p_ref[:, 128:256]
        v_ext[SEG:, :] = p_ref[:, 256:384]
        k_ext[:SEG, :] = pp_ref[:, 128:256]
        v_ext[:SEG, :] = pp_ref[:, 256:384]
        qw_v, kw_v = qw_ref[...], kw_ref[...]

        def blk_pair(i2, carry):
            dqw, dkw = carry
            done = []
            for u in range(2):
                b = 2 * i2 + u
                j, r = b // d, b % d
                qs = j * (ATT_BLOCK * d) + r
                ks = SEG + qs - ATT_BLOCK * d
                has_prev = (seg > 0) | (j > 0)
                qrows, krows = _rows(qs, ATT_BLOCK, d), _rows(ks, 2 * ATT_BLOCK, d)
                dq, dk, dv, dqw_b, dkw_b = _att_block_bwd(
                    q_s[qrows, :], k_ext[krows, :], v_ext[krows, :], o_ref[qrows, :], l_ref[qrows, :],
                    do_ref[qrows, :], dl_ref[qrows, :], qw_v, kw_v, has_prev)
                dqw, dkw = dqw + dqw_b, dkw + dkw_b
                done.append((qrows, krows, dq, dk, dv))
            for qrows, krows, dq, dk, dv in done:
                dq_s[qrows, :] = dq
                dk_ext[krows, :] = dk_ext[krows, :] + dk
                dv_ext[krows, :] = dv_ext[krows, :] + dv
            return dqw, dkw

        zero = jnp.zeros((1, LANES), f32)
        dqw, dkw = lax.fori_loop(0, nblk // 2, blk_pair, (zero, zero))
        dqw_ref[...] += dqw
        dkw_ref[...] += dkw
        dp_ref[:, 0:128] = dq_s[...].astype(bf16)
        dp_ref[:, 128:256] = dk_ext[SEG:, :].astype(bf16)
        dp_ref[:, 256:384] = dv_ext[SEG:, :].astype(bf16)

    rev = lambda i: nseg - 1 - i
    vec = pl.BlockSpec((1, LANES), lambda hh, i: (0, 0))
    blk = MAIN_ATT_BLOCK + 2 * g
    cur = pl.BlockSpec((SEG, 384), lambda hh, i: (rev(i), blk + hh))
    prev = pl.BlockSpec((SEG, 384), lambda hh, i: (jnp.maximum(rev(i) - 1, 0), blk + hh))
    col = pl.BlockSpec((SEG, LANES), lambda hh, i: (rev(i), hh))
    big = pltpu.VMEM((2 * SEG, LANES), f32)
    one = pltpu.VMEM((SEG, LANES), f32)
    return _call(body, name, (2, nseg), [cur, prev, col, col, col, col, vec, vec, _ANY], [cur, vec, vec],
                 [jax.ShapeDtypeStruct((S, MAIN_WIDTH), bf16), jax.ShapeDtypeStruct((1, LANES), f32),
                  jax.ShapeDtypeStruct((1, LANES), f32)],
                 scratch=[one, big, big, one, big, big], sem=("arbitrary", "arbitrary"),
                 aliases={8: 0})(p_att, p_att, o, lse, do, dlse, qw, kw, dp_main)


def conv_fwd(p_ssd, conv_w, conv_b, name):
    S = p_ssd.shape[0]
    tm, C = CONV_ROWS, SSD_XBC

    def body(x_ref, xp_ref, w_ref, b_ref, o_ref):
        first = (pl.program_id(0) == 0)
        ext = jnp.concatenate([jnp.where(first, 0.0, xp_ref[:, 0:C]), x_ref[:, 0:C]], axis=0)
        acc = b_ref[...] + w_ref[3:4, :] * ext[8:, :]
        for k in range(1, 4):
            acc = acc + w_ref[3 - k:4 - k, :] * pltpu.roll(ext, k, 0)[8:, :]
        o_ref[...] = jax.nn.silu(acc)

    return _call(body, name, (S // tm,),
                 [pl.BlockSpec((tm, 1536), lambda i: (i, MAIN_SSD_BLOCK)),
                  pl.BlockSpec((8, 1536), lambda i: (jnp.maximum(i * (tm // 8) - 1, 0), MAIN_SSD_BLOCK)),
                  pl.BlockSpec((4, C), lambda i: (0, 0)), pl.BlockSpec((1, C), lambda i: (0, 0))],
                 pl.BlockSpec((tm, C), lambda i: (i, 0)), jax.ShapeDtypeStruct((S, C), f32),
                 sem=("parallel",))(p_ssd, p_ssd, conv_w, conv_b)


def conv_bwd(p_ssd, dact, ddt, conv_w, conv_b, dp_main, name):
    S = p_ssd.shape[0]
    tm, C = CONV_ROWS, SSD_XBC
    nblk = S // tm

    def body(x_ref, xp_ref, xn_ref, da_ref, dan_ref, ddt_ref, w_ref, b_ref, _, dp_ref, dw_ref, db_ref):
        i = pl.program_id(0)
        rows = tm + 8
        ext = jnp.concatenate([jnp.where(i == 0, 0.0, xp_ref[:, 0:C]), x_ref[:, 0:C], xn_ref[:, 0:C]], axis=0)
        shifted = [ext[8:, :]] + [pltpu.roll(ext, k, 0)[8:, :] for k in range(1, 4)]
        pre = b_ref[...] + w_ref[3:4, :] * shifted[0]
        for k in range(1, 4):
            pre = pre + w_ref[3 - k:4 - k, :] * shifted[k]
        sg = jax.nn.sigmoid(pre)
        dact = jnp.concatenate([da_ref[...], jnp.where(i == nblk - 1, 0.0, dan_ref[...])], axis=0)
        dpre = dact * (sg * (1.0 + pre * (1.0 - sg)))
        dx = w_ref[3:4, :] * dpre[0:tm, :]
        for k in range(1, 4):
            dx = dx + w_ref[3 - k:4 - k, :] * pltpu.roll(dpre, rows - k, 0)[0:tm, :]
        dp_ref[:, 0:C] = dx.astype(bf16)
        dp_ref[:, C:C + 128] = ddt_ref[...].astype(bf16)
        dp_ref[:, C + 128:] = jnp.zeros((tm, 128), bf16)
        dcur = dpre[0:tm, :]
        dws = [jnp.sum(dcur * shifted[3 - j][0:tm, :], axis=0, keepdims=True) for j in range(4)]
        dbs = jnp.sum(dcur, axis=0, keepdims=True)

        @pl.when(i == 0)
        def _():
            dw_ref[...] = jnp.zeros_like(dw_ref)
            db_ref[...] = jnp.zeros_like(db_ref)

        for j in range(4):
            dw_ref[j:j + 1, :] += dws[j]
        db_ref[...] += dbs

    t8 = tm // 8
    blk = MAIN_SSD_BLOCK
    return _call(body, name, (nblk,),
                 [pl.BlockSpec((tm, 1536), lambda i: (i, blk)),
                  pl.BlockSpec((8, 1536), lambda i: (jnp.maximum(i * t8 - 1, 0), blk)),
                  pl.BlockSpec((8, 1536), lambda i: (jnp.minimum((i + 1) * t8, S // 8 - 1), blk)),
                  pl.BlockSpec((tm, C), lambda i: (i, 0)),
                  pl.BlockSpec((8, C), lambda i: (jnp.minimum((i + 1) * t8, S // 8 - 1), 0)),
                  pl.BlockSpec((tm, 128), lambda i: (i, 0)),
                  pl.BlockSpec((4, C), lambda i: (0, 0)), pl.BlockSpec((1, C), lambda i: (0, 0)), _ANY],
                 [pl.BlockSpec((tm, 1536), lambda i: (i, blk)), pl.BlockSpec((4, C), lambda i: (0, 0)),
                  pl.BlockSpec((1, C), lambda i: (0, 0))],
                 [jax.ShapeDtypeStruct((S, MAIN_WIDTH), bf16), jax.ShapeDtypeStruct((4, C), f32),
                  jax.ShapeDtypeStruct((1, C), f32)],
                 sem=("arbitrary",), aliases={8: 0})(p_ssd, p_ssd, p_ssd, dact, dact, ddt, conv_w, conv_b, dp_main)


def _ssd_chunk(xbc, dtr, state, dt_bias, a_log, d_full):
    T = SSD_CHUNK
    r_i = lax.broadcasted_iota(jnp.int32, (T, T), 0)
    c_i = lax.broadcasted_iota(jnp.int32, (T, T), 1)
    tril = c_i <= r_i
    tri = tril.astype(bf16)
    lane = lax.broadcasted_iota(jnp.int32, (1, LANES), 1)
    hm = [(lane < 64).astype(f32), (lane >= 64).astype(f32)]
    column = lambda v, h: jnp.broadcast_to(v[:, h:h + 1], (T, LANES))

    def per_head_lanes(v):
        return jnp.concatenate([jnp.where(lane < 64, column(v, 2 * pp), column(v, 2 * pp + 1)) for pp in range(6)],
                               axis=1)

    xs, bm, cm = xbc[:, :768], xbc[:, 768:1024], xbc[:, 1024:1280]
    dt = _softplus(dtr + dt_bias)
    a_dt = dt * (-jnp.exp(a_log))
    a_cs = _xdot_l(tri, a_dt)
    dt_full = per_head_lanes(dt)
    acs_full = per_head_lanes(a_cs)
    last = lax.broadcasted_iota(jnp.int32, (T, SSD_WIDTH), 0) == T - 1
    tot_full = jnp.sum(jnp.where(last, acs_full, 0.0), axis=0, keepdims=True)
    xdt = xs * dt_full
    xw = xdt * jnp.exp(tot_full - acs_full)
    eacs = jnp.exp(acs_full)
    st_parts, off_parts, diag_parts = [], [], []
    for g in range(2):
        bg, cg = bm[:, 128 * g:128 * (g + 1)], cm[:, 128 * g:128 * (g + 1)]
        cols = slice(384 * g, 384 * (g + 1))
        st_parts.append(_bdot(bg, xw[:, cols], TN))
        off_parts.append(_bdot(cg, state[:, cols], NN))
        cb = _bdot(cg, bg, NT)
        for pp in range(3 * g, 3 * g + 3):
            xp = xdt[:, 128 * pp:128 * (pp + 1)]
            acc = jnp.zeros((T, LANES), f32)
            for hh in range(2):
                a_col = column(a_cs, 2 * pp + hh)
                decay = jnp.where(tril, jnp.exp(jnp.minimum(a_col - a_col.T, 0.0)), 0.0)
                acc = acc + _bdot(cb * decay, xp * hm[hh], NN)
            diag_parts.append(acc)
    new_state = state * jnp.exp(tot_full) + jnp.concatenate(st_parts, axis=1)
    y = jnp.concatenate(diag_parts, axis=1) + jnp.concatenate(off_parts, axis=1) * eacs + xs * d_full
    return y, new_state


def ssd_fwd(xact, p_ssd, dt_bias, a_log, d_full, name):
    S = xact.shape[0]
    T = SSD_CHUNK

    U = SSD_CHUNKS_PER_STEP

    def body(x_ref, p_ref, b_ref, a_ref, d_ref, y_ref, s_ref, state):
        @pl.when(pl.program_id(0) == 0)
        def _():
            state[...] = jnp.zeros_like(state)

        st = state[...]
        for u in range(U):
            rows = slice(T * u, T * (u + 1))
            s_ref[u] = st
            y, st = _ssd_chunk(x_ref[rows, :], p_ref[rows, :], st, b_ref[...], a_ref[...], d_ref[...])
            y_ref[rows, :] = y
        state[...] = st

    vec = lambda n: pl.BlockSpec((1, n), lambda i: (0, 0))
    return _call(body, name, (S // (U * T),),
                 [pl.BlockSpec((U * T, SSD_XBC), lambda i: (i, 0)),
                  pl.BlockSpec((U * T, 128), lambda i: (i, MAIN_DT_BLOCK)), vec(128), vec(128), vec(768)],
                 [pl.BlockSpec((U * T, 768), lambda i: (i, 0)), pl.BlockSpec((U, T, 768), lambda i: (i, 0, 0))],
                 [jax.ShapeDtypeStruct((S, 768), f32), jax.ShapeDtypeStruct((S // T, T, 768), f32)],
                 scratch=[pltpu.VMEM((T, 768), f32)], sem=("arbitrary",))(xact, p_ssd, dt_bias, a_log, d_full)


def ssd_bwd(xact, p_ssd, states, dy, dt_bias, a_log, d_full, name):
    S = xact.shape[0]
    T = SSD_CHUNK
    U = 1
    nc = S // (U * T)

    def body(x_ref, p_ref, s_ref, dy_ref, b_ref, a_ref, d_ref, dx_ref, ddt_ref, db_ref, da_ref, dd_ref, dstate):
        i = pl.program_id(0)

        @pl.when(i == 0)
        def _():
            for ref in (dstate, db_ref, da_ref, dd_ref):
                ref[...] = jnp.zeros_like(ref)

        dst = dstate[...]
        for u in reversed(range(U)):
            rows = slice(T * u, T * (u + 1))
            _, vjp = jax.vjp(_ssd_chunk, x_ref[rows, :], p_ref[rows, :], s_ref[u], b_ref[...], a_ref[...], d_ref[...])
            dx, ddt, dst, db, da, dd = vjp((dy_ref[rows, :], dst))
            dx_ref[rows, :] = dx
            ddt_ref[rows, :] = ddt
            db_ref[...] += db
            da_ref[...] += da
            dd_ref[...] += dd
        dstate[...] = dst

    rev = lambda i: nc - 1 - i
    vec = lambda n: pl.BlockSpec((1, n), lambda i: (0, 0))
    return _call(body, name, (nc,),
                 [pl.BlockSpec((U * T, SSD_XBC), lambda i: (rev(i), 0)),
                  pl.BlockSpec((U * T, 128), lambda i: (rev(i), MAIN_DT_BLOCK)),
                  pl.BlockSpec((U, T, 768), lambda i: (rev(i), 0, 0)), pl.BlockSpec((U * T, 768), lambda i: (rev(i), 0)),
                  vec(128), vec(128), vec(768)],
                 [pl.BlockSpec((U * T, SSD_XBC), lambda i: (rev(i), 0)), pl.BlockSpec((U * T, 128), lambda i: (rev(i), 0)),
                  vec(128), vec(128), vec(768)],
                 [jax.ShapeDtypeStruct((S, SSD_XBC), f32), jax.ShapeDtypeStruct((S, 128), f32),
                  jax.ShapeDtypeStruct((1, 128), f32), jax.ShapeDtypeStruct((1, 128), f32),
                  jax.ShapeDtypeStruct((1, 768), f32)],
                 scratch=[pltpu.VMEM((T, 768), f32)],
                 sem=("arbitrary",))(xact, p_ssd, states, dy, dt_bias, a_log, d_full)


def _tail_fn(ys5, pt, o0, o1, o2, l0, l1, l2, yssd, glu_b, nw, pr_glu, pr_a, pr_b, pr_c, x, weights):
    glu_w, pa, pb, pc, wo = weights
    gates = jax.nn.sigmoid(pt[:, :3072])
    za, zb, zc = pt[:, 3072:3584], pt[:, 3584:3840], pt[:, 3840:4608]
    g = jax.nn.gelu(ys5)
    ya = g * jax.nn.sigmoid(_cdot(g, glu_w, NN) + glu_b + pr_glu) * jax.nn.silu(za)
    m = jnp.maximum(jnp.maximum(l0, l1), l2)
    e0, e1, e2 = jnp.exp(l0 - m), jnp.exp(l1 - m), jnp.exp(l2 - m)
    yb = (e0 * o0 + e1 * o1 + e2 * o2) / (e0 + e1 + e2) * jax.nn.silu(zb)
    yc = _rms(yssd * jax.nn.silu(zc), nw)
    merged = (gates[:, :1024] * (_cdot(ya, pa, NN) + pr_a) + gates[:, 1024:2048] * (_cdot(yb, pb, NN) + pr_b)
              + gates[:, 2048:] * (_cdot(yc, pc, NN) + pr_c))
    out = x + _cdot(merged, wo, NN)
    return out, (g, ya, yb, yc, merged)


def _tail_specs(tm):
    row = lambda n: pl.BlockSpec((tm, n), lambda i: (i, 0))
    full = lambda a, b: pl.BlockSpec((a, b), lambda i: (0, 0))
    acts = [row(512), row(4608)] + [row(256)] * 6 + [row(768), row(D_MODEL)]
    consts = [full(1, 512), full(1, 768), full(512, 512), full(512, D_MODEL), full(256, D_MODEL),
              full(768, D_MODEL), full(D_MODEL, D_MODEL)]
    return row, full, acts, consts


def tail_fwd(ys5, pt, os_, ls_, yssd, x, glu_b, nw, weights, name, next_norm_w=None, target=None):
    S = x.shape[0]
    tm = TAIL_ROWS
    row, full, acts, consts = _tail_specs(tm)

    def body(ys5_ref, pt_ref, o0, o1, o2, l0, l1, l2, yssd_ref, x_ref, gb_ref, nw_ref, gw, pa, pb, pc, wo, *rest):
        z = lambda n: jnp.zeros((tm, n), f32)
        out, _ = _tail_fn(ys5_ref[...], pt_ref[...], o0[...], o1[...], o2[...], l0[...], l1[...], l2[...],
                          yssd_ref[...], gb_ref[...], nw_ref[...], z(512), z(D_MODEL), z(D_MODEL), z(D_MODEL),
                          x_ref[...], (gw[...], pa[...], pb[...], pc[...], wo[...]))
        if target is not None:
            t_ref, dy_ref, l_ref = rest
            diff = out - t_ref[...]
            dy_ref[...] = diff * (1.0 / D_MODEL)
            part = jnp.full((8, LANES), 0.5 / D_MODEL * jnp.sum(diff * diff), f32)

            @pl.when(pl.program_id(0) == 0)
            def _():
                l_ref[...] = part

            @pl.when(pl.program_id(0) > 0)
            def _():
                l_ref[...] += part
        elif next_norm_w is not None:
            n_ref, out_ref, h_ref = rest
            out_ref[...] = out
            h_ref[...] = _rms(out, n_ref[...]).astype(bf16)
        else:
            rest[0][...] = out

    sd = jax.ShapeDtypeStruct((S, D_MODEL), f32)
    if target is not None:
        extra_in, extra_specs = [target], [row(D_MODEL)]
        out_specs = [row(D_MODEL), pl.BlockSpec((8, LANES), lambda i: (0, 0))]
        out_shape = [sd, jax.ShapeDtypeStruct((8, LANES), f32)]
    elif next_norm_w is not None:
        extra_in, extra_specs = [next_norm_w], [full(1, D_MODEL)]
        out_specs, out_shape = [row(D_MODEL), row(D_MODEL)], [sd, jax.ShapeDtypeStruct((S, D_MODEL), bf16)]
    else:
        extra_in, extra_specs, out_specs, out_shape = [], [], row(D_MODEL), sd
    return _call(body, name, (S // tm,), acts + consts + extra_specs, out_specs, out_shape,
                 sem=("arbitrary",))(ys5, pt, *os_, *ls_, yssd, x, glu_b, nw, *weights, *extra_in)


def tail_bwd(ys5, pt, os_, ls_, yssd, dout, glu_b, nw, weights, name):
    S = dout.shape[0]
    tm = TAIL_ROWS
    row, full, acts, consts = _tail_specs(tm)

    def body(ys5_ref, pt_ref, o0, o1, o2, l0, l1, l2, yssd_ref, dout_ref, gb_ref, nw_ref, gw, pa, pb, pc, wo,
             dys5_ref, dpt_ref, do0, do1, do2, dl0, dl1, dl2, dyssd_ref, dgb_ref, dnw_ref,
             g_ref, ya_ref, yb_ref, yc_ref, mg_ref, dglu_ref, dpa_ref, dpb_ref, dpc_ref):
        z = lambda n: jnp.zeros((tm, n), f32)
        w = (gw[...], pa[...], pb[...], pc[...], wo[...])
        fn = lambda *a: _tail_fn(*a, z(D_MODEL), w)
        _, vjp, aux = jax.vjp(fn, ys5_ref[...], pt_ref[...], o0[...], o1[...], o2[...], l0[...], l1[...], l2[...],
                              yssd_ref[...], gb_ref[...], nw_ref[...], z(512), z(D_MODEL), z(D_MODEL), z(D_MODEL),
                              has_aux=True)
        (dys5, dpt, d0, d1, d2, e0, e1, e2, dyssd, dgb, dnw, dglu, dpa, dpb, dpc) = vjp(dout_ref[...])
        dys5_ref[...] = dys5
        dpt_ref[...] = dpt.astype(bf16)
        for ref, val in ((do0, d0), (do1, d1), (do2, d2), (dl0, e0), (dl1, e1), (dl2, e2)):
            ref[...] = val
        dyssd_ref[...] = dyssd
        g, ya, yb, yc, merged = aux
        for ref, val in ((g_ref, g), (ya_ref, ya), (yb_ref, yb), (yc_ref, yc), (mg_ref, merged),
                         (dglu_ref, dglu), (dpa_ref, dpa), (dpb_ref, dpb), (dpc_ref, dpc)):
            ref[...] = val.astype(bf16)

        @pl.when(pl.program_id(0) == 0)
        def _():
            dgb_ref[...] = dgb
            dnw_ref[...] = dnw

        @pl.when(pl.program_id(0) > 0)
        def _():
            dgb_ref[...] += dgb
            dnw_ref[...] += dnw

    sd = lambda n, dt=f32: jax.ShapeDtypeStruct((S, n), dt)
    out_specs = ([row(512), row(4608)] + [row(256)] * 6 + [row(768), full(1, 512), full(1, 768)]
                 + [row(512), row(512), row(256), row(768), row(D_MODEL), row(512)] + [row(D_MODEL)] * 3)
    out_shape = ([sd(512), sd(MAIN_WIDTH, bf16)] + [sd(256)] * 6 + [sd(768), jax.ShapeDtypeStruct((1, 512), f32),
                                                          jax.ShapeDtypeStruct((1, 768), f32)]
                 + [sd(512, bf16), sd(512, bf16), sd(256, bf16), sd(768, bf16), sd(D_MODEL, bf16), sd(512, bf16)]
                 + [sd(D_MODEL, bf16)] * 3)
    return _call(body, name, (S // tm,), acts + consts, out_specs, out_shape,
                 sem=("arbitrary",))(ys5, pt, *os_, *ls_, yssd, dout, glu_b, nw, *weights)


def _in_proj_segments(shards):
    dtype = shards[0].dtype

    def c(a, b):
        parts = []
        for k, sh in enumerate(shards):
            lo, hi = max(a, W_IN_SHARD * k), min(b, W_IN_SHARD * (k + 1))
            if lo < hi:
                parts.append(sh[:, lo - W_IN_SHARD * k:hi - W_IN_SHARD * k])
        return parts[0] if len(parts) == 1 else jnp.concatenate(parts, axis=1)

    atts = []
    for g in range(3):
        parts = []
        for hh in range(2):
            o = 64 * (4 * g + 2 * hh)
            parts += [c(_C_Q + o, _C_Q + o + 128), c(_C_K + o, _C_K + o + 128), c(_C_V + o, _C_V + o + 128)]
        atts.append(jnp.concatenate(parts, axis=1))
    ssd = jnp.concatenate([c(_C_XBC, _C_ZC), jnp.zeros((D_MODEL, 1536 - (_C_ZC - _C_XBC)), dtype)], axis=1)
    tail = jnp.concatenate([c(_C_GATE, _C_END), c(_C_ZA, _C_Q), c(_C_ZB, _C_XBC), c(_C_ZC, _C_GATE)], axis=1)
    return [c(_C_UA, _C_ZA), jnp.concatenate([tail, ssd] + atts, axis=1)]


def _in_proj_grad(ds5, dmain):
    dtail, dssd = dmain[:, :4608], dmain[:, 4608:6144]
    datts = [dmain[:, 6144 + 768 * g:6144 + 768 * (g + 1)] for g in range(3)]
    pick = lambda off: [datts[g][:, 384 * hh + off:384 * hh + off + 128] for g in range(3) for hh in range(2)]
    pieces = ([ds5, dtail[:, 3072:3584]] + pick(0) + pick(128) + pick(256)
              + [dtail[:, 3584:3840], dssd[:, :_C_ZC - _C_XBC], dtail[:, 3840:4608], dtail[:, :3072]])
    shards, start = [[] for _ in range(4)], 0
    for piece in pieces:
        width = piece.shape[1]
        for k in range(4):
            lo, hi = max(start, W_IN_SHARD * k), min(start + width, W_IN_SHARD * (k + 1))
            if lo < hi:
                shards[k].append(piece[:, lo - start:hi - start])
        start += width
    return jnp.stack([jnp.concatenate(s, axis=1) for s in shards])


def _prep_layer(p):
    q = {}
    q["segs"] = [s.astype(bf16) for s in _in_proj_segments(p["w_in"])]
    disc = _s5_discretize(p["s5_a_re"], p["s5_a_im"], p["s5_log_step"], p["s5_b_re"], p["s5_b_im"],
                          p["s5_c_re"], p["s5_c_im"])
    q["s5"] = disc
    q["pw"] = _lam_powers(disc[0], disc[1])
    q["s5_d"] = p["s5_d"].reshape(1, 512)
    q["qw"] = jnp.tile(p["q_norm_w"], 2).reshape(1, LANES)
    q["kw"] = jnp.tile(p["k_norm_w"], 2).reshape(1, LANES)
    q["conv_w"] = p["conv_w"]
    q["conv_b"] = p["conv_b"].reshape(1, SSD_XBC)
    pad = lambda v: jnp.pad(v, (0, LANES - v.shape[0])).reshape(1, LANES)
    q["dt_bias"], q["a_log"] = pad(p["dt_bias"]), pad(p["ssd_a_log"])
    q["d_full"] = jnp.repeat(p["ssd_d"], 64).reshape(1, SSD_WIDTH)
    q["glu_b"] = p["s5_glu_b"].reshape(1, 512)
    q["nw"] = p["ssd_norm_w"].reshape(1, SSD_WIDTH)
    q["norm_w"] = p["norm_w"].reshape(1, D_MODEL)
    q["tailw"] = tuple(p[n].astype(bf16) for n in ("s5_glu_w", "proj_a", "proj_b", "proj_c", "w_out"))
    return q


_DILATIONS = (1, 4, 16)


def layer_fwd(x, q, tag, h=None, next_norm_w=None, target=None):
    if h is None:
        h = rms_fwd(x, q["norm_w"], f"rms_fwd{tag}")
    p_s5, p_main = [mm_nn(h, w, f"inproj{k}{tag}") for k, w in enumerate(q["segs"])]
    _, _, w_re, w_im, c_re, c_im = q["s5"]
    ys5, h_re, h_im = s5_fwd(p_s5, *q["pw"], w_re, w_im, c_re, c_im, q["s5_d"], f"s5_fwd{tag}")
    os_, ls_ = [], []
    for g, d in enumerate(_DILATIONS):
        o, l = att_fwd(p_main, q["qw"], q["kw"], d, g, f"att_fwd{g}{tag}")
        os_.append(o)
        ls_.append(l)
    xact = conv_fwd(p_main, q["conv_w"], q["conv_b"], f"conv_fwd{tag}")
    yssd, states = ssd_fwd(xact, p_main, q["dt_bias"], q["a_log"], q["d_full"], f"ssd_fwd{tag}")
    out = tail_fwd(ys5, p_main, os_, ls_, yssd, x, q["glu_b"], q["nw"], q["tailw"], f"tail_fwd{tag}",
                   next_norm_w=next_norm_w, target=target)
    saved = dict(x=x, h=h, p_s5=p_s5, p_main=p_main, ys5=ys5, h_re=h_re, h_im=h_im,
                 os=os_, ls=ls_, xact=xact, yssd=yssd, states=states)
    return out, saved


def layer_bwd(dout, sv, q, p, tag):
    (dys5, dp_main, do0, do1, do2, dl0, dl1, dl2, dyssd, dglu_b, dnw, g_b, ya_b, yb_b, yc_b, mg_b, dglu_b16,
     dpa_b, dpb_b, dpc_b) = tail_bwd(sv["ys5"], sv["p_main"], sv["os"], sv["ls"], sv["yssd"], dout, q["glu_b"],
                                     q["nw"], q["tailw"], f"tail_bwd{tag}")
    grads = {}
    grads["s5_glu_w"] = mm_tn(g_b, dglu_b16, f"dglu_w{tag}")
    grads["proj_a"] = mm_tn(ya_b, dpa_b, f"dproj_a{tag}")
    grads["proj_b"] = mm_tn(yb_b, dpb_b, f"dproj_b{tag}")
    grads["proj_c"] = mm_tn(yc_b, dpc_b, f"dproj_c{tag}")
    grads["w_out"] = mm_tn(mg_b, dout, f"dw_out{tag}")
    grads["s5_glu_b"] = dglu_b.reshape(512)
    grads["ssd_norm_w"] = dnw.reshape(SSD_WIDTH)

    dxact, ddt, ddt_bias, da_log, dd_full = ssd_bwd(sv["xact"], sv["p_main"], sv["states"], dyssd, q["dt_bias"],
                                                    q["a_log"], q["d_full"], f"ssd_bwd{tag}")
    dp_main, dconv_w, dconv_b = conv_bwd(sv["p_main"], dxact, ddt, q["conv_w"], q["conv_b"], dp_main,
                                         f"conv_bwd{tag}")
    grads["dt_bias"] = ddt_bias[0, :12]
    grads["ssd_a_log"] = da_log[0, :12]
    grads["ssd_d"] = dd_full.reshape(12, 64).sum(axis=1)
    grads["conv_w"] = dconv_w
    grads["conv_b"] = dconv_b.reshape(SSD_XBC)

    dqw, dkw = 0.0, 0.0
    for g, d in enumerate(_DILATIONS):
        dp_main, a, b = att_bwd(sv["p_main"], sv["os"][g], sv["ls"][g], (do0, do1, do2)[g], (dl0, dl1, dl2)[g],
                                q["qw"], q["kw"], d, g, dp_main, f"att_bwd{g}{tag}")
        dqw, dkw = dqw + a, dkw + b
    grads["q_norm_w"] = dqw.reshape(2, 64).sum(axis=0)
    grads["k_norm_w"] = dkw.reshape(2, 64).sum(axis=0)

    _, _, w_re, w_im, c_re, c_im = q["s5"]
    dp_s5, dwre, dwim, dcre, dcim, dlam_re, dlam_im, dd = s5_bwd(
        dys5, sv["p_s5"], sv["h_re"], sv["h_im"], *q["pw"], w_re, w_im, c_re, c_im, q["s5_d"], f"s5_bwd{tag}")
    s5_names = ("s5_a_re", "s5_a_im", "s5_log_step", "s5_b_re", "s5_b_im", "s5_c_re", "s5_c_im")
    _, disc_vjp = jax.vjp(_s5_discretize, *[p[n] for n in s5_names])
    for n, gr in zip(s5_names, disc_vjp((dlam_re, dlam_im, dwre, dwim, dcre, dcim))):
        grads[n] = gr
    grads["s5_d"] = dd.reshape(512)

    dsegs = [dp_s5, dp_main]
    dws = [mm_tn(sv["h"], ds, f"dw_in{k}{tag}") for k, ds in enumerate(dsegs)]
    grads["w_in"] = _in_proj_grad(*dws)
    dh_main = mm_nt(dp_main, q["segs"][1], f"dh1{tag}")
    dx, dnorm_w = mm_nt_rms_bwd(dp_s5, q["segs"][0], dh_main, sv["x"], q["norm_w"], dout, f"dh0_rms_bwd{tag}")
    grads["norm_w"] = dnorm_w.reshape(D_MODEL)
    return dx, grads


def _exchange(name, scatter=(), gather=(), sibling=(), sibling_both=False):
    scatter, gather, sibling = list(scatter), list(gather), list(sibling)
    chip_xs = scatter + gather
    ns, nc, nb = len(scatter), len(chip_xs), len(sibling)
    n = nc + nb

    def body(*refs):
        x_refs, o_refs, send_sems, recv_sems = refs[:n], refs[n:2 * n], refs[2 * n], refs[2 * n + 1]
        mx, my, mc = lax.axis_index("x"), lax.axis_index("y"), lax.axis_index("c")
        me = 2 * mx + my
        copies = []
        for a in range(nc):
            for t, (px, py) in enumerate(((1 - mx, my), (mx, 1 - my), (1 - mx, 1 - my))):
                src = x_refs[a].at[2 * px + py] if a < ns else x_refs[a]
                copies.append(pltpu.make_async_remote_copy(
                    src_ref=src, dst_ref=o_refs[a].at[me], send_sem=send_sems.at[3 * a + t],
                    recv_sem=recv_sems.at[3 * a + t], device_id=(px, py, mc), device_id_type=pl.DeviceIdType.MESH))
        for b in range(nc, n):
            k = 3 * nc + b - nc
            copies.append(pltpu.make_async_remote_copy(
                src_ref=x_refs[b], dst_ref=o_refs[b].at[mc] if sibling_both else o_refs[b], send_sem=send_sems.at[k],
                recv_sem=recv_sems.at[k], device_id=(mx, my, 1 - mc), device_id_type=pl.DeviceIdType.MESH))
        for cp in copies:
            cp.start()
        for cp in copies:
            cp.wait()

    shapes = ([(4,) + tuple(x.shape[1:]) for x in scatter] + [(4,) + tuple(x.shape) for x in gather]
              + [((2,) if sibling_both else ()) + tuple(x.shape) for x in sibling])
    xs = chip_xs + sibling
    outs = pl.pallas_call(
        body, name=name, in_specs=[_ANY] * n, out_specs=[_ANY] * n,
        out_shape=[jax.ShapeDtypeStruct(s, x.dtype) for s, x in zip(shapes, xs)],
        scratch_shapes=[pltpu.SemaphoreType.DMA((3 * nc + nb,)), pltpu.SemaphoreType.DMA((3 * nc + nb,))],
    )(*xs)
    me, c = 2 * lax.axis_index("x") + lax.axis_index("y"), lax.axis_index("c")
    fixed = []
    for a, (o, x) in enumerate(zip(outs, xs)):
        if a < ns:
            o = lax.dynamic_update_index_in_dim(o, lax.dynamic_index_in_dim(x, me, 0, keepdims=True), me, 0)
        elif a < nc:
            o = lax.dynamic_update_index_in_dim(o, x[None], me, 0)
        elif sibling_both:
            o = lax.dynamic_update_index_in_dim(o, x[None], c, 0)
        fixed.append(o)
    return fixed[:ns], fixed[ns:nc], fixed[nc:]


def _rows_tile(rows, row_bytes, budget=5 << 19):
    return next(t for t in (512, 256, 128, 64, 32, 16, 8) if rows % t == 0 and t * row_bytes <= budget)


def _padded_row_bytes(cols):
    return -(-cols // LANES) * LANES * 4


def _add2(a, b, name, out_dtype=f32):
    R, C = a.shape
    tr = _rows_tile(R, _padded_row_bytes(C))

    def body(a_ref, b_ref, o_ref):
        o_ref[...] = (a_ref[...] + b_ref[...]).astype(out_dtype)

    spec = pl.BlockSpec((tr, C), lambda i: (i, 0))
    return _call(body, name, (R // tr,), [spec, spec], spec, jax.ShapeDtypeStruct((R, C), out_dtype),
                 sem=("parallel",))(a, b)


def _sum4(x, name):
    R = x.shape[1]
    tr = _tile(R, (2560, 1024, 512, 256, 128))

    def body(x_ref, o_ref):
        p = [x_ref[j].astype(f32) for j in range(4)]
        o_ref[...] = ((p[0] + p[1]) + p[2]) + p[3]

    return _call(body, name, (R // tr,), [pl.BlockSpec((4, tr, LANES), lambda i: (0, i, 0))],
                 pl.BlockSpec((tr, LANES), lambda i: (i, 0)), jax.ShapeDtypeStruct((R, LANES), f32),
                 sem=("parallel",))(x)


def _adamw(g_parts, w, m, v, name):
    stacked = not isinstance(g_parts, (tuple, list))
    k = g_parts.shape[0] if stacked else len(g_parts)
    R, C = w.shape
    tr = _rows_tile(R, _padded_row_bytes(C))

    def body(*refs):
        w_ref, m_ref, v_ref, g_ref, d_ref, nm_ref, nv_ref = refs[-7:]
        if stacked:
            g = refs[0][0].astype(f32)
            for j in range(1, k):
                g = g + refs[0][j].astype(f32)
        else:
            g = refs[0][...]
            for r in refs[1:k]:
                g = g + r[...]
        g_ref[...] = g
        d_ref[...], nm_ref[...], nv_ref[...] = _adamw_update(g, w_ref[...], m_ref[...], v_ref[...])

    spec = pl.BlockSpec((tr, C), lambda i: (i, 0))
    sd = jax.ShapeDtypeStruct((R, C), f32)
    g_specs = [pl.BlockSpec((k, tr, C), lambda i: (0, i, 0))] if stacked else [spec] * k
    g_args = [g_parts] if stacked else list(g_parts)
    return _call(body, name, (R // tr,), g_specs + [spec] * 3, [spec] * 4, [sd] * 4,
                 sem=("parallel",))(*g_args, w, m, v)


def _adamw_update(g, w, m, v):
    m = ADAM_B1 * m + (1.0 - ADAM_B1) * g
    v = ADAM_B2 * v + (1.0 - ADAM_B2) * (g * g)
    c1 = 1.0 - ADAM_B1 ** ADAM_STEP
    c2 = 1.0 - ADAM_B2 ** ADAM_STEP
    return -ADAM_LR * ((m / c1) / (jnp.sqrt(v / c2) + ADAM_EPS) + ADAM_WD * w), m, v


def _adamw_small(gs, ws, ms, vs, name):
    n = len(gs)

    def body(*refs):
        ins, outs = refs[:4 * n], refs[4 * n:]
        for t in range(n):
            d, m, v = _adamw_update(ins[t][...], ins[n + t][...], ins[2 * n + t][...], ins[3 * n + t][...])
            outs[t][...] = d
            outs[n + t][...] = m
            outs[2 * n + t][...] = v

    vmem = pl.BlockSpec(memory_space=pltpu.VMEM)
    outs = pl.pallas_call(
        body, name=name, in_specs=[vmem] * (4 * n), out_specs=[vmem] * (3 * n),
        out_shape=[jax.ShapeDtypeStruct(w.shape, f32) for w in ws] * 3,
        compiler_params=pltpu.CompilerParams(vmem_limit_bytes=V7X_VMEM_LIMIT))(*gs, *ws, *ms, *vs)
    return outs[:n], outs[n:2 * n], outs[2 * n:]


def _pack(arrays, row_multiple=PACK_ROWS):
    flat = jnp.concatenate([a.reshape(-1) for a in arrays])
    unit = row_multiple * LANES
    n = -(-flat.shape[0] // unit) * unit
    return jnp.pad(flat, (0, n - flat.shape[0])).reshape(n // LANES, LANES)


def _unpack(buf, shapes, lead=()):
    flat = buf.reshape(lead + (-1,))
    out, off = [], 0
    for s in shapes:
        n = 1
        for dim in s:
            n *= dim
        out.append(flat[..., off:off + n].reshape(lead + tuple(s)))
        off += n
    return out


def _to_shards(full, axis):
    s = full.shape
    t = full.reshape(s[:axis] + (4, s[axis] // 4) + s[axis + 1:])
    return jnp.moveaxis(t, axis, 0)


def _from_shards(sh, axis):
    t = jnp.moveaxis(sh, 0, axis)
    s = t.shape
    return t.reshape(s[:axis] + (s[axis] * s[axis + 1],) + s[axis + 2:])


def kernel(x, norm_w, w_in, s5_a_re, s5_a_im, s5_log_step, s5_b_re, s5_b_im, s5_c_re, s5_c_im, s5_d, s5_glu_w, s5_glu_b, q_norm_w, k_norm_w, conv_w, conv_b, dt_bias, ssd_a_log, ssd_d, ssd_norm_w, proj_a, proj_b, proj_c, w_out, loss_target, m_norm_w, m_w_in, m_s5_a_re, m_s5_a_im, m_s5_log_step, m_s5_b_re, m_s5_b_im, m_s5_c_re, m_s5_c_im, m_s5_d, m_s5_glu_w, m_s5_glu_b, m_q_norm_w, m_k_norm_w, m_conv_w, m_conv_b, m_dt_bias, m_ssd_a_log, m_ssd_d, m_ssd_norm_w, m_proj_a, m_proj_b, m_proj_c, m_w_out, v_norm_w, v_w_in, v_s5_a_re, v_s5_a_im, v_s5_log_step, v_s5_b_re, v_s5_b_im, v_s5_c_re, v_s5_c_im, v_s5_d, v_s5_glu_w, v_s5_glu_b, v_q_norm_w, v_k_norm_w, v_conv_w, v_conv_b, v_dt_bias, v_ssd_a_log, v_ssd_d, v_ssd_norm_w, v_proj_a, v_proj_b, v_proj_c, v_w_out):
    given = dict(locals())
    W = {n: given[n] for n in _WEIGHTS}
    M = {n: given["m_" + n] for n in _WEIGHTS}
    V = {n: given["v_" + n] for n in _WEIGHTS}
    n_layers = norm_w.shape[0]
    assert n_layers == 2
    c = lax.axis_index("c")

    mine_of = lambda t: lax.dynamic_index_in_dim(t, c, 0, keepdims=False)
    as_payload = lambda n: lax.bitcast_convert_type(W[n], bf16) if n == "conv_w" else W[n].astype(bf16)
    payload_shapes = [W[n].shape + ((2,) if n == "conv_w" else ()) for n, _ in _SHARDED]
    wpack = _pack([as_payload(n) for n, _ in _SHARDED])
    half_rows = wpack.shape[0] // 2
    _, (pack_half, w_in_mine_layer), _ = _exchange(
        "gather_weights", gather=[lax.dynamic_slice_in_dim(wpack, c * half_rows, half_rows),
                                  mine_of(w_in).astype(bf16)])
    _, _, (w_in_layers, pack_halves) = _exchange("share_weights", sibling=[w_in_mine_layer, pack_half],
                                                 sibling_both=True)
    gathered = jnp.moveaxis(pack_halves, 0, 1).reshape(4, 2 * half_rows, LANES)
    full = dict(W)
    pieces = _unpack(gathered.reshape(4, -1), payload_shapes, lead=(4,))
    for (n, axis), sh in zip(_SHARDED, pieces):
        full[n] = _from_shards(lax.bitcast_convert_type(sh, f32) if n == "conv_w" else sh, axis)

    qs, saves = [], []
    for l in range(n_layers):
        p = {n: full[n][l] for n in _WEIGHTS if n != "w_in"}
        p["w_in"] = [w_in_layers[l, k] for k in range(4)]
        qs.append((_prep_layer(p), p))
    (act, h), sv = layer_fwd(x[0], qs[0][0], "_l0", next_norm_w=qs[1][0]["norm_w"])
    saves.append(sv)
    (dact, lsum), sv = layer_fwd(act, qs[1][0], "_l1", h=h, target=loss_target[0])
    saves.append(sv)
    loss = lax.psum(lsum[0, 0], ("x", "y", "c"))
    layer_grads = [None] * n_layers
    for l in reversed(range(n_layers)):
        q, p = qs[l]
        dact, layer_grads[l] = layer_bwd(dact, saves[l], q, p, f"_l{l}")
    grad_x = dact[None]
    G = {n: jnp.stack([layer_grads[l][n] for l in range(n_layers)]) for n in _WEIGHTS if n != "w_in"}

    repl_shapes = [W[n].shape for n in _REPL]
    small = _pack([G[n] for n in _REPL], 4 * PACK_ROWS)
    quarter = small.shape[0] // 4
    big = [_to_shards(G[n], axis).reshape(4, -1) for n, axis in _SHARDED]
    big = jnp.concatenate(big, axis=1)
    unit = PACK_ROWS * LANES
    nbig = -(-big.shape[1] // unit) * unit
    big = jnp.pad(big, ((0, 0), (0, nbig - big.shape[1]))).reshape(4, nbig // LANES, LANES)
    gpack = jnp.concatenate([big, small.reshape(4, quarter, LANES)], axis=1)
    rbig = nbig // LANES
    g0, g1 = layer_grads[0]["w_in"], layer_grads[1]["w_in"]

    (landed_pack,), _, (from_sibling,) = _exchange(
        "swap_w_in_grads_and_scatter_grads", scatter=[gpack.astype(bf16)], sibling=[jnp.where(c == 0, g1, g0)])
    flat = lambda t: t.reshape(4 * D_MODEL, W_IN_SHARD)
    shards = _add2(flat(jnp.where(c == 0, g0, g1)), flat(from_sibling), "sum_cores_w_in", out_dtype=bf16)
    mine = _sum4(landed_pack, "sum_chips")

    (landed,), _, (other,) = _exchange(
        "scatter_w_in_grads_and_swap_cores", scatter=[shards.reshape(4, D_MODEL, W_IN_SHARD)], sibling=[mine])
    w_in_mine = _adamw(landed, mine_of(w_in), mine_of(m_w_in), mine_of(v_w_in), "adamw_w_in")
    gq = _add2(mine[rbig:], other[rbig:], "sum_cores_small")

    _, (gsmall,), w_in_out = _exchange(
        "share_w_in_updates_and_gather_small", gather=[gq], sibling=w_in_mine, sibling_both=True)
    gsmall = gsmall.reshape(4 * quarter, LANES)

    shard_shapes = [W[n].shape for n, _ in _SHARDED]
    g_mine, g_other = _unpack(mine[:rbig], shard_shapes), _unpack(other[:rbig], shard_shapes)
    rows_of = lambda t: t.reshape(-1, t.shape[-1])
    res = [dict(), dict(), dict(), dict()]
    for k, (n, _) in enumerate(_SHARDED):
        outs = _adamw((rows_of(g_mine[k]), rows_of(g_other[k])), rows_of(W[n]), rows_of(M[n]), rows_of(V[n]),
                      f"adamw_{n}")
        for kind in range(4):
            res[kind][n] = outs[kind].reshape(W[n].shape)
    g_small = _unpack(gsmall, repl_shapes)
    small_out = _adamw_small([rows_of(g) for g in g_small], *([rows_of(T[n]) for n in _REPL] for T in (W, M, V)),
                             "adamw_replicated")
    for kind in range(4):
        res[kind]["w_in"] = w_in_out[kind]
        for k, n in enumerate(_REPL):
            res[kind][n] = g_small[k] if kind == 0 else small_out[kind - 1][k].reshape(W[n].shape)
    return (loss, grad_x, *[res[0][n] for n in _WEIGHTS], *[res[1][n] for n in _WEIGHTS],
            *[res[2][n] for n in _WEIGHTS], *[res[3][n] for n in _WEIGHTS])
```

```python
import functools

import jax
import jax.numpy as jnp
from jax import lax
from jax.experimental import pallas as pl
from jax.experimental.pallas import tpu as pltpu

f32 = jnp.float32
bf16 = jnp.bfloat16

D_MODEL = 1024
RMS_EPS = 1e-6
V7X_VMEM_LIMIT = 60 * 1024 * 1024
LANES = 128
NN, NT, TN = ((1,), (0,)), ((1,), (1,)), ((0,), (0,))

S5_STATES = 2048
S5_ROWS = 512
ATT_SEG = 2048
ATT_BLOCK = 128
SSD_CHUNK = 128
SSD_CHUNKS_PER_STEP = 2
SSD_WIDTH = 768
SSD_XBC = 1280
CONV_ROWS = 512
TAIL_ROWS = 256

ADAM_LR, ADAM_B1, ADAM_B2, ADAM_EPS, ADAM_WD, ADAM_STEP = 0.001, 0.9, 0.999, 1e-08, 0.01, 10

_C_UA, _C_ZA, _C_Q, _C_K, _C_V, _C_ZB, _C_XBC, _C_DT, _C_ZC, _C_GATE, _C_END = (
    0, 512, 1024, 1792, 2560, 3328, 3584, 4864, 4876, 5644, 8716)

_SHARDED = (("s5_glu_w", 1), ("conv_w", 2), ("proj_a", 2), ("proj_b", 2), ("proj_c", 2), ("w_out", 1))
W_IN_SHARD = 2179
_REPL = ("norm_w", "s5_a_re", "s5_a_im", "s5_log_step", "s5_b_re", "s5_b_im", "s5_c_re", "s5_c_im", "s5_d",
         "s5_glu_b", "q_norm_w", "k_norm_w", "conv_b", "dt_bias", "ssd_a_log", "ssd_d", "ssd_norm_w")
_WEIGHTS = ("norm_w", "w_in", "s5_a_re", "s5_a_im", "s5_log_step", "s5_b_re", "s5_b_im", "s5_c_re", "s5_c_im",
            "s5_d", "s5_glu_w", "s5_glu_b", "q_norm_w", "k_norm_w", "conv_w", "conv_b", "dt_bias", "ssd_a_log",
            "ssd_d", "ssd_norm_w", "proj_a", "proj_b", "proj_c", "w_out")
PACK_ROWS = 512


def _dot(a, b, dims):
    return lax.dot_general(a.astype(bf16), b.astype(bf16), (dims, ((), ())), preferred_element_type=f32)


_ANY = pl.BlockSpec(memory_space=pl.ANY)

MAIN_WIDTH = 8448
MAIN_SSD_BLOCK = 3
MAIN_DT_BLOCK = 46
MAIN_ATT_BLOCK = 16


def _call(body, name, grid, in_specs, out_specs, out_shape, scratch=(), sem=None, aliases=None):
    return pl.pallas_call(
        body, name=name, grid=grid, in_specs=in_specs, out_specs=out_specs, out_shape=out_shape,
        scratch_shapes=list(scratch), input_output_aliases=aliases or {},
        compiler_params=pltpu.CompilerParams(dimension_semantics=sem, vmem_limit_bytes=V7X_VMEM_LIMIT))


def _tile(n, options=(1024, 768, 512, 384, 256, 128)):
    return next(t for t in options if n % t == 0)


@functools.partial(jax.custom_vjp, nondiff_argnums=(2,))
def _bdot(a, b, dims):
    return _dot(a, b, dims)


def _bdot_fwd(a, b, dims):
    return _dot(a, b, dims), (a, b)


def _bdot_bwd(dims, res, g):
    a, b = res
    if dims == NN:
        da, db = _dot(g, b, NT), _dot(a, g, TN)
    elif dims == NT:
        da, db = _dot(g, b, NN), _dot(g, a, TN)
    else:
        da, db = _dot(b, g, NT), _dot(a, g, NN)
    return da.astype(a.dtype), db.astype(b.dtype)


_bdot.defvjp(_bdot_fwd, _bdot_bwd)


@functools.partial(jax.custom_vjp, nondiff_argnums=(2,))
def _cdot(a, w, dims):
    return _dot(a, w, dims)


def _cdot_fwd(a, w, dims):
    return _dot(a, w, dims), w


def _cdot_bwd(dims, w, g):
    da = _dot(g, w, NT) if dims == NN else _dot(g, w, NN)
    return da, jnp.zeros_like(w)


_cdot.defvjp(_cdot_fwd, _cdot_bwd)


def _split3(x):
    hi = x.astype(bf16)
    r = x - hi.astype(f32)
    mid = r.astype(bf16)
    lo = (r - mid.astype(f32)).astype(bf16)
    return hi, mid, lo


@jax.custom_vjp
def _xdot_l(m, x):
    return sum(_dot(m, p, NN) for p in _split3(x))


def _xdot_l_fwd(m, x):
    return _xdot_l(m, x), m


def _xdot_l_bwd(m, g):
    return jnp.zeros_like(m), sum(_dot(m, p, TN) for p in _split3(g))


_xdot_l.defvjp(_xdot_l_fwd, _xdot_l_bwd)


@jax.custom_vjp
def _softplus(x):
    e = jnp.exp(-jnp.abs(x))
    u = 1.0 + e
    log1p = jnp.where(u == 1.0, e, jnp.log(u) * (e / jnp.where(u == 1.0, 1.0, u - 1.0)))
    return jnp.maximum(x, 0.0) + log1p


def _softplus_fwd(x):
    return _softplus(x), x


def _softplus_bwd(x, g):
    return (g * jax.nn.sigmoid(x),)


_softplus.defvjp(_softplus_fwd, _softplus_bwd)


def _rms(x, w):
    return x * lax.rsqrt(jnp.mean(x * x, axis=-1, keepdims=True) + RMS_EPS) * w


def mm_nn(a, b, name, tm=2048):
    M, K = a.shape
    N = b.shape[1]
    tn = _tile(N)

    def body(a_ref, b_ref, o_ref):
        o_ref[...] = _dot(a_ref[...], b_ref[...], NN)

    return _call(body, name, (M // tm, N // tn),
                 [pl.BlockSpec((tm, K), lambda i, j: (i, 0)), pl.BlockSpec((K, tn), lambda i, j: (0, j))],
                 pl.BlockSpec((tm, tn), lambda i, j: (i, j)), jax.ShapeDtypeStruct((M, N), f32),
                 sem=("parallel", "parallel"))(a, b)


def mm_nt(a, b, name, tm=1024):
    M, K = a.shape
    N = b.shape[0]
    tk = _tile(K, (2816, 1024, 768, 512, 256, 128))

    def body(a_ref, b_ref, o_ref):
        k = pl.program_id(1)
        p = _dot(a_ref[...], b_ref[...], NT)

        @pl.when(k == 0)
        def _():
            o_ref[...] = p

        @pl.when(k > 0)
        def _():
            o_ref[...] += p

    return _call(body, name, (M // tm, K // tk),
                 [pl.BlockSpec((tm, tk), lambda i, k: (i, k)), pl.BlockSpec((N, tk), lambda i, k: (0, k))],
                 pl.BlockSpec((tm, N), lambda i, k: (i, 0)), jax.ShapeDtypeStruct((M, N), f32),
                 sem=("parallel", "arbitrary"))(a, b)


def mm_tn(a, b, name, tk=2048):
    K, M = a.shape
    N = b.shape[1]
    tn = _tile(N)

    def body(a_ref, b_ref, o_ref):
        k = pl.program_id(1)
        p = _dot(a_ref[...], b_ref[...], TN)

        @pl.when(k == 0)
        def _():
            o_ref[...] = p

        @pl.when(k > 0)
        def _():
            o_ref[...] += p

    return _call(body, name, (N // tn, K // tk),
                 [pl.BlockSpec((tk, M), lambda j, k: (k, 0)), pl.BlockSpec((tk, tn), lambda j, k: (k, j))],
                 pl.BlockSpec((M, tn), lambda j, k: (0, j)), jax.ShapeDtypeStruct((M, N), f32),
                 sem=("parallel", "arbitrary"))(a, b)


def rms_fwd(x, w, name, tm=512):
    S = x.shape[0]

    def body(x_ref, w_ref, o_ref):
        o_ref[...] = _rms(x_ref[...], w_ref[...]).astype(bf16)

    return _call(body, name, (S // tm,),
                 [pl.BlockSpec((tm, D_MODEL), lambda i: (i, 0)), pl.BlockSpec((1, D_MODEL), lambda i: (0, 0))],
                 pl.BlockSpec((tm, D_MODEL), lambda i: (i, 0)), jax.ShapeDtypeStruct((S, D_MODEL), bf16),
                 sem=("parallel",))(x, w)


def mm_nt_rms_bwd(a, b, acc, x, w, dres, name, tm=1024):
    S, K = a.shape

    def body(a_ref, b_ref, acc_ref, x_ref, w_ref, dr_ref, dx_ref, dw_ref):
        dh = _dot(a_ref[...], b_ref[...], NT) + acc_ref[...]
        _, vjp = jax.vjp(_rms, x_ref[...], w_ref[...])
        dx, dw = vjp(dh)
        dx_ref[...] = dx + dr_ref[...]

        @pl.when(pl.program_id(0) == 0)
        def _():
            dw_ref[...] = dw

        @pl.when(pl.program_id(0) > 0)
        def _():
            dw_ref[...] += dw

    row = pl.BlockSpec((tm, D_MODEL), lambda i: (i, 0))
    vec = pl.BlockSpec((1, D_MODEL), lambda i: (0, 0))
    return _call(body, name, (S // tm,),
                 [pl.BlockSpec((tm, K), lambda i: (i, 0)), pl.BlockSpec((D_MODEL, K), lambda i: (0, 0)), row, row, vec,
                  row], [row, vec],
                 [jax.ShapeDtypeStruct((S, D_MODEL), f32), jax.ShapeDtypeStruct((1, D_MODEL), f32)],
                 sem=("arbitrary",))(a, b, acc, x, w, dres)


def _s5_discretize(a_re, a_im, log_step, b_re, b_im, c_re, c_im):
    step = jnp.exp(log_step)[:, None]
    mag = jnp.exp(a_re * step)
    ang = a_im * step
    lam_re, lam_im = mag * jnp.cos(ang), mag * jnp.sin(ang)
    num_re, num_im = lam_re - 1.0, lam_im
    den = a_re * a_re + a_im * a_im
    f_re = (num_re * a_re + num_im * a_im) / den
    f_im = (num_im * a_re - num_re * a_im) / den
    bb_re = f_re[..., None] * b_re - f_im[..., None] * b_im
    bb_im = f_re[..., None] * b_im + f_im[..., None] * b_re
    eye = jnp.eye(8, dtype=f32)

    def block_in(bb):
        t = bb.transpose(0, 2, 1).reshape(4, 8, 16, 1, 64)
        return (t * eye[None, :, None, :, None]).reshape(4, 128, 512)

    def block_out(c):
        t = c.transpose(0, 2, 1).reshape(4, 8, 64, 1, 16)
        return (t * eye[None, :, None, :, None]).reshape(4, 512, 128)

    return (lam_re.reshape(1, S5_STATES), lam_im.reshape(1, S5_STATES), block_in(bb_re), block_in(bb_im),
            block_out(c_re), block_out(c_im))


def _lam_powers(lam_re, lam_im):
    rows_re, rows_im = [lam_re], [lam_im]
    for _ in range(7):
        pr, pi = rows_re[-1], rows_im[-1]
        rows_re.append(pr * lam_re - pi * lam_im)
        rows_im.append(pr * lam_im + pi * lam_re)
    return jnp.concatenate(rows_re, 0), jnp.concatenate(rows_im, 0)


def s5_fwd(u, pw_re, pw_im, w_re, w_im, c_re, c_im, dvec, name):
    S = u.shape[0]
    R, NS = S5_ROWS, S5_STATES
    nb = R // 8

    def body(u_ref, pwr_ref, pwi_ref, wre_ref, wim_ref, cre_ref, cim_ref, d_ref, y_ref, hr_ref, hi_ref,
             car_re, car_im, cin_re, cin_im, up, yp):
        @pl.when(pl.program_id(0) == 0)
        def _():
            car_re[...] = jnp.zeros_like(car_re)
            car_im[...] = jnp.zeros_like(car_im)

        slab = lambda r: pl.ds(r * nb, nb)
        for r in range(8):
            up[slab(r), :] = u_ref[:, r, :]
        u = up[...]
        for j in range(4):
            uj = u[:, 128 * j:128 * (j + 1)]
            hr_ref[:, 512 * j:512 * (j + 1)] = _dot(uj, wre_ref[j], NN)
            hi_ref[:, 512 * j:512 * (j + 1)] = _dot(uj, wim_ref[j], NN)
        lr, li = pwr_ref[0:1, :], pwi_ref[0:1, :]
        for r in range(1, 8):
            pr, pi = hr_ref[slab(r - 1), :], hi_ref[slab(r - 1), :]
            hr_ref[slab(r), :] = lr * pr - li * pi + hr_ref[slab(r), :]
            hi_ref[slab(r), :] = lr * pi + li * pr + hi_ref[slab(r), :]
        l8r, l8i = pwr_ref[7:8, :], pwi_ref[7:8, :]

        def across(c, carry):
            gr, gi = carry
            cin_re[pl.ds(c, 1), :] = gr
            cin_im[pl.ds(c, 1), :] = gi
            er, ei = hr_ref[pl.ds(7 * nb + c, 1), :], hi_ref[pl.ds(7 * nb + c, 1), :]
            return l8r * gr - l8i * gi + er, l8r * gi + l8i * gr + ei

        gr, gi = lax.fori_loop(0, nb, across, (car_re[...], car_im[...]))
        car_re[...] = gr
        car_im[...] = gi
        cr, ci = cin_re[...], cin_im[...]
        for r in range(8):
            pr, pi = pwr_ref[r:r + 1, :], pwi_ref[r:r + 1, :]
            hr_ref[slab(r), :] = hr_ref[slab(r), :] + pr * cr - pi * ci
            hi_ref[slab(r), :] = hi_ref[slab(r), :] + pr * ci + pi * cr
        for j in range(4):
            sl = slice(512 * j, 512 * (j + 1))
            cs = slice(128 * j, 128 * (j + 1))
            yp[:, cs] = (_dot(hr_ref[:, sl], cre_ref[j], NN) - _dot(hi_ref[:, sl], cim_ref[j], NN)
                         + d_ref[:, cs] * u[:, cs])
        for r in range(8):
            y_ref[:, r, :] = yp[slab(r), :]

    full = lambda shape: pl.BlockSpec(shape, lambda i: (0,) * len(shape))
    hspec = pl.BlockSpec((R, NS), lambda i: (i, 0))
    uspec = pl.BlockSpec((nb, 8, 512), lambda i: (i, 0, 0))
    y, h_re, h_im = _call(
        body, name, (S // R,),
        [uspec, full((8, NS)), full((8, NS)), full((4, 128, 512)),
         full((4, 128, 512)), full((4, 512, 128)), full((4, 512, 128)), full((1, 512))],
        [uspec, hspec, hspec],
        [jax.ShapeDtypeStruct((S // 8, 8, 512), f32), jax.ShapeDtypeStruct((S, NS), f32),
         jax.ShapeDtypeStruct((S, NS), f32)],
        scratch=[pltpu.VMEM((1, NS), f32), pltpu.VMEM((1, NS), f32), pltpu.VMEM((nb, NS), f32),
                 pltpu.VMEM((nb, NS), f32), pltpu.VMEM((R, 512), f32), pltpu.VMEM((R, 512), f32)],
        sem=("arbitrary",))(u.reshape(S // 8, 8, 512), pw_re, pw_im, w_re.astype(bf16), w_im.astype(bf16),
                            c_re.astype(bf16), c_im.astype(bf16), dvec)
    return y.reshape(S, 512), h_re, h_im


def s5_bwd(dy, u, h_re, h_im, pw_re, pw_im, w_re, w_im, c_re, c_im, dvec, name):
    S = u.shape[0]
    R, NS = S5_ROWS, S5_STATES
    nb = R // 8
    nchunk = S // R

    def body(dy_ref, u_ref, hr_ref, hi_ref, hpr_ref, hpi_ref, pwr_ref, pwi_ref, wre_ref, wim_ref, cre_ref, cim_ref,
             d_ref, du_ref, dwre_ref, dwim_ref, dcre_ref, dcim_ref, dlr_ref, dli_ref, dd_ref,
             ar, ai, car_re, car_im, cin_re, cin_im, up, dyp, dup):
        i = pl.program_id(0)

        @pl.when(i == 0)
        def _():
            for ref in (car_re, car_im, dwre_ref, dwim_ref, dcre_ref, dcim_ref, dlr_ref, dli_ref, dd_ref):
                ref[...] = jnp.zeros_like(ref)

        slab = lambda r: pl.ds(r * nb, nb)
        for r in range(8):
            up[slab(r), :] = u_ref[:, r, :]
            dyp[slab(r), :] = dy_ref[:, r, :]
        dy = dyp[...]
        u = up[...]
        for j in range(4):
            dyj = dy[:, 128 * j:128 * (j + 1)]
            ar[:, 512 * j:512 * (j + 1)] = _dot(dyj, cre_ref[j], NT)
            ai[:, 512 * j:512 * (j + 1)] = -_dot(dyj, cim_ref[j], NT)
        lr, li = pwr_ref[0:1, :], pwi_ref[0:1, :]
        for r in range(6, -1, -1):
            nr, ni = ar[slab(r + 1), :], ai[slab(r + 1), :]
            ar[slab(r), :] = lr * nr + li * ni + ar[slab(r), :]
            ai[slab(r), :] = lr * ni - li * nr + ai[slab(r), :]
        l8r, l8i = pwr_ref[7:8, :], pwi_ref[7:8, :]

        def across(k, carry):
            c = nb - 1 - k
            gr, gi = carry
            cin_re[pl.ds(c, 1), :] = gr
            cin_im[pl.ds(c, 1), :] = gi
            er, ei = ar[pl.ds(c, 1), :], ai[pl.ds(c, 1), :]
            return l8r * gr + l8i * gi + er, l8r * gi - l8i * gr + ei

        gr, gi = lax.fori_loop(0, nb, across, (car_re[...], car_im[...]))
        car_re[...] = gr
        car_im[...] = gi
        cr, ci = cin_re[...], cin_im[...]
        for r in range(8):
            pr, pi = pwr_ref[7 - r:8 - r, :], pwi_ref[7 - r:8 - r, :]
            ar[slab(r), :] = ar[slab(r), :] + pr * cr + pi * ci
            ai[slab(r), :] = ai[slab(r), :] + pr * ci - pi * cr

        acc_r = jnp.zeros((1, NS), f32)
        acc_i = jnp.zeros((1, NS), f32)
        has_prev = (i < nchunk - 1).astype(f32)
        top = lax.broadcasted_iota(jnp.int32, (nb, NS), 0) == 0
        for r in range(8):
            if r == 0:
                xr = jnp.where(top, hpr_ref[7:8, :] * has_prev, pltpu.roll(hr_ref[slab(7), :], 1, 0))
                xi = jnp.where(top, hpi_ref[7:8, :] * has_prev, pltpu.roll(hi_ref[slab(7), :], 1, 0))
            else:
                xr, xi = hr_ref[slab(r - 1), :], hi_ref[slab(r - 1), :]
            br, bi = ar[slab(r), :], ai[slab(r), :]
            acc_r += jnp.sum(br * xr + bi * xi, axis=0, keepdims=True)
            acc_i += jnp.sum(bi * xr - br * xi, axis=0, keepdims=True)
        dlr_ref[...] += acc_r
        dli_ref[...] += acc_i
        dd_ref[...] += jnp.sum(dy * u, axis=0, keepdims=True)

        for j in range(4):
            sl = slice(512 * j, 512 * (j + 1))
            cs = slice(128 * j, 128 * (j + 1))
            arj, aij = ar[:, sl], ai[:, sl]
            uj, dyj = u[:, cs], dy[:, cs]
            dup[:, cs] = _dot(arj, wre_ref[j], NT) + _dot(aij, wim_ref[j], NT) + d_ref[:, cs] * dyj
            dwre_ref[j] += _dot(uj, arj, TN)
            dwim_ref[j] += _dot(uj, aij, TN)
            dcre_ref[j] += _dot(hr_ref[:, sl], dyj, TN)
            dcim_ref[j] -= _dot(hi_ref[:, sl], dyj, TN)
        for r in range(8):
            du_ref[:, r, :] = dup[slab(r), :]

    rev = lambda i: nchunk - 1 - i
    full = lambda shape: pl.BlockSpec(shape, lambda i: (0,) * len(shape))
    row = pl.BlockSpec((nb, 8, 512), lambda i: (rev(i), 0, 0))
    hspec = pl.BlockSpec((R, NS), lambda i: (rev(i), 0))
    hprev = pl.BlockSpec((8, NS), lambda i: (jnp.maximum(rev(i) * nb - 1, 0), 0))
    outs = _call(
        body, name, (nchunk,),
        [row, row, hspec, hspec, hprev, hprev, full((8, NS)), full((8, NS)), full((4, 128, 512)), full((4, 128, 512)),
         full((4, 512, 128)), full((4, 512, 128)), full((1, 512))],
        [row, full((4, 128, 512)), full((4, 128, 512)), full((4, 512, 128)), full((4, 512, 128)),
         full((1, NS)), full((1, NS)), full((1, 512))],
        [jax.ShapeDtypeStruct((S // 8, 8, 512), f32), jax.ShapeDtypeStruct((4, 128, 512), f32),
         jax.ShapeDtypeStruct((4, 128, 512), f32), jax.ShapeDtypeStruct((4, 512, 128), f32),
         jax.ShapeDtypeStruct((4, 512, 128), f32), jax.ShapeDtypeStruct((1, NS), f32),
         jax.ShapeDtypeStruct((1, NS), f32), jax.ShapeDtypeStruct((1, 512), f32)],
        scratch=[pltpu.VMEM((R, NS), f32), pltpu.VMEM((R, NS), f32), pltpu.VMEM((1, NS), f32),
                 pltpu.VMEM((1, NS), f32), pltpu.VMEM((nb, NS), f32), pltpu.VMEM((nb, NS), f32),
                 pltpu.VMEM((R, 512), f32), pltpu.VMEM((R, 512), f32), pltpu.VMEM((R, 512), f32)],
        sem=("arbitrary",))(dy.reshape(S // 8, 8, 512), u.reshape(S // 8, 8, 512), h_re, h_im, h_re, h_im, pw_re,
                            pw_im, w_re.astype(bf16), w_im.astype(bf16), c_re.astype(bf16), c_im.astype(bf16), dvec)
    return (outs[0].reshape(S, 512),) + tuple(outs[1:])


def _rows(start, n, d):
    return pl.ds(pl.multiple_of(start, ATT_BLOCK), n) if d == 1 else pl.ds(start, n, stride=d)


def _head_masks():
    lane = lax.broadcasted_iota(jnp.int32, (1, LANES), 1)
    return [(lane < 64).astype(f32), (lane >= 64).astype(f32)]


def _head_norm(x, w, hm):
    x2 = x * x
    r = [lax.rsqrt(jnp.sum(x2 * hm[h], axis=-1, keepdims=True) * (1.0 / 64) + RMS_EPS) for h in range(2)]
    sc = hm[0] * r[0] + hm[1] * r[1]
    return x * sc * w, sc, r


def _head_norm_bwd(x, w, sc, r, dxn, hm):
    dw = jnp.sum(dxn * x * sc, axis=0, keepdims=True)
    t = dxn * w
    tx = t * x
    corr = sum(hm[h] * (r[h] * r[h] * r[h]) * jnp.sum(tx * hm[h], axis=-1, keepdims=True) for h in range(2))
    return t * sc - x * corr * (1.0 / 64), dw


def _att_mask(has_prev):
    qi = lax.broadcasted_iota(jnp.int32, (ATT_BLOCK, 2 * ATT_BLOCK), 0) + ATT_BLOCK
    kj = lax.broadcasted_iota(jnp.int32, (ATT_BLOCK, 2 * ATT_BLOCK), 1)
    return (qi - kj >= 0) & (qi - kj <= ATT_BLOCK) & (has_prev | (kj >= ATT_BLOCK))


def _att_block_bwd(q, k, v, o, lse, do, dlse, qw, kw, has_prev):
    hm = _head_masks()
    mask = _att_mask(has_prev)
    qn, qsc, qr = _head_norm(q, qw, hm)
    kn, ksc, kr = _head_norm(k, kw, hm)
    dqn = jnp.zeros((ATT_BLOCK, LANES), f32)
    dkn = jnp.zeros((2 * ATT_BLOCK, LANES), f32)
    dv = jnp.zeros((2 * ATT_BLOCK, LANES), f32)
    for h in range(2):
        qh, do_h = qn * hm[h], do * hm[h]
        s = _dot(qh, kn, NT) * 0.125
        p = jnp.exp(jnp.where(mask, s - lse[:, 64 * h:64 * h + 1], -jnp.inf))
        dp = _dot(do_h, v, NT)
        delta = jnp.sum(do_h * o, axis=-1, keepdims=True)
        dl = jnp.sum(dlse * hm[h], axis=-1, keepdims=True)
        ds = p * (dp - delta + dl) * 0.125
        dqn = dqn + hm[h] * _dot(ds, kn, NN)
        dkn = dkn + _dot(ds, qh, TN)
        dv = dv + _dot(p, do_h, TN)
    dq, dqw = _head_norm_bwd(q, qw, qsc, qr, dqn, hm)
    dk, dkw = _head_norm_bwd(k, kw, ksc, kr, dkn, hm)
    return dq, dk, dv, dqw, dkw


def _att_block(q, k, v, qw, kw, has_prev):
    hm = _head_masks()
    qn, kn = _head_norm(q, qw, hm)[0], _head_norm(k, kw, hm)[0]
    mask = _att_mask(has_prev)
    o = jnp.zeros((ATT_BLOCK, LANES), f32)
    lse = jnp.zeros((ATT_BLOCK, LANES), f32)
    for h in range(2):
        s = _bdot(qn * hm[h], kn, NT) * 0.125
        s = jnp.where(mask, s, -jnp.inf)
        m = jnp.max(s, axis=-1, keepdims=True)
        p = jnp.exp(s - m)
        l = jnp.sum(p, axis=-1, keepdims=True)
        o = o + hm[h] * _bdot(p / l, v, NN)
        lse = lse + hm[h] * (m + jnp.log(l))
    return o, lse


def att_fwd(p_att, qw, kw, d, g, name):
    S = p_att.shape[0]
    SEG = ATT_SEG
    nblk = SEG // ATT_BLOCK

    def body(p_ref, qw_ref, kw_ref, o_ref, l_ref, q_s, k_ext, v_ext, o_s, l_s):
        seg = pl.program_id(1)

        @pl.when(seg == 0)
        def _():
            k_ext[SEG:, :] = jnp.zeros((SEG, LANES), f32)
            v_ext[SEG:, :] = jnp.zeros((SEG, LANES), f32)

        k_ext[:SEG, :] = k_ext[SEG:, :]
        v_ext[:SEG, :] = v_ext[SEG:, :]
        q_s[...] = p_ref[:, 0:128]
        k_ext[SEG:, :] = p_ref[:, 128:256]
        v_ext[SEG:, :] = p_ref[:, 256:384]
        qw_v, kw_v = qw_ref[...], kw_ref[...]

        def blk(b, carry):
            j, r = b // d, b % d
            qs = j * (ATT_BLOCK * d) + r
            ks = SEG + qs - ATT_BLOCK * d
            o, lse = _att_block(q_s[_rows(qs, ATT_BLOCK, d), :], k_ext[_rows(ks, 2 * ATT_BLOCK, d), :],
                                v_ext[_rows(ks, 2 * ATT_BLOCK, d), :], qw_v, kw_v, (seg > 0) | (j > 0))
            o_s[_rows(qs, ATT_BLOCK, d), :] = o
            l_s[_rows(qs, ATT_BLOCK, d), :] = lse
            return carry

        lax.fori_loop(0, nblk, blk, 0, unroll=8)
        o_ref[...] = o_s[...]
        l_ref[...] = l_s[...]

    vec = pl.BlockSpec((1, LANES), lambda hh, s: (0, 0))
    out = pl.BlockSpec((SEG, LANES), lambda hh, s: (s, hh))
    return _call(body, name, (2, S // SEG), [pl.BlockSpec((SEG, 384), lambda hh, s: (s, MAIN_ATT_BLOCK + 2 * g + hh)), vec, vec],
                 [out, out], [jax.ShapeDtypeStruct((S, 256), f32), jax.ShapeDtypeStruct((S, 256), f32)],
                 scratch=[pltpu.VMEM((SEG, LANES), f32), pltpu.VMEM((2 * SEG, LANES), f32),
                          pltpu.VMEM((2 * SEG, LANES), f32), pltpu.VMEM((SEG, LANES), f32),
                          pltpu.VMEM((SEG, LANES), f32)],
                 sem=("arbitrary", "arbitrary"))(p_att, qw, kw)


def att_bwd(p_att, o, lse, do, dlse, qw, kw, d, g, dp_main, name):
    S = p_att.shape[0]
    SEG = ATT_SEG
    nseg = S // SEG
    nblk = SEG // ATT_BLOCK

    def body(p_ref, pp_ref, o_ref, l_ref, do_ref, dl_ref, qw_ref, kw_ref, _, dp_ref, dqw_ref, dkw_ref,
             q_s, k_ext, v_ext, dq_s, dk_ext, dv_ext):
        hh, i = pl.program_id(0), pl.program_id(1)
        seg = nseg - 1 - i

        @pl.when(i == 0)
        def _():
            dk_ext[...] = jnp.zeros_like(dk_ext)
            dv_ext[...] = jnp.zeros_like(dv_ext)

        @pl.when((i == 0) & (hh == 0))
        def _():
            dqw_ref[...] = jnp.zeros_like(dqw_ref)
            dkw_ref[...] = jnp.zeros_like(dkw_ref)

        dk_ext[SEG:, :] = dk_ext[:SEG, :]
        dv_ext[SEG:, :] = dv_ext[:SEG, :]
        dk_ext[:SEG, :] = jnp.zeros((SEG, LANES), f32)
        dv_ext[:SEG, :] = jnp.zeros((SEG, LANES), f32)
        q_s[...] = p_ref[:, 0:128]
        k_ext[SEG:, :] = p_ref[:, 128:256]
        v_ext[SEG:, :] = p_ref[:, 256:384]
        k_ext[:SEG, :] = pp_ref[:, 128:256]
        v_ext[:SEG, :] = pp_ref[:, 256:384]
        qw_v, kw_v = qw_ref[...], kw_ref[...]

        def blk_pair(i2, carry):
            dqw, dkw = carry
            done = []
            for u in range(2):
                b = 2 * i2 + u
                j, r = b // d, b % d
                qs = j * (ATT_BLOCK * d) + r
                ks = SEG + qs - ATT_BLOCK * d
                has_prev = (seg > 0) | (j > 0)
                qrows, krows = _rows(qs, ATT_BLOCK, d), _rows(ks, 2 * ATT_BLOCK, d)
                dq, dk, dv, dqw_b, dkw_b = _att_block_bwd(
                    q_s[qrows, :], k_ext[krows, :], v_ext[krows, :], o_ref[qrows, :], l_ref[qrows, :],
                    do_ref[qrows, :], dl_ref[qrows, :], qw_v, kw_v, has_prev)
                dqw, dkw = dqw + dqw_b, dkw + dkw_b
                done.append((qrows, krows, dq, dk, dv))
            for qrows, krows, dq, dk, dv in done:
                dq_s[qrows, :] = dq
                dk_ext[krows, :] = dk_ext[krows, :] + dk
                dv_ext[krows, :] = dv_ext[krows, :] + dv
            return dqw, dkw

        zero = jnp.zeros((1, LANES), f32)
        dqw, dkw = lax.fori_loop(0, nblk // 2, blk_pair, (zero, zero))
        dqw_ref[...] += dqw
        dkw_ref[...] += dkw
        dp_ref[:, 0:128] = dq_s[...].astype(bf16)
        dp_ref[:, 128:256] = dk_ext[SEG:, :].astype(bf16)
        dp_ref[:, 256:384] = dv_ext[SEG:, :].astype(bf16)

    rev = lambda i: nseg - 1 - i
    vec = pl.BlockSpec((1, LANES), lambda hh, i: (0, 0))
    blk = MAIN_ATT_BLOCK + 2 * g
    cur = pl.BlockSpec((SEG, 384), lambda hh, i: (rev(i), blk + hh))
    prev = pl.BlockSpec((SEG, 384), lambda hh, i: (jnp.maximum(rev(i) - 1, 0), blk + hh))
    col = pl.BlockSpec((SEG, LANES), lambda hh, i: (rev(i), hh))
    big = pltpu.VMEM((2 * SEG, LANES), f32)
    one = pltpu.VMEM((SEG, LANES), f32)
    return _call(body, name, (2, nseg), [cur, prev, col, col, col, col, vec, vec, _ANY], [cur, vec, vec],
                 [jax.ShapeDtypeStruct((S, MAIN_WIDTH), bf16), jax.ShapeDtypeStruct((1, LANES), f32),
                  jax.ShapeDtypeStruct((1, LANES), f32)],
                 scratch=[one, big, big, one, big, big], sem=("arbitrary", "arbitrary"),
                 aliases={8: 0})(p_att, p_att, o, lse, do, dlse, qw, kw, dp_main)


def conv_fwd(p_ssd, conv_w, conv_b, name):
    S = p_ssd.shape[0]
    tm, C = CONV_ROWS, SSD_XBC

    def body(x_ref, xp_ref, w_ref, b_ref, o_ref):
        first = (pl.program_id(0) == 0)
        ext = jnp.concatenate([jnp.where(first, 0.0, xp_ref[:, 0:C]), x_ref[:, 0:C]], axis=0)
        acc = b_ref[...] + w_ref[3:4, :] * ext[8:, :]
        for k in range(1, 4):
            acc = acc + w_ref[3 - k:4 - k, :] * pltpu.roll(ext, k, 0)[8:, :]
        o_ref[...] = jax.nn.silu(acc)

    return _call(body, name, (S // tm,),
                 [pl.BlockSpec((tm, 1536), lambda i: (i, MAIN_SSD_BLOCK)),
                  pl.BlockSpec((8, 1536), lambda i: (jnp.maximum(i * (tm // 8) - 1, 0), MAIN_SSD_BLOCK)),
                  pl.BlockSpec((4, C), lambda i: (0, 0)), pl.BlockSpec((1, C), lambda i: (0, 0))],
                 pl.BlockSpec((tm, C), lambda i: (i, 0)), jax.ShapeDtypeStruct((S, C), f32),
                 sem=("parallel",))(p_ssd, p_ssd, conv_w, conv_b)


def conv_bwd(p_ssd, dact, ddt, conv_w, conv_b, dp_main, name):
    S = p_ssd.shape[0]
    tm, C = CONV_ROWS, SSD_XBC
    nblk = S // tm

    def body(x_ref, xp_ref, xn_ref, da_ref, dan_ref, ddt_ref, w_ref, b_ref, _, dp_ref, dw_ref, db_ref):
        i = pl.program_id(0)
        rows = tm + 8
        ext = jnp.concatenate([jnp.where(i == 0, 0.0, xp_ref[:, 0:C]), x_ref[:, 0:C], xn_ref[:, 0:C]], axis=0)
        shifted = [ext[8:, :]] + [pltpu.roll(ext, k, 0)[8:, :] for k in range(1, 4)]
        pre = b_ref[...] + w_ref[3:4, :] * shifted[0]
        for k in range(1, 4):
            pre = pre + w_ref[3 - k:4 - k, :] * shifted[k]
        sg = jax.nn.sigmoid(pre)
        dact = jnp.concatenate([da_ref[...], jnp.where(i == nblk - 1, 0.0, dan_ref[...])], axis=0)
        dpre = dact * (sg * (1.0 + pre * (1.0 - sg)))
        dx = w_ref[3:4, :] * dpre[0:tm, :]
        for k in range(1, 4):
            dx = dx + w_ref[3 - k:4 - k, :] * pltpu.roll(dpre, rows - k, 0)[0:tm, :]
        dp_ref[:, 0:C] = dx.astype(bf16)
        dp_ref[:, C:C + 128] = ddt_ref[...].astype(bf16)
        dp_ref[:, C + 128:] = jnp.zeros((tm, 128), bf16)
        dcur = dpre[0:tm, :]
        dws = [jnp.sum(dcur * shifted[3 - j][0:tm, :], axis=0, keepdims=True) for j in range(4)]
        dbs = jnp.sum(dcur, axis=0, keepdims=True)

        @pl.when(i == 0)
        def _():
            dw_ref[...] = jnp.zeros_like(dw_ref)
            db_ref[...] = jnp.zeros_like(db_ref)

        for j in range(4):
            dw_ref[j:j + 1, :] += dws[j]
        db_ref[...] += dbs

    t8 = tm // 8
    blk = MAIN_SSD_BLOCK
    return _call(body, name, (nblk,),
                 [pl.BlockSpec((tm, 1536), lambda i: (i, blk)),
                  pl.BlockSpec((8, 1536), lambda i: (jnp.maximum(i * t8 - 1, 0), blk)),
                  pl.BlockSpec((8, 1536), lambda i: (jnp.minimum((i + 1) * t8, S // 8 - 1), blk)),
                  pl.BlockSpec((tm, C), lambda i: (i, 0)),
                  pl.BlockSpec((8, C), lambda i: (jnp.minimum((i + 1) * t8, S // 8 - 1), 0)),
                  pl.BlockSpec((tm, 128), lambda i: (i, 0)),
                  pl.BlockSpec((4, C), lambda i: (0, 0)), pl.BlockSpec((1, C), lambda i: (0, 0)), _ANY],
                 [pl.BlockSpec((tm, 1536), lambda i: (i, blk)), pl.BlockSpec((4, C), lambda i: (0, 0)),
                  pl.BlockSpec((1, C), lambda i: (0, 0))],
                 [jax.ShapeDtypeStruct((S, MAIN_WIDTH), bf16), jax.ShapeDtypeStruct((4, C), f32),
                  jax.ShapeDtypeStruct((1, C), f32)],
                 sem=("arbitrary",), aliases={8: 0})(p_ssd, p_ssd, p_ssd, dact, dact, ddt, conv_w, conv_b, dp_main)


def _ssd_chunk(xbc, dtr, state, dt_bias, a_log, d_full):
    T = SSD_CHUNK
    r_i = lax.broadcasted_iota(jnp.int32, (T, T), 0)
    c_i = lax.broadcasted_iota(jnp.int32, (T, T), 1)
    tril = c_i <= r_i
    tri = tril.astype(bf16)
    lane = lax.broadcasted_iota(jnp.int32, (1, LANES), 1)
    hm = [(lane < 64).astype(f32), (lane >= 64).astype(f32)]
    column = lambda v, h: jnp.broadcast_to(v[:, h:h + 1], (T, LANES))

    def per_head_lanes(v):
        return jnp.concatenate([jnp.where(lane < 64, column(v, 2 * pp), column(v, 2 * pp + 1)) for pp in range(6)],
                               axis=1)

    xs, bm, cm = xbc[:, :768], xbc[:, 768:1024], xbc[:, 1024:1280]
    dt = _softplus(dtr + dt_bias)
    a_dt = dt * (-jnp.exp(a_log))
    a_cs = _xdot_l(tri, a_dt)
    dt_full = per_head_lanes(dt)
    acs_full = per_head_lanes(a_cs)
    last = lax.broadcasted_iota(jnp.int32, (T, SSD_WIDTH), 0) == T - 1
    tot_full = jnp.sum(jnp.where(last, acs_full, 0.0), axis=0, keepdims=True)
    xdt = xs * dt_full
    xw = xdt * jnp.exp(tot_full - acs_full)
    eacs = jnp.exp(acs_full)
    st_parts, off_parts, diag_parts = [], [], []
    for g in range(2):
        bg, cg = bm[:, 128 * g:128 * (g + 1)], cm[:, 128 * g:128 * (g + 1)]
        cols = slice(384 * g, 384 * (g + 1))
        st_parts.append(_bdot(bg, xw[:, cols], TN))
        off_parts.append(_bdot(cg, state[:, cols], NN))
        cb = _bdot(cg, bg, NT)
        for pp in range(3 * g, 3 * g + 3):
            xp = xdt[:, 128 * pp:128 * (pp + 1)]
            acc = jnp.zeros((T, LANES), f32)
            for hh in range(2):
                a_col = column(a_cs, 2 * pp + hh)
                decay = jnp.where(tril, jnp.exp(jnp.minimum(a_col - a_col.T, 0.0)), 0.0)
                acc = acc + _bdot(cb * decay, xp * hm[hh], NN)
            diag_parts.append(acc)
    new_state = state * jnp.exp(tot_full) + jnp.concatenate(st_parts, axis=1)
    y = jnp.concatenate(diag_parts, axis=1) + jnp.concatenate(off_parts, axis=1) * eacs + xs * d_full
    return y, new_state


def ssd_fwd(xact, p_ssd, dt_bias, a_log, d_full, name):
    S = xact.shape[0]
    T = SSD_CHUNK

    U = SSD_CHUNKS_PER_STEP

    def body(x_ref, p_ref, b_ref, a_ref, d_ref, y_ref, s_ref, state):
        @pl.when(pl.program_id(0) == 0)
        def _():
            state[...] = jnp.zeros_like(state)

        st = state[...]
        for u in range(U):
            rows = slice(T * u, T * (u + 1))
            s_ref[u] = st
            y, st = _ssd_chunk(x_ref[rows, :], p_ref[rows, :], st, b_ref[...], a_ref[...], d_ref[...])
            y_ref[rows, :] = y
        state[...] = st

    vec = lambda n: pl.BlockSpec((1, n), lambda i: (0, 0))
    return _call(body, name, (S // (U * T),),
                 [pl.BlockSpec((U * T, SSD_XBC), lambda i: (i, 0)),
                  pl.BlockSpec((U * T, 128), lambda i: (i, MAIN_DT_BLOCK)), vec(128), vec(128), vec(768)],
                 [pl.BlockSpec((U * T, 768), lambda i: (i, 0)), pl.BlockSpec((U, T, 768), lambda i: (i, 0, 0))],
                 [jax.ShapeDtypeStruct((S, 768), f32), jax.ShapeDtypeStruct((S // T, T, 768), f32)],
                 scratch=[pltpu.VMEM((T, 768), f32)], sem=("arbitrary",))(xact, p_ssd, dt_bias, a_log, d_full)


def ssd_bwd(xact, p_ssd, states, dy, dt_bias, a_log, d_full, name):
    S = xact.shape[0]
    T = SSD_CHUNK
    U = 1
    nc = S // (U * T)

    def body(x_ref, p_ref, s_ref, dy_ref, b_ref, a_ref, d_ref, dx_ref, ddt_ref, db_ref, da_ref, dd_ref, dstate):
        i = pl.program_id(0)

        @pl.when(i == 0)
        def _():
            for ref in (dstate, db_ref, da_ref, dd_ref):
                ref[...] = jnp.zeros_like(ref)

        dst = dstate[...]
        for u in reversed(range(U)):
            rows = slice(T * u, T * (u + 1))
            _, vjp = jax.vjp(_ssd_chunk, x_ref[rows, :], p_ref[rows, :], s_ref[u], b_ref[...], a_ref[...], d_ref[...])
            dx, ddt, dst, db, da, dd = vjp((dy_ref[rows, :], dst))
            dx_ref[rows, :] = dx
            ddt_ref[rows, :] = ddt
            db_ref[...] += db
            da_ref[...] += da
            dd_ref[...] += dd
        dstate[...] = dst

    rev = lambda i: nc - 1 - i
    vec = lambda n: pl.BlockSpec((1, n), lambda i: (0, 0))
    return _call(body, name, (nc,),
                 [pl.BlockSpec((U * T, SSD_XBC), lambda i: (rev(i), 0)),
                  pl.BlockSpec((U * T, 128), lambda i: (rev(i), MAIN_DT_BLOCK)),
                  pl.BlockSpec((U, T, 768), lambda i: (rev(i), 0, 0)), pl.BlockSpec((U * T, 768), lambda i: (rev(i), 0)),
                  vec(128), vec(128), vec(768)],
                 [pl.BlockSpec((U * T, SSD_XBC), lambda i: (rev(i), 0)), pl.BlockSpec((U * T, 128), lambda i: (rev(i), 0)),
                  vec(128), vec(128), vec(768)],
                 [jax.ShapeDtypeStruct((S, SSD_XBC), f32), jax.ShapeDtypeStruct((S, 128), f32),
                  jax.ShapeDtypeStruct((1, 128), f32), jax.ShapeDtypeStruct((1, 128), f32),
                  jax.ShapeDtypeStruct((1, 768), f32)],
                 scratch=[pltpu.VMEM((T, 768), f32)],
                 sem=("arbitrary",))(xact, p_ssd, states, dy, dt_bias, a_log, d_full)


def _tail_fn(ys5, pt, o0, o1, o2, l0, l1, l2, yssd, glu_b, nw, pr_glu, pr_a, pr_b, pr_c, x, weights):
    glu_w, pa, pb, pc, wo = weights
    gates = jax.nn.sigmoid(pt[:, :3072])
    za, zb, zc = pt[:, 3072:3584], pt[:, 3584:3840], pt[:, 3840:4608]
    g = jax.nn.gelu(ys5)
    ya = g * jax.nn.sigmoid(_cdot(g, glu_w, NN) + glu_b + pr_glu) * jax.nn.silu(za)
    m = jnp.maximum(jnp.maximum(l0, l1), l2)
    e0, e1, e2 = jnp.exp(l0 - m), jnp.exp(l1 - m), jnp.exp(l2 - m)
    yb = (e0 * o0 + e1 * o1 + e2 * o2) / (e0 + e1 + e2) * jax.nn.silu(zb)
    yc = _rms(yssd * jax.nn.silu(zc), nw)
    merged = (gates[:, :1024] * (_cdot(ya, pa, NN) + pr_a) + gates[:, 1024:2048] * (_cdot(yb, pb, NN) + pr_b)
              + gates[:, 2048:] * (_cdot(yc, pc, NN) + pr_c))
    out = x + _cdot(merged, wo, NN)
    return out, (g, ya, yb, yc, merged)


def _tail_specs(tm):
    row = lambda n: pl.BlockSpec((tm, n), lambda i: (i, 0))
    full = lambda a, b: pl.BlockSpec((a, b), lambda i: (0, 0))
    acts = [row(512), row(4608)] + [row(256)] * 6 + [row(768), row(D_MODEL)]
    consts = [full(1, 512), full(1, 768), full(512, 512), full(512, D_MODEL), full(256, D_MODEL),
              full(768, D_MODEL), full(D_MODEL, D_MODEL)]
    return row, full, acts, consts


def tail_fwd(ys5, pt, os_, ls_, yssd, x, glu_b, nw, weights, name, next_norm_w=None, target=None):
    S = x.shape[0]
    tm = TAIL_ROWS
    row, full, acts, consts = _tail_specs(tm)

    def body(ys5_ref, pt_ref, o0, o1, o2, l0, l1, l2, yssd_ref, x_ref, gb_ref, nw_ref, gw, pa, pb, pc, wo, *rest):
        z = lambda n: jnp.zeros((tm, n), f32)
        out, _ = _tail_fn(ys5_ref[...], pt_ref[...], o0[...], o1[...], o2[...], l0[...], l1[...], l2[...],
                          yssd_ref[...], gb_ref[...], nw_ref[...], z(512), z(D_MODEL), z(D_MODEL), z(D_MODEL),
                          x_ref[...], (gw[...], pa[...], pb[...], pc[...], wo[...]))
        if target is not None:
            t_ref, dy_ref, l_ref = rest
            diff = out - t_ref[...]
            dy_ref[...] = diff * (1.0 / D_MODEL)
            part = jnp.full((8, LANES), 0.5 / D_MODEL * jnp.sum(diff * diff), f32)

            @pl.when(pl.program_id(0) == 0)
            def _():
                l_ref[...] = part

            @pl.when(pl.program_id(0) > 0)
            def _():
                l_ref[...] += part
        elif next_norm_w is not None:
            n_ref, out_ref, h_ref = rest
            out_ref[...] = out
            h_ref[...] = _rms(out, n_ref[...]).astype(bf16)
        else:
            rest[0][...] = out

    sd = jax.ShapeDtypeStruct((S, D_MODEL), f32)
    if target is not None:
        extra_in, extra_specs = [target], [row(D_MODEL)]
        out_specs = [row(D_MODEL), pl.BlockSpec((8, LANES), lambda i: (0, 0))]
        out_shape = [sd, jax.ShapeDtypeStruct((8, LANES), f32)]
    elif next_norm_w is not None:
        extra_in, extra_specs = [next_norm_w], [full(1, D_MODEL)]
        out_specs, out_shape = [row(D_MODEL), row(D_MODEL)], [sd, jax.ShapeDtypeStruct((S, D_MODEL), bf16)]
    else:
        extra_in, extra_specs, out_specs, out_shape = [], [], row(D_MODEL), sd
    return _call(body, name, (S // tm,), acts + consts + extra_specs, out_specs, out_shape,
                 sem=("arbitrary",))(ys5, pt, *os_, *ls_, yssd, x, glu_b, nw, *weights, *extra_in)


def tail_bwd(ys5, pt, os_, ls_, yssd, dout, glu_b, nw, weights, name):
    S = dout.shape[0]
    tm = TAIL_ROWS
    row, full, acts, consts = _tail_specs(tm)

    def body(ys5_ref, pt_ref, o0, o1, o2, l0, l1, l2, yssd_ref, dout_ref, gb_ref, nw_ref, gw, pa, pb, pc, wo,
             dys5_ref, dpt_ref, do0, do1, do2, dl0, dl1, dl2, dyssd_ref, dgb_ref, dnw_ref,
             g_ref, ya_ref, yb_ref, yc_ref, mg_ref, dglu_ref, dpa_ref, dpb_ref, dpc_ref):
        z = lambda n: jnp.zeros((tm, n), f32)
        w = (gw[...], pa[...], pb[...], pc[...], wo[...])
        fn = lambda *a: _tail_fn(*a, z(D_MODEL), w)
        _, vjp, aux = jax.vjp(fn, ys5_ref[...], pt_ref[...], o0[...], o1[...], o2[...], l0[...], l1[...], l2[...],
                              yssd_ref[...], gb_ref[...], nw_ref[...], z(512), z(D_MODEL), z(D_MODEL), z(D_MODEL),
                              has_aux=True)
        (dys5, dpt, d0, d1, d2, e0, e1, e2, dyssd, dgb, dnw, dglu, dpa, dpb, dpc) = vjp(dout_ref[...])
        dys5_ref[...] = dys5
        dpt_ref[...] = dpt.astype(bf16)
        for ref, val in ((do0, d0), (do1, d1), (do2, d2), (dl0, e0), (dl1, e1), (dl2, e2)):
            ref[...] = val
        dyssd_ref[...] = dyssd
        g, ya, yb, yc, merged = aux
        for ref, val in ((g_ref, g), (ya_ref, ya), (yb_ref, yb), (yc_ref, yc), (mg_ref, merged),
                         (dglu_ref, dglu), (dpa_ref, dpa), (dpb_ref, dpb), (dpc_ref, dpc)):
            ref[...] = val.astype(bf16)

        @pl.when(pl.program_id(0) == 0)
        def _():
            dgb_ref[...] = dgb
            dnw_ref[...] = dnw

        @pl.when(pl.program_id(0) > 0)
        def _():
            dgb_ref[...] += dgb
            dnw_ref[...] += dnw

    sd = lambda n, dt=f32: jax.ShapeDtypeStruct((S, n), dt)
    out_specs = ([row(512), row(4608)] + [row(256)] * 6 + [row(768), full(1, 512), full(1, 768)]
                 + [row(512), row(512), row(256), row(768), row(D_MODEL), row(512)] + [row(D_MODEL)] * 3)
    out_shape = ([sd(512), sd(MAIN_WIDTH, bf16)] + [sd(256)] * 6 + [sd(768), jax.ShapeDtypeStruct((1, 512), f32),
                                                          jax.ShapeDtypeStruct((1, 768), f32)]
                 + [sd(512, bf16), sd(512, bf16), sd(256, bf16), sd(768, bf16), sd(D_MODEL, bf16), sd(512, bf16)]
                 + [sd(D_MODEL, bf16)] * 3)
    return _call(body, name, (S // tm,), acts + consts, out_specs, out_shape,
                 sem=("arbitrary",))(ys5, pt, *os_, *ls_, yssd, dout, glu_b, nw, *weights)


def _in_proj_segments(shards):
    dtype = shards[0].dtype

    def c(a, b):
        parts = []
        for k, sh in enumerate(shards):
            lo, hi = max(a, W_IN_SHARD * k), min(b, W_IN_SHARD * (k + 1))
            if lo < hi:
                parts.append(sh[:, lo - W_IN_SHARD * k:hi - W_IN_SHARD * k])
        return parts[0] if len(parts) == 1 else jnp.concatenate(parts, axis=1)

    atts = []
    for g in range(3):
        parts = []
        for hh in range(2):
            o = 64 * (4 * g + 2 * hh)
            parts += [c(_C_Q + o, _C_Q + o + 128), c(_C_K + o, _C_K + o + 128), c(_C_V + o, _C_V + o + 128)]
        atts.append(jnp.concatenate(parts, axis=1))
    ssd = jnp.concatenate([c(_C_XBC, _C_ZC), jnp.zeros((D_MODEL, 1536 - (_C_ZC - _C_XBC)), dtype)], axis=1)
    tail = jnp.concatenate([c(_C_GATE, _C_END), c(_C_ZA, _C_Q), c(_C_ZB, _C_XBC), c(_C_ZC, _C_GATE)], axis=1)
    return [c(_C_UA, _C_ZA), jnp.concatenate([tail, ssd] + atts, axis=1)]


def _in_proj_grad(ds5, dmain):
    dtail, dssd = dmain[:, :4608], dmain[:, 4608:6144]
    datts = [dmain[:, 6144 + 768 * g:6144 + 768 * (g + 1)] for g in range(3)]
    pick = lambda off: [datts[g][:, 384 * hh + off:384 * hh + off + 128] for g in range(3) for hh in range(2)]
    pieces = ([ds5, dtail[:, 3072:3584]] + pick(0) + pick(128) + pick(256)
              + [dtail[:, 3584:3840], dssd[:, :_C_ZC - _C_XBC], dtail[:, 3840:4608], dtail[:, :3072]])
    shards, start = [[] for _ in range(4)], 0
    for piece in pieces:
        width = piece.shape[1]
        for k in range(4):
            lo, hi = max(start, W_IN_SHARD * k), min(start + width, W_IN_SHARD * (k + 1))
            if lo < hi:
                shards[k].append(piece[:, lo - start:hi - start])
        start += width
    return jnp.stack([jnp.concatenate(s, axis=1) for s in shards])


def _prep_layer(p):
    q = {}
    q["segs"] = [s.astype(bf16) for s in _in_proj_segments(p["w_in"])]
    disc = _s5_discretize(p["s5_a_re"], p["s5_a_im"], p["s5_log_step"], p["s5_b_re"], p["s5_b_im"],
                          p["s5_c_re"], p["s5_c_im"])
    q["s5"] = disc
    q["pw"] = _lam_powers(disc[0], disc[1])
    q["s5_d"] = p["s5_d"].reshape(1, 512)
    q["qw"] = jnp.tile(p["q_norm_w"], 2).reshape(1, LANES)
    q["kw"] = jnp.tile(p["k_norm_w"], 2).reshape(1, LANES)
    q["conv_w"] = p["conv_w"]
    q["conv_b"] = p["conv_b"].reshape(1, SSD_XBC)
    pad = lambda v: jnp.pad(v, (0, LANES - v.shape[0])).reshape(1, LANES)
    q["dt_bias"], q["a_log"] = pad(p["dt_bias"]), pad(p["ssd_a_log"])
    q["d_full"] = jnp.repeat(p["ssd_d"], 64).reshape(1, SSD_WIDTH)
    q["glu_b"] = p["s5_glu_b"].reshape(1, 512)
    q["nw"] = p["ssd_norm_w"].reshape(1, SSD_WIDTH)
    q["norm_w"] = p["norm_w"].reshape(1, D_MODEL)
    q["tailw"] = tuple(p[n].astype(bf16) for n in ("s5_glu_w", "proj_a", "proj_b", "proj_c", "w_out"))
    return q


_DILATIONS = (1, 4, 16)


def layer_fwd(x, q, tag, h=None, next_norm_w=None, target=None):
    if h is None:
        h = rms_fwd(x, q["norm_w"], f"rms_fwd{tag}")
    p_s5, p_main = [mm_nn(h, w, f"inproj{k}{tag}") for k, w in enumerate(q["segs"])]
    _, _, w_re, w_im, c_re, c_im = q["s5"]
    ys5, h_re, h_im = s5_fwd(p_s5, *q["pw"], w_re, w_im, c_re, c_im, q["s5_d"], f"s5_fwd{tag}")
    os_, ls_ = [], []
    for g, d in enumerate(_DILATIONS):
        o, l = att_fwd(p_main, q["qw"], q["kw"], d, g, f"att_fwd{g}{tag}")
        os_.append(o)
        ls_.append(l)
    xact = conv_fwd(p_main, q["conv_w"], q["conv_b"], f"conv_fwd{tag}")
    yssd, states = ssd_fwd(xact, p_main, q["dt_bias"], q["a_log"], q["d_full"], f"ssd_fwd{tag}")
    out = tail_fwd(ys5, p_main, os_, ls_, yssd, x, q["glu_b"], q["nw"], q["tailw"], f"tail_fwd{tag}",
                   next_norm_w=next_norm_w, target=target)
    saved = dict(x=x, h=h, p_s5=p_s5, p_main=p_main, ys5=ys5, h_re=h_re, h_im=h_im,
                 os=os_, ls=ls_, xact=xact, yssd=yssd, states=states)
    return out, saved


def layer_bwd(dout, sv, q, p, tag):
    (dys5, dp_main, do0, do1, do2, dl0, dl1, dl2, dyssd, dglu_b, dnw, g_b, ya_b, yb_b, yc_b, mg_b, dglu_b16,
     dpa_b, dpb_b, dpc_b) = tail_bwd(sv["ys5"], sv["p_main"], sv["os"], sv["ls"], sv["yssd"], dout, q["glu_b"],
                                     q["nw"], q["tailw"], f"tail_bwd{tag}")
    grads = {}
    grads["s5_glu_w"] = mm_tn(g_b, dglu_b16, f"dglu_w{tag}")
    grads["proj_a"] = mm_tn(ya_b, dpa_b, f"dproj_a{tag}")
    grads["proj_b"] = mm_tn(yb_b, dpb_b, f"dproj_b{tag}")
    grads["proj_c"] = mm_tn(yc_b, dpc_b, f"dproj_c{tag}")
    grads["w_out"] = mm_tn(mg_b, dout, f"dw_out{tag}")
    grads["s5_glu_b"] = dglu_b.reshape(512)
    grads["ssd_norm_w"] = dnw.reshape(SSD_WIDTH)

    dxact, ddt, ddt_bias, da_log, dd_full = ssd_bwd(sv["xact"], sv["p_main"], sv["states"], dyssd, q["dt_bias"],
                                                    q["a_log"], q["d_full"], f"ssd_bwd{tag}")
    dp_main, dconv_w, dconv_b = conv_bwd(sv["p_main"], dxact, ddt, q["conv_w"], q["conv_b"], dp_main,
                                         f"conv_bwd{tag}")
    grads["dt_bias"] = ddt_bias[0, :12]
    grads["ssd_a_log"] = da_log[0, :12]
    grads["ssd_d"] = dd_full.reshape(12, 64).sum(axis=1)
    grads["conv_w"] = dconv_w
    grads["conv_b"] = dconv_b.reshape(SSD_XBC)

    dqw, dkw = 0.0, 0.0
    for g, d in enumerate(_DILATIONS):
        dp_main, a, b = att_bwd(sv["p_main"], sv["os"][g], sv["ls"][g], (do0, do1, do2)[g], (dl0, dl1, dl2)[g],
                                q["qw"], q["kw"], d, g, dp_main, f"att_bwd{g}{tag}")
        dqw, dkw = dqw + a, dkw + b
    grads["q_norm_w"] = dqw.reshape(2, 64).sum(axis=0)
    grads["k_norm_w"] = dkw.reshape(2, 64).sum(axis=0)

    _, _, w_re, w_im, c_re, c_im = q["s5"]
    dp_s5, dwre, dwim, dcre, dcim, dlam_re, dlam_im, dd = s5_bwd(
        dys5, sv["p_s5"], sv["h_re"], sv["h_im"], *q["pw"], w_re, w_im, c_re, c_im, q["s5_d"], f"s5_bwd{tag}")
    s5_names = ("s5_a_re", "s5_a_im", "s5_log_step", "s5_b_re", "s5_b_im", "s5_c_re", "s5_c_im")
    _, disc_vjp = jax.vjp(_s5_discretize, *[p[n] for n in s5_names])
    for n, gr in zip(s5_names, disc_vjp((dlam_re, dlam_im, dwre, dwim, dcre, dcim))):
        grads[n] = gr
    grads["s5_d"] = dd.reshape(512)

    dsegs = [dp_s5, dp_main]
    dws = [mm_tn(sv["h"], ds, f"dw_in{k}{tag}") for k, ds in enumerate(dsegs)]
    grads["w_in"] = _in_proj_grad(*dws)
    dh_main = mm_nt(dp_main, q["segs"][1], f"dh1{tag}")
    dx, dnorm_w = mm_nt_rms_bwd(dp_s5, q["segs"][0], dh_main, sv["x"], q["norm_w"], dout, f"dh0_rms_bwd{tag}")
    grads["norm_w"] = dnorm_w.reshape(D_MODEL)
    return dx, grads


def _exchange(name, scatter=(), gather=(), sibling=(), sibling_both=False, sibling_by_core=None):
    scatter, gather, sibling = list(scatter), list(gather), list(sibling)
    chip_xs = scatter + gather
    ns, nc, nb = len(scatter), len(chip_xs), len(sibling)
    n = nc + nb
    n_in = n + (2 if sibling_by_core else 0)
    n_out = n + (1 if sibling_by_core else 0)
    n_sem = 3 * nc + nb + (1 if sibling_by_core else 0)

    def body(*refs):
        x_refs, o_refs, send_sems, recv_sems = refs[:n_in], refs[n_in:n_in + n_out], refs[-2], refs[-1]
        mx, my, mc = lax.axis_index("x"), lax.axis_index("y"), lax.axis_index("c")
        me = 2 * mx + my
        copies = []
        for a in range(nc):
            for t, (px, py) in enumerate(((1 - mx, my), (mx, 1 - my), (1 - mx, 1 - my))):
                src = x_refs[a].at[2 * px + py] if a < ns else x_refs[a]
                copies.append(pltpu.make_async_remote_copy(
                    src_ref=src, dst_ref=o_refs[a].at[me], send_sem=send_sems.at[3 * a + t],
                    recv_sem=recv_sems.at[3 * a + t], device_id=(px, py, mc), device_id_type=pl.DeviceIdType.MESH))
        for b in range(nc, n):
            k = 3 * nc + b - nc
            copies.append(pltpu.make_async_remote_copy(
                src_ref=x_refs[b], dst_ref=o_refs[b].at[mc] if sibling_both else o_refs[b], send_sem=send_sems.at[k],
                recv_sem=recv_sems.at[k], device_id=(mx, my, 1 - mc), device_id_type=pl.DeviceIdType.MESH))
        for cp in copies:
            cp.start()
        if sibling_by_core:
            def pick(src):
                return pltpu.make_async_remote_copy(
                    src_ref=src, dst_ref=o_refs[n], send_sem=send_sems.at[n_sem - 1], recv_sem=recv_sems.at[n_sem - 1],
                    device_id=(mx, my, 1 - mc), device_id_type=pl.DeviceIdType.MESH)

            @pl.when(mc == 0)
            def _():
                pick(x_refs[n]).start()

            @pl.when(mc == 1)
            def _():
                pick(x_refs[n + 1]).start()

            copies.append(pick(x_refs[n]))
        for cp in copies:
            cp.wait()

    shapes = ([(4,) + tuple(x.shape[1:]) for x in scatter] + [(4,) + tuple(x.shape) for x in gather]
              + [((2,) if sibling_both else ()) + tuple(x.shape) for x in sibling])
    xs = chip_xs + sibling
    out_shape = [jax.ShapeDtypeStruct(s, x.dtype) for s, x in zip(shapes, xs)]
    if sibling_by_core:
        out_shape.append(jax.ShapeDtypeStruct(sibling_by_core[0].shape, sibling_by_core[0].dtype))
    outs = pl.pallas_call(
        body, name=name, in_specs=[_ANY] * n_in, out_specs=[_ANY] * n_out, out_shape=out_shape,
        scratch_shapes=[pltpu.SemaphoreType.DMA((n_sem,)), pltpu.SemaphoreType.DMA((n_sem,))],
    )(*xs, *(sibling_by_core or ()))
    me, c = 2 * lax.axis_index("x") + lax.axis_index("y"), lax.axis_index("c")
    fixed = []
    for a, (o, x) in enumerate(zip(outs, xs)):
        if a < ns:
            o = lax.dynamic_update_index_in_dim(o, lax.dynamic_index_in_dim(x, me, 0, keepdims=True), me, 0)
        elif a < nc:
            o = lax.dynamic_update_index_in_dim(o, x[None], me, 0)
        elif sibling_both:
            o = lax.dynamic_update_index_in_dim(o, x[None], c, 0)
        fixed.append(o)
    if sibling_by_core:
        fixed.append(outs[n])
    return fixed[:ns], fixed[ns:nc], fixed[nc:]


def _rows_tile(rows, row_bytes, budget=5 << 19):
    return next(t for t in (512, 256, 128, 64, 32, 16, 8) if rows % t == 0 and t * row_bytes <= budget)


def _padded_row_bytes(cols):
    return -(-cols // LANES) * LANES * 4


def _add2(a, b, name, out_dtype=f32):
    by_core = isinstance(a, (tuple, list))
    parts = list(a) if by_core else [a]
    R, C = b.shape
    tr = _rows_tile(R, _padded_row_bytes(C))

    def body(*refs):
        b_ref, o_ref = refs[-2], refs[-1]
        mine = jnp.where(lax.axis_index("c") == 0, refs[0][...], refs[1][...]) if by_core else refs[0][...]
        o_ref[...] = (mine + b_ref[...]).astype(out_dtype)

    spec = pl.BlockSpec((tr, C), lambda i: (i, 0))
    return _call(body, name, (R // tr,), [spec] * (len(parts) + 1), spec, jax.ShapeDtypeStruct((R, C), out_dtype),
                 sem=("parallel",))(*parts, b)


def _sum4(x, name):
    R = x.shape[1]
    tr = _tile(R, (2560, 1024, 512, 256, 128))

    def body(x_ref, o_ref):
        p = [x_ref[j].astype(f32) for j in range(4)]
        o_ref[...] = ((p[0] + p[1]) + p[2]) + p[3]

    return _call(body, name, (R // tr,), [pl.BlockSpec((4, tr, LANES), lambda i: (0, i, 0))],
                 pl.BlockSpec((tr, LANES), lambda i: (i, 0)), jax.ShapeDtypeStruct((R, LANES), f32),
                 sem=("parallel",))(x)


def _adamw(g_parts, w, m, v, name):
    stacked = not isinstance(g_parts, (tuple, list))
    k = g_parts.shape[0] if stacked else len(g_parts)
    R, C = w.shape
    tr = _rows_tile(R, _padded_row_bytes(C))

    def body(*refs):
        w_ref, m_ref, v_ref, g_ref, d_ref, nm_ref, nv_ref = refs[-7:]
        if stacked:
            g = refs[0][0].astype(f32)
            for j in range(1, k):
                g = g + refs[0][j].astype(f32)
        else:
            g = refs[0][...]
            for r in refs[1:k]:
                g = g + r[...]
        g_ref[...] = g
        d_ref[...], nm_ref[...], nv_ref[...] = _adamw_update(g, w_ref[...], m_ref[...], v_ref[...])

    spec = pl.BlockSpec((tr, C), lambda i: (i, 0))
    sd = jax.ShapeDtypeStruct((R, C), f32)
    g_specs = [pl.BlockSpec((k, tr, C), lambda i: (0, i, 0))] if stacked else [spec] * k
    g_args = [g_parts] if stacked else list(g_parts)
    return _call(body, name, (R // tr,), g_specs + [spec] * 3, [spec] * 4, [sd] * 4,
                 sem=("parallel",))(*g_args, w, m, v)


def _adamw_update(g, w, m, v):
    m = ADAM_B1 * m + (1.0 - ADAM_B1) * g
    v = ADAM_B2 * v + (1.0 - ADAM_B2) * (g * g)
    c1 = 1.0 - ADAM_B1 ** ADAM_STEP
    c2 = 1.0 - ADAM_B2 ** ADAM_STEP
    return -ADAM_LR * ((m / c1) / (jnp.sqrt(v / c2) + ADAM_EPS) + ADAM_WD * w), m, v


def _adamw_small(gs, ws, ms, vs, name):
    n = len(gs)

    def body(*refs):
        ins, outs = refs[:4 * n], refs[4 * n:]
        for t in range(n):
            d, m, v = _adamw_update(ins[t][...], ins[n + t][...], ins[2 * n + t][...], ins[3 * n + t][...])
            outs[t][...] = d
            outs[n + t][...] = m
            outs[2 * n + t][...] = v

    vmem = pl.BlockSpec(memory_space=pltpu.VMEM)
    outs = pl.pallas_call(
        body, name=name, in_specs=[vmem] * (4 * n), out_specs=[vmem] * (3 * n),
        out_shape=[jax.ShapeDtypeStruct(w.shape, f32) for w in ws] * 3,
        compiler_params=pltpu.CompilerParams(vmem_limit_bytes=V7X_VMEM_LIMIT))(*gs, *ws, *ms, *vs)
    return outs[:n], outs[n:2 * n], outs[2 * n:]


def _pack(arrays, row_multiple=PACK_ROWS):
    flat = jnp.concatenate([a.reshape(-1) for a in arrays])
    unit = row_multiple * LANES
    n = -(-flat.shape[0] // unit) * unit
    return jnp.pad(flat, (0, n - flat.shape[0])).reshape(n // LANES, LANES)


def _unpack(buf, shapes, lead=()):
    flat = buf.reshape(lead + (-1,))
    out, off = [], 0
    for s in shapes:
        n = 1
        for dim in s:
            n *= dim
        out.append(flat[..., off:off + n].reshape(lead + tuple(s)))
        off += n
    return out


def _to_shards(full, axis):
    s = full.shape
    t = full.reshape(s[:axis] + (4, s[axis] // 4) + s[axis + 1:])
    return jnp.moveaxis(t, axis, 0)


def _from_shards(sh, axis):
    t = jnp.moveaxis(sh, 0, axis)
    s = t.shape
    return t.reshape(s[:axis] + (s[axis] * s[axis + 1],) + s[axis + 2:])


def kernel(x, norm_w, w_in, s5_a_re, s5_a_im, s5_log_step, s5_b_re, s5_b_im, s5_c_re, s5_c_im, s5_d, s5_glu_w, s5_glu_b, q_norm_w, k_norm_w, conv_w, conv_b, dt_bias, ssd_a_log, ssd_d, ssd_norm_w, proj_a, proj_b, proj_c, w_out, loss_target, m_norm_w, m_w_in, m_s5_a_re, m_s5_a_im, m_s5_log_step, m_s5_b_re, m_s5_b_im, m_s5_c_re, m_s5_c_im, m_s5_d, m_s5_glu_w, m_s5_glu_b, m_q_norm_w, m_k_norm_w, m_conv_w, m_conv_b, m_dt_bias, m_ssd_a_log, m_ssd_d, m_ssd_norm_w, m_proj_a, m_proj_b, m_proj_c, m_w_out, v_norm_w, v_w_in, v_s5_a_re, v_s5_a_im, v_s5_log_step, v_s5_b_re, v_s5_b_im, v_s5_c_re, v_s5_c_im, v_s5_d, v_s5_glu_w, v_s5_glu_b, v_q_norm_w, v_k_norm_w, v_conv_w, v_conv_b, v_dt_bias, v_ssd_a_log, v_ssd_d, v_ssd_norm_w, v_proj_a, v_proj_b, v_proj_c, v_w_out):
    given = dict(locals())
    W = {n: given[n] for n in _WEIGHTS}
    M = {n: given["m_" + n] for n in _WEIGHTS}
    V = {n: given["v_" + n] for n in _WEIGHTS}
    n_layers = norm_w.shape[0]
    assert n_layers == 2
    c = lax.axis_index("c")

    mine_of = lambda t: lax.dynamic_index_in_dim(t, c, 0, keepdims=False)
    as_payload = lambda n: lax.bitcast_convert_type(W[n], bf16) if n == "conv_w" else W[n].astype(bf16)
    payload_shapes = [W[n].shape + ((2,) if n == "conv_w" else ()) for n, _ in _SHARDED]
    wpack = _pack([as_payload(n) for n, _ in _SHARDED])
    half_rows = wpack.shape[0] // 2
    _, (pack_half, w_in_mine_layer), _ = _exchange(
        "gather_weights", gather=[lax.dynamic_slice_in_dim(wpack, c * half_rows, half_rows),
                                  mine_of(w_in).astype(bf16)])
    _, _, (w_in_layers, pack_halves) = _exchange("share_weights", sibling=[w_in_mine_layer, pack_half],
                                                 sibling_both=True)
    gathered = jnp.moveaxis(pack_halves, 0, 1).reshape(4, 2 * half_rows, LANES)
    full = dict(W)
    pieces = _unpack(gathered.reshape(4, -1), payload_shapes, lead=(4,))
    for (n, axis), sh in zip(_SHARDED, pieces):
        full[n] = _from_shards(lax.bitcast_convert_type(sh, f32) if n == "conv_w" else sh, axis)

    qs, saves = [], []
    for l in range(n_layers):
        p = {n: full[n][l] for n in _WEIGHTS if n != "w_in"}
        p["w_in"] = [w_in_layers[l, k] for k in range(4)]
        qs.append((_prep_layer(p), p))
    (act, h), sv = layer_fwd(x[0], qs[0][0], "_l0", next_norm_w=qs[1][0]["norm_w"])
    saves.append(sv)
    (dact, lsum), sv = layer_fwd(act, qs[1][0], "_l1", h=h, target=loss_target[0])
    saves.append(sv)
    loss = lax.psum(lsum[0, 0], ("x", "y", "c"))
    layer_grads = [None] * n_layers
    for l in reversed(range(n_layers)):
        q, p = qs[l]
        dact, layer_grads[l] = layer_bwd(dact, saves[l], q, p, f"_l{l}")
    grad_x = dact[None]
    G = {n: jnp.stack([layer_grads[l][n] for l in range(n_layers)]) for n in _WEIGHTS if n != "w_in"}

    repl_shapes = [W[n].shape for n in _REPL]
    small = _pack([G[n] for n in _REPL], 4 * PACK_ROWS)
    quarter = small.shape[0] // 4
    big = [_to_shards(G[n], axis).reshape(4, -1) for n, axis in _SHARDED]
    big = jnp.concatenate(big, axis=1)
    unit = PACK_ROWS * LANES
    nbig = -(-big.shape[1] // unit) * unit
    big = jnp.pad(big, ((0, 0), (0, nbig - big.shape[1]))).reshape(4, nbig // LANES, LANES)
    gpack = jnp.concatenate([big, small.reshape(4, quarter, LANES)], axis=1)
    rbig = nbig // LANES
    g0, g1 = layer_grads[0]["w_in"], layer_grads[1]["w_in"]

    (landed_pack,), _, (from_sibling,) = _exchange(
        "swap_w_in_grads_and_scatter_grads", scatter=[gpack.astype(bf16)], sibling_by_core=(g1, g0))
    flat = lambda t: t.reshape(4 * D_MODEL, W_IN_SHARD)
    shards = _add2((flat(g0), flat(g1)), flat(from_sibling), "sum_cores_w_in", out_dtype=bf16)
    mine = _sum4(landed_pack, "sum_chips")

    (landed,), _, (other,) = _exchange(
        "scatter_w_in_grads_and_swap_cores", scatter=[shards.reshape(4, D_MODEL, W_IN_SHARD)], sibling=[mine])
    w_in_mine = _adamw(landed, mine_of(w_in), mine_of(m_w_in), mine_of(v_w_in), "adamw_w_in")
    gq = _add2(mine[rbig:], other[rbig:], "sum_cores_small")

    _, (gsmall,), w_in_out = _exchange(
        "share_w_in_updates_and_gather_small", gather=[gq], sibling=w_in_mine, sibling_both=True)
    gsmall = gsmall.reshape(4 * quarter, LANES)

    shard_shapes = [W[n].shape for n, _ in _SHARDED]
    g_mine, g_other = _unpack(mine[:rbig], shard_shapes), _unpack(other[:rbig], shard_shapes)
    rows_of = lambda t: t.reshape(-1, t.shape[-1])
    res = [dict(), dict(), dict(), dict()]
    for k, (n, _) in enumerate(_SHARDED):
        outs = _adamw((rows_of(g_mine[k]), rows_of(g_other[k])), rows_of(W[n]), rows_of(M[n]), rows_of(V[n]),
                      f"adamw_{n}")
        for kind in range(4):
            res[kind][n] = outs[kind].reshape(W[n].shape)
    g_small = _unpack(gsmall, repl_shapes)
    small_out = _adamw_small([rows_of(g) for g in g_small], *([rows_of(T[n]) for n in _REPL] for T in (W, M, V)),
                             "adamw_replicated")
    for kind in range(4):
        res[kind]["w_in"] = w_in_out[kind]
        for k, n in enumerate(_REPL):
            res[kind][n] = g_small[k] if kind == 0 else small_out[kind - 1][k].reshape(W[n].shape)
    return (loss, grad_x, *[res[0][n] for n in _WEIGHTS], *[res[1][n] for n in _WEIGHTS],
            *[res[2][n] for n in _WEIGHTS], *[res[3][n] for n in _WEIGHTS])
```

```python
import functools

import jax
import jax.numpy as jnp
from jax import lax
from jax.experimental import pallas as pl
from jax.experimental.pallas import tpu as pltpu

f32 = jnp.float32
bf16 = jnp.bfloat16

D_MODEL = 1024
RMS_EPS = 1e-6
V7X_VMEM_LIMIT = 60 * 1024 * 1024
LANES = 128
NN, NT, TN = ((1,), (0,)), ((1,), (1,)), ((0,), (0,))

S5_STATES = 2048
S5_ROWS = 512
ATT_SEG = 2048
ATT_BLOCK = 128
SSD_CHUNK = 128
SSD_CHUNKS_PER_STEP = 2
SSD_WIDTH = 768
SSD_XBC = 1280
CONV_ROWS = 512
TAIL_ROWS = 256

ADAM_LR, ADAM_B1, ADAM_B2, ADAM_EPS, ADAM_WD, ADAM_STEP = 0.001, 0.9, 0.999, 1e-08, 0.01, 10

_C_UA, _C_ZA, _C_Q, _C_K, _C_V, _C_ZB, _C_XBC, _C_DT, _C_ZC, _C_GATE, _C_END = (
    0, 512, 1024, 1792, 2560, 3328, 3584, 4864, 4876, 5644, 8716)

_SHARDED = (("s5_glu_w", 1), ("conv_w", 2), ("proj_a", 2), ("proj_b", 2), ("proj_c", 2), ("w_out", 1))
W_IN_SHARD = 2179
_REPL = ("norm_w", "s5_a_re", "s5_a_im", "s5_log_step", "s5_b_re", "s5_b_im", "s5_c_re", "s5_c_im", "s5_d",
         "s5_glu_b", "q_norm_w", "k_norm_w", "conv_b", "dt_bias", "ssd_a_log", "ssd_d", "ssd_norm_w")
_WEIGHTS = ("norm_w", "w_in", "s5_a_re", "s5_a_im", "s5_log_step", "s5_b_re", "s5_b_im", "s5_c_re", "s5_c_im",
            "s5_d", "s5_glu_w", "s5_glu_b", "q_norm_w", "k_norm_w", "conv_w", "conv_b", "dt_bias", "ssd_a_log",
            "ssd_d", "ssd_norm_w", "proj_a", "proj_b", "proj_c", "w_out")
PACK_ROWS = 512


def _dot(a, b, dims):
    return lax.dot_general(a.astype(bf16), b.astype(bf16), (dims, ((), ())), preferred_element_type=f32)


_ANY = pl.BlockSpec(memory_space=pl.ANY)

MAIN_WIDTH = 8448
MAIN_SSD_BLOCK = 3
MAIN_DT_BLOCK = 46
MAIN_ATT_BLOCK = 16


def _call(body, name, grid, in_specs, out_specs, out_shape, scratch=(), sem=None, aliases=None):
    return pl.pallas_call(
        body, name=name, grid=grid, in_specs=in_specs, out_specs=out_specs, out_shape=out_shape,
        scratch_shapes=list(scratch), input_output_aliases=aliases or {},
        compiler_params=pltpu.CompilerParams(dimension_semantics=sem, vmem_limit_bytes=V7X_VMEM_LIMIT))


def _tile(n, options=(1024, 768, 512, 384, 256, 128)):
    return next(t for t in options if n % t == 0)


@functools.partial(jax.custom_vjp, nondiff_argnums=(2,))
def _bdot(a, b, dims):
    return _dot(a, b, dims)


def _bdot_fwd(a, b, dims):
    return _dot(a, b, dims), (a, b)


def _bdot_bwd(dims, res, g):
    a, b = res
    if dims == NN:
        da, db = _dot(g, b, NT), _dot(a, g, TN)
    elif dims == NT:
        da, db = _dot(g, b, NN), _dot(g, a, TN)
    else:
        da, db = _dot(b, g, NT), _dot(a, g, NN)
    return da.astype(a.dtype), db.astype(b.dtype)


_bdot.defvjp(_bdot_fwd, _bdot_bwd)


@functools.partial(jax.custom_vjp, nondiff_argnums=(2,))
def _cdot(a, w, dims):
    return _dot(a, w, dims)


def _cdot_fwd(a, w, dims):
    return _dot(a, w, dims), w


def _cdot_bwd(dims, w, g):
    da = _dot(g, w, NT) if dims == NN else _dot(g, w, NN)
    return da, jnp.zeros_like(w)


_cdot.defvjp(_cdot_fwd, _cdot_bwd)


def _split3(x):
    hi = x.astype(bf16)
    r = x - hi.astype(f32)
    mid = r.astype(bf16)
    lo = (r - mid.astype(f32)).astype(bf16)
    return hi, mid, lo


@jax.custom_vjp
def _xdot_l(m, x):
    return sum(_dot(m, p, NN) for p in _split3(x))


def _xdot_l_fwd(m, x):
    return _xdot_l(m, x), m


def _xdot_l_bwd(m, g):
    return jnp.zeros_like(m), sum(_dot(m, p, TN) for p in _split3(g))


_xdot_l.defvjp(_xdot_l_fwd, _xdot_l_bwd)


@jax.custom_vjp
def _softplus(x):
    e = jnp.exp(-jnp.abs(x))
    u = 1.0 + e
    log1p = jnp.where(u == 1.0, e, jnp.log(u) * (e / jnp.where(u == 1.0, 1.0, u - 1.0)))
    return jnp.maximum(x, 0.0) + log1p


def _softplus_fwd(x):
    return _softplus(x), x


def _softplus_bwd(x, g):
    return (g * jax.nn.sigmoid(x),)


_softplus.defvjp(_softplus_fwd, _softplus_bwd)


def _rms(x, w):
    return x * lax.rsqrt(jnp.mean(x * x, axis=-1, keepdims=True) + RMS_EPS) * w


def mm_nn(a, b, name, tm=2048):
    M, K = a.shape
    N = b.shape[1]
    tn = _tile(N)

    def body(a_ref, b_ref, o_ref):
        o_ref[...] = _dot(a_ref[...], b_ref[...], NN)

    return _call(body, name, (M // tm, N // tn),
                 [pl.BlockSpec((tm, K), lambda i, j: (i, 0)), pl.BlockSpec((K, tn), lambda i, j: (0, j))],
                 pl.BlockSpec((tm, tn), lambda i, j: (i, j)), jax.ShapeDtypeStruct((M, N), f32),
                 sem=("parallel", "parallel"))(a, b)


def mm_nt(a, b, name, tm=1024):
    M, K = a.shape
    N = b.shape[0]
    tk = _tile(K, (2816, 1024, 768, 512, 256, 128))

    def body(a_ref, b_ref, o_ref):
        k = pl.program_id(1)
        p = _dot(a_ref[...], b_ref[...], NT)

        @pl.when(k == 0)
        def _():
            o_ref[...] = p

        @pl.when(k > 0)
        def _():
            o_ref[...] += p

    return _call(body, name, (M // tm, K // tk),
                 [pl.BlockSpec((tm, tk), lambda i, k: (i, k)), pl.BlockSpec((N, tk), lambda i, k: (0, k))],
                 pl.BlockSpec((tm, N), lambda i, k: (i, 0)), jax.ShapeDtypeStruct((M, N), f32),
                 sem=("parallel", "arbitrary"))(a, b)


def mm_tn(a, b, name, tk=2048):
    K, M = a.shape
    N = b.shape[1]
    tn = _tile(N)

    def body(a_ref, b_ref, o_ref):
        k = pl.program_id(1)
        p = _dot(a_ref[...], b_ref[...], TN)

        @pl.when(k == 0)
        def _():
            o_ref[...] = p

        @pl.when(k > 0)
        def _():
            o_ref[...] += p

    return _call(body, name, (N // tn, K // tk),
                 [pl.BlockSpec((tk, M), lambda j, k: (k, 0)), pl.BlockSpec((tk, tn), lambda j, k: (k, j))],
                 pl.BlockSpec((M, tn), lambda j, k: (0, j)), jax.ShapeDtypeStruct((M, N), f32),
                 sem=("parallel", "arbitrary"))(a, b)


def rms_fwd(x, w, name, tm=512):
    S = x.shape[0]

    def body(x_ref, w_ref, o_ref):
        o_ref[...] = _rms(x_ref[...], w_ref[...]).astype(bf16)

    return _call(body, name, (S // tm,),
                 [pl.BlockSpec((tm, D_MODEL), lambda i: (i, 0)), pl.BlockSpec((1, D_MODEL), lambda i: (0, 0))],
                 pl.BlockSpec((tm, D_MODEL), lambda i: (i, 0)), jax.ShapeDtypeStruct((S, D_MODEL), bf16),
                 sem=("parallel",))(x, w)


def mm_nt_rms_bwd(a, b, acc, x, w, dres, name, tm=1024):
    S, K = a.shape

    def body(a_ref, b_ref, acc_ref, x_ref, w_ref, dr_ref, dx_ref, dw_ref):
        dh = _dot(a_ref[...], b_ref[...], NT) + acc_ref[...]
        _, vjp = jax.vjp(_rms, x_ref[...], w_ref[...])
        dx, dw = vjp(dh)
        dx_ref[...] = dx + dr_ref[...]

        @pl.when(pl.program_id(0) == 0)
        def _():
            dw_ref[...] = dw

        @pl.when(pl.program_id(0) > 0)
        def _():
            dw_ref[...] += dw

    row = pl.BlockSpec((tm, D_MODEL), lambda i: (i, 0))
    vec = pl.BlockSpec((1, D_MODEL), lambda i: (0, 0))
    return _call(body, name, (S // tm,),
                 [pl.BlockSpec((tm, K), lambda i: (i, 0)), pl.BlockSpec((D_MODEL, K), lambda i: (0, 0)), row, row, vec,
                  row], [row, vec],
                 [jax.ShapeDtypeStruct((S, D_MODEL), f32), jax.ShapeDtypeStruct((1, D_MODEL), f32)],
                 sem=("arbitrary",))(a, b, acc, x, w, dres)


def _s5_discretize(a_re, a_im, log_step, b_re, b_im, c_re, c_im):
    step = jnp.exp(log_step)[:, None]
    mag = jnp.exp(a_re * step)
    ang = a_im * step
    lam_re, lam_im = mag * jnp.cos(ang), mag * jnp.sin(ang)
    num_re, num_im = lam_re - 1.0, lam_im
    den = a_re * a_re + a_im * a_im
    f_re = (num_re * a_re + num_im * a_im) / den
    f_im = (num_im * a_re - num_re * a_im) / den
    bb_re = f_re[..., None] * b_re - f_im[..., None] * b_im
    bb_im = f_re[..., None] * b_im + f_im[..., None] * b_re
    eye = jnp.eye(8, dtype=f32)

    def block_in(bb):
        t = bb.transpose(0, 2, 1).reshape(4, 8, 16, 1, 64)
        return (t * eye[None, :, None, :, None]).reshape(4, 128, 512)

    def block_out(c):
        t = c.transpose(0, 2, 1).reshape(4, 8, 64, 1, 16)
        return (t * eye[None, :, None, :, None]).reshape(4, 512, 128)

    return (lam_re.reshape(1, S5_STATES), lam_im.reshape(1, S5_STATES), block_in(bb_re), block_in(bb_im),
            block_out(c_re), block_out(c_im))


def _lam_powers(lam_re, lam_im):
    rows_re, rows_im = [lam_re], [lam_im]
    for _ in range(7):
        pr, pi = rows_re[-1], rows_im[-1]
        rows_re.append(pr * lam_re - pi * lam_im)
        rows_im.append(pr * lam_im + pi * lam_re)
    return jnp.concatenate(rows_re, 0), jnp.concatenate(rows_im, 0)


def s5_fwd(u, pw_re, pw_im, w_re, w_im, c_re, c_im, dvec, name):
    S = u.shape[0]
    R, NS = S5_ROWS, S5_STATES
    nb = R // 8

    def body(u_ref, pwr_ref, pwi_ref, wre_ref, wim_ref, cre_ref, cim_ref, d_ref, y_ref, hr_ref, hi_ref,
             car_re, car_im, cin_re, cin_im, up, yp):
        @pl.when(pl.program_id(0) == 0)
        def _():
            car_re[...] = jnp.zeros_like(car_re)
            car_im[...] = jnp.zeros_like(car_im)

        slab = lambda r: pl.ds(r * nb, nb)
        for r in range(8):
            up[slab(r), :] = u_ref[:, r, :]
        u = up[...]
        for j in range(4):
            uj = u[:, 128 * j:128 * (j + 1)]
            hr_ref[:, 512 * j:512 * (j + 1)] = _dot(uj, wre_ref[j], NN)
            hi_ref[:, 512 * j:512 * (j + 1)] = _dot(uj, wim_ref[j], NN)
        lr, li = pwr_ref[0:1, :], pwi_ref[0:1, :]
        for r in range(1, 8):
            pr, pi = hr_ref[slab(r - 1), :], hi_ref[slab(r - 1), :]
            hr_ref[slab(r), :] = lr * pr - li * pi + hr_ref[slab(r), :]
            hi_ref[slab(r), :] = lr * pi + li * pr + hi_ref[slab(r), :]
        l8r, l8i = pwr_ref[7:8, :], pwi_ref[7:8, :]

        def across(c, carry):
            gr, gi = carry
            cin_re[pl.ds(c, 1), :] = gr
            cin_im[pl.ds(c, 1), :] = gi
            er, ei = hr_ref[pl.ds(7 * nb + c, 1), :], hi_ref[pl.ds(7 * nb + c, 1), :]
            return l8r * gr - l8i * gi + er, l8r * gi + l8i * gr + ei

        gr, gi = lax.fori_loop(0, nb, across, (car_re[...], car_im[...]))
        car_re[...] = gr
        car_im[...] = gi
        cr, ci = cin_re[...], cin_im[...]
        for r in range(8):
            pr, pi = pwr_ref[r:r + 1, :], pwi_ref[r:r + 1, :]
            hr_ref[slab(r), :] = hr_ref[slab(r), :] + pr * cr - pi * ci
            hi_ref[slab(r), :] = hi_ref[slab(r), :] + pr * ci + pi * cr
        for j in range(4):
            sl = slice(512 * j, 512 * (j + 1))
            cs = slice(128 * j, 128 * (j + 1))
            yp[:, cs] = (_dot(hr_ref[:, sl], cre_ref[j], NN) - _dot(hi_ref[:, sl], cim_ref[j], NN)
                         + d_ref[:, cs] * u[:, cs])
        for r in range(8):
            y_ref[:, r, :] = yp[slab(r), :]

    full = lambda shape: pl.BlockSpec(shape, lambda i: (0,) * len(shape))
    hspec = pl.BlockSpec((R, NS), lambda i: (i, 0))
    uspec = pl.BlockSpec((nb, 8, 512), lambda i: (i, 0, 0))
    y, h_re, h_im = _call(
        body, name, (S // R,),
        [uspec, full((8, NS)), full((8, NS)), full((4, 128, 512)),
         full((4, 128, 512)), full((4, 512, 128)), full((4, 512, 128)), full((1, 512))],
        [uspec, hspec, hspec],
        [jax.ShapeDtypeStruct((S // 8, 8, 512), f32), jax.ShapeDtypeStruct((S, NS), f32),
         jax.ShapeDtypeStruct((S, NS), f32)],
        scratch=[pltpu.VMEM((1, NS), f32), pltpu.VMEM((1, NS), f32), pltpu.VMEM((nb, NS), f32),
                 pltpu.VMEM((nb, NS), f32), pltpu.VMEM((R, 512), f32), pltpu.VMEM((R, 512), f32)],
        sem=("arbitrary",))(u.reshape(S // 8, 8, 512), pw_re, pw_im, w_re.astype(bf16), w_im.astype(bf16),
                            c_re.astype(bf16), c_im.astype(bf16), dvec)
    return y.reshape(S, 512), h_re, h_im


def s5_bwd(dy, u, h_re, h_im, pw_re, pw_im, w_re, w_im, c_re, c_im, dvec, name):
    S = u.shape[0]
    R, NS = S5_ROWS, S5_STATES
    nb = R // 8
    nchunk = S // R

    def body(dy_ref, u_ref, hr_ref, hi_ref, hpr_ref, hpi_ref, pwr_ref, pwi_ref, wre_ref, wim_ref, cre_ref, cim_ref,
             d_ref, du_ref, dwre_ref, dwim_ref, dcre_ref, dcim_ref, dlr_ref, dli_ref, dd_ref,
             ar, ai, car_re, car_im, cin_re, cin_im, up, dyp, dup):
        i = pl.program_id(0)

        @pl.when(i == 0)
        def _():
            for ref in (car_re, car_im, dwre_ref, dwim_ref, dcre_ref, dcim_ref, dlr_ref, dli_ref, dd_ref):
                ref[...] = jnp.zeros_like(ref)

        slab = lambda r: pl.ds(r * nb, nb)
        for r in range(8):
            up[slab(r), :] = u_ref[:, r, :]
            dyp[slab(r), :] = dy_ref[:, r, :]
        dy = dyp[...]
        u = up[...]
        for j in range(4):
            dyj = dy[:, 128 * j:128 * (j + 1)]
            ar[:, 512 * j:512 * (j + 1)] = _dot(dyj, cre_ref[j], NT)
            ai[:, 512 * j:512 * (j + 1)] = -_dot(dyj, cim_ref[j], NT)
        lr, li = pwr_ref[0:1, :], pwi_ref[0:1, :]
        for r in range(6, -1, -1):
            nr, ni = ar[slab(r + 1), :], ai[slab(r + 1), :]
            ar[slab(r), :] = lr * nr + li * ni + ar[slab(r), :]
            ai[slab(r), :] = lr * ni - li * nr + ai[slab(r), :]
        l8r, l8i = pwr_ref[7:8, :], pwi_ref[7:8, :]

        def across(k, carry):
            c = nb - 1 - k
            gr, gi = carry
            cin_re[pl.ds(c, 1), :] = gr
            cin_im[pl.ds(c, 1), :] = gi
            er, ei = ar[pl.ds(c, 1), :], ai[pl.ds(c, 1), :]
            return l8r * gr + l8i * gi + er, l8r * gi - l8i * gr + ei

        gr, gi = lax.fori_loop(0, nb, across, (car_re[...], car_im[...]))
        car_re[...] = gr
        car_im[...] = gi
        cr, ci = cin_re[...], cin_im[...]
        for r in range(8):
            pr, pi = pwr_ref[7 - r:8 - r, :], pwi_ref[7 - r:8 - r, :]
            ar[slab(r), :] = ar[slab(r), :] + pr * cr + pi * ci
            ai[slab(r), :] = ai[slab(r), :] + pr * ci - pi * cr

        acc_r = jnp.zeros((1, NS), f32)
        acc_i = jnp.zeros((1, NS), f32)
        has_prev = (i < nchunk - 1).astype(f32)
        top = lax.broadcasted_iota(jnp.int32, (nb, NS), 0) == 0
        for r in range(8):
            if r == 0:
                xr = jnp.where(top, hpr_ref[7:8, :] * has_prev, pltpu.roll(hr_ref[slab(7), :], 1, 0))
                xi = jnp.where(top, hpi_ref[7:8, :] * has_prev, pltpu.roll(hi_ref[slab(7), :], 1, 0))
            else:
                xr, xi = hr_ref[slab(r - 1), :], hi_ref[slab(r - 1), :]
            br, bi = ar[slab(r), :], ai[slab(r), :]
            acc_r += jnp.sum(br * xr + bi * xi, axis=0, keepdims=True)
            acc_i += jnp.sum(bi * xr - br * xi, axis=0, keepdims=True)
        dlr_ref[...] += acc_r
        dli_ref[...] += acc_i
        dd_ref[...] += jnp.sum(dy * u, axis=0, keepdims=True)

        for j in range(4):
            sl = slice(512 * j, 512 * (j + 1))
            cs = slice(128 * j, 128 * (j + 1))
            arj, aij = ar[:, sl], ai[:, sl]
            uj, dyj = u[:, cs], dy[:, cs]
            dup[:, cs] = _dot(arj, wre_ref[j], NT) + _dot(aij, wim_ref[j], NT) + d_ref[:, cs] * dyj
            dwre_ref[j] += _dot(uj, arj, TN)
            dwim_ref[j] += _dot(uj, aij, TN)
            dcre_ref[j] += _dot(hr_ref[:, sl], dyj, TN)
            dcim_ref[j] -= _dot(hi_ref[:, sl], dyj, TN)
        for r in range(8):
            du_ref[:, r, :] = dup[slab(r), :]

    rev = lambda i: nchunk - 1 - i
    full = lambda shape: pl.BlockSpec(shape, lambda i: (0,) * len(shape))
    row = pl.BlockSpec((nb, 8, 512), lambda i: (rev(i), 0, 0))
    hspec = pl.BlockSpec((R, NS), lambda i: (rev(i), 0))
    hprev = pl.BlockSpec((8, NS), lambda i: (jnp.maximum(rev(i) * nb - 1, 0), 0))
    outs = _call(
        body, name, (nchunk,),
        [row, row, hspec, hspec, hprev, hprev, full((8, NS)), full((8, NS)), full((4, 128, 512)), full((4, 128, 512)),
         full((4, 512, 128)), full((4, 512, 128)), full((1, 512))],
        [row, full((4, 128, 512)), full((4, 128, 512)), full((4, 512, 128)), full((4, 512, 128)),
         full((1, NS)), full((1, NS)), full((1, 512))],
        [jax.ShapeDtypeStruct((S // 8, 8, 512), f32), jax.ShapeDtypeStruct((4, 128, 512), f32),
         jax.ShapeDtypeStruct((4, 128, 512), f32), jax.ShapeDtypeStruct((4, 512, 128), f32),
         jax.ShapeDtypeStruct((4, 512, 128), f32), jax.ShapeDtypeStruct((1, NS), f32),
         jax.ShapeDtypeStruct((1, NS), f32), jax.ShapeDtypeStruct((1, 512), f32)],
        scratch=[pltpu.VMEM((R, NS), f32), pltpu.VMEM((R, NS), f32), pltpu.VMEM((1, NS), f32),
                 pltpu.VMEM((1, NS), f32), pltpu.VMEM((nb, NS), f32), pltpu.VMEM((nb, NS), f32),
                 pltpu.VMEM((R, 512), f32), pltpu.VMEM((R, 512), f32), pltpu.VMEM((R, 512), f32)],
        sem=("arbitrary",))(dy.reshape(S // 8, 8, 512), u.reshape(S // 8, 8, 512), h_re, h_im, h_re, h_im, pw_re,
                            pw_im, w_re.astype(bf16), w_im.astype(bf16), c_re.astype(bf16), c_im.astype(bf16), dvec)
    return (outs[0].reshape(S, 512),) + tuple(outs[1:])


def _rows(start, n, d):
    return pl.ds(pl.multiple_of(start, ATT_BLOCK), n) if d == 1 else pl.ds(start, n, stride=d)


def _head_masks():
    lane = lax.broadcasted_iota(jnp.int32, (1, LANES), 1)
    return [(lane < 64).astype(f32), (lane >= 64).astype(f32)]


def _head_norm(x, w, hm):
    x2 = x * x
    r = [lax.rsqrt(jnp.sum(x2 * hm[h], axis=-1, keepdims=True) * (1.0 / 64) + RMS_EPS) for h in range(2)]
    sc = hm[0] * r[0] + hm[1] * r[1]
    return x * sc * w, sc, r


def _head_norm_bwd(x, w, sc, r, dxn, hm):
    dw = jnp.sum(dxn * x * sc, axis=0, keepdims=True)
    t = dxn * w
    tx = t * x
    corr = sum(hm[h] * (r[h] * r[h] * r[h]) * jnp.sum(tx * hm[h], axis=-1, keepdims=True) for h in range(2))
    return t * sc - x * corr * (1.0 / 64), dw


def _att_mask(has_prev):
    qi = lax.broadcasted_iota(jnp.int32, (ATT_BLOCK, 2 * ATT_BLOCK), 0) + ATT_BLOCK
    kj = lax.broadcasted_iota(jnp.int32, (ATT_BLOCK, 2 * ATT_BLOCK), 1)
    return (qi - kj >= 0) & (qi - kj <= ATT_BLOCK) & (has_prev | (kj >= ATT_BLOCK))


def _att_block_bwd(q, k, v, o, lse, do, dlse, qw, kw, has_prev):
    hm = _head_masks()
    mask = _att_mask(has_prev)
    qn, qsc, qr = _head_norm(q, qw, hm)
    kn, ksc, kr = _head_norm(k, kw, hm)
    dqn = jnp.zeros((ATT_BLOCK, LANES), f32)
    dkn = jnp.zeros((2 * ATT_BLOCK, LANES), f32)
    dv = jnp.zeros((2 * ATT_BLOCK, LANES), f32)
    for h in range(2):
        qh, do_h = qn * hm[h], do * hm[h]
        s = _dot(qh, kn, NT) * 0.125
        p = jnp.exp(jnp.where(mask, s - lse[:, 64 * h:64 * h + 1], -jnp.inf))
        dp = _dot(do_h, v, NT)
        delta = jnp.sum(do_h * o, axis=-1, keepdims=True)
        dl = jnp.sum(dlse * hm[h], axis=-1, keepdims=True)
        ds = p * (dp - delta + dl) * 0.125
        dqn = dqn + hm[h] * _dot(ds, kn, NN)
        dkn = dkn + _dot(ds, qh, TN)
        dv = dv + _dot(p, do_h, TN)
    dq, dqw = _head_norm_bwd(q, qw, qsc, qr, dqn, hm)
    dk, dkw = _head_norm_bwd(k, kw, ksc, kr, dkn, hm)
    return dq, dk, dv, dqw, dkw


def _att_block(q, k, v, qw, kw, has_prev):
    hm = _head_masks()
    qn, kn = _head_norm(q, qw, hm)[0], _head_norm(k, kw, hm)[0]
    mask = _att_mask(has_prev)
    o = jnp.zeros((ATT_BLOCK, LANES), f32)
    lse = jnp.zeros((ATT_BLOCK, LANES), f32)
    for h in range(2):
        s = _bdot(qn * hm[h], kn, NT) * 0.125
        s = jnp.where(mask, s, -jnp.inf)
        m = jnp.max(s, axis=-1, keepdims=True)
        p = jnp.exp(s - m)
        l = jnp.sum(p, axis=-1, keepdims=True)
        o = o + hm[h] * _bdot(p / l, v, NN)
        lse = lse + hm[h] * (m + jnp.log(l))
    return o, lse


def att_fwd(p_att, qw, kw, d, g, name):
    S = p_att.shape[0]
    SEG = ATT_SEG
    nblk = SEG // ATT_BLOCK

    def body(p_ref, qw_ref, kw_ref, o_ref, l_ref, q_s, k_ext, v_ext, o_s, l_s):
        seg = pl.program_id(1)

        @pl.when(seg == 0)
        def _():
            k_ext[SEG:, :] = jnp.zeros((SEG, LANES), f32)
            v_ext[SEG:, :] = jnp.zeros((SEG, LANES), f32)

        k_ext[:SEG, :] = k_ext[SEG:, :]
        v_ext[:SEG, :] = v_ext[SEG:, :]
        q_s[...] = p_ref[:, 0:128]
        k_ext[SEG:, :] = p_ref[:, 128:256]
        v_ext[SEG:, :] = p_ref[:, 256:384]
        qw_v, kw_v = qw_ref[...], kw_ref[...]

        def blk(b, carry):
            j, r = b // d, b % d
            qs = j * (ATT_BLOCK * d) + r
            ks = SEG + qs - ATT_BLOCK * d
            o, lse = _att_block(q_s[_rows(qs, ATT_BLOCK, d), :], k_ext[_rows(ks, 2 * ATT_BLOCK, d), :],
                                v_ext[_rows(ks, 2 * ATT_BLOCK, d), :], qw_v, kw_v, (seg > 0) | (j > 0))
            o_s[_rows(qs, ATT_BLOCK, d), :] = o
            l_s[_rows(qs, ATT_BLOCK, d), :] = lse
            return carry

        lax.fori_loop(0, nblk, blk, 0, unroll=True)
        o_ref[...] = o_s[...]
        l_ref[...] = l_s[...]

    vec = pl.BlockSpec((1, LANES), lambda hh, s: (0, 0))
    out = pl.BlockSpec((SEG, LANES), lambda hh, s: (s, hh))
    return _call(body, name, (2, S // SEG), [pl.BlockSpec((SEG, 384), lambda hh, s: (s, MAIN_ATT_BLOCK + 2 * g + hh)), vec, vec],
                 [out, out], [jax.ShapeDtypeStruct((S, 256), f32), jax.ShapeDtypeStruct((S, 256), f32)],
                 scratch=[pltpu.VMEM((SEG, LANES), f32), pltpu.VMEM((2 * SEG, LANES), f32),
                          pltpu.VMEM((2 * SEG, LANES), f32), pltpu.VMEM((SEG, LANES), f32),
                          pltpu.VMEM((SEG, LANES), f32)],
                 sem=("arbitrary", "arbitrary"))(p_att, qw, kw)


def att_bwd(p_att, o, lse, do, dlse, qw, kw, d, g, dp_main, name):
    S = p_att.shape[0]
    SEG = ATT_SEG
    nseg = S // SEG
    nblk = SEG // ATT_BLOCK

    def body(p_ref, pp_ref, o_ref, l_ref, do_ref, dl_ref, qw_ref, kw_ref, _, dp_ref, dqw_ref, dkw_ref,
             q_s, k_ext, v_ext, dq_s, dk_ext, dv_ext):
        hh, i = pl.program_id(0), pl.program_id(1)
        seg = nseg - 1 - i

        @pl.when(i == 0)
        def _():
            dk_ext[...] = jnp.zeros_like(dk_ext)
            dv_ext[...] = jnp.zeros_like(dv_ext)

        @pl.when((i == 0) & (hh == 0))
        def _():
            dqw_ref[...] = jnp.zeros_like(dqw_ref)
            dkw_ref[...] = jnp.zeros_like(dkw_ref)

        dk_ext[SEG:, :] = dk_ext[:SEG, :]
        dv_ext[SEG:, :] = dv_ext[:SEG, :]
        dk_ext[:SEG, :] = jnp.zeros((SEG, LANES), f32)
        dv_ext[:SEG, :] = jnp.zeros((SEG, LANES), f32)
        q_s[...] = p_ref[:, 0:128]
        k_ext[SEG:, :] = p_ref[:, 128:256]
        v_ext[SEG:, :] = p_ref[:, 256:384]
        k_ext[:SEG, :] = pp_ref[:, 128:256]
        v_ext[:SEG, :] = pp_ref[:, 256:384]
        qw_v, kw_v = qw_ref[...], kw_ref[...]

        per_step = 4

        def blk_group(i2, carry):
            dqw, dkw = carry
            done = []
            for u in range(per_step):
                b = per_step * i2 + u
                j, r = b // d, b % d
                qs = j * (ATT_BLOCK * d) + r
                ks = SEG + qs - ATT_BLOCK * d
                has_prev = (seg > 0) | (j > 0)
                qrows, krows = _rows(qs, ATT_BLOCK, d), _rows(ks, 2 * ATT_BLOCK, d)
                dq, dk, dv, dqw_b, dkw_b = _att_block_bwd(
                    q_s[qrows, :], k_ext[krows, :], v_ext[krows, :], o_ref[qrows, :], l_ref[qrows, :],
                    do_ref[qrows, :], dl_ref[qrows, :], qw_v, kw_v, has_prev)
                dqw, dkw = dqw + dqw_b, dkw + dkw_b
                done.append((qrows, krows, dq, dk, dv))
            for qrows, krows, dq, dk, dv in done:
                dq_s[qrows, :] = dq
                dk_ext[krows, :] = dk_ext[krows, :] + dk
                dv_ext[krows, :] = dv_ext[krows, :] + dv
            return dqw, dkw

        zero = jnp.zeros((1, LANES), f32)
        dqw, dkw = lax.fori_loop(0, nblk // per_step, blk_group, (zero, zero))
        dqw_ref[...] += dqw
        dkw_ref[...] += dkw
        dp_ref[:, 0:128] = dq_s[...].astype(bf16)
        dp_ref[:, 128:256] = dk_ext[SEG:, :].astype(bf16)
        dp_ref[:, 256:384] = dv_ext[SEG:, :].astype(bf16)

    rev = lambda i: nseg - 1 - i
    vec = pl.BlockSpec((1, LANES), lambda hh, i: (0, 0))
    blk = MAIN_ATT_BLOCK + 2 * g
    cur = pl.BlockSpec((SEG, 384), lambda hh, i: (rev(i), blk + hh))
    prev = pl.BlockSpec((SEG, 384), lambda hh, i: (jnp.maximum(rev(i) - 1, 0), blk + hh))
    col = pl.BlockSpec((SEG, LANES), lambda hh, i: (rev(i), hh))
    big = pltpu.VMEM((2 * SEG, LANES), f32)
    one = pltpu.VMEM((SEG, LANES), f32)
    return _call(body, name, (2, nseg), [cur, prev, col, col, col, col, vec, vec, _ANY], [cur, vec, vec],
                 [jax.ShapeDtypeStruct((S, MAIN_WIDTH), bf16), jax.ShapeDtypeStruct((1, LANES), f32),
                  jax.ShapeDtypeStruct((1, LANES), f32)],
                 scratch=[one, big, big, one, big, big], sem=("arbitrary", "arbitrary"),
                 aliases={8: 0})(p_att, p_att, o, lse, do, dlse, qw, kw, dp_main)


def conv_fwd(p_ssd, conv_w, conv_b, name):
    S = p_ssd.shape[0]
    tm, C = CONV_ROWS, SSD_XBC

    def body(x_ref, xp_ref, w_ref, b_ref, o_ref):
        first = (pl.program_id(0) == 0)
        ext = jnp.concatenate([jnp.where(first, 0.0, xp_ref[:, 0:C]), x_ref[:, 0:C]], axis=0)
        acc = b_ref[...] + w_ref[3:4, :] * ext[8:, :]
        for k in range(1, 4):
            acc = acc + w_ref[3 - k:4 - k, :] * pltpu.roll(ext, k, 0)[8:, :]
        o_ref[...] = jax.nn.silu(acc)

    return _call(body, name, (S // tm,),
                 [pl.BlockSpec((tm, 1536), lambda i: (i, MAIN_SSD_BLOCK)),
                  pl.BlockSpec((8, 1536), lambda i: (jnp.maximum(i * (tm // 8) - 1, 0), MAIN_SSD_BLOCK)),
                  pl.BlockSpec((4, C), lambda i: (0, 0)), pl.BlockSpec((1, C), lambda i: (0, 0))],
                 pl.BlockSpec((tm, C), lambda i: (i, 0)), jax.ShapeDtypeStruct((S, C), f32),
                 sem=("parallel",))(p_ssd, p_ssd, conv_w, conv_b)


def conv_bwd(p_ssd, dact, ddt, conv_w, conv_b, dp_main, name):
    S = p_ssd.shape[0]
    tm, C = CONV_ROWS, SSD_XBC
    nblk = S // tm

    def body(x_ref, xp_ref, xn_ref, da_ref, dan_ref, ddt_ref, w_ref, b_ref, _, dp_ref, dw_ref, db_ref):
        i = pl.program_id(0)
        rows = tm + 8
        ext = jnp.concatenate([jnp.where(i == 0, 0.0, xp_ref[:, 0:C]), x_ref[:, 0:C], xn_ref[:, 0:C]], axis=0)
        shifted = [ext[8:, :]] + [pltpu.roll(ext, k, 0)[8:, :] for k in range(1, 4)]
        pre = b_ref[...] + w_ref[3:4, :] * shifted[0]
        for k in range(1, 4):
            pre = pre + w_ref[3 - k:4 - k, :] * shifted[k]
        sg = jax.nn.sigmoid(pre)
        dact = jnp.concatenate([da_ref[...], jnp.where(i == nblk - 1, 0.0, dan_ref[...])], axis=0)
        dpre = dact * (sg * (1.0 + pre * (1.0 - sg)))
        dx = w_ref[3:4, :] * dpre[0:tm, :]
        for k in range(1, 4):
            dx = dx + w_ref[3 - k:4 - k, :] * pltpu.roll(dpre, rows - k, 0)[0:tm, :]
        dp_ref[:, 0:C] = dx.astype(bf16)
        dp_ref[:, C:C + 128] = ddt_ref[...].astype(bf16)
        dp_ref[:, C + 128:] = jnp.zeros((tm, 128), bf16)
        dcur = dpre[0:tm, :]
        dws = [jnp.sum(dcur * shifted[3 - j][0:tm, :], axis=0, keepdims=True) for j in range(4)]
        dbs = jnp.sum(dcur, axis=0, keepdims=True)

        @pl.when(i == 0)
        def _():
            dw_ref[...] = jnp.zeros_like(dw_ref)
            db_ref[...] = jnp.zeros_like(db_ref)

        for j in range(4):
            dw_ref[j:j + 1, :] += dws[j]
        db_ref[...] += dbs

    t8 = tm // 8
    blk = MAIN_SSD_BLOCK
    return _call(body, name, (nblk,),
                 [pl.BlockSpec((tm, 1536), lambda i: (i, blk)),
                  pl.BlockSpec((8, 1536), lambda i: (jnp.maximum(i * t8 - 1, 0), blk)),
                  pl.BlockSpec((8, 1536), lambda i: (jnp.minimum((i + 1) * t8, S // 8 - 1), blk)),
                  pl.BlockSpec((tm, C), lambda i: (i, 0)),
                  pl.BlockSpec((8, C), lambda i: (jnp.minimum((i + 1) * t8, S // 8 - 1), 0)),
                  pl.BlockSpec((tm, 128), lambda i: (i, 0)),
                  pl.BlockSpec((4, C), lambda i: (0, 0)), pl.BlockSpec((1, C), lambda i: (0, 0)), _ANY],
                 [pl.BlockSpec((tm, 1536), lambda i: (i, blk)), pl.BlockSpec((4, C), lambda i: (0, 0)),
                  pl.BlockSpec((1, C), lambda i: (0, 0))],
                 [jax.ShapeDtypeStruct((S, MAIN_WIDTH), bf16), jax.ShapeDtypeStruct((4, C), f32),
                  jax.ShapeDtypeStruct((1, C), f32)],
                 sem=("arbitrary",), aliases={8: 0})(p_ssd, p_ssd, p_ssd, dact, dact, ddt, conv_w, conv_b, dp_main)


def _ssd_chunk(xbc, dtr, state, dt_bias, a_log, d_full):
    T = SSD_CHUNK
    r_i = lax.broadcasted_iota(jnp.int32, (T, T), 0)
    c_i = lax.broadcasted_iota(jnp.int32, (T, T), 1)
    tril = c_i <= r_i
    tri = tril.astype(bf16)
    lane = lax.broadcasted_iota(jnp.int32, (1, LANES), 1)
    hm = [(lane < 64).astype(f32), (lane >= 64).astype(f32)]
    column = lambda v, h: jnp.broadcast_to(v[:, h:h + 1], (T, LANES))

    def per_head_lanes(v):
        return jnp.concatenate([jnp.where(lane < 64, column(v, 2 * pp), column(v, 2 * pp + 1)) for pp in range(6)],
                               axis=1)

    xs, bm, cm = xbc[:, :768], xbc[:, 768:1024], xbc[:, 1024:1280]
    dt = _softplus(dtr + dt_bias)
    a_dt = dt * (-jnp.exp(a_log))
    a_cs = _xdot_l(tri, a_dt)
    dt_full = per_head_lanes(dt)
    acs_full = per_head_lanes(a_cs)
    last = lax.broadcasted_iota(jnp.int32, (T, SSD_WIDTH), 0) == T - 1
    tot_full = jnp.sum(jnp.where(last, acs_full, 0.0), axis=0, keepdims=True)
    xdt = xs * dt_full
    xw = xdt * jnp.exp(tot_full - acs_full)
    eacs = jnp.exp(acs_full)
    st_parts, off_parts, diag_parts = [], [], []
    for g in range(2):
        bg, cg = bm[:, 128 * g:128 * (g + 1)], cm[:, 128 * g:128 * (g + 1)]
        cols = slice(384 * g, 384 * (g + 1))
        st_parts.append(_bdot(bg, xw[:, cols], TN))
        off_parts.append(_bdot(cg, state[:, cols], NN))
        cb = _bdot(cg, bg, NT)
        for pp in range(3 * g, 3 * g + 3):
            xp = xdt[:, 128 * pp:128 * (pp + 1)]
            acc = jnp.zeros((T, LANES), f32)
            for hh in range(2):
                a_col = column(a_cs, 2 * pp + hh)
                decay = jnp.where(tril, jnp.exp(jnp.minimum(a_col - a_col.T, 0.0)), 0.0)
                acc = acc + _bdot(cb * decay, xp * hm[hh], NN)
            diag_parts.append(acc)
    new_state = state * jnp.exp(tot_full) + jnp.concatenate(st_parts, axis=1)
    y = jnp.concatenate(diag_parts, axis=1) + jnp.concatenate(off_parts, axis=1) * eacs + xs * d_full
    return y, new_state


def ssd_fwd(xact, p_ssd, dt_bias, a_log, d_full, name):
    S = xact.shape[0]
    T = SSD_CHUNK

    U = SSD_CHUNKS_PER_STEP

    def body(x_ref, p_ref, b_ref, a_ref, d_ref, y_ref, s_ref, state):
        @pl.when(pl.program_id(0) == 0)
        def _():
            state[...] = jnp.zeros_like(state)

        st = state[...]
        for u in range(U):
            rows = slice(T * u, T * (u + 1))
            s_ref[u] = st
            y, st = _ssd_chunk(x_ref[rows, :], p_ref[rows, :], st, b_ref[...], a_ref[...], d_ref[...])
            y_ref[rows, :] = y
        state[...] = st

    vec = lambda n: pl.BlockSpec((1, n), lambda i: (0, 0))
    return _call(body, name, (S // (U * T),),
                 [pl.BlockSpec((U * T, SSD_XBC), lambda i: (i, 0)),
                  pl.BlockSpec((U * T, 128), lambda i: (i, MAIN_DT_BLOCK)), vec(128), vec(128), vec(768)],
                 [pl.BlockSpec((U * T, 768), lambda i: (i, 0)), pl.BlockSpec((U, T, 768), lambda i: (i, 0, 0))],
                 [jax.ShapeDtypeStruct((S, 768), f32), jax.ShapeDtypeStruct((S // T, T, 768), f32)],
                 scratch=[pltpu.VMEM((T, 768), f32)], sem=("arbitrary",))(xact, p_ssd, dt_bias, a_log, d_full)


def ssd_bwd(xact, p_ssd, states, dy, dt_bias, a_log, d_full, name):
    S = xact.shape[0]
    T = SSD_CHUNK
    U = 1
    nc = S // (U * T)

    def body(x_ref, p_ref, s_ref, dy_ref, b_ref, a_ref, d_ref, dx_ref, ddt_ref, db_ref, da_ref, dd_ref, dstate):
        i = pl.program_id(0)

        @pl.when(i == 0)
        def _():
            for ref in (dstate, db_ref, da_ref, dd_ref):
                ref[...] = jnp.zeros_like(ref)

        dst = dstate[...]
        for u in reversed(range(U)):
            rows = slice(T * u, T * (u + 1))
            _, vjp = jax.vjp(_ssd_chunk, x_ref[rows, :], p_ref[rows, :], s_ref[u], b_ref[...], a_ref[...], d_ref[...])
            dx, ddt, dst, db, da, dd = vjp((dy_ref[rows, :], dst))
            dx_ref[rows, :] = dx
            ddt_ref[rows, :] = ddt
            db_ref[...] += db
            da_ref[...] += da
            dd_ref[...] += dd
        dstate[...] = dst

    rev = lambda i: nc - 1 - i
    vec = lambda n: pl.BlockSpec((1, n), lambda i: (0, 0))
    return _call(body, name, (nc,),
                 [pl.BlockSpec((U * T, SSD_XBC), lambda i: (rev(i), 0)),
                  pl.BlockSpec((U * T, 128), lambda i: (rev(i), MAIN_DT_BLOCK)),
                  pl.BlockSpec((U, T, 768), lambda i: (rev(i), 0, 0)), pl.BlockSpec((U * T, 768), lambda i: (rev(i), 0)),
                  vec(128), vec(128), vec(768)],
                 [pl.BlockSpec((U * T, SSD_XBC), lambda i: (rev(i), 0)), pl.BlockSpec((U * T, 128), lambda i: (rev(i), 0)),
                  vec(128), vec(128), vec(768)],
                 [jax.ShapeDtypeStruct((S, SSD_XBC), f32), jax.ShapeDtypeStruct((S, 128), f32),
                  jax.ShapeDtypeStruct((1, 128), f32), jax.ShapeDtypeStruct((1, 128), f32),
                  jax.ShapeDtypeStruct((1, 768), f32)],
                 scratch=[pltpu.VMEM((T, 768), f32)],
                 sem=("arbitrary",))(xact, p_ssd, states, dy, dt_bias, a_log, d_full)


def _tail_fn(ys5, pt, o0, o1, o2, l0, l1, l2, yssd, glu_b, nw, pr_glu, pr_a, pr_b, pr_c, x, weights):
    glu_w, pa, pb, pc, wo = weights
    gates = jax.nn.sigmoid(pt[:, :3072])
    za, zb, zc = pt[:, 3072:3584], pt[:, 3584:3840], pt[:, 3840:4608]
    g = jax.nn.gelu(ys5)
    ya = g * jax.nn.sigmoid(_cdot(g, glu_w, NN) + glu_b + pr_glu) * jax.nn.silu(za)
    m = jnp.maximum(jnp.maximum(l0, l1), l2)
    e0, e1, e2 = jnp.exp(l0 - m), jnp.exp(l1 - m), jnp.exp(l2 - m)
    yb = (e0 * o0 + e1 * o1 + e2 * o2) / (e0 + e1 + e2) * jax.nn.silu(zb)
    yc = _rms(yssd * jax.nn.silu(zc), nw)
    merged = (gates[:, :1024] * (_cdot(ya, pa, NN) + pr_a) + gates[:, 1024:2048] * (_cdot(yb, pb, NN) + pr_b)
              + gates[:, 2048:] * (_cdot(yc, pc, NN) + pr_c))
    out = x + _cdot(merged, wo, NN)
    return out, (g, ya, yb, yc, merged)


def _tail_specs(tm):
    row = lambda n: pl.BlockSpec((tm, n), lambda i: (i, 0))
    full = lambda a, b: pl.BlockSpec((a, b), lambda i: (0, 0))
    acts = [row(512), row(4608)] + [row(256)] * 6 + [row(768), row(D_MODEL)]
    consts = [full(1, 512), full(1, 768), full(512, 512), full(512, D_MODEL), full(256, D_MODEL),
              full(768, D_MODEL), full(D_MODEL, D_MODEL)]
    return row, full, acts, consts


def tail_fwd(ys5, pt, os_, ls_, yssd, x, glu_b, nw, weights, name, next_norm_w=None, target=None):
    S = x.shape[0]
    tm = TAIL_ROWS
    row, full, acts, consts = _tail_specs(tm)

    def body(ys5_ref, pt_ref, o0, o1, o2, l0, l1, l2, yssd_ref, x_ref, gb_ref, nw_ref, gw, pa, pb, pc, wo, *rest):
        z = lambda n: jnp.zeros((tm, n), f32)
        out, _ = _tail_fn(ys5_ref[...], pt_ref[...], o0[...], o1[...], o2[...], l0[...], l1[...], l2[...],
                          yssd_ref[...], gb_ref[...], nw_ref[...], z(512), z(D_MODEL), z(D_MODEL), z(D_MODEL),
                          x_ref[...], (gw[...], pa[...], pb[...], pc[...], wo[...]))
        if target is not None:
            t_ref, dy_ref, l_ref = rest
            diff = out - t_ref[...]
            dy_ref[...] = diff * (1.0 / D_MODEL)
            part = jnp.full((8, LANES), 0.5 / D_MODEL * jnp.sum(diff * diff), f32)

            @pl.when(pl.program_id(0) == 0)
            def _():
                l_ref[...] = part

            @pl.when(pl.program_id(0) > 0)
            def _():
                l_ref[...] += part
        elif next_norm_w is not None:
            n_ref, out_ref, h_ref = rest
            out_ref[...] = out
            h_ref[...] = _rms(out, n_ref[...]).astype(bf16)
        else:
            rest[0][...] = out

    sd = jax.ShapeDtypeStruct((S, D_MODEL), f32)
    if target is not None:
        extra_in, extra_specs = [target], [row(D_MODEL)]
        out_specs = [row(D_MODEL), pl.BlockSpec((8, LANES), lambda i: (0, 0))]
        out_shape = [sd, jax.ShapeDtypeStruct((8, LANES), f32)]
    elif next_norm_w is not None:
        extra_in, extra_specs = [next_norm_w], [full(1, D_MODEL)]
        out_specs, out_shape = [row(D_MODEL), row(D_MODEL)], [sd, jax.ShapeDtypeStruct((S, D_MODEL), bf16)]
    else:
        extra_in, extra_specs, out_specs, out_shape = [], [], row(D_MODEL), sd
    return _call(body, name, (S // tm,), acts + consts + extra_specs, out_specs, out_shape,
                 sem=("arbitrary",))(ys5, pt, *os_, *ls_, yssd, x, glu_b, nw, *weights, *extra_in)


def tail_bwd(ys5, pt, os_, ls_, yssd, dout, glu_b, nw, weights, name):
    S = dout.shape[0]
    tm = TAIL_ROWS
    row, full, acts, consts = _tail_specs(tm)

    def body(ys5_ref, pt_ref, o0, o1, o2, l0, l1, l2, yssd_ref, dout_ref, gb_ref, nw_ref, gw, pa, pb, pc, wo,
             dys5_ref, dpt_ref, do0, do1, do2, dl0, dl1, dl2, dyssd_ref, dgb_ref, dnw_ref,
             g_ref, ya_ref, yb_ref, yc_ref, mg_ref, dglu_ref, dpa_ref, dpb_ref, dpc_ref):
        z = lambda n: jnp.zeros((tm, n), f32)
        w = (gw[...], pa[...], pb[...], pc[...], wo[...])
        fn = lambda *a: _tail_fn(*a, z(D_MODEL), w)
        _, vjp, aux = jax.vjp(fn, ys5_ref[...], pt_ref[...], o0[...], o1[...], o2[...], l0[...], l1[...], l2[...],
                              yssd_ref[...], gb_ref[...], nw_ref[...], z(512), z(D_MODEL), z(D_MODEL), z(D_MODEL),
                              has_aux=True)
        (dys5, dpt, d0, d1, d2, e0, e1, e2, dyssd, dgb, dnw, dglu, dpa, dpb, dpc) = vjp(dout_ref[...])
        dys5_ref[...] = dys5
        dpt_ref[...] = dpt.astype(bf16)
        for ref, val in ((do0, d0), (do1, d1), (do2, d2), (dl0, e0), (dl1, e1), (dl2, e2)):
            ref[...] = val
        dyssd_ref[...] = dyssd
        g, ya, yb, yc, merged = aux
        for ref, val in ((g_ref, g), (ya_ref, ya), (yb_ref, yb), (yc_ref, yc), (mg_ref, merged),
                         (dglu_ref, dglu), (dpa_ref, dpa), (dpb_ref, dpb), (dpc_ref, dpc)):
            ref[...] = val.astype(bf16)

        @pl.when(pl.program_id(0) == 0)
        def _():
            dgb_ref[...] = dgb
            dnw_ref[...] = dnw

        @pl.when(pl.program_id(0) > 0)
        def _():
            dgb_ref[...] += dgb
            dnw_ref[...] += dnw

    sd = lambda n, dt=f32: jax.ShapeDtypeStruct((S, n), dt)
    out_specs = ([row(512), row(4608)] + [row(256)] * 6 + [row(768), full(1, 512), full(1, 768)]
                 + [row(512), row(512), row(256), row(768), row(D_MODEL), row(512)] + [row(D_MODEL)] * 3)
    out_shape = ([sd(512), sd(MAIN_WIDTH, bf16)] + [sd(256)] * 6 + [sd(768), jax.ShapeDtypeStruct((1, 512), f32),
                                                          jax.ShapeDtypeStruct((1, 768), f32)]
                 + [sd(512, bf16), sd(512, bf16), sd(256, bf16), sd(768, bf16), sd(D_MODEL, bf16), sd(512, bf16)]
                 + [sd(D_MODEL, bf16)] * 3)
    return _call(body, name, (S // tm,), acts + consts, out_specs, out_shape,
                 sem=("arbitrary",))(ys5, pt, *os_, *ls_, yssd, dout, glu_b, nw, *weights)


def _in_proj_segments(shards):
    dtype = shards[0].dtype

    def c(a, b):
        parts = []
        for k, sh in enumerate(shards):
            lo, hi = max(a, W_IN_SHARD * k), min(b, W_IN_SHARD * (k + 1))
            if lo < hi:
                parts.append(sh[:, lo - W_IN_SHARD * k:hi - W_IN_SHARD * k])
        return parts[0] if len(parts) == 1 else jnp.concatenate(parts, axis=1)

    atts = []
    for g in range(3):
        parts = []
        for hh in range(2):
            o = 64 * (4 * g + 2 * hh)
            parts += [c(_C_Q + o, _C_Q + o + 128), c(_C_K + o, _C_K + o + 128), c(_C_V + o, _C_V + o + 128)]
        atts.append(jnp.concatenate(parts, axis=1))
    ssd = jnp.concatenate([c(_C_XBC, _C_ZC), jnp.zeros((D_MODEL, 1536 - (_C_ZC - _C_XBC)), dtype)], axis=1)
    tail = jnp.concatenate([c(_C_GATE, _C_END), c(_C_ZA, _C_Q), c(_C_ZB, _C_XBC), c(_C_ZC, _C_GATE)], axis=1)
    return [c(_C_UA, _C_ZA), jnp.concatenate([tail, ssd] + atts, axis=1)]


def _in_proj_grad(ds5, dmain):
    dtail, dssd = dmain[:, :4608], dmain[:, 4608:6144]
    datts = [dmain[:, 6144 + 768 * g:6144 + 768 * (g + 1)] for g in range(3)]
    pick = lambda off: [datts[g][:, 384 * hh + off:384 * hh + off + 128] for g in range(3) for hh in range(2)]
    pieces = ([ds5, dtail[:, 3072:3584]] + pick(0) + pick(128) + pick(256)
              + [dtail[:, 3584:3840], dssd[:, :_C_ZC - _C_XBC], dtail[:, 3840:4608], dtail[:, :3072]])
    shards, start = [[] for _ in range(4)], 0
    for piece in pieces:
        width = piece.shape[1]
        for k in range(4):
            lo, hi = max(start, W_IN_SHARD * k), min(start + width, W_IN_SHARD * (k + 1))
            if lo < hi:
                shards[k].append(piece[:, lo - start:hi - start])
        start += width
    return jnp.stack([jnp.concatenate(s, axis=1) for s in shards])


def _prep_layer(p):
    q = {}
    q["segs"] = [s.astype(bf16) for s in _in_proj_segments(p["w_in"])]
    disc = _s5_discretize(p["s5_a_re"], p["s5_a_im"], p["s5_log_step"], p["s5_b_re"], p["s5_b_im"],
                          p["s5_c_re"], p["s5_c_im"])
    q["s5"] = disc
    q["pw"] = _lam_powers(disc[0], disc[1])
    q["s5_d"] = p["s5_d"].reshape(1, 512)
    q["qw"] = jnp.tile(p["q_norm_w"], 2).reshape(1, LANES)
    q["kw"] = jnp.tile(p["k_norm_w"], 2).reshape(1, LANES)
    q["conv_w"] = p["conv_w"]
    q["conv_b"] = p["conv_b"].reshape(1, SSD_XBC)
    pad = lambda v: jnp.pad(v, (0, LANES - v.shape[0])).reshape(1, LANES)
    q["dt_bias"], q["a_log"] = pad(p["dt_bias"]), pad(p["ssd_a_log"])
    q["d_full"] = jnp.repeat(p["ssd_d"], 64).reshape(1, SSD_WIDTH)
    q["glu_b"] = p["s5_glu_b"].reshape(1, 512)
    q["nw"] = p["ssd_norm_w"].reshape(1, SSD_WIDTH)
    q["norm_w"] = p["norm_w"].reshape(1, D_MODEL)
    q["tailw"] = tuple(p[n].astype(bf16) for n in ("s5_glu_w", "proj_a", "proj_b", "proj_c", "w_out"))
    return q


_DILATIONS = (1, 4, 16)


def layer_fwd(x, q, tag, h=None, next_norm_w=None, target=None):
    if h is None:
        h = rms_fwd(x, q["norm_w"], f"rms_fwd{tag}")
    p_s5, p_main = [mm_nn(h, w, f"inproj{k}{tag}") for k, w in enumerate(q["segs"])]
    _, _, w_re, w_im, c_re, c_im = q["s5"]
    ys5, h_re, h_im = s5_fwd(p_s5, *q["pw"], w_re, w_im, c_re, c_im, q["s5_d"], f"s5_fwd{tag}")
    os_, ls_ = [], []
    for g, d in enumerate(_DILATIONS):
        o, l = att_fwd(p_main, q["qw"], q["kw"], d, g, f"att_fwd{g}{tag}")
        os_.append(o)
        ls_.append(l)
    xact = conv_fwd(p_main, q["conv_w"], q["conv_b"], f"conv_fwd{tag}")
    yssd, states = ssd_fwd(xact, p_main, q["dt_bias"], q["a_log"], q["d_full"], f"ssd_fwd{tag}")
    out = tail_fwd(ys5, p_main, os_, ls_, yssd, x, q["glu_b"], q["nw"], q["tailw"], f"tail_fwd{tag}",
                   next_norm_w=next_norm_w, target=target)
    saved = dict(x=x, h=h, p_s5=p_s5, p_main=p_main, ys5=ys5, h_re=h_re, h_im=h_im,
                 os=os_, ls=ls_, xact=xact, yssd=yssd, states=states)
    return out, saved


def layer_bwd(dout, sv, q, p, tag):
    (dys5, dp_main, do0, do1, do2, dl0, dl1, dl2, dyssd, dglu_b, dnw, g_b, ya_b, yb_b, yc_b, mg_b, dglu_b16,
     dpa_b, dpb_b, dpc_b) = tail_bwd(sv["ys5"], sv["p_main"], sv["os"], sv["ls"], sv["yssd"], dout, q["glu_b"],
                                     q["nw"], q["tailw"], f"tail_bwd{tag}")
    grads = {}
    grads["s5_glu_w"] = mm_tn(g_b, dglu_b16, f"dglu_w{tag}")
    grads["proj_a"] = mm_tn(ya_b, dpa_b, f"dproj_a{tag}")
    grads["proj_b"] = mm_tn(yb_b, dpb_b, f"dproj_b{tag}")
    grads["proj_c"] = mm_tn(yc_b, dpc_b, f"dproj_c{tag}")
    grads["w_out"] = mm_tn(mg_b, dout, f"dw_out{tag}")
    grads["s5_glu_b"] = dglu_b.reshape(512)
    grads["ssd_norm_w"] = dnw.reshape(SSD_WIDTH)

    dxact, ddt, ddt_bias, da_log, dd_full = ssd_bwd(sv["xact"], sv["p_main"], sv["states"], dyssd, q["dt_bias"],
                                                    q["a_log"], q["d_full"], f"ssd_bwd{tag}")
    dp_main, dconv_w, dconv_b = conv_bwd(sv["p_main"], dxact, ddt, q["conv_w"], q["conv_b"], dp_main,
                                         f"conv_bwd{tag}")
    grads["dt_bias"] = ddt_bias[0, :12]
    grads["ssd_a_log"] = da_log[0, :12]
    grads["ssd_d"] = dd_full.reshape(12, 64).sum(axis=1)
    grads["conv_w"] = dconv_w
    grads["conv_b"] = dconv_b.reshape(SSD_XBC)

    dqw, dkw = 0.0, 0.0
    for g, d in enumerate(_DILATIONS):
        dp_main, a, b = att_bwd(sv["p_main"], sv["os"][g], sv["ls"][g], (do0, do1, do2)[g], (dl0, dl1, dl2)[g],
                                q["qw"], q["kw"], d, g, dp_main, f"att_bwd{g}{tag}")
        dqw, dkw = dqw + a, dkw + b
    grads["q_norm_w"] = dqw.reshape(2, 64).sum(axis=0)
    grads["k_norm_w"] = dkw.reshape(2, 64).sum(axis=0)

    _, _, w_re, w_im, c_re, c_im = q["s5"]
    dp_s5, dwre, dwim, dcre, dcim, dlam_re, dlam_im, dd = s5_bwd(
        dys5, sv["p_s5"], sv["h_re"], sv["h_im"], *q["pw"], w_re, w_im, c_re, c_im, q["s5_d"], f"s5_bwd{tag}")
    s5_names = ("s5_a_re", "s5_a_im", "s5_log_step", "s5_b_re", "s5_b_im", "s5_c_re", "s5_c_im")
    _, disc_vjp = jax.vjp(_s5_discretize, *[p[n] for n in s5_names])
    for n, gr in zip(s5_names, disc_vjp((dlam_re, dlam_im, dwre, dwim, dcre, dcim))):
        grads[n] = gr
    grads["s5_d"] = dd.reshape(512)

    dsegs = [dp_s5, dp_main]
    dws = [mm_tn(sv["h"], ds, f"dw_in{k}{tag}") for k, ds in enumerate(dsegs)]
    grads["w_in"] = _in_proj_grad(*dws)
    dh_main = mm_nt(dp_main, q["segs"][1], f"dh1{tag}")
    dx, dnorm_w = mm_nt_rms_bwd(dp_s5, q["segs"][0], dh_main, sv["x"], q["norm_w"], dout, f"dh0_rms_bwd{tag}")
    grads["norm_w"] = dnorm_w.reshape(D_MODEL)
    return dx, grads


def _exchange(name, scatter=(), gather=(), sibling=(), sibling_both=False, sibling_by_core=None):
    scatter, gather, sibling = list(scatter), list(gather), list(sibling)
    chip_xs = scatter + gather
    ns, nc, nb = len(scatter), len(chip_xs), len(sibling)
    n = nc + nb
    n_in = n + (2 if sibling_by_core else 0)
    n_out = n + (1 if sibling_by_core else 0)
    n_sem = 3 * nc + nb + (1 if sibling_by_core else 0)

    def body(*refs):
        x_refs, o_refs, send_sems, recv_sems = refs[:n_in], refs[n_in:n_in + n_out], refs[-2], refs[-1]
        mx, my, mc = lax.axis_index("x"), lax.axis_index("y"), lax.axis_index("c")
        me = 2 * mx + my
        copies = []
        for a in range(nc):
            for t, (px, py) in enumerate(((1 - mx, my), (mx, 1 - my), (1 - mx, 1 - my))):
                src = x_refs[a].at[2 * px + py] if a < ns else x_refs[a]
                copies.append(pltpu.make_async_remote_copy(
                    src_ref=src, dst_ref=o_refs[a].at[me], send_sem=send_sems.at[3 * a + t],
                    recv_sem=recv_sems.at[3 * a + t], device_id=(px, py, mc), device_id_type=pl.DeviceIdType.MESH))
        for b in range(nc, n):
            k = 3 * nc + b - nc
            copies.append(pltpu.make_async_remote_copy(
                src_ref=x_refs[b], dst_ref=o_refs[b].at[mc] if sibling_both else o_refs[b], send_sem=send_sems.at[k],
                recv_sem=recv_sems.at[k], device_id=(mx, my, 1 - mc), device_id_type=pl.DeviceIdType.MESH))
        for cp in copies:
            cp.start()
        if sibling_by_core:
            def pick(src):
                return pltpu.make_async_remote_copy(
                    src_ref=src, dst_ref=o_refs[n], send_sem=send_sems.at[n_sem - 1], recv_sem=recv_sems.at[n_sem - 1],
                    device_id=(mx, my, 1 - mc), device_id_type=pl.DeviceIdType.MESH)

            @pl.when(mc == 0)
            def _():
                pick(x_refs[n]).start()

            @pl.when(mc == 1)
            def _():
                pick(x_refs[n + 1]).start()

            copies.append(pick(x_refs[n]))
        for cp in copies:
            cp.wait()

    shapes = ([(4,) + tuple(x.shape[1:]) for x in scatter] + [(4,) + tuple(x.shape) for x in gather]
              + [((2,) if sibling_both else ()) + tuple(x.shape) for x in sibling])
    xs = chip_xs + sibling
    out_shape = [jax.ShapeDtypeStruct(s, x.dtype) for s, x in zip(shapes, xs)]
    if sibling_by_core:
        out_shape.append(jax.ShapeDtypeStruct(sibling_by_core[0].shape, sibling_by_core[0].dtype))
    outs = pl.pallas_call(
        body, name=name, in_specs=[_ANY] * n_in, out_specs=[_ANY] * n_out, out_shape=out_shape,
        scratch_shapes=[pltpu.SemaphoreType.DMA((n_sem,)), pltpu.SemaphoreType.DMA((n_sem,))],
    )(*xs, *(sibling_by_core or ()))
    me, c = 2 * lax.axis_index("x") + lax.axis_index("y"), lax.axis_index("c")
    fixed = []
    for a, (o, x) in enumerate(zip(outs, xs)):
        if a < ns:
            o = lax.dynamic_update_index_in_dim(o, lax.dynamic_index_in_dim(x, me, 0, keepdims=True), me, 0)
        elif a < nc:
            o = lax.dynamic_update_index_in_dim(o, x[None], me, 0)
        elif sibling_both:
            o = lax.dynamic_update_index_in_dim(o, x[None], c, 0)
        fixed.append(o)
    if sibling_by_core:
        fixed.append(outs[n])
    return fixed[:ns], fixed[ns:nc], fixed[nc:]


def _rows_tile(rows, row_bytes, budget=5 << 19):
    return next(t for t in (512, 256, 128, 64, 32, 16, 8) if rows % t == 0 and t * row_bytes <= budget)


def _padded_row_bytes(cols):
    return -(-cols // LANES) * LANES * 4


def _add2(a, b, name, out_dtype=f32):
    by_core = isinstance(a, (tuple, list))
    parts = list(a) if by_core else [a]
    R, C = b.shape
    tr = _rows_tile(R, _padded_row_bytes(C))

    def body(*refs):
        b_ref, o_ref = refs[-2], refs[-1]
        mine = jnp.where(lax.axis_index("c") == 0, refs[0][...], refs[1][...]) if by_core else refs[0][...]
        o_ref[...] = (mine + b_ref[...]).astype(out_dtype)

    spec = pl.BlockSpec((tr, C), lambda i: (i, 0))
    return _call(body, name, (R // tr,), [spec] * (len(parts) + 1), spec, jax.ShapeDtypeStruct((R, C), out_dtype),
                 sem=("parallel",))(*parts, b)


def _sum4(x, name):
    R = x.shape[1]
    tr = _tile(R, (2560, 1024, 512, 256, 128))

    def body(x_ref, o_ref):
        p = [x_ref[j].astype(f32) for j in range(4)]
        o_ref[...] = ((p[0] + p[1]) + p[2]) + p[3]

    return _call(body, name, (R // tr,), [pl.BlockSpec((4, tr, LANES), lambda i: (0, i, 0))],
                 pl.BlockSpec((tr, LANES), lambda i: (i, 0)), jax.ShapeDtypeStruct((R, LANES), f32),
                 sem=("parallel",))(x)


def _adamw(g_parts, w, m, v, name):
    stacked = not isinstance(g_parts, (tuple, list))
    k = g_parts.shape[0] if stacked else len(g_parts)
    R, C = w.shape
    tr = _rows_tile(R, _padded_row_bytes(C))

    def body(*refs):
        w_ref, m_ref, v_ref, g_ref, d_ref, nm_ref, nv_ref = refs[-7:]
        if stacked:
            g = refs[0][0].astype(f32)
            for j in range(1, k):
                g = g + refs[0][j].astype(f32)
        else:
            g = refs[0][...]
            for r in refs[1:k]:
                g = g + r[...]
        g_ref[...] = g
        d_ref[...], nm_ref[...], nv_ref[...] = _adamw_update(g, w_ref[...], m_ref[...], v_ref[...])

    spec = pl.BlockSpec((tr, C), lambda i: (i, 0))
    sd = jax.ShapeDtypeStruct((R, C), f32)
    g_specs = [pl.BlockSpec((k, tr, C), lambda i: (0, i, 0))] if stacked else [spec] * k
    g_args = [g_parts] if stacked else list(g_parts)
    return _call(body, name, (R // tr,), g_specs + [spec] * 3, [spec] * 4, [sd] * 4,
                 sem=("parallel",))(*g_args, w, m, v)


def _adamw_update(g, w, m, v):
    m = ADAM_B1 * m + (1.0 - ADAM_B1) * g
    v = ADAM_B2 * v + (1.0 - ADAM_B2) * (g * g)
    c1 = 1.0 - ADAM_B1 ** ADAM_STEP
    c2 = 1.0 - ADAM_B2 ** ADAM_STEP
    return -ADAM_LR * ((m / c1) / (jnp.sqrt(v / c2) + ADAM_EPS) + ADAM_WD * w), m, v


def _adamw_small(gs, ws, ms, vs, name):
    n = len(gs)

    def body(*refs):
        ins, outs = refs[:4 * n], refs[4 * n:]
        for t in range(n):
            d, m, v = _adamw_update(ins[t][...], ins[n + t][...], ins[2 * n + t][...], ins[3 * n + t][...])
            outs[t][...] = d
            outs[n + t][...] = m
            outs[2 * n + t][...] = v

    vmem = pl.BlockSpec(memory_space=pltpu.VMEM)
    outs = pl.pallas_call(
        body, name=name, in_specs=[vmem] * (4 * n), out_specs=[vmem] * (3 * n),
        out_shape=[jax.ShapeDtypeStruct(w.shape, f32) for w in ws] * 3,
        compiler_params=pltpu.CompilerParams(vmem_limit_bytes=V7X_VMEM_LIMIT))(*gs, *ws, *ms, *vs)
    return outs[:n], outs[n:2 * n], outs[2 * n:]


def _pack(arrays, row_multiple=PACK_ROWS):
    flat = jnp.concatenate([a.reshape(-1) for a in arrays])
    unit = row_multiple * LANES
    n = -(-flat.shape[0] // unit) * unit
    return jnp.pad(flat, (0, n - flat.shape[0])).reshape(n // LANES, LANES)


def _unpack(buf, shapes, lead=()):
    flat = buf.reshape(lead + (-1,))
    out, off = [], 0
    for s in shapes:
        n = 1
        for dim in s:
            n *= dim
        out.append(flat[..., off:off + n].reshape(lead + tuple(s)))
        off += n
    return out


def _to_shards(full, axis):
    s = full.shape
    t = full.reshape(s[:axis] + (4, s[axis] // 4) + s[axis + 1:])
    return jnp.moveaxis(t, axis, 0)


def _from_shards(sh, axis):
    t = jnp.moveaxis(sh, 0, axis)
    s = t.shape
    return t.reshape(s[:axis] + (s[axis] * s[axis + 1],) + s[axis + 2:])


def kernel(x, norm_w, w_in, s5_a_re, s5_a_im, s5_log_step, s5_b_re, s5_b_im, s5_c_re, s5_c_im, s5_d, s5_glu_w, s5_glu_b, q_norm_w, k_norm_w, conv_w, conv_b, dt_bias, ssd_a_log, ssd_d, ssd_norm_w, proj_a, proj_b, proj_c, w_out, loss_target, m_norm_w, m_w_in, m_s5_a_re, m_s5_a_im, m_s5_log_step, m_s5_b_re, m_s5_b_im, m_s5_c_re, m_s5_c_im, m_s5_d, m_s5_glu_w, m_s5_glu_b, m_q_norm_w, m_k_norm_w, m_conv_w, m_conv_b, m_dt_bias, m_ssd_a_log, m_ssd_d, m_ssd_norm_w, m_proj_a, m_proj_b, m_proj_c, m_w_out, v_norm_w, v_w_in, v_s5_a_re, v_s5_a_im, v_s5_log_step, v_s5_b_re, v_s5_b_im, v_s5_c_re, v_s5_c_im, v_s5_d, v_s5_glu_w, v_s5_glu_b, v_q_norm_w, v_k_norm_w, v_conv_w, v_conv_b, v_dt_bias, v_ssd_a_log, v_ssd_d, v_ssd_norm_w, v_proj_a, v_proj_b, v_proj_c, v_w_out):
    given = dict(locals())
    W = {n: given[n] for n in _WEIGHTS}
    M = {n: given["m_" + n] for n in _WEIGHTS}
    V = {n: given["v_" + n] for n in _WEIGHTS}
    n_layers = norm_w.shape[0]
    assert n_layers == 2
    c = lax.axis_index("c")

    mine_of = lambda t: lax.dynamic_index_in_dim(t, c, 0, keepdims=False)
    as_payload = lambda n: lax.bitcast_convert_type(W[n], bf16) if n == "conv_w" else W[n].astype(bf16)
    payload_shapes = [W[n].shape + ((2,) if n == "conv_w" else ()) for n, _ in _SHARDED]
    wpack = _pack([as_payload(n) for n, _ in _SHARDED])
    half_rows = wpack.shape[0] // 2
    _, (pack_half, w_in_mine_layer), _ = _exchange(
        "gather_weights", gather=[lax.dynamic_slice_in_dim(wpack, c * half_rows, half_rows),
                                  mine_of(w_in).astype(bf16)])
    _, _, (w_in_layers, pack_halves) = _exchange("share_weights", sibling=[w_in_mine_layer, pack_half],
                                                 sibling_both=True)
    gathered = jnp.moveaxis(pack_halves, 0, 1).reshape(4, 2 * half_rows, LANES)
    full = dict(W)
    pieces = _unpack(gathered.reshape(4, -1), payload_shapes, lead=(4,))
    for (n, axis), sh in zip(_SHARDED, pieces):
        full[n] = _from_shards(lax.bitcast_convert_type(sh, f32) if n == "conv_w" else sh, axis)

    qs, saves = [], []
    for l in range(n_layers):
        p = {n: full[n][l] for n in _WEIGHTS if n != "w_in"}
        p["w_in"] = [w_in_layers[l, k] for k in range(4)]
        qs.append((_prep_layer(p), p))
    (act, h), sv = layer_fwd(x[0], qs[0][0], "_l0", next_norm_w=qs[1][0]["norm_w"])
    saves.append(sv)
    (dact, lsum), sv = layer_fwd(act, qs[1][0], "_l1", h=h, target=loss_target[0])
    saves.append(sv)
    loss = lax.psum(lsum[0, 0], ("x", "y", "c"))
    layer_grads = [None] * n_layers
    for l in reversed(range(n_layers)):
        q, p = qs[l]
        dact, layer_grads[l] = layer_bwd(dact, saves[l], q, p, f"_l{l}")
    grad_x = dact[None]
    G = {n: jnp.stack([layer_grads[l][n] for l in range(n_layers)]) for n in _WEIGHTS if n != "w_in"}

    repl_shapes = [W[n].shape for n in _REPL]
    small = _pack([G[n] for n in _REPL], 4 * PACK_ROWS)
    quarter = small.shape[0] // 4
    big = [_to_shards(G[n], axis).reshape(4, -1) for n, axis in _SHARDED]
    big = jnp.concatenate(big, axis=1)
    unit = PACK_ROWS * LANES
    nbig = -(-big.shape[1] // unit) * unit
    big = jnp.pad(big, ((0, 0), (0, nbig - big.shape[1]))).reshape(4, nbig // LANES, LANES)
    gpack = jnp.concatenate([big, small.reshape(4, quarter, LANES)], axis=1)
    rbig = nbig // LANES
    g0, g1 = layer_grads[0]["w_in"], layer_grads[1]["w_in"]

    (landed_pack,), _, (from_sibling,) = _exchange(
        "swap_w_in_grads_and_scatter_grads", scatter=[gpack.astype(bf16)], sibling_by_core=(g1, g0))
    flat = lambda t: t.reshape(4 * D_MODEL, W_IN_SHARD)
    shards = _add2((flat(g0), flat(g1)), flat(from_sibling), "sum_cores_w_in", out_dtype=bf16)
    mine = _sum4(landed_pack, "sum_chips")

    (landed,), _, (other,) = _exchange(
        "scatter_w_in_grads_and_swap_cores", scatter=[shards.reshape(4, D_MODEL, W_IN_SHARD)], sibling=[mine])
    w_in_mine = _adamw(landed, mine_of(w_in), mine_of(m_w_in), mine_of(v_w_in), "adamw_w_in")
    gq = _add2(mine[rbig:], other[rbig:], "sum_cores_small")

    _, (gsmall,), w_in_out = _exchange(
        "share_w_in_updates_and_gather_small", gather=[gq], sibling=w_in_mine, sibling_both=True)
    gsmall = gsmall.reshape(4 * quarter, LANES)

    shard_shapes = [W[n].shape for n, _ in _SHARDED]
    g_mine, g_other = _unpack(mine[:rbig], shard_shapes), _unpack(other[:rbig], shard_shapes)
    rows_of = lambda t: t.reshape(-1, t.shape[-1])
    res = [dict(), dict(), dict(), dict()]
    for k, (n, _) in enumerate(_SHARDED):
        outs = _adamw((rows_of(g_mine[k]), rows_of(g_other[k])), rows_of(W[n]), rows_of(M[n]), rows_of(V[n]),
                      f"adamw_{n}")
        for kind in range(4):
            res[kind][n] = outs[kind].reshape(W[n].shape)
    g_small = _unpack(gsmall, repl_shapes)
    small_out = _adamw_small([rows_of(g) for g in g_small], *([rows_of(T[n]) for n in _REPL] for T in (W, M, V)),
                             "adamw_replicated")
    for kind in range(4):
        res[kind]["w_in"] = w_in_out[kind]
        for k, n in enumerate(_REPL):
            res[kind][n] = g_small[k] if kind == 0 else small_out[kind - 1][k].reshape(W[n].shape)
    return (loss, grad_x, *[res[0][n] for n in _WEIGHTS], *[res[1][n] for n in _WEIGHTS],
            *[res[2][n] for n in _WEIGHTS], *[res[3][n] for n in _WEIGHTS])
```

```python
import functools

import jax
import jax.numpy as jnp
from jax import lax
from jax.experimental import pallas as pl
from jax.experimental.pallas import tpu as pltpu

f32 = jnp.float32
bf16 = jnp.bfloat16

D_MODEL = 1024
RMS_EPS = 1e-6
V7X_VMEM_LIMIT = 60 * 1024 * 1024
LANES = 128
NN, NT, TN = ((1,), (0,)), ((1,), (1,)), ((0,), (0,))

S5_STATES = 2048
S5_ROWS = 512
ATT_SEG = 2048
ATT_BLOCK = 128
SSD_CHUNK = 128
SSD_CHUNKS_PER_STEP = 2
SSD_WIDTH = 768
SSD_XBC = 1280
CONV_ROWS = 512
TAIL_ROWS = 256

ADAM_LR, ADAM_B1, ADAM_B2, ADAM_EPS, ADAM_WD, ADAM_STEP = 0.001, 0.9, 0.999, 1e-08, 0.01, 10

_C_UA, _C_ZA, _C_Q, _C_K, _C_V, _C_ZB, _C_XBC, _C_DT, _C_ZC, _C_GATE, _C_END = (
    0, 512, 1024, 1792, 2560, 3328, 3584, 4864, 4876, 5644, 8716)

_SHARDED = (("s5_glu_w", 1), ("conv_w", 2), ("proj_a", 2), ("proj_b", 2), ("proj_c", 2), ("w_out", 1))
W_IN_SHARD = 2179
_REPL = ("norm_w", "s5_a_re", "s5_a_im", "s5_log_step", "s5_b_re", "s5_b_im", "s5_c_re", "s5_c_im", "s5_d",
         "s5_glu_b", "q_norm_w", "k_norm_w", "conv_b", "dt_bias", "ssd_a_log", "ssd_d", "ssd_norm_w")
_WEIGHTS = ("norm_w", "w_in", "s5_a_re", "s5_a_im", "s5_log_step", "s5_b_re", "s5_b_im", "s5_c_re", "s5_c_im",
            "s5_d", "s5_glu_w", "s5_glu_b", "q_norm_w", "k_norm_w", "conv_w", "conv_b", "dt_bias", "ssd_a_log",
            "ssd_d", "ssd_norm_w", "proj_a", "proj_b", "proj_c", "w_out")
PACK_ROWS = 512


def _dot(a, b, dims):
    return lax.dot_general(a.astype(bf16), b.astype(bf16), (dims, ((), ())), preferred_element_type=f32)


_ANY = pl.BlockSpec(memory_space=pl.ANY)

MAIN_WIDTH = 8448
MAIN_SSD_BLOCK = 3
MAIN_DT_BLOCK = 46
MAIN_ATT_BLOCK = 16


def _call(body, name, grid, in_specs, out_specs, out_shape, scratch=(), sem=None, aliases=None):
    return pl.pallas_call(
        body, name=name, grid=grid, in_specs=in_specs, out_specs=out_specs, out_shape=out_shape,
        scratch_shapes=list(scratch), input_output_aliases=aliases or {},
        compiler_params=pltpu.CompilerParams(dimension_semantics=sem, vmem_limit_bytes=V7X_VMEM_LIMIT))


def _tile(n, options=(1024, 768, 512, 384, 256, 128)):
    return next(t for t in options if n % t == 0)


@functools.partial(jax.custom_vjp, nondiff_argnums=(2,))
def _bdot(a, b, dims):
    return _dot(a, b, dims)


def _bdot_fwd(a, b, dims):
    return _dot(a, b, dims), (a, b)


def _bdot_bwd(dims, res, g):
    a, b = res
    if dims == NN:
        da, db = _dot(g, b, NT), _dot(a, g, TN)
    elif dims == NT:
        da, db = _dot(g, b, NN), _dot(g, a, TN)
    else:
        da, db = _dot(b, g, NT), _dot(a, g, NN)
    return da.astype(a.dtype), db.astype(b.dtype)


_bdot.defvjp(_bdot_fwd, _bdot_bwd)


@functools.partial(jax.custom_vjp, nondiff_argnums=(2,))
def _cdot(a, w, dims):
    return _dot(a, w, dims)


def _cdot_fwd(a, w, dims):
    return _dot(a, w, dims), w


def _cdot_bwd(dims, w, g):
    da = _dot(g, w, NT) if dims == NN else _dot(g, w, NN)
    return da, jnp.zeros_like(w)


_cdot.defvjp(_cdot_fwd, _cdot_bwd)


def _split3(x):
    hi = x.astype(bf16)
    r = x - hi.astype(f32)
    mid = r.astype(bf16)
    lo = (r - mid.astype(f32)).astype(bf16)
    return hi, mid, lo


@jax.custom_vjp
def _xdot_l(m, x):
    return sum(_dot(m, p, NN) for p in _split3(x))


def _xdot_l_fwd(m, x):
    return _xdot_l(m, x), m


def _xdot_l_bwd(m, g):
    return jnp.zeros_like(m), sum(_dot(m, p, TN) for p in _split3(g))


_xdot_l.defvjp(_xdot_l_fwd, _xdot_l_bwd)


@jax.custom_vjp
def _softplus(x):
    e = jnp.exp(-jnp.abs(x))
    u = 1.0 + e
    log1p = jnp.where(u == 1.0, e, jnp.log(u) * (e / jnp.where(u == 1.0, 1.0, u - 1.0)))
    return jnp.maximum(x, 0.0) + log1p


def _softplus_fwd(x):
    return _softplus(x), x


def _softplus_bwd(x, g):
    return (g * jax.nn.sigmoid(x),)


_softplus.defvjp(_softplus_fwd, _softplus_bwd)


def _rms(x, w):
    return x * lax.rsqrt(jnp.mean(x * x, axis=-1, keepdims=True) + RMS_EPS) * w


def mm_nn(a, b, name, tm=2048):
    M, K = a.shape
    N = b.shape[1]
    tn = _tile(N)

    def body(a_ref, b_ref, o_ref):
        o_ref[...] = _dot(a_ref[...], b_ref[...], NN)

    return _call(body, name, (M // tm, N // tn),
                 [pl.BlockSpec((tm, K), lambda i, j: (i, 0)), pl.BlockSpec((K, tn), lambda i, j: (0, j))],
                 pl.BlockSpec((tm, tn), lambda i, j: (i, j)), jax.ShapeDtypeStruct((M, N), f32),
                 sem=("parallel", "parallel"))(a, b)


def mm_nt(a, b, name, tm=1024):
    M, K = a.shape
    N = b.shape[0]
    tk = _tile(K, (2816, 1024, 768, 512, 256, 128))

    def body(a_ref, b_ref, o_ref):
        k = pl.program_id(1)
        p = _dot(a_ref[...], b_ref[...], NT)

        @pl.when(k == 0)
        def _():
            o_ref[...] = p

        @pl.when(k > 0)
        def _():
            o_ref[...] += p

    return _call(body, name, (M // tm, K // tk),
                 [pl.BlockSpec((tm, tk), lambda i, k: (i, k)), pl.BlockSpec((N, tk), lambda i, k: (0, k))],
                 pl.BlockSpec((tm, N), lambda i, k: (i, 0)), jax.ShapeDtypeStruct((M, N), f32),
                 sem=("parallel", "arbitrary"))(a, b)


def mm_tn(a, b, name, tk=2048):
    K, M = a.shape
    N = b.shape[1]
    tn = _tile(N)

    def body(a_ref, b_ref, o_ref):
        k = pl.program_id(1)
        p = _dot(a_ref[...], b_ref[...], TN)

        @pl.when(k == 0)
        def _():
            o_ref[...] = p

        @pl.when(k > 0)
        def _():
            o_ref[...] += p

    return _call(body, name, (N // tn, K // tk),
                 [pl.BlockSpec((tk, M), lambda j, k: (k, 0)), pl.BlockSpec((tk, tn), lambda j, k: (k, j))],
                 pl.BlockSpec((M, tn), lambda j, k: (0, j)), jax.ShapeDtypeStruct((M, N), f32),
                 sem=("parallel", "arbitrary"))(a, b)


def rms_fwd(x, w, name, tm=512):
    S = x.shape[0]

    def body(x_ref, w_ref, o_ref):
        o_ref[...] = _rms(x_ref[...], w_ref[...]).astype(bf16)

    return _call(body, name, (S // tm,),
                 [pl.BlockSpec((tm, D_MODEL), lambda i: (i, 0)), pl.BlockSpec((1, D_MODEL), lambda i: (0, 0))],
                 pl.BlockSpec((tm, D_MODEL), lambda i: (i, 0)), jax.ShapeDtypeStruct((S, D_MODEL), bf16),
                 sem=("parallel",))(x, w)


def mm_nt_rms_bwd(a, b, acc, x, w, dres, name, tm=1024):
    S, K = a.shape

    def body(a_ref, b_ref, acc_ref, x_ref, w_ref, dr_ref, dx_ref, dw_ref):
        dh = _dot(a_ref[...], b_ref[...], NT) + acc_ref[...]
        _, vjp = jax.vjp(_rms, x_ref[...], w_ref[...])
        dx, dw = vjp(dh)
        dx_ref[...] = dx + dr_ref[...]

        @pl.when(pl.program_id(0) == 0)
        def _():
            dw_ref[...] = dw

        @pl.when(pl.program_id(0) > 0)
        def _():
            dw_ref[...] += dw

    row = pl.BlockSpec((tm, D_MODEL), lambda i: (i, 0))
    vec = pl.BlockSpec((1, D_MODEL), lambda i: (0, 0))
    return _call(body, name, (S // tm,),
                 [pl.BlockSpec((tm, K), lambda i: (i, 0)), pl.BlockSpec((D_MODEL, K), lambda i: (0, 0)), row, row, vec,
                  row], [row, vec],
                 [jax.ShapeDtypeStruct((S, D_MODEL), f32), jax.ShapeDtypeStruct((1, D_MODEL), f32)],
                 sem=("arbitrary",))(a, b, acc, x, w, dres)


def _s5_discretize(a_re, a_im, log_step, b_re, b_im, c_re, c_im):
    step = jnp.exp(log_step)[:, None]
    mag = jnp.exp(a_re * step)
    ang = a_im * step
    lam_re, lam_im = mag * jnp.cos(ang), mag * jnp.sin(ang)
    num_re, num_im = lam_re - 1.0, lam_im
    den = a_re * a_re + a_im * a_im
    f_re = (num_re * a_re + num_im * a_im) / den
    f_im = (num_im * a_re - num_re * a_im) / den
    bb_re = f_re[..., None] * b_re - f_im[..., None] * b_im
    bb_im = f_re[..., None] * b_im + f_im[..., None] * b_re
    eye = jnp.eye(8, dtype=f32)

    def block_in(bb):
        t = bb.transpose(0, 2, 1).reshape(4, 8, 16, 1, 64)
        return (t * eye[None, :, None, :, None]).reshape(4, 128, 512)

    def block_out(c):
        t = c.transpose(0, 2, 1).reshape(4, 8, 64, 1, 16)
        return (t * eye[None, :, None, :, None]).reshape(4, 512, 128)

    return (lam_re.reshape(1, S5_STATES), lam_im.reshape(1, S5_STATES), block_in(bb_re), block_in(bb_im),
            block_out(c_re), block_out(c_im))


def _lam_powers(lam_re, lam_im):
    rows_re, rows_im = [lam_re], [lam_im]
    for _ in range(7):
        pr, pi = rows_re[-1], rows_im[-1]
        rows_re.append(pr * lam_re - pi * lam_im)
        rows_im.append(pr * lam_im + pi * lam_re)
    return jnp.concatenate(rows_re, 0), jnp.concatenate(rows_im, 0)


def s5_fwd(u, pw_re, pw_im, w_re, w_im, c_re, c_im, dvec, name):
    S = u.shape[0]
    R, NS = S5_ROWS, S5_STATES
    nb = R // 8

    def body(u_ref, pwr_ref, pwi_ref, wre_ref, wim_ref, cre_ref, cim_ref, d_ref, y_ref, hr_ref, hi_ref,
             car_re, car_im, cin_re, cin_im, up, yp):
        @pl.when(pl.program_id(0) == 0)
        def _():
            car_re[...] = jnp.zeros_like(car_re)
            car_im[...] = jnp.zeros_like(car_im)

        slab = lambda r: pl.ds(r * nb, nb)
        for r in range(8):
            up[slab(r), :] = u_ref[:, r, :]
        u = up[...]
        for j in range(4):
            uj = u[:, 128 * j:128 * (j + 1)]
            hr_ref[:, 512 * j:512 * (j + 1)] = _dot(uj, wre_ref[j], NN)
            hi_ref[:, 512 * j:512 * (j + 1)] = _dot(uj, wim_ref[j], NN)
        lr, li = pwr_ref[0:1, :], pwi_ref[0:1, :]
        for r in range(1, 8):
            pr, pi = hr_ref[slab(r - 1), :], hi_ref[slab(r - 1), :]
            hr_ref[slab(r), :] = lr * pr - li * pi + hr_ref[slab(r), :]
            hi_ref[slab(r), :] = lr * pi + li * pr + hi_ref[slab(r), :]
        l8r, l8i = pwr_ref[7:8, :], pwi_ref[7:8, :]

        def across(c, carry):
            gr, gi = carry
            cin_re[pl.ds(c, 1), :] = gr
            cin_im[pl.ds(c, 1), :] = gi
            er, ei = hr_ref[pl.ds(7 * nb + c, 1), :], hi_ref[pl.ds(7 * nb + c, 1), :]
            return l8r * gr - l8i * gi + er, l8r * gi + l8i * gr + ei

        gr, gi = lax.fori_loop(0, nb, across, (car_re[...], car_im[...]))
        car_re[...] = gr
        car_im[...] = gi
        cr, ci = cin_re[...], cin_im[...]
        for r in range(8):
            pr, pi = pwr_ref[r:r + 1, :], pwi_ref[r:r + 1, :]
            hr_ref[slab(r), :] = hr_ref[slab(r), :] + pr * cr - pi * ci
            hi_ref[slab(r), :] = hi_ref[slab(r), :] + pr * ci + pi * cr
        for j in range(4):
            sl = slice(512 * j, 512 * (j + 1))
            cs = slice(128 * j, 128 * (j + 1))
            yp[:, cs] = (_dot(hr_ref[:, sl], cre_ref[j], NN) - _dot(hi_ref[:, sl], cim_ref[j], NN)
                         + d_ref[:, cs] * u[:, cs])
        for r in range(8):
            y_ref[:, r, :] = yp[slab(r), :]

    full = lambda shape: pl.BlockSpec(shape, lambda i: (0,) * len(shape))
    hspec = pl.BlockSpec((R, NS), lambda i: (i, 0))
    uspec = pl.BlockSpec((nb, 8, 512), lambda i: (i, 0, 0))
    y, h_re, h_im = _call(
        body, name, (S // R,),
        [uspec, full((8, NS)), full((8, NS)), full((4, 128, 512)),
         full((4, 128, 512)), full((4, 512, 128)), full((4, 512, 128)), full((1, 512))],
        [uspec, hspec, hspec],
        [jax.ShapeDtypeStruct((S // 8, 8, 512), f32), jax.ShapeDtypeStruct((S, NS), f32),
         jax.ShapeDtypeStruct((S, NS), f32)],
        scratch=[pltpu.VMEM((1, NS), f32), pltpu.VMEM((1, NS), f32), pltpu.VMEM((nb, NS), f32),
                 pltpu.VMEM((nb, NS), f32), pltpu.VMEM((R, 512), f32), pltpu.VMEM((R, 512), f32)],
        sem=("arbitrary",))(u.reshape(S // 8, 8, 512), pw_re, pw_im, w_re.astype(bf16), w_im.astype(bf16),
                            c_re.astype(bf16), c_im.astype(bf16), dvec)
    return y.reshape(S, 512), h_re, h_im


def s5_bwd(dy, u, h_re, h_im, pw_re, pw_im, w_re, w_im, c_re, c_im, dvec, name):
    S = u.shape[0]
    R, NS = S5_ROWS, S5_STATES
    nb = R // 8
    nchunk = S // R

    def body(dy_ref, u_ref, hr_ref, hi_ref, hpr_ref, hpi_ref, pwr_ref, pwi_ref, wre_ref, wim_ref, cre_ref, cim_ref,
             d_ref, du_ref, dwre_ref, dwim_ref, dcre_ref, dcim_ref, dlr_ref, dli_ref, dd_ref,
             ar, ai, car_re, car_im, cin_re, cin_im, up, dyp, dup):
        i = pl.program_id(0)

        @pl.when(i == 0)
        def _():
            for ref in (car_re, car_im, dwre_ref, dwim_ref, dcre_ref, dcim_ref, dlr_ref, dli_ref, dd_ref):
                ref[...] = jnp.zeros_like(ref)

        slab = lambda r: pl.ds(r * nb, nb)
        for r in range(8):
            up[slab(r), :] = u_ref[:, r, :]
            dyp[slab(r), :] = dy_ref[:, r, :]
        dy = dyp[...]
        u = up[...]
        for j in range(4):
            dyj = dy[:, 128 * j:128 * (j + 1)]
            ar[:, 512 * j:512 * (j + 1)] = _dot(dyj, cre_ref[j], NT)
            ai[:, 512 * j:512 * (j + 1)] = -_dot(dyj, cim_ref[j], NT)
        lr, li = pwr_ref[0:1, :], pwi_ref[0:1, :]
        for r in range(6, -1, -1):
            nr, ni = ar[slab(r + 1), :], ai[slab(r + 1), :]
            ar[slab(r), :] = lr * nr + li * ni + ar[slab(r), :]
            ai[slab(r), :] = lr * ni - li * nr + ai[slab(r), :]
        l8r, l8i = pwr_ref[7:8, :], pwi_ref[7:8, :]

        def across(k, carry):
            c = nb - 1 - k
            gr, gi = carry
            cin_re[pl.ds(c, 1), :] = gr
            cin_im[pl.ds(c, 1), :] = gi
            er, ei = ar[pl.ds(c, 1), :], ai[pl.ds(c, 1), :]
            return l8r * gr + l8i * gi + er, l8r * gi - l8i * gr + ei

        gr, gi = lax.fori_loop(0, nb, across, (car_re[...], car_im[...]))
        car_re[...] = gr
        car_im[...] = gi
        cr, ci = cin_re[...], cin_im[...]
        for r in range(8):
            pr, pi = pwr_ref[7 - r:8 - r, :], pwi_ref[7 - r:8 - r, :]
            ar[slab(r), :] = ar[slab(r), :] + pr * cr + pi * ci
            ai[slab(r), :] = ai[slab(r), :] + pr * ci - pi * cr

        acc_r = jnp.zeros((1, NS), f32)
        acc_i = jnp.zeros((1, NS), f32)
        has_prev = (i < nchunk - 1).astype(f32)
        top = lax.broadcasted_iota(jnp.int32, (nb, NS), 0) == 0
        for r in range(8):
            if r == 0:
                xr = jnp.where(top, hpr_ref[7:8, :] * has_prev, pltpu.roll(hr_ref[slab(7), :], 1, 0))
                xi = jnp.where(top, hpi_ref[7:8, :] * has_prev, pltpu.roll(hi_ref[slab(7), :], 1, 0))
            else:
                xr, xi = hr_ref[slab(r - 1), :], hi_ref[slab(r - 1), :]
            br, bi = ar[slab(r), :], ai[slab(r), :]
            acc_r += jnp.sum(br * xr + bi * xi, axis=0, keepdims=True)
            acc_i += jnp.sum(bi * xr - br * xi, axis=0, keepdims=True)
        dlr_ref[...] += acc_r
        dli_ref[...] += acc_i
        dd_ref[...] += jnp.sum(dy * u, axis=0, keepdims=True)

        for j in range(4):
            sl = slice(512 * j, 512 * (j + 1))
            cs = slice(128 * j, 128 * (j + 1))
            arj, aij = ar[:, sl], ai[:, sl]
            uj, dyj = u[:, cs], dy[:, cs]
            dup[:, cs] = _dot(arj, wre_ref[j], NT) + _dot(aij, wim_ref[j], NT) + d_ref[:, cs] * dyj
            dwre_ref[j] += _dot(uj, arj, TN)
            dwim_ref[j] += _dot(uj, aij, TN)
            dcre_ref[j] += _dot(hr_ref[:, sl], dyj, TN)
            dcim_ref[j] -= _dot(hi_ref[:, sl], dyj, TN)
        for r in range(8):
            du_ref[:, r, :] = dup[slab(r), :]

    rev = lambda i: nchunk - 1 - i
    full = lambda shape: pl.BlockSpec(shape, lambda i: (0,) * len(shape))
    row = pl.BlockSpec((nb, 8, 512), lambda i: (rev(i), 0, 0))
    hspec = pl.BlockSpec((R, NS), lambda i: (rev(i), 0))
    hprev = pl.BlockSpec((8, NS), lambda i: (jnp.maximum(rev(i) * nb - 1, 0), 0))
    outs = _call(
        body, name, (nchunk,),
        [row, row, hspec, hspec, hprev, hprev, full((8, NS)), full((8, NS)), full((4, 128, 512)), full((4, 128, 512)),
         full((4, 512, 128)), full((4, 512, 128)), full((1, 512))],
        [row, full((4, 128, 512)), full((4, 128, 512)), full((4, 512, 128)), full((4, 512, 128)),
         full((1, NS)), full((1, NS)), full((1, 512))],
        [jax.ShapeDtypeStruct((S // 8, 8, 512), f32), jax.ShapeDtypeStruct((4, 128, 512), f32),
         jax.ShapeDtypeStruct((4, 128, 512), f32), jax.ShapeDtypeStruct((4, 512, 128), f32),
         jax.ShapeDtypeStruct((4, 512, 128), f32), jax.ShapeDtypeStruct((1, NS), f32),
         jax.ShapeDtypeStruct((1, NS), f32), jax.ShapeDtypeStruct((1, 512), f32)],
        scratch=[pltpu.VMEM((R, NS), f32), pltpu.VMEM((R, NS), f32), pltpu.VMEM((1, NS), f32),
                 pltpu.VMEM((1, NS), f32), pltpu.VMEM((nb, NS), f32), pltpu.VMEM((nb, NS), f32),
                 pltpu.VMEM((R, 512), f32), pltpu.VMEM((R, 512), f32), pltpu.VMEM((R, 512), f32)],
        sem=("arbitrary",))(dy.reshape(S // 8, 8, 512), u.reshape(S // 8, 8, 512), h_re, h_im, h_re, h_im, pw_re,
                            pw_im, w_re.astype(bf16), w_im.astype(bf16), c_re.astype(bf16), c_im.astype(bf16), dvec)
    return (outs[0].reshape(S, 512),) + tuple(outs[1:])


def _rows(start, n, d):
    return pl.ds(pl.multiple_of(start, ATT_BLOCK), n) if d == 1 else pl.ds(start, n, stride=d)


def _head_masks():
    lane = lax.broadcasted_iota(jnp.int32, (1, LANES), 1)
    return [(lane < 64).astype(f32), (lane >= 64).astype(f32)]


def _head_norm(x, w, hm):
    x2 = x * x
    r = [lax.rsqrt(jnp.sum(x2 * hm[h], axis=-1, keepdims=True) * (1.0 / 64) + RMS_EPS) for h in range(2)]
    sc = hm[0] * r[0] + hm[1] * r[1]
    return x * sc * w, sc, r


def _head_norm_bwd(x, w, sc, r, dxn, hm):
    dw = jnp.sum(dxn * x * sc, axis=0, keepdims=True)
    t = dxn * w
    tx = t * x
    corr = sum(hm[h] * (r[h] * r[h] * r[h]) * jnp.sum(tx * hm[h], axis=-1, keepdims=True) for h in range(2))
    return t * sc - x * corr * (1.0 / 64), dw


def _att_mask(has_prev):
    qi = lax.broadcasted_iota(jnp.int32, (ATT_BLOCK, 2 * ATT_BLOCK), 0) + ATT_BLOCK
    kj = lax.broadcasted_iota(jnp.int32, (ATT_BLOCK, 2 * ATT_BLOCK), 1)
    return (qi - kj >= 0) & (qi - kj <= ATT_BLOCK) & (has_prev | (kj >= ATT_BLOCK))


def _att_block_bwd(q, k, v, o, lse, do, dlse, qw, kw, has_prev):
    hm = _head_masks()
    mask = _att_mask(has_prev)
    qn, qsc, qr = _head_norm(q, qw, hm)
    kn, ksc, kr = _head_norm(k, kw, hm)
    dqn = jnp.zeros((ATT_BLOCK, LANES), f32)
    dkn = jnp.zeros((2 * ATT_BLOCK, LANES), f32)
    dv = jnp.zeros((2 * ATT_BLOCK, LANES), f32)
    for h in range(2):
        qh, do_h = qn * hm[h], do * hm[h]
        s = _dot(qh, kn, NT) * 0.125
        p = jnp.exp(jnp.where(mask, s - lse[:, 64 * h:64 * h + 1], -jnp.inf))
        dp = _dot(do_h, v, NT)
        delta = jnp.sum(do_h * o, axis=-1, keepdims=True)
        dl = jnp.sum(dlse * hm[h], axis=-1, keepdims=True)
        ds = p * (dp - delta + dl) * 0.125
        dqn = dqn + hm[h] * _dot(ds, kn, NN)
        dkn = dkn + _dot(ds, qh, TN)
        dv = dv + _dot(p, do_h, TN)
    dq, dqw = _head_norm_bwd(q, qw, qsc, qr, dqn, hm)
    dk, dkw = _head_norm_bwd(k, kw, ksc, kr, dkn, hm)
    return dq, dk, dv, dqw, dkw


def _att_block(q, k, v, qw, kw, has_prev):
    hm = _head_masks()
    qn, kn = _head_norm(q, qw, hm)[0], _head_norm(k, kw, hm)[0]
    mask = _att_mask(has_prev)
    o = jnp.zeros((ATT_BLOCK, LANES), f32)
    lse = jnp.zeros((ATT_BLOCK, LANES), f32)
    for h in range(2):
        s = _bdot(qn * hm[h], kn, NT) * 0.125
        s = jnp.where(mask, s, -jnp.inf)
        m = jnp.max(s, axis=-1, keepdims=True)
        p = jnp.exp(s - m)
        l = jnp.sum(p, axis=-1, keepdims=True)
        o = o + hm[h] * _bdot(p / l, v, NN)
        lse = lse + hm[h] * (m + jnp.log(l))
    return o, lse


def att_fwd(p_att, qw, kw, d, g, name):
    S = p_att.shape[0]
    SEG = ATT_SEG
    nblk = SEG // ATT_BLOCK

    def body(p_ref, qw_ref, kw_ref, o_ref, l_ref, q_s, k_ext, v_ext, o_s, l_s):
        seg = pl.program_id(1)

        @pl.when(seg == 0)
        def _():
            k_ext[SEG:, :] = jnp.zeros((SEG, LANES), f32)
            v_ext[SEG:, :] = jnp.zeros((SEG, LANES), f32)

        k_ext[:SEG, :] = k_ext[SEG:, :]
        v_ext[:SEG, :] = v_ext[SEG:, :]
        q_s[...] = p_ref[:, 0:128]
        k_ext[SEG:, :] = p_ref[:, 128:256]
        v_ext[SEG:, :] = p_ref[:, 256:384]
        qw_v, kw_v = qw_ref[...], kw_ref[...]

        def blk(b, carry):
            j, r = b // d, b % d
            qs = j * (ATT_BLOCK * d) + r
            ks = SEG + qs - ATT_BLOCK * d
            o, lse = _att_block(q_s[_rows(qs, ATT_BLOCK, d), :], k_ext[_rows(ks, 2 * ATT_BLOCK, d), :],
                                v_ext[_rows(ks, 2 * ATT_BLOCK, d), :], qw_v, kw_v, (seg > 0) | (j > 0))
            o_s[_rows(qs, ATT_BLOCK, d), :] = o
            l_s[_rows(qs, ATT_BLOCK, d), :] = lse
            return carry

        lax.fori_loop(0, nblk, blk, 0, unroll=8)
        o_ref[...] = o_s[...]
        l_ref[...] = l_s[...]

    vec = pl.BlockSpec((1, LANES), lambda hh, s: (0, 0))
    out = pl.BlockSpec((SEG, LANES), lambda hh, s: (s, hh))
    return _call(body, name, (2, S // SEG), [pl.BlockSpec((SEG, 384), lambda hh, s: (s, MAIN_ATT_BLOCK + 2 * g + hh)), vec, vec],
                 [out, out], [jax.ShapeDtypeStruct((S, 256), f32), jax.ShapeDtypeStruct((S, 256), f32)],
                 scratch=[pltpu.VMEM((SEG, LANES), f32), pltpu.VMEM((2 * SEG, LANES), f32),
                          pltpu.VMEM((2 * SEG, LANES), f32), pltpu.VMEM((SEG, LANES), f32),
                          pltpu.VMEM((SEG, LANES), f32)],
                 sem=("arbitrary", "arbitrary"))(p_att, qw, kw)


def att_bwd(p_att, o, lse, do, dlse, qw, kw, d, g, dp_main, name):
    S = p_att.shape[0]
    SEG = ATT_SEG
    nseg = S // SEG
    nblk = SEG // ATT_BLOCK

    def body(p_ref, pp_ref, o_ref, l_ref, do_ref, dl_ref, qw_ref, kw_ref, _, dp_ref, dqw_ref, dkw_ref,
             q_s, k_ext, v_ext, dq_s, dk_ext, dv_ext):
        hh, i = pl.program_id(0), pl.program_id(1)
        seg = nseg - 1 - i

        @pl.when(i == 0)
        def _():
            dk_ext[...] = jnp.zeros_like(dk_ext)
            dv_ext[...] = jnp.zeros_like(dv_ext)

        @pl.when((i == 0) & (hh == 0))
        def _():
            dqw_ref[...] = jnp.zeros_like(dqw_ref)
            dkw_ref[...] = jnp.zeros_like(dkw_ref)

        dk_ext[SEG:, :] = dk_ext[:SEG, :]
        dv_ext[SEG:, :] = dv_ext[:SEG, :]
        dk_ext[:SEG, :] = jnp.zeros((SEG, LANES), f32)
        dv_ext[:SEG, :] = jnp.zeros((SEG, LANES), f32)
        q_s[...] = p_ref[:, 0:128]
        k_ext[SEG:, :] = p_ref[:, 128:256]
        v_ext[SEG:, :] = p_ref[:, 256:384]
        k_ext[:SEG, :] = pp_ref[:, 128:256]
        v_ext[:SEG, :] = pp_ref[:, 256:384]
        qw_v, kw_v = qw_ref[...], kw_ref[...]

        per_step = 4

        def blk_group(i2, carry):
            dqw, dkw = carry
            done = []
            for u in range(per_step):
                b = per_step * i2 + u
                j, r = b // d, b % d
                qs = j * (ATT_BLOCK * d) + r
                ks = SEG + qs - ATT_BLOCK * d
                has_prev = (seg > 0) | (j > 0)
                qrows, krows = _rows(qs, ATT_BLOCK, d), _rows(ks, 2 * ATT_BLOCK, d)
                dq, dk, dv, dqw_b, dkw_b = _att_block_bwd(
                    q_s[qrows, :], k_ext[krows, :], v_ext[krows, :], o_ref[qrows, :], l_ref[qrows, :],
                    do_ref[qrows, :], dl_ref[qrows, :], qw_v, kw_v, has_prev)
                dqw, dkw = dqw + dqw_b, dkw + dkw_b
                done.append((qrows, krows, dq, dk, dv))
            for qrows, krows, dq, dk, dv in done:
                dq_s[qrows, :] = dq
                dk_ext[krows, :] = dk_ext[krows, :] + dk
                dv_ext[krows, :] = dv_ext[krows, :] + dv
            return dqw, dkw

        zero = jnp.zeros((1, LANES), f32)
        dqw, dkw = lax.fori_loop(0, nblk // per_step, blk_group, (zero, zero))
        dqw_ref[...] += dqw
        dkw_ref[...] += dkw
        dp_ref[:, 0:128] = dq_s[...].astype(bf16)
        dp_ref[:, 128:256] = dk_ext[SEG:, :].astype(bf16)
        dp_ref[:, 256:384] = dv_ext[SEG:, :].astype(bf16)

    rev = lambda i: nseg - 1 - i
    vec = pl.BlockSpec((1, LANES), lambda hh, i: (0, 0))
    blk = MAIN_ATT_BLOCK + 2 * g
    cur = pl.BlockSpec((SEG, 384), lambda hh, i: (rev(i), blk + hh))
    prev = pl.BlockSpec((SEG, 384), lambda hh, i: (jnp.maximum(rev(i) - 1, 0), blk + hh))
    col = pl.BlockSpec((SEG, LANES), lambda hh, i: (rev(i), hh))
    big = pltpu.VMEM((2 * SEG, LANES), f32)
    one = pltpu.VMEM((SEG, LANES), f32)
    return _call(body, name, (2, nseg), [cur, prev, col, col, col, col, vec, vec, _ANY], [cur, vec, vec],
                 [jax.ShapeDtypeStruct((S, MAIN_WIDTH), bf16), jax.ShapeDtypeStruct((1, LANES), f32),
                  jax.ShapeDtypeStruct((1, LANES), f32)],
                 scratch=[one, big, big, one, big, big], sem=("arbitrary", "arbitrary"),
                 aliases={8: 0})(p_att, p_att, o, lse, do, dlse, qw, kw, dp_main)


def conv_fwd(p_ssd, conv_w, conv_b, name):
    S = p_ssd.shape[0]
    tm, C = CONV_ROWS, SSD_XBC

    def body(x_ref, xp_ref, w_ref, b_ref, o_ref):
        first = (pl.program_id(0) == 0)
        ext = jnp.concatenate([jnp.where(first, 0.0, xp_ref[:, 0:C]), x_ref[:, 0:C]], axis=0)
        acc = b_ref[...] + w_ref[3:4, :] * ext[8:, :]
        for k in range(1, 4):
            acc = acc + w_ref[3 - k:4 - k, :] * pltpu.roll(ext, k, 0)[8:, :]
        o_ref[...] = jax.nn.silu(acc)

    return _call(body, name, (S // tm,),
                 [pl.BlockSpec((tm, 1536), lambda i: (i, MAIN_SSD_BLOCK)),
                  pl.BlockSpec((8, 1536), lambda i: (jnp.maximum(i * (tm // 8) - 1, 0), MAIN_SSD_BLOCK)),
                  pl.BlockSpec((4, C), lambda i: (0, 0)), pl.BlockSpec((1, C), lambda i: (0, 0))],
                 pl.BlockSpec((tm, C), lambda i: (i, 0)), jax.ShapeDtypeStruct((S, C), f32),
                 sem=("parallel",))(p_ssd, p_ssd, conv_w, conv_b)


def conv_bwd(p_ssd, dact, ddt, conv_w, conv_b, dp_main, name):
    S = p_ssd.shape[0]
    tm, C = CONV_ROWS, SSD_XBC
    nblk = S // tm

    def body(x_ref, xp_ref, xn_ref, da_ref, dan_ref, ddt_ref, w_ref, b_ref, _, dp_ref, dw_ref, db_ref):
        i = pl.program_id(0)
        rows = tm + 8
        ext = jnp.concatenate([jnp.where(i == 0, 0.0, xp_ref[:, 0:C]), x_ref[:, 0:C], xn_ref[:, 0:C]], axis=0)
        shifted = [ext[8:, :]] + [pltpu.roll(ext, k, 0)[8:, :] for k in range(1, 4)]
        pre = b_ref[...] + w_ref[3:4, :] * shifted[0]
        for k in range(1, 4):
            pre = pre + w_ref[3 - k:4 - k, :] * shifted[k]
        sg = jax.nn.sigmoid(pre)
        dact = jnp.concatenate([da_ref[...], jnp.where(i == nblk - 1, 0.0, dan_ref[...])], axis=0)
        dpre = dact * (sg * (1.0 + pre * (1.0 - sg)))
        dx = w_ref[3:4, :] * dpre[0:tm, :]
        for k in range(1, 4):
            dx = dx + w_ref[3 - k:4 - k, :] * pltpu.roll(dpre, rows - k, 0)[0:tm, :]
        dp_ref[:, 0:C] = dx.astype(bf16)
        dp_ref[:, C:C + 128] = ddt_ref[...].astype(bf16)
        dp_ref[:, C + 128:] = jnp.zeros((tm, 128), bf16)
        dcur = dpre[0:tm, :]
        dws = [jnp.sum(dcur * shifted[3 - j][0:tm, :], axis=0, keepdims=True) for j in range(4)]
        dbs = jnp.sum(dcur, axis=0, keepdims=True)

        @pl.when(i == 0)
        def _():
            dw_ref[...] = jnp.zeros_like(dw_ref)
            db_ref[...] = jnp.zeros_like(db_ref)

        for j in range(4):
            dw_ref[j:j + 1, :] += dws[j]
        db_ref[...] += dbs

    t8 = tm // 8
    blk = MAIN_SSD_BLOCK
    return _call(body, name, (nblk,),
                 [pl.BlockSpec((tm, 1536), lambda i: (i, blk)),
                  pl.BlockSpec((8, 1536), lambda i: (jnp.maximum(i * t8 - 1, 0), blk)),
                  pl.BlockSpec((8, 1536), lambda i: (jnp.minimum((i + 1) * t8, S // 8 - 1), blk)),
                  pl.BlockSpec((tm, C), lambda i: (i, 0)),
                  pl.BlockSpec((8, C), lambda i: (jnp.minimum((i + 1) * t8, S // 8 - 1), 0)),
                  pl.BlockSpec((tm, 128), lambda i: (i, 0)),
                  pl.BlockSpec((4, C), lambda i: (0, 0)), pl.BlockSpec((1, C), lambda i: (0, 0)), _ANY],
                 [pl.BlockSpec((tm, 1536), lambda i: (i, blk)), pl.BlockSpec((4, C), lambda i: (0, 0)),
                  pl.BlockSpec((1, C), lambda i: (0, 0))],
                 [jax.ShapeDtypeStruct((S, MAIN_WIDTH), bf16), jax.ShapeDtypeStruct((4, C), f32),
                  jax.ShapeDtypeStruct((1, C), f32)],
                 sem=("arbitrary",), aliases={8: 0})(p_ssd, p_ssd, p_ssd, dact, dact, ddt, conv_w, conv_b, dp_main)


def _ssd_chunk(xbc, dtr, state, dt_bias, a_log, d_full):
    T = SSD_CHUNK
    r_i = lax.broadcasted_iota(jnp.int32, (T, T), 0)
    c_i = lax.broadcasted_iota(jnp.int32, (T, T), 1)
    tril = c_i <= r_i
    tri = tril.astype(bf16)
    lane = lax.broadcasted_iota(jnp.int32, (1, LANES), 1)
    hm = [(lane < 64).astype(f32), (lane >= 64).astype(f32)]
    column = lambda v, h: jnp.broadcast_to(v[:, h:h + 1], (T, LANES))

    def per_head_lanes(v):
        return jnp.concatenate([jnp.where(lane < 64, column(v, 2 * pp), column(v, 2 * pp + 1)) for pp in range(6)],
                               axis=1)

    xs, bm, cm = xbc[:, :768], xbc[:, 768:1024], xbc[:, 1024:1280]
    dt = _softplus(dtr + dt_bias)
    a_dt = dt * (-jnp.exp(a_log))
    a_cs = _xdot_l(tri, a_dt)
    dt_full = per_head_lanes(dt)
    acs_full = per_head_lanes(a_cs)
    last = lax.broadcasted_iota(jnp.int32, (T, SSD_WIDTH), 0) == T - 1
    tot_full = jnp.sum(jnp.where(last, acs_full, 0.0), axis=0, keepdims=True)
    xdt = xs * dt_full
    xw = xdt * jnp.exp(tot_full - acs_full)
    eacs = jnp.exp(acs_full)
    st_parts, off_parts, diag_parts = [], [], []
    for g in range(2):
        bg, cg = bm[:, 128 * g:128 * (g + 1)], cm[:, 128 * g:128 * (g + 1)]
        cols = slice(384 * g, 384 * (g + 1))
        st_parts.append(_bdot(bg, xw[:, cols], TN))
        off_parts.append(_bdot(cg, state[:, cols], NN))
        cb = _bdot(cg, bg, NT)
        for pp in range(3 * g, 3 * g + 3):
            xp = xdt[:, 128 * pp:128 * (pp + 1)]
            acc = jnp.zeros((T, LANES), f32)
            for hh in range(2):
                a_col = column(a_cs, 2 * pp + hh)
                decay = jnp.where(tril, jnp.exp(jnp.minimum(a_col - a_col.T, 0.0)), 0.0)
                acc = acc + _bdot(cb * decay, xp * hm[hh], NN)
            diag_parts.append(acc)
    new_state = state * jnp.exp(tot_full) + jnp.concatenate(st_parts, axis=1)
    y = jnp.concatenate(diag_parts, axis=1) + jnp.concatenate(off_parts, axis=1) * eacs + xs * d_full
    return y, new_state


def ssd_fwd(xact, p_ssd, dt_bias, a_log, d_full, name):
    S = xact.shape[0]
    T = SSD_CHUNK

    U = SSD_CHUNKS_PER_STEP

    def body(x_ref, p_ref, b_ref, a_ref, d_ref, y_ref, s_ref, state):
        @pl.when(pl.program_id(0) == 0)
        def _():
            state[...] = jnp.zeros_like(state)

        st = state[...]
        for u in range(U):
            rows = slice(T * u, T * (u + 1))
            s_ref[u] = st
            y, st = _ssd_chunk(x_ref[rows, :], p_ref[rows, :], st, b_ref[...], a_ref[...], d_ref[...])
            y_ref[rows, :] = y
        state[...] = st

    vec = lambda n: pl.BlockSpec((1, n), lambda i: (0, 0))
    return _call(body, name, (S // (U * T),),
                 [pl.BlockSpec((U * T, SSD_XBC), lambda i: (i, 0)),
                  pl.BlockSpec((U * T, 128), lambda i: (i, MAIN_DT_BLOCK)), vec(128), vec(128), vec(768)],
                 [pl.BlockSpec((U * T, 768), lambda i: (i, 0)), pl.BlockSpec((U, T, 768), lambda i: (i, 0, 0))],
                 [jax.ShapeDtypeStruct((S, 768), f32), jax.ShapeDtypeStruct((S // T, T, 768), f32)],
                 scratch=[pltpu.VMEM((T, 768), f32)], sem=("arbitrary",))(xact, p_ssd, dt_bias, a_log, d_full)


def ssd_bwd(xact, p_ssd, states, dy, dt_bias, a_log, d_full, name):
    S = xact.shape[0]
    T = SSD_CHUNK
    U = 1
    nc = S // (U * T)

    def body(x_ref, p_ref, s_ref, dy_ref, b_ref, a_ref, d_ref, dx_ref, ddt_ref, db_ref, da_ref, dd_ref, dstate):
        i = pl.program_id(0)

        @pl.when(i == 0)
        def _():
            for ref in (dstate, db_ref, da_ref, dd_ref):
                ref[...] = jnp.zeros_like(ref)

        dst = dstate[...]
        for u in reversed(range(U)):
            rows = slice(T * u, T * (u + 1))
            _, vjp = jax.vjp(_ssd_chunk, x_ref[rows, :], p_ref[rows, :], s_ref[u], b_ref[...], a_ref[...], d_ref[...])
            dx, ddt, dst, db, da, dd = vjp((dy_ref[rows, :], dst))
            dx_ref[rows, :] = dx
            ddt_ref[rows, :] = ddt
            db_ref[...] += db
            da_ref[...] += da
            dd_ref[...] += dd
        dstate[...] = dst

    rev = lambda i: nc - 1 - i
    vec = lambda n: pl.BlockSpec((1, n), lambda i: (0, 0))
    return _call(body, name, (nc,),
                 [pl.BlockSpec((U * T, SSD_XBC), lambda i: (rev(i), 0)),
                  pl.BlockSpec((U * T, 128), lambda i: (rev(i), MAIN_DT_BLOCK)),
                  pl.BlockSpec((U, T, 768), lambda i: (rev(i), 0, 0)), pl.BlockSpec((U * T, 768), lambda i: (rev(i), 0)),
                  vec(128), vec(128), vec(768)],
                 [pl.BlockSpec((U * T, SSD_XBC), lambda i: (rev(i), 0)), pl.BlockSpec((U * T, 128), lambda i: (rev(i), 0)),
                  vec(128), vec(128), vec(768)],
                 [jax.ShapeDtypeStruct((S, SSD_XBC), f32), jax.ShapeDtypeStruct((S, 128), f32),
                  jax.ShapeDtypeStruct((1, 128), f32), jax.ShapeDtypeStruct((1, 128), f32),
                  jax.ShapeDtypeStruct((1, 768), f32)],
                 scratch=[pltpu.VMEM((T, 768), f32)],
                 sem=("arbitrary",))(xact, p_ssd, states, dy, dt_bias, a_log, d_full)


def _tail_fn(ys5, pt, o0, o1, o2, l0, l1, l2, yssd, glu_b, nw, pr_glu, pr_a, pr_b, pr_c, x, weights):
    glu_w, pa, pb, pc, wo = weights
    gates = jax.nn.sigmoid(pt[:, :3072])
    za, zb, zc = pt[:, 3072:3584], pt[:, 3584:3840], pt[:, 3840:4608]
    g = jax.nn.gelu(ys5)
    ya = g * jax.nn.sigmoid(_cdot(g, glu_w, NN) + glu_b + pr_glu) * jax.nn.silu(za)
    m = jnp.maximum(jnp.maximum(l0, l1), l2)
    e0, e1, e2 = jnp.exp(l0 - m), jnp.exp(l1 - m), jnp.exp(l2 - m)
    yb = (e0 * o0 + e1 * o1 + e2 * o2) / (e0 + e1 + e2) * jax.nn.silu(zb)
    yc = _rms(yssd * jax.nn.silu(zc), nw)
    merged = (gates[:, :1024] * (_cdot(ya, pa, NN) + pr_a) + gates[:, 1024:2048] * (_cdot(yb, pb, NN) + pr_b)
              + gates[:, 2048:] * (_cdot(yc, pc, NN) + pr_c))
    out = x + _cdot(merged, wo, NN)
    return out, (g, ya, yb, yc, merged)


def _tail_specs(tm):
    row = lambda n: pl.BlockSpec((tm, n), lambda i: (i, 0))
    full = lambda a, b: pl.BlockSpec((a, b), lambda i: (0, 0))
    acts = [row(512), row(4608)] + [row(256)] * 6 + [row(768), row(D_MODEL)]
    consts = [full(1, 512), full(1, 768), full(512, 512), full(512, D_MODEL), full(256, D_MODEL),
              full(768, D_MODEL), full(D_MODEL, D_MODEL)]
    return row, full, acts, consts


def tail_fwd(ys5, pt, os_, ls_, yssd, x, glu_b, nw, weights, name, next_norm_w=None, target=None):
    S = x.shape[0]
    tm = TAIL_ROWS
    row, full, acts, consts = _tail_specs(tm)

    def body(ys5_ref, pt_ref, o0, o1, o2, l0, l1, l2, yssd_ref, x_ref, gb_ref, nw_ref, gw, pa, pb, pc, wo, *rest):
        z = lambda n: jnp.zeros((tm, n), f32)
        out, _ = _tail_fn(ys5_ref[...], pt_ref[...], o0[...], o1[...], o2[...], l0[...], l1[...], l2[...],
                          yssd_ref[...], gb_ref[...], nw_ref[...], z(512), z(D_MODEL), z(D_MODEL), z(D_MODEL),
                          x_ref[...], (gw[...], pa[...], pb[...], pc[...], wo[...]))
        if target is not None:
            t_ref, dy_ref, l_ref = rest
            diff = out - t_ref[...]
            dy_ref[...] = diff * (1.0 / D_MODEL)
            part = jnp.full((8, LANES), 0.5 / D_MODEL * jnp.sum(diff * diff), f32)

            @pl.when(pl.program_id(0) == 0)
            def _():
                l_ref[...] = part

            @pl.when(pl.program_id(0) > 0)
            def _():
                l_ref[...] += part
        elif next_norm_w is not None:
            n_ref, out_ref, h_ref = rest
            out_ref[...] = out
            h_ref[...] = _rms(out, n_ref[...]).astype(bf16)
        else:
            rest[0][...] = out

    sd = jax.ShapeDtypeStruct((S, D_MODEL), f32)
    if target is not None:
        extra_in, extra_specs = [target], [row(D_MODEL)]
        out_specs = [row(D_MODEL), pl.BlockSpec((8, LANES), lambda i: (0, 0))]
        out_shape = [sd, jax.ShapeDtypeStruct((8, LANES), f32)]
    elif next_norm_w is not None:
        extra_in, extra_specs = [next_norm_w], [full(1, D_MODEL)]
        out_specs, out_shape = [row(D_MODEL), row(D_MODEL)], [sd, jax.ShapeDtypeStruct((S, D_MODEL), bf16)]
    else:
        extra_in, extra_specs, out_specs, out_shape = [], [], row(D_MODEL), sd
    return _call(body, name, (S // tm,), acts + consts + extra_specs, out_specs, out_shape,
                 sem=("arbitrary",))(ys5, pt, *os_, *ls_, yssd, x, glu_b, nw, *weights, *extra_in)


def tail_bwd(ys5, pt, os_, ls_, yssd, dout, glu_b, nw, weights, name):
    S = dout.shape[0]
    tm = TAIL_ROWS
    row, full, acts, consts = _tail_specs(tm)

    def body(ys5_ref, pt_ref, o0, o1, o2, l0, l1, l2, yssd_ref, dout_ref, gb_ref, nw_ref, gw, pa, pb, pc, wo,
             dys5_ref, dpt_ref, do0, do1, do2, dl0, dl1, dl2, dyssd_ref, dgb_ref, dnw_ref,
             g_ref, ya_ref, yb_ref, yc_ref, mg_ref, dglu_ref, dpa_ref, dpb_ref, dpc_ref):
        z = lambda n: jnp.zeros((tm, n), f32)
        w = (gw[...], pa[...], pb[...], pc[...], wo[...])
        fn = lambda *a: _tail_fn(*a, z(D_MODEL), w)
        _, vjp, aux = jax.vjp(fn, ys5_ref[...], pt_ref[...], o0[...], o1[...], o2[...], l0[...], l1[...], l2[...],
                              yssd_ref[...], gb_ref[...], nw_ref[...], z(512), z(D_MODEL), z(D_MODEL), z(D_MODEL),
                              has_aux=True)
        (dys5, dpt, d0, d1, d2, e0, e1, e2, dyssd, dgb, dnw, dglu, dpa, dpb, dpc) = vjp(dout_ref[...])
        dys5_ref[...] = dys5
        dpt_ref[...] = dpt.astype(bf16)
        for ref, val in ((do0, d0), (do1, d1), (do2, d2), (dl0, e0), (dl1, e1), (dl2, e2)):
            ref[...] = val
        dyssd_ref[...] = dyssd
        g, ya, yb, yc, merged = aux
        for ref, val in ((g_ref, g), (ya_ref, ya), (yb_ref, yb), (yc_ref, yc), (mg_ref, merged),
                         (dglu_ref, dglu), (dpa_ref, dpa), (dpb_ref, dpb), (dpc_ref, dpc)):
            ref[...] = val.astype(bf16)

        @pl.when(pl.program_id(0) == 0)
        def _():
            dgb_ref[...] = dgb
            dnw_ref[...] = dnw

        @pl.when(pl.program_id(0) > 0)
        def _():
            dgb_ref[...] += dgb
            dnw_ref[...] += dnw

    sd = lambda n, dt=f32: jax.ShapeDtypeStruct((S, n), dt)
    out_specs = ([row(512), row(4608)] + [row(256)] * 6 + [row(768), full(1, 512), full(1, 768)]
                 + [row(512), row(512), row(256), row(768), row(D_MODEL), row(512)] + [row(D_MODEL)] * 3)
    out_shape = ([sd(512), sd(MAIN_WIDTH, bf16)] + [sd(256)] * 6 + [sd(768), jax.ShapeDtypeStruct((1, 512), f32),
                                                          jax.ShapeDtypeStruct((1, 768), f32)]
                 + [sd(512, bf16), sd(512, bf16), sd(256, bf16), sd(768, bf16), sd(D_MODEL, bf16), sd(512, bf16)]
                 + [sd(D_MODEL, bf16)] * 3)
    return _call(body, name, (S // tm,), acts + consts, out_specs, out_shape,
                 sem=("arbitrary",))(ys5, pt, *os_, *ls_, yssd, dout, glu_b, nw, *weights)


def _in_proj_segments(shards):
    dtype = shards[0].dtype

    def c(a, b):
        parts = []
        for k, sh in enumerate(shards):
            lo, hi = max(a, W_IN_SHARD * k), min(b, W_IN_SHARD * (k + 1))
            if lo < hi:
                parts.append(sh[:, lo - W_IN_SHARD * k:hi - W_IN_SHARD * k])
        return parts[0] if len(parts) == 1 else jnp.concatenate(parts, axis=1)

    atts = []
    for g in range(3):
        parts = []
        for hh in range(2):
            o = 64 * (4 * g + 2 * hh)
            parts += [c(_C_Q + o, _C_Q + o + 128), c(_C_K + o, _C_K + o + 128), c(_C_V + o, _C_V + o + 128)]
        atts.append(jnp.concatenate(parts, axis=1))
    ssd = jnp.concatenate([c(_C_XBC, _C_ZC), jnp.zeros((D_MODEL, 1536 - (_C_ZC - _C_XBC)), dtype)], axis=1)
    tail = jnp.concatenate([c(_C_GATE, _C_END), c(_C_ZA, _C_Q), c(_C_ZB, _C_XBC), c(_C_ZC, _C_GATE)], axis=1)
    return [c(_C_UA, _C_ZA), jnp.concatenate([tail, ssd] + atts, axis=1)]


def _in_proj_grad(ds5, dmain):
    dtail, dssd = dmain[:, :4608], dmain[:, 4608:6144]
    datts = [dmain[:, 6144 + 768 * g:6144 + 768 * (g + 1)] for g in range(3)]
    pick = lambda off: [datts[g][:, 384 * hh + off:384 * hh + off + 128] for g in range(3) for hh in range(2)]
    pieces = ([ds5, dtail[:, 3072:3584]] + pick(0) + pick(128) + pick(256)
              + [dtail[:, 3584:3840], dssd[:, :_C_ZC - _C_XBC], dtail[:, 3840:4608], dtail[:, :3072]])
    shards, start = [[] for _ in range(4)], 0
    for piece in pieces:
        width = piece.shape[1]
        for k in range(4):
            lo, hi = max(start, W_IN_SHARD * k), min(start + width, W_IN_SHARD * (k + 1))
            if lo < hi:
                shards[k].append(piece[:, lo - start:hi - start])
        start += width
    return jnp.stack([jnp.concatenate(s, axis=1) for s in shards])


def _prep_layer(p):
    q = {}
    q["segs"] = [s.astype(bf16) for s in _in_proj_segments(p["w_in"])]
    disc = _s5_discretize(p["s5_a_re"], p["s5_a_im"], p["s5_log_step"], p["s5_b_re"], p["s5_b_im"],
                          p["s5_c_re"], p["s5_c_im"])
    q["s5"] = disc
    q["pw"] = _lam_powers(disc[0], disc[1])
    q["s5_d"] = p["s5_d"].reshape(1, 512)
    q["qw"] = jnp.tile(p["q_norm_w"], 2).reshape(1, LANES)
    q["kw"] = jnp.tile(p["k_norm_w"], 2).reshape(1, LANES)
    q["conv_w"] = p["conv_w"]
    q["conv_b"] = p["conv_b"].reshape(1, SSD_XBC)
    pad = lambda v: jnp.pad(v, (0, LANES - v.shape[0])).reshape(1, LANES)
    q["dt_bias"], q["a_log"] = pad(p["dt_bias"]), pad(p["ssd_a_log"])
    q["d_full"] = jnp.repeat(p["ssd_d"], 64).reshape(1, SSD_WIDTH)
    q["glu_b"] = p["s5_glu_b"].reshape(1, 512)
    q["nw"] = p["ssd_norm_w"].reshape(1, SSD_WIDTH)
    q["norm_w"] = p["norm_w"].reshape(1, D_MODEL)
    q["tailw"] = tuple(p[n].astype(bf16) for n in ("s5_glu_w", "proj_a", "proj_b", "proj_c", "w_out"))
    return q


_DILATIONS = (1, 4, 16)


def layer_fwd(x, q, tag, h=None, next_norm_w=None, target=None):
    if h is None:
        h = rms_fwd(x, q["norm_w"], f"rms_fwd{tag}")
    p_s5, p_main = [mm_nn(h, w, f"inproj{k}{tag}") for k, w in enumerate(q["segs"])]
    _, _, w_re, w_im, c_re, c_im = q["s5"]
    ys5, h_re, h_im = s5_fwd(p_s5, *q["pw"], w_re, w_im, c_re, c_im, q["s5_d"], f"s5_fwd{tag}")
    os_, ls_ = [], []
    for g, d in enumerate(_DILATIONS):
        o, l = att_fwd(p_main, q["qw"], q["kw"], d, g, f"att_fwd{g}{tag}")
        os_.append(o)
        ls_.append(l)
    xact = conv_fwd(p_main, q["conv_w"], q["conv_b"], f"conv_fwd{tag}")
    yssd, states = ssd_fwd(xact, p_main, q["dt_bias"], q["a_log"], q["d_full"], f"ssd_fwd{tag}")
    out = tail_fwd(ys5, p_main, os_, ls_, yssd, x, q["glu_b"], q["nw"], q["tailw"], f"tail_fwd{tag}",
                   next_norm_w=next_norm_w, target=target)
    saved = dict(x=x, h=h, p_s5=p_s5, p_main=p_main, ys5=ys5, h_re=h_re, h_im=h_im,
                 os=os_, ls=ls_, xact=xact, yssd=yssd, states=states)
    return out, saved


def layer_bwd(dout, sv, q, p, tag):
    (dys5, dp_main, do0, do1, do2, dl0, dl1, dl2, dyssd, dglu_b, dnw, g_b, ya_b, yb_b, yc_b, mg_b, dglu_b16,
     dpa_b, dpb_b, dpc_b) = tail_bwd(sv["ys5"], sv["p_main"], sv["os"], sv["ls"], sv["yssd"], dout, q["glu_b"],
                                     q["nw"], q["tailw"], f"tail_bwd{tag}")
    grads = {}
    grads["s5_glu_w"] = mm_tn(g_b, dglu_b16, f"dglu_w{tag}")
    grads["proj_a"] = mm_tn(ya_b, dpa_b, f"dproj_a{tag}")
    grads["proj_b"] = mm_tn(yb_b, dpb_b, f"dproj_b{tag}")
    grads["proj_c"] = mm_tn(yc_b, dpc_b, f"dproj_c{tag}")
    grads["w_out"] = mm_tn(mg_b, dout, f"dw_out{tag}")
    grads["s5_glu_b"] = dglu_b.reshape(512)
    grads["ssd_norm_w"] = dnw.reshape(SSD_WIDTH)

    dxact, ddt, ddt_bias, da_log, dd_full = ssd_bwd(sv["xact"], sv["p_main"], sv["states"], dyssd, q["dt_bias"],
                                                    q["a_log"], q["d_full"], f"ssd_bwd{tag}")
    dp_main, dconv_w, dconv_b = conv_bwd(sv["p_main"], dxact, ddt, q["conv_w"], q["conv_b"], dp_main,
                                         f"conv_bwd{tag}")
    grads["dt_bias"] = ddt_bias[0, :12]
    grads["ssd_a_log"] = da_log[0, :12]
    grads["ssd_d"] = dd_full.reshape(12, 64).sum(axis=1)
    grads["conv_w"] = dconv_w
    grads["conv_b"] = dconv_b.reshape(SSD_XBC)

    dqw, dkw = 0.0, 0.0
    for g, d in enumerate(_DILATIONS):
        dp_main, a, b = att_bwd(sv["p_main"], sv["os"][g], sv["ls"][g], (do0, do1, do2)[g], (dl0, dl1, dl2)[g],
                                q["qw"], q["kw"], d, g, dp_main, f"att_bwd{g}{tag}")
        dqw, dkw = dqw + a, dkw + b
    grads["q_norm_w"] = dqw.reshape(2, 64).sum(axis=0)
    grads["k_norm_w"] = dkw.reshape(2, 64).sum(axis=0)

    _, _, w_re, w_im, c_re, c_im = q["s5"]
    dp_s5, dwre, dwim, dcre, dcim, dlam_re, dlam_im, dd = s5_bwd(
        dys5, sv["p_s5"], sv["h_re"], sv["h_im"], *q["pw"], w_re, w_im, c_re, c_im, q["s5_d"], f"s5_bwd{tag}")
    s5_names = ("s5_a_re", "s5_a_im", "s5_log_step", "s5_b_re", "s5_b_im", "s5_c_re", "s5_c_im")
    _, disc_vjp = jax.vjp(_s5_discretize, *[p[n] for n in s5_names])
    for n, gr in zip(s5_names, disc_vjp((dlam_re, dlam_im, dwre, dwim, dcre, dcim))):
        grads[n] = gr
    grads["s5_d"] = dd.reshape(512)

    dsegs = [dp_s5, dp_main]
    dws = [mm_tn(sv["h"], ds, f"dw_in{k}{tag}") for k, ds in enumerate(dsegs)]
    grads["w_in"] = _in_proj_grad(*dws)
    dh_main = mm_nt(dp_main, q["segs"][1], f"dh1{tag}")
    dx, dnorm_w = mm_nt_rms_bwd(dp_s5, q["segs"][0], dh_main, sv["x"], q["norm_w"], dout, f"dh0_rms_bwd{tag}")
    grads["norm_w"] = dnorm_w.reshape(D_MODEL)
    return dx, grads


def _exchange(name, scatter=(), gather=(), sibling=(), sibling_both=False, sibling_by_core=None):
    scatter, gather, sibling = list(scatter), list(gather), list(sibling)
    chip_xs = scatter + gather
    ns, nc, nb = len(scatter), len(chip_xs), len(sibling)
    n = nc + nb
    n_in = n + (2 if sibling_by_core else 0)
    n_out = n + (1 if sibling_by_core else 0)
    n_sem = 3 * nc + nb + (1 if sibling_by_core else 0)

    def body(*refs):
        x_refs, o_refs, send_sems, recv_sems = refs[:n_in], refs[n_in:n_in + n_out], refs[-2], refs[-1]
        mx, my, mc = lax.axis_index("x"), lax.axis_index("y"), lax.axis_index("c")
        me = 2 * mx + my
        copies = []
        for a in range(nc):
            for t, (px, py) in enumerate(((1 - mx, my), (mx, 1 - my), (1 - mx, 1 - my))):
                src = x_refs[a].at[2 * px + py] if a < ns else x_refs[a]
                copies.append(pltpu.make_async_remote_copy(
                    src_ref=src, dst_ref=o_refs[a].at[me], send_sem=send_sems.at[3 * a + t],
                    recv_sem=recv_sems.at[3 * a + t], device_id=(px, py, mc), device_id_type=pl.DeviceIdType.MESH))
        for b in range(nc, n):
            k = 3 * nc + b - nc
            copies.append(pltpu.make_async_remote_copy(
                src_ref=x_refs[b], dst_ref=o_refs[b].at[mc] if sibling_both else o_refs[b], send_sem=send_sems.at[k],
                recv_sem=recv_sems.at[k], device_id=(mx, my, 1 - mc), device_id_type=pl.DeviceIdType.MESH))
        for cp in copies:
            cp.start()
        if sibling_by_core:
            def pick(src):
                return pltpu.make_async_remote_copy(
                    src_ref=src, dst_ref=o_refs[n], send_sem=send_sems.at[n_sem - 1], recv_sem=recv_sems.at[n_sem - 1],
                    device_id=(mx, my, 1 - mc), device_id_type=pl.DeviceIdType.MESH)

            @pl.when(mc == 0)
            def _():
                pick(x_refs[n]).start()

            @pl.when(mc == 1)
            def _():
                pick(x_refs[n + 1]).start()

            copies.append(pick(x_refs[n]))
        for cp in copies:
            cp.wait()

    shapes = ([(4,) + tuple(x.shape[1:]) for x in scatter] + [(4,) + tuple(x.shape) for x in gather]
              + [((2,) if sibling_both else ()) + tuple(x.shape) for x in sibling])
    xs = chip_xs + sibling
    out_shape = [jax.ShapeDtypeStruct(s, x.dtype) for s, x in zip(shapes, xs)]
    if sibling_by_core:
        out_shape.append(jax.ShapeDtypeStruct(sibling_by_core[0].shape, sibling_by_core[0].dtype))
    outs = pl.pallas_call(
        body, name=name, in_specs=[_ANY] * n_in, out_specs=[_ANY] * n_out, out_shape=out_shape,
        scratch_shapes=[pltpu.SemaphoreType.DMA((n_sem,)), pltpu.SemaphoreType.DMA((n_sem,))],
    )(*xs, *(sibling_by_core or ()))
    me, c = 2 * lax.axis_index("x") + lax.axis_index("y"), lax.axis_index("c")
    fixed = []
    for a, (o, x) in enumerate(zip(outs, xs)):
        if a < ns:
            o = lax.dynamic_update_index_in_dim(o, lax.dynamic_index_in_dim(x, me, 0, keepdims=True), me, 0)
        elif a < nc:
            o = lax.dynamic_update_index_in_dim(o, x[None], me, 0)
        elif sibling_both:
            o = lax.dynamic_update_index_in_dim(o, x[None], c, 0)
        fixed.append(o)
    if sibling_by_core:
        fixed.append(outs[n])
    return fixed[:ns], fixed[ns:nc], fixed[nc:]


def _rows_tile(rows, row_bytes, budget=5 << 19):
    return next(t for t in (512, 256, 128, 64, 32, 16, 8) if rows % t == 0 and t * row_bytes <= budget)


def _padded_row_bytes(cols):
    return -(-cols // LANES) * LANES * 4


def _add2(a, b, name, out_dtype=f32):
    by_core = isinstance(a, (tuple, list))
    parts = list(a) if by_core else [a]
    R, C = b.shape
    tr = _rows_tile(R, _padded_row_bytes(C))

    def body(*refs):
        b_ref, o_ref = refs[-2], refs[-1]
        mine = jnp.where(lax.axis_index("c") == 0, refs[0][...], refs[1][...]) if by_core else refs[0][...]
        o_ref[...] = (mine + b_ref[...]).astype(out_dtype)

    spec = pl.BlockSpec((tr, C), lambda i: (i, 0))
    return _call(body, name, (R // tr,), [spec] * (len(parts) + 1), spec, jax.ShapeDtypeStruct((R, C), out_dtype),
                 sem=("parallel",))(*parts, b)


def _sum4(x, name):
    R = x.shape[1]
    tr = _tile(R, (2560, 1024, 512, 256, 128))

    def body(x_ref, o_ref):
        p = [x_ref[j].astype(f32) for j in range(4)]
        o_ref[...] = ((p[0] + p[1]) + p[2]) + p[3]

    return _call(body, name, (R // tr,), [pl.BlockSpec((4, tr, LANES), lambda i: (0, i, 0))],
                 pl.BlockSpec((tr, LANES), lambda i: (i, 0)), jax.ShapeDtypeStruct((R, LANES), f32),
                 sem=("parallel",))(x)


def _adamw(g_parts, w, m, v, name):
    stacked = not isinstance(g_parts, (tuple, list))
    k = g_parts.shape[0] if stacked else len(g_parts)
    R, C = w.shape
    tr = _rows_tile(R, _padded_row_bytes(C))

    def body(*refs):
        w_ref, m_ref, v_ref, g_ref, d_ref, nm_ref, nv_ref = refs[-7:]
        if stacked:
            g = refs[0][0].astype(f32)
            for j in range(1, k):
                g = g + refs[0][j].astype(f32)
        else:
            g = refs[0][...]
            for r in refs[1:k]:
                g = g + r[...]
        g_ref[...] = g
        d_ref[...], nm_ref[...], nv_ref[...] = _adamw_update(g, w_ref[...], m_ref[...], v_ref[...])

    spec = pl.BlockSpec((tr, C), lambda i: (i, 0))
    sd = jax.ShapeDtypeStruct((R, C), f32)
    g_specs = [pl.BlockSpec((k, tr, C), lambda i: (0, i, 0))] if stacked else [spec] * k
    g_args = [g_parts] if stacked else list(g_parts)
    return _call(body, name, (R // tr,), g_specs + [spec] * 3, [spec] * 4, [sd] * 4,
                 sem=("parallel",))(*g_args, w, m, v)


def _adamw_update(g, w, m, v):
    m = ADAM_B1 * m + (1.0 - ADAM_B1) * g
    v = ADAM_B2 * v + (1.0 - ADAM_B2) * (g * g)
    c1 = 1.0 - ADAM_B1 ** ADAM_STEP
    c2 = 1.0 - ADAM_B2 ** ADAM_STEP
    return -ADAM_LR * ((m / c1) / (jnp.sqrt(v / c2) + ADAM_EPS) + ADAM_WD * w), m, v


def _adamw_small(gs, ws, ms, vs, name):
    n = len(gs)

    def body(*refs):
        ins, outs = refs[:4 * n], refs[4 * n:]
        for t in range(n):
            d, m, v = _adamw_update(ins[t][...], ins[n + t][...], ins[2 * n + t][...], ins[3 * n + t][...])
            outs[t][...] = d
            outs[n + t][...] = m
            outs[2 * n + t][...] = v

    vmem = pl.BlockSpec(memory_space=pltpu.VMEM)
    outs = pl.pallas_call(
        body, name=name, in_specs=[vmem] * (4 * n), out_specs=[vmem] * (3 * n),
        out_shape=[jax.ShapeDtypeStruct(w.shape, f32) for w in ws] * 3,
        compiler_params=pltpu.CompilerParams(vmem_limit_bytes=V7X_VMEM_LIMIT))(*gs, *ws, *ms, *vs)
    return outs[:n], outs[n:2 * n], outs[2 * n:]


def _pack(arrays, row_multiple=PACK_ROWS):
    flat = jnp.concatenate([a.reshape(-1) for a in arrays])
    unit = row_multiple * LANES
    n = -(-flat.shape[0] // unit) * unit
    return jnp.pad(flat, (0, n - flat.shape[0])).reshape(n // LANES, LANES)


def _unpack(buf, shapes, lead=()):
    flat = buf.reshape(lead + (-1,))
    out, off = [], 0
    for s in shapes:
        n = 1
        for dim in s:
            n *= dim
        out.append(flat[..., off:off + n].reshape(lead + tuple(s)))
        off += n
    return out


def _to_shards(full, axis):
    s = full.shape
    t = full.reshape(s[:axis] + (4, s[axis] // 4) + s[axis + 1:])
    return jnp.moveaxis(t, axis, 0)


def _from_shards(sh, axis):
    t = jnp.moveaxis(sh, 0, axis)
    s = t.shape
    return t.reshape(s[:axis] + (s[axis] * s[axis + 1],) + s[axis + 2:])


def kernel(x, norm_w, w_in, s5_a_re, s5_a_im, s5_log_step, s5_b_re, s5_b_im, s5_c_re, s5_c_im, s5_d, s5_glu_w, s5_glu_b, q_norm_w, k_norm_w, conv_w, conv_b, dt_bias, ssd_a_log, ssd_d, ssd_norm_w, proj_a, proj_b, proj_c, w_out, loss_target, m_norm_w, m_w_in, m_s5_a_re, m_s5_a_im, m_s5_log_step, m_s5_b_re, m_s5_b_im, m_s5_c_re, m_s5_c_im, m_s5_d, m_s5_glu_w, m_s5_glu_b, m_q_norm_w, m_k_norm_w, m_conv_w, m_conv_b, m_dt_bias, m_ssd_a_log, m_ssd_d, m_ssd_norm_w, m_proj_a, m_proj_b, m_proj_c, m_w_out, v_norm_w, v_w_in, v_s5_a_re, v_s5_a_im, v_s5_log_step, v_s5_b_re, v_s5_b_im, v_s5_c_re, v_s5_c_im, v_s5_d, v_s5_glu_w, v_s5_glu_b, v_q_norm_w, v_k_norm_w, v_conv_w, v_conv_b, v_dt_bias, v_ssd_a_log, v_ssd_d, v_ssd_norm_w, v_proj_a, v_proj_b, v_proj_c, v_w_out):
    given = dict(locals())
    W = {n: given[n] for n in _WEIGHTS}
    M = {n: given["m_" + n] for n in _WEIGHTS}
    V = {n: given["v_" + n] for n in _WEIGHTS}
    n_layers = norm_w.shape[0]
    assert n_layers == 2
    c = lax.axis_index("c")

    mine_of = lambda t: lax.dynamic_index_in_dim(t, c, 0, keepdims=False)
    as_payload = lambda n: lax.bitcast_convert_type(W[n], bf16) if n == "conv_w" else W[n].astype(bf16)
    payload_shapes = [W[n].shape + ((2,) if n == "conv_w" else ()) for n, _ in _SHARDED]
    wpack = _pack([as_payload(n) for n, _ in _SHARDED])
    half_rows = wpack.shape[0] // 2
    _, (pack_half, w_in_mine_layer), _ = _exchange(
        "gather_weights", gather=[lax.dynamic_slice_in_dim(wpack, c * half_rows, half_rows),
                                  mine_of(w_in).astype(bf16)])
    _, _, (w_in_layers, pack_halves) = _exchange("share_weights", sibling=[w_in_mine_layer, pack_half],
                                                 sibling_both=True)
    gathered = jnp.moveaxis(pack_halves, 0, 1).reshape(4, 2 * half_rows, LANES)
    full = dict(W)
    pieces = _unpack(gathered.reshape(4, -1), payload_shapes, lead=(4,))
    for (n, axis), sh in zip(_SHARDED, pieces):
        full[n] = _from_shards(lax.bitcast_convert_type(sh, f32) if n == "conv_w" else sh, axis)

    qs, saves = [], []
    for l in range(n_layers):
        p = {n: full[n][l] for n in _WEIGHTS if n != "w_in"}
        p["w_in"] = [w_in_layers[l, k] for k in range(4)]
        qs.append((_prep_layer(p), p))
    (act, h), sv = layer_fwd(x[0], qs[0][0], "_l0", next_norm_w=qs[1][0]["norm_w"])
    saves.append(sv)
    (dact, lsum), sv = layer_fwd(act, qs[1][0], "_l1", h=h, target=loss_target[0])
    saves.append(sv)
    loss = lax.psum(lsum[0, 0], ("x", "y", "c"))
    layer_grads = [None] * n_layers
    for l in reversed(range(n_layers)):
        q, p = qs[l]
        dact, layer_grads[l] = layer_bwd(dact, saves[l], q, p, f"_l{l}")
    grad_x = dact[None]
    G = {n: jnp.stack([layer_grads[l][n] for l in range(n_layers)]) for n in _WEIGHTS if n != "w_in"}

    repl_shapes = [W[n].shape for n in _REPL]
    small = _pack([G[n] for n in _REPL], 4 * PACK_ROWS)
    quarter = small.shape[0] // 4
    big = [_to_shards(G[n], axis).reshape(4, -1) for n, axis in _SHARDED]
    big = jnp.concatenate(big, axis=1)
    unit = PACK_ROWS * LANES
    nbig = -(-big.shape[1] // unit) * unit
    big = jnp.pad(big, ((0, 0), (0, nbig - big.shape[1]))).reshape(4, nbig // LANES, LANES)
    gpack = jnp.concatenate([big, small.reshape(4, quarter, LANES)], axis=1)
    rbig = nbig // LANES
    g0, g1 = layer_grads[0]["w_in"], layer_grads[1]["w_in"]

    (landed_pack,), _, (from_sibling,) = _exchange(
        "swap_w_in_grads_and_scatter_grads", scatter=[gpack.astype(bf16)], sibling_by_core=(g1, g0))
    flat = lambda t: t.reshape(4 * D_MODEL, W_IN_SHARD)
    shards = _add2((flat(g0), flat(g1)), flat(from_sibling), "sum_cores_w_in", out_dtype=bf16)
    mine = _sum4(landed_pack, "sum_chips")

    (landed,), _, (other,) = _exchange(
        "scatter_w_in_grads_and_swap_cores", scatter=[shards.reshape(4, D_MODEL, W_IN_SHARD)], sibling=[mine])
    w_in_mine = _adamw(landed, mine_of(w_in), mine_of(m_w_in), mine_of(v_w_in), "adamw_w_in")
    gq = _add2(mine[rbig:], other[rbig:], "sum_cores_small")

    _, (gsmall,), w_in_out = _exchange(
        "share_w_in_updates_and_gather_small", gather=[gq], sibling=w_in_mine, sibling_both=True)
    gsmall = gsmall.reshape(4 * quarter, LANES)

    shard_shapes = [W[n].shape for n, _ in _SHARDED]
    g_mine, g_other = _unpack(mine[:rbig], shard_shapes), _unpack(other[:rbig], shard_shapes)
    rows_of = lambda t: t.reshape(-1, t.shape[-1])
    res = [dict(), dict(), dict(), dict()]
    for k, (n, _) in enumerate(_SHARDED):
        outs = _adamw((rows_of(g_mine[k]), rows_of(g_other[k])), rows_of(W[n]), rows_of(M[n]), rows_of(V[n]),
                      f"adamw_{n}")
        for kind in range(4):
            res[kind][n] = outs[kind].reshape(W[n].shape)
    g_small = _unpack(gsmall, repl_shapes)
    small_out = _adamw_small([rows_of(g) for g in g_small], *([rows_of(T[n]) for n in _REPL] for T in (W, M, V)),
                             "adamw_replicated")
    for kind in range(4):
        res[kind]["w_in"] = w_in_out[kind]
        for k, n in enumerate(_REPL):
            res[kind][n] = g_small[k] if kind == 0 else small_out[kind - 1][k].reshape(W[n].shape)
    return (loss, grad_x, *[res[0][n] for n in _WEIGHTS], *[res[1][n] for n in _WEIGHTS],
            *[res[2][n] for n in _WEIGHTS], *[res[3][n] for n in _WEIGHTS])
```

```python
import functools

import jax
import jax.numpy as jnp
from jax import lax
from jax.experimental import pallas as pl
from jax.experimental.pallas import tpu as pltpu

f32 = jnp.float32
bf16 = jnp.bfloat16

D_MODEL = 1024
RMS_EPS = 1e-6
V7X_VMEM_LIMIT = 60 * 1024 * 1024
LANES = 128
NN, NT, TN = ((1,), (0,)), ((1,), (1,)), ((0,), (0,))

S5_STATES = 2048
S5_ROWS = 512
ATT_SEG = 2048
ATT_BLOCK = 128
SSD_CHUNK = 128
SSD_CHUNKS_PER_STEP = 2
SSD_WIDTH = 768
SSD_XBC = 1280
CONV_ROWS = 512
TAIL_ROWS = 256

ADAM_LR, ADAM_B1, ADAM_B2, ADAM_EPS, ADAM_WD, ADAM_STEP = 0.001, 0.9, 0.999, 1e-08, 0.01, 10

_C_UA, _C_ZA, _C_Q, _C_K, _C_V, _C_ZB, _C_XBC, _C_DT, _C_ZC, _C_GATE, _C_END = (
    0, 512, 1024, 1792, 2560, 3328, 3584, 4864, 4876, 5644, 8716)

_SHARDED = (("s5_glu_w", 1), ("conv_w", 2), ("proj_a", 2), ("proj_b", 2), ("proj_c", 2), ("w_out", 1))
W_IN_SHARD = 2179
_REPL = ("norm_w", "s5_a_re", "s5_a_im", "s5_log_step", "s5_b_re", "s5_b_im", "s5_c_re", "s5_c_im", "s5_d",
         "s5_glu_b", "q_norm_w", "k_norm_w", "conv_b", "dt_bias", "ssd_a_log", "ssd_d", "ssd_norm_w")
_WEIGHTS = ("norm_w", "w_in", "s5_a_re", "s5_a_im", "s5_log_step", "s5_b_re", "s5_b_im", "s5_c_re", "s5_c_im",
            "s5_d", "s5_glu_w", "s5_glu_b", "q_norm_w", "k_norm_w", "conv_w", "conv_b", "dt_bias", "ssd_a_log",
            "ssd_d", "ssd_norm_w", "proj_a", "proj_b", "proj_c", "w_out")
PACK_ROWS = 512


def _dot(a, b, dims):
    return lax.dot_general(a.astype(bf16), b.astype(bf16), (dims, ((), ())), preferred_element_type=f32)


_ANY = pl.BlockSpec(memory_space=pl.ANY)

MAIN_WIDTH = 8448
MAIN_SSD_BLOCK = 3
MAIN_DT_BLOCK = 46
MAIN_ATT_BLOCK = 16


def _call(body, name, grid, in_specs, out_specs, out_shape, scratch=(), sem=None, aliases=None):
    return pl.pallas_call(
        body, name=name, grid=grid, in_specs=in_specs, out_specs=out_specs, out_shape=out_shape,
        scratch_shapes=list(scratch), input_output_aliases=aliases or {},
        compiler_params=pltpu.CompilerParams(dimension_semantics=sem, vmem_limit_bytes=V7X_VMEM_LIMIT))


def _tile(n, options=(1024, 768, 512, 384, 256, 128)):
    return next(t for t in options if n % t == 0)


@functools.partial(jax.custom_vjp, nondiff_argnums=(2,))
def _bdot(a, b, dims):
    return _dot(a, b, dims)


def _bdot_fwd(a, b, dims):
    return _dot(a, b, dims), (a, b)


def _bdot_bwd(dims, res, g):
    a, b = res
    if dims == NN:
        da, db = _dot(g, b, NT), _dot(a, g, TN)
    elif dims == NT:
        da, db = _dot(g, b, NN), _dot(g, a, TN)
    else:
        da, db = _dot(b, g, NT), _dot(a, g, NN)
    return da.astype(a.dtype), db.astype(b.dtype)


_bdot.defvjp(_bdot_fwd, _bdot_bwd)


@functools.partial(jax.custom_vjp, nondiff_argnums=(2,))
def _cdot(a, w, dims):
    return _dot(a, w, dims)


def _cdot_fwd(a, w, dims):
    return _dot(a, w, dims), w


def _cdot_bwd(dims, w, g):
    da = _dot(g, w, NT) if dims == NN else _dot(g, w, NN)
    return da, jnp.zeros_like(w)


_cdot.defvjp(_cdot_fwd, _cdot_bwd)


def _split3(x):
    hi = x.astype(bf16)
    r = x - hi.astype(f32)
    mid = r.astype(bf16)
    lo = (r - mid.astype(f32)).astype(bf16)
    return hi, mid, lo


@jax.custom_vjp
def _xdot_l(m, x):
    return sum(_dot(m, p, NN) for p in _split3(x))


def _xdot_l_fwd(m, x):
    return _xdot_l(m, x), m


def _xdot_l_bwd(m, g):
    return jnp.zeros_like(m), sum(_dot(m, p, TN) for p in _split3(g))


_xdot_l.defvjp(_xdot_l_fwd, _xdot_l_bwd)


@jax.custom_vjp
def _softplus(x):
    e = jnp.exp(-jnp.abs(x))
    u = 1.0 + e
    log1p = jnp.where(u == 1.0, e, jnp.log(u) * (e / jnp.where(u == 1.0, 1.0, u - 1.0)))
    return jnp.maximum(x, 0.0) + log1p


def _softplus_fwd(x):
    return _softplus(x), x


def _softplus_bwd(x, g):
    return (g * jax.nn.sigmoid(x),)


_softplus.defvjp(_softplus_fwd, _softplus_bwd)


def _rms(x, w):
    return x * lax.rsqrt(jnp.mean(x * x, axis=-1, keepdims=True) + RMS_EPS) * w


def mm_nn(a, b, name, tm=2048):
    M, K = a.shape
    N = b.shape[1]
    tn = _tile(N)

    def body(a_ref, b_ref, o_ref):
        o_ref[...] = _dot(a_ref[...], b_ref[...], NN)

    return _call(body, name, (M // tm, N // tn),
                 [pl.BlockSpec((tm, K), lambda i, j: (i, 0)), pl.BlockSpec((K, tn), lambda i, j: (0, j))],
                 pl.BlockSpec((tm, tn), lambda i, j: (i, j)), jax.ShapeDtypeStruct((M, N), f32),
                 sem=("parallel", "parallel"))(a, b)


def mm_nt(a, b, name, tm=1024):
    M, K = a.shape
    N = b.shape[0]
    tk = _tile(K, (2816, 1024, 768, 512, 256, 128))

    def body(a_ref, b_ref, o_ref):
        k = pl.program_id(1)
        p = _dot(a_ref[...], b_ref[...], NT)

        @pl.when(k == 0)
        def _():
            o_ref[...] = p

        @pl.when(k > 0)
        def _():
            o_ref[...] += p

    return _call(body, name, (M // tm, K // tk),
                 [pl.BlockSpec((tm, tk), lambda i, k: (i, k)), pl.BlockSpec((N, tk), lambda i, k: (0, k))],
                 pl.BlockSpec((tm, N), lambda i, k: (i, 0)), jax.ShapeDtypeStruct((M, N), f32),
                 sem=("parallel", "arbitrary"))(a, b)


def mm_tn(a, b, name, tk=2048):
    K, M = a.shape
    N = b.shape[1]
    tn = _tile(N)

    def body(a_ref, b_ref, o_ref):
        k = pl.program_id(1)
        p = _dot(a_ref[...], b_ref[...], TN)

        @pl.when(k == 0)
        def _():
            o_ref[...] = p

        @pl.when(k > 0)
        def _():
            o_ref[...] += p

    return _call(body, name, (N // tn, K // tk),
                 [pl.BlockSpec((tk, M), lambda j, k: (k, 0)), pl.BlockSpec((tk, tn), lambda j, k: (k, j))],
                 pl.BlockSpec((M, tn), lambda j, k: (0, j)), jax.ShapeDtypeStruct((M, N), f32),
                 sem=("parallel", "arbitrary"))(a, b)


def rms_fwd(x, w, name, tm=512):
    S = x.shape[0]

    def body(x_ref, w_ref, o_ref):
        o_ref[...] = _rms(x_ref[...], w_ref[...]).astype(bf16)

    return _call(body, name, (S // tm,),
                 [pl.BlockSpec((tm, D_MODEL), lambda i: (i, 0)), pl.BlockSpec((1, D_MODEL), lambda i: (0, 0))],
                 pl.BlockSpec((tm, D_MODEL), lambda i: (i, 0)), jax.ShapeDtypeStruct((S, D_MODEL), bf16),
                 sem=("parallel",))(x, w)


def mm_nt_rms_bwd(a, b, acc, x, w, dres, name, tm=1024):
    S, K = a.shape

    def body(a_ref, b_ref, acc_ref, x_ref, w_ref, dr_ref, dx_ref, dw_ref):
        dh = _dot(a_ref[...], b_ref[...], NT) + acc_ref[...]
        _, vjp = jax.vjp(_rms, x_ref[...], w_ref[...])
        dx, dw = vjp(dh)
        dx_ref[...] = dx + dr_ref[...]

        @pl.when(pl.program_id(0) == 0)
        def _():
            dw_ref[...] = dw

        @pl.when(pl.program_id(0) > 0)
        def _():
            dw_ref[...] += dw

    row = pl.BlockSpec((tm, D_MODEL), lambda i: (i, 0))
    vec = pl.BlockSpec((1, D_MODEL), lambda i: (0, 0))
    return _call(body, name, (S // tm,),
                 [pl.BlockSpec((tm, K), lambda i: (i, 0)), pl.BlockSpec((D_MODEL, K), lambda i: (0, 0)), row, row, vec,
                  row], [row, vec],
                 [jax.ShapeDtypeStruct((S, D_MODEL), f32), jax.ShapeDtypeStruct((1, D_MODEL), f32)],
                 sem=("arbitrary",))(a, b, acc, x, w, dres)


def _s5_discretize(a_re, a_im, log_step, b_re, b_im, c_re, c_im):
    step = jnp.exp(log_step)[:, None]
    mag = jnp.exp(a_re * step)
    ang = a_im * step
    lam_re, lam_im = mag * jnp.cos(ang), mag * jnp.sin(ang)
    num_re, num_im = lam_re - 1.0, lam_im
    den = a_re * a_re + a_im * a_im
    f_re = (num_re * a_re + num_im * a_im) / den
    f_im = (num_im * a_re - num_re * a_im) / den
    bb_re = f_re[..., None] * b_re - f_im[..., None] * b_im
    bb_im = f_re[..., None] * b_im + f_im[..., None] * b_re
    eye = jnp.eye(8, dtype=f32)

    def block_in(bb):
        t = bb.transpose(0, 2, 1).reshape(4, 8, 16, 1, 64)
        return (t * eye[None, :, None, :, None]).reshape(4, 128, 512)

    def block_out(c):
        t = c.transpose(0, 2, 1).reshape(4, 8, 64, 1, 16)
        return (t * eye[None, :, None, :, None]).reshape(4, 512, 128)

    return (lam_re.reshape(1, S5_STATES), lam_im.reshape(1, S5_STATES), block_in(bb_re), block_in(bb_im),
            block_out(c_re), block_out(c_im))


def _lam_powers(lam_re, lam_im):
    rows_re, rows_im = [lam_re], [lam_im]
    for _ in range(7):
        pr, pi = rows_re[-1], rows_im[-1]
        rows_re.append(pr * lam_re - pi * lam_im)
        rows_im.append(pr * lam_im + pi * lam_re)
    return jnp.concatenate(rows_re, 0), jnp.concatenate(rows_im, 0)


def s5_fwd(u, pw_re, pw_im, w_re, w_im, c_re, c_im, dvec, name):
    S = u.shape[0]
    R, NS = S5_ROWS, S5_STATES
    nb = R // 8

    def body(u_ref, pwr_ref, pwi_ref, wre_ref, wim_ref, cre_ref, cim_ref, d_ref, y_ref, hr_ref, hi_ref,
             car_re, car_im, cin_re, cin_im, up, yp):
        @pl.when(pl.program_id(0) == 0)
        def _():
            car_re[...] = jnp.zeros_like(car_re)
            car_im[...] = jnp.zeros_like(car_im)

        slab = lambda r: pl.ds(r * nb, nb)
        for r in range(8):
            up[slab(r), :] = u_ref[:, r, :]
        u = up[...]
        for j in range(4):
            uj = u[:, 128 * j:128 * (j + 1)]
            hr_ref[:, 512 * j:512 * (j + 1)] = _dot(uj, wre_ref[j], NN)
            hi_ref[:, 512 * j:512 * (j + 1)] = _dot(uj, wim_ref[j], NN)
        lr, li = pwr_ref[0:1, :], pwi_ref[0:1, :]
        for r in range(1, 8):
            pr, pi = hr_ref[slab(r - 1), :], hi_ref[slab(r - 1), :]
            hr_ref[slab(r), :] = lr * pr - li * pi + hr_ref[slab(r), :]
            hi_ref[slab(r), :] = lr * pi + li * pr + hi_ref[slab(r), :]
        l8r, l8i = pwr_ref[7:8, :], pwi_ref[7:8, :]

        def across(c, carry):
            gr, gi = carry
            cin_re[pl.ds(c, 1), :] = gr
            cin_im[pl.ds(c, 1), :] = gi
            er, ei = hr_ref[pl.ds(7 * nb + c, 1), :], hi_ref[pl.ds(7 * nb + c, 1), :]
            return l8r * gr - l8i * gi + er, l8r * gi + l8i * gr + ei

        gr, gi = lax.fori_loop(0, nb, across, (car_re[...], car_im[...]))
        car_re[...] = gr
        car_im[...] = gi
        cr, ci = cin_re[...], cin_im[...]
        for r in range(8):
            pr, pi = pwr_ref[r:r + 1, :], pwi_ref[r:r + 1, :]
            hr_ref[slab(r), :] = hr_ref[slab(r), :] + pr * cr - pi * ci
            hi_ref[slab(r), :] = hi_ref[slab(r), :] + pr * ci + pi * cr
        for j in range(4):
            sl = slice(512 * j, 512 * (j + 1))
            cs = slice(128 * j, 128 * (j + 1))
            yp[:, cs] = (_dot(hr_ref[:, sl], cre_ref[j], NN) - _dot(hi_ref[:, sl], cim_ref[j], NN)
                         + d_ref[:, cs] * u[:, cs])
        for r in range(8):
            y_ref[:, r, :] = yp[slab(r), :]

    full = lambda shape: pl.BlockSpec(shape, lambda i: (0,) * len(shape))
    hspec = pl.BlockSpec((R, NS), lambda i: (i, 0))
    uspec = pl.BlockSpec((nb, 8, 512), lambda i: (i, 0, 0))
    y, h_re, h_im = _call(
        body, name, (S // R,),
        [uspec, full((8, NS)), full((8, NS)), full((4, 128, 512)),
         full((4, 128, 512)), full((4, 512, 128)), full((4, 512, 128)), full((1, 512))],
        [uspec, hspec, hspec],
        [jax.ShapeDtypeStruct((S // 8, 8, 512), f32), jax.ShapeDtypeStruct((S, NS), f32),
         jax.ShapeDtypeStruct((S, NS), f32)],
        scratch=[pltpu.VMEM((1, NS), f32), pltpu.VMEM((1, NS), f32), pltpu.VMEM((nb, NS), f32),
                 pltpu.VMEM((nb, NS), f32), pltpu.VMEM((R, 512), f32), pltpu.VMEM((R, 512), f32)],
        sem=("arbitrary",))(u.reshape(S // 8, 8, 512), pw_re, pw_im, w_re.astype(bf16), w_im.astype(bf16),
                            c_re.astype(bf16), c_im.astype(bf16), dvec)
    return y.reshape(S, 512), h_re, h_im


def s5_bwd(dy, u, h_re, h_im, pw_re, pw_im, w_re, w_im, c_re, c_im, dvec, name):
    S = u.shape[0]
    R, NS = S5_ROWS, S5_STATES
    nb = R // 8
    nchunk = S // R

    def body(dy_ref, u_ref, hr_ref, hi_ref, hpr_ref, hpi_ref, pwr_ref, pwi_ref, wre_ref, wim_ref, cre_ref, cim_ref,
             d_ref, du_ref, dwre_ref, dwim_ref, dcre_ref, dcim_ref, dlr_ref, dli_ref, dd_ref,
             ar, ai, car_re, car_im, cin_re, cin_im, up, dyp, dup):
        i = pl.program_id(0)

        @pl.when(i == 0)
        def _():
            for ref in (car_re, car_im, dwre_ref, dwim_ref, dcre_ref, dcim_ref, dlr_ref, dli_ref, dd_ref):
                ref[...] = jnp.zeros_like(ref)

        slab = lambda r: pl.ds(r * nb, nb)
        for r in range(8):
            up[slab(r), :] = u_ref[:, r, :]
            dyp[slab(r), :] = dy_ref[:, r, :]
        dy = dyp[...]
        u = up[...]
        for j in range(4):
            dyj = dy[:, 128 * j:128 * (j + 1)]
            ar[:, 512 * j:512 * (j + 1)] = _dot(dyj, cre_ref[j], NT)
            ai[:, 512 * j:512 * (j + 1)] = -_dot(dyj, cim_ref[j], NT)
        lr, li = pwr_ref[0:1, :], pwi_ref[0:1, :]
        for r in range(6, -1, -1):
            nr, ni = ar[slab(r + 1), :], ai[slab(r + 1), :]
            ar[slab(r), :] = lr * nr + li * ni + ar[slab(r), :]
            ai[slab(r), :] = lr * ni - li * nr + ai[slab(r), :]
        l8r, l8i = pwr_ref[7:8, :], pwi_ref[7:8, :]

        def across(k, carry):
            c = nb - 1 - k
            gr, gi = carry
            cin_re[pl.ds(c, 1), :] = gr
            cin_im[pl.ds(c, 1), :] = gi
            er, ei = ar[pl.ds(c, 1), :], ai[pl.ds(c, 1), :]
            return l8r * gr + l8i * gi + er, l8r * gi - l8i * gr + ei

        gr, gi = lax.fori_loop(0, nb, across, (car_re[...], car_im[...]))
        car_re[...] = gr
        car_im[...] = gi
        cr, ci = cin_re[...], cin_im[...]
        for r in range(8):
            pr, pi = pwr_ref[7 - r:8 - r, :], pwi_ref[7 - r:8 - r, :]
            ar[slab(r), :] = ar[slab(r), :] + pr * cr + pi * ci
            ai[slab(r), :] = ai[slab(r), :] + pr * ci - pi * cr

        acc_r = jnp.zeros((1, NS), f32)
        acc_i = jnp.zeros((1, NS), f32)
        has_prev = (i < nchunk - 1).astype(f32)
        top = lax.broadcasted_iota(jnp.int32, (nb, NS), 0) == 0
        for r in range(8):
            if r == 0:
                xr = jnp.where(top, hpr_ref[7:8, :] * has_prev, pltpu.roll(hr_ref[slab(7), :], 1, 0))
                xi = jnp.where(top, hpi_ref[7:8, :] * has_prev, pltpu.roll(hi_ref[slab(7), :], 1, 0))
            else:
                xr, xi = hr_ref[slab(r - 1), :], hi_ref[slab(r - 1), :]
            br, bi = ar[slab(r), :], ai[slab(r), :]
            acc_r += jnp.sum(br * xr + bi * xi, axis=0, keepdims=True)
            acc_i += jnp.sum(bi * xr - br * xi, axis=0, keepdims=True)
        dlr_ref[...] += acc_r
        dli_ref[...] += acc_i
        dd_ref[...] += jnp.sum(dy * u, axis=0, keepdims=True)

        for j in range(4):
            sl = slice(512 * j, 512 * (j + 1))
            cs = slice(128 * j, 128 * (j + 1))
            arj, aij = ar[:, sl], ai[:, sl]
            uj, dyj = u[:, cs], dy[:, cs]
            dup[:, cs] = _dot(arj, wre_ref[j], NT) + _dot(aij, wim_ref[j], NT) + d_ref[:, cs] * dyj
            dwre_ref[j] += _dot(uj, arj, TN)
            dwim_ref[j] += _dot(uj, aij, TN)
            dcre_ref[j] += _dot(hr_ref[:, sl], dyj, TN)
            dcim_ref[j] -= _dot(hi_ref[:, sl], dyj, TN)
        for r in range(8):
            du_ref[:, r, :] = dup[slab(r), :]

    rev = lambda i: nchunk - 1 - i
    full = lambda shape: pl.BlockSpec(shape, lambda i: (0,) * len(shape))
    row = pl.BlockSpec((nb, 8, 512), lambda i: (rev(i), 0, 0))
    hspec = pl.BlockSpec((R, NS), lambda i: (rev(i), 0))
    hprev = pl.BlockSpec((8, NS), lambda i: (jnp.maximum(rev(i) * nb - 1, 0), 0))
    outs = _call(
        body, name, (nchunk,),
        [row, row, hspec, hspec, hprev, hprev, full((8, NS)), full((8, NS)), full((4, 128, 512)), full((4, 128, 512)),
         full((4, 512, 128)), full((4, 512, 128)), full((1, 512))],
        [row, full((4, 128, 512)), full((4, 128, 512)), full((4, 512, 128)), full((4, 512, 128)),
         full((1, NS)), full((1, NS)), full((1, 512))],
        [jax.ShapeDtypeStruct((S // 8, 8, 512), f32), jax.ShapeDtypeStruct((4, 128, 512), f32),
         jax.ShapeDtypeStruct((4, 128, 512), f32), jax.ShapeDtypeStruct((4, 512, 128), f32),
         jax.ShapeDtypeStruct((4, 512, 128), f32), jax.ShapeDtypeStruct((1, NS), f32),
         jax.ShapeDtypeStruct((1, NS), f32), jax.ShapeDtypeStruct((1, 512), f32)],
        scratch=[pltpu.VMEM((R, NS), f32), pltpu.VMEM((R, NS), f32), pltpu.VMEM((1, NS), f32),
                 pltpu.VMEM((1, NS), f32), pltpu.VMEM((nb, NS), f32), pltpu.VMEM((nb, NS), f32),
                 pltpu.VMEM((R, 512), f32), pltpu.VMEM((R, 512), f32), pltpu.VMEM((R, 512), f32)],
        sem=("arbitrary",))(dy.reshape(S // 8, 8, 512), u.reshape(S // 8, 8, 512), h_re, h_im, h_re, h_im, pw_re,
                            pw_im, w_re.astype(bf16), w_im.astype(bf16), c_re.astype(bf16), c_im.astype(bf16), dvec)
    return (outs[0].reshape(S, 512),) + tuple(outs[1:])


def _rows(start, n, d):
    return pl.ds(pl.multiple_of(start, ATT_BLOCK), n) if d == 1 else pl.ds(start, n, stride=d)


def _head_masks():
    lane = lax.broadcasted_iota(jnp.int32, (1, LANES), 1)
    return [(lane < 64).astype(f32), (lane >= 64).astype(f32)]


def _head_norm(x, w, hm):
    x2 = x * x
    r = [lax.rsqrt(jnp.sum(x2 * hm[h], axis=-1, keepdims=True) * (1.0 / 64) + RMS_EPS) for h in range(2)]
    sc = hm[0] * r[0] + hm[1] * r[1]
    return x * sc * w, sc, r


def _head_norm_bwd(x, w, sc, r, dxn, hm):
    dw = jnp.sum(dxn * x * sc, axis=0, keepdims=True)
    t = dxn * w
    tx = t * x
    corr = sum(hm[h] * (r[h] * r[h] * r[h]) * jnp.sum(tx * hm[h], axis=-1, keepdims=True) for h in range(2))
    return t * sc - x * corr * (1.0 / 64), dw


def _att_mask(has_prev):
    qi = lax.broadcasted_iota(jnp.int32, (ATT_BLOCK, 2 * ATT_BLOCK), 0) + ATT_BLOCK
    kj = lax.broadcasted_iota(jnp.int32, (ATT_BLOCK, 2 * ATT_BLOCK), 1)
    return (qi - kj >= 0) & (qi - kj <= ATT_BLOCK) & (has_prev | (kj >= ATT_BLOCK))


def _att_block_bwd(q, k, v, o, lse, do, dlse, qw, kw, has_prev):
    hm = _head_masks()
    mask = _att_mask(has_prev)
    qn, qsc, qr = _head_norm(q, qw, hm)
    kn, ksc, kr = _head_norm(k, kw, hm)
    dqn = jnp.zeros((ATT_BLOCK, LANES), f32)
    dkn = jnp.zeros((2 * ATT_BLOCK, LANES), f32)
    dv = jnp.zeros((2 * ATT_BLOCK, LANES), f32)
    for h in range(2):
        qh, do_h = qn * hm[h], do * hm[h]
        s = _dot(qh, kn, NT) * 0.125
        p = jnp.exp(jnp.where(mask, s - lse[:, 64 * h:64 * h + 1], -jnp.inf))
        dp = _dot(do_h, v, NT)
        delta = jnp.sum(do_h * o, axis=-1, keepdims=True)
        dl = jnp.sum(dlse * hm[h], axis=-1, keepdims=True)
        ds = p * (dp - delta + dl) * 0.125
        dqn = dqn + hm[h] * _dot(ds, kn, NN)
        dkn = dkn + _dot(ds, qh, TN)
        dv = dv + _dot(p, do_h, TN)
    dq, dqw = _head_norm_bwd(q, qw, qsc, qr, dqn, hm)
    dk, dkw = _head_norm_bwd(k, kw, ksc, kr, dkn, hm)
    return dq, dk, dv, dqw, dkw


def _att_block(q, k, v, qw, kw, has_prev):
    hm = _head_masks()
    qn, kn = _head_norm(q, qw, hm)[0], _head_norm(k, kw, hm)[0]
    mask = _att_mask(has_prev)
    o = jnp.zeros((ATT_BLOCK, LANES), f32)
    lse = jnp.zeros((ATT_BLOCK, LANES), f32)
    for h in range(2):
        s = _bdot(qn * hm[h], kn, NT) * 0.125
        s = jnp.where(mask, s, -jnp.inf)
        m = jnp.max(s, axis=-1, keepdims=True)
        p = jnp.exp(s - m)
        l = jnp.sum(p, axis=-1, keepdims=True)
        o = o + hm[h] * _bdot(p / l, v, NN)
        lse = lse + hm[h] * (m + jnp.log(l))
    return o, lse


def att_fwd(p_att, qw, kw, d, g, name):
    S = p_att.shape[0]
    SEG = ATT_SEG
    nblk = SEG // ATT_BLOCK

    def body(p_ref, qw_ref, kw_ref, o_ref, l_ref, q_s, k_ext, v_ext, o_s, l_s):
        seg = pl.program_id(1)

        @pl.when(seg == 0)
        def _():
            k_ext[SEG:, :] = jnp.zeros((SEG, LANES), f32)
            v_ext[SEG:, :] = jnp.zeros((SEG, LANES), f32)

        k_ext[:SEG, :] = k_ext[SEG:, :]
        v_ext[:SEG, :] = v_ext[SEG:, :]
        q_s[...] = p_ref[:, 0:128]
        k_ext[SEG:, :] = p_ref[:, 128:256]
        v_ext[SEG:, :] = p_ref[:, 256:384]
        qw_v, kw_v = qw_ref[...], kw_ref[...]

        def blk(b, carry):
            j, r = b // d, b % d
            qs = j * (ATT_BLOCK * d) + r
            ks = SEG + qs - ATT_BLOCK * d
            o, lse = _att_block(q_s[_rows(qs, ATT_BLOCK, d), :], k_ext[_rows(ks, 2 * ATT_BLOCK, d), :],
                                v_ext[_rows(ks, 2 * ATT_BLOCK, d), :], qw_v, kw_v, (seg > 0) | (j > 0))
            o_s[_rows(qs, ATT_BLOCK, d), :] = o
            l_s[_rows(qs, ATT_BLOCK, d), :] = lse
            return carry

        lax.fori_loop(0, nblk, blk, 0, unroll=8)
        o_ref[...] = o_s[...]
        l_ref[...] = l_s[...]

    vec = pl.BlockSpec((1, LANES), lambda hh, s: (0, 0))
    out = pl.BlockSpec((SEG, LANES), lambda hh, s: (s, hh))
    return _call(body, name, (2, S // SEG), [pl.BlockSpec((SEG, 384), lambda hh, s: (s, MAIN_ATT_BLOCK + 2 * g + hh)), vec, vec],
                 [out, out], [jax.ShapeDtypeStruct((S, 256), f32), jax.ShapeDtypeStruct((S, 256), f32)],
                 scratch=[pltpu.VMEM((SEG, LANES), f32), pltpu.VMEM((2 * SEG, LANES), f32),
                          pltpu.VMEM((2 * SEG, LANES), f32), pltpu.VMEM((SEG, LANES), f32),
                          pltpu.VMEM((SEG, LANES), f32)],
                 sem=("arbitrary", "arbitrary"))(p_att, qw, kw)


def att_bwd(p_att, o, lse, do, dlse, qw, kw, d, g, dp_main, name):
    S = p_att.shape[0]
    SEG = ATT_SEG
    nseg = S // SEG
    nblk = SEG // ATT_BLOCK

    def body(p_ref, pp_ref, o_ref, l_ref, do_ref, dl_ref, qw_ref, kw_ref, _, dp_ref, dqw_ref, dkw_ref,
             q_s, k_ext, v_ext, dq_s, dk_ext, dv_ext):
        hh, i = pl.program_id(0), pl.program_id(1)
        seg = nseg - 1 - i

        @pl.when(i == 0)
        def _():
            dk_ext[...] = jnp.zeros_like(dk_ext)
            dv_ext[...] = jnp.zeros_like(dv_ext)

        @pl.when((i == 0) & (hh == 0))
        def _():
            dqw_ref[...] = jnp.zeros_like(dqw_ref)
            dkw_ref[...] = jnp.zeros_like(dkw_ref)

        dk_ext[SEG:, :] = dk_ext[:SEG, :]
        dv_ext[SEG:, :] = dv_ext[:SEG, :]
        dk_ext[:SEG, :] = jnp.zeros((SEG, LANES), f32)
        dv_ext[:SEG, :] = jnp.zeros((SEG, LANES), f32)
        q_s[...] = p_ref[:, 0:128]
        k_ext[SEG:, :] = p_ref[:, 128:256]
        v_ext[SEG:, :] = p_ref[:, 256:384]
        k_ext[:SEG, :] = pp_ref[:, 128:256]
        v_ext[:SEG, :] = pp_ref[:, 256:384]
        qw_v, kw_v = qw_ref[...], kw_ref[...]

        per_step = 8

        def blk_group(i2, carry):
            dqw, dkw = carry
            done = []
            for u in range(per_step):
                b = per_step * i2 + u
                j, r = b // d, b % d
                qs = j * (ATT_BLOCK * d) + r
                ks = SEG + qs - ATT_BLOCK * d
                has_prev = (seg > 0) | (j > 0)
                qrows, krows = _rows(qs, ATT_BLOCK, d), _rows(ks, 2 * ATT_BLOCK, d)
                dq, dk, dv, dqw_b, dkw_b = _att_block_bwd(
                    q_s[qrows, :], k_ext[krows, :], v_ext[krows, :], o_ref[qrows, :], l_ref[qrows, :],
                    do_ref[qrows, :], dl_ref[qrows, :], qw_v, kw_v, has_prev)
                dqw, dkw = dqw + dqw_b, dkw + dkw_b
                done.append((qrows, krows, dq, dk, dv))
            for qrows, krows, dq, dk, dv in done:
                dq_s[qrows, :] = dq
                dk_ext[krows, :] = dk_ext[krows, :] + dk
                dv_ext[krows, :] = dv_ext[krows, :] + dv
            return dqw, dkw

        zero = jnp.zeros((1, LANES), f32)
        dqw, dkw = lax.fori_loop(0, nblk // per_step, blk_group, (zero, zero))
        dqw_ref[...] += dqw
        dkw_ref[...] += dkw
        dp_ref[:, 0:128] = dq_s[...].astype(bf16)
        dp_ref[:, 128:256] = dk_ext[SEG:, :].astype(bf16)
        dp_ref[:, 256:384] = dv_ext[SEG:, :].astype(bf16)

    rev = lambda i: nseg - 1 - i
    vec = pl.BlockSpec((1, LANES), lambda hh, i: (0, 0))
    blk = MAIN_ATT_BLOCK + 2 * g
    cur = pl.BlockSpec((SEG, 384), lambda hh, i: (rev(i), blk + hh))
    prev = pl.BlockSpec((SEG, 384), lambda hh, i: (jnp.maximum(rev(i) - 1, 0), blk + hh))
    col = pl.BlockSpec((SEG, LANES), lambda hh, i: (rev(i), hh))
    big = pltpu.VMEM((2 * SEG, LANES), f32)
    one = pltpu.VMEM((SEG, LANES), f32)
    return _call(body, name, (2, nseg), [cur, prev, col, col, col, col, vec, vec, _ANY], [cur, vec, vec],
                 [jax.ShapeDtypeStruct((S, MAIN_WIDTH), bf16), jax.ShapeDtypeStruct((1, LANES), f32),
                  jax.ShapeDtypeStruct((1, LANES), f32)],
                 scratch=[one, big, big, one, big, big], sem=("arbitrary", "arbitrary"),
                 aliases={8: 0})(p_att, p_att, o, lse, do, dlse, qw, kw, dp_main)


def conv_fwd(p_ssd, conv_w, conv_b, name):
    S = p_ssd.shape[0]
    tm, C = CONV_ROWS, SSD_XBC

    def body(x_ref, xp_ref, w_ref, b_ref, o_ref):
        first = (pl.program_id(0) == 0)
        ext = jnp.concatenate([jnp.where(first, 0.0, xp_ref[:, 0:C]), x_ref[:, 0:C]], axis=0)
        acc = b_ref[...] + w_ref[3:4, :] * ext[8:, :]
        for k in range(1, 4):
            acc = acc + w_ref[3 - k:4 - k, :] * pltpu.roll(ext, k, 0)[8:, :]
        o_ref[...] = jax.nn.silu(acc)

    return _call(body, name, (S // tm,),
                 [pl.BlockSpec((tm, 1536), lambda i: (i, MAIN_SSD_BLOCK)),
                  pl.BlockSpec((8, 1536), lambda i: (jnp.maximum(i * (tm // 8) - 1, 0), MAIN_SSD_BLOCK)),
                  pl.BlockSpec((4, C), lambda i: (0, 0)), pl.BlockSpec((1, C), lambda i: (0, 0))],
                 pl.BlockSpec((tm, C), lambda i: (i, 0)), jax.ShapeDtypeStruct((S, C), f32),
                 sem=("parallel",))(p_ssd, p_ssd, conv_w, conv_b)


def conv_bwd(p_ssd, dact, ddt, conv_w, conv_b, dp_main, name):
    S = p_ssd.shape[0]
    tm, C = CONV_ROWS, SSD_XBC
    nblk = S // tm

    def body(x_ref, xp_ref, xn_ref, da_ref, dan_ref, ddt_ref, w_ref, b_ref, _, dp_ref, dw_ref, db_ref):
        i = pl.program_id(0)
        rows = tm + 8
        ext = jnp.concatenate([jnp.where(i == 0, 0.0, xp_ref[:, 0:C]), x_ref[:, 0:C], xn_ref[:, 0:C]], axis=0)
        shifted = [ext[8:, :]] + [pltpu.roll(ext, k, 0)[8:, :] for k in range(1, 4)]
        pre = b_ref[...] + w_ref[3:4, :] * shifted[0]
        for k in range(1, 4):
            pre = pre + w_ref[3 - k:4 - k, :] * shifted[k]
        sg = jax.nn.sigmoid(pre)
        dact = jnp.concatenate([da_ref[...], jnp.where(i == nblk - 1, 0.0, dan_ref[...])], axis=0)
        dpre = dact * (sg * (1.0 + pre * (1.0 - sg)))
        dx = w_ref[3:4, :] * dpre[0:tm, :]
        for k in range(1, 4):
            dx = dx + w_ref[3 - k:4 - k, :] * pltpu.roll(dpre, rows - k, 0)[0:tm, :]
        dp_ref[:, 0:C] = dx.astype(bf16)
        dp_ref[:, C:C + 128] = ddt_ref[...].astype(bf16)
        dp_ref[:, C + 128:] = jnp.zeros((tm, 128), bf16)
        dcur = dpre[0:tm, :]
        dws = [jnp.sum(dcur * shifted[3 - j][0:tm, :], axis=0, keepdims=True) for j in range(4)]
        dbs = jnp.sum(dcur, axis=0, keepdims=True)

        @pl.when(i == 0)
        def _():
            dw_ref[...] = jnp.zeros_like(dw_ref)
            db_ref[...] = jnp.zeros_like(db_ref)

        for j in range(4):
            dw_ref[j:j + 1, :] += dws[j]
        db_ref[...] += dbs

    t8 = tm // 8
    blk = MAIN_SSD_BLOCK
    return _call(body, name, (nblk,),
                 [pl.BlockSpec((tm, 1536), lambda i: (i, blk)),
                  pl.BlockSpec((8, 1536), lambda i: (jnp.maximum(i * t8 - 1, 0), blk)),
                  pl.BlockSpec((8, 1536), lambda i: (jnp.minimum((i + 1) * t8, S // 8 - 1), blk)),
                  pl.BlockSpec((tm, C), lambda i: (i, 0)),
                  pl.BlockSpec((8, C), lambda i: (jnp.minimum((i + 1) * t8, S // 8 - 1), 0)),
                  pl.BlockSpec((tm, 128), lambda i: (i, 0)),
                  pl.BlockSpec((4, C), lambda i: (0, 0)), pl.BlockSpec((1, C), lambda i: (0, 0)), _ANY],
                 [pl.BlockSpec((tm, 1536), lambda i: (i, blk)), pl.BlockSpec((4, C), lambda i: (0, 0)),
                  pl.BlockSpec((1, C), lambda i: (0, 0))],
                 [jax.ShapeDtypeStruct((S, MAIN_WIDTH), bf16), jax.ShapeDtypeStruct((4, C), f32),
                  jax.ShapeDtypeStruct((1, C), f32)],
                 sem=("arbitrary",), aliases={8: 0})(p_ssd, p_ssd, p_ssd, dact, dact, ddt, conv_w, conv_b, dp_main)


def _ssd_chunk(xbc, dtr, state, dt_bias, a_log, d_full):
    T = SSD_CHUNK
    r_i = lax.broadcasted_iota(jnp.int32, (T, T), 0)
    c_i = lax.broadcasted_iota(jnp.int32, (T, T), 1)
    tril = c_i <= r_i
    tri = tril.astype(bf16)
    lane = lax.broadcasted_iota(jnp.int32, (1, LANES), 1)
    hm = [(lane < 64).astype(f32), (lane >= 64).astype(f32)]
    column = lambda v, h: jnp.broadcast_to(v[:, h:h + 1], (T, LANES))

    def per_head_lanes(v):
        return jnp.concatenate([jnp.where(lane < 64, column(v, 2 * pp), column(v, 2 * pp + 1)) for pp in range(6)],
                               axis=1)

    xs, bm, cm = xbc[:, :768], xbc[:, 768:1024], xbc[:, 1024:1280]
    dt = _softplus(dtr + dt_bias)
    a_dt = dt * (-jnp.exp(a_log))
    a_cs = _xdot_l(tri, a_dt)
    dt_full = per_head_lanes(dt)
    acs_full = per_head_lanes(a_cs)
    last = lax.broadcasted_iota(jnp.int32, (T, SSD_WIDTH), 0) == T - 1
    tot_full = jnp.sum(jnp.where(last, acs_full, 0.0), axis=0, keepdims=True)
    xdt = xs * dt_full
    xw = xdt * jnp.exp(tot_full - acs_full)
    eacs = jnp.exp(acs_full)
    st_parts, off_parts, diag_parts = [], [], []
    for g in range(2):
        bg, cg = bm[:, 128 * g:128 * (g + 1)], cm[:, 128 * g:128 * (g + 1)]
        cols = slice(384 * g, 384 * (g + 1))
        st_parts.append(_bdot(bg, xw[:, cols], TN))
        off_parts.append(_bdot(cg, state[:, cols], NN))
        cb = _bdot(cg, bg, NT)
        for pp in range(3 * g, 3 * g + 3):
            xp = xdt[:, 128 * pp:128 * (pp + 1)]
            acc = jnp.zeros((T, LANES), f32)
            for hh in range(2):
                a_col = column(a_cs, 2 * pp + hh)
                decay = jnp.where(tril, jnp.exp(jnp.minimum(a_col - a_col.T, 0.0)), 0.0)
                acc = acc + _bdot(cb * decay, xp * hm[hh], NN)
            diag_parts.append(acc)
    new_state = state * jnp.exp(tot_full) + jnp.concatenate(st_parts, axis=1)
    y = jnp.concatenate(diag_parts, axis=1) + jnp.concatenate(off_parts, axis=1) * eacs + xs * d_full
    return y, new_state


def ssd_fwd(xact, p_ssd, dt_bias, a_log, d_full, name):
    S = xact.shape[0]
    T = SSD_CHUNK

    U = SSD_CHUNKS_PER_STEP

    def body(x_ref, p_ref, b_ref, a_ref, d_ref, y_ref, s_ref, state):
        @pl.when(pl.program_id(0) == 0)
        def _():
            state[...] = jnp.zeros_like(state)

        st = state[...]
        for u in range(U):
            rows = slice(T * u, T * (u + 1))
            s_ref[u] = st
            y, st = _ssd_chunk(x_ref[rows, :], p_ref[rows, :], st, b_ref[...], a_ref[...], d_ref[...])
            y_ref[rows, :] = y
        state[...] = st

    vec = lambda n: pl.BlockSpec((1, n), lambda i: (0, 0))
    return _call(body, name, (S // (U * T),),
                 [pl.BlockSpec((U * T, SSD_XBC), lambda i: (i, 0)),
                  pl.BlockSpec((U * T, 128), lambda i: (i, MAIN_DT_BLOCK)), vec(128), vec(128), vec(768)],
                 [pl.BlockSpec((U * T, 768), lambda i: (i, 0)), pl.BlockSpec((U, T, 768), lambda i: (i, 0, 0))],
                 [jax.ShapeDtypeStruct((S, 768), f32), jax.ShapeDtypeStruct((S // T, T, 768), f32)],
                 scratch=[pltpu.VMEM((T, 768), f32)], sem=("arbitrary",))(xact, p_ssd, dt_bias, a_log, d_full)


def ssd_bwd(xact, p_ssd, states, dy, dt_bias, a_log, d_full, name):
    S = xact.shape[0]
    T = SSD_CHUNK
    U = 1
    nc = S // (U * T)

    def body(x_ref, p_ref, s_ref, dy_ref, b_ref, a_ref, d_ref, dx_ref, ddt_ref, db_ref, da_ref, dd_ref, dstate):
        i = pl.program_id(0)

        @pl.when(i == 0)
        def _():
            for ref in (dstate, db_ref, da_ref, dd_ref):
                ref[...] = jnp.zeros_like(ref)

        dst = dstate[...]
        for u in reversed(range(U)):
            rows = slice(T * u, T * (u + 1))
            _, vjp = jax.vjp(_ssd_chunk, x_ref[rows, :], p_ref[rows, :], s_ref[u], b_ref[...], a_ref[...], d_ref[...])
            dx, ddt, dst, db, da, dd = vjp((dy_ref[rows, :], dst))
            dx_ref[rows, :] = dx
            ddt_ref[rows, :] = ddt
            db_ref[...] += db
            da_ref[...] += da
            dd_ref[...] += dd
        dstate[...] = dst

    rev = lambda i: nc - 1 - i
    vec = lambda n: pl.BlockSpec((1, n), lambda i: (0, 0))
    return _call(body, name, (nc,),
                 [pl.BlockSpec((U * T, SSD_XBC), lambda i: (rev(i), 0)),
                  pl.BlockSpec((U * T, 128), lambda i: (rev(i), MAIN_DT_BLOCK)),
                  pl.BlockSpec((U, T, 768), lambda i: (rev(i), 0, 0)), pl.BlockSpec((U * T, 768), lambda i: (rev(i), 0)),
                  vec(128), vec(128), vec(768)],
                 [pl.BlockSpec((U * T, SSD_XBC), lambda i: (rev(i), 0)), pl.BlockSpec((U * T, 128), lambda i: (rev(i), 0)),
                  vec(128), vec(128), vec(768)],
                 [jax.ShapeDtypeStruct((S, SSD_XBC), f32), jax.ShapeDtypeStruct((S, 128), f32),
                  jax.ShapeDtypeStruct((1, 128), f32), jax.ShapeDtypeStruct((1, 128), f32),
                  jax.ShapeDtypeStruct((1, 768), f32)],
                 scratch=[pltpu.VMEM((T, 768), f32)],
                 sem=("arbitrary",))(xact, p_ssd, states, dy, dt_bias, a_log, d_full)


def _tail_fn(ys5, pt, o0, o1, o2, l0, l1, l2, yssd, glu_b, nw, pr_glu, pr_a, pr_b, pr_c, x, weights):
    glu_w, pa, pb, pc, wo = weights
    gates = jax.nn.sigmoid(pt[:, :3072])
    za, zb, zc = pt[:, 3072:3584], pt[:, 3584:3840], pt[:, 3840:4608]
    g = jax.nn.gelu(ys5)
    ya = g * jax.nn.sigmoid(_cdot(g, glu_w, NN) + glu_b + pr_glu) * jax.nn.silu(za)
    m = jnp.maximum(jnp.maximum(l0, l1), l2)
    e0, e1, e2 = jnp.exp(l0 - m), jnp.exp(l1 - m), jnp.exp(l2 - m)
    yb = (e0 * o0 + e1 * o1 + e2 * o2) / (e0 + e1 + e2) * jax.nn.silu(zb)
    yc = _rms(yssd * jax.nn.silu(zc), nw)
    merged = (gates[:, :1024] * (_cdot(ya, pa, NN) + pr_a) + gates[:, 1024:2048] * (_cdot(yb, pb, NN) + pr_b)
              + gates[:, 2048:] * (_cdot(yc, pc, NN) + pr_c))
    out = x + _cdot(merged, wo, NN)
    return out, (g, ya, yb, yc, merged)


def _tail_specs(tm):
    row = lambda n: pl.BlockSpec((tm, n), lambda i: (i, 0))
    full = lambda a, b: pl.BlockSpec((a, b), lambda i: (0, 0))
    acts = [row(512), row(4608)] + [row(256)] * 6 + [row(768), row(D_MODEL)]
    consts = [full(1, 512), full(1, 768), full(512, 512), full(512, D_MODEL), full(256, D_MODEL),
              full(768, D_MODEL), full(D_MODEL, D_MODEL)]
    return row, full, acts, consts


def tail_fwd(ys5, pt, os_, ls_, yssd, x, glu_b, nw, weights, name, next_norm_w=None, target=None):
    S = x.shape[0]
    tm = TAIL_ROWS
    row, full, acts, consts = _tail_specs(tm)

    def body(ys5_ref, pt_ref, o0, o1, o2, l0, l1, l2, yssd_ref, x_ref, gb_ref, nw_ref, gw, pa, pb, pc, wo, *rest):
        z = lambda n: jnp.zeros((tm, n), f32)
        out, _ = _tail_fn(ys5_ref[...], pt_ref[...], o0[...], o1[...], o2[...], l0[...], l1[...], l2[...],
                          yssd_ref[...], gb_ref[...], nw_ref[...], z(512), z(D_MODEL), z(D_MODEL), z(D_MODEL),
                          x_ref[...], (gw[...], pa[...], pb[...], pc[...], wo[...]))
        if target is not None:
            t_ref, dy_ref, l_ref = rest
            diff = out - t_ref[...]
            dy_ref[...] = diff * (1.0 / D_MODEL)
            part = jnp.full((8, LANES), 0.5 / D_MODEL * jnp.sum(diff * diff), f32)

            @pl.when(pl.program_id(0) == 0)
            def _():
                l_ref[...] = part

            @pl.when(pl.program_id(0) > 0)
            def _():
                l_ref[...] += part
        elif next_norm_w is not None:
            n_ref, out_ref, h_ref = rest
            out_ref[...] = out
            h_ref[...] = _rms(out, n_ref[...]).astype(bf16)
        else:
            rest[0][...] = out

    sd = jax.ShapeDtypeStruct((S, D_MODEL), f32)
    if target is not None:
        extra_in, extra_specs = [target], [row(D_MODEL)]
        out_specs = [row(D_MODEL), pl.BlockSpec((8, LANES), lambda i: (0, 0))]
        out_shape = [sd, jax.ShapeDtypeStruct((8, LANES), f32)]
    elif next_norm_w is not None:
        extra_in, extra_specs = [next_norm_w], [full(1, D_MODEL)]
        out_specs, out_shape = [row(D_MODEL), row(D_MODEL)], [sd, jax.ShapeDtypeStruct((S, D_MODEL), bf16)]
    else:
        extra_in, extra_specs, out_specs, out_shape = [], [], row(D_MODEL), sd
    return _call(body, name, (S // tm,), acts + consts + extra_specs, out_specs, out_shape,
                 sem=("arbitrary",))(ys5, pt, *os_, *ls_, yssd, x, glu_b, nw, *weights, *extra_in)


def tail_bwd(ys5, pt, os_, ls_, yssd, dout, glu_b, nw, weights, name):
    S = dout.shape[0]
    tm = TAIL_ROWS
    row, full, acts, consts = _tail_specs(tm)

    def body(ys5_ref, pt_ref, o0, o1, o2, l0, l1, l2, yssd_ref, dout_ref, gb_ref, nw_ref, gw, pa, pb, pc, wo,
             dys5_ref, dpt_ref, do0, do1, do2, dl0, dl1, dl2, dyssd_ref, dgb_ref, dnw_ref,
             g_ref, ya_ref, yb_ref, yc_ref, mg_ref, dglu_ref, dpa_ref, dpb_ref, dpc_ref):
        z = lambda n: jnp.zeros((tm, n), f32)
        w = (gw[...], pa[...], pb[...], pc[...], wo[...])
        fn = lambda *a: _tail_fn(*a, z(D_MODEL), w)
        _, vjp, aux = jax.vjp(fn, ys5_ref[...], pt_ref[...], o0[...], o1[...], o2[...], l0[...], l1[...], l2[...],
                              yssd_ref[...], gb_ref[...], nw_ref[...], z(512), z(D_MODEL), z(D_MODEL), z(D_MODEL),
                              has_aux=True)
        (dys5, dpt, d0, d1, d2, e0, e1, e2, dyssd, dgb, dnw, dglu, dpa, dpb, dpc) = vjp(dout_ref[...])
        dys5_ref[...] = dys5
        dpt_ref[...] = dpt.astype(bf16)
        for ref, val in ((do0, d0), (do1, d1), (do2, d2), (dl0, e0), (dl1, e1), (dl2, e2)):
            ref[...] = val
        dyssd_ref[...] = dyssd
        g, ya, yb, yc, merged = aux
        for ref, val in ((g_ref, g), (ya_ref, ya), (yb_ref, yb), (yc_ref, yc), (mg_ref, merged),
                         (dglu_ref, dglu), (dpa_ref, dpa), (dpb_ref, dpb), (dpc_ref, dpc)):
            ref[...] = val.astype(bf16)

        @pl.when(pl.program_id(0) == 0)
        def _():
            dgb_ref[...] = dgb
            dnw_ref[...] = dnw

        @pl.when(pl.program_id(0) > 0)
        def _():
            dgb_ref[...] += dgb
            dnw_ref[...] += dnw

    sd = lambda n, dt=f32: jax.ShapeDtypeStruct((S, n), dt)
    out_specs = ([row(512), row(4608)] + [row(256)] * 6 + [row(768), full(1, 512), full(1, 768)]
                 + [row(512), row(512), row(256), row(768), row(D_MODEL), row(512)] + [row(D_MODEL)] * 3)
    out_shape = ([sd(512), sd(MAIN_WIDTH, bf16)] + [sd(256)] * 6 + [sd(768), jax.ShapeDtypeStruct((1, 512), f32),
                                                          jax.ShapeDtypeStruct((1, 768), f32)]
                 + [sd(512, bf16), sd(512, bf16), sd(256, bf16), sd(768, bf16), sd(D_MODEL, bf16), sd(512, bf16)]
                 + [sd(D_MODEL, bf16)] * 3)
    return _call(body, name, (S // tm,), acts + consts, out_specs, out_shape,
                 sem=("arbitrary",))(ys5, pt, *os_, *ls_, yssd, dout, glu_b, nw, *weights)


def _in_proj_segments(shards):
    dtype = shards[0].dtype

    def c(a, b):
        parts = []
        for k, sh in enumerate(shards):
            lo, hi = max(a, W_IN_SHARD * k), min(b, W_IN_SHARD * (k + 1))
            if lo < hi:
                parts.append(sh[:, lo - W_IN_SHARD * k:hi - W_IN_SHARD * k])
        return parts[0] if len(parts) == 1 else jnp.concatenate(parts, axis=1)

    atts = []
    for g in range(3):
        parts = []
        for hh in range(2):
            o = 64 * (4 * g + 2 * hh)
            parts += [c(_C_Q + o, _C_Q + o + 128), c(_C_K + o, _C_K + o + 128), c(_C_V + o, _C_V + o + 128)]
        atts.append(jnp.concatenate(parts, axis=1))
    ssd = jnp.concatenate([c(_C_XBC, _C_ZC), jnp.zeros((D_MODEL, 1536 - (_C_ZC - _C_XBC)), dtype)], axis=1)
    tail = jnp.concatenate([c(_C_GATE, _C_END), c(_C_ZA, _C_Q), c(_C_ZB, _C_XBC), c(_C_ZC, _C_GATE)], axis=1)
    return [c(_C_UA, _C_ZA), jnp.concatenate([tail, ssd] + atts, axis=1)]


def _in_proj_grad(ds5, dmain):
    dtail, dssd = dmain[:, :4608], dmain[:, 4608:6144]
    datts = [dmain[:, 6144 + 768 * g:6144 + 768 * (g + 1)] for g in range(3)]
    pick = lambda off: [datts[g][:, 384 * hh + off:384 * hh + off + 128] for g in range(3) for hh in range(2)]
    pieces = ([ds5, dtail[:, 3072:3584]] + pick(0) + pick(128) + pick(256)
              + [dtail[:, 3584:3840], dssd[:, :_C_ZC - _C_XBC], dtail[:, 3840:4608], dtail[:, :3072]])
    shards, start = [[] for _ in range(4)], 0
    for piece in pieces:
        width = piece.shape[1]
        for k in range(4):
            lo, hi = max(start, W_IN_SHARD * k), min(start + width, W_IN_SHARD * (k + 1))
            if lo < hi:
                shards[k].append(piece[:, lo - start:hi - start])
        start += width
    return jnp.stack([jnp.concatenate(s, axis=1) for s in shards])


def _prep_layer(p):
    q = {}
    q["segs"] = [s.astype(bf16) for s in _in_proj_segments(p["w_in"])]
    disc = _s5_discretize(p["s5_a_re"], p["s5_a_im"], p["s5_log_step"], p["s5_b_re"], p["s5_b_im"],
                          p["s5_c_re"], p["s5_c_im"])
    q["s5"] = disc
    q["pw"] = _lam_powers(disc[0], disc[1])
    q["s5_d"] = p["s5_d"].reshape(1, 512)
    q["qw"] = jnp.tile(p["q_norm_w"], 2).reshape(1, LANES)
    q["kw"] = jnp.tile(p["k_norm_w"], 2).reshape(1, LANES)
    q["conv_w"] = p["conv_w"]
    q["conv_b"] = p["conv_b"].reshape(1, SSD_XBC)
    pad = lambda v: jnp.pad(v, (0, LANES - v.shape[0])).reshape(1, LANES)
    q["dt_bias"], q["a_log"] = pad(p["dt_bias"]), pad(p["ssd_a_log"])
    q["d_full"] = jnp.repeat(p["ssd_d"], 64).reshape(1, SSD_WIDTH)
    q["glu_b"] = p["s5_glu_b"].reshape(1, 512)
    q["nw"] = p["ssd_norm_w"].reshape(1, SSD_WIDTH)
    q["norm_w"] = p["norm_w"].reshape(1, D_MODEL)
    q["tailw"] = tuple(p[n].astype(bf16) for n in ("s5_glu_w", "proj_a", "proj_b", "proj_c", "w_out"))
    return q


_DILATIONS = (1, 4, 16)


def layer_fwd(x, q, tag, h=None, next_norm_w=None, target=None):
    if h is None:
        h = rms_fwd(x, q["norm_w"], f"rms_fwd{tag}")
    p_s5, p_main = [mm_nn(h, w, f"inproj{k}{tag}") for k, w in enumerate(q["segs"])]
    _, _, w_re, w_im, c_re, c_im = q["s5"]
    ys5, h_re, h_im = s5_fwd(p_s5, *q["pw"], w_re, w_im, c_re, c_im, q["s5_d"], f"s5_fwd{tag}")
    os_, ls_ = [], []
    for g, d in enumerate(_DILATIONS):
        o, l = att_fwd(p_main, q["qw"], q["kw"], d, g, f"att_fwd{g}{tag}")
        os_.append(o)
        ls_.append(l)
    xact = conv_fwd(p_main, q["conv_w"], q["conv_b"], f"conv_fwd{tag}")
    yssd, states = ssd_fwd(xact, p_main, q["dt_bias"], q["a_log"], q["d_full"], f"ssd_fwd{tag}")
    out = tail_fwd(ys5, p_main, os_, ls_, yssd, x, q["glu_b"], q["nw"], q["tailw"], f"tail_fwd{tag}",
                   next_norm_w=next_norm_w, target=target)
    saved = dict(x=x, h=h, p_s5=p_s5, p_main=p_main, ys5=ys5, h_re=h_re, h_im=h_im,
                 os=os_, ls=ls_, xact=xact, yssd=yssd, states=states)
    return out, saved


def layer_bwd(dout, sv, q, p, tag):
    (dys5, dp_main, do0, do1, do2, dl0, dl1, dl2, dyssd, dglu_b, dnw, g_b, ya_b, yb_b, yc_b, mg_b, dglu_b16,
     dpa_b, dpb_b, dpc_b) = tail_bwd(sv["ys5"], sv["p_main"], sv["os"], sv["ls"], sv["yssd"], dout, q["glu_b"],
                                     q["nw"], q["tailw"], f"tail_bwd{tag}")
    grads = {}
    grads["s5_glu_w"] = mm_tn(g_b, dglu_b16, f"dglu_w{tag}")
    grads["proj_a"] = mm_tn(ya_b, dpa_b, f"dproj_a{tag}")
    grads["proj_b"] = mm_tn(yb_b, dpb_b, f"dproj_b{tag}")
    grads["proj_c"] = mm_tn(yc_b, dpc_b, f"dproj_c{tag}")
    grads["w_out"] = mm_tn(mg_b, dout, f"dw_out{tag}")
    grads["s5_glu_b"] = dglu_b.reshape(512)
    grads["ssd_norm_w"] = dnw.reshape(SSD_WIDTH)

    dxact, ddt, ddt_bias, da_log, dd_full = ssd_bwd(sv["xact"], sv["p_main"], sv["states"], dyssd, q["dt_bias"],
                                                    q["a_log"], q["d_full"], f"ssd_bwd{tag}")
    dp_main, dconv_w, dconv_b = conv_bwd(sv["p_main"], dxact, ddt, q["conv_w"], q["conv_b"], dp_main,
                                         f"conv_bwd{tag}")
    grads["dt_bias"] = ddt_bias[0, :12]
    grads["ssd_a_log"] = da_log[0, :12]
    grads["ssd_d"] = dd_full.reshape(12, 64).sum(axis=1)
    grads["conv_w"] = dconv_w
    grads["conv_b"] = dconv_b.reshape(SSD_XBC)

    dqw, dkw = 0.0, 0.0
    for g, d in enumerate(_DILATIONS):
        dp_main, a, b = att_bwd(sv["p_main"], sv["os"][g], sv["ls"][g], (do0, do1, do2)[g], (dl0, dl1, dl2)[g],
                                q["qw"], q["kw"], d, g, dp_main, f"att_bwd{g}{tag}")
        dqw, dkw = dqw + a, dkw + b
    grads["q_norm_w"] = dqw.reshape(2, 64).sum(axis=0)
    grads["k_norm_w"] = dkw.reshape(2, 64).sum(axis=0)

    _, _, w_re, w_im, c_re, c_im = q["s5"]
    dp_s5, dwre, dwim, dcre, dcim, dlam_re, dlam_im, dd = s5_bwd(
        dys5, sv["p_s5"], sv["h_re"], sv["h_im"], *q["pw"], w_re, w_im, c_re, c_im, q["s5_d"], f"s5_bwd{tag}")
    s5_names = ("s5_a_re", "s5_a_im", "s5_log_step", "s5_b_re", "s5_b_im", "s5_c_re", "s5_c_im")
    _, disc_vjp = jax.vjp(_s5_discretize, *[p[n] for n in s5_names])
    for n, gr in zip(s5_names, disc_vjp((dlam_re, dlam_im, dwre, dwim, dcre, dcim))):
        grads[n] = gr
    grads["s5_d"] = dd.reshape(512)

    dsegs = [dp_s5, dp_main]
    dws = [mm_tn(sv["h"], ds, f"dw_in{k}{tag}") for k, ds in enumerate(dsegs)]
    grads["w_in"] = _in_proj_grad(*dws)
    dh_main = mm_nt(dp_main, q["segs"][1], f"dh1{tag}")
    dx, dnorm_w = mm_nt_rms_bwd(dp_s5, q["segs"][0], dh_main, sv["x"], q["norm_w"], dout, f"dh0_rms_bwd{tag}")
    grads["norm_w"] = dnorm_w.reshape(D_MODEL)
    return dx, grads


def _exchange(name, scatter=(), gather=(), sibling=(), sibling_both=False, sibling_by_core=None):
    scatter, gather, sibling = list(scatter), list(gather), list(sibling)
    chip_xs = scatter + gather
    ns, nc, nb = len(scatter), len(chip_xs), len(sibling)
    n = nc + nb
    n_in = n + (2 if sibling_by_core else 0)
    n_out = n + (1 if sibling_by_core else 0)
    n_sem = 3 * nc + nb + (1 if sibling_by_core else 0)

    def body(*refs):
        x_refs, o_refs, send_sems, recv_sems = refs[:n_in], refs[n_in:n_in + n_out], refs[-2], refs[-1]
        mx, my, mc = lax.axis_index("x"), lax.axis_index("y"), lax.axis_index("c")
        me = 2 * mx + my
        copies = []
        for a in range(nc):
            for t, (px, py) in enumerate(((1 - mx, my), (mx, 1 - my), (1 - mx, 1 - my))):
                src = x_refs[a].at[2 * px + py] if a < ns else x_refs[a]
                copies.append(pltpu.make_async_remote_copy(
                    src_ref=src, dst_ref=o_refs[a].at[me], send_sem=send_sems.at[3 * a + t],
                    recv_sem=recv_sems.at[3 * a + t], device_id=(px, py, mc), device_id_type=pl.DeviceIdType.MESH))
        for b in range(nc, n):
            k = 3 * nc + b - nc
            copies.append(pltpu.make_async_remote_copy(
                src_ref=x_refs[b], dst_ref=o_refs[b].at[mc] if sibling_both else o_refs[b], send_sem=send_sems.at[k],
                recv_sem=recv_sems.at[k], device_id=(mx, my, 1 - mc), device_id_type=pl.DeviceIdType.MESH))
        for cp in copies:
            cp.start()
        if sibling_by_core:
            def pick(src):
                return pltpu.make_async_remote_copy(
                    src_ref=src, dst_ref=o_refs[n], send_sem=send_sems.at[n_sem - 1], recv_sem=recv_sems.at[n_sem - 1],
                    device_id=(mx, my, 1 - mc), device_id_type=pl.DeviceIdType.MESH)

            @pl.when(mc == 0)
            def _():
                pick(x_refs[n]).start()

            @pl.when(mc == 1)
            def _():
                pick(x_refs[n + 1]).start()

            copies.append(pick(x_refs[n]))
        for cp in copies:
            cp.wait()

    shapes = ([(4,) + tuple(x.shape[1:]) for x in scatter] + [(4,) + tuple(x.shape) for x in gather]
              + [((2,) if sibling_both else ()) + tuple(x.shape) for x in sibling])
    xs = chip_xs + sibling
    out_shape = [jax.ShapeDtypeStruct(s, x.dtype) for s, x in zip(shapes, xs)]
    if sibling_by_core:
        out_shape.append(jax.ShapeDtypeStruct(sibling_by_core[0].shape, sibling_by_core[0].dtype))
    outs = pl.pallas_call(
        body, name=name, in_specs=[_ANY] * n_in, out_specs=[_ANY] * n_out, out_shape=out_shape,
        scratch_shapes=[pltpu.SemaphoreType.DMA((n_sem,)), pltpu.SemaphoreType.DMA((n_sem,))],
    )(*xs, *(sibling_by_core or ()))
    me, c = 2 * lax.axis_index("x") + lax.axis_index("y"), lax.axis_index("c")
    fixed = []
    for a, (o, x) in enumerate(zip(outs, xs)):
        if a < ns:
            o = lax.dynamic_update_index_in_dim(o, lax.dynamic_index_in_dim(x, me, 0, keepdims=True), me, 0)
        elif a < nc:
            o = lax.dynamic_update_index_in_dim(o, x[None], me, 0)
        elif sibling_both:
            o = lax.dynamic_update_index_in_dim(o, x[None], c, 0)
        fixed.append(o)
    if sibling_by_core:
        fixed.append(outs[n])
    return fixed[:ns], fixed[ns:nc], fixed[nc:]


def _rows_tile(rows, row_bytes, budget=5 << 19):
    return next(t for t in (512, 256, 128, 64, 32, 16, 8) if rows % t == 0 and t * row_bytes <= budget)


def _padded_row_bytes(cols):
    return -(-cols // LANES) * LANES * 4


def _add2(a, b, name, out_dtype=f32):
    by_core = isinstance(a, (tuple, list))
    parts = list(a) if by_core else [a]
    R, C = b.shape
    tr = _rows_tile(R, _padded_row_bytes(C))

    def body(*refs):
        b_ref, o_ref = refs[-2], refs[-1]
        mine = jnp.where(lax.axis_index("c") == 0, refs[0][...], refs[1][...]) if by_core else refs[0][...]
        o_ref[...] = (mine + b_ref[...]).astype(out_dtype)

    spec = pl.BlockSpec((tr, C), lambda i: (i, 0))
    return _call(body, name, (R // tr,), [spec] * (len(parts) + 1), spec, jax.ShapeDtypeStruct((R, C), out_dtype),
                 sem=("parallel",))(*parts, b)


def _sum4(x, name):
    R = x.shape[1]
    tr = _tile(R, (2560, 1024, 512, 256, 128))

    def body(x_ref, o_ref):
        p = [x_ref[j].astype(f32) for j in range(4)]
        o_ref[...] = ((p[0] + p[1]) + p[2]) + p[3]

    return _call(body, name, (R // tr,), [pl.BlockSpec((4, tr, LANES), lambda i: (0, i, 0))],
                 pl.BlockSpec((tr, LANES), lambda i: (i, 0)), jax.ShapeDtypeStruct((R, LANES), f32),
                 sem=("parallel",))(x)


def _adamw(g_parts, w, m, v, name):
    stacked = not isinstance(g_parts, (tuple, list))
    k = g_parts.shape[0] if stacked else len(g_parts)
    R, C = w.shape
    tr = _rows_tile(R, _padded_row_bytes(C))

    def body(*refs):
        w_ref, m_ref, v_ref, g_ref, d_ref, nm_ref, nv_ref = refs[-7:]
        if stacked:
            g = refs[0][0].astype(f32)
            for j in range(1, k):
                g = g + refs[0][j].astype(f32)
        else:
            g = refs[0][...]
            for r in refs[1:k]:
                g = g + r[...]
        g_ref[...] = g
        d_ref[...], nm_ref[...], nv_ref[...] = _adamw_update(g, w_ref[...], m_ref[...], v_ref[...])

    spec = pl.BlockSpec((tr, C), lambda i: (i, 0))
    sd = jax.ShapeDtypeStruct((R, C), f32)
    g_specs = [pl.BlockSpec((k, tr, C), lambda i: (0, i, 0))] if stacked else [spec] * k
    g_args = [g_parts] if stacked else list(g_parts)
    return _call(body, name, (R // tr,), g_specs + [spec] * 3, [spec] * 4, [sd] * 4,
                 sem=("parallel",))(*g_args, w, m, v)


def _adamw_update(g, w, m, v):
    m = ADAM_B1 * m + (1.0 - ADAM_B1) * g
    v = ADAM_B2 * v + (1.0 - ADAM_B2) * (g * g)
    c1 = 1.0 - ADAM_B1 ** ADAM_STEP
    c2 = 1.0 - ADAM_B2 ** ADAM_STEP
    return -ADAM_LR * ((m / c1) / (jnp.sqrt(v / c2) + ADAM_EPS) + ADAM_WD * w), m, v


def _adamw_small(gs, ws, ms, vs, name):
    n = len(gs)

    def body(*refs):
        ins, outs = refs[:4 * n], refs[4 * n:]
        for t in range(n):
            d, m, v = _adamw_update(ins[t][...], ins[n + t][...], ins[2 * n + t][...], ins[3 * n + t][...])
            outs[t][...] = d
            outs[n + t][...] = m
            outs[2 * n + t][...] = v

    vmem = pl.BlockSpec(memory_space=pltpu.VMEM)
    outs = pl.pallas_call(
        body, name=name, in_specs=[vmem] * (4 * n), out_specs=[vmem] * (3 * n),
        out_shape=[jax.ShapeDtypeStruct(w.shape, f32) for w in ws] * 3,
        compiler_params=pltpu.CompilerParams(vmem_limit_bytes=V7X_VMEM_LIMIT))(*gs, *ws, *ms, *vs)
    return outs[:n], outs[n:2 * n], outs[2 * n:]


def _pack(arrays, row_multiple=PACK_ROWS):
    flat = jnp.concatenate([a.reshape(-1) for a in arrays])
    unit = row_multiple * LANES
    n = -(-flat.shape[0] // unit) * unit
    return jnp.pad(flat, (0, n - flat.shape[0])).reshape(n // LANES, LANES)


def _unpack(buf, shapes, lead=()):
    flat = buf.reshape(lead + (-1,))
    out, off = [], 0
    for s in shapes:
        n = 1
        for dim in s:
            n *= dim
        out.append(flat[..., off:off + n].reshape(lead + tuple(s)))
        off += n
    return out


def _to_shards(full, axis):
    s = full.shape
    t = full.reshape(s[:axis] + (4, s[axis] // 4) + s[axis + 1:])
    return jnp.moveaxis(t, axis, 0)


def _from_shards(sh, axis):
    t = jnp.moveaxis(sh, 0, axis)
    s = t.shape
    return t.reshape(s[:axis] + (s[axis] * s[axis + 1],) + s[axis + 2:])


def kernel(x, norm_w, w_in, s5_a_re, s5_a_im, s5_log_step, s5_b_re, s5_b_im, s5_c_re, s5_c_im, s5_d, s5_glu_w, s5_glu_b, q_norm_w, k_norm_w, conv_w, conv_b, dt_bias, ssd_a_log, ssd_d, ssd_norm_w, proj_a, proj_b, proj_c, w_out, loss_target, m_norm_w, m_w_in, m_s5_a_re, m_s5_a_im, m_s5_log_step, m_s5_b_re, m_s5_b_im, m_s5_c_re, m_s5_c_im, m_s5_d, m_s5_glu_w, m_s5_glu_b, m_q_norm_w, m_k_norm_w, m_conv_w, m_conv_b, m_dt_bias, m_ssd_a_log, m_ssd_d, m_ssd_norm_w, m_proj_a, m_proj_b, m_proj_c, m_w_out, v_norm_w, v_w_in, v_s5_a_re, v_s5_a_im, v_s5_log_step, v_s5_b_re, v_s5_b_im, v_s5_c_re, v_s5_c_im, v_s5_d, v_s5_glu_w, v_s5_glu_b, v_q_norm_w, v_k_norm_w, v_conv_w, v_conv_b, v_dt_bias, v_ssd_a_log, v_ssd_d, v_ssd_norm_w, v_proj_a, v_proj_b, v_proj_c, v_w_out):
    given = dict(locals())
    W = {n: given[n] for n in _WEIGHTS}
    M = {n: given["m_" + n] for n in _WEIGHTS}
    V = {n: given["v_" + n] for n in _WEIGHTS}
    n_layers = norm_w.shape[0]
    assert n_layers == 2
    c = lax.axis_index("c")

    mine_of = lambda t: lax.dynamic_index_in_dim(t, c, 0, keepdims=False)
    as_payload = lambda n: lax.bitcast_convert_type(W[n], bf16) if n == "conv_w" else W[n].astype(bf16)
    payload_shapes = [W[n].shape + ((2,) if n == "conv_w" else ()) for n, _ in _SHARDED]
    wpack = _pack([as_payload(n) for n, _ in _SHARDED])
    half_rows = wpack.shape[0] // 2
    _, (pack_half, w_in_mine_layer), _ = _exchange(
        "gather_weights", gather=[lax.dynamic_slice_in_dim(wpack, c * half_rows, half_rows),
                                  mine_of(w_in).astype(bf16)])
    _, _, (w_in_layers, pack_halves) = _exchange("share_weights", sibling=[w_in_mine_layer, pack_half],
                                                 sibling_both=True)
    gathered = jnp.moveaxis(pack_halves, 0, 1).reshape(4, 2 * half_rows, LANES)
    full = dict(W)
    pieces = _unpack(gathered.reshape(4, -1), payload_shapes, lead=(4,))
    for (n, axis), sh in zip(_SHARDED, pieces):
        full[n] = _from_shards(lax.bitcast_convert_type(sh, f32) if n == "conv_w" else sh, axis)

    qs, saves = [], []
    for l in range(n_layers):
        p = {n: full[n][l] for n in _WEIGHTS if n != "w_in"}
        p["w_in"] = [w_in_layers[l, k] for k in range(4)]
        qs.append((_prep_layer(p), p))
    (act, h), sv = layer_fwd(x[0], qs[0][0], "_l0", next_norm_w=qs[1][0]["norm_w"])
    saves.append(sv)
    (dact, lsum), sv = layer_fwd(act, qs[1][0], "_l1", h=h, target=loss_target[0])
    saves.append(sv)
    loss = lax.psum(lsum[0, 0], ("x", "y", "c"))
    layer_grads = [None] * n_layers
    for l in reversed(range(n_layers)):
        q, p = qs[l]
        dact, layer_grads[l] = layer_bwd(dact, saves[l], q, p, f"_l{l}")
    grad_x = dact[None]
    G = {n: jnp.stack([layer_grads[l][n] for l in range(n_layers)]) for n in _WEIGHTS if n != "w_in"}

    repl_shapes = [W[n].shape for n in _REPL]
    small = _pack([G[n] for n in _REPL], 4 * PACK_ROWS)
    quarter = small.shape[0] // 4
    big = [_to_shards(G[n], axis).reshape(4, -1) for n, axis in _SHARDED]
    big = jnp.concatenate(big, axis=1)
    unit = PACK_ROWS * LANES
    nbig = -(-big.shape[1] // unit) * unit
    big = jnp.pad(big, ((0, 0), (0, nbig - big.shape[1]))).reshape(4, nbig // LANES, LANES)
    gpack = jnp.concatenate([big, small.reshape(4, quarter, LANES)], axis=1)
    rbig = nbig // LANES
    g0, g1 = layer_grads[0]["w_in"], layer_grads[1]["w_in"]

    (landed_pack,), _, (from_sibling,) = _exchange(
        "swap_w_in_grads_and_scatter_grads", scatter=[gpack.astype(bf16)], sibling_by_core=(g1, g0))
    flat = lambda t: t.reshape(4 * D_MODEL, W_IN_SHARD)
    shards = _add2((flat(g0), flat(g1)), flat(from_sibling), "sum_cores_w_in", out_dtype=bf16)
    mine = _sum4(landed_pack, "sum_chips")

    (landed,), _, (other,) = _exchange(
        "scatter_w_in_grads_and_swap_cores", scatter=[shards.reshape(4, D_MODEL, W_IN_SHARD)], sibling=[mine])
    w_in_mine = _adamw(landed, mine_of(w_in), mine_of(m_w_in), mine_of(v_w_in), "adamw_w_in")
    gq = _add2(mine[rbig:], other[rbig:], "sum_cores_small")

    _, (gsmall,), w_in_out = _exchange(
        "share_w_in_updates_and_gather_small", gather=[gq], sibling=w_in_mine, sibling_both=True)
    gsmall = gsmall.reshape(4 * quarter, LANES)

    shard_shapes = [W[n].shape for n, _ in _SHARDED]
    g_mine, g_other = _unpack(mine[:rbig], shard_shapes), _unpack(other[:rbig], shard_shapes)
    rows_of = lambda t: t.reshape(-1, t.shape[-1])
    res = [dict(), dict(), dict(), dict()]
    for k, (n, _) in enumerate(_SHARDED):
        outs = _adamw((rows_of(g_mine[k]), rows_of(g_other[k])), rows_of(W[n]), rows_of(M[n]), rows_of(V[n]),
                      f"adamw_{n}")
        for kind in range(4):
            res[kind][n] = outs[kind].reshape(W[n].shape)
    g_small = _unpack(gsmall, repl_shapes)
    small_out = _adamw_small([rows_of(g) for g in g_small], *([rows_of(T[n]) for n in _REPL] for T in (W, M, V)),
                             "adamw_replicated")
    for kind in range(4):
        res[kind]["w_in"] = w_in_out[kind]
        for k, n in enumerate(_REPL):
            res[kind][n] = g_small[k] if kind == 0 else small_out[kind - 1][k].reshape(W[n].shape)
    return (loss, grad_x, *[res[0][n] for n in _WEIGHTS], *[res[1][n] for n in _WEIGHTS],
            *[res[2][n] for n in _WEIGHTS], *[res[3][n] for n in _WEIGHTS])
```

```python
import functools

import jax
import jax.numpy as jnp
from jax import lax
from jax.experimental import pallas as pl
from jax.experimental.pallas import tpu as pltpu

f32 = jnp.float32
bf16 = jnp.bfloat16

D_MODEL = 1024
RMS_EPS = 1e-6
V7X_VMEM_LIMIT = 60 * 1024 * 1024
LANES = 128
NN, NT, TN = ((1,), (0,)), ((1,), (1,)), ((0,), (0,))

S5_STATES = 2048
S5_ROWS = 512
ATT_SEG = 2048
ATT_BLOCK = 128
SSD_CHUNK = 128
SSD_CHUNKS_PER_STEP = 4
SSD_WIDTH = 768
SSD_XBC = 1280
CONV_ROWS = 512
TAIL_ROWS = 256

ADAM_LR, ADAM_B1, ADAM_B2, ADAM_EPS, ADAM_WD, ADAM_STEP = 0.001, 0.9, 0.999, 1e-08, 0.01, 10

_C_UA, _C_ZA, _C_Q, _C_K, _C_V, _C_ZB, _C_XBC, _C_DT, _C_ZC, _C_GATE, _C_END = (
    0, 512, 1024, 1792, 2560, 3328, 3584, 4864, 4876, 5644, 8716)

_SHARDED = (("s5_glu_w", 1), ("conv_w", 2), ("proj_a", 2), ("proj_b", 2), ("proj_c", 2), ("w_out", 1))
W_IN_SHARD = 2179
_REPL = ("norm_w", "s5_a_re", "s5_a_im", "s5_log_step", "s5_b_re", "s5_b_im", "s5_c_re", "s5_c_im", "s5_d",
         "s5_glu_b", "q_norm_w", "k_norm_w", "conv_b", "dt_bias", "ssd_a_log", "ssd_d", "ssd_norm_w")
_WEIGHTS = ("norm_w", "w_in", "s5_a_re", "s5_a_im", "s5_log_step", "s5_b_re", "s5_b_im", "s5_c_re", "s5_c_im",
            "s5_d", "s5_glu_w", "s5_glu_b", "q_norm_w", "k_norm_w", "conv_w", "conv_b", "dt_bias", "ssd_a_log",
            "ssd_d", "ssd_norm_w", "proj_a", "proj_b", "proj_c", "w_out")
PACK_ROWS = 512


def _dot(a, b, dims):
    return lax.dot_general(a.astype(bf16), b.astype(bf16), (dims, ((), ())), preferred_element_type=f32)


_ANY = pl.BlockSpec(memory_space=pl.ANY)

MAIN_WIDTH = 8448
MAIN_SSD_BLOCK = 3
MAIN_DT_BLOCK = 46
MAIN_ATT_BLOCK = 16


def _call(body, name, grid, in_specs, out_specs, out_shape, scratch=(), sem=None, aliases=None):
    return pl.pallas_call(
        body, name=name, grid=grid, in_specs=in_specs, out_specs=out_specs, out_shape=out_shape,
        scratch_shapes=list(scratch), input_output_aliases=aliases or {},
        compiler_params=pltpu.CompilerParams(dimension_semantics=sem, vmem_limit_bytes=V7X_VMEM_LIMIT))


def _tile(n, options=(1024, 768, 512, 384, 256, 128)):
    return next(t for t in options if n % t == 0)


@functools.partial(jax.custom_vjp, nondiff_argnums=(2,))
def _bdot(a, b, dims):
    return _dot(a, b, dims)


def _bdot_fwd(a, b, dims):
    return _dot(a, b, dims), (a, b)


def _bdot_bwd(dims, res, g):
    a, b = res
    if dims == NN:
        da, db = _dot(g, b, NT), _dot(a, g, TN)
    elif dims == NT:
        da, db = _dot(g, b, NN), _dot(g, a, TN)
    else:
        da, db = _dot(b, g, NT), _dot(a, g, NN)
    return da.astype(a.dtype), db.astype(b.dtype)


_bdot.defvjp(_bdot_fwd, _bdot_bwd)


@functools.partial(jax.custom_vjp, nondiff_argnums=(2,))
def _cdot(a, w, dims):
    return _dot(a, w, dims)


def _cdot_fwd(a, w, dims):
    return _dot(a, w, dims), w


def _cdot_bwd(dims, w, g):
    da = _dot(g, w, NT) if dims == NN else _dot(g, w, NN)
    return da, jnp.zeros_like(w)


_cdot.defvjp(_cdot_fwd, _cdot_bwd)


def _split3(x):
    hi = x.astype(bf16)
    r = x - hi.astype(f32)
    mid = r.astype(bf16)
    lo = (r - mid.astype(f32)).astype(bf16)
    return hi, mid, lo


@jax.custom_vjp
def _xdot_l(m, x):
    return sum(_dot(m, p, NN) for p in _split3(x))


def _xdot_l_fwd(m, x):
    return _xdot_l(m, x), m


def _xdot_l_bwd(m, g):
    return jnp.zeros_like(m), sum(_dot(m, p, TN) for p in _split3(g))


_xdot_l.defvjp(_xdot_l_fwd, _xdot_l_bwd)


@jax.custom_vjp
def _softplus(x):
    e = jnp.exp(-jnp.abs(x))
    u = 1.0 + e
    log1p = jnp.where(u == 1.0, e, jnp.log(u) * (e / jnp.where(u == 1.0, 1.0, u - 1.0)))
    return jnp.maximum(x, 0.0) + log1p


def _softplus_fwd(x):
    return _softplus(x), x


def _softplus_bwd(x, g):
    return (g * jax.nn.sigmoid(x),)


_softplus.defvjp(_softplus_fwd, _softplus_bwd)


def _rms(x, w):
    return x * lax.rsqrt(jnp.mean(x * x, axis=-1, keepdims=True) + RMS_EPS) * w


def mm_nn(a, b, name, tm=2048):
    M, K = a.shape
    N = b.shape[1]
    tn = _tile(N)

    def body(a_ref, b_ref, o_ref):
        o_ref[...] = _dot(a_ref[...], b_ref[...], NN)

    return _call(body, name, (M // tm, N // tn),
                 [pl.BlockSpec((tm, K), lambda i, j: (i, 0)), pl.BlockSpec((K, tn), lambda i, j: (0, j))],
                 pl.BlockSpec((tm, tn), lambda i, j: (i, j)), jax.ShapeDtypeStruct((M, N), f32),
                 sem=("parallel", "parallel"))(a, b)


def mm_nt(a, b, name, tm=1024):
    M, K = a.shape
    N = b.shape[0]
    tk = _tile(K, (2816, 1024, 768, 512, 256, 128))

    def body(a_ref, b_ref, o_ref):
        k = pl.program_id(1)
        p = _dot(a_ref[...], b_ref[...], NT)

        @pl.when(k == 0)
        def _():
            o_ref[...] = p

        @pl.when(k > 0)
        def _():
            o_ref[...] += p

    return _call(body, name, (M // tm, K // tk),
                 [pl.BlockSpec((tm, tk), lambda i, k: (i, k)), pl.BlockSpec((N, tk), lambda i, k: (0, k))],
                 pl.BlockSpec((tm, N), lambda i, k: (i, 0)), jax.ShapeDtypeStruct((M, N), f32),
                 sem=("parallel", "arbitrary"))(a, b)


def mm_tn(a, b, name, tk=2048):
    K, M = a.shape
    N = b.shape[1]
    tn = _tile(N)

    def body(a_ref, b_ref, o_ref):
        k = pl.program_id(1)
        p = _dot(a_ref[...], b_ref[...], TN)

        @pl.when(k == 0)
        def _():
            o_ref[...] = p

        @pl.when(k > 0)
        def _():
            o_ref[...] += p

    return _call(body, name, (N // tn, K // tk),
                 [pl.BlockSpec((tk, M), lambda j, k: (k, 0)), pl.BlockSpec((tk, tn), lambda j, k: (k, j))],
                 pl.BlockSpec((M, tn), lambda j, k: (0, j)), jax.ShapeDtypeStruct((M, N), f32),
                 sem=("parallel", "arbitrary"))(a, b)


def rms_fwd(x, w, name, tm=512):
    S = x.shape[0]

    def body(x_ref, w_ref, o_ref):
        o_ref[...] = _rms(x_ref[...], w_ref[...]).astype(bf16)

    return _call(body, name, (S // tm,),
                 [pl.BlockSpec((tm, D_MODEL), lambda i: (i, 0)), pl.BlockSpec((1, D_MODEL), lambda i: (0, 0))],
                 pl.BlockSpec((tm, D_MODEL), lambda i: (i, 0)), jax.ShapeDtypeStruct((S, D_MODEL), bf16),
                 sem=("parallel",))(x, w)


def mm_nt_rms_bwd(a, b, acc, x, w, dres, name, tm=1024):
    S, K = a.shape

    def body(a_ref, b_ref, acc_ref, x_ref, w_ref, dr_ref, dx_ref, dw_ref):
        dh = _dot(a_ref[...], b_ref[...], NT) + acc_ref[...]
        _, vjp = jax.vjp(_rms, x_ref[...], w_ref[...])
        dx, dw = vjp(dh)
        dx_ref[...] = dx + dr_ref[...]

        @pl.when(pl.program_id(0) == 0)
        def _():
            dw_ref[...] = dw

        @pl.when(pl.program_id(0) > 0)
        def _():
            dw_ref[...] += dw

    row = pl.BlockSpec((tm, D_MODEL), lambda i: (i, 0))
    vec = pl.BlockSpec((1, D_MODEL), lambda i: (0, 0))
    return _call(body, name, (S // tm,),
                 [pl.BlockSpec((tm, K), lambda i: (i, 0)), pl.BlockSpec((D_MODEL, K), lambda i: (0, 0)), row, row, vec,
                  row], [row, vec],
                 [jax.ShapeDtypeStruct((S, D_MODEL), f32), jax.ShapeDtypeStruct((1, D_MODEL), f32)],
                 sem=("arbitrary",))(a, b, acc, x, w, dres)


def _s5_discretize(a_re, a_im, log_step, b_re, b_im, c_re, c_im):
    step = jnp.exp(log_step)[:, None]
    mag = jnp.exp(a_re * step)
    ang = a_im * step
    lam_re, lam_im = mag * jnp.cos(ang), mag * jnp.sin(ang)
    num_re, num_im = lam_re - 1.0, lam_im
    den = a_re * a_re + a_im * a_im
    f_re = (num_re * a_re + num_im * a_im) / den
    f_im = (num_im * a_re - num_re * a_im) / den
    bb_re = f_re[..., None] * b_re - f_im[..., None] * b_im
    bb_im = f_re[..., None] * b_im + f_im[..., None] * b_re
    eye = jnp.eye(8, dtype=f32)

    def block_in(bb):
        t = bb.transpose(0, 2, 1).reshape(4, 8, 16, 1, 64)
        return (t * eye[None, :, None, :, None]).reshape(4, 128, 512)

    def block_out(c):
        t = c.transpose(0, 2, 1).reshape(4, 8, 64, 1, 16)
        return (t * eye[None, :, None, :, None]).reshape(4, 512, 128)

    return (lam_re.reshape(1, S5_STATES), lam_im.reshape(1, S5_STATES), block_in(bb_re), block_in(bb_im),
            block_out(c_re), block_out(c_im))


def _lam_powers(lam_re, lam_im):
    rows_re, rows_im = [lam_re], [lam_im]
    for _ in range(7):
        pr, pi = rows_re[-1], rows_im[-1]
        rows_re.append(pr * lam_re - pi * lam_im)
        rows_im.append(pr * lam_im + pi * lam_re)
    return jnp.concatenate(rows_re, 0), jnp.concatenate(rows_im, 0)


def s5_fwd(u, pw_re, pw_im, w_re, w_im, c_re, c_im, dvec, name):
    S = u.shape[0]
    R, NS = S5_ROWS, S5_STATES
    nb = R // 8

    def body(u_ref, pwr_ref, pwi_ref, wre_ref, wim_ref, cre_ref, cim_ref, d_ref, y_ref, hr_ref, hi_ref,
             car_re, car_im, cin_re, cin_im, up, yp):
        @pl.when(pl.program_id(0) == 0)
        def _():
            car_re[...] = jnp.zeros_like(car_re)
            car_im[...] = jnp.zeros_like(car_im)

        slab = lambda r: pl.ds(r * nb, nb)
        for r in range(8):
            up[slab(r), :] = u_ref[:, r, :]
        u = up[...]
        for j in range(4):
            uj = u[:, 128 * j:128 * (j + 1)]
            hr_ref[:, 512 * j:512 * (j + 1)] = _dot(uj, wre_ref[j], NN)
            hi_ref[:, 512 * j:512 * (j + 1)] = _dot(uj, wim_ref[j], NN)
        lr, li = pwr_ref[0:1, :], pwi_ref[0:1, :]
        for r in range(1, 8):
            pr, pi = hr_ref[slab(r - 1), :], hi_ref[slab(r - 1), :]
            hr_ref[slab(r), :] = lr * pr - li * pi + hr_ref[slab(r), :]
            hi_ref[slab(r), :] = lr * pi + li * pr + hi_ref[slab(r), :]
        l8r, l8i = pwr_ref[7:8, :], pwi_ref[7:8, :]

        def across(c, carry):
            gr, gi = carry
            cin_re[pl.ds(c, 1), :] = gr
            cin_im[pl.ds(c, 1), :] = gi
            er, ei = hr_ref[pl.ds(7 * nb + c, 1), :], hi_ref[pl.ds(7 * nb + c, 1), :]
            return l8r * gr - l8i * gi + er, l8r * gi + l8i * gr + ei

        gr, gi = lax.fori_loop(0, nb, across, (car_re[...], car_im[...]))
        car_re[...] = gr
        car_im[...] = gi
        cr, ci = cin_re[...], cin_im[...]
        for r in range(8):
            pr, pi = pwr_ref[r:r + 1, :], pwi_ref[r:r + 1, :]
            hr_ref[slab(r), :] = hr_ref[slab(r), :] + pr * cr - pi * ci
            hi_ref[slab(r), :] = hi_ref[slab(r), :] + pr * ci + pi * cr
        for j in range(4):
            sl = slice(512 * j, 512 * (j + 1))
            cs = slice(128 * j, 128 * (j + 1))
            yp[:, cs] = (_dot(hr_ref[:, sl], cre_ref[j], NN) - _dot(hi_ref[:, sl], cim_ref[j], NN)
                         + d_ref[:, cs] * u[:, cs])
        for r in range(8):
            y_ref[:, r, :] = yp[slab(r), :]

    full = lambda shape: pl.BlockSpec(shape, lambda i: (0,) * len(shape))
    hspec = pl.BlockSpec((R, NS), lambda i: (i, 0))
    uspec = pl.BlockSpec((nb, 8, 512), lambda i: (i, 0, 0))
    y, h_re, h_im = _call(
        body, name, (S // R,),
        [uspec, full((8, NS)), full((8, NS)), full((4, 128, 512)),
         full((4, 128, 512)), full((4, 512, 128)), full((4, 512, 128)), full((1, 512))],
        [uspec, hspec, hspec],
        [jax.ShapeDtypeStruct((S // 8, 8, 512), f32), jax.ShapeDtypeStruct((S, NS), f32),
         jax.ShapeDtypeStruct((S, NS), f32)],
        scratch=[pltpu.VMEM((1, NS), f32), pltpu.VMEM((1, NS), f32), pltpu.VMEM((nb, NS), f32),
                 pltpu.VMEM((nb, NS), f32), pltpu.VMEM((R, 512), f32), pltpu.VMEM((R, 512), f32)],
        sem=("arbitrary",))(u.reshape(S // 8, 8, 512), pw_re, pw_im, w_re.astype(bf16), w_im.astype(bf16),
                            c_re.astype(bf16), c_im.astype(bf16), dvec)
    return y.reshape(S, 512), h_re, h_im


def s5_bwd(dy, u, h_re, h_im, pw_re, pw_im, w_re, w_im, c_re, c_im, dvec, name):
    S = u.shape[0]
    R, NS = S5_ROWS, S5_STATES
    nb = R // 8
    nchunk = S // R

    def body(dy_ref, u_ref, hr_ref, hi_ref, hpr_ref, hpi_ref, pwr_ref, pwi_ref, wre_ref, wim_ref, cre_ref, cim_ref,
             d_ref, du_ref, dwre_ref, dwim_ref, dcre_ref, dcim_ref, dlr_ref, dli_ref, dd_ref,
             ar, ai, car_re, car_im, cin_re, cin_im, up, dyp, dup):
        i = pl.program_id(0)

        @pl.when(i == 0)
        def _():
            for ref in (car_re, car_im, dwre_ref, dwim_ref, dcre_ref, dcim_ref, dlr_ref, dli_ref, dd_ref):
                ref[...] = jnp.zeros_like(ref)

        slab = lambda r: pl.ds(r * nb, nb)
        for r in range(8):
            up[slab(r), :] = u_ref[:, r, :]
            dyp[slab(r), :] = dy_ref[:, r, :]
        dy = dyp[...]
        u = up[...]
        for j in range(4):
            dyj = dy[:, 128 * j:128 * (j + 1)]
            ar[:, 512 * j:512 * (j + 1)] = _dot(dyj, cre_ref[j], NT)
            ai[:, 512 * j:512 * (j + 1)] = -_dot(dyj, cim_ref[j], NT)
        lr, li = pwr_ref[0:1, :], pwi_ref[0:1, :]
        for r in range(6, -1, -1):
            nr, ni = ar[slab(r + 1), :], ai[slab(r + 1), :]
            ar[slab(r), :] = lr * nr + li * ni + ar[slab(r), :]
            ai[slab(r), :] = lr * ni - li * nr + ai[slab(r), :]
        l8r, l8i = pwr_ref[7:8, :], pwi_ref[7:8, :]

        def across(k, carry):
            c = nb - 1 - k
            gr, gi = carry
            cin_re[pl.ds(c, 1), :] = gr
            cin_im[pl.ds(c, 1), :] = gi
            er, ei = ar[pl.ds(c, 1), :], ai[pl.ds(c, 1), :]
            return l8r * gr + l8i * gi + er, l8r * gi - l8i * gr + ei

        gr, gi = lax.fori_loop(0, nb, across, (car_re[...], car_im[...]))
        car_re[...] = gr
        car_im[...] = gi
        cr, ci = cin_re[...], cin_im[...]
        for r in range(8):
            pr, pi = pwr_ref[7 - r:8 - r, :], pwi_ref[7 - r:8 - r, :]
            ar[slab(r), :] = ar[slab(r), :] + pr * cr + pi * ci
            ai[slab(r), :] = ai[slab(r), :] + pr * ci - pi * cr

        acc_r = jnp.zeros((1, NS), f32)
        acc_i = jnp.zeros((1, NS), f32)
        has_prev = (i < nchunk - 1).astype(f32)
        top = lax.broadcasted_iota(jnp.int32, (nb, NS), 0) == 0
        for r in range(8):
            if r == 0:
                xr = jnp.where(top, hpr_ref[7:8, :] * has_prev, pltpu.roll(hr_ref[slab(7), :], 1, 0))
                xi = jnp.where(top, hpi_ref[7:8, :] * has_prev, pltpu.roll(hi_ref[slab(7), :], 1, 0))
            else:
                xr, xi = hr_ref[slab(r - 1), :], hi_ref[slab(r - 1), :]
            br, bi = ar[slab(r), :], ai[slab(r), :]
            acc_r += jnp.sum(br * xr + bi * xi, axis=0, keepdims=True)
            acc_i += jnp.sum(bi * xr - br * xi, axis=0, keepdims=True)
        dlr_ref[...] += acc_r
        dli_ref[...] += acc_i
        dd_ref[...] += jnp.sum(dy * u, axis=0, keepdims=True)

        for j in range(4):
            sl = slice(512 * j, 512 * (j + 1))
            cs = slice(128 * j, 128 * (j + 1))
            arj, aij = ar[:, sl], ai[:, sl]
            uj, dyj = u[:, cs], dy[:, cs]
            dup[:, cs] = _dot(arj, wre_ref[j], NT) + _dot(aij, wim_ref[j], NT) + d_ref[:, cs] * dyj
            dwre_ref[j] += _dot(uj, arj, TN)
            dwim_ref[j] += _dot(uj, aij, TN)
            dcre_ref[j] += _dot(hr_ref[:, sl], dyj, TN)
            dcim_ref[j] -= _dot(hi_ref[:, sl], dyj, TN)
        for r in range(8):
            du_ref[:, r, :] = dup[slab(r), :]

    rev = lambda i: nchunk - 1 - i
    full = lambda shape: pl.BlockSpec(shape, lambda i: (0,) * len(shape))
    row = pl.BlockSpec((nb, 8, 512), lambda i: (rev(i), 0, 0))
    hspec = pl.BlockSpec((R, NS), lambda i: (rev(i), 0))
    hprev = pl.BlockSpec((8, NS), lambda i: (jnp.maximum(rev(i) * nb - 1, 0), 0))
    outs = _call(
        body, name, (nchunk,),
        [row, row, hspec, hspec, hprev, hprev, full((8, NS)), full((8, NS)), full((4, 128, 512)), full((4, 128, 512)),
         full((4, 512, 128)), full((4, 512, 128)), full((1, 512))],
        [row, full((4, 128, 512)), full((4, 128, 512)), full((4, 512, 128)), full((4, 512, 128)),
         full((1, NS)), full((1, NS)), full((1, 512))],
        [jax.ShapeDtypeStruct((S // 8, 8, 512), f32), jax.ShapeDtypeStruct((4, 128, 512), f32),
         jax.ShapeDtypeStruct((4, 128, 512), f32), jax.ShapeDtypeStruct((4, 512, 128), f32),
         jax.ShapeDtypeStruct((4, 512, 128), f32), jax.ShapeDtypeStruct((1, NS), f32),
         jax.ShapeDtypeStruct((1, NS), f32), jax.ShapeDtypeStruct((1, 512), f32)],
        scratch=[pltpu.VMEM((R, NS), f32), pltpu.VMEM((R, NS), f32), pltpu.VMEM((1, NS), f32),
                 pltpu.VMEM((1, NS), f32), pltpu.VMEM((nb, NS), f32), pltpu.VMEM((nb, NS), f32),
                 pltpu.VMEM((R, 512), f32), pltpu.VMEM((R, 512), f32), pltpu.VMEM((R, 512), f32)],
        sem=("arbitrary",))(dy.reshape(S // 8, 8, 512), u.reshape(S // 8, 8, 512), h_re, h_im, h_re, h_im, pw_re,
                            pw_im, w_re.astype(bf16), w_im.astype(bf16), c_re.astype(bf16), c_im.astype(bf16), dvec)
    return (outs[0].reshape(S, 512),) + tuple(outs[1:])


def _rows(start, n, d):
    return pl.ds(pl.multiple_of(start, ATT_BLOCK), n) if d == 1 else pl.ds(start, n, stride=d)


def _head_masks():
    lane = lax.broadcasted_iota(jnp.int32, (1, LANES), 1)
    return [(lane < 64).astype(f32), (lane >= 64).astype(f32)]


def _head_norm(x, w, hm):
    x2 = x * x
    r = [lax.rsqrt(jnp.sum(x2 * hm[h], axis=-1, keepdims=True) * (1.0 / 64) + RMS_EPS) for h in range(2)]
    sc = hm[0] * r[0] + hm[1] * r[1]
    return x * sc * w, sc, r


def _head_norm_bwd(x, w, sc, r, dxn, hm):
    dw = jnp.sum(dxn * x * sc, axis=0, keepdims=True)
    t = dxn * w
    tx = t * x
    corr = sum(hm[h] * (r[h] * r[h] * r[h]) * jnp.sum(tx * hm[h], axis=-1, keepdims=True) for h in range(2))
    return t * sc - x * corr * (1.0 / 64), dw


def _att_mask(has_prev):
    qi = lax.broadcasted_iota(jnp.int32, (ATT_BLOCK, 2 * ATT_BLOCK), 0) + ATT_BLOCK
    kj = lax.broadcasted_iota(jnp.int32, (ATT_BLOCK, 2 * ATT_BLOCK), 1)
    return (qi - kj >= 0) & (qi - kj <= ATT_BLOCK) & (has_prev | (kj >= ATT_BLOCK))


def _att_block_bwd(q, k, v, o, lse, do, dlse, qw, kw, has_prev):
    hm = _head_masks()
    mask = _att_mask(has_prev)
    qn, qsc, qr = _head_norm(q, qw, hm)
    kn, ksc, kr = _head_norm(k, kw, hm)
    dqn = jnp.zeros((ATT_BLOCK, LANES), f32)
    dkn = jnp.zeros((2 * ATT_BLOCK, LANES), f32)
    dv = jnp.zeros((2 * ATT_BLOCK, LANES), f32)
    for h in range(2):
        qh, do_h = qn * hm[h], do * hm[h]
        s = _dot(qh, kn, NT) * 0.125
        p = jnp.exp(jnp.where(mask, s - lse[:, 64 * h:64 * h + 1], -jnp.inf))
        dp = _dot(do_h, v, NT)
        delta = jnp.sum(do_h * o, axis=-1, keepdims=True)
        dl = jnp.sum(dlse * hm[h], axis=-1, keepdims=True)
        ds = p * (dp - delta + dl) * 0.125
        dqn = dqn + hm[h] * _dot(ds, kn, NN)
        dkn = dkn + _dot(ds, qh, TN)
        dv = dv + _dot(p, do_h, TN)
    dq, dqw = _head_norm_bwd(q, qw, qsc, qr, dqn, hm)
    dk, dkw = _head_norm_bwd(k, kw, ksc, kr, dkn, hm)
    return dq, dk, dv, dqw, dkw


def _att_block(q, k, v, qw, kw, has_prev):
    hm = _head_masks()
    qn, kn = _head_norm(q, qw, hm)[0], _head_norm(k, kw, hm)[0]
    mask = _att_mask(has_prev)
    o = jnp.zeros((ATT_BLOCK, LANES), f32)
    lse = jnp.zeros((ATT_BLOCK, LANES), f32)
    for h in range(2):
        s = _bdot(qn * hm[h], kn, NT) * 0.125
        s = jnp.where(mask, s, -jnp.inf)
        m = jnp.max(s, axis=-1, keepdims=True)
        p = jnp.exp(s - m)
        l = jnp.sum(p, axis=-1, keepdims=True)
        o = o + hm[h] * _bdot(p / l, v, NN)
        lse = lse + hm[h] * (m + jnp.log(l))
    return o, lse


def att_fwd(p_att, qw, kw, d, g, name):
    S = p_att.shape[0]
    SEG = ATT_SEG
    nblk = SEG // ATT_BLOCK

    def body(p_ref, qw_ref, kw_ref, o_ref, l_ref, q_s, k_ext, v_ext, o_s, l_s):
        seg = pl.program_id(1)

        @pl.when(seg == 0)
        def _():
            k_ext[SEG:, :] = jnp.zeros((SEG, LANES), f32)
            v_ext[SEG:, :] = jnp.zeros((SEG, LANES), f32)

        k_ext[:SEG, :] = k_ext[SEG:, :]
        v_ext[:SEG, :] = v_ext[SEG:, :]
        q_s[...] = p_ref[:, 0:128]
        k_ext[SEG:, :] = p_ref[:, 128:256]
        v_ext[SEG:, :] = p_ref[:, 256:384]
        qw_v, kw_v = qw_ref[...], kw_ref[...]

        def blk(b, carry):
            j, r = b // d, b % d
            qs = j * (ATT_BLOCK * d) + r
            ks = SEG + qs - ATT_BLOCK * d
            o, lse = _att_block(q_s[_rows(qs, ATT_BLOCK, d), :], k_ext[_rows(ks, 2 * ATT_BLOCK, d), :],
                                v_ext[_rows(ks, 2 * ATT_BLOCK, d), :], qw_v, kw_v, (seg > 0) | (j > 0))
            o_s[_rows(qs, ATT_BLOCK, d), :] = o
            l_s[_rows(qs, ATT_BLOCK, d), :] = lse
            return carry

        lax.fori_loop(0, nblk, blk, 0, unroll=8)
        o_ref[...] = o_s[...]
        l_ref[...] = l_s[...]

    vec = pl.BlockSpec((1, LANES), lambda hh, s: (0, 0))
    out = pl.BlockSpec((SEG, LANES), lambda hh, s: (s, hh))
    return _call(body, name, (2, S // SEG), [pl.BlockSpec((SEG, 384), lambda hh, s: (s, MAIN_ATT_BLOCK + 2 * g + hh)), vec, vec],
                 [out, out], [jax.ShapeDtypeStruct((S, 256), f32), jax.ShapeDtypeStruct((S, 256), f32)],
                 scratch=[pltpu.VMEM((SEG, LANES), f32), pltpu.VMEM((2 * SEG, LANES), f32),
                          pltpu.VMEM((2 * SEG, LANES), f32), pltpu.VMEM((SEG, LANES), f32),
                          pltpu.VMEM((SEG, LANES), f32)],
                 sem=("arbitrary", "arbitrary"))(p_att, qw, kw)


def att_bwd(p_att, o, lse, do, dlse, qw, kw, d, g, dp_main, name):
    S = p_att.shape[0]
    SEG = ATT_SEG
    nseg = S // SEG
    nblk = SEG // ATT_BLOCK

    def body(p_ref, pp_ref, o_ref, l_ref, do_ref, dl_ref, qw_ref, kw_ref, _, dp_ref, dqw_ref, dkw_ref,
             q_s, k_ext, v_ext, dq_s, dk_ext, dv_ext):
        hh, i = pl.program_id(0), pl.program_id(1)
        seg = nseg - 1 - i

        @pl.when(i == 0)
        def _():
            dk_ext[...] = jnp.zeros_like(dk_ext)
            dv_ext[...] = jnp.zeros_like(dv_ext)

        @pl.when((i == 0) & (hh == 0))
        def _():
            dqw_ref[...] = jnp.zeros_like(dqw_ref)
            dkw_ref[...] = jnp.zeros_like(dkw_ref)

        dk_ext[SEG:, :] = dk_ext[:SEG, :]
        dv_ext[SEG:, :] = dv_ext[:SEG, :]
        dk_ext[:SEG, :] = jnp.zeros((SEG, LANES), f32)
        dv_ext[:SEG, :] = jnp.zeros((SEG, LANES), f32)
        q_s[...] = p_ref[:, 0:128]
        k_ext[SEG:, :] = p_ref[:, 128:256]
        v_ext[SEG:, :] = p_ref[:, 256:384]
        k_ext[:SEG, :] = pp_ref[:, 128:256]
        v_ext[:SEG, :] = pp_ref[:, 256:384]
        qw_v, kw_v = qw_ref[...], kw_ref[...]

        per_step = 8

        def blk_group(i2, carry):
            dqw, dkw = carry
            done = []
            for u in range(per_step):
                b = per_step * i2 + u
                j, r = b // d, b % d
                qs = j * (ATT_BLOCK * d) + r
                ks = SEG + qs - ATT_BLOCK * d
                has_prev = (seg > 0) | (j > 0)
                qrows, krows = _rows(qs, ATT_BLOCK, d), _rows(ks, 2 * ATT_BLOCK, d)
                dq, dk, dv, dqw_b, dkw_b = _att_block_bwd(
                    q_s[qrows, :], k_ext[krows, :], v_ext[krows, :], o_ref[qrows, :], l_ref[qrows, :],
                    do_ref[qrows, :], dl_ref[qrows, :], qw_v, kw_v, has_prev)
                dqw, dkw = dqw + dqw_b, dkw + dkw_b
                done.append((qrows, krows, dq, dk, dv))
            for qrows, krows, dq, dk, dv in done:
                dq_s[qrows, :] = dq
                dk_ext[krows, :] = dk_ext[krows, :] + dk
                dv_ext[krows, :] = dv_ext[krows, :] + dv
            return dqw, dkw

        zero = jnp.zeros((1, LANES), f32)
        dqw, dkw = lax.fori_loop(0, nblk // per_step, blk_group, (zero, zero))
        dqw_ref[...] += dqw
        dkw_ref[...] += dkw
        dp_ref[:, 0:128] = dq_s[...].astype(bf16)
        dp_ref[:, 128:256] = dk_ext[SEG:, :].astype(bf16)
        dp_ref[:, 256:384] = dv_ext[SEG:, :].astype(bf16)

    rev = lambda i: nseg - 1 - i
    vec = pl.BlockSpec((1, LANES), lambda hh, i: (0, 0))
    blk = MAIN_ATT_BLOCK + 2 * g
    cur = pl.BlockSpec((SEG, 384), lambda hh, i: (rev(i), blk + hh))
    prev = pl.BlockSpec((SEG, 384), lambda hh, i: (jnp.maximum(rev(i) - 1, 0), blk + hh))
    col = pl.BlockSpec((SEG, LANES), lambda hh, i: (rev(i), hh))
    big = pltpu.VMEM((2 * SEG, LANES), f32)
    one = pltpu.VMEM((SEG, LANES), f32)
    return _call(body, name, (2, nseg), [cur, prev, col, col, col, col, vec, vec, _ANY], [cur, vec, vec],
                 [jax.ShapeDtypeStruct((S, MAIN_WIDTH), bf16), jax.ShapeDtypeStruct((1, LANES), f32),
                  jax.ShapeDtypeStruct((1, LANES), f32)],
                 scratch=[one, big, big, one, big, big], sem=("arbitrary", "arbitrary"),
                 aliases={8: 0})(p_att, p_att, o, lse, do, dlse, qw, kw, dp_main)


def conv_fwd(p_ssd, conv_w, conv_b, name):
    S = p_ssd.shape[0]
    tm, C = CONV_ROWS, SSD_XBC

    def body(x_ref, xp_ref, w_ref, b_ref, o_ref):
        first = (pl.program_id(0) == 0)
        ext = jnp.concatenate([jnp.where(first, 0.0, xp_ref[:, 0:C]), x_ref[:, 0:C]], axis=0)
        acc = b_ref[...] + w_ref[3:4, :] * ext[8:, :]
        for k in range(1, 4):
            acc = acc + w_ref[3 - k:4 - k, :] * pltpu.roll(ext, k, 0)[8:, :]
        o_ref[...] = jax.nn.silu(acc)

    return _call(body, name, (S // tm,),
                 [pl.BlockSpec((tm, 1536), lambda i: (i, MAIN_SSD_BLOCK)),
                  pl.BlockSpec((8, 1536), lambda i: (jnp.maximum(i * (tm // 8) - 1, 0), MAIN_SSD_BLOCK)),
                  pl.BlockSpec((4, C), lambda i: (0, 0)), pl.BlockSpec((1, C), lambda i: (0, 0))],
                 pl.BlockSpec((tm, C), lambda i: (i, 0)), jax.ShapeDtypeStruct((S, C), f32),
                 sem=("parallel",))(p_ssd, p_ssd, conv_w, conv_b)


def conv_bwd(p_ssd, dact, ddt, conv_w, conv_b, dp_main, name):
    S = p_ssd.shape[0]
    tm, C = CONV_ROWS, SSD_XBC
    nblk = S // tm

    def body(x_ref, xp_ref, xn_ref, da_ref, dan_ref, ddt_ref, w_ref, b_ref, _, dp_ref, dw_ref, db_ref):
        i = pl.program_id(0)
        rows = tm + 8
        ext = jnp.concatenate([jnp.where(i == 0, 0.0, xp_ref[:, 0:C]), x_ref[:, 0:C], xn_ref[:, 0:C]], axis=0)
        shifted = [ext[8:, :]] + [pltpu.roll(ext, k, 0)[8:, :] for k in range(1, 4)]
        pre = b_ref[...] + w_ref[3:4, :] * shifted[0]
        for k in range(1, 4):
            pre = pre + w_ref[3 - k:4 - k, :] * shifted[k]
        sg = jax.nn.sigmoid(pre)
        dact = jnp.concatenate([da_ref[...], jnp.where(i == nblk - 1, 0.0, dan_ref[...])], axis=0)
        dpre = dact * (sg * (1.0 + pre * (1.0 - sg)))
        dx = w_ref[3:4, :] * dpre[0:tm, :]
        for k in range(1, 4):
            dx = dx + w_ref[3 - k:4 - k, :] * pltpu.roll(dpre, rows - k, 0)[0:tm, :]
        dp_ref[:, 0:C] = dx.astype(bf16)
        dp_ref[:, C:C + 128] = ddt_ref[...].astype(bf16)
        dp_ref[:, C + 128:] = jnp.zeros((tm, 128), bf16)
        dcur = dpre[0:tm, :]
        dws = [jnp.sum(dcur * shifted[3 - j][0:tm, :], axis=0, keepdims=True) for j in range(4)]
        dbs = jnp.sum(dcur, axis=0, keepdims=True)

        @pl.when(i == 0)
        def _():
            dw_ref[...] = jnp.zeros_like(dw_ref)
            db_ref[...] = jnp.zeros_like(db_ref)

        for j in range(4):
            dw_ref[j:j + 1, :] += dws[j]
        db_ref[...] += dbs

    t8 = tm // 8
    blk = MAIN_SSD_BLOCK
    return _call(body, name, (nblk,),
                 [pl.BlockSpec((tm, 1536), lambda i: (i, blk)),
                  pl.BlockSpec((8, 1536), lambda i: (jnp.maximum(i * t8 - 1, 0), blk)),
                  pl.BlockSpec((8, 1536), lambda i: (jnp.minimum((i + 1) * t8, S // 8 - 1), blk)),
                  pl.BlockSpec((tm, C), lambda i: (i, 0)),
                  pl.BlockSpec((8, C), lambda i: (jnp.minimum((i + 1) * t8, S // 8 - 1), 0)),
                  pl.BlockSpec((tm, 128), lambda i: (i, 0)),
                  pl.BlockSpec((4, C), lambda i: (0, 0)), pl.BlockSpec((1, C), lambda i: (0, 0)), _ANY],
                 [pl.BlockSpec((tm, 1536), lambda i: (i, blk)), pl.BlockSpec((4, C), lambda i: (0, 0)),
                  pl.BlockSpec((1, C), lambda i: (0, 0))],
                 [jax.ShapeDtypeStruct((S, MAIN_WIDTH), bf16), jax.ShapeDtypeStruct((4, C), f32),
                  jax.ShapeDtypeStruct((1, C), f32)],
                 sem=("arbitrary",), aliases={8: 0})(p_ssd, p_ssd, p_ssd, dact, dact, ddt, conv_w, conv_b, dp_main)


def _ssd_chunk(xbc, dtr, state, dt_bias, a_log, d_full):
    T = SSD_CHUNK
    r_i = lax.broadcasted_iota(jnp.int32, (T, T), 0)
    c_i = lax.broadcasted_iota(jnp.int32, (T, T), 1)
    tril = c_i <= r_i
    tri = tril.astype(bf16)
    lane = lax.broadcasted_iota(jnp.int32, (1, LANES), 1)
    hm = [(lane < 64).astype(f32), (lane >= 64).astype(f32)]
    column = lambda v, h: jnp.broadcast_to(v[:, h:h + 1], (T, LANES))

    def per_head_lanes(v):
        return jnp.concatenate([jnp.where(lane < 64, column(v, 2 * pp), column(v, 2 * pp + 1)) for pp in range(6)],
                               axis=1)

    xs, bm, cm = xbc[:, :768], xbc[:, 768:1024], xbc[:, 1024:1280]
    dt = _softplus(dtr + dt_bias)
    a_dt = dt * (-jnp.exp(a_log))
    a_cs = _xdot_l(tri, a_dt)
    dt_full = per_head_lanes(dt)
    acs_full = per_head_lanes(a_cs)
    last = lax.broadcasted_iota(jnp.int32, (T, SSD_WIDTH), 0) == T - 1
    tot_full = jnp.sum(jnp.where(last, acs_full, 0.0), axis=0, keepdims=True)
    xdt = xs * dt_full
    xw = xdt * jnp.exp(tot_full - acs_full)
    eacs = jnp.exp(acs_full)
    st_parts, off_parts, diag_parts = [], [], []
    for g in range(2):
        bg, cg = bm[:, 128 * g:128 * (g + 1)], cm[:, 128 * g:128 * (g + 1)]
        cols = slice(384 * g, 384 * (g + 1))
        st_parts.append(_bdot(bg, xw[:, cols], TN))
        off_parts.append(_bdot(cg, state[:, cols], NN))
        cb = _bdot(cg, bg, NT)
        for pp in range(3 * g, 3 * g + 3):
            xp = xdt[:, 128 * pp:128 * (pp + 1)]
            acc = jnp.zeros((T, LANES), f32)
            for hh in range(2):
                a_col = column(a_cs, 2 * pp + hh)
                decay = jnp.where(tril, jnp.exp(jnp.minimum(a_col - a_col.T, 0.0)), 0.0)
                acc = acc + _bdot(cb * decay, xp * hm[hh], NN)
            diag_parts.append(acc)
    new_state = state * jnp.exp(tot_full) + jnp.concatenate(st_parts, axis=1)
    y = jnp.concatenate(diag_parts, axis=1) + jnp.concatenate(off_parts, axis=1) * eacs + xs * d_full
    return y, new_state


def ssd_fwd(xact, p_ssd, dt_bias, a_log, d_full, name):
    S = xact.shape[0]
    T = SSD_CHUNK

    U = SSD_CHUNKS_PER_STEP

    def body(x_ref, p_ref, b_ref, a_ref, d_ref, y_ref, s_ref, state):
        @pl.when(pl.program_id(0) == 0)
        def _():
            state[...] = jnp.zeros_like(state)

        st = state[...]
        for u in range(U):
            rows = slice(T * u, T * (u + 1))
            s_ref[u] = st
            y, st = _ssd_chunk(x_ref[rows, :], p_ref[rows, :], st, b_ref[...], a_ref[...], d_ref[...])
            y_ref[rows, :] = y
        state[...] = st

    vec = lambda n: pl.BlockSpec((1, n), lambda i: (0, 0))
    return _call(body, name, (S // (U * T),),
                 [pl.BlockSpec((U * T, SSD_XBC), lambda i: (i, 0)),
                  pl.BlockSpec((U * T, 128), lambda i: (i, MAIN_DT_BLOCK)), vec(128), vec(128), vec(768)],
                 [pl.BlockSpec((U * T, 768), lambda i: (i, 0)), pl.BlockSpec((U, T, 768), lambda i: (i, 0, 0))],
                 [jax.ShapeDtypeStruct((S, 768), f32), jax.ShapeDtypeStruct((S // T, T, 768), f32)],
                 scratch=[pltpu.VMEM((T, 768), f32)], sem=("arbitrary",))(xact, p_ssd, dt_bias, a_log, d_full)


def ssd_bwd(xact, p_ssd, states, dy, dt_bias, a_log, d_full, name):
    S = xact.shape[0]
    T = SSD_CHUNK
    U = 1
    nc = S // (U * T)

    def body(x_ref, p_ref, s_ref, dy_ref, b_ref, a_ref, d_ref, dx_ref, ddt_ref, db_ref, da_ref, dd_ref, dstate):
        i = pl.program_id(0)

        @pl.when(i == 0)
        def _():
            for ref in (dstate, db_ref, da_ref, dd_ref):
                ref[...] = jnp.zeros_like(ref)

        dst = dstate[...]
        for u in reversed(range(U)):
            rows = slice(T * u, T * (u + 1))
            _, vjp = jax.vjp(_ssd_chunk, x_ref[rows, :], p_ref[rows, :], s_ref[u], b_ref[...], a_ref[...], d_ref[...])
            dx, ddt, dst, db, da, dd = vjp((dy_ref[rows, :], dst))
            dx_ref[rows, :] = dx
            ddt_ref[rows, :] = ddt
            db_ref[...] += db
            da_ref[...] += da
            dd_ref[...] += dd
        dstate[...] = dst

    rev = lambda i: nc - 1 - i
    vec = lambda n: pl.BlockSpec((1, n), lambda i: (0, 0))
    return _call(body, name, (nc,),
                 [pl.BlockSpec((U * T, SSD_XBC), lambda i: (rev(i), 0)),
                  pl.BlockSpec((U * T, 128), lambda i: (rev(i), MAIN_DT_BLOCK)),
                  pl.BlockSpec((U, T, 768), lambda i: (rev(i), 0, 0)), pl.BlockSpec((U * T, 768), lambda i: (rev(i), 0)),
                  vec(128), vec(128), vec(768)],
                 [pl.BlockSpec((U * T, SSD_XBC), lambda i: (rev(i), 0)), pl.BlockSpec((U * T, 128), lambda i: (rev(i), 0)),
                  vec(128), vec(128), vec(768)],
                 [jax.ShapeDtypeStruct((S, SSD_XBC), f32), jax.ShapeDtypeStruct((S, 128), f32),
                  jax.ShapeDtypeStruct((1, 128), f32), jax.ShapeDtypeStruct((1, 128), f32),
                  jax.ShapeDtypeStruct((1, 768), f32)],
                 scratch=[pltpu.VMEM((T, 768), f32)],
                 sem=("arbitrary",))(xact, p_ssd, states, dy, dt_bias, a_log, d_full)


def _tail_fn(ys5, pt, o0, o1, o2, l0, l1, l2, yssd, glu_b, nw, pr_glu, pr_a, pr_b, pr_c, x, weights):
    glu_w, pa, pb, pc, wo = weights
    gates = jax.nn.sigmoid(pt[:, :3072])
    za, zb, zc = pt[:, 3072:3584], pt[:, 3584:3840], pt[:, 3840:4608]
    g = jax.nn.gelu(ys5)
    ya = g * jax.nn.sigmoid(_cdot(g, glu_w, NN) + glu_b + pr_glu) * jax.nn.silu(za)
    m = jnp.maximum(jnp.maximum(l0, l1), l2)
    e0, e1, e2 = jnp.exp(l0 - m), jnp.exp(l1 - m), jnp.exp(l2 - m)
    yb = (e0 * o0 + e1 * o1 + e2 * o2) / (e0 + e1 + e2) * jax.nn.silu(zb)
    yc = _rms(yssd * jax.nn.silu(zc), nw)
    merged = (gates[:, :1024] * (_cdot(ya, pa, NN) + pr_a) + gates[:, 1024:2048] * (_cdot(yb, pb, NN) + pr_b)
              + gates[:, 2048:] * (_cdot(yc, pc, NN) + pr_c))
    out = x + _cdot(merged, wo, NN)
    return out, (g, ya, yb, yc, merged)


def _tail_specs(tm):
    row = lambda n: pl.BlockSpec((tm, n), lambda i: (i, 0))
    full = lambda a, b: pl.BlockSpec((a, b), lambda i: (0, 0))
    acts = [row(512), row(4608)] + [row(256)] * 6 + [row(768), row(D_MODEL)]
    consts = [full(1, 512), full(1, 768), full(512, 512), full(512, D_MODEL), full(256, D_MODEL),
              full(768, D_MODEL), full(D_MODEL, D_MODEL)]
    return row, full, acts, consts


def tail_fwd(ys5, pt, os_, ls_, yssd, x, glu_b, nw, weights, name, next_norm_w=None, target=None):
    S = x.shape[0]
    tm = TAIL_ROWS
    row, full, acts, consts = _tail_specs(tm)

    def body(ys5_ref, pt_ref, o0, o1, o2, l0, l1, l2, yssd_ref, x_ref, gb_ref, nw_ref, gw, pa, pb, pc, wo, *rest):
        z = lambda n: jnp.zeros((tm, n), f32)
        out, _ = _tail_fn(ys5_ref[...], pt_ref[...], o0[...], o1[...], o2[...], l0[...], l1[...], l2[...],
                          yssd_ref[...], gb_ref[...], nw_ref[...], z(512), z(D_MODEL), z(D_MODEL), z(D_MODEL),
                          x_ref[...], (gw[...], pa[...], pb[...], pc[...], wo[...]))
        if target is not None:
            t_ref, dy_ref, l_ref = rest
            diff = out - t_ref[...]
            dy_ref[...] = diff * (1.0 / D_MODEL)
            part = jnp.full((8, LANES), 0.5 / D_MODEL * jnp.sum(diff * diff), f32)

            @pl.when(pl.program_id(0) == 0)
            def _():
                l_ref[...] = part

            @pl.when(pl.program_id(0) > 0)
            def _():
                l_ref[...] += part
        elif next_norm_w is not None:
            n_ref, out_ref, h_ref = rest
            out_ref[...] = out
            h_ref[...] = _rms(out, n_ref[...]).astype(bf16)
        else:
            rest[0][...] = out

    sd = jax.ShapeDtypeStruct((S, D_MODEL), f32)
    if target is not None:
        extra_in, extra_specs = [target], [row(D_MODEL)]
        out_specs = [row(D_MODEL), pl.BlockSpec((8, LANES), lambda i: (0, 0))]
        out_shape = [sd, jax.ShapeDtypeStruct((8, LANES), f32)]
    elif next_norm_w is not None:
        extra_in, extra_specs = [next_norm_w], [full(1, D_MODEL)]
        out_specs, out_shape = [row(D_MODEL), row(D_MODEL)], [sd, jax.ShapeDtypeStruct((S, D_MODEL), bf16)]
    else:
        extra_in, extra_specs, out_specs, out_shape = [], [], row(D_MODEL), sd
    return _call(body, name, (S // tm,), acts + consts + extra_specs, out_specs, out_shape,
                 sem=("arbitrary",))(ys5, pt, *os_, *ls_, yssd, x, glu_b, nw, *weights, *extra_in)


def tail_bwd(ys5, pt, os_, ls_, yssd, dout, glu_b, nw, weights, name):
    S = dout.shape[0]
    tm = TAIL_ROWS
    row, full, acts, consts = _tail_specs(tm)

    def body(ys5_ref, pt_ref, o0, o1, o2, l0, l1, l2, yssd_ref, dout_ref, gb_ref, nw_ref, gw, pa, pb, pc, wo,
             dys5_ref, dpt_ref, do0, do1, do2, dl0, dl1, dl2, dyssd_ref, dgb_ref, dnw_ref,
             g_ref, ya_ref, yb_ref, yc_ref, mg_ref, dglu_ref, dpa_ref, dpb_ref, dpc_ref):
        z = lambda n: jnp.zeros((tm, n), f32)
        w = (gw[...], pa[...], pb[...], pc[...], wo[...])
        fn = lambda *a: _tail_fn(*a, z(D_MODEL), w)
        _, vjp, aux = jax.vjp(fn, ys5_ref[...], pt_ref[...], o0[...], o1[...], o2[...], l0[...], l1[...], l2[...],
                              yssd_ref[...], gb_ref[...], nw_ref[...], z(512), z(D_MODEL), z(D_MODEL), z(D_MODEL),
                              has_aux=True)
        (dys5, dpt, d0, d1, d2, e0, e1, e2, dyssd, dgb, dnw, dglu, dpa, dpb, dpc) = vjp(dout_ref[...])
        dys5_ref[...] = dys5
        dpt_ref[...] = dpt.astype(bf16)
        for ref, val in ((do0, d0), (do1, d1), (do2, d2), (dl0, e0), (dl1, e1), (dl2, e2)):
            ref[...] = val
        dyssd_ref[...] = dyssd
        g, ya, yb, yc, merged = aux
        for ref, val in ((g_ref, g), (ya_ref, ya), (yb_ref, yb), (yc_ref, yc), (mg_ref, merged),
                         (dglu_ref, dglu), (dpa_ref, dpa), (dpb_ref, dpb), (dpc_ref, dpc)):
            ref[...] = val.astype(bf16)

        @pl.when(pl.program_id(0) == 0)
        def _():
            dgb_ref[...] = dgb
            dnw_ref[...] = dnw

        @pl.when(pl.program_id(0) > 0)
        def _():
            dgb_ref[...] += dgb
            dnw_ref[...] += dnw

    sd = lambda n, dt=f32: jax.ShapeDtypeStruct((S, n), dt)
    out_specs = ([row(512), row(4608)] + [row(256)] * 6 + [row(768), full(1, 512), full(1, 768)]
                 + [row(512), row(512), row(256), row(768), row(D_MODEL), row(512)] + [row(D_MODEL)] * 3)
    out_shape = ([sd(512), sd(MAIN_WIDTH, bf16)] + [sd(256)] * 6 + [sd(768), jax.ShapeDtypeStruct((1, 512), f32),
                                                          jax.ShapeDtypeStruct((1, 768), f32)]
                 + [sd(512, bf16), sd(512, bf16), sd(256, bf16), sd(768, bf16), sd(D_MODEL, bf16), sd(512, bf16)]
                 + [sd(D_MODEL, bf16)] * 3)
    return _call(body, name, (S // tm,), acts + consts, out_specs, out_shape,
                 sem=("arbitrary",))(ys5, pt, *os_, *ls_, yssd, dout, glu_b, nw, *weights)


def _in_proj_segments(shards):
    dtype = shards[0].dtype

    def c(a, b):
        parts = []
        for k, sh in enumerate(shards):
            lo, hi = max(a, W_IN_SHARD * k), min(b, W_IN_SHARD * (k + 1))
            if lo < hi:
                parts.append(sh[:, lo - W_IN_SHARD * k:hi - W_IN_SHARD * k])
        return parts[0] if len(parts) == 1 else jnp.concatenate(parts, axis=1)

    atts = []
    for g in range(3):
        parts = []
        for hh in range(2):
            o = 64 * (4 * g + 2 * hh)
            parts += [c(_C_Q + o, _C_Q + o + 128), c(_C_K + o, _C_K + o + 128), c(_C_V + o, _C_V + o + 128)]
        atts.append(jnp.concatenate(parts, axis=1))
    ssd = jnp.concatenate([c(_C_XBC, _C_ZC), jnp.zeros((D_MODEL, 1536 - (_C_ZC - _C_XBC)), dtype)], axis=1)
    tail = jnp.concatenate([c(_C_GATE, _C_END), c(_C_ZA, _C_Q), c(_C_ZB, _C_XBC), c(_C_ZC, _C_GATE)], axis=1)
    return [c(_C_UA, _C_ZA), jnp.concatenate([tail, ssd] + atts, axis=1)]


def _in_proj_grad(ds5, dmain):
    dtail, dssd = dmain[:, :4608], dmain[:, 4608:6144]
    datts = [dmain[:, 6144 + 768 * g:6144 + 768 * (g + 1)] for g in range(3)]
    pick = lambda off: [datts[g][:, 384 * hh + off:384 * hh + off + 128] for g in range(3) for hh in range(2)]
    pieces = ([ds5, dtail[:, 3072:3584]] + pick(0) + pick(128) + pick(256)
              + [dtail[:, 3584:3840], dssd[:, :_C_ZC - _C_XBC], dtail[:, 3840:4608], dtail[:, :3072]])
    shards, start = [[] for _ in range(4)], 0
    for piece in pieces:
        width = piece.shape[1]
        for k in range(4):
            lo, hi = max(start, W_IN_SHARD * k), min(start + width, W_IN_SHARD * (k + 1))
            if lo < hi:
                shards[k].append(piece[:, lo - start:hi - start])
        start += width
    return jnp.stack([jnp.concatenate(s, axis=1) for s in shards])


def _prep_layer(p):
    q = {}
    q["segs"] = [s.astype(bf16) for s in _in_proj_segments(p["w_in"])]
    disc = _s5_discretize(p["s5_a_re"], p["s5_a_im"], p["s5_log_step"], p["s5_b_re"], p["s5_b_im"],
                          p["s5_c_re"], p["s5_c_im"])
    q["s5"] = disc
    q["pw"] = _lam_powers(disc[0], disc[1])
    q["s5_d"] = p["s5_d"].reshape(1, 512)
    q["qw"] = jnp.tile(p["q_norm_w"], 2).reshape(1, LANES)
    q["kw"] = jnp.tile(p["k_norm_w"], 2).reshape(1, LANES)
    q["conv_w"] = p["conv_w"]
    q["conv_b"] = p["conv_b"].reshape(1, SSD_XBC)
    pad = lambda v: jnp.pad(v, (0, LANES - v.shape[0])).reshape(1, LANES)
    q["dt_bias"], q["a_log"] = pad(p["dt_bias"]), pad(p["ssd_a_log"])
    q["d_full"] = jnp.repeat(p["ssd_d"], 64).reshape(1, SSD_WIDTH)
    q["glu_b"] = p["s5_glu_b"].reshape(1, 512)
    q["nw"] = p["ssd_norm_w"].reshape(1, SSD_WIDTH)
    q["norm_w"] = p["norm_w"].reshape(1, D_MODEL)
    q["tailw"] = tuple(p[n].astype(bf16) for n in ("s5_glu_w", "proj_a", "proj_b", "proj_c", "w_out"))
    return q


_DILATIONS = (1, 4, 16)


def layer_fwd(x, q, tag, h=None, next_norm_w=None, target=None):
    if h is None:
        h = rms_fwd(x, q["norm_w"], f"rms_fwd{tag}")
    p_s5, p_main = [mm_nn(h, w, f"inproj{k}{tag}") for k, w in enumerate(q["segs"])]
    _, _, w_re, w_im, c_re, c_im = q["s5"]
    ys5, h_re, h_im = s5_fwd(p_s5, *q["pw"], w_re, w_im, c_re, c_im, q["s5_d"], f"s5_fwd{tag}")
    os_, ls_ = [], []
    for g, d in enumerate(_DILATIONS):
        o, l = att_fwd(p_main, q["qw"], q["kw"], d, g, f"att_fwd{g}{tag}")
        os_.append(o)
        ls_.append(l)
    xact = conv_fwd(p_main, q["conv_w"], q["conv_b"], f"conv_fwd{tag}")
    yssd, states = ssd_fwd(xact, p_main, q["dt_bias"], q["a_log"], q["d_full"], f"ssd_fwd{tag}")
    out = tail_fwd(ys5, p_main, os_, ls_, yssd, x, q["glu_b"], q["nw"], q["tailw"], f"tail_fwd{tag}",
                   next_norm_w=next_norm_w, target=target)
    saved = dict(x=x, h=h, p_s5=p_s5, p_main=p_main, ys5=ys5, h_re=h_re, h_im=h_im,
                 os=os_, ls=ls_, xact=xact, yssd=yssd, states=states)
    return out, saved


def layer_bwd(dout, sv, q, p, tag):
    (dys5, dp_main, do0, do1, do2, dl0, dl1, dl2, dyssd, dglu_b, dnw, g_b, ya_b, yb_b, yc_b, mg_b, dglu_b16,
     dpa_b, dpb_b, dpc_b) = tail_bwd(sv["ys5"], sv["p_main"], sv["os"], sv["ls"], sv["yssd"], dout, q["glu_b"],
                                     q["nw"], q["tailw"], f"tail_bwd{tag}")
    grads = {}
    grads["s5_glu_w"] = mm_tn(g_b, dglu_b16, f"dglu_w{tag}")
    grads["proj_a"] = mm_tn(ya_b, dpa_b, f"dproj_a{tag}")
    grads["proj_b"] = mm_tn(yb_b, dpb_b, f"dproj_b{tag}")
    grads["proj_c"] = mm_tn(yc_b, dpc_b, f"dproj_c{tag}")
    grads["w_out"] = mm_tn(mg_b, dout, f"dw_out{tag}")
    grads["s5_glu_b"] = dglu_b.reshape(512)
    grads["ssd_norm_w"] = dnw.reshape(SSD_WIDTH)

    dxact, ddt, ddt_bias, da_log, dd_full = ssd_bwd(sv["xact"], sv["p_main"], sv["states"], dyssd, q["dt_bias"],
                                                    q["a_log"], q["d_full"], f"ssd_bwd{tag}")
    dp_main, dconv_w, dconv_b = conv_bwd(sv["p_main"], dxact, ddt, q["conv_w"], q["conv_b"], dp_main,
                                         f"conv_bwd{tag}")
    grads["dt_bias"] = ddt_bias[0, :12]
    grads["ssd_a_log"] = da_log[0, :12]
    grads["ssd_d"] = dd_full.reshape(12, 64).sum(axis=1)
    grads["conv_w"] = dconv_w
    grads["conv_b"] = dconv_b.reshape(SSD_XBC)

    dqw, dkw = 0.0, 0.0
    for g, d in enumerate(_DILATIONS):
        dp_main, a, b = att_bwd(sv["p_main"], sv["os"][g], sv["ls"][g], (do0, do1, do2)[g], (dl0, dl1, dl2)[g],
                                q["qw"], q["kw"], d, g, dp_main, f"att_bwd{g}{tag}")
        dqw, dkw = dqw + a, dkw + b
    grads["q_norm_w"] = dqw.reshape(2, 64).sum(axis=0)
    grads["k_norm_w"] = dkw.reshape(2, 64).sum(axis=0)

    _, _, w_re, w_im, c_re, c_im = q["s5"]
    dp_s5, dwre, dwim, dcre, dcim, dlam_re, dlam_im, dd = s5_bwd(
        dys5, sv["p_s5"], sv["h_re"], sv["h_im"], *q["pw"], w_re, w_im, c_re, c_im, q["s5_d"], f"s5_bwd{tag}")
    s5_names = ("s5_a_re", "s5_a_im", "s5_log_step", "s5_b_re", "s5_b_im", "s5_c_re", "s5_c_im")
    _, disc_vjp = jax.vjp(_s5_discretize, *[p[n] for n in s5_names])
    for n, gr in zip(s5_names, disc_vjp((dlam_re, dlam_im, dwre, dwim, dcre, dcim))):
        grads[n] = gr
    grads["s5_d"] = dd.reshape(512)

    dsegs = [dp_s5, dp_main]
    dws = [mm_tn(sv["h"], ds, f"dw_in{k}{tag}") for k, ds in enumerate(dsegs)]
    grads["w_in"] = _in_proj_grad(*dws)
    dh_main = mm_nt(dp_main, q["segs"][1], f"dh1{tag}")
    dx, dnorm_w = mm_nt_rms_bwd(dp_s5, q["segs"][0], dh_main, sv["x"], q["norm_w"], dout, f"dh0_rms_bwd{tag}")
    grads["norm_w"] = dnorm_w.reshape(D_MODEL)
    return dx, grads


def _exchange(name, scatter=(), gather=(), sibling=(), sibling_both=False, sibling_by_core=None):
    scatter, gather, sibling = list(scatter), list(gather), list(sibling)
    chip_xs = scatter + gather
    ns, nc, nb = len(scatter), len(chip_xs), len(sibling)
    n = nc + nb
    n_in = n + (2 if sibling_by_core else 0)
    n_out = n + (1 if sibling_by_core else 0)
    n_sem = 3 * nc + nb + (1 if sibling_by_core else 0)

    def body(*refs):
        x_refs, o_refs, send_sems, recv_sems = refs[:n_in], refs[n_in:n_in + n_out], refs[-2], refs[-1]
        mx, my, mc = lax.axis_index("x"), lax.axis_index("y"), lax.axis_index("c")
        me = 2 * mx + my
        copies = []
        for a in range(nc):
            for t, (px, py) in enumerate(((1 - mx, my), (mx, 1 - my), (1 - mx, 1 - my))):
                src = x_refs[a].at[2 * px + py] if a < ns else x_refs[a]
                copies.append(pltpu.make_async_remote_copy(
                    src_ref=src, dst_ref=o_refs[a].at[me], send_sem=send_sems.at[3 * a + t],
                    recv_sem=recv_sems.at[3 * a + t], device_id=(px, py, mc), device_id_type=pl.DeviceIdType.MESH))
        for b in range(nc, n):
            k = 3 * nc + b - nc
            copies.append(pltpu.make_async_remote_copy(
                src_ref=x_refs[b], dst_ref=o_refs[b].at[mc] if sibling_both else o_refs[b], send_sem=send_sems.at[k],
                recv_sem=recv_sems.at[k], device_id=(mx, my, 1 - mc), device_id_type=pl.DeviceIdType.MESH))
        for cp in copies:
            cp.start()
        if sibling_by_core:
            def pick(src):
                return pltpu.make_async_remote_copy(
                    src_ref=src, dst_ref=o_refs[n], send_sem=send_sems.at[n_sem - 1], recv_sem=recv_sems.at[n_sem - 1],
                    device_id=(mx, my, 1 - mc), device_id_type=pl.DeviceIdType.MESH)

            @pl.when(mc == 0)
            def _():
                pick(x_refs[n]).start()

            @pl.when(mc == 1)
            def _():
                pick(x_refs[n + 1]).start()

            copies.append(pick(x_refs[n]))
        for cp in copies:
            cp.wait()

    shapes = ([(4,) + tuple(x.shape[1:]) for x in scatter] + [(4,) + tuple(x.shape) for x in gather]
              + [((2,) if sibling_both else ()) + tuple(x.shape) for x in sibling])
    xs = chip_xs + sibling
    out_shape = [jax.ShapeDtypeStruct(s, x.dtype) for s, x in zip(shapes, xs)]
    if sibling_by_core:
        out_shape.append(jax.ShapeDtypeStruct(sibling_by_core[0].shape, sibling_by_core[0].dtype))
    outs = pl.pallas_call(
        body, name=name, in_specs=[_ANY] * n_in, out_specs=[_ANY] * n_out, out_shape=out_shape,
        scratch_shapes=[pltpu.SemaphoreType.DMA((n_sem,)), pltpu.SemaphoreType.DMA((n_sem,))],
    )(*xs, *(sibling_by_core or ()))
    me, c = 2 * lax.axis_index("x") + lax.axis_index("y"), lax.axis_index("c")
    fixed = []
    for a, (o, x) in enumerate(zip(outs, xs)):
        if a < ns:
            o = lax.dynamic_update_index_in_dim(o, lax.dynamic_index_in_dim(x, me, 0, keepdims=True), me, 0)
        elif a < nc:
            o = lax.dynamic_update_index_in_dim(o, x[None], me, 0)
        elif sibling_both:
            o = lax.dynamic_update_index_in_dim(o, x[None], c, 0)
        fixed.append(o)
    if sibling_by_core:
        fixed.append(outs[n])
    return fixed[:ns], fixed[ns:nc], fixed[nc:]


def _rows_tile(rows, row_bytes, budget=5 << 19):
    return next(t for t in (512, 256, 128, 64, 32, 16, 8) if rows % t == 0 and t * row_bytes <= budget)


def _padded_row_bytes(cols):
    return -(-cols // LANES) * LANES * 4


def _add2(a, b, name, out_dtype=f32):
    by_core = isinstance(a, (tuple, list))
    parts = list(a) if by_core else [a]
    R, C = b.shape
    tr = _rows_tile(R, _padded_row_bytes(C))

    def body(*refs):
        b_ref, o_ref = refs[-2], refs[-1]
        mine = jnp.where(lax.axis_index("c") == 0, refs[0][...], refs[1][...]) if by_core else refs[0][...]
        o_ref[...] = (mine + b_ref[...]).astype(out_dtype)

    spec = pl.BlockSpec((tr, C), lambda i: (i, 0))
    return _call(body, name, (R // tr,), [spec] * (len(parts) + 1), spec, jax.ShapeDtypeStruct((R, C), out_dtype),
                 sem=("parallel",))(*parts, b)


def _sum4(x, name):
    R = x.shape[1]
    tr = _tile(R, (2560, 1024, 512, 256, 128))

    def body(x_ref, o_ref):
        p = [x_ref[j].astype(f32) for j in range(4)]
        o_ref[...] = ((p[0] + p[1]) + p[2]) + p[3]

    return _call(body, name, (R // tr,), [pl.BlockSpec((4, tr, LANES), lambda i: (0, i, 0))],
                 pl.BlockSpec((tr, LANES), lambda i: (i, 0)), jax.ShapeDtypeStruct((R, LANES), f32),
                 sem=("parallel",))(x)


def _adamw(g_parts, w, m, v, name):
    stacked = not isinstance(g_parts, (tuple, list))
    k = g_parts.shape[0] if stacked else len(g_parts)
    R, C = w.shape
    tr = _rows_tile(R, _padded_row_bytes(C))

    def body(*refs):
        w_ref, m_ref, v_ref, g_ref, d_ref, nm_ref, nv_ref = refs[-7:]
        if stacked:
            g = refs[0][0].astype(f32)
            for j in range(1, k):
                g = g + refs[0][j].astype(f32)
        else:
            g = refs[0][...]
            for r in refs[1:k]:
                g = g + r[...]
        g_ref[...] = g
        d_ref[...], nm_ref[...], nv_ref[...] = _adamw_update(g, w_ref[...], m_ref[...], v_ref[...])

    spec = pl.BlockSpec((tr, C), lambda i: (i, 0))
    sd = jax.ShapeDtypeStruct((R, C), f32)
    g_specs = [pl.BlockSpec((k, tr, C), lambda i: (0, i, 0))] if stacked else [spec] * k
    g_args = [g_parts] if stacked else list(g_parts)
    return _call(body, name, (R // tr,), g_specs + [spec] * 3, [spec] * 4, [sd] * 4,
                 sem=("parallel",))(*g_args, w, m, v)


def _adamw_update(g, w, m, v):
    m = ADAM_B1 * m + (1.0 - ADAM_B1) * g
    v = ADAM_B2 * v + (1.0 - ADAM_B2) * (g * g)
    c1 = 1.0 - ADAM_B1 ** ADAM_STEP
    c2 = 1.0 - ADAM_B2 ** ADAM_STEP
    return -ADAM_LR * ((m / c1) / (jnp.sqrt(v / c2) + ADAM_EPS) + ADAM_WD * w), m, v


def _adamw_small(gs, ws, ms, vs, name):
    n = len(gs)

    def body(*refs):
        ins, outs = refs[:4 * n], refs[4 * n:]
        for t in range(n):
            d, m, v = _adamw_update(ins[t][...], ins[n + t][...], ins[2 * n + t][...], ins[3 * n + t][...])
            outs[t][...] = d
            outs[n + t][...] = m
            outs[2 * n + t][...] = v

    vmem = pl.BlockSpec(memory_space=pltpu.VMEM)
    outs = pl.pallas_call(
        body, name=name, in_specs=[vmem] * (4 * n), out_specs=[vmem] * (3 * n),
        out_shape=[jax.ShapeDtypeStruct(w.shape, f32) for w in ws] * 3,
        compiler_params=pltpu.CompilerParams(vmem_limit_bytes=V7X_VMEM_LIMIT))(*gs, *ws, *ms, *vs)
    return outs[:n], outs[n:2 * n], outs[2 * n:]


def _pack(arrays, row_multiple=PACK_ROWS):
    flat = jnp.concatenate([a.reshape(-1) for a in arrays])
    unit = row_multiple * LANES
    n = -(-flat.shape[0] // unit) * unit
    return jnp.pad(flat, (0, n - flat.shape[0])).reshape(n // LANES, LANES)


def _unpack(buf, shapes, lead=()):
    flat = buf.reshape(lead + (-1,))
    out, off = [], 0
    for s in shapes:
        n = 1
        for dim in s:
            n *= dim
        out.append(flat[..., off:off + n].reshape(lead + tuple(s)))
        off += n
    return out


def _to_shards(full, axis):
    s = full.shape
    t = full.reshape(s[:axis] + (4, s[axis] // 4) + s[axis + 1:])
    return jnp.moveaxis(t, axis, 0)


def _from_shards(sh, axis):
    t = jnp.moveaxis(sh, 0, axis)
    s = t.shape
    return t.reshape(s[:axis] + (s[axis] * s[axis + 1],) + s[axis + 2:])


def kernel(x, norm_w, w_in, s5_a_re, s5_a_im, s5_log_step, s5_b_re, s5_b_im, s5_c_re, s5_c_im, s5_d, s5_glu_w, s5_glu_b, q_norm_w, k_norm_w, conv_w, conv_b, dt_bias, ssd_a_log, ssd_d, ssd_norm_w, proj_a, proj_b, proj_c, w_out, loss_target, m_norm_w, m_w_in, m_s5_a_re, m_s5_a_im, m_s5_log_step, m_s5_b_re, m_s5_b_im, m_s5_c_re, m_s5_c_im, m_s5_d, m_s5_glu_w, m_s5_glu_b, m_q_norm_w, m_k_norm_w, m_conv_w, m_conv_b, m_dt_bias, m_ssd_a_log, m_ssd_d, m_ssd_norm_w, m_proj_a, m_proj_b, m_proj_c, m_w_out, v_norm_w, v_w_in, v_s5_a_re, v_s5_a_im, v_s5_log_step, v_s5_b_re, v_s5_b_im, v_s5_c_re, v_s5_c_im, v_s5_d, v_s5_glu_w, v_s5_glu_b, v_q_norm_w, v_k_norm_w, v_conv_w, v_conv_b, v_dt_bias, v_ssd_a_log, v_ssd_d, v_ssd_norm_w, v_proj_a, v_proj_b, v_proj_c, v_w_out):
    given = dict(locals())
    W = {n: given[n] for n in _WEIGHTS}
    M = {n: given["m_" + n] for n in _WEIGHTS}
    V = {n: given["v_" + n] for n in _WEIGHTS}
    n_layers = norm_w.shape[0]
    assert n_layers == 2
    c = lax.axis_index("c")

    mine_of = lambda t: lax.dynamic_index_in_dim(t, c, 0, keepdims=False)
    as_payload = lambda n: lax.bitcast_convert_type(W[n], bf16) if n == "conv_w" else W[n].astype(bf16)
    payload_shapes = [W[n].shape + ((2,) if n == "conv_w" else ()) for n, _ in _SHARDED]
    wpack = _pack([as_payload(n) for n, _ in _SHARDED])
    half_rows = wpack.shape[0] // 2
    _, (pack_half, w_in_mine_layer), _ = _exchange(
        "gather_weights", gather=[lax.dynamic_slice_in_dim(wpack, c * half_rows, half_rows),
                                  mine_of(w_in).astype(bf16)])
    _, _, (w_in_layers, pack_halves) = _exchange("share_weights", sibling=[w_in_mine_layer, pack_half],
                                                 sibling_both=True)
    gathered = jnp.moveaxis(pack_halves, 0, 1).reshape(4, 2 * half_rows, LANES)
    full = dict(W)
    pieces = _unpack(gathered.reshape(4, -1), payload_shapes, lead=(4,))
    for (n, axis), sh in zip(_SHARDED, pieces):
        full[n] = _from_shards(lax.bitcast_convert_type(sh, f32) if n == "conv_w" else sh, axis)

    qs, saves = [], []
    for l in range(n_layers):
        p = {n: full[n][l] for n in _WEIGHTS if n != "w_in"}
        p["w_in"] = [w_in_layers[l, k] for k in range(4)]
        qs.append((_prep_layer(p), p))
    (act, h), sv = layer_fwd(x[0], qs[0][0], "_l0", next_norm_w=qs[1][0]["norm_w"])
    saves.append(sv)
    (dact, lsum), sv = layer_fwd(act, qs[1][0], "_l1", h=h, target=loss_target[0])
    saves.append(sv)
    loss = lax.psum(lsum[0, 0], ("x", "y", "c"))
    layer_grads = [None] * n_layers
    for l in reversed(range(n_layers)):
        q, p = qs[l]
        dact, layer_grads[l] = layer_bwd(dact, saves[l], q, p, f"_l{l}")
    grad_x = dact[None]
    G = {n: jnp.stack([layer_grads[l][n] for l in range(n_layers)]) for n in _WEIGHTS if n != "w_in"}

    repl_shapes = [W[n].shape for n in _REPL]
    small = _pack([G[n] for n in _REPL], 4 * PACK_ROWS)
    quarter = small.shape[0] // 4
    big = [_to_shards(G[n], axis).reshape(4, -1) for n, axis in _SHARDED]
    big = jnp.concatenate(big, axis=1)
    unit = PACK_ROWS * LANES
    nbig = -(-big.shape[1] // unit) * unit
    big = jnp.pad(big, ((0, 0), (0, nbig - big.shape[1]))).reshape(4, nbig // LANES, LANES)
    gpack = jnp.concatenate([big, small.reshape(4, quarter, LANES)], axis=1)
    rbig = nbig // LANES
    g0, g1 = layer_grads[0]["w_in"], layer_grads[1]["w_in"]

    (landed_pack,), _, (from_sibling,) = _exchange(
        "swap_w_in_grads_and_scatter_grads", scatter=[gpack.astype(bf16)], sibling_by_core=(g1, g0))
    flat = lambda t: t.reshape(4 * D_MODEL, W_IN_SHARD)
    shards = _add2((flat(g0), flat(g1)), flat(from_sibling), "sum_cores_w_in", out_dtype=bf16)
    mine = _sum4(landed_pack, "sum_chips")

    (landed,), _, (other,) = _exchange(
        "scatter_w_in_grads_and_swap_cores", scatter=[shards.reshape(4, D_MODEL, W_IN_SHARD)], sibling=[mine])
    w_in_mine = _adamw(landed, mine_of(w_in), mine_of(m_w_in), mine_of(v_w_in), "adamw_w_in")
    gq = _add2(mine[rbig:], other[rbig:], "sum_cores_small")

    _, (gsmall,), w_in_out = _exchange(
        "share_w_in_updates_and_gather_small", gather=[gq], sibling=w_in_mine, sibling_both=True)
    gsmall = gsmall.reshape(4 * quarter, LANES)

    shard_shapes = [W[n].shape for n, _ in _SHARDED]
    g_mine, g_other = _unpack(mine[:rbig], shard_shapes), _unpack(other[:rbig], shard_shapes)
    rows_of = lambda t: t.reshape(-1, t.shape[-1])
    res = [dict(), dict(), dict(), dict()]
    for k, (n, _) in enumerate(_SHARDED):
        outs = _adamw((rows_of(g_mine[k]), rows_of(g_other[k])), rows_of(W[n]), rows_of(M[n]), rows_of(V[n]),
                      f"adamw_{n}")
        for kind in range(4):
            res[kind][n] = outs[kind].reshape(W[n].shape)
    g_small = _unpack(gsmall, repl_shapes)
    small_out = _adamw_small([rows_of(g) for g in g_small], *([rows_of(T[n]) for n in _REPL] for T in (W, M, V)),
                             "adamw_replicated")
    for kind in range(4):
        res[kind]["w_in"] = w_in_out[kind]
        for k, n in enumerate(_REPL):
            res[kind][n] = g_small[k] if kind == 0 else small_out[kind - 1][k].reshape(W[n].shape)
    return (loss, grad_x, *[res[0][n] for n in _WEIGHTS], *[res[1][n] for n in _WEIGHTS],
            *[res[2][n] for n in _WEIGHTS], *[res[3][n] for n in _WEIGHTS])
```
